```python
import jax, jax.numpy as jnp
from jax import lax
import numpy as np

D_MODEL = 1024
BATCH = 32
SEQ = 2048
DEPTH = 2

N_MEM = 256
EPS = 1e-6
FOX_HEAD_DIM = 64
FOX_WIDTH = D_MODEL // 2
FOX_HEADS = FOX_WIDTH // FOX_HEAD_DIM
GMLP_GROUP_DIM = 64
GMLP_WIDTH = D_MODEL // 2
GMLP_GROUPS = GMLP_WIDTH // GMLP_GROUP_DIM
CHUNK = 128
Q_BLOCK = 128
MIX_WIDTH = FOX_WIDTH + GMLP_WIDTH
IN_WIDTH = 3 * FOX_WIDTH + FOX_HEADS + 2 * GMLP_WIDTH
CONV_WIDTH = D_MODEL
CONV_KERNEL = 31
XA_HEADS = 4
XA_HEAD_DIM = D_MODEL // XA_HEADS
FFN_HIDDEN = -(-8 * D_MODEL // (3 * 256)) * 256
N_EVEN = (DEPTH + 1) // 2
N_ODD = DEPTH // 2

kernel_name = "hybrid_gmlp_fox_conformer_memxattn"


def rmsnorm(x, g):
    x32 = x.astype(jnp.float32)
    y = x32 * lax.rsqrt(jnp.mean(x32 * x32, axis=-1, keepdims=True) + EPS)
    return (y * g.astype(jnp.float32)).astype(x.dtype)


def layernorm(x, g, b):
    x32 = x.astype(jnp.float32)
    mu = jnp.mean(x32, axis=-1, keepdims=True)
    xc = x32 - mu
    y = xc * lax.rsqrt(jnp.mean(xc * xc, axis=-1, keepdims=True) + EPS)
    return (y * g.astype(jnp.float32) + b.astype(jnp.float32)).astype(x.dtype)


def fox_attention(q, k, v, f_logit, f_bias):
    B, T, _ = q.shape
    scale = FOX_HEAD_DIM ** -0.5
    q = q.reshape(B, T, FOX_HEADS, FOX_HEAD_DIM) * scale
    k = k.reshape(B, T, FOX_HEADS, FOX_HEAD_DIM)
    v = v.reshape(B, T, FOX_HEADS, FOX_HEAD_DIM)
    log_f = jax.nn.log_sigmoid((f_logit + f_bias).astype(jnp.float32))
    cum = jnp.cumsum(log_f, axis=1).transpose(0, 2, 1)
    outs = []
    for i in range(T // Q_BLOCK):
        q0 = i * Q_BLOCK
        q1 = q0 + Q_BLOCK
        s = jnp.einsum('bqhd,bkhd->bhqk', q[:, q0:q1], k[:, :q1]).astype(jnp.float32)
        s = s + cum[:, :, q0:q1, None] - cum[:, :, None, :q1]
        causal = (q0 + jnp.arange(Q_BLOCK))[:, None] >= jnp.arange(q1)[None, :]
        p = jax.nn.softmax(jnp.where(causal, s, -jnp.inf), axis=-1).astype(v.dtype)
        outs.append(jnp.einsum('bhqk,bkhd->bqhd', p, v[:, :q1]))
    return jnp.concatenate(outs, axis=1).reshape(B, T, FOX_WIDTH)


def gmlp_spatial_gate(z, ln_g, ln_b, w_s, b_s):
    B, T, _ = z.shape
    z = jax.nn.gelu(z)
    u, vg = jnp.split(z, 2, axis=-1)
    vg = layernorm(vg, ln_g, ln_b)
    vg = vg.reshape(B, T // CHUNK, CHUNK, GMLP_GROUPS, GMLP_GROUP_DIM)
    w = w_s * jnp.tril(jnp.ones((CHUNK, CHUNK), dtype=w_s.dtype))
    mixed = jnp.einsum('gts,bcsgd->bctgd', w, vg) + b_s.T[:, :, None]
    return u * mixed.reshape(B, T, GMLP_WIDTH)


def even_mixer(h, w_in, f_bias, ln_g, ln_b, w_s, b_s, w_out):
    proj = h @ w_in
    F = FOX_WIDTH
    q, k, v, f_logit, z = jnp.split(proj, [F, 2 * F, 3 * F, 3 * F + FOX_HEADS], axis=-1)
    a_out = gmlp_spatial_gate(z, ln_g, ln_b, w_s, b_s)
    b_out = fox_attention(q, k, v, f_logit, f_bias)
    return jnp.concatenate([b_out, a_out], axis=-1) @ w_out


def conformer_conv(h, w_in, b_in, dw_w, dw_b, ln_g, ln_b, w_out, b_out):
    a, g = jnp.split(h @ w_in + b_in, 2, axis=-1)
    y = a * jax.nn.sigmoid(g)
    y = lax.conv_general_dilated(
        y, dw_w[:, None, :].astype(y.dtype), window_strides=(1,),
        padding=[(CONV_KERNEL - 1, 0)], dimension_numbers=('NWC', 'WIO', 'NWC'),
        feature_group_count=CONV_WIDTH) + dw_b
    y = jax.nn.silu(layernorm(y, ln_g, ln_b))
    return y @ w_out + b_out


def memory_cross_attention(h, m, wq, wkv, wo):
    B, T, _ = h.shape
    q = (h @ wq).reshape(B, T, XA_HEADS, XA_HEAD_DIM) * (XA_HEAD_DIM ** -0.5)
    k, v = jnp.split(m @ wkv, 2, axis=-1)
    k = k.reshape(B, -1, XA_HEADS, XA_HEAD_DIM)
    v = v.reshape(B, -1, XA_HEADS, XA_HEAD_DIM)
    s = jnp.einsum('bthd,bmhd->bhtm', q, k).astype(jnp.float32)
    p = jax.nn.softmax(s, axis=-1).astype(v.dtype)
    o = jnp.einsum('bhtm,bmhd->bthd', p, v).reshape(B, T, D_MODEL)
    return o @ wo


def swiglu(h, w_gu, w_down):
    g, u = jnp.split(h @ w_gu, 2, axis=-1)
    return (jax.nn.silu(g) * u) @ w_down


def _fwd_setup_inputs(seed: int = 0) -> dict:
    key = jax.random.key(seed)
    ks = iter(jax.random.split(key, 40))

    def nrm(shape, scale):
        return jax.random.normal(next(ks), shape, jnp.float32) * scale

    def gain(shape):
        return 1.0 + nrm(shape, 0.02)

    D = D_MODEL
    return {
        "x": nrm((BATCH, SEQ, D), 1.0),
        "mem": nrm((BATCH, N_MEM, D), 1.0),
        "mix_norm_e": gain((N_EVEN, D)),
        "w_in_e": nrm((N_EVEN, D, IN_WIDTH), D ** -0.5),
        "fox_f_bias": 2.0 + nrm((N_EVEN, FOX_HEADS), 0.5),
        "gmlp_ln_g": gain((N_EVEN, GMLP_WIDTH)),
        "gmlp_ln_b": nrm((N_EVEN, GMLP_WIDTH), 0.02),
        "gmlp_w_s": nrm((N_EVEN, GMLP_GROUPS, CHUNK, CHUNK), CHUNK ** -0.5),
        "gmlp_b_s": gain((N_EVEN, GMLP_GROUPS, CHUNK)),
        "w_out_e": nrm((N_EVEN, MIX_WIDTH, D), MIX_WIDTH ** -0.5),
        "mix_norm_o": gain((N_ODD, D)),
        "conv_w_in": nrm((N_ODD, D, 2 * CONV_WIDTH), D ** -0.5),
        "conv_b_in": nrm((N_ODD, 2 * CONV_WIDTH), 0.02),
        "conv_dw_w": nrm((N_ODD, CONV_KERNEL, CONV_WIDTH), CONV_KERNEL ** -0.5),
        "conv_dw_b": nrm((N_ODD, CONV_WIDTH), 0.02),
        "conv_ln_g": gain((N_ODD, CONV_WIDTH)),
        "conv_ln_b": nrm((N_ODD, CONV_WIDTH), 0.02),
        "conv_w_out": nrm((N_ODD, CONV_WIDTH, D), CONV_WIDTH ** -0.5),
        "conv_b_out": nrm((N_ODD, D), 0.02),
        "xa_norm": gain((DEPTH, D)),
        "mem_norm": gain((DEPTH, D)),
        "xa_wq": nrm((DEPTH, D, D), D ** -0.5),
        "xa_wkv": nrm((DEPTH, D, 2 * D), D ** -0.5),
        "xa_wo": nrm((DEPTH, D, D), D ** -0.5),
        "ffn_norm": gain((DEPTH, D)),
        "ffn_w_gu": nrm((DEPTH, D, 2 * FFN_HIDDEN), D ** -0.5),
        "ffn_w_down": nrm((DEPTH, FFN_HIDDEN, D), FFN_HIDDEN ** -0.5),
        "final_norm": gain((D,)),
    }


def _fwd_reference(x, mem, mix_norm_e, w_in_e, fox_f_bias, gmlp_ln_g, gmlp_ln_b, gmlp_w_s,
              gmlp_b_s, w_out_e, mix_norm_o, conv_w_in, conv_b_in, conv_dw_w, conv_dw_b,
              conv_ln_g, conv_ln_b, conv_w_out, conv_b_out, xa_norm, mem_norm, xa_wq,
              xa_wkv, xa_wo, ffn_norm, ffn_w_gu, ffn_w_down, final_norm):
    for layer in range(DEPTH):
        li = layer // 2
        if layer % 2 == 0:
            h = rmsnorm(x, mix_norm_e[li])
            x = x + even_mixer(h, w_in_e[li], fox_f_bias[li], gmlp_ln_g[li], gmlp_ln_b[li],
                               gmlp_w_s[li], gmlp_b_s[li], w_out_e[li])
        else:
            h = rmsnorm(x, mix_norm_o[li])
            x = x + conformer_conv(h, conv_w_in[li], conv_b_in[li], conv_dw_w[li],
                                   conv_dw_b[li], conv_ln_g[li], conv_ln_b[li],
                                   conv_w_out[li], conv_b_out[li])
        h = rmsnorm(x, xa_norm[layer])
        m = rmsnorm(mem, mem_norm[layer])
        x = x + memory_cross_attention(h, m, xa_wq[layer], xa_wkv[layer], xa_wo[layer])
        h = rmsnorm(x, ffn_norm[layer])
        x = x + swiglu(h, ffn_w_gu[layer], ffn_w_down[layer])
    return rmsnorm(x, final_norm)


import jax as _jax
import jax.numpy as _jnp

TWIN_FORMAT = 'train_step'
FWD_PARAMS = ['x', 'mem', 'mix_norm_e', 'w_in_e', 'fox_f_bias', 'gmlp_ln_g', 'gmlp_ln_b', 'gmlp_w_s', 'gmlp_b_s', 'w_out_e', 'mix_norm_o', 'conv_w_in', 'conv_b_in', 'conv_dw_w', 'conv_dw_b', 'conv_ln_g', 'conv_ln_b', 'conv_w_out', 'conv_b_out', 'xa_norm', 'mem_norm', 'xa_wq', 'xa_wkv', 'xa_wo', 'ffn_norm', 'ffn_w_gu', 'ffn_w_down', 'final_norm']
TWIN_WEIGHTS = ['mix_norm_e', 'w_in_e', 'fox_f_bias', 'gmlp_ln_g', 'gmlp_ln_b', 'gmlp_w_s', 'gmlp_b_s', 'w_out_e', 'mix_norm_o', 'conv_w_in', 'conv_b_in', 'conv_dw_w', 'conv_dw_b', 'conv_ln_g', 'conv_ln_b', 'conv_w_out', 'conv_b_out', 'xa_norm', 'mem_norm', 'xa_wq', 'xa_wkv', 'xa_wo', 'ffn_norm', 'ffn_w_gu', 'ffn_w_down', 'final_norm']
TWIN_DIFF_INPUT = 'x'
TWIN_INPUTS = ['x', 'mem', 'mix_norm_e', 'w_in_e', 'fox_f_bias', 'gmlp_ln_g', 'gmlp_ln_b', 'gmlp_w_s', 'gmlp_b_s', 'w_out_e', 'mix_norm_o', 'conv_w_in', 'conv_b_in', 'conv_dw_w', 'conv_dw_b', 'conv_ln_g', 'conv_ln_b', 'conv_w_out', 'conv_b_out', 'xa_norm', 'mem_norm', 'xa_wq', 'xa_wkv', 'xa_wo', 'ffn_norm', 'ffn_w_gu', 'ffn_w_down', 'final_norm', 'loss_target', 'm_mix_norm_e', 'm_w_in_e', 'm_fox_f_bias', 'm_gmlp_ln_g', 'm_gmlp_ln_b', 'm_gmlp_w_s', 'm_gmlp_b_s', 'm_w_out_e', 'm_mix_norm_o', 'm_conv_w_in', 'm_conv_b_in', 'm_conv_dw_w', 'm_conv_dw_b', 'm_conv_ln_g', 'm_conv_ln_b', 'm_conv_w_out', 'm_conv_b_out', 'm_xa_norm', 'm_mem_norm', 'm_xa_wq', 'm_xa_wkv', 'm_xa_wo', 'm_ffn_norm', 'm_ffn_w_gu', 'm_ffn_w_down', 'm_final_norm', 'v_mix_norm_e', 'v_w_in_e', 'v_fox_f_bias', 'v_gmlp_ln_g', 'v_gmlp_ln_b', 'v_gmlp_w_s', 'v_gmlp_b_s', 'v_w_out_e', 'v_mix_norm_o', 'v_conv_w_in', 'v_conv_b_in', 'v_conv_dw_w', 'v_conv_dw_b', 'v_conv_ln_g', 'v_conv_ln_b', 'v_conv_w_out', 'v_conv_b_out', 'v_xa_norm', 'v_mem_norm', 'v_xa_wq', 'v_xa_wkv', 'v_xa_wo', 'v_ffn_norm', 'v_ffn_w_gu', 'v_ffn_w_down', 'v_final_norm']
TWIN_OUTPUTS = ['loss', 'grad_x', 'grad_mix_norm_e', 'grad_w_in_e', 'grad_fox_f_bias', 'grad_gmlp_ln_g', 'grad_gmlp_ln_b', 'grad_gmlp_w_s', 'grad_gmlp_b_s', 'grad_w_out_e', 'grad_mix_norm_o', 'grad_conv_w_in', 'grad_conv_b_in', 'grad_conv_dw_w', 'grad_conv_dw_b', 'grad_conv_ln_g', 'grad_conv_ln_b', 'grad_conv_w_out', 'grad_conv_b_out', 'grad_xa_norm', 'grad_mem_norm', 'grad_xa_wq', 'grad_xa_wkv', 'grad_xa_wo', 'grad_ffn_norm', 'grad_ffn_w_gu', 'grad_ffn_w_down', 'grad_final_norm', 'delta_mix_norm_e', 'delta_w_in_e', 'delta_fox_f_bias', 'delta_gmlp_ln_g', 'delta_gmlp_ln_b', 'delta_gmlp_w_s', 'delta_gmlp_b_s', 'delta_w_out_e', 'delta_mix_norm_o', 'delta_conv_w_in', 'delta_conv_b_in', 'delta_conv_dw_w', 'delta_conv_dw_b', 'delta_conv_ln_g', 'delta_conv_ln_b', 'delta_conv_w_out', 'delta_conv_b_out', 'delta_xa_norm', 'delta_mem_norm', 'delta_xa_wq', 'delta_xa_wkv', 'delta_xa_wo', 'delta_ffn_norm', 'delta_ffn_w_gu', 'delta_ffn_w_down', 'delta_final_norm', 'new_m_mix_norm_e', 'new_m_w_in_e', 'new_m_fox_f_bias', 'new_m_gmlp_ln_g', 'new_m_gmlp_ln_b', 'new_m_gmlp_w_s', 'new_m_gmlp_b_s', 'new_m_w_out_e', 'new_m_mix_norm_o', 'new_m_conv_w_in', 'new_m_conv_b_in', 'new_m_conv_dw_w', 'new_m_conv_dw_b', 'new_m_conv_ln_g', 'new_m_conv_ln_b', 'new_m_conv_w_out', 'new_m_conv_b_out', 'new_m_xa_norm', 'new_m_mem_norm', 'new_m_xa_wq', 'new_m_xa_wkv', 'new_m_xa_wo', 'new_m_ffn_norm', 'new_m_ffn_w_gu', 'new_m_ffn_w_down', 'new_m_final_norm', 'new_v_mix_norm_e', 'new_v_w_in_e', 'new_v_fox_f_bias', 'new_v_gmlp_ln_g', 'new_v_gmlp_ln_b', 'new_v_gmlp_w_s', 'new_v_gmlp_b_s', 'new_v_w_out_e', 'new_v_mix_norm_o', 'new_v_conv_w_in', 'new_v_conv_b_in', 'new_v_conv_dw_w', 'new_v_conv_dw_b', 'new_v_conv_ln_g', 'new_v_conv_ln_b', 'new_v_conv_w_out', 'new_v_conv_b_out', 'new_v_xa_norm', 'new_v_mem_norm', 'new_v_xa_wq', 'new_v_xa_wkv', 'new_v_xa_wo', 'new_v_ffn_norm', 'new_v_ffn_w_gu', 'new_v_ffn_w_down', 'new_v_final_norm']
TWIN_LEAF_KINDS = {'loss': 'loss', 'grad_x': 'grad_x', 'grad_mix_norm_e': 'grad_w', 'grad_w_in_e': 'grad_w', 'grad_fox_f_bias': 'grad_w', 'grad_gmlp_ln_g': 'grad_w', 'grad_gmlp_ln_b': 'grad_w', 'grad_gmlp_w_s': 'grad_w', 'grad_gmlp_b_s': 'grad_w', 'grad_w_out_e': 'grad_w', 'grad_mix_norm_o': 'grad_w', 'grad_conv_w_in': 'grad_w', 'grad_conv_b_in': 'grad_w', 'grad_conv_dw_w': 'grad_w', 'grad_conv_dw_b': 'grad_w', 'grad_conv_ln_g': 'grad_w', 'grad_conv_ln_b': 'grad_w', 'grad_conv_w_out': 'grad_w', 'grad_conv_b_out': 'grad_w', 'grad_xa_norm': 'grad_w', 'grad_mem_norm': 'grad_w', 'grad_xa_wq': 'grad_w', 'grad_xa_wkv': 'grad_w', 'grad_xa_wo': 'grad_w', 'grad_ffn_norm': 'grad_w', 'grad_ffn_w_gu': 'grad_w', 'grad_ffn_w_down': 'grad_w', 'grad_final_norm': 'grad_w', 'delta_mix_norm_e': 'delta_w', 'delta_w_in_e': 'delta_w', 'delta_fox_f_bias': 'delta_w', 'delta_gmlp_ln_g': 'delta_w', 'delta_gmlp_ln_b': 'delta_w', 'delta_gmlp_w_s': 'delta_w', 'delta_gmlp_b_s': 'delta_w', 'delta_w_out_e': 'delta_w', 'delta_mix_norm_o': 'delta_w', 'delta_conv_w_in': 'delta_w', 'delta_conv_b_in': 'delta_w', 'delta_conv_dw_w': 'delta_w', 'delta_conv_dw_b': 'delta_w', 'delta_conv_ln_g': 'delta_w', 'delta_conv_ln_b': 'delta_w', 'delta_conv_w_out': 'delta_w', 'delta_conv_b_out': 'delta_w', 'delta_xa_norm': 'delta_w', 'delta_mem_norm': 'delta_w', 'delta_xa_wq': 'delta_w', 'delta_xa_wkv': 'delta_w', 'delta_xa_wo': 'delta_w', 'delta_ffn_norm': 'delta_w', 'delta_ffn_w_gu': 'delta_w', 'delta_ffn_w_down': 'delta_w', 'delta_final_norm': 'delta_w', 'new_m_mix_norm_e': 'new_m', 'new_m_w_in_e': 'new_m', 'new_m_fox_f_bias': 'new_m', 'new_m_gmlp_ln_g': 'new_m', 'new_m_gmlp_ln_b': 'new_m', 'new_m_gmlp_w_s': 'new_m', 'new_m_gmlp_b_s': 'new_m', 'new_m_w_out_e': 'new_m', 'new_m_mix_norm_o': 'new_m', 'new_m_conv_w_in': 'new_m', 'new_m_conv_b_in': 'new_m', 'new_m_conv_dw_w': 'new_m', 'new_m_conv_dw_b': 'new_m', 'new_m_conv_ln_g': 'new_m', 'new_m_conv_ln_b': 'new_m', 'new_m_conv_w_out': 'new_m', 'new_m_conv_b_out': 'new_m', 'new_m_xa_norm': 'new_m', 'new_m_mem_norm': 'new_m', 'new_m_xa_wq': 'new_m', 'new_m_xa_wkv': 'new_m', 'new_m_xa_wo': 'new_m', 'new_m_ffn_norm': 'new_m', 'new_m_ffn_w_gu': 'new_m', 'new_m_ffn_w_down': 'new_m', 'new_m_final_norm': 'new_m', 'new_v_mix_norm_e': 'new_v', 'new_v_w_in_e': 'new_v', 'new_v_fox_f_bias': 'new_v', 'new_v_gmlp_ln_g': 'new_v', 'new_v_gmlp_ln_b': 'new_v', 'new_v_gmlp_w_s': 'new_v', 'new_v_gmlp_b_s': 'new_v', 'new_v_w_out_e': 'new_v', 'new_v_mix_norm_o': 'new_v', 'new_v_conv_w_in': 'new_v', 'new_v_conv_b_in': 'new_v', 'new_v_conv_dw_w': 'new_v', 'new_v_conv_dw_b': 'new_v', 'new_v_conv_ln_g': 'new_v', 'new_v_conv_ln_b': 'new_v', 'new_v_conv_w_out': 'new_v', 'new_v_conv_b_out': 'new_v', 'new_v_xa_norm': 'new_v', 'new_v_mem_norm': 'new_v', 'new_v_xa_wq': 'new_v', 'new_v_xa_wkv': 'new_v', 'new_v_xa_wo': 'new_v', 'new_v_ffn_norm': 'new_v', 'new_v_ffn_w_gu': 'new_v', 'new_v_ffn_w_down': 'new_v', 'new_v_final_norm': 'new_v'}


def _forward(args):
    return _fwd_reference(*[args[k] for k in FWD_PARAMS])


def _output_shape():
    out = _jax.eval_shape(lambda: _forward(_fwd_setup_inputs(0)))
    return out.shape, out.dtype

N_MICROBATCH = 1
ADAM_LR = 0.001
ADAM_B1 = 0.9
ADAM_B2 = 0.999
ADAM_EPS = 1e-08
ADAM_WD = 0.01
ADAM_STEP = 10
PER_EXAMPLE_BATCH_AXIS = {'x': 0, 'mem': 0, 'loss_target': 0}
SHARED_INPUTS = []
_WEIGHT_DTYPES = {'mix_norm_e': _jnp.float32, 'w_in_e': _jnp.float32, 'fox_f_bias': _jnp.float32, 'gmlp_ln_g': _jnp.float32, 'gmlp_ln_b': _jnp.float32, 'gmlp_w_s': _jnp.float32, 'gmlp_b_s': _jnp.float32, 'w_out_e': _jnp.float32, 'mix_norm_o': _jnp.float32, 'conv_w_in': _jnp.float32, 'conv_b_in': _jnp.float32, 'conv_dw_w': _jnp.float32, 'conv_dw_b': _jnp.float32, 'conv_ln_g': _jnp.float32, 'conv_ln_b': _jnp.float32, 'conv_w_out': _jnp.float32, 'conv_b_out': _jnp.float32, 'xa_norm': _jnp.float32, 'mem_norm': _jnp.float32, 'xa_wq': _jnp.float32, 'xa_wkv': _jnp.float32, 'xa_wo': _jnp.float32, 'ffn_norm': _jnp.float32, 'ffn_w_gu': _jnp.float32, 'ffn_w_down': _jnp.float32, 'final_norm': _jnp.float32}
MOMENT_SCALE = {'mix_norm_e': 2.090941e-01, 'w_in_e': 1.324646e-01, 'fox_f_bias': 6.487675e-01, 'gmlp_ln_g': 1.184555e-01, 'gmlp_ln_b': 1.243459e-01, 'gmlp_w_s': 8.242333e-02, 'gmlp_b_s': 1.185399e-01, 'w_out_e': 1.647419e-01, 'mix_norm_o': 1.311354e-01, 'conv_w_in': 8.699073e-02, 'conv_b_in': 1.241732e-01, 'conv_dw_w': 1.152217e-01, 'conv_dw_b': 2.410505e-01, 'conv_ln_g': 1.472912e-01, 'conv_ln_b': 1.482808e-01, 'conv_w_out': 1.133047e-01, 'conv_b_out': 2.218789e-01, 'xa_norm': 2.516829e-02, 'mem_norm': 3.466447e-02, 'xa_wq': 2.323032e-02, 'xa_wkv': 2.338346e-02, 'xa_wo': 2.351058e-02, 'ffn_norm': 1.603446e-01, 'ffn_w_gu': 6.732130e-02, 'ffn_w_down': 1.099907e-01, 'final_norm': 6.402116e+01}


def _to_microbatches(a, axis):
    t = _jnp.moveaxis(a, axis, 0)
    t = t.reshape((N_MICROBATCH, t.shape[0] // N_MICROBATCH) + t.shape[1:])
    return _jnp.moveaxis(t, 1, axis + 1)


def setup_inputs(seed: int = 0) -> dict:
    inp = _fwd_setup_inputs(seed)
    key = _jax.random.fold_in(_jax.random.key(seed), 7919)
    shape, _ = _output_shape()
    out = dict(inp)
    out["loss_target"] = _jax.random.normal(_jax.random.fold_in(key, 0), shape, _jnp.float32)
    for i, name in enumerate(TWIN_WEIGHTS):
        w = inp[name].astype(_jnp.float32)
        if MOMENT_SCALE is None:
            s = _jnp.sqrt(_jnp.mean(_jnp.square(w)) + 1e-30)
        else:
            s = MOMENT_SCALE[name]
        km, kv = _jax.random.split(_jax.random.fold_in(key, i + 1))
        out[name] = w
        out["m_" + name] = s * _jax.random.normal(km, w.shape, _jnp.float32)
        out["v_" + name] = (s * s) * _jax.random.uniform(kv, w.shape, _jnp.float32, 0.5, 1.5)
    if N_MICROBATCH > 1:
        for name, axis in PER_EXAMPLE_BATCH_AXIS.items():
            out[name] = _to_microbatches(out[name], axis)
    return {'x': out['x'], 'mem': out['mem'], 'mix_norm_e': out['mix_norm_e'], 'w_in_e': out['w_in_e'], 'fox_f_bias': out['fox_f_bias'], 'gmlp_ln_g': out['gmlp_ln_g'], 'gmlp_ln_b': out['gmlp_ln_b'], 'gmlp_w_s': out['gmlp_w_s'], 'gmlp_b_s': out['gmlp_b_s'], 'w_out_e': out['w_out_e'], 'mix_norm_o': out['mix_norm_o'], 'conv_w_in': out['conv_w_in'], 'conv_b_in': out['conv_b_in'], 'conv_dw_w': out['conv_dw_w'], 'conv_dw_b': out['conv_dw_b'], 'conv_ln_g': out['conv_ln_g'], 'conv_ln_b': out['conv_ln_b'], 'conv_w_out': out['conv_w_out'], 'conv_b_out': out['conv_b_out'], 'xa_norm': out['xa_norm'], 'mem_norm': out['mem_norm'], 'xa_wq': out['xa_wq'], 'xa_wkv': out['xa_wkv'], 'xa_wo': out['xa_wo'], 'ffn_norm': out['ffn_norm'], 'ffn_w_gu': out['ffn_w_gu'], 'ffn_w_down': out['ffn_w_down'], 'final_norm': out['final_norm'], 'loss_target': out['loss_target'], 'm_mix_norm_e': out['m_mix_norm_e'], 'm_w_in_e': out['m_w_in_e'], 'm_fox_f_bias': out['m_fox_f_bias'], 'm_gmlp_ln_g': out['m_gmlp_ln_g'], 'm_gmlp_ln_b': out['m_gmlp_ln_b'], 'm_gmlp_w_s': out['m_gmlp_w_s'], 'm_gmlp_b_s': out['m_gmlp_b_s'], 'm_w_out_e': out['m_w_out_e'], 'm_mix_norm_o': out['m_mix_norm_o'], 'm_conv_w_in': out['m_conv_w_in'], 'm_conv_b_in': out['m_conv_b_in'], 'm_conv_dw_w': out['m_conv_dw_w'], 'm_conv_dw_b': out['m_conv_dw_b'], 'm_conv_ln_g': out['m_conv_ln_g'], 'm_conv_ln_b': out['m_conv_ln_b'], 'm_conv_w_out': out['m_conv_w_out'], 'm_conv_b_out': out['m_conv_b_out'], 'm_xa_norm': out['m_xa_norm'], 'm_mem_norm': out['m_mem_norm'], 'm_xa_wq': out['m_xa_wq'], 'm_xa_wkv': out['m_xa_wkv'], 'm_xa_wo': out['m_xa_wo'], 'm_ffn_norm': out['m_ffn_norm'], 'm_ffn_w_gu': out['m_ffn_w_gu'], 'm_ffn_w_down': out['m_ffn_w_down'], 'm_final_norm': out['m_final_norm'], 'v_mix_norm_e': out['v_mix_norm_e'], 'v_w_in_e': out['v_w_in_e'], 'v_fox_f_bias': out['v_fox_f_bias'], 'v_gmlp_ln_g': out['v_gmlp_ln_g'], 'v_gmlp_ln_b': out['v_gmlp_ln_b'], 'v_gmlp_w_s': out['v_gmlp_w_s'], 'v_gmlp_b_s': out['v_gmlp_b_s'], 'v_w_out_e': out['v_w_out_e'], 'v_mix_norm_o': out['v_mix_norm_o'], 'v_conv_w_in': out['v_conv_w_in'], 'v_conv_b_in': out['v_conv_b_in'], 'v_conv_dw_w': out['v_conv_dw_w'], 'v_conv_dw_b': out['v_conv_dw_b'], 'v_conv_ln_g': out['v_conv_ln_g'], 'v_conv_ln_b': out['v_conv_ln_b'], 'v_conv_w_out': out['v_conv_w_out'], 'v_conv_b_out': out['v_conv_b_out'], 'v_xa_norm': out['v_xa_norm'], 'v_mem_norm': out['v_mem_norm'], 'v_xa_wq': out['v_xa_wq'], 'v_xa_wkv': out['v_xa_wkv'], 'v_xa_wo': out['v_xa_wo'], 'v_ffn_norm': out['v_ffn_norm'], 'v_ffn_w_gu': out['v_ffn_w_gu'], 'v_ffn_w_down': out['v_ffn_w_down'], 'v_final_norm': out['v_final_norm']}


def _loss(weights, diff, rest, loss_target):
    with _jax.named_scope("forward"):
        args = {**rest, TWIN_DIFF_INPUT: diff, **{k: w.astype(_WEIGHT_DTYPES[k]) for k, w in weights.items()}}
        y = _forward(args)
    with _jax.named_scope("loss_head"):
        err = _jnp.square(y.astype(_jnp.float32) - loss_target)
        return 0.5 * _jnp.sum(_jnp.mean(err, axis=-1)) if err.ndim else 0.5 * err


def _adamw(w, g, m, v):
    m = ADAM_B1 * m + (1.0 - ADAM_B1) * g
    v = ADAM_B2 * v + (1.0 - ADAM_B2) * _jnp.square(g)
    m_hat = m / (1.0 - ADAM_B1 ** ADAM_STEP)
    v_hat = v / (1.0 - ADAM_B2 ** ADAM_STEP)
    delta = -ADAM_LR * (m_hat / (_jnp.sqrt(v_hat) + ADAM_EPS) + ADAM_WD * w)
    return delta, m, v


def reference(x, mem, mix_norm_e, w_in_e, fox_f_bias, gmlp_ln_g, gmlp_ln_b, gmlp_w_s, gmlp_b_s, w_out_e, mix_norm_o, conv_w_in, conv_b_in, conv_dw_w, conv_dw_b, conv_ln_g, conv_ln_b, conv_w_out, conv_b_out, xa_norm, mem_norm, xa_wq, xa_wkv, xa_wo, ffn_norm, ffn_w_gu, ffn_w_down, final_norm, loss_target, m_mix_norm_e, m_w_in_e, m_fox_f_bias, m_gmlp_ln_g, m_gmlp_ln_b, m_gmlp_w_s, m_gmlp_b_s, m_w_out_e, m_mix_norm_o, m_conv_w_in, m_conv_b_in, m_conv_dw_w, m_conv_dw_b, m_conv_ln_g, m_conv_ln_b, m_conv_w_out, m_conv_b_out, m_xa_norm, m_mem_norm, m_xa_wq, m_xa_wkv, m_xa_wo, m_ffn_norm, m_ffn_w_gu, m_ffn_w_down, m_final_norm, v_mix_norm_e, v_w_in_e, v_fox_f_bias, v_gmlp_ln_g, v_gmlp_ln_b, v_gmlp_w_s, v_gmlp_b_s, v_w_out_e, v_mix_norm_o, v_conv_w_in, v_conv_b_in, v_conv_dw_w, v_conv_dw_b, v_conv_ln_g, v_conv_ln_b, v_conv_w_out, v_conv_b_out, v_xa_norm, v_mem_norm, v_xa_wq, v_xa_wkv, v_xa_wo, v_ffn_norm, v_ffn_w_gu, v_ffn_w_down, v_final_norm):
    given = dict(x=x, mem=mem, mix_norm_e=mix_norm_e, w_in_e=w_in_e, fox_f_bias=fox_f_bias, gmlp_ln_g=gmlp_ln_g, gmlp_ln_b=gmlp_ln_b, gmlp_w_s=gmlp_w_s, gmlp_b_s=gmlp_b_s, w_out_e=w_out_e, mix_norm_o=mix_norm_o, conv_w_in=conv_w_in, conv_b_in=conv_b_in, conv_dw_w=conv_dw_w, conv_dw_b=conv_dw_b, conv_ln_g=conv_ln_g, conv_ln_b=conv_ln_b, conv_w_out=conv_w_out, conv_b_out=conv_b_out, xa_norm=xa_norm, mem_norm=mem_norm, xa_wq=xa_wq, xa_wkv=xa_wkv, xa_wo=xa_wo, ffn_norm=ffn_norm, ffn_w_gu=ffn_w_gu, ffn_w_down=ffn_w_down, final_norm=final_norm, loss_target=loss_target, m_mix_norm_e=m_mix_norm_e, m_w_in_e=m_w_in_e, m_fox_f_bias=m_fox_f_bias, m_gmlp_ln_g=m_gmlp_ln_g, m_gmlp_ln_b=m_gmlp_ln_b, m_gmlp_w_s=m_gmlp_w_s, m_gmlp_b_s=m_gmlp_b_s, m_w_out_e=m_w_out_e, m_mix_norm_o=m_mix_norm_o, m_conv_w_in=m_conv_w_in, m_conv_b_in=m_conv_b_in, m_conv_dw_w=m_conv_dw_w, m_conv_dw_b=m_conv_dw_b, m_conv_ln_g=m_conv_ln_g, m_conv_ln_b=m_conv_ln_b, m_conv_w_out=m_conv_w_out, m_conv_b_out=m_conv_b_out, m_xa_norm=m_xa_norm, m_mem_norm=m_mem_norm, m_xa_wq=m_xa_wq, m_xa_wkv=m_xa_wkv, m_xa_wo=m_xa_wo, m_ffn_norm=m_ffn_norm, m_ffn_w_gu=m_ffn_w_gu, m_ffn_w_down=m_ffn_w_down, m_final_norm=m_final_norm, v_mix_norm_e=v_mix_norm_e, v_w_in_e=v_w_in_e, v_fox_f_bias=v_fox_f_bias, v_gmlp_ln_g=v_gmlp_ln_g, v_gmlp_ln_b=v_gmlp_ln_b, v_gmlp_w_s=v_gmlp_w_s, v_gmlp_b_s=v_gmlp_b_s, v_w_out_e=v_w_out_e, v_mix_norm_o=v_mix_norm_o, v_conv_w_in=v_conv_w_in, v_conv_b_in=v_conv_b_in, v_conv_dw_w=v_conv_dw_w, v_conv_dw_b=v_conv_dw_b, v_conv_ln_g=v_conv_ln_g, v_conv_ln_b=v_conv_ln_b, v_conv_w_out=v_conv_w_out, v_conv_b_out=v_conv_b_out, v_xa_norm=v_xa_norm, v_mem_norm=v_mem_norm, v_xa_wq=v_xa_wq, v_xa_wkv=v_xa_wkv, v_xa_wo=v_xa_wo, v_ffn_norm=v_ffn_norm, v_ffn_w_gu=v_ffn_w_gu, v_ffn_w_down=v_ffn_w_down, v_final_norm=v_final_norm)
    weights = {n: given[n] for n in TWIN_WEIGHTS}
    shared = {n: given[n] for n in SHARED_INPUTS}
    per_example = {n: given[n] for n in ['x', 'mem']}
    grad_fn = _jax.value_and_grad(_loss, argnums=(0, 1))

    def one_microbatch(ex, loss_target):
        ex = dict(ex)
        diff = ex.pop(TWIN_DIFF_INPUT)
        return grad_fn(weights, diff, {**shared, **ex}, loss_target)

    if N_MICROBATCH == 1:
        loss, (grad_w, grad_x) = one_microbatch(per_example, given["loss_target"])
    else:
        def body(carry, xs):
            loss_sum, grad_sum = carry
            l_k, (gw_k, gx_k) = one_microbatch(xs[0], xs[1])
            with _jax.named_scope("update"):
                return (loss_sum + l_k, _jax.tree.map(_jnp.add, grad_sum, gw_k)), gx_k

        init = (_jnp.zeros((), _jnp.float32), _jax.tree.map(_jnp.zeros_like, weights))
        (loss, grad_w), grad_x = _jax.lax.scan(body, init, (per_example, given["loss_target"]))
    with _jax.named_scope("update"):
        delta_w, new_m, new_v = {}, {}, {}
        for n in TWIN_WEIGHTS:
            delta_w[n], new_m[n], new_v[n] = _adamw(weights[n], grad_w[n], given["m_" + n], given["v_" + n])
    return (loss, grad_x, *[grad_w[n] for n in TWIN_WEIGHTS], *[delta_w[n] for n in TWIN_WEIGHTS],
            *[new_m[n] for n in TWIN_WEIGHTS], *[new_v[n] for n in TWIN_WEIGHTS])
```

```python
import functools
import math

import jax
import jax.numpy as jnp
from jax import lax
from jax.experimental import pallas as pl
from jax.experimental.pallas import tpu as pltpu

_F32 = jnp.float32
_MXU = jnp.bfloat16
_VMEM_LIMIT = 48 * 1024 * 1024
_LANES = 128
_EPS = 1e-6
_N_DEV = 8
_FOX_HD = 64
_FOX_SCALE = _FOX_HD ** -0.5
_CHUNK = 128
_GRP = 64
_CONV_K = 31
_HALO = 32
_XA_HEADS = 4
_GELU_C = math.sqrt(2.0 / math.pi)
_ADAM_LR, _ADAM_B1, _ADAM_B2, _ADAM_EPS, _ADAM_WD, _ADAM_STEP = 0.001, 0.9, 0.999, 1e-08, 0.01, 10
_FLAT_W = 1024
_FLAT_ALIGN = 16 * _FLAT_W
_BIG_ROWS = 128


def _params(*sem):
    return pltpu.CompilerParams(dimension_semantics=sem if sem else None, vmem_limit_bytes=_VMEM_LIMIT)


def _pick(n, pref):
    if n <= pref:
        return n
    best = None
    for t in range(_LANES, pref + 1, _LANES):
        if n % t == 0:
            best = t
    assert best is not None, (n, pref)
    return best


def _rows(n, pref):
    if n <= pref:
        return n
    t = pref
    while n % t:
        t //= 2
    assert t >= 8, (n, pref)
    return t


def _sigmoid(x):
    return 1.0 / (1.0 + jnp.exp(-x))


def _gelu(x):
    t = jnp.tanh(_GELU_C * (x + 0.044715 * (x * x * x)))
    return 0.5 * x * (1.0 + t)


def _gelu_grad(x):
    x2 = x * x
    t = jnp.tanh(_GELU_C * (x + 0.044715 * (x2 * x)))
    return 0.5 * (1.0 + t) + 0.5 * x * (1.0 - t * t) * (_GELU_C * (1.0 + 3.0 * 0.044715 * x2))


def _dot(a, b, ca, cb):
    return lax.dot_general(a, b, (((ca,), (cb,)), ((), ())), preferred_element_type=_F32)


def _mm(a, b, *, name, ta=False, tb=False, bias=None, res=None, out_dtype=_F32, tm=1024, tn=512, tk=1024):
    if ta:
        K, M = a.shape
    else:
        M, K = a.shape
    if tb:
        N, K2 = b.shape
    else:
        K2, N = b.shape
    assert K == K2, (a.shape, b.shape, ta, tb)
    tm, tn = _pick(M, tm), _pick(N, tn)
    tk = K if (not ta and K <= 2816) else _pick(K, tk)
    nk = K // tk
    grid = (M // tm, N // tn, nk)
    a_spec = pl.BlockSpec((tk, tm), lambda i, j, k: (k, i)) if ta else pl.BlockSpec((tm, tk), lambda i, j, k: (i, k))
    b_spec = pl.BlockSpec((tn, tk), lambda i, j, k: (j, k)) if tb else pl.BlockSpec((tk, tn), lambda i, j, k: (k, j))
    in_specs, args = [a_spec, b_spec], [a, b]
    if bias is not None:
        in_specs.append(pl.BlockSpec((1, tn), lambda i, j, k: (0, j)))
        args.append(bias.reshape(1, N).astype(_F32))
    if res is not None:
        in_specs.append(pl.BlockSpec((tm, tn), lambda i, j, k: (i, j)))
        args.append(res)
    has_bias, has_res = bias is not None, res is not None

    def body(*refs):
        a_ref, b_ref = refs[0], refs[1]
        pos = 2
        bias_ref = res_ref = None
        if has_bias:
            bias_ref = refs[pos]
            pos += 1
        if has_res:
            res_ref = refs[pos]
            pos += 1
        o_ref = refs[pos]
        acc_ref = refs[pos + 1] if nk > 1 else None
        p = _dot(a_ref[...].astype(_MXU), b_ref[...].astype(_MXU), 0 if ta else 1, 1 if tb else 0)

        def finish(acc):
            if has_bias:
                acc = acc + bias_ref[...]
            if has_res:
                acc = acc + res_ref[...]
            o_ref[...] = acc.astype(o_ref.dtype)

        if nk == 1:
            finish(p)
        else:
            k = pl.program_id(2)

            @pl.when(k == 0)
            def _():
                acc_ref[...] = p

            @pl.when(k > 0)
            def _():
                acc_ref[...] += p

            @pl.when(k == nk - 1)
            def _():
                finish(acc_ref[...])

    return pl.pallas_call(
        body,
        name=name,
        out_shape=jax.ShapeDtypeStruct((M, N), out_dtype),
        grid=grid,
        in_specs=in_specs,
        out_specs=pl.BlockSpec((tm, tn), lambda i, j, k: (i, j)),
        scratch_shapes=[pltpu.VMEM((tm, tn), _F32)] if nk > 1 else [],
        compiler_params=_params("parallel", "parallel", "arbitrary"),
    )(*args)


def _rms_fwd(x, g, *, name):
    N, D = x.shape
    tr = _rows(N, 512)

    def body(x_ref, g_ref, o_ref):
        xv = x_ref[...]
        r = lax.rsqrt(jnp.mean(xv * xv, axis=-1, keepdims=True) + _EPS)
        o_ref[...] = (xv * r * g_ref[...]).astype(o_ref.dtype)

    return pl.pallas_call(
        body,
        name=name,
        out_shape=jax.ShapeDtypeStruct((N, D), _MXU),
        grid=(N // tr,),
        in_specs=[pl.BlockSpec((tr, D), lambda i: (i, 0)), pl.BlockSpec((1, D), lambda i: (0, 0))],
        out_specs=pl.BlockSpec((tr, D), lambda i: (i, 0)),
        compiler_params=_params("parallel"),
    )(x, g.reshape(1, D))


def _rms_bwd(x, g, dh, dres, *, name):
    N, D = x.shape
    tr = _rows(N, 256)
    has_res = dres is not None

    def body(*refs):
        if has_res:
            x_ref, g_ref, dh_ref, dres_ref, dx_ref, dg_ref = refs
        else:
            x_ref, g_ref, dh_ref, dg_ref = refs
        xv = x_ref[...]
        r = lax.rsqrt(jnp.mean(xv * xv, axis=-1, keepdims=True) + _EPS)
        xh = xv * r
        dhv = dh_ref[...].astype(_F32)

        @pl.when(pl.program_id(0) == 0)
        def _():
            dg_ref[...] = jnp.zeros_like(dg_ref)

        dg_ref[...] += jnp.sum(dhv * xh, axis=0, keepdims=True)
        if has_res:
            dxn = dhv * g_ref[...]
            dx = r * (dxn - xh * jnp.mean(dxn * xh, axis=-1, keepdims=True))
            dx_ref[...] = dres_ref[...] + dx

    row = pl.BlockSpec((tr, D), lambda i: (i, 0))
    vec = pl.BlockSpec((1, D), lambda i: (0, 0))
    if has_res:
        out_shape = (jax.ShapeDtypeStruct((N, D), _F32), jax.ShapeDtypeStruct((1, D), _F32))
        out_specs = (row, vec)
        in_specs, args = [row, vec, row, row], (x, g.reshape(1, D), dh, dres)
    else:
        out_shape = jax.ShapeDtypeStruct((1, D), _F32)
        out_specs = vec
        in_specs, args = [row, vec, row], (x, g.reshape(1, D), dh)
    return pl.pallas_call(
        body, name=name, out_shape=out_shape, grid=(N // tr,), in_specs=in_specs, out_specs=out_specs,
        compiler_params=_params("arbitrary"),
    )(*args)


def _colsum(a, *, name):
    M, C = a.shape
    tr = _rows(M, 512)

    def body(a_ref, o_ref):
        @pl.when(pl.program_id(0) == 0)
        def _():
            o_ref[...] = jnp.zeros_like(o_ref)

        o_ref[...] += jnp.sum(a_ref[...].astype(_F32), axis=0, keepdims=True)

    return pl.pallas_call(
        body, name=name, out_shape=jax.ShapeDtypeStruct((1, C), _F32), grid=(M // tr,),
        in_specs=[pl.BlockSpec((tr, C), lambda i: (i, 0))], out_specs=pl.BlockSpec((1, C), lambda i: (0, 0)),
        compiler_params=_params("arbitrary"),
    )(a)


def _final_loss(x, g, tgt, *, name):
    N, D = x.shape
    tr = _rows(N, 256)

    def body(x_ref, g_ref, t_ref, loss_ref, dx_ref, dg_ref):
        xv = x_ref[...]
        r = lax.rsqrt(jnp.mean(xv * xv, axis=-1, keepdims=True) + _EPS)
        xh = xv * r
        gv = g_ref[...]
        diff = xh * gv - t_ref[...]

        @pl.when(pl.program_id(0) == 0)
        def _():
            loss_ref[...] = jnp.zeros_like(loss_ref)
            dg_ref[...] = jnp.zeros_like(dg_ref)

        part = jnp.sum(jnp.sum(diff * diff, axis=1, keepdims=True), axis=0, keepdims=True) * (0.5 / D)
        loss_ref[...] += jnp.broadcast_to(part, loss_ref.shape)
        dy = diff * (1.0 / D)
        dg_ref[...] += jnp.sum(dy * xh, axis=0, keepdims=True)
        dxn = dy * gv
        dx_ref[...] = r * (dxn - xh * jnp.mean(dxn * xh, axis=-1, keepdims=True))

    row = pl.BlockSpec((tr, D), lambda i: (i, 0))
    vec = pl.BlockSpec((1, D), lambda i: (0, 0))
    return pl.pallas_call(
        body, name=name,
        out_shape=(jax.ShapeDtypeStruct((8, _LANES), _F32), jax.ShapeDtypeStruct((N, D), _F32), jax.ShapeDtypeStruct((1, D), _F32)),
        grid=(N // tr,), in_specs=[row, vec, row],
        out_specs=(pl.BlockSpec((8, _LANES), lambda i: (0, 0)), row, vec),
        compiler_params=_params("arbitrary"),
    )(x, g.reshape(1, D), tgt)


def _swiglu_fwd(gu, *, name):
    N, H2 = gu.shape
    H = H2 // 2
    tr = _rows(N, 256)

    def body(g_ref, u_ref, o_ref):
        g = g_ref[...]
        o_ref[...] = (g * _sigmoid(g) * u_ref[...]).astype(o_ref.dtype)

    return pl.pallas_call(
        body, name=name, out_shape=jax.ShapeDtypeStruct((N, H), _MXU), grid=(N // tr,),
        in_specs=[pl.BlockSpec((tr, H), lambda i: (i, 0)), pl.BlockSpec((tr, H), lambda i: (i, 1))],
        out_specs=pl.BlockSpec((tr, H), lambda i: (i, 0)), compiler_params=_params("parallel"),
    )(gu, gu)


def _swiglu_bwd(gu, dact, *, name):
    N, H2 = gu.shape
    H = H2 // 2
    tr = _rows(N, 256)

    def body(g_ref, u_ref, d_ref, o_ref):
        g, u, d = g_ref[...], u_ref[...], d_ref[...]
        sg = _sigmoid(g)
        o_ref[:, :H] = (d * u * (sg * (1.0 + g * (1.0 - sg)))).astype(o_ref.dtype)
        o_ref[:, H:] = (d * (g * sg)).astype(o_ref.dtype)

    return pl.pallas_call(
        body, name=name, out_shape=jax.ShapeDtypeStruct((N, H2), _MXU), grid=(N // tr,),
        in_specs=[pl.BlockSpec((tr, H), lambda i: (i, 0)), pl.BlockSpec((tr, H), lambda i: (i, 1)),
                  pl.BlockSpec((tr, H), lambda i: (i, 0))],
        out_specs=pl.BlockSpec((tr, H2), lambda i: (i, 0)), compiler_params=_params("parallel"),
    )(gu, gu, dact)


def _gmlp_mix(vb, w, trans):
    tr, W = vb.shape
    lane = lax.broadcasted_iota(jnp.int32, (_CHUNK, _LANES), 1)
    rows = []
    for c in range(tr // _CHUNK):
        tiles = []
        for j in range(W // _LANES):
            t = vb[c * _CHUNK:(c + 1) * _CHUNK, j * _LANES:(j + 1) * _LANES]
            ma = _dot(w[2 * j], t, 0 if trans else 1, 0)
            mb = _dot(w[2 * j + 1], t, 0 if trans else 1, 0)
            tiles.append(jnp.where(lane < _GRP, ma, mb))
        rows.append(jnp.concatenate(tiles, axis=1))
    return jnp.concatenate(rows, axis=0)


def _tril_w(w_ref):
    r = lax.broadcasted_iota(jnp.int32, (_CHUNK, _CHUNK), 0)
    c = lax.broadcasted_iota(jnp.int32, (_CHUNK, _CHUNK), 1)
    return jnp.where((r >= c)[None], w_ref[...], 0.0).astype(_MXU)


def _layernorm_stats(v):
    mu = jnp.mean(v, axis=-1, keepdims=True)
    xc = v - mu
    rstd = lax.rsqrt(jnp.mean(xc * xc, axis=-1, keepdims=True) + _EPS)
    return xc * rstd, rstd


def _gmlp_fwd(proj, ln_g, ln_b, w_s, bias_full, *, name):
    N = proj.shape[0]
    W = ln_g.shape[-1]
    G = w_s.shape[0]
    tr = _rows(N, 512)
    ub, vb_ = 3, 4

    def body(u_ref, v_ref, g_ref, b_ref, w_ref, bias_ref, o_ref):
        u = _gelu(u_ref[...])
        xh, _ = _layernorm_stats(_gelu(v_ref[...]))
        vgn = xh * g_ref[...] + b_ref[...]
        mixed = _gmlp_mix(vgn.astype(_MXU), _tril_w(w_ref), False)
        bias = jnp.concatenate([bias_ref[...]] * (tr // _CHUNK), axis=0)
        o_ref[...] = (u * (mixed + bias)).astype(o_ref.dtype)

    vec = pl.BlockSpec((1, W), lambda i: (0, 0))
    return pl.pallas_call(
        body, name=name, out_shape=jax.ShapeDtypeStruct((N, W), _MXU), grid=(N // tr,),
        in_specs=[pl.BlockSpec((tr, W), lambda i: (i, ub)), pl.BlockSpec((tr, W), lambda i: (i, vb_)), vec, vec,
                  pl.BlockSpec((G, _CHUNK, _CHUNK), lambda i: (0, 0, 0)), pl.BlockSpec((_CHUNK, W), lambda i: (0, 0))],
        out_specs=pl.BlockSpec((tr, W), lambda i: (i, 0)), compiler_params=_params("parallel"),
    )(proj, proj, ln_g.reshape(1, W), ln_b.reshape(1, W), w_s, bias_full)


def _gmlp_bwd(proj, da_src, da_blk, ln_g, ln_b, w_s, bias_full, *, name):
    N = proj.shape[0]
    W = ln_g.shape[-1]
    G = w_s.shape[0]
    tr = _rows(N, 512)
    nch = tr // _CHUNK

    def body(u_ref, v_ref, da_ref, g_ref, b_ref, w_ref, bias_ref, dz_ref, dg_ref, db_ref, dw_ref, dbias_ref):
        @pl.when(pl.program_id(0) == 0)
        def _():
            dg_ref[...] = jnp.zeros_like(dg_ref)
            db_ref[...] = jnp.zeros_like(db_ref)
            dw_ref[...] = jnp.zeros_like(dw_ref)
            dbias_ref[...] = jnp.zeros_like(dbias_ref)

        u_pre, v_pre = u_ref[...], v_ref[...]
        ug = _gelu(u_pre)
        xh, rstd = _layernorm_stats(_gelu(v_pre))
        lg = g_ref[...]
        vgn = xh * lg + b_ref[...]
        vb = vgn.astype(_MXU)
        wt = _tril_w(w_ref)
        mixed = _gmlp_mix(vb, wt, False)
        bias = jnp.concatenate([bias_ref[...]] * nch, axis=0)
        da = da_ref[...].astype(_F32)
        du = da * (mixed + bias)
        dm = da * ug
        dmb = dm.astype(_MXU)
        lane = lax.broadcasted_iota(jnp.int32, (_CHUNK, _LANES), 1)
        r = lax.broadcasted_iota(jnp.int32, (_CHUNK, _CHUNK), 0)
        c = lax.broadcasted_iota(jnp.int32, (_CHUNK, _CHUNK), 1)
        tril = r >= c
        dmsum = dm[0:_CHUNK]
        for ch in range(1, nch):
            dmsum = dmsum + dm[ch * _CHUNK:(ch + 1) * _CHUNK]
        dbias = jnp.zeros((_CHUNK, _LANES), _F32)
        for j in range(W // _LANES):
            tile = dmsum[:, j * _LANES:(j + 1) * _LANES]
            sa = jnp.sum(jnp.where(lane < _GRP, tile, 0.0), axis=1, keepdims=True)
            sb = jnp.sum(jnp.where(lane >= _GRP, tile, 0.0), axis=1, keepdims=True)
            dbias = dbias + jnp.where(lane == 2 * j, sa, 0.0) + jnp.where(lane == 2 * j + 1, sb, 0.0)
            acc_a = jnp.zeros((_CHUNK, _CHUNK), _F32)
            acc_b = jnp.zeros((_CHUNK, _CHUNK), _F32)
            for ch in range(nch):
                dt = dmb[ch * _CHUNK:(ch + 1) * _CHUNK, j * _LANES:(j + 1) * _LANES]
                vt = vb[ch * _CHUNK:(ch + 1) * _CHUNK, j * _LANES:(j + 1) * _LANES]
                acc_a = acc_a + _dot(jnp.where(lane < _GRP, dt, jnp.zeros_like(dt)), vt, 1, 1)
                acc_b = acc_b + _dot(jnp.where(lane >= _GRP, dt, jnp.zeros_like(dt)), vt, 1, 1)
            dw_ref[2 * j] += jnp.where(tril, acc_a, 0.0)
            dw_ref[2 * j + 1] += jnp.where(tril, acc_b, 0.0)
        dbias_ref[...] += dbias
        dvgn = _gmlp_mix(dmb, wt, True)
        dg_ref[...] += jnp.sum(dvgn * xh, axis=0, keepdims=True)
        db_ref[...] += jnp.sum(dvgn, axis=0, keepdims=True)
        dxh = dvgn * lg
        dvg = rstd * (dxh - jnp.mean(dxh, axis=-1, keepdims=True) - xh * jnp.mean(dxh * xh, axis=-1, keepdims=True))
        dz_ref[:, :W] = (du * _gelu_grad(u_pre)).astype(dz_ref.dtype)
        dz_ref[:, W:] = (dvg * _gelu_grad(v_pre)).astype(dz_ref.dtype)

    vec = pl.BlockSpec((1, W), lambda i: (0, 0))
    wspec = pl.BlockSpec((G, _CHUNK, _CHUNK), lambda i: (0, 0, 0))
    return pl.pallas_call(
        body, name=name,
        out_shape=(jax.ShapeDtypeStruct((N, 2 * W), _MXU), jax.ShapeDtypeStruct((1, W), _F32), jax.ShapeDtypeStruct((1, W), _F32),
                   jax.ShapeDtypeStruct((G, _CHUNK, _CHUNK), _F32), jax.ShapeDtypeStruct((_CHUNK, _LANES), _F32)),
        grid=(N // tr,),
        in_specs=[pl.BlockSpec((tr, W), lambda i: (i, 3)), pl.BlockSpec((tr, W), lambda i: (i, 4)),
                  pl.BlockSpec((tr, W), lambda i: (i, da_blk)), vec, vec, wspec, pl.BlockSpec((_CHUNK, W), lambda i: (0, 0))],
        out_specs=(pl.BlockSpec((tr, 2 * W), lambda i: (i, 0)), vec, vec, wspec, pl.BlockSpec((_CHUNK, _LANES), lambda i: (0, 0))),
        compiler_params=_params("arbitrary"),
    )(proj, proj, da_src, ln_g.reshape(1, W), ln_b.reshape(1, W), w_s, bias_full)


def _lane_cumsum(v):
    T = v.shape[1]
    lane = lax.broadcasted_iota(jnp.int32, (8, _LANES), 1)
    carry = jnp.zeros((8, 1), _F32)
    out = []
    for ch in range(T // _LANES):
        blk = v[:, ch * _LANES:(ch + 1) * _LANES]
        sh = 1
        while sh < _LANES:
            blk = blk + jnp.where(lane >= sh, pltpu.roll(blk, sh, 1), 0.0)
            sh *= 2
        blk = blk + carry
        carry = blk[:, _LANES - 1:_LANES]
        out.append(blk)
    return jnp.concatenate(out, axis=1), carry


def _log_sigmoid(x):
    return jnp.minimum(x, 0.0) - jnp.log(1.0 + jnp.exp(-jnp.abs(x)))


def _fox_cum(proj3, f_blk, f_bias, *, name):
    B, T, _ = proj3.shape
    H = f_bias.shape[-1]
    assert H == 8

    def body(f_ref, b_ref, o_ref):
        x = f_ref[0].T[0:8, :] + b_ref[...]
        cum, _ = _lane_cumsum(_log_sigmoid(x))
        o_ref[0] = cum

    return pl.pallas_call(
        body, name=name, out_shape=jax.ShapeDtypeStruct((B, 8, T), _F32), grid=(B,),
        in_specs=[pl.BlockSpec((1, T, _LANES), lambda b: (b, 0, f_blk)), pl.BlockSpec((8, 1), lambda b: (0, 0))],
        out_specs=pl.BlockSpec((1, 8, T), lambda b: (b, 0, 0)), compiler_params=_params("parallel"),
    )(proj3, f_bias.reshape(8, 1))


def _fox_cum_bwd(proj3, f_blk, f_bias, dcum, *, name):
    B, T, _ = proj3.shape

    def body(f_ref, b_ref, dc_ref, df_ref, dbias_ref):
        @pl.when(pl.program_id(0) == 0)
        def _():
            dbias_ref[...] = jnp.zeros_like(dbias_ref)

        x = f_ref[0].T[0:8, :] + b_ref[...]
        dc = dc_ref[0]
        incl, total = _lane_cumsum(dc)
        dlf = total - incl + dc
        df = dlf * _sigmoid(-x)
        dbias_ref[...] += jnp.broadcast_to(jnp.sum(df, axis=1, keepdims=True), dbias_ref.shape)
        full = jnp.concatenate([df, jnp.zeros((_LANES - 8, T), _F32)], axis=0)
        df_ref[0] = full.T

    return pl.pallas_call(
        body, name=name,
        out_shape=(jax.ShapeDtypeStruct((B, T, _LANES), _F32), jax.ShapeDtypeStruct((8, _LANES), _F32)), grid=(B,),
        in_specs=[pl.BlockSpec((1, T, _LANES), lambda b: (b, 0, f_blk)), pl.BlockSpec((8, 1), lambda b: (0, 0)),
                  pl.BlockSpec((1, 8, T), lambda b: (b, 0, 0))],
        out_specs=(pl.BlockSpec((1, T, _LANES), lambda b: (b, 0, 0)), pl.BlockSpec((8, _LANES), lambda b: (0, 0))),
        compiler_params=_params("arbitrary"),
    )(proj3, f_bias.reshape(8, 1), dcum)


def _cum_row(cum_ref, h, start, size):
    blk = cum_ref[0, :, pl.ds(start, size)]
    sub = lax.broadcasted_iota(jnp.int32, (blk.shape[0], 1), 0)
    return jnp.sum(jnp.where(sub == h, blk, 0.0), axis=0, keepdims=True)


def _causal(tq, q0, k0):
    r = lax.broadcasted_iota(jnp.int32, (tq, tq), 0)
    c = lax.broadcasted_iota(jnp.int32, (tq, tq), 1)
    return (r + q0) >= (c + k0)


def _fox_fwd(proj3, cum, *, name):
    B, T, _ = proj3.shape
    H = cum.shape[1]
    W = H * _FOX_HD
    npair = W // _LANES
    tq = _rows(T, 256)
    nq = T // tq

    def body(q_ref, k_ref, v_ref, cum_ref, o_ref, lse_ref):
        p = pl.program_id(1)
        i = pl.program_id(2)
        q0 = pl.multiple_of(i * tq, tq)
        lane = lax.broadcasted_iota(jnp.int32, (1, _LANES), 1)
        q2 = q_ref[0] * _FOX_SCALE
        o2 = jnp.zeros((tq, _LANES), _F32)
        for hh in range(2):
            msk = (lane < _FOX_HD) if hh == 0 else (lane >= _FOX_HD)
            h = 2 * p + hh
            qm = jnp.where(msk, q2, 0.0).astype(_MXU)
            c0 = _cum_row(cum_ref, h, q0, _LANES)[:, 0:1]

            def step(jj, carry, masked, qm=qm, msk=msk, h=h, c0=c0):
                m_prev, l_prev, acc = carry
                k0 = pl.multiple_of(jj * tq, tq)
                k2 = k_ref[0, pl.ds(k0, tq), :].astype(_MXU)
                vm = jnp.where(msk, v_ref[0, pl.ds(k0, tq), :], 0.0).astype(_MXU)
                s = _dot(qm, k2, 1, 1) + (c0 - _cum_row(cum_ref, h, k0, tq))
                if masked:
                    s = jnp.where(_causal(tq, q0, k0), s, -jnp.inf)
                m_new = jnp.maximum(m_prev, jnp.max(s, axis=1, keepdims=True))
                alpha = jnp.exp(m_prev - m_new)
                e = jnp.exp(s - m_new)
                l_new = alpha * l_prev + jnp.sum(e, axis=1, keepdims=True)
                acc = alpha * acc + _dot(e.astype(_MXU), vm, 1, 0)
                return m_new, l_new, acc

            init = (jnp.full((tq, 1), -jnp.inf, _F32), jnp.zeros((tq, 1), _F32), jnp.zeros((tq, _LANES), _F32))
            carry = lax.fori_loop(0, i, functools.partial(step, masked=False), init)
            m, l, acc = step(i, carry, True)
            o2 = o2 + acc / l
            lse_ref[0, hh] = jnp.broadcast_to(m + jnp.log(l), (tq, _LANES))
        o_ref[0] = o2.astype(o_ref.dtype)

    return pl.pallas_call(
        body, name=name,
        out_shape=(jax.ShapeDtypeStruct((B, T, W), _MXU), jax.ShapeDtypeStruct((B, H, T, _LANES), _F32)),
        grid=(B, npair, nq),
        in_specs=[pl.BlockSpec((1, tq, _LANES), lambda b, p, i: (b, i, p)),
                  pl.BlockSpec((1, T, _LANES), lambda b, p, i: (b, 0, npair + p)),
                  pl.BlockSpec((1, T, _LANES), lambda b, p, i: (b, 0, 2 * npair + p)),
                  pl.BlockSpec((1, H, T), lambda b, p, i: (b, 0, 0))],
        out_specs=(pl.BlockSpec((1, tq, _LANES), lambda b, p, i: (b, i, p)),
                   pl.BlockSpec((1, 2, tq, _LANES), lambda b, p, i: (b, p, i, 0))),
        compiler_params=_params("parallel", "parallel", "parallel"),
    )(proj3, proj3, proj3, cum)


def _fox_bwd(proj3, cum, do3, lse, *, name):
    B, T, _ = proj3.shape
    H = cum.shape[1]
    W = H * _FOX_HD
    npair = W // _LANES
    tq = _rows(T, 256)
    nq = T // tq

    def body(q_ref, k_ref, v_ref, cum_ref, do_ref, lse_ref, dq_ref, dk_ref, dv_ref, dc_ref, p_scr, dp_scr, dk_acc, dv_acc, dc_acc):
        p = pl.program_id(1)
        i = pl.program_id(2)
        q0 = pl.multiple_of(i * tq, tq)
        lane = lax.broadcasted_iota(jnp.int32, (1, _LANES), 1)

        @pl.when(i == 0)
        def _():
            dk_acc[...] = jnp.zeros_like(dk_acc)
            dv_acc[...] = jnp.zeros_like(dv_acc)
            dc_acc[...] = jnp.zeros_like(dc_acc)

        q2 = q_ref[0] * _FOX_SCALE
        do2 = do_ref[0].astype(_F32)
        dq2 = jnp.zeros((tq, _LANES), _F32)
        for hh in range(2):
            msk = (lane < _FOX_HD) if hh == 0 else (lane >= _FOX_HD)
            h = 2 * p + hh
            qm = jnp.where(msk, q2, 0.0).astype(_MXU)
            dom = jnp.where(msk, do2, 0.0).astype(_MXU)
            c0 = _cum_row(cum_ref, h, q0, _LANES)[:, 0:1]
            lse_h = lse_ref[0, hh][:, 0:1]

            def first(jj, delta, masked, h=h, qm=qm, dom=dom, c0=c0, lse_h=lse_h):
                k0 = pl.multiple_of(jj * tq, tq)
                kb = k_ref[0, pl.ds(k0, tq), :].astype(_MXU)
                vb = v_ref[0, pl.ds(k0, tq), :].astype(_MXU)
                s = _dot(qm, kb, 1, 1) + (c0 - _cum_row(cum_ref, h, k0, tq))
                pr = jnp.exp(s - lse_h)
                if masked:
                    pr = jnp.where(_causal(tq, q0, k0), pr, 0.0)
                dp = _dot(dom, vb, 1, 1)
                p_scr[jj] = pr
                dp_scr[jj] = dp
                return delta + jnp.sum(pr * dp, axis=1, keepdims=True)

            delta = lax.fori_loop(0, i, functools.partial(first, masked=False), jnp.zeros((tq, 1), _F32))
            delta = first(i, delta, True)

            def second(jj, dq, msk=msk, qm=qm, dom=dom, delta=delta, hh=hh):
                k0 = pl.multiple_of(jj * tq, tq)
                km = jnp.where(msk, k_ref[0, pl.ds(k0, tq), :], 0.0).astype(_MXU)
                pr = p_scr[jj]
                ds = pr * (dp_scr[jj] - delta)
                dsb = ds.astype(_MXU)
                dv_acc[pl.ds(k0, tq), :] += _dot(pr.astype(_MXU), dom, 0, 0)
                dk_acc[pl.ds(k0, tq), :] += _dot(dsb, qm, 0, 0)
                dc_acc[hh:hh + 1, pl.ds(k0, tq)] += jnp.sum(ds, axis=0, keepdims=True)
                return dq + _dot(dsb, km, 1, 0)

            dq2 = dq2 + lax.fori_loop(0, i + 1, second, jnp.zeros((tq, _LANES), _F32))
        dq_ref[0] = (dq2 * _FOX_SCALE).astype(dq_ref.dtype)

        @pl.when(i == nq - 1)
        def _():
            dk_ref[0] = dk_acc[...].astype(dk_ref.dtype)
            dv_ref[0] = dv_acc[...].astype(dv_ref.dtype)
            dc_ref[0, 0] = -dc_acc[...]

    full = lambda blk: pl.BlockSpec((1, T, _LANES), lambda b, p, i, blk=blk: (b, 0, blk * npair + p))
    part = lambda blk: pl.BlockSpec((1, tq, _LANES), lambda b, p, i, blk=blk: (b, i, blk * npair + p))
    dq, dk, dv, dcum = pl.pallas_call(
        body, name=name,
        out_shape=(jax.ShapeDtypeStruct((B, T, W), _MXU), jax.ShapeDtypeStruct((B, T, W), _MXU),
                   jax.ShapeDtypeStruct((B, T, W), _MXU), jax.ShapeDtypeStruct((B, npair, 2, T), _F32)),
        grid=(B, npair, nq),
        in_specs=[part(0), full(1), full(2), pl.BlockSpec((1, H, T), lambda b, p, i: (b, 0, 0)), part(0),
                  pl.BlockSpec((1, 2, tq, _LANES), lambda b, p, i: (b, p, i, 0))],
        out_specs=(part(0), full(0), full(0), pl.BlockSpec((1, 1, 2, T), lambda b, p, i: (b, p, 0, 0))),
        scratch_shapes=[pltpu.VMEM((nq, tq, tq), _F32), pltpu.VMEM((nq, tq, tq), _F32), pltpu.VMEM((T, _LANES), _F32),
                        pltpu.VMEM((T, _LANES), _F32), pltpu.VMEM((2, T), _F32)],
        compiler_params=_params("parallel", "parallel", "arbitrary"),
    )(proj3, proj3, proj3, cum, do3, lse)
    return dq, dk, dv, dcum


def _xa_probs(qh, kh, scale):
    s = _dot(qh, kh, 1, 1) * scale
    e = jnp.exp(s - jnp.max(s, axis=1, keepdims=True))
    return e / jnp.sum(e, axis=1, keepdims=True)


def _xa_fwd(q3, kv3, *, name):
    B, T, D = q3.shape
    M = kv3.shape[1]
    hd = D // _XA_HEADS
    scale = hd ** -0.5
    tq = _rows(T, 512)

    def body(q_ref, kv_ref, o_ref):
        for h in range(_XA_HEADS):
            sl = slice(h * hd, (h + 1) * hd)
            p = _xa_probs(q_ref[0, :, sl], kv_ref[0, :, sl], scale)
            o_ref[0, :, sl] = _dot(p.astype(_MXU), kv_ref[0, :, D + h * hd:D + (h + 1) * hd], 1, 0).astype(o_ref.dtype)

    return pl.pallas_call(
        body, name=name, out_shape=jax.ShapeDtypeStruct((B, T, D), _MXU), grid=(B, T // tq),
        in_specs=[pl.BlockSpec((1, tq, D), lambda b, i: (b, i, 0)), pl.BlockSpec((1, M, 2 * D), lambda b, i: (b, 0, 0))],
        out_specs=pl.BlockSpec((1, tq, D), lambda b, i: (b, i, 0)), compiler_params=_params("parallel", "parallel"),
    )(q3, kv3)


def _xa_bwd(q3, kv3, do3, *, name):
    B, T, D = q3.shape
    M = kv3.shape[1]
    hd = D // _XA_HEADS
    scale = hd ** -0.5
    tq = _rows(T, 512)

    def body(q_ref, kv_ref, do_ref, dq_ref, dkv_ref):
        @pl.when(pl.program_id(1) == 0)
        def _():
            dkv_ref[...] = jnp.zeros_like(dkv_ref)

        for h in range(_XA_HEADS):
            sl = slice(h * hd, (h + 1) * hd)
            slv = slice(D + h * hd, D + (h + 1) * hd)
            qh, kh, vh, doh = q_ref[0, :, sl], kv_ref[0, :, sl], kv_ref[0, :, slv], do_ref[0, :, sl]
            p = _xa_probs(qh, kh, scale)
            dkv_ref[0, :, slv] += _dot(p.astype(_MXU), doh, 0, 0)
            dp = _dot(doh, vh, 1, 1)
            ds = (p * (dp - jnp.sum(p * dp, axis=1, keepdims=True))).astype(_MXU)
            dq_ref[0, :, sl] = (_dot(ds, kh, 1, 0) * scale).astype(dq_ref.dtype)
            dkv_ref[0, :, sl] += _dot(ds, qh, 0, 0) * scale

    blk = pl.BlockSpec((1, tq, D), lambda b, i: (b, i, 0))
    kvs = pl.BlockSpec((1, M, 2 * D), lambda b, i: (b, 0, 0))
    return pl.pallas_call(
        body, name=name,
        out_shape=(jax.ShapeDtypeStruct((B, T, D), _MXU), jax.ShapeDtypeStruct((B, M, 2 * D), _F32)),
        grid=(B, T // tq), in_specs=[blk, kvs, blk], out_specs=(blk, kvs),
        compiler_params=_params("parallel", "arbitrary"),
    )(q3, kv3, do3)


def _glu_fwd(ag, *, name):
    N, C2 = ag.shape
    C = C2 // 2
    tr = _rows(N, 512)

    def body(a_ref, g_ref, o_ref):
        o_ref[...] = a_ref[...] * _sigmoid(g_ref[...])

    return pl.pallas_call(
        body, name=name, out_shape=jax.ShapeDtypeStruct((N, C), _F32), grid=(N // tr,),
        in_specs=[pl.BlockSpec((tr, C), lambda i: (i, 0)), pl.BlockSpec((tr, C), lambda i: (i, 1))],
        out_specs=pl.BlockSpec((tr, C), lambda i: (i, 0)), compiler_params=_params("parallel"),
    )(ag, ag)


def _conv_fwd(y3, dw_w, dw_b, ln_g, ln_b, *, name):
    B, T, C = y3.shape
    tt = _rows(T, 256)
    nt = T // tt

    def body(prev_ref, cur_ref, w_ref, b_ref, g_ref, lb_ref, y2_ref, y4_ref, ext):
        i = pl.program_id(1)
        ext[0:_HALO, :] = jnp.where(i > 0, prev_ref[0, tt - _HALO:tt, :], 0.0)
        ext[_HALO:_HALO + tt, :] = cur_ref[0]
        acc = jnp.broadcast_to(b_ref[...], (tt, C))
        for j in range(_CONV_K):
            off = _HALO - (_CONV_K - 1) + j
            acc = acc + w_ref[j:j + 1, :] * ext[off:off + tt, :]
        y2_ref[0] = acc
        xh, _ = _layernorm_stats(acc)
        z = xh * g_ref[...] + lb_ref[...]
        y4_ref[0] = (z * _sigmoid(z)).astype(y4_ref.dtype)

    vec = pl.BlockSpec((1, C), lambda b, i: (0, 0))
    blk = pl.BlockSpec((1, tt, C), lambda b, i: (b, i, 0))
    return pl.pallas_call(
        body, name=name,
        out_shape=(jax.ShapeDtypeStruct((B, T, C), _F32), jax.ShapeDtypeStruct((B, T, C), _MXU)),
        grid=(B, nt),
        in_specs=[pl.BlockSpec((1, tt, C), lambda b, i: (b, jnp.maximum(i - 1, 0), 0)), blk,
                  pl.BlockSpec((_HALO, C), lambda b, i: (0, 0)), vec, vec, vec],
        out_specs=(blk, blk), scratch_shapes=[pltpu.VMEM((tt + _HALO, C), _F32)],
        compiler_params=_params("parallel", "parallel"),
    )(y3, y3, dw_w, dw_b.reshape(1, C), ln_g.reshape(1, C), ln_b.reshape(1, C))


def _conv_ln_bwd(y2, dy4, ln_g, ln_b, *, name):
    N, C = y2.shape
    tr = _rows(N, 256)

    def body(y_ref, d_ref, g_ref, b_ref, dy_ref, dg_ref, db_ref, dwb_ref):
        @pl.when(pl.program_id(0) == 0)
        def _():
            dg_ref[...] = jnp.zeros_like(dg_ref)
            db_ref[...] = jnp.zeros_like(db_ref)
            dwb_ref[...] = jnp.zeros_like(dwb_ref)

        xh, rstd = _layernorm_stats(y_ref[...])
        gv = g_ref[...]
        z = xh * gv + b_ref[...]
        sg = _sigmoid(z)
        dz = d_ref[...] * (sg * (1.0 + z * (1.0 - sg)))
        dg_ref[...] += jnp.sum(dz * xh, axis=0, keepdims=True)
        db_ref[...] += jnp.sum(dz, axis=0, keepdims=True)
        dxh = dz * gv
        dy = rstd * (dxh - jnp.mean(dxh, axis=-1, keepdims=True) - xh * jnp.mean(dxh * xh, axis=-1, keepdims=True))
        dwb_ref[...] += jnp.sum(dy, axis=0, keepdims=True)
        dy_ref[...] = dy

    row = pl.BlockSpec((tr, C), lambda i: (i, 0))
    vec = pl.BlockSpec((1, C), lambda i: (0, 0))
    v = jax.ShapeDtypeStruct((1, C), _F32)
    return pl.pallas_call(
        body, name=name, out_shape=(jax.ShapeDtypeStruct((N, C), _F32), v, v, v), grid=(N // tr,),
        in_specs=[row, row, vec, vec], out_specs=(row, vec, vec, vec), compiler_params=_params("arbitrary"),
    )(y2, dy4, ln_g.reshape(1, C), ln_b.reshape(1, C))


def _conv_bwd(y3, dy23, ag3, dw_w, *, name):
    B, T, C = y3.shape
    tt = _rows(T, 256)
    nt = T // tt

    def body(yp_ref, yc_ref, dc_ref, dn_ref, a_ref, g_ref, w_ref, dag_ref, dw_ref, dbin_ref, yext, dext):
        b = pl.program_id(0)
        i = pl.program_id(1)

        @pl.when((b == 0) & (i == 0))
        def _():
            dw_ref[...] = jnp.zeros_like(dw_ref)
            dbin_ref[...] = jnp.zeros_like(dbin_ref)

        yext[0:_HALO, :] = jnp.where(i > 0, yp_ref[0, tt - _HALO:tt, :], 0.0)
        yext[_HALO:_HALO + tt, :] = yc_ref[0]
        d_cur = dc_ref[0]
        dext[0:tt, :] = d_cur
        dext[tt:tt + _HALO, :] = jnp.where(i < nt - 1, dn_ref[0, 0:_HALO, :], 0.0)
        dy = jnp.zeros((tt, C), _F32)
        for j in range(_CONV_K):
            sh = _CONV_K - 1 - j
            dy = dy + w_ref[j:j + 1, :] * dext[sh:sh + tt, :]
            off = _HALO - sh
            dw_ref[j:j + 1, :] += jnp.sum(d_cur * yext[off:off + tt, :], axis=0, keepdims=True)
        a, g = a_ref[0], g_ref[0]
        sg = _sigmoid(g)
        da = dy * sg
        dg = dy * a * (sg * (1.0 - sg))
        dag_ref[0, :, :C] = da.astype(dag_ref.dtype)
        dag_ref[0, :, C:] = dg.astype(dag_ref.dtype)
        dbin_ref[:, :C] += jnp.sum(da, axis=0, keepdims=True)
        dbin_ref[:, C:] += jnp.sum(dg, axis=0, keepdims=True)

    blk = pl.BlockSpec((1, tt, C), lambda b, i: (b, i, 0))
    return pl.pallas_call(
        body, name=name,
        out_shape=(jax.ShapeDtypeStruct((B, T, 2 * C), _MXU), jax.ShapeDtypeStruct((_HALO, C), _F32),
                   jax.ShapeDtypeStruct((1, 2 * C), _F32)),
        grid=(B, nt),
        in_specs=[pl.BlockSpec((1, tt, C), lambda b, i: (b, jnp.maximum(i - 1, 0), 0)), blk, blk,
                  pl.BlockSpec((1, tt, C), lambda b, i: (b, jnp.minimum(i + 1, nt - 1), 0)),
                  blk, pl.BlockSpec((1, tt, C), lambda b, i: (b, i, 1)), pl.BlockSpec((_HALO, C), lambda b, i: (0, 0))],
        out_specs=(pl.BlockSpec((1, tt, 2 * C), lambda b, i: (b, i, 0)), pl.BlockSpec((_HALO, C), lambda b, i: (0, 0)),
                   pl.BlockSpec((1, 2 * C), lambda b, i: (0, 0))),
        scratch_shapes=[pltpu.VMEM((tt + _HALO, C), _F32), pltpu.VMEM((tt + _HALO, C), _F32)],
        compiler_params=_params("arbitrary", "arbitrary"),
    )(y3, y3, dy23, dy23, ag3, ag3, dw_w)


def _exchange(src, *, per_peer, name):
    R, C = src.shape[-2:]

    def body(src_ref, dst_ref, send_sems, recv_sems, loc_sem):
        x, y, c = lax.axis_index("x"), lax.axis_index("y"), lax.axis_index("c")
        me = 4 * x + 2 * y + c

        def src_for(p):
            return src_ref.at[p] if per_peer else src_ref

        def peer(k):
            px = 1 - x if k & 4 else x
            py = 1 - y if k & 2 else y
            pc = 1 - c if k & 1 else c
            return (px, py, pc), 4 * px + 2 * py + pc

        def copy(k):
            dev, p = peer(k)
            return pltpu.make_async_remote_copy(
                src_ref=src_for(p), dst_ref=dst_ref.at[me], send_sem=send_sems.at[k - 1], recv_sem=recv_sems.at[k - 1],
                device_id=dev, device_id_type=pl.DeviceIdType.MESH)

        def arrival(k):
            dev, p = peer(k)
            return pltpu.make_async_remote_copy(
                src_ref=src_for(p), dst_ref=dst_ref.at[p], send_sem=send_sems.at[k - 1], recv_sem=recv_sems.at[k - 1],
                device_id=dev, device_id_type=pl.DeviceIdType.MESH)

        mine = pltpu.make_async_copy(src_for(me), dst_ref.at[me], loc_sem)
        mine.start()
        sends = [copy(k) for k in range(1, _N_DEV)]
        for cp in sends:
            cp.start()
        for cp in sends:
            cp.wait_send()
        for k in range(1, _N_DEV):
            arrival(k).wait_recv()
        mine.wait()

    return pl.pallas_call(
        body, name=name, out_shape=jax.ShapeDtypeStruct((_N_DEV, R, C), src.dtype),
        in_specs=[pl.BlockSpec(memory_space=pl.ANY)], out_specs=pl.BlockSpec(memory_space=pl.ANY),
        scratch_shapes=[pltpu.SemaphoreType.DMA((_N_DEV - 1,)), pltpu.SemaphoreType.DMA((_N_DEV - 1,)), pltpu.SemaphoreType.DMA],
        compiler_params=pltpu.CompilerParams(has_side_effects=True),
    )(src)


def _adamw(recv, w, m, v, *, name):
    R, C = w.shape
    tr = _rows(R, 128)
    c1 = 1.0 / (1.0 - _ADAM_B1 ** _ADAM_STEP)
    c2 = 1.0 / (1.0 - _ADAM_B2 ** _ADAM_STEP)

    def body(r_ref, w_ref, m_ref, v_ref, g_ref, d_ref, mo_ref, vo_ref):
        g = r_ref[0].astype(_F32)
        for k in range(1, _N_DEV):
            g = g + r_ref[k].astype(_F32)
        m2 = _ADAM_B1 * m_ref[...] + (1.0 - _ADAM_B1) * g
        v2 = _ADAM_B2 * v_ref[...] + (1.0 - _ADAM_B2) * (g * g)
        g_ref[...] = g
        mo_ref[...] = m2
        vo_ref[...] = v2
        d_ref[...] = -_ADAM_LR * ((m2 * c1) / (jnp.sqrt(v2 * c2) + _ADAM_EPS) + _ADAM_WD * w_ref[...])

    blk = pl.BlockSpec((tr, C), lambda i: (i, 0))
    o = jax.ShapeDtypeStruct((R, C), _F32)
    return pl.pallas_call(
        body, name=name, out_shape=(o, o, o, o), grid=(R // tr,),
        in_specs=[pl.BlockSpec((_N_DEV, tr, C), lambda i: (0, i, 0)), blk, blk, blk], out_specs=(blk, blk, blk, blk),
        compiler_params=_params("parallel"),
    )(recv, w, m, v)


_BIG = (("w_in_e", 2), ("w_out_e", 1), ("conv_w_in", 2), ("conv_w_out", 1), ("xa_wq", 1), ("xa_wkv", 2), ("xa_wo", 1),
        ("ffn_w_gu", 2), ("ffn_w_down", 1))
_SMALL_SHARDED = (("mix_norm_o", 1), ("conv_b_in", 1), ("conv_dw_w", 2), ("conv_dw_b", 1), ("conv_ln_g", 1),
                  ("conv_ln_b", 1), ("conv_b_out", 1))
_REPLICATED = ("mix_norm_e", "fox_f_bias", "gmlp_ln_g", "gmlp_ln_b", "gmlp_w_s", "gmlp_b_s", "xa_norm", "mem_norm",
               "ffn_norm", "final_norm")
_WEIGHTS = ("mix_norm_e", "w_in_e", "fox_f_bias", "gmlp_ln_g", "gmlp_ln_b", "gmlp_w_s", "gmlp_b_s", "w_out_e", "mix_norm_o",
            "conv_w_in", "conv_b_in", "conv_dw_w", "conv_dw_b", "conv_ln_g", "conv_ln_b", "conv_w_out", "conv_b_out",
            "xa_norm", "mem_norm", "xa_wq", "xa_wkv", "xa_wo", "ffn_norm", "ffn_w_gu", "ffn_w_down", "final_norm")


def _padded(n, align):
    return -(-n // align) * align


def _flatten(parts, align, lead=(), rows_multiple=8):
    nl = len(lead)
    cols = []
    for a in parts:
        f = a.reshape(lead + (-1,))
        n = f.shape[-1]
        cols.append(jnp.pad(f, [(0, 0)] * nl + [(0, _padded(n, align) - n)]))
    flat = jnp.concatenate(cols, axis=-1)
    n = flat.shape[-1]
    flat = jnp.pad(flat, [(0, 0)] * nl + [(0, _padded(n, rows_multiple * _FLAT_W) - n)])
    return flat.reshape(lead + (flat.shape[-1] // _FLAT_W, _FLAT_W))


def _unflatten(flat, shapes, align, lead=()):
    f = flat.reshape(lead + (-1,))
    out, off = [], 0
    for s in shapes:
        n = math.prod(s)
        out.append(f[..., off:off + n].reshape(lead + tuple(s)))
        off += _padded(n, align)
    return out


def _split8(full, axis):
    return jnp.stack(jnp.split(full, _N_DEV, axis=axis), axis=0)


def _join8(parts, axis):
    return jnp.concatenate([parts[k] for k in range(_N_DEV)], axis=axis)


def _local_step(x, mem, tgt, P):
    B, T, D = x.shape
    M = mem.shape[1]
    N = B * T
    W = D // 2
    H = W // _FOX_HD
    f_blk = 5 * W // _LANES
    G = {}
    x0 = x.reshape(N, D)
    memf = mem.reshape(B * M, D)

    h_e = _rms_fwd(x0, P["mix_norm_e"][0], name="rms_mix_e")
    proj = _mm(h_e, P["w_in_pad"], name="mm_in_e", tn=384)
    proj3 = proj.reshape(B, T, -1)
    cum = _fox_cum(proj3, f_blk, P["fox_f_bias"][0], name="fox_cum")
    o_fox, lse = _fox_fwd(proj3, cum, name="fox_fwd")
    bias_full = jnp.repeat(P["gmlp_b_s"][0].T, _GRP, axis=1)
    a_out = _gmlp_fwd(proj, P["gmlp_ln_g"][0], P["gmlp_ln_b"][0], P["gmlp_w_s"][0], bias_full, name="gmlp_fwd")
    mixcat = jnp.concatenate([o_fox.reshape(N, W), a_out], axis=1)
    x1 = _mm(mixcat, P["w_out_e"], res=x0, name="mm_out_e")

    def xa_ffn_fwd(xin, l):
        s = {}
        s["h_xa"] = _rms_fwd(xin, P["xa_norm"][l], name=f"rms_xa{l}")
        s["q"] = _mm(s["h_xa"], P["xa_wq"][l], out_dtype=_MXU, name=f"mm_q{l}")
        s["mn"] = _rms_fwd(memf, P["mem_norm"][l], name=f"rms_mem{l}")
        s["kv"] = _mm(s["mn"], P["xa_wkv"][l], out_dtype=_MXU, name=f"mm_kv{l}")
        s["o"] = _xa_fwd(s["q"].reshape(B, T, D), s["kv"].reshape(B, M, 2 * D), name=f"xa_fwd{l}").reshape(N, D)
        s["x_mid"] = _mm(s["o"], P["xa_wo"][l], res=xin, name=f"mm_o{l}")
        s["h_ffn"] = _rms_fwd(s["x_mid"], P["ffn_norm"][l], name=f"rms_ffn{l}")
        s["gu"] = _mm(s["h_ffn"], P["ffn_w_gu"][l], name=f"mm_gu{l}")
        s["act"] = _swiglu_fwd(s["gu"], name=f"swiglu_fwd{l}")
        s["x_in"] = xin
        xout = _mm(s["act"], P["ffn_w_down"][l], res=s["x_mid"], name=f"mm_down{l}", tn=512)
        return xout, s

    x3, s0 = xa_ffn_fwd(x1, 0)
    h_o = _rms_fwd(x3, P["mix_norm_o"][0], name="rms_mix_o")
    ag = _mm(h_o, P["conv_w_in"], bias=P["conv_b_in"][0], name="mm_conv_in")
    C = ag.shape[1] // 2
    y = _glu_fwd(ag, name="glu_fwd")
    dw_w = jnp.pad(P["conv_dw_w"][0], ((0, _HALO - _CONV_K), (0, 0)))
    y2, y4 = _conv_fwd(y.reshape(B, T, C), dw_w, P["conv_dw_b"][0], P["conv_ln_g"][0], P["conv_ln_b"][0], name="conv_fwd")
    x4 = _mm(y4.reshape(N, C), P["conv_w_out"], bias=P["conv_b_out"][0], res=x3, name="mm_conv_out")
    x6, s1 = xa_ffn_fwd(x4, 1)
    loss, dx, dg = _final_loss(x6, P["final_norm"], tgt.reshape(N, D), name="final_loss")
    G["final_norm"] = dg.reshape(D)

    def xa_ffn_bwd(dx, s, l):
        g = {}
        dact = _mm(dx, P["ffn_w_down"][l], tb=True, name=f"mm_dact{l}", tn=256)
        g["ffn_w_down"] = _mm(s["act"], dx, ta=True, name=f"mm_dwdown{l}", tm=1408)
        dgu = _swiglu_bwd(s["gu"], dact, name=f"swiglu_bwd{l}")
        g["ffn_w_gu"] = _mm(s["h_ffn"], dgu, ta=True, name=f"mm_dwgu{l}")
        dh = _mm(dgu, P["ffn_w_gu"][l], tb=True, name=f"mm_dhffn{l}", tk=1408)
        dx, g["ffn_norm"] = _rms_bwd(s["x_mid"], P["ffn_norm"][l], dh, dx, name=f"rms_ffn_bwd{l}")
        do = _mm(dx, P["xa_wo"][l], tb=True, out_dtype=_MXU, name=f"mm_do{l}")
        g["xa_wo"] = _mm(s["o"], dx, ta=True, name=f"mm_dwo{l}")
        dq, dkv = _xa_bwd(s["q"].reshape(B, T, D), s["kv"].reshape(B, M, 2 * D), do.reshape(B, T, D), name=f"xa_bwd{l}")
        dq, dkv = dq.reshape(N, D), dkv.reshape(B * M, 2 * D)
        g["xa_wq"] = _mm(s["h_xa"], dq, ta=True, name=f"mm_dwq{l}")
        dh = _mm(dq, P["xa_wq"][l], tb=True, name=f"mm_dhxa{l}")
        g["xa_wkv"] = _mm(s["mn"], dkv, ta=True, name=f"mm_dwkv{l}")
        dmn = _mm(dkv, P["xa_wkv"][l], tb=True, name=f"mm_dmn{l}")
        g["mem_norm"] = _rms_bwd(memf, P["mem_norm"][l], dmn, None, name=f"rms_mem_bwd{l}")
        dx, g["xa_norm"] = _rms_bwd(s["x_in"], P["xa_norm"][l], dh, dx, name=f"rms_xa_bwd{l}")
        return dx, g

    dx, g1 = xa_ffn_bwd(dx, s1, 1)
    G["conv_b_out"] = _colsum(dx, name="colsum_b_out")
    dy4 = _mm(dx, P["conv_w_out"], tb=True, name="mm_dy4")
    G["conv_w_out"] = _mm(y4.reshape(N, C), dx, ta=True, name="mm_dwconv_out")[None]
    dy2, G["conv_ln_g"], G["conv_ln_b"], G["conv_dw_b"] = _conv_ln_bwd(
        y2.reshape(N, C), dy4, P["conv_ln_g"][0], P["conv_ln_b"][0], name="conv_ln_bwd")
    dag, ddw, G["conv_b_in"] = _conv_bwd(y.reshape(B, T, C), dy2.reshape(B, T, C), ag.reshape(B, T, 2 * C), dw_w, name="conv_bwd")
    G["conv_dw_w"] = ddw[None, :_CONV_K]
    dag = dag.reshape(N, 2 * C)
    G["conv_w_in"] = _mm(h_o, dag, ta=True, name="mm_dwconv_in")[None]
    dh = _mm(dag, P["conv_w_in"], tb=True, name="mm_dh_o")
    dx, G["mix_norm_o"] = _rms_bwd(x3, P["mix_norm_o"][0], dh, dx, name="rms_mix_o_bwd")
    dx, g0 = xa_ffn_bwd(dx, s0, 0)
    for k in g0:
        G[k] = jnp.stack([g0[k], g1[k]], axis=0) if g0[k].ndim == 2 and g0[k].shape[0] != 1 else jnp.concatenate([g0[k], g1[k]], axis=0)
    G["w_out_e"] = _mm(mixcat, dx, ta=True, name="mm_dwout_e")[None]
    dmix = _mm(dx, P["w_out_e"], tb=True, name="mm_dmix")
    dz, dlg, dlb, dws, dbias = _gmlp_bwd(proj, dmix, 1, P["gmlp_ln_g"][0], P["gmlp_ln_b"][0], P["gmlp_w_s"][0], bias_full,
                                         name="gmlp_bwd")
    G["gmlp_ln_g"], G["gmlp_ln_b"], G["gmlp_w_s"] = dlg, dlb, dws[None]
    G["gmlp_b_s"] = dbias[:, :2 * (W // _LANES)].T[None]
    dmix3 = dmix.reshape(B, T, D)
    dq, dk, dv, dcum = _fox_bwd(proj3, cum, dmix3, lse, name="fox_bwd")
    df, dfb = _fox_cum_bwd(proj3, f_blk, P["fox_f_bias"][0], dcum.reshape(B, H, T), name="fox_cum_bwd")
    G["fox_f_bias"] = dfb[:, 0].reshape(1, H)
    dproj = jnp.concatenate([dq.reshape(N, W), dk.reshape(N, W), dv.reshape(N, W), dz, df.reshape(N, _LANES).astype(_MXU)], axis=1)
    G["w_in_pad"] = _mm(h_e, dproj, ta=True, name="mm_dwin_e", tn=384)
    dh = _mm(dproj, P["w_in_pad"], tb=True, name="mm_dh_e", tk=384)
    dx, G["mix_norm_e"] = _rms_bwd(x0, P["mix_norm_e"][0], dh, dx, name="rms_mix_e_bwd")
    return loss, dx.reshape(B, T, D), G


def _pad_w_in(w_in, W, H):
    f = w_in[:, 3 * W:3 * W + H]
    return jnp.concatenate([w_in[:, :3 * W], w_in[:, 3 * W + H:], jnp.pad(f, ((0, 0), (0, _LANES - H)))], axis=1)


def _unpad_w_in(g, W, H):
    return jnp.concatenate([g[:, :3 * W], g[:, 5 * W:5 * W + H], g[:, 3 * W:5 * W]], axis=1)


def kernel(x, mem, mix_norm_e, w_in_e, fox_f_bias, gmlp_ln_g, gmlp_ln_b, gmlp_w_s, gmlp_b_s, w_out_e, mix_norm_o, conv_w_in, conv_b_in, conv_dw_w, conv_dw_b, conv_ln_g, conv_ln_b, conv_w_out, conv_b_out, xa_norm, mem_norm, xa_wq, xa_wkv, xa_wo, ffn_norm, ffn_w_gu, ffn_w_down, final_norm, loss_target, m_mix_norm_e, m_w_in_e, m_fox_f_bias, m_gmlp_ln_g, m_gmlp_ln_b, m_gmlp_w_s, m_gmlp_b_s, m_w_out_e, m_mix_norm_o, m_conv_w_in, m_conv_b_in, m_conv_dw_w, m_conv_dw_b, m_conv_ln_g, m_conv_ln_b, m_conv_w_out, m_conv_b_out, m_xa_norm, m_mem_norm, m_xa_wq, m_xa_wkv, m_xa_wo, m_ffn_norm, m_ffn_w_gu, m_ffn_w_down, m_final_norm, v_mix_norm_e, v_w_in_e, v_fox_f_bias, v_gmlp_ln_g, v_gmlp_ln_b, v_gmlp_w_s, v_gmlp_b_s, v_w_out_e, v_mix_norm_o, v_conv_w_in, v_conv_b_in, v_conv_dw_w, v_conv_dw_b, v_conv_ln_g, v_conv_ln_b, v_conv_w_out, v_conv_b_out, v_xa_norm, v_mem_norm, v_xa_wq, v_xa_wkv, v_xa_wo, v_ffn_norm, v_ffn_w_gu, v_ffn_w_down, v_final_norm):
    env = dict(locals())
    w = {n: env[n] for n in _WEIGHTS}
    mom = {n: env["m_" + n] for n in _WEIGHTS}
    var = {n: env["v_" + n] for n in _WEIGHTS}
    D = x.shape[-1]
    W = D // 2
    H = W // _FOX_HD

    big_names = [n for n, _ in _BIG]
    ss_names = [n for n, _ in _SMALL_SHARDED]
    big_shapes = [w[n].shape for n in big_names]
    ss_shapes = [w[n].shape for n in ss_names]
    rep_shapes = [w[n].shape for n in _REPLICATED]
    big_all = _exchange(_flatten([w[n] for n in big_names], _FLAT_ALIGN, rows_multiple=_BIG_ROWS).astype(_MXU),
                        per_peer=False, name="gather_big")
    ss_all = _exchange(_flatten([w[n] for n in ss_names], _FLAT_W), per_peer=False, name="gather_small")
    P = {n: w[n] for n in _REPLICATED}
    for (n, ax), parts in zip(_BIG, _unflatten(big_all, big_shapes, _FLAT_ALIGN, lead=(_N_DEV,))):
        P[n] = _join8(parts, ax)
    for (n, ax), parts in zip(_SMALL_SHARDED, _unflatten(ss_all, ss_shapes, _FLAT_W, lead=(_N_DEV,))):
        P[n] = _join8(parts, ax)
    P["w_in_pad"] = _pad_w_in(P.pop("w_in_e")[0], W, H)
    for n in ("w_out_e", "conv_w_in", "conv_w_out"):
        P[n] = P[n][0]

    loss, grad_x, G = _local_step(x, mem, loss_target, P)
    loss = lax.psum(loss[0, 0], ("x", "y", "c"))
    G["w_in_e"] = _unpad_w_in(G.pop("w_in_pad"), W, H)[None]

    big_src = _flatten([_split8(G[n], ax) for n, ax in _BIG], _FLAT_ALIGN, lead=(_N_DEV,), rows_multiple=_BIG_ROWS).astype(_MXU)
    ss_src = _flatten([_split8(G[n], ax) for n, ax in _SMALL_SHARDED], _FLAT_W, lead=(_N_DEV,))
    rep_src = _flatten([G[n].reshape(w[n].shape) for n in _REPLICATED], _FLAT_W)
    small_src = jnp.concatenate([ss_src, jnp.broadcast_to(rep_src[None], (_N_DEV,) + rep_src.shape)], axis=1)
    big_recv = _exchange(big_src, per_peer=True, name="scatter_big")
    small_recv = _exchange(small_src, per_peer=True, name="scatter_small")

    def flat_big(d):
        return _flatten([d[n] for n in big_names], _FLAT_ALIGN, rows_multiple=_BIG_ROWS)

    def flat_small(d):
        return jnp.concatenate([_flatten([d[n] for n in ss_names], _FLAT_W), _flatten([d[n] for n in _REPLICATED], _FLAT_W)], axis=0)

    outs_big = _adamw(big_recv, flat_big(w), flat_big(mom), flat_big(var), name="adamw_big")
    outs_small = _adamw(small_recv, flat_small(w), flat_small(mom), flat_small(var), name="adamw_small")
    n_ss_rows = ss_src.shape[1]
    res = []
    for ob, os_ in zip(outs_big, outs_small):
        d = dict(zip(big_names, _unflatten(ob, big_shapes, _FLAT_ALIGN)))
        d.update(zip(ss_names, _unflatten(os_[:n_ss_rows], ss_shapes, _FLAT_W)))
        d.update(zip(_REPLICATED, _unflatten(os_[n_ss_rows:], rep_shapes, _FLAT_W)))
        res.append(d)
    return (loss, grad_x, *[res[0][n] for n in _WEIGHTS], *[res[1][n] for n in _WEIGHTS],
            *[res[2][n] for n in _WEIGHTS], *[res[3][n] for n in _WEIGHTS])
```

```python
import functools
import math

import jax
import jax.numpy as jnp
from jax import lax
from jax.experimental import pallas as pl
from jax.experimental.pallas import tpu as pltpu

_F32 = jnp.float32
_MXU = jnp.bfloat16
_VMEM_LIMIT = 48 * 1024 * 1024
_LANES = 128
_EPS = 1e-6
_N_DEV = 8
_FOX_HD = 64
_FOX_SCALE = _FOX_HD ** -0.5
_CHUNK = 128
_GRP = 64
_CONV_K = 31
_HALO = 32
_XA_HEADS = 4
_GELU_C = math.sqrt(2.0 / math.pi)
_ADAM_LR, _ADAM_B1, _ADAM_B2, _ADAM_EPS, _ADAM_WD, _ADAM_STEP = 0.001, 0.9, 0.999, 1e-08, 0.01, 10
_FLAT_W = 1024
_FLAT_ALIGN = 16 * _FLAT_W
_BIG_ROWS = 128


def _params(*sem):
    return pltpu.CompilerParams(dimension_semantics=sem if sem else None, vmem_limit_bytes=_VMEM_LIMIT)


def _pick(n, pref):
    if n <= pref:
        return n
    best = None
    for t in range(_LANES, pref + 1, _LANES):
        if n % t == 0:
            best = t
    assert best is not None, (n, pref)
    return best


def _rows(n, pref):
    if n <= pref:
        return n
    t = pref
    while n % t:
        t //= 2
    assert t >= 8, (n, pref)
    return t


def _sigmoid(x):
    return 1.0 / (1.0 + jnp.exp(-x))


def _gelu(x):
    t = jnp.tanh(_GELU_C * (x + 0.044715 * (x * x * x)))
    return 0.5 * x * (1.0 + t)


def _gelu_grad(x):
    x2 = x * x
    t = jnp.tanh(_GELU_C * (x + 0.044715 * (x2 * x)))
    return 0.5 * (1.0 + t) + 0.5 * x * (1.0 - t * t) * (_GELU_C * (1.0 + 3.0 * 0.044715 * x2))


def _dot(a, b, ca, cb):
    return lax.dot_general(a, b, (((ca,), (cb,)), ((), ())), preferred_element_type=_F32)


def _mm(a, b, *, name, ta=False, tb=False, bl=None, bias=None, res=None, out_dtype=_F32, tm=1024, tn=512, tk=1024):
    if ta:
        K, M = a.shape
    else:
        M, K = a.shape
    if tb:
        N, K2 = b.shape[-2:]
    else:
        K2, N = b.shape[-2:]
    assert K == K2, (a.shape, b.shape, ta, tb)
    tm, tn = _pick(M, tm), _pick(N, tn)
    tk = K if (not ta and K <= 2816) else _pick(K, tk)
    nk = K // tk
    grid = (M // tm, N // tn, nk)
    a_spec = pl.BlockSpec((tk, tm), lambda i, j, k: (k, i)) if ta else pl.BlockSpec((tm, tk), lambda i, j, k: (i, k))
    if b.ndim == 3:
        b_spec = (pl.BlockSpec((None, tn, tk), lambda i, j, k: (bl, j, k)) if tb
                  else pl.BlockSpec((None, tk, tn), lambda i, j, k: (bl, k, j)))
    else:
        b_spec = pl.BlockSpec((tn, tk), lambda i, j, k: (j, k)) if tb else pl.BlockSpec((tk, tn), lambda i, j, k: (k, j))
    in_specs, args = [a_spec, b_spec], [a, b]
    if bias is not None:
        in_specs.append(pl.BlockSpec((1, tn), lambda i, j, k: (0, j)))
        args.append(bias.reshape(1, N).astype(_F32))
    if res is not None:
        in_specs.append(pl.BlockSpec((tm, tn), lambda i, j, k: (i, j)))
        args.append(res)
    has_bias, has_res = bias is not None, res is not None

    def body(*refs):
        a_ref, b_ref = refs[0], refs[1]
        pos = 2
        bias_ref = res_ref = None
        if has_bias:
            bias_ref = refs[pos]
            pos += 1
        if has_res:
            res_ref = refs[pos]
            pos += 1
        o_ref = refs[pos]
        acc_ref = refs[pos + 1] if nk > 1 else None
        p = _dot(a_ref[...].astype(_MXU), b_ref[...].astype(_MXU), 0 if ta else 1, 1 if tb else 0)

        def finish(acc):
            if has_bias:
                acc = acc + bias_ref[...]
            if has_res:
                acc = acc + res_ref[...]
            o_ref[...] = acc.astype(o_ref.dtype)

        if nk == 1:
            finish(p)
        else:
            k = pl.program_id(2)

            @pl.when(k == 0)
            def _():
                acc_ref[...] = p

            @pl.when(k > 0)
            def _():
                acc_ref[...] += p

            @pl.when(k == nk - 1)
            def _():
                finish(acc_ref[...])

    return pl.pallas_call(
        body,
        name=name,
        out_shape=jax.ShapeDtypeStruct((M, N), out_dtype),
        grid=grid,
        in_specs=in_specs,
        out_specs=pl.BlockSpec((tm, tn), lambda i, j, k: (i, j)),
        scratch_shapes=[pltpu.VMEM((tm, tn), _F32)] if nk > 1 else [],
        compiler_params=_params("parallel", "parallel", "arbitrary"),
    )(*args)


def _rms_fwd(x, g, *, name):
    N, D = x.shape
    tr = _rows(N, 512)

    def body(x_ref, g_ref, o_ref):
        xv = x_ref[...]
        r = lax.rsqrt(jnp.mean(xv * xv, axis=-1, keepdims=True) + _EPS)
        o_ref[...] = (xv * r * g_ref[...]).astype(o_ref.dtype)

    return pl.pallas_call(
        body,
        name=name,
        out_shape=jax.ShapeDtypeStruct((N, D), _MXU),
        grid=(N // tr,),
        in_specs=[pl.BlockSpec((tr, D), lambda i: (i, 0)), pl.BlockSpec((1, D), lambda i: (0, 0))],
        out_specs=pl.BlockSpec((tr, D), lambda i: (i, 0)),
        compiler_params=_params("parallel"),
    )(x, g.reshape(1, D))


def _rms_bwd(x, g, dh, dres, *, name):
    N, D = x.shape
    tr = _rows(N, 256)
    has_res = dres is not None

    def body(*refs):
        if has_res:
            x_ref, g_ref, dh_ref, dres_ref, dx_ref, dg_ref = refs
        else:
            x_ref, g_ref, dh_ref, dg_ref = refs
        xv = x_ref[...]
        r = lax.rsqrt(jnp.mean(xv * xv, axis=-1, keepdims=True) + _EPS)
        xh = xv * r
        dhv = dh_ref[...].astype(_F32)

        @pl.when(pl.program_id(0) == 0)
        def _():
            dg_ref[...] = jnp.zeros_like(dg_ref)

        dg_ref[...] += jnp.sum(dhv * xh, axis=0, keepdims=True)
        if has_res:
            dxn = dhv * g_ref[...]
            dx = r * (dxn - xh * jnp.mean(dxn * xh, axis=-1, keepdims=True))
            dx_ref[...] = dres_ref[...] + dx

    row = pl.BlockSpec((tr, D), lambda i: (i, 0))
    vec = pl.BlockSpec((1, D), lambda i: (0, 0))
    if has_res:
        out_shape = (jax.ShapeDtypeStruct((N, D), _F32), jax.ShapeDtypeStruct((1, D), _F32))
        out_specs = (row, vec)
        in_specs, args = [row, vec, row, row], (x, g.reshape(1, D), dh, dres)
    else:
        out_shape = jax.ShapeDtypeStruct((1, D), _F32)
        out_specs = vec
        in_specs, args = [row, vec, row], (x, g.reshape(1, D), dh)
    return pl.pallas_call(
        body, name=name, out_shape=out_shape, grid=(N // tr,), in_specs=in_specs, out_specs=out_specs,
        compiler_params=_params("arbitrary"),
    )(*args)


def _colsum(a, *, name):
    M, C = a.shape
    tr = _rows(M, 512)

    def body(a_ref, o_ref):
        @pl.when(pl.program_id(0) == 0)
        def _():
            o_ref[...] = jnp.zeros_like(o_ref)

        o_ref[...] += jnp.sum(a_ref[...].astype(_F32), axis=0, keepdims=True)

    return pl.pallas_call(
        body, name=name, out_shape=jax.ShapeDtypeStruct((1, C), _F32), grid=(M // tr,),
        in_specs=[pl.BlockSpec((tr, C), lambda i: (i, 0))], out_specs=pl.BlockSpec((1, C), lambda i: (0, 0)),
        compiler_params=_params("arbitrary"),
    )(a)


def _final_loss(x, g, tgt, *, name):
    N, D = x.shape
    tr = _rows(N, 256)

    def body(x_ref, g_ref, t_ref, loss_ref, dx_ref, dg_ref):
        xv = x_ref[...]
        r = lax.rsqrt(jnp.mean(xv * xv, axis=-1, keepdims=True) + _EPS)
        xh = xv * r
        gv = g_ref[...]
        diff = xh * gv - t_ref[...]

        @pl.when(pl.program_id(0) == 0)
        def _():
            loss_ref[...] = jnp.zeros_like(loss_ref)
            dg_ref[...] = jnp.zeros_like(dg_ref)

        part = jnp.sum(jnp.sum(diff * diff, axis=1, keepdims=True), axis=0, keepdims=True) * (0.5 / D)
        loss_ref[...] += jnp.broadcast_to(part, loss_ref.shape)
        dy = diff * (1.0 / D)
        dg_ref[...] += jnp.sum(dy * xh, axis=0, keepdims=True)
        dxn = dy * gv
        dx_ref[...] = r * (dxn - xh * jnp.mean(dxn * xh, axis=-1, keepdims=True))

    row = pl.BlockSpec((tr, D), lambda i: (i, 0))
    vec = pl.BlockSpec((1, D), lambda i: (0, 0))
    return pl.pallas_call(
        body, name=name,
        out_shape=(jax.ShapeDtypeStruct((8, _LANES), _F32), jax.ShapeDtypeStruct((N, D), _F32), jax.ShapeDtypeStruct((1, D), _F32)),
        grid=(N // tr,), in_specs=[row, vec, row],
        out_specs=(pl.BlockSpec((8, _LANES), lambda i: (0, 0)), row, vec),
        compiler_params=_params("arbitrary"),
    )(x, g.reshape(1, D), tgt)


def _swiglu_fwd(gu, *, name):
    N, H2 = gu.shape
    H = H2 // 2
    tr = _rows(N, 256)

    def body(g_ref, u_ref, o_ref):
        g = g_ref[...]
        o_ref[...] = (g * _sigmoid(g) * u_ref[...]).astype(o_ref.dtype)

    return pl.pallas_call(
        body, name=name, out_shape=jax.ShapeDtypeStruct((N, H), _MXU), grid=(N // tr,),
        in_specs=[pl.BlockSpec((tr, H), lambda i: (i, 0)), pl.BlockSpec((tr, H), lambda i: (i, 1))],
        out_specs=pl.BlockSpec((tr, H), lambda i: (i, 0)), compiler_params=_params("parallel"),
    )(gu, gu)


def _swiglu_bwd(gu, dact, *, name):
    N, H2 = gu.shape
    H = H2 // 2
    tr = _rows(N, 256)

    def body(g_ref, u_ref, d_ref, o_ref):
        g, u, d = g_ref[...], u_ref[...], d_ref[...]
        sg = _sigmoid(g)
        o_ref[:, :H] = (d * u * (sg * (1.0 + g * (1.0 - sg)))).astype(o_ref.dtype)
        o_ref[:, H:] = (d * (g * sg)).astype(o_ref.dtype)

    return pl.pallas_call(
        body, name=name, out_shape=jax.ShapeDtypeStruct((N, H2), _MXU), grid=(N // tr,),
        in_specs=[pl.BlockSpec((tr, H), lambda i: (i, 0)), pl.BlockSpec((tr, H), lambda i: (i, 1)),
                  pl.BlockSpec((tr, H), lambda i: (i, 0))],
        out_specs=pl.BlockSpec((tr, H2), lambda i: (i, 0)), compiler_params=_params("parallel"),
    )(gu, gu, dact)


def _gmlp_mix(vb, w, trans):
    tr, W = vb.shape
    lane = lax.broadcasted_iota(jnp.int32, (_CHUNK, _LANES), 1)
    rows = []
    for c in range(tr // _CHUNK):
        tiles = []
        for j in range(W // _LANES):
            t = vb[c * _CHUNK:(c + 1) * _CHUNK, j * _LANES:(j + 1) * _LANES]
            ma = _dot(w[2 * j], t, 0 if trans else 1, 0)
            mb = _dot(w[2 * j + 1], t, 0 if trans else 1, 0)
            tiles.append(jnp.where(lane < _GRP, ma, mb))
        rows.append(jnp.concatenate(tiles, axis=1))
    return jnp.concatenate(rows, axis=0)


def _tril_w(w_ref):
    r = lax.broadcasted_iota(jnp.int32, (_CHUNK, _CHUNK), 0)
    c = lax.broadcasted_iota(jnp.int32, (_CHUNK, _CHUNK), 1)
    return jnp.where((r >= c)[None], w_ref[...], 0.0).astype(_MXU)


def _layernorm_stats(v):
    mu = jnp.mean(v, axis=-1, keepdims=True)
    xc = v - mu
    rstd = lax.rsqrt(jnp.mean(xc * xc, axis=-1, keepdims=True) + _EPS)
    return xc * rstd, rstd


def _gmlp_fwd(proj, ln_g, ln_b, w_s, bias_full, *, name):
    N = proj.shape[0]
    W = ln_g.shape[-1]
    G = w_s.shape[0]
    tr = _rows(N, 512)
    ub, vb_ = 3, 4

    def body(u_ref, v_ref, g_ref, b_ref, w_ref, bias_ref, o_ref):
        u = _gelu(u_ref[...])
        xh, _ = _layernorm_stats(_gelu(v_ref[...]))
        vgn = xh * g_ref[...] + b_ref[...]
        mixed = _gmlp_mix(vgn.astype(_MXU), _tril_w(w_ref), False)
        bias = jnp.concatenate([bias_ref[...]] * (tr // _CHUNK), axis=0)
        o_ref[...] = (u * (mixed + bias)).astype(o_ref.dtype)

    vec = pl.BlockSpec((1, W), lambda i: (0, 0))
    return pl.pallas_call(
        body, name=name, out_shape=jax.ShapeDtypeStruct((N, W), _MXU), grid=(N // tr,),
        in_specs=[pl.BlockSpec((tr, W), lambda i: (i, ub)), pl.BlockSpec((tr, W), lambda i: (i, vb_)), vec, vec,
                  pl.BlockSpec((G, _CHUNK, _CHUNK), lambda i: (0, 0, 0)), pl.BlockSpec((_CHUNK, W), lambda i: (0, 0))],
        out_specs=pl.BlockSpec((tr, W), lambda i: (i, 0)), compiler_params=_params("parallel"),
    )(proj, proj, ln_g.reshape(1, W), ln_b.reshape(1, W), w_s, bias_full)


def _gmlp_bwd(proj, da_src, da_blk, ln_g, ln_b, w_s, bias_full, *, name):
    N = proj.shape[0]
    W = ln_g.shape[-1]
    G = w_s.shape[0]
    tr = _rows(N, 512)
    nch = tr // _CHUNK

    def body(u_ref, v_ref, da_ref, g_ref, b_ref, w_ref, bias_ref, dz_ref, dg_ref, db_ref, dw_ref, dbias_ref):
        @pl.when(pl.program_id(0) == 0)
        def _():
            dg_ref[...] = jnp.zeros_like(dg_ref)
            db_ref[...] = jnp.zeros_like(db_ref)
            dw_ref[...] = jnp.zeros_like(dw_ref)
            dbias_ref[...] = jnp.zeros_like(dbias_ref)

        u_pre, v_pre = u_ref[...], v_ref[...]
        ug = _gelu(u_pre)
        xh, rstd = _layernorm_stats(_gelu(v_pre))
        lg = g_ref[...]
        vgn = xh * lg + b_ref[...]
        vb = vgn.astype(_MXU)
        wt = _tril_w(w_ref)
        mixed = _gmlp_mix(vb, wt, False)
        bias = jnp.concatenate([bias_ref[...]] * nch, axis=0)
        da = da_ref[...].astype(_F32)
        du = da * (mixed + bias)
        dm = da * ug
        dmb = dm.astype(_MXU)
        lane = lax.broadcasted_iota(jnp.int32, (_CHUNK, _LANES), 1)
        r = lax.broadcasted_iota(jnp.int32, (_CHUNK, _CHUNK), 0)
        c = lax.broadcasted_iota(jnp.int32, (_CHUNK, _CHUNK), 1)
        tril = r >= c
        dmsum = dm[0:_CHUNK]
        for ch in range(1, nch):
            dmsum = dmsum + dm[ch * _CHUNK:(ch + 1) * _CHUNK]
        dbias = jnp.zeros((_CHUNK, _LANES), _F32)
        for j in range(W // _LANES):
            tile = dmsum[:, j * _LANES:(j + 1) * _LANES]
            sa = jnp.sum(jnp.where(lane < _GRP, tile, 0.0), axis=1, keepdims=True)
            sb = jnp.sum(jnp.where(lane >= _GRP, tile, 0.0), axis=1, keepdims=True)
            dbias = dbias + jnp.where(lane == 2 * j, sa, 0.0) + jnp.where(lane == 2 * j + 1, sb, 0.0)
            acc_a = jnp.zeros((_CHUNK, _CHUNK), _F32)
            acc_b = jnp.zeros((_CHUNK, _CHUNK), _F32)
            for ch in range(nch):
                dt = dmb[ch * _CHUNK:(ch + 1) * _CHUNK, j * _LANES:(j + 1) * _LANES]
                vt = vb[ch * _CHUNK:(ch + 1) * _CHUNK, j * _LANES:(j + 1) * _LANES]
                acc_a = acc_a + _dot(jnp.where(lane < _GRP, dt, jnp.zeros_like(dt)), vt, 1, 1)
                acc_b = acc_b + _dot(jnp.where(lane >= _GRP, dt, jnp.zeros_like(dt)), vt, 1, 1)
            dw_ref[2 * j] += jnp.where(tril, acc_a, 0.0)
            dw_ref[2 * j + 1] += jnp.where(tril, acc_b, 0.0)
        dbias_ref[...] += dbias
        dvgn = _gmlp_mix(dmb, wt, True)
        dg_ref[...] += jnp.sum(dvgn * xh, axis=0, keepdims=True)
        db_ref[...] += jnp.sum(dvgn, axis=0, keepdims=True)
        dxh = dvgn * lg
        dvg = rstd * (dxh - jnp.mean(dxh, axis=-1, keepdims=True) - xh * jnp.mean(dxh * xh, axis=-1, keepdims=True))
        dz_ref[:, :W] = (du * _gelu_grad(u_pre)).astype(dz_ref.dtype)
        dz_ref[:, W:] = (dvg * _gelu_grad(v_pre)).astype(dz_ref.dtype)

    vec = pl.BlockSpec((1, W), lambda i: (0, 0))
    wspec = pl.BlockSpec((G, _CHUNK, _CHUNK), lambda i: (0, 0, 0))
    return pl.pallas_call(
        body, name=name,
        out_shape=(jax.ShapeDtypeStruct((N, 2 * W), _MXU), jax.ShapeDtypeStruct((1, W), _F32), jax.ShapeDtypeStruct((1, W), _F32),
                   jax.ShapeDtypeStruct((G, _CHUNK, _CHUNK), _F32), jax.ShapeDtypeStruct((_CHUNK, _LANES), _F32)),
        grid=(N // tr,),
        in_specs=[pl.BlockSpec((tr, W), lambda i: (i, 3)), pl.BlockSpec((tr, W), lambda i: (i, 4)),
                  pl.BlockSpec((tr, W), lambda i: (i, da_blk)), vec, vec, wspec, pl.BlockSpec((_CHUNK, W), lambda i: (0, 0))],
        out_specs=(pl.BlockSpec((tr, 2 * W), lambda i: (i, 0)), vec, vec, wspec, pl.BlockSpec((_CHUNK, _LANES), lambda i: (0, 0))),
        compiler_params=_params("arbitrary"),
    )(proj, proj, da_src, ln_g.reshape(1, W), ln_b.reshape(1, W), w_s, bias_full)


def _lane_cumsum(v):
    T = v.shape[1]
    lane = lax.broadcasted_iota(jnp.int32, (8, _LANES), 1)
    carry = jnp.zeros((8, 1), _F32)
    out = []
    for ch in range(T // _LANES):
        blk = v[:, ch * _LANES:(ch + 1) * _LANES]
        sh = 1
        while sh < _LANES:
            blk = blk + jnp.where(lane >= sh, pltpu.roll(blk, sh, 1), 0.0)
            sh *= 2
        blk = blk + carry
        carry = blk[:, _LANES - 1:_LANES]
        out.append(blk)
    return jnp.concatenate(out, axis=1), carry


def _log_sigmoid(x):
    return jnp.minimum(x, 0.0) - jnp.log(1.0 + jnp.exp(-jnp.abs(x)))


def _fox_cum(proj3, f_blk, f_bias, *, name):
    B, T, _ = proj3.shape
    H = f_bias.shape[-1]
    assert H == 8

    def body(f_ref, b_ref, o_ref):
        x = f_ref[0].T[0:8, :] + b_ref[...]
        cum, _ = _lane_cumsum(_log_sigmoid(x))
        o_ref[0] = cum

    return pl.pallas_call(
        body, name=name, out_shape=jax.ShapeDtypeStruct((B, 8, T), _F32), grid=(B,),
        in_specs=[pl.BlockSpec((1, T, _LANES), lambda b: (b, 0, f_blk)), pl.BlockSpec((8, 1), lambda b: (0, 0))],
        out_specs=pl.BlockSpec((1, 8, T), lambda b: (b, 0, 0)), compiler_params=_params("parallel"),
    )(proj3, f_bias.reshape(8, 1))


def _fox_cum_bwd(proj3, f_blk, f_bias, dcum, *, name):
    B, T, _ = proj3.shape

    def body(f_ref, b_ref, dc_ref, df_ref, dbias_ref):
        @pl.when(pl.program_id(0) == 0)
        def _():
            dbias_ref[...] = jnp.zeros_like(dbias_ref)

        x = f_ref[0].T[0:8, :] + b_ref[...]
        dc = dc_ref[0]
        incl, total = _lane_cumsum(dc)
        dlf = total - incl + dc
        df = dlf * _sigmoid(-x)
        full = jnp.concatenate([df, jnp.zeros((_LANES - 8, T), _F32)], axis=0).T
        dbias_ref[...] += jnp.sum(full, axis=0, keepdims=True)
        df_ref[0] = full

    return pl.pallas_call(
        body, name=name,
        out_shape=(jax.ShapeDtypeStruct((B, T, _LANES), _F32), jax.ShapeDtypeStruct((1, _LANES), _F32)), grid=(B,),
        in_specs=[pl.BlockSpec((1, T, _LANES), lambda b: (b, 0, f_blk)), pl.BlockSpec((8, 1), lambda b: (0, 0)),
                  pl.BlockSpec((1, 8, T), lambda b: (b, 0, 0))],
        out_specs=(pl.BlockSpec((1, T, _LANES), lambda b: (b, 0, 0)), pl.BlockSpec((1, _LANES), lambda b: (0, 0))),
        compiler_params=_params("arbitrary"),
    )(proj3, f_bias.reshape(8, 1), dcum)


def _cum_row(cum_ref, h, start, size):
    blk = cum_ref[0, :, pl.ds(start, size)]
    sub = lax.broadcasted_iota(jnp.int32, (blk.shape[0], 1), 0)
    return jnp.sum(jnp.where(sub == h, blk, 0.0), axis=0, keepdims=True)


def _causal(tq, q0, k0):
    r = lax.broadcasted_iota(jnp.int32, (tq, tq), 0)
    c = lax.broadcasted_iota(jnp.int32, (tq, tq), 1)
    return (r + q0) >= (c + k0)


def _fox_fwd(proj3, cum, *, name):
    B, T, _ = proj3.shape
    H = cum.shape[1]
    W = H * _FOX_HD
    npair = W // _LANES
    tq = _rows(T, 256)
    nq = T // tq

    def body(q_ref, k_ref, v_ref, cum_ref, o_ref, lse_ref):
        p = pl.program_id(1)
        i = pl.program_id(2)
        q0 = pl.multiple_of(i * tq, tq)
        lane = lax.broadcasted_iota(jnp.int32, (1, _LANES), 1)
        q2 = q_ref[0] * _FOX_SCALE
        o2 = jnp.zeros((tq, _LANES), _F32)
        for hh in range(2):
            msk = (lane < _FOX_HD) if hh == 0 else (lane >= _FOX_HD)
            h = 2 * p + hh
            qm = jnp.where(msk, q2, 0.0).astype(_MXU)
            c0 = _cum_row(cum_ref, h, q0, _LANES)[:, 0:1]

            def step(jj, carry, masked, qm=qm, msk=msk, h=h, c0=c0):
                m_prev, l_prev, acc = carry
                k0 = pl.multiple_of(jj * tq, tq)
                k2 = k_ref[0, pl.ds(k0, tq), :].astype(_MXU)
                vm = jnp.where(msk, v_ref[0, pl.ds(k0, tq), :], 0.0).astype(_MXU)
                s = _dot(qm, k2, 1, 1) + (c0 - _cum_row(cum_ref, h, k0, tq))
                if masked:
                    s = jnp.where(_causal(tq, q0, k0), s, -jnp.inf)
                m_new = jnp.maximum(m_prev, jnp.max(s, axis=1, keepdims=True))
                alpha = jnp.exp(m_prev - m_new)
                e = jnp.exp(s - m_new)
                l_new = alpha * l_prev + jnp.sum(e, axis=1, keepdims=True)
                acc = alpha * acc + _dot(e.astype(_MXU), vm, 1, 0)
                return m_new, l_new, acc

            init = (jnp.full((tq, 1), -jnp.inf, _F32), jnp.zeros((tq, 1), _F32), jnp.zeros((tq, _LANES), _F32))
            carry = lax.fori_loop(0, i, functools.partial(step, masked=False), init)
            m, l, acc = step(i, carry, True)
            o2 = o2 + acc / l
            lse_ref[0, hh] = jnp.broadcast_to(m + jnp.log(l), (tq, _LANES))
        o_ref[0] = o2.astype(o_ref.dtype)

    return pl.pallas_call(
        body, name=name,
        out_shape=(jax.ShapeDtypeStruct((B, T, W), _MXU), jax.ShapeDtypeStruct((B, H, T, _LANES), _F32)),
        grid=(B, npair, nq),
        in_specs=[pl.BlockSpec((1, tq, _LANES), lambda b, p, i: (b, i, p)),
                  pl.BlockSpec((1, T, _LANES), lambda b, p, i: (b, 0, npair + p)),
                  pl.BlockSpec((1, T, _LANES), lambda b, p, i: (b, 0, 2 * npair + p)),
                  pl.BlockSpec((1, H, T), lambda b, p, i: (b, 0, 0))],
        out_specs=(pl.BlockSpec((1, tq, _LANES), lambda b, p, i: (b, i, p)),
                   pl.BlockSpec((1, 2, tq, _LANES), lambda b, p, i: (b, p, i, 0))),
        compiler_params=_params("parallel", "parallel", "parallel"),
    )(proj3, proj3, proj3, cum)


def _fox_bwd(proj3, cum, do3, lse, *, name):
    B, T, _ = proj3.shape
    H = cum.shape[1]
    W = H * _FOX_HD
    npair = W // _LANES
    tq = _rows(T, 256)
    nq = T // tq

    def body(q_ref, k_ref, v_ref, cum_ref, do_ref, lse_ref, dq_ref, dk_ref, dv_ref, dc_ref, p_scr, dp_scr, dk_acc, dv_acc, dc_acc):
        p = pl.program_id(1)
        i = pl.program_id(2)
        q0 = pl.multiple_of(i * tq, tq)
        lane = lax.broadcasted_iota(jnp.int32, (1, _LANES), 1)

        @pl.when(i == 0)
        def _():
            dk_acc[...] = jnp.zeros_like(dk_acc)
            dv_acc[...] = jnp.zeros_like(dv_acc)
            dc_acc[...] = jnp.zeros_like(dc_acc)

        q2 = q_ref[0] * _FOX_SCALE
        do2 = do_ref[0].astype(_F32)
        dq2 = jnp.zeros((tq, _LANES), _F32)
        for hh in range(2):
            msk = (lane < _FOX_HD) if hh == 0 else (lane >= _FOX_HD)
            h = 2 * p + hh
            qm = jnp.where(msk, q2, 0.0).astype(_MXU)
            dom = jnp.where(msk, do2, 0.0).astype(_MXU)
            c0 = _cum_row(cum_ref, h, q0, _LANES)[:, 0:1]
            lse_h = lse_ref[0, hh][:, 0:1]

            def first(jj, delta, masked, h=h, qm=qm, dom=dom, c0=c0, lse_h=lse_h):
                k0 = pl.multiple_of(jj * tq, tq)
                kb = k_ref[0, pl.ds(k0, tq), :].astype(_MXU)
                vb = v_ref[0, pl.ds(k0, tq), :].astype(_MXU)
                s = _dot(qm, kb, 1, 1) + (c0 - _cum_row(cum_ref, h, k0, tq))
                pr = jnp.exp(s - lse_h)
                if masked:
                    pr = jnp.where(_causal(tq, q0, k0), pr, 0.0)
                dp = _dot(dom, vb, 1, 1)
                p_scr[jj] = pr
                dp_scr[jj] = dp
                return delta + jnp.sum(pr * dp, axis=1, keepdims=True)

            delta = lax.fori_loop(0, i, functools.partial(first, masked=False), jnp.zeros((tq, 1), _F32))
            delta = first(i, delta, True)

            def second(jj, dq, msk=msk, qm=qm, dom=dom, delta=delta, hh=hh):
                k0 = pl.multiple_of(jj * tq, tq)
                km = jnp.where(msk, k_ref[0, pl.ds(k0, tq), :], 0.0).astype(_MXU)
                pr = p_scr[jj]
                ds = pr * (dp_scr[jj] - delta)
                dsb = ds.astype(_MXU)
                dv_acc[pl.ds(k0, tq), :] += _dot(pr.astype(_MXU), dom, 0, 0)
                dk_acc[pl.ds(k0, tq), :] += _dot(dsb, qm, 0, 0)
                dc_acc[hh:hh + 1, pl.ds(k0, tq)] += jnp.sum(ds, axis=0, keepdims=True)
                return dq + _dot(dsb, km, 1, 0)

            dq2 = dq2 + lax.fori_loop(0, i + 1, second, jnp.zeros((tq, _LANES), _F32))
        dq_ref[0] = (dq2 * _FOX_SCALE).astype(dq_ref.dtype)

        @pl.when(i == nq - 1)
        def _():
            dk_ref[0] = dk_acc[...].astype(dk_ref.dtype)
            dv_ref[0] = dv_acc[...].astype(dv_ref.dtype)
            dc_ref[0, 0] = -dc_acc[...]

    full = lambda blk: pl.BlockSpec((1, T, _LANES), lambda b, p, i, blk=blk: (b, 0, blk * npair + p))
    part = lambda blk: pl.BlockSpec((1, tq, _LANES), lambda b, p, i, blk=blk: (b, i, blk * npair + p))
    dq, dk, dv, dcum = pl.pallas_call(
        body, name=name,
        out_shape=(jax.ShapeDtypeStruct((B, T, W), _MXU), jax.ShapeDtypeStruct((B, T, W), _MXU),
                   jax.ShapeDtypeStruct((B, T, W), _MXU), jax.ShapeDtypeStruct((B, npair, 2, T), _F32)),
        grid=(B, npair, nq),
        in_specs=[part(0), full(1), full(2), pl.BlockSpec((1, H, T), lambda b, p, i: (b, 0, 0)), part(0),
                  pl.BlockSpec((1, 2, tq, _LANES), lambda b, p, i: (b, p, i, 0))],
        out_specs=(part(0), full(0), full(0), pl.BlockSpec((1, 1, 2, T), lambda b, p, i: (b, p, 0, 0))),
        scratch_shapes=[pltpu.VMEM((nq, tq, tq), _F32), pltpu.VMEM((nq, tq, tq), _F32), pltpu.VMEM((T, _LANES), _F32),
                        pltpu.VMEM((T, _LANES), _F32), pltpu.VMEM((2, T), _F32)],
        compiler_params=_params("parallel", "parallel", "arbitrary"),
    )(proj3, proj3, proj3, cum, do3, lse)
    return dq, dk, dv, dcum


def _xa_probs(qh, kh, scale):
    s = _dot(qh, kh, 1, 1) * scale
    e = jnp.exp(s - jnp.max(s, axis=1, keepdims=True))
    return e / jnp.sum(e, axis=1, keepdims=True)


def _xa_fwd(q3, kv3, *, name):
    B, T, D = q3.shape
    M = kv3.shape[1]
    hd = D // _XA_HEADS
    scale = hd ** -0.5
    tq = _rows(T, 512)

    def body(q_ref, kv_ref, o_ref):
        for h in range(_XA_HEADS):
            sl = slice(h * hd, (h + 1) * hd)
            p = _xa_probs(q_ref[0, :, sl], kv_ref[0, :, sl], scale)
            o_ref[0, :, sl] = _dot(p.astype(_MXU), kv_ref[0, :, D + h * hd:D + (h + 1) * hd], 1, 0).astype(o_ref.dtype)

    return pl.pallas_call(
        body, name=name, out_shape=jax.ShapeDtypeStruct((B, T, D), _MXU), grid=(B, T // tq),
        in_specs=[pl.BlockSpec((1, tq, D), lambda b, i: (b, i, 0)), pl.BlockSpec((1, M, 2 * D), lambda b, i: (b, 0, 0))],
        out_specs=pl.BlockSpec((1, tq, D), lambda b, i: (b, i, 0)), compiler_params=_params("parallel", "parallel"),
    )(q3, kv3)


def _xa_bwd(q3, kv3, do3, *, name):
    B, T, D = q3.shape
    M = kv3.shape[1]
    hd = D // _XA_HEADS
    scale = hd ** -0.5
    tq = _rows(T, 512)

    def body(q_ref, kv_ref, do_ref, dq_ref, dkv_ref):
        @pl.when(pl.program_id(1) == 0)
        def _():
            dkv_ref[...] = jnp.zeros_like(dkv_ref)

        for h in range(_XA_HEADS):
            sl = slice(h * hd, (h + 1) * hd)
            slv = slice(D + h * hd, D + (h + 1) * hd)
            qh, kh, vh, doh = q_ref[0, :, sl], kv_ref[0, :, sl], kv_ref[0, :, slv], do_ref[0, :, sl]
            p = _xa_probs(qh, kh, scale)
            dkv_ref[0, :, slv] += _dot(p.astype(_MXU), doh, 0, 0)
            dp = _dot(doh, vh, 1, 1)
            ds = (p * (dp - jnp.sum(p * dp, axis=1, keepdims=True))).astype(_MXU)
            dq_ref[0, :, sl] = (_dot(ds, kh, 1, 0) * scale).astype(dq_ref.dtype)
            dkv_ref[0, :, sl] += _dot(ds, qh, 0, 0) * scale

    blk = pl.BlockSpec((1, tq, D), lambda b, i: (b, i, 0))
    kvs = pl.BlockSpec((1, M, 2 * D), lambda b, i: (b, 0, 0))
    return pl.pallas_call(
        body, name=name,
        out_shape=(jax.ShapeDtypeStruct((B, T, D), _MXU), jax.ShapeDtypeStruct((B, M, 2 * D), _F32)),
        grid=(B, T // tq), in_specs=[blk, kvs, blk], out_specs=(blk, kvs),
        compiler_params=_params("parallel", "arbitrary"),
    )(q3, kv3, do3)


def _glu_fwd(ag, *, name):
    N, C2 = ag.shape
    C = C2 // 2
    tr = _rows(N, 512)

    def body(a_ref, g_ref, o_ref):
        o_ref[...] = a_ref[...] * _sigmoid(g_ref[...])

    return pl.pallas_call(
        body, name=name, out_shape=jax.ShapeDtypeStruct((N, C), _F32), grid=(N // tr,),
        in_specs=[pl.BlockSpec((tr, C), lambda i: (i, 0)), pl.BlockSpec((tr, C), lambda i: (i, 1))],
        out_specs=pl.BlockSpec((tr, C), lambda i: (i, 0)), compiler_params=_params("parallel"),
    )(ag, ag)


def _conv_fwd(y3, dw_w, dw_b, ln_g, ln_b, *, name):
    B, T, C = y3.shape
    tt = _rows(T, 256)
    nt = T // tt

    def body(prev_ref, cur_ref, w_ref, b_ref, g_ref, lb_ref, y2_ref, y4_ref, ext):
        i = pl.program_id(1)
        ext[0:_HALO, :] = jnp.where(i > 0, prev_ref[0, tt - _HALO:tt, :], 0.0)
        ext[_HALO:_HALO + tt, :] = cur_ref[0]
        acc = jnp.broadcast_to(b_ref[...], (tt, C))
        for j in range(_CONV_K):
            off = _HALO - (_CONV_K - 1) + j
            acc = acc + w_ref[j:j + 1, :] * ext[off:off + tt, :]
        y2_ref[0] = acc
        xh, _ = _layernorm_stats(acc)
        z = xh * g_ref[...] + lb_ref[...]
        y4_ref[0] = (z * _sigmoid(z)).astype(y4_ref.dtype)

    vec = pl.BlockSpec((1, C), lambda b, i: (0, 0))
    blk = pl.BlockSpec((1, tt, C), lambda b, i: (b, i, 0))
    return pl.pallas_call(
        body, name=name,
        out_shape=(jax.ShapeDtypeStruct((B, T, C), _F32), jax.ShapeDtypeStruct((B, T, C), _MXU)),
        grid=(B, nt),
        in_specs=[pl.BlockSpec((1, tt, C), lambda b, i: (b, jnp.maximum(i - 1, 0), 0)), blk,
                  pl.BlockSpec((_HALO, C), lambda b, i: (0, 0)), vec, vec, vec],
        out_specs=(blk, blk), scratch_shapes=[pltpu.VMEM((tt + _HALO, C), _F32)],
        compiler_params=_params("parallel", "parallel"),
    )(y3, y3, dw_w, dw_b.reshape(1, C), ln_g.reshape(1, C), ln_b.reshape(1, C))


def _conv_ln_bwd(y2, dy4, ln_g, ln_b, *, name):
    N, C = y2.shape
    tr = _rows(N, 256)

    def body(y_ref, d_ref, g_ref, b_ref, dy_ref, dg_ref, db_ref, dwb_ref):
        @pl.when(pl.program_id(0) == 0)
        def _():
            dg_ref[...] = jnp.zeros_like(dg_ref)
            db_ref[...] = jnp.zeros_like(db_ref)
            dwb_ref[...] = jnp.zeros_like(dwb_ref)

        xh, rstd = _layernorm_stats(y_ref[...])
        gv = g_ref[...]
        z = xh * gv + b_ref[...]
        sg = _sigmoid(z)
        dz = d_ref[...] * (sg * (1.0 + z * (1.0 - sg)))
        dg_ref[...] += jnp.sum(dz * xh, axis=0, keepdims=True)
        db_ref[...] += jnp.sum(dz, axis=0, keepdims=True)
        dxh = dz * gv
        dy = rstd * (dxh - jnp.mean(dxh, axis=-1, keepdims=True) - xh * jnp.mean(dxh * xh, axis=-1, keepdims=True))
        dwb_ref[...] += jnp.sum(dy, axis=0, keepdims=True)
        dy_ref[...] = dy

    row = pl.BlockSpec((tr, C), lambda i: (i, 0))
    vec = pl.BlockSpec((1, C), lambda i: (0, 0))
    v = jax.ShapeDtypeStruct((1, C), _F32)
    return pl.pallas_call(
        body, name=name, out_shape=(jax.ShapeDtypeStruct((N, C), _F32), v, v, v), grid=(N // tr,),
        in_specs=[row, row, vec, vec], out_specs=(row, vec, vec, vec), compiler_params=_params("arbitrary"),
    )(y2, dy4, ln_g.reshape(1, C), ln_b.reshape(1, C))


def _conv_bwd(y3, dy23, ag3, dw_w, *, name):
    B, T, C = y3.shape
    tt = _rows(T, 256)
    nt = T // tt

    def body(yp_ref, yc_ref, dc_ref, dn_ref, a_ref, g_ref, w_ref, dag_ref, dw_ref, dbin_ref, yext, dext):
        b = pl.program_id(0)
        i = pl.program_id(1)

        @pl.when((b == 0) & (i == 0))
        def _():
            dw_ref[...] = jnp.zeros_like(dw_ref)
            dbin_ref[...] = jnp.zeros_like(dbin_ref)

        yext[0:_HALO, :] = jnp.where(i > 0, yp_ref[0, tt - _HALO:tt, :], 0.0)
        yext[_HALO:_HALO + tt, :] = yc_ref[0]
        d_cur = dc_ref[0]
        dext[0:tt, :] = d_cur
        dext[tt:tt + _HALO, :] = jnp.where(i < nt - 1, dn_ref[0, 0:_HALO, :], 0.0)
        dy = jnp.zeros((tt, C), _F32)
        for j in range(_CONV_K):
            sh = _CONV_K - 1 - j
            dy = dy + w_ref[j:j + 1, :] * dext[sh:sh + tt, :]
            off = _HALO - sh
            dw_ref[j:j + 1, :] += jnp.sum(d_cur * yext[off:off + tt, :], axis=0, keepdims=True)
        a, g = a_ref[0], g_ref[0]
        sg = _sigmoid(g)
        da = dy * sg
        dg = dy * a * (sg * (1.0 - sg))
        dag_ref[0, :, :C] = da.astype(dag_ref.dtype)
        dag_ref[0, :, C:] = dg.astype(dag_ref.dtype)
        dbin_ref[:, :C] += jnp.sum(da, axis=0, keepdims=True)
        dbin_ref[:, C:] += jnp.sum(dg, axis=0, keepdims=True)

    blk = pl.BlockSpec((1, tt, C), lambda b, i: (b, i, 0))
    return pl.pallas_call(
        body, name=name,
        out_shape=(jax.ShapeDtypeStruct((B, T, 2 * C), _MXU), jax.ShapeDtypeStruct((_HALO, C), _F32),
                   jax.ShapeDtypeStruct((1, 2 * C), _F32)),
        grid=(B, nt),
        in_specs=[pl.BlockSpec((1, tt, C), lambda b, i: (b, jnp.maximum(i - 1, 0), 0)), blk, blk,
                  pl.BlockSpec((1, tt, C), lambda b, i: (b, jnp.minimum(i + 1, nt - 1), 0)),
                  blk, pl.BlockSpec((1, tt, C), lambda b, i: (b, i, 1)), pl.BlockSpec((_HALO, C), lambda b, i: (0, 0))],
        out_specs=(pl.BlockSpec((1, tt, 2 * C), lambda b, i: (b, i, 0)), pl.BlockSpec((_HALO, C), lambda b, i: (0, 0)),
                   pl.BlockSpec((1, 2 * C), lambda b, i: (0, 0))),
        scratch_shapes=[pltpu.VMEM((tt + _HALO, C), _F32), pltpu.VMEM((tt + _HALO, C), _F32)],
        compiler_params=_params("arbitrary", "arbitrary"),
    )(y3, y3, dy23, dy23, ag3, ag3, dw_w)


def _exchange(items, *, name):
    flat_srcs, out_shapes, pieces = [], [], []
    for t, (srcs, per_peer) in enumerate(items):
        blk = srcs[0].shape[1:] if per_peer else srcs[0].shape
        out_shapes.append(jax.ShapeDtypeStruct((len(srcs), _N_DEV) + tuple(blk), srcs[0].dtype))
        for l, s in enumerate(srcs):
            pieces.append((t, l, len(flat_srcs)))
            flat_srcs.append(s)
    n_src, n_dst, n_pc = len(flat_srcs), len(items), len(pieces)

    def body(*refs):
        src_refs, dst_refs = refs[:n_src], refs[n_src:n_src + n_dst]
        send_sems, recv_sems, loc_sems = refs[n_src + n_dst:]
        x, y, c = lax.axis_index("x"), lax.axis_index("y"), lax.axis_index("c")
        me = 4 * x + 2 * y + c

        def peer(k):
            px = 1 - x if k & 4 else x
            py = 1 - y if k & 2 else y
            pc = 1 - c if k & 1 else c
            return (px, py, pc), 4 * px + 2 * py + pc

        def src_for(i, p):
            t, _, s = pieces[i]
            return src_refs[s].at[p] if items[t][1] else src_refs[s]

        def remote(i, k, slot):
            t, l, _ = pieces[i]
            dev, p = peer(k)
            return pltpu.make_async_remote_copy(
                src_ref=src_for(i, p), dst_ref=dst_refs[t].at[l, me if slot is None else slot],
                send_sem=send_sems.at[i, k - 1], recv_sem=recv_sems.at[i, k - 1], device_id=dev, device_id_type=pl.DeviceIdType.MESH)

        local = []
        for i, (t, l, _) in enumerate(pieces):
            cp = pltpu.make_async_copy(src_for(i, me), dst_refs[t].at[l, me], loc_sems.at[i])
            cp.start()
            local.append(cp)
        sends = [remote(i, k, None) for i in range(n_pc) for k in range(1, _N_DEV)]
        for cp in sends:
            cp.start()
        for cp in sends:
            cp.wait_send()
        for i in range(n_pc):
            for k in range(1, _N_DEV):
                remote(i, k, peer(k)[1]).wait_recv()
        for cp in local:
            cp.wait()

    return pl.pallas_call(
        body, name=name, out_shape=out_shapes,
        in_specs=[pl.BlockSpec(memory_space=pl.ANY)] * n_src, out_specs=[pl.BlockSpec(memory_space=pl.ANY)] * n_dst,
        scratch_shapes=[pltpu.SemaphoreType.DMA((n_pc, _N_DEV - 1)), pltpu.SemaphoreType.DMA((n_pc, _N_DEV - 1)),
                        pltpu.SemaphoreType.DMA((n_pc,))],
        compiler_params=pltpu.CompilerParams(has_side_effects=True),
    )(*flat_srcs)


def _adam_update(g, w, m, v):
    c1 = 1.0 / (1.0 - _ADAM_B1 ** _ADAM_STEP)
    c2 = 1.0 / (1.0 - _ADAM_B2 ** _ADAM_STEP)
    m2 = _ADAM_B1 * m + (1.0 - _ADAM_B1) * g
    v2 = _ADAM_B2 * v + (1.0 - _ADAM_B2) * (g * g)
    return -_ADAM_LR * ((m2 * c1) / (jnp.sqrt(v2 * c2) + _ADAM_EPS) + _ADAM_WD * w), m2, v2


def _adamw_big(recv, w, m, v, *, name):
    L, R, C = w.shape
    tr = _rows(R, 256)

    def body(r_ref, w_ref, m_ref, v_ref, g_ref, d_ref, mo_ref, vo_ref):
        g = r_ref[0, 0].astype(_F32)
        for k in range(1, _N_DEV):
            g = g + r_ref[0, k].astype(_F32)
        g_ref[0] = g
        d_ref[0], mo_ref[0], vo_ref[0] = _adam_update(g, w_ref[0], m_ref[0], v_ref[0])

    blk = pl.BlockSpec((1, tr, C), lambda l, i: (l, i, 0))
    o = jax.ShapeDtypeStruct((L, R, C), _F32)
    return pl.pallas_call(
        body, name=name, out_shape=(o, o, o, o), grid=(L, R // tr),
        in_specs=[pl.BlockSpec((1, _N_DEV, tr, C), lambda l, i: (l, 0, i, 0)), blk, blk, blk], out_specs=(blk, blk, blk, blk),
        compiler_params=_params("parallel", "parallel"),
    )(recv, w, m, v)


def _adamw_small(tensors, *, name):
    n = len(tensors)
    lanes = [t[4] for t in tensors]

    def body(*refs):
        ins, outs = refs[:4 * n], refs[4 * n:]
        for t in range(n):
            r_ref, w_ref, m_ref, v_ref = ins[4 * t:4 * t + 4]
            g_ref, d_ref, mo_ref, vo_ref = outs[4 * t:4 * t + 4]
            for l in range(w_ref.shape[0]):
                g = r_ref[l, 0]
                for k in range(1, _N_DEV):
                    g = g + r_ref[l, k]
                if lanes[t] is not None:
                    g = g[..., :lanes[t]]
                g_ref[l] = g
                d_ref[l], mo_ref[l], vo_ref[l] = _adam_update(g, w_ref[l], m_ref[l], v_ref[l])

    args, out_shape = [], []
    for recv, w, m, v, _ in tensors:
        args += [recv, w, m, v]
        out_shape += [jax.ShapeDtypeStruct(w.shape, _F32)] * 4
    outs = pl.pallas_call(
        body, name=name, out_shape=out_shape,
        in_specs=[pl.BlockSpec(memory_space=pltpu.VMEM)] * len(args), out_specs=[pl.BlockSpec(memory_space=pltpu.VMEM)] * len(out_shape),
        compiler_params=_params(),
    )(*args)
    return [tuple(outs[4 * t:4 * t + 4]) for t in range(n)]


_BIG = (("w_in_e", 2), ("w_out_e", 1), ("conv_w_in", 2), ("conv_w_out", 1), ("xa_wq", 1), ("xa_wkv", 2), ("xa_wo", 1),
        ("ffn_w_gu", 2), ("ffn_w_down", 1))
_SMALL_SHARDED = (("mix_norm_o", 1), ("conv_b_in", 1), ("conv_dw_w", 2), ("conv_dw_b", 1), ("conv_ln_g", 1),
                  ("conv_ln_b", 1), ("conv_b_out", 1))
_REPLICATED = ("mix_norm_e", "fox_f_bias", "gmlp_ln_g", "gmlp_ln_b", "gmlp_w_s", "gmlp_b_s", "xa_norm", "mem_norm",
               "ffn_norm", "final_norm")
_WEIGHTS = ("mix_norm_e", "w_in_e", "fox_f_bias", "gmlp_ln_g", "gmlp_ln_b", "gmlp_w_s", "gmlp_b_s", "w_out_e", "mix_norm_o",
            "conv_w_in", "conv_b_in", "conv_dw_w", "conv_dw_b", "conv_ln_g", "conv_ln_b", "conv_w_out", "conv_b_out",
            "xa_norm", "mem_norm", "xa_wq", "xa_wkv", "xa_wo", "ffn_norm", "ffn_w_gu", "ffn_w_down", "final_norm")


def _cols_to_peers(g):
    K, N = g.shape[-2:]
    return jnp.swapaxes(g.reshape(g.shape[:-1] + (_N_DEV, N // _N_DEV)), -3, -2)


def _peers_to_cols(d):
    K, c = d.shape[-2:]
    return jnp.swapaxes(d, -3, -2).reshape(d.shape[:-3] + (K, _N_DEV * c))


def _local_step(x, mem, tgt, P):
    B, T, D = x.shape
    M = mem.shape[1]
    N = B * T
    W = D // 2
    H = W // _FOX_HD
    f_blk = 5 * W // _LANES
    G = {}
    x0 = x.reshape(N, D)
    memf = mem.reshape(B * M, D)

    h_e = _rms_fwd(x0, P["mix_norm_e"][0], name="rms_mix_e")
    proj = _mm(h_e, P["w_in_pad"], bl=0, name="mm_in_e", tn=384)
    proj3 = proj.reshape(B, T, -1)
    cum = _fox_cum(proj3, f_blk, P["fox_f_bias"][0], name="fox_cum")
    o_fox, lse = _fox_fwd(proj3, cum, name="fox_fwd")
    bias_full = jnp.repeat(P["gmlp_b_s"][0].T, _GRP, axis=1)
    a_out = _gmlp_fwd(proj, P["gmlp_ln_g"][0], P["gmlp_ln_b"][0], P["gmlp_w_s"][0], bias_full, name="gmlp_fwd")
    mixcat = jnp.concatenate([o_fox.reshape(N, W), a_out], axis=1)
    x1 = _mm(mixcat, P["w_out_e"], bl=0, res=x0, name="mm_out_e")

    def xa_ffn_fwd(xin, l):
        s = {}
        s["h_xa"] = _rms_fwd(xin, P["xa_norm"][l], name=f"rms_xa{l}")
        s["q"] = _mm(s["h_xa"], P["xa_wq"], bl=l, out_dtype=_MXU, name=f"mm_q{l}")
        s["mn"] = _rms_fwd(memf, P["mem_norm"][l], name=f"rms_mem{l}")
        s["kv"] = _mm(s["mn"], P["xa_wkv"], bl=l, out_dtype=_MXU, name=f"mm_kv{l}")
        s["o"] = _xa_fwd(s["q"].reshape(B, T, D), s["kv"].reshape(B, M, 2 * D), name=f"xa_fwd{l}").reshape(N, D)
        s["x_mid"] = _mm(s["o"], P["xa_wo"], bl=l, res=xin, name=f"mm_o{l}")
        s["h_ffn"] = _rms_fwd(s["x_mid"], P["ffn_norm"][l], name=f"rms_ffn{l}")
        s["gu"] = _mm(s["h_ffn"], P["ffn_w_gu"], bl=l, name=f"mm_gu{l}")
        s["act"] = _swiglu_fwd(s["gu"], name=f"swiglu_fwd{l}")
        s["x_in"] = xin
        xout = _mm(s["act"], P["ffn_w_down"], bl=l, res=s["x_mid"], name=f"mm_down{l}", tn=512)
        return xout, s

    x3, s0 = xa_ffn_fwd(x1, 0)
    h_o = _rms_fwd(x3, P["mix_norm_o"][0], name="rms_mix_o")
    ag = _mm(h_o, P["conv_w_in"], bl=0, bias=P["conv_b_in"][0], name="mm_conv_in")
    C = ag.shape[1] // 2
    y = _glu_fwd(ag, name="glu_fwd")
    dw_w = jnp.pad(P["conv_dw_w"][0], ((0, _HALO - _CONV_K), (0, 0)))
    y2, y4 = _conv_fwd(y.reshape(B, T, C), dw_w, P["conv_dw_b"][0], P["conv_ln_g"][0], P["conv_ln_b"][0], name="conv_fwd")
    x4 = _mm(y4.reshape(N, C), P["conv_w_out"], bl=0, bias=P["conv_b_out"][0], res=x3, name="mm_conv_out")
    x6, s1 = xa_ffn_fwd(x4, 1)
    loss, dx, dg = _final_loss(x6, P["final_norm"], tgt.reshape(N, D), name="final_loss")
    G["final_norm"] = [dg]

    def xa_ffn_bwd(dx, s, l):
        g = {}
        dact = _mm(dx, P["ffn_w_down"], bl=l, tb=True, name=f"mm_dact{l}", tn=256)
        g["ffn_w_down"] = _mm(s["act"], dx, ta=True, out_dtype=_MXU, name=f"mm_dwdown{l}", tm=1408)
        dgu = _swiglu_bwd(s["gu"], dact, name=f"swiglu_bwd{l}")
        g["ffn_w_gu"] = _mm(s["h_ffn"], dgu, ta=True, out_dtype=_MXU, name=f"mm_dwgu{l}")
        dh = _mm(dgu, P["ffn_w_gu"], bl=l, tb=True, name=f"mm_dhffn{l}", tk=1408)
        dx, g["ffn_norm"] = _rms_bwd(s["x_mid"], P["ffn_norm"][l], dh, dx, name=f"rms_ffn_bwd{l}")
        do = _mm(dx, P["xa_wo"], bl=l, tb=True, out_dtype=_MXU, name=f"mm_do{l}")
        g["xa_wo"] = _mm(s["o"], dx, ta=True, out_dtype=_MXU, name=f"mm_dwo{l}")
        dq, dkv = _xa_bwd(s["q"].reshape(B, T, D), s["kv"].reshape(B, M, 2 * D), do.reshape(B, T, D), name=f"xa_bwd{l}")
        dq, dkv = dq.reshape(N, D), dkv.reshape(B * M, 2 * D)
        g["xa_wq"] = _mm(s["h_xa"], dq, ta=True, out_dtype=_MXU, name=f"mm_dwq{l}")
        dh = _mm(dq, P["xa_wq"], bl=l, tb=True, name=f"mm_dhxa{l}")
        g["xa_wkv"] = _mm(s["mn"], dkv, ta=True, out_dtype=_MXU, name=f"mm_dwkv{l}")
        dmn = _mm(dkv, P["xa_wkv"], bl=l, tb=True, name=f"mm_dmn{l}")
        g["mem_norm"] = _rms_bwd(memf, P["mem_norm"][l], dmn, None, name=f"rms_mem_bwd{l}")
        dx, g["xa_norm"] = _rms_bwd(s["x_in"], P["xa_norm"][l], dh, dx, name=f"rms_xa_bwd{l}")
        return dx, g

    dx, g1 = xa_ffn_bwd(dx, s1, 1)
    G["conv_b_out"] = [_colsum(dx, name="colsum_b_out")]
    dy4 = _mm(dx, P["conv_w_out"], bl=0, tb=True, name="mm_dy4")
    G["conv_w_out"] = [_mm(y4.reshape(N, C), dx, ta=True, out_dtype=_MXU, name="mm_dwconv_out")]
    dy2, dlg, dlb, ddb = _conv_ln_bwd(y2.reshape(N, C), dy4, P["conv_ln_g"][0], P["conv_ln_b"][0], name="conv_ln_bwd")
    G["conv_ln_g"], G["conv_ln_b"], G["conv_dw_b"] = [dlg], [dlb], [ddb]
    dag, ddw, dbin = _conv_bwd(y.reshape(B, T, C), dy2.reshape(B, T, C), ag.reshape(B, T, 2 * C), dw_w, name="conv_bwd")
    G["conv_dw_w"], G["conv_b_in"] = [ddw[:_CONV_K]], [dbin]
    dag = dag.reshape(N, 2 * C)
    G["conv_w_in"] = [_mm(h_o, dag, ta=True, out_dtype=_MXU, name="mm_dwconv_in")]
    dh = _mm(dag, P["conv_w_in"], bl=0, tb=True, name="mm_dh_o")
    dx, dg = _rms_bwd(x3, P["mix_norm_o"][0], dh, dx, name="rms_mix_o_bwd")
    G["mix_norm_o"] = [dg]
    dx, g0 = xa_ffn_bwd(dx, s0, 0)
    for k in g0:
        G[k] = [g0[k], g1[k]]
    G["w_out_e"] = [_mm(mixcat, dx, ta=True, out_dtype=_MXU, name="mm_dwout_e")]
    dmix = _mm(dx, P["w_out_e"], bl=0, tb=True, name="mm_dmix")
    dz, dlg, dlb, dws, dbias = _gmlp_bwd(proj, dmix, 1, P["gmlp_ln_g"][0], P["gmlp_ln_b"][0], P["gmlp_w_s"][0], bias_full,
                                         name="gmlp_bwd")
    G["gmlp_ln_g"], G["gmlp_ln_b"], G["gmlp_w_s"] = [dlg], [dlb], [dws]
    G["gmlp_b_s"] = [dbias[:, :2 * (W // _LANES)].T]
    dmix3 = dmix.reshape(B, T, D)
    dq, dk, dv, dcum = _fox_bwd(proj3, cum, dmix3, lse, name="fox_bwd")
    df, dfb = _fox_cum_bwd(proj3, f_blk, P["fox_f_bias"][0], dcum.reshape(B, H, T), name="fox_cum_bwd")
    G["fox_f_bias"] = [dfb]
    dproj = jnp.concatenate([dq.reshape(N, W), dk.reshape(N, W), dv.reshape(N, W), dz, df.reshape(N, _LANES).astype(_MXU)], axis=1)
    G["w_in_pad"] = [_mm(h_e, dproj, ta=True, out_dtype=_MXU, name="mm_dwin_e", tn=384)]
    dh = _mm(dproj, P["w_in_pad"], bl=0, tb=True, name="mm_dh_e", tk=384)
    dx, dg = _rms_bwd(x0, P["mix_norm_e"][0], dh, dx, name="rms_mix_e_bwd")
    G["mix_norm_e"] = [dg]
    return loss, dx.reshape(B, T, D), G


def _pad_w_in(w_in, W, H):
    f = w_in[:, 3 * W:3 * W + H]
    return jnp.concatenate([w_in[:, :3 * W], w_in[:, 3 * W + H:], jnp.pad(f, ((0, 0), (0, _LANES - H)))], axis=1)


def _unpad_w_in(g, W, H):
    return jnp.concatenate([g[:, :3 * W], g[:, 5 * W:5 * W + H], g[:, 3 * W:5 * W]], axis=1)


def kernel(x, mem, mix_norm_e, w_in_e, fox_f_bias, gmlp_ln_g, gmlp_ln_b, gmlp_w_s, gmlp_b_s, w_out_e, mix_norm_o, conv_w_in, conv_b_in, conv_dw_w, conv_dw_b, conv_ln_g, conv_ln_b, conv_w_out, conv_b_out, xa_norm, mem_norm, xa_wq, xa_wkv, xa_wo, ffn_norm, ffn_w_gu, ffn_w_down, final_norm, loss_target, m_mix_norm_e, m_w_in_e, m_fox_f_bias, m_gmlp_ln_g, m_gmlp_ln_b, m_gmlp_w_s, m_gmlp_b_s, m_w_out_e, m_mix_norm_o, m_conv_w_in, m_conv_b_in, m_conv_dw_w, m_conv_dw_b, m_conv_ln_g, m_conv_ln_b, m_conv_w_out, m_conv_b_out, m_xa_norm, m_mem_norm, m_xa_wq, m_xa_wkv, m_xa_wo, m_ffn_norm, m_ffn_w_gu, m_ffn_w_down, m_final_norm, v_mix_norm_e, v_w_in_e, v_fox_f_bias, v_gmlp_ln_g, v_gmlp_ln_b, v_gmlp_w_s, v_gmlp_b_s, v_w_out_e, v_mix_norm_o, v_conv_w_in, v_conv_b_in, v_conv_dw_w, v_conv_dw_b, v_conv_ln_g, v_conv_ln_b, v_conv_w_out, v_conv_b_out, v_xa_norm, v_mem_norm, v_xa_wq, v_xa_wkv, v_xa_wo, v_ffn_norm, v_ffn_w_gu, v_ffn_w_down, v_final_norm):
    env = dict(locals())
    w = {n: env[n] for n in _WEIGHTS}
    mom = {n: env["m_" + n] for n in _WEIGHTS}
    var = {n: env["v_" + n] for n in _WEIGHTS}
    D = x.shape[-1]
    W = D // 2
    H = W // _FOX_HD

    big_src = {n: w[n].astype(_MXU) for n, _ in _BIG}
    items = [([big_src[n][l] for l in range(w[n].shape[0])], False) for n, _ in _BIG]
    items += [([w[n]], False) for n, _ in _SMALL_SHARDED]
    gathered = _exchange(items, name="gather")
    P = {n: w[n] for n in _REPLICATED}
    for (n, ax), g in zip(_BIG, gathered):
        P[n] = g.reshape(g.shape[0], -1, g.shape[-1]) if ax == 1 else _peers_to_cols(g)
    for (n, ax), g in zip(_SMALL_SHARDED, gathered[len(_BIG):]):
        P[n] = _peers_to_cols(g[0, :, 0])[None] if ax == 2 else g.reshape(1, -1)
    P["w_in_pad"] = _pad_w_in(P.pop("w_in_e")[0], W, H)[None]

    loss, grad_x, G = _local_step(x, mem, loss_target, P)
    loss = lax.psum(loss[0, 0], ("x", "y", "c"))
    G["w_in_e"] = [_unpad_w_in(G.pop("w_in_pad")[0], W, H)]

    items = []
    for n, ax in _BIG:
        items.append(([g.reshape(_N_DEV, -1, g.shape[-1]) if ax == 1 else _cols_to_peers(g) for g in G[n]], True))
    for n, ax in _SMALL_SHARDED:
        g = G[n][0]
        items.append(([_cols_to_peers(g) if ax == 2 else g.reshape(_N_DEV, 1, -1)], True))
    items += [(G[n], False) for n in _REPLICATED]
    recv = dict(zip([n for n, _ in _BIG + _SMALL_SHARDED] + list(_REPLICATED), _exchange(items, name="scatter")))

    res = {n: _adamw_big(recv[n], w[n], mom[n], var[n], name="adamw_" + n) for n, _ in _BIG}
    small = [n for n, _ in _SMALL_SHARDED] + list(_REPLICATED)

    def rows(a, r):
        return a.reshape((r.shape[0],) + r.shape[2:-1] + (-1,))

    outs = _adamw_small([(recv[n], rows(w[n], recv[n]), rows(mom[n], recv[n]), rows(var[n], recv[n]),
                          w[n].shape[-1] if w[n].shape[-1] != recv[n].shape[-1] else None) for n in small], name="adamw_small")
    for n, o in zip(small, outs):
        res[n] = tuple(a.reshape(w[n].shape) for a in o)
    return (loss, grad_x, *[res[n][0] for n in _WEIGHTS], *[res[n][1] for n in _WEIGHTS],
            *[res[n][2] for n in _WEIGHTS], *[res[n][3] for n in _WEIGHTS])
```

```python
import functools
import math

import jax
import jax.numpy as jnp
from jax import lax
from jax.experimental import pallas as pl
from jax.experimental.pallas import tpu as pltpu

_F32 = jnp.float32
_MXU = jnp.bfloat16
_VMEM_LIMIT = 48 * 1024 * 1024
_LANES = 128
_EPS = 1e-6
_N_DEV = 8
_FOX_HD = 64
_FOX_SCALE = _FOX_HD ** -0.5
_CHUNK = 128
_GRP = 64
_CONV_K = 31
_HALO = 32
_XA_HEADS = 4
_GELU_C = math.sqrt(2.0 / math.pi)
_ADAM_LR, _ADAM_B1, _ADAM_B2, _ADAM_EPS, _ADAM_WD, _ADAM_STEP = 0.001, 0.9, 0.999, 1e-08, 0.01, 10
_FLAT_W = 1024
_FLAT_ALIGN = 16 * _FLAT_W
_BIG_ROWS = 128


def _params(*sem):
    return pltpu.CompilerParams(dimension_semantics=sem if sem else None, vmem_limit_bytes=_VMEM_LIMIT)


def _pick(n, pref):
    if n <= pref:
        return n
    best = None
    for t in range(_LANES, pref + 1, _LANES):
        if n % t == 0:
            best = t
    assert best is not None, (n, pref)
    return best


def _rows(n, pref):
    if n <= pref:
        return n
    t = pref
    while n % t:
        t //= 2
    assert t >= 8, (n, pref)
    return t


def _sigmoid(x):
    return 1.0 / (1.0 + jnp.exp(-x))


def _gelu(x):
    t = jnp.tanh(_GELU_C * (x + 0.044715 * (x * x * x)))
    return 0.5 * x * (1.0 + t)


def _gelu_grad(x):
    x2 = x * x
    t = jnp.tanh(_GELU_C * (x + 0.044715 * (x2 * x)))
    return 0.5 * (1.0 + t) + 0.5 * x * (1.0 - t * t) * (_GELU_C * (1.0 + 3.0 * 0.044715 * x2))


def _dot(a, b, ca, cb):
    return lax.dot_general(a, b, (((ca,), (cb,)), ((), ())), preferred_element_type=_F32)


def _mm(a, b, *, name, ta=False, tb=False, al=None, bl=None, bk0=0, bias=None, res=None, out_dtype=_F32, tm=1024, tn=512, tk=1024):
    if ta:
        K, M = a.shape[-2:]
    else:
        M, K = a.shape[-2:]
    if tb:
        N, K2 = b.shape[-2:]
    else:
        K2, N = b.shape[-2:]
    assert K == K2 or (tb and K2 > K), (a.shape, b.shape, ta, tb)
    tm, tn = _pick(M, tm), _pick(N, tn)
    tk = K if (not ta and K <= 2816 and K2 == K) else _pick(K, tk)
    nk = K // tk
    assert bk0 % tk == 0
    kb = bk0 // tk
    grid = (M // tm, N // tn, nk)
    if a.ndim == 3:
        a_spec = (pl.BlockSpec((None, tk, tm), lambda i, j, k: (al, k, i)) if ta
                  else pl.BlockSpec((None, tm, tk), lambda i, j, k: (al, i, k)))
    else:
        a_spec = pl.BlockSpec((tk, tm), lambda i, j, k: (k, i)) if ta else pl.BlockSpec((tm, tk), lambda i, j, k: (i, k))
    if b.ndim == 3:
        b_spec = (pl.BlockSpec((None, tn, tk), lambda i, j, k: (bl, j, k + kb)) if tb
                  else pl.BlockSpec((None, tk, tn), lambda i, j, k: (bl, k, j)))
    else:
        b_spec = pl.BlockSpec((tn, tk), lambda i, j, k: (j, k)) if tb else pl.BlockSpec((tk, tn), lambda i, j, k: (k, j))
    in_specs, args = [a_spec, b_spec], [a, b]
    if bias is not None:
        in_specs.append(pl.BlockSpec((1, tn), lambda i, j, k: (0, j)))
        args.append(bias.reshape(1, N).astype(_F32))
    if res is not None:
        in_specs.append(pl.BlockSpec((tm, tn), lambda i, j, k: (i, j)))
        args.append(res)
    has_bias, has_res = bias is not None, res is not None

    def body(*refs):
        a_ref, b_ref = refs[0], refs[1]
        pos = 2
        bias_ref = res_ref = None
        if has_bias:
            bias_ref = refs[pos]
            pos += 1
        if has_res:
            res_ref = refs[pos]
            pos += 1
        o_ref = refs[pos]
        acc_ref = refs[pos + 1] if nk > 1 else None
        p = _dot(a_ref[...].astype(_MXU), b_ref[...].astype(_MXU), 0 if ta else 1, 1 if tb else 0)

        def finish(acc):
            if has_bias:
                acc = acc + bias_ref[...]
            if has_res:
                acc = acc + res_ref[...]
            o_ref[...] = acc.astype(o_ref.dtype)

        if nk == 1:
            finish(p)
        else:
            k = pl.program_id(2)

            @pl.when(k == 0)
            def _():
                acc_ref[...] = p

            @pl.when(k > 0)
            def _():
                acc_ref[...] += p

            @pl.when(k == nk - 1)
            def _():
                finish(acc_ref[...])

    return pl.pallas_call(
        body,
        name=name,
        out_shape=jax.ShapeDtypeStruct((M, N), out_dtype),
        grid=grid,
        in_specs=in_specs,
        out_specs=pl.BlockSpec((tm, tn), lambda i, j, k: (i, j)),
        scratch_shapes=[pltpu.VMEM((tm, tn), _F32)] if nk > 1 else [],
        compiler_params=_params("parallel", "parallel", "arbitrary"),
    )(*args)


def _rms_fwd(x, g, *, name):
    N, D = x.shape
    tr = _rows(N, 512)

    def body(x_ref, g_ref, o_ref):
        xv = x_ref[...]
        r = lax.rsqrt(jnp.mean(xv * xv, axis=-1, keepdims=True) + _EPS)
        o_ref[...] = (xv * r * g_ref[...]).astype(o_ref.dtype)

    return pl.pallas_call(
        body,
        name=name,
        out_shape=jax.ShapeDtypeStruct((N, D), _MXU),
        grid=(N // tr,),
        in_specs=[pl.BlockSpec((tr, D), lambda i: (i, 0)), pl.BlockSpec((1, D), lambda i: (0, 0))],
        out_specs=pl.BlockSpec((tr, D), lambda i: (i, 0)),
        compiler_params=_params("parallel"),
    )(x, g.reshape(1, D))


def _rms_bwd(x, g, dh, dres, *, name):
    N, D = x.shape
    tr = _rows(N, 256)
    has_res = dres is not None

    def body(*refs):
        if has_res:
            x_ref, g_ref, dh_ref, dres_ref, dx_ref, dg_ref = refs
        else:
            x_ref, g_ref, dh_ref, dg_ref = refs
        xv = x_ref[...]
        r = lax.rsqrt(jnp.mean(xv * xv, axis=-1, keepdims=True) + _EPS)
        xh = xv * r
        dhv = dh_ref[...].astype(_F32)

        @pl.when(pl.program_id(0) == 0)
        def _():
            dg_ref[...] = jnp.zeros_like(dg_ref)

        dg_ref[...] += jnp.sum(dhv * xh, axis=0, keepdims=True)
        if has_res:
            dxn = dhv * g_ref[...]
            dx = r * (dxn - xh * jnp.mean(dxn * xh, axis=-1, keepdims=True))
            dx_ref[...] = dres_ref[...] + dx

    row = pl.BlockSpec((tr, D), lambda i: (i, 0))
    vec = pl.BlockSpec((1, D), lambda i: (0, 0))
    if has_res:
        out_shape = (jax.ShapeDtypeStruct((N, D), _F32), jax.ShapeDtypeStruct((1, D), _F32))
        out_specs = (row, vec)
        in_specs, args = [row, vec, row, row], (x, g.reshape(1, D), dh, dres)
    else:
        out_shape = jax.ShapeDtypeStruct((1, D), _F32)
        out_specs = vec
        in_specs, args = [row, vec, row], (x, g.reshape(1, D), dh)
    return pl.pallas_call(
        body, name=name, out_shape=out_shape, grid=(N // tr,), in_specs=in_specs, out_specs=out_specs,
        compiler_params=_params("arbitrary"),
    )(*args)


def _colsum(a, *, name):
    M, C = a.shape
    tr = _rows(M, 512)

    def body(a_ref, o_ref):
        @pl.when(pl.program_id(0) == 0)
        def _():
            o_ref[...] = jnp.zeros_like(o_ref)

        o_ref[...] += jnp.sum(a_ref[...].astype(_F32), axis=0, keepdims=True)

    return pl.pallas_call(
        body, name=name, out_shape=jax.ShapeDtypeStruct((1, C), _F32), grid=(M // tr,),
        in_specs=[pl.BlockSpec((tr, C), lambda i: (i, 0))], out_specs=pl.BlockSpec((1, C), lambda i: (0, 0)),
        compiler_params=_params("arbitrary"),
    )(a)


def _final_loss(x, g, tgt, *, name):
    N, D = x.shape
    tr = _rows(N, 256)

    def body(x_ref, g_ref, t_ref, loss_ref, dx_ref, dg_ref):
        xv = x_ref[...]
        r = lax.rsqrt(jnp.mean(xv * xv, axis=-1, keepdims=True) + _EPS)
        xh = xv * r
        gv = g_ref[...]
        diff = xh * gv - t_ref[...]

        @pl.when(pl.program_id(0) == 0)
        def _():
            loss_ref[...] = jnp.zeros_like(loss_ref)
            dg_ref[...] = jnp.zeros_like(dg_ref)

        part = jnp.sum(jnp.sum(diff * diff, axis=1, keepdims=True), axis=0, keepdims=True) * (0.5 / D)
        loss_ref[...] += jnp.broadcast_to(part, loss_ref.shape)
        dy = diff * (1.0 / D)
        dg_ref[...] += jnp.sum(dy * xh, axis=0, keepdims=True)
        dxn = dy * gv
        dx_ref[...] = r * (dxn - xh * jnp.mean(dxn * xh, axis=-1, keepdims=True))

    row = pl.BlockSpec((tr, D), lambda i: (i, 0))
    vec = pl.BlockSpec((1, D), lambda i: (0, 0))
    return pl.pallas_call(
        body, name=name,
        out_shape=(jax.ShapeDtypeStruct((8, _LANES), _F32), jax.ShapeDtypeStruct((N, D), _F32), jax.ShapeDtypeStruct((1, D), _F32)),
        grid=(N // tr,), in_specs=[row, vec, row],
        out_specs=(pl.BlockSpec((8, _LANES), lambda i: (0, 0)), row, vec),
        compiler_params=_params("arbitrary"),
    )(x, g.reshape(1, D), tgt)


def _mm_gu(h, w_gu, l, *, name, tm=512, tn=1408):
    N, K = h.shape
    H = w_gu.shape[-1] // 2
    tm, tn = _pick(N, tm), _pick(H, tn)
    nj = H // tn

    def body(h_ref, wg_ref, wu_ref, gu_ref, act_ref):
        hv = h_ref[...].astype(_MXU)
        g = _dot(hv, wg_ref[...].astype(_MXU), 1, 0)
        u = _dot(hv, wu_ref[...].astype(_MXU), 1, 0)
        gu_ref[0] = g.astype(gu_ref.dtype)
        gu_ref[1] = u.astype(gu_ref.dtype)
        act_ref[...] = (g * _sigmoid(g) * u).astype(act_ref.dtype)

    return pl.pallas_call(
        body, name=name,
        out_shape=(jax.ShapeDtypeStruct((2, N, H), _MXU), jax.ShapeDtypeStruct((N, H), _MXU)), grid=(N // tm, nj),
        in_specs=[pl.BlockSpec((tm, K), lambda i, j: (i, 0)), pl.BlockSpec((None, K, tn), lambda i, j: (l, 0, j)),
                  pl.BlockSpec((None, K, tn), lambda i, j: (l, 0, j + nj))],
        out_specs=(pl.BlockSpec((2, tm, tn), lambda i, j: (0, i, j)), pl.BlockSpec((tm, tn), lambda i, j: (i, j))),
        compiler_params=_params("parallel", "parallel"),
    )(h, w_gu, w_gu)


def _mm_dgu(dx, w_down, l, gu, *, name, tm=512, tn=1408):
    N, K = dx.shape
    H = w_down.shape[-2]
    tm, tn = _pick(N, tm), _pick(H, tn)

    def body(dx_ref, w_ref, gu_ref, o_ref):
        d = _dot(dx_ref[...].astype(_MXU), w_ref[...].astype(_MXU), 1, 1)
        g, u = gu_ref[0].astype(_F32), gu_ref[1].astype(_F32)
        sg = _sigmoid(g)
        o_ref[0] = (d * u * (sg * (1.0 + g * (1.0 - sg)))).astype(o_ref.dtype)
        o_ref[1] = (d * (g * sg)).astype(o_ref.dtype)

    return pl.pallas_call(
        body, name=name, out_shape=jax.ShapeDtypeStruct((2, N, H), _MXU), grid=(N // tm, H // tn),
        in_specs=[pl.BlockSpec((tm, K), lambda i, j: (i, 0)), pl.BlockSpec((None, tn, K), lambda i, j: (l, j, 0)),
                  pl.BlockSpec((2, tm, tn), lambda i, j: (0, i, j))],
        out_specs=pl.BlockSpec((2, tm, tn), lambda i, j: (0, i, j)), compiler_params=_params("parallel", "parallel"),
    )(dx, w_down, gu)


def _gmlp_mix(vb, w, trans):
    tr, W = vb.shape
    lane = lax.broadcasted_iota(jnp.int32, (_CHUNK, _LANES), 1)
    rows = []
    for c in range(tr // _CHUNK):
        tiles = []
        for j in range(W // _LANES):
            t = vb[c * _CHUNK:(c + 1) * _CHUNK, j * _LANES:(j + 1) * _LANES]
            ma = _dot(w[2 * j], t, 0 if trans else 1, 0)
            mb = _dot(w[2 * j + 1], t, 0 if trans else 1, 0)
            tiles.append(jnp.where(lane < _GRP, ma, mb))
        rows.append(jnp.concatenate(tiles, axis=1))
    return jnp.concatenate(rows, axis=0)


def _tril_w(w_ref):
    r = lax.broadcasted_iota(jnp.int32, (_CHUNK, _CHUNK), 0)
    c = lax.broadcasted_iota(jnp.int32, (_CHUNK, _CHUNK), 1)
    return jnp.where((r >= c)[None], w_ref[...], 0.0).astype(_MXU)


def _layernorm_stats(v):
    mu = jnp.mean(v, axis=-1, keepdims=True)
    xc = v - mu
    rstd = lax.rsqrt(jnp.mean(xc * xc, axis=-1, keepdims=True) + _EPS)
    return xc * rstd, rstd


def _gmlp_fwd(proj, ln_g, ln_b, w_s, bias_full, *, name):
    N = proj.shape[0]
    W = ln_g.shape[-1]
    G = w_s.shape[0]
    tr = _rows(N, 512)
    ub, vb_ = 3, 4

    def body(u_ref, v_ref, g_ref, b_ref, w_ref, bias_ref, o_ref):
        u = _gelu(u_ref[...])
        xh, _ = _layernorm_stats(_gelu(v_ref[...]))
        vgn = xh * g_ref[...] + b_ref[...]
        mixed = _gmlp_mix(vgn.astype(_MXU), _tril_w(w_ref), False)
        bias = jnp.concatenate([bias_ref[...]] * (tr // _CHUNK), axis=0)
        o_ref[...] = (u * (mixed + bias)).astype(o_ref.dtype)

    vec = pl.BlockSpec((1, W), lambda i: (0, 0))
    return pl.pallas_call(
        body, name=name, out_shape=jax.ShapeDtypeStruct((N, W), _MXU), grid=(N // tr,),
        in_specs=[pl.BlockSpec((tr, W), lambda i: (i, ub)), pl.BlockSpec((tr, W), lambda i: (i, vb_)), vec, vec,
                  pl.BlockSpec((G, _CHUNK, _CHUNK), lambda i: (0, 0, 0)), pl.BlockSpec((_CHUNK, W), lambda i: (0, 0))],
        out_specs=pl.BlockSpec((tr, W), lambda i: (i, 0)), compiler_params=_params("parallel"),
    )(proj, proj, ln_g.reshape(1, W), ln_b.reshape(1, W), w_s, bias_full)


def _gmlp_bwd(proj, da_src, da_blk, ln_g, ln_b, w_s, bias_full, *, name):
    N = proj.shape[0]
    W = ln_g.shape[-1]
    G = w_s.shape[0]
    tr = _rows(N, 512)
    nch = tr // _CHUNK

    def body(u_ref, v_ref, da_ref, g_ref, b_ref, w_ref, bias_ref, dz_ref, dg_ref, db_ref, dw_ref, dbias_ref):
        @pl.when(pl.program_id(0) == 0)
        def _():
            dg_ref[...] = jnp.zeros_like(dg_ref)
            db_ref[...] = jnp.zeros_like(db_ref)
            dw_ref[...] = jnp.zeros_like(dw_ref)
            dbias_ref[...] = jnp.zeros_like(dbias_ref)

        u_pre, v_pre = u_ref[...], v_ref[...]
        ug = _gelu(u_pre)
        xh, rstd = _layernorm_stats(_gelu(v_pre))
        lg = g_ref[...]
        vgn = xh * lg + b_ref[...]
        vb = vgn.astype(_MXU)
        wt = _tril_w(w_ref)
        mixed = _gmlp_mix(vb, wt, False)
        bias = jnp.concatenate([bias_ref[...]] * nch, axis=0)
        da = da_ref[...].astype(_F32)
        du = da * (mixed + bias)
        dm = da * ug
        dmb = dm.astype(_MXU)
        lane = lax.broadcasted_iota(jnp.int32, (_CHUNK, _LANES), 1)
        r = lax.broadcasted_iota(jnp.int32, (_CHUNK, _CHUNK), 0)
        c = lax.broadcasted_iota(jnp.int32, (_CHUNK, _CHUNK), 1)
        tril = r >= c
        dmsum = dm[0:_CHUNK]
        for ch in range(1, nch):
            dmsum = dmsum + dm[ch * _CHUNK:(ch + 1) * _CHUNK]
        dbias = jnp.zeros((_CHUNK, _LANES), _F32)
        for j in range(W // _LANES):
            tile = dmsum[:, j * _LANES:(j + 1) * _LANES]
            sa = jnp.sum(jnp.where(lane < _GRP, tile, 0.0), axis=1, keepdims=True)
            sb = jnp.sum(jnp.where(lane >= _GRP, tile, 0.0), axis=1, keepdims=True)
            dbias = dbias + jnp.where(lane == 2 * j, sa, 0.0) + jnp.where(lane == 2 * j + 1, sb, 0.0)
            acc_a = jnp.zeros((_CHUNK, _CHUNK), _F32)
            acc_b = jnp.zeros((_CHUNK, _CHUNK), _F32)
            for ch in range(nch):
                dt = dmb[ch * _CHUNK:(ch + 1) * _CHUNK, j * _LANES:(j + 1) * _LANES]
                vt = vb[ch * _CHUNK:(ch + 1) * _CHUNK, j * _LANES:(j + 1) * _LANES]
                acc_a = acc_a + _dot(jnp.where(lane < _GRP, dt, jnp.zeros_like(dt)), vt, 1, 1)
                acc_b = acc_b + _dot(jnp.where(lane >= _GRP, dt, jnp.zeros_like(dt)), vt, 1, 1)
            dw_ref[2 * j] += jnp.where(tril, acc_a, 0.0)
            dw_ref[2 * j + 1] += jnp.where(tril, acc_b, 0.0)
        dbias_ref[...] += dbias
        dvgn = _gmlp_mix(dmb, wt, True)
        dg_ref[...] += jnp.sum(dvgn * xh, axis=0, keepdims=True)
        db_ref[...] += jnp.sum(dvgn, axis=0, keepdims=True)
        dxh = dvgn * lg
        dvg = rstd * (dxh - jnp.mean(dxh, axis=-1, keepdims=True) - xh * jnp.mean(dxh * xh, axis=-1, keepdims=True))
        dz_ref[:, :W] = (du * _gelu_grad(u_pre)).astype(dz_ref.dtype)
        dz_ref[:, W:] = (dvg * _gelu_grad(v_pre)).astype(dz_ref.dtype)

    vec = pl.BlockSpec((1, W), lambda i: (0, 0))
    wspec = pl.BlockSpec((G, _CHUNK, _CHUNK), lambda i: (0, 0, 0))
    return pl.pallas_call(
        body, name=name,
        out_shape=(jax.ShapeDtypeStruct((N, 2 * W), _MXU), jax.ShapeDtypeStruct((1, W), _F32), jax.ShapeDtypeStruct((1, W), _F32),
                   jax.ShapeDtypeStruct((G, _CHUNK, _CHUNK), _F32), jax.ShapeDtypeStruct((_CHUNK, _LANES), _F32)),
        grid=(N // tr,),
        in_specs=[pl.BlockSpec((tr, W), lambda i: (i, 3)), pl.BlockSpec((tr, W), lambda i: (i, 4)),
                  pl.BlockSpec((tr, W), lambda i: (i, da_blk)), vec, vec, wspec, pl.BlockSpec((_CHUNK, W), lambda i: (0, 0))],
        out_specs=(pl.BlockSpec((tr, 2 * W), lambda i: (i, 0)), vec, vec, wspec, pl.BlockSpec((_CHUNK, _LANES), lambda i: (0, 0))),
        compiler_params=_params("arbitrary"),
    )(proj, proj, da_src, ln_g.reshape(1, W), ln_b.reshape(1, W), w_s, bias_full)


def _lane_cumsum(v):
    T = v.shape[1]
    lane = lax.broadcasted_iota(jnp.int32, (8, _LANES), 1)
    carry = jnp.zeros((8, 1), _F32)
    out = []
    for ch in range(T // _LANES):
        blk = v[:, ch * _LANES:(ch + 1) * _LANES]
        sh = 1
        while sh < _LANES:
            blk = blk + jnp.where(lane >= sh, pltpu.roll(blk, sh, 1), 0.0)
            sh *= 2
        blk = blk + carry
        carry = blk[:, _LANES - 1:_LANES]
        out.append(blk)
    return jnp.concatenate(out, axis=1), carry


def _log_sigmoid(x):
    return jnp.minimum(x, 0.0) - jnp.log(1.0 + jnp.exp(-jnp.abs(x)))


def _fox_cum(proj3, f_blk, f_bias, *, name):
    B, T, _ = proj3.shape
    H = f_bias.shape[-1]
    assert H == 8

    def body(f_ref, b_ref, o_ref):
        x = f_ref[0].T[0:8, :] + b_ref[...]
        cum, _ = _lane_cumsum(_log_sigmoid(x))
        o_ref[0] = cum

    return pl.pallas_call(
        body, name=name, out_shape=jax.ShapeDtypeStruct((B, 8, T), _F32), grid=(B,),
        in_specs=[pl.BlockSpec((1, T, _LANES), lambda b: (b, 0, f_blk)), pl.BlockSpec((8, 1), lambda b: (0, 0))],
        out_specs=pl.BlockSpec((1, 8, T), lambda b: (b, 0, 0)), compiler_params=_params("parallel"),
    )(proj3, f_bias.reshape(8, 1))


def _fox_cum_bwd(proj3, f_blk, f_bias, dcum, *, name):
    B, T, _ = proj3.shape

    def body(f_ref, b_ref, dc_ref, df_ref, dbias_ref):
        @pl.when(pl.program_id(0) == 0)
        def _():
            dbias_ref[...] = jnp.zeros_like(dbias_ref)

        x = f_ref[0].T[0:8, :] + b_ref[...]
        dc = dc_ref[0]
        incl, total = _lane_cumsum(dc)
        dlf = total - incl + dc
        df = dlf * _sigmoid(-x)
        full = jnp.concatenate([df, jnp.zeros((_LANES - 8, T), _F32)], axis=0).T
        dbias_ref[...] += jnp.sum(full, axis=0, keepdims=True)
        df_ref[0] = full

    return pl.pallas_call(
        body, name=name,
        out_shape=(jax.ShapeDtypeStruct((B, T, _LANES), _F32), jax.ShapeDtypeStruct((1, _LANES), _F32)), grid=(B,),
        in_specs=[pl.BlockSpec((1, T, _LANES), lambda b: (b, 0, f_blk)), pl.BlockSpec((8, 1), lambda b: (0, 0)),
                  pl.BlockSpec((1, 8, T), lambda b: (b, 0, 0))],
        out_specs=(pl.BlockSpec((1, T, _LANES), lambda b: (b, 0, 0)), pl.BlockSpec((1, _LANES), lambda b: (0, 0))),
        compiler_params=_params("arbitrary"),
    )(proj3, f_bias.reshape(8, 1), dcum)


def _cum_row(cum_ref, h, start, size):
    blk = cum_ref[0, :, pl.ds(start, size)]
    sub = lax.broadcasted_iota(jnp.int32, (blk.shape[0], 1), 0)
    return jnp.sum(jnp.where(sub == h, blk, 0.0), axis=0, keepdims=True)


def _causal(tq, q0, k0):
    r = lax.broadcasted_iota(jnp.int32, (tq, tq), 0)
    c = lax.broadcasted_iota(jnp.int32, (tq, tq), 1)
    return (r + q0) >= (c + k0)


def _fused_call(body, *, name, out_shape, grid, in_specs, out_specs, scratch_shapes, sem, args, xchg):
    out_shape, in_specs, out_specs, scratch_shapes = list(out_shape), list(in_specs), list(out_specs), list(scratch_shapes)
    if xchg is None:
        res = pl.pallas_call(body, name=name, out_shape=out_shape, grid=grid, in_specs=in_specs, out_specs=out_specs,
                             scratch_shapes=scratch_shapes, compiler_params=_params(*sem))(*args)
        return list(res), []
    n_in, n_out, n_scr = len(in_specs), len(out_specs), len(scratch_shapes)

    def fused(*refs):
        ins, refs = refs[:n_in], refs[n_in:]
        xs, refs = refs[:xchg.n_src], refs[xchg.n_src:]
        outs, refs = refs[:n_out], refs[n_out:]
        xd, refs = refs[:xchg.n_dst], refs[xchg.n_dst:]
        scr, sems = refs[:n_scr], refs[n_scr:]
        first = last = None
        for d, g in enumerate(grid):
            i = pl.program_id(d)
            first = (i == 0) if first is None else first & (i == 0)
            last = (i == g - 1) if last is None else last & (i == g - 1)

        @pl.when(first)
        def _():
            xchg.start(xs, xd, sems)

        body(*ins, *outs, *scr)

        @pl.when(last)
        def _():
            xchg.finish(xs, xd, sems)

    res = pl.pallas_call(
        fused, name=name, out_shape=out_shape + xchg.out_shapes, grid=grid, in_specs=in_specs + xchg.in_specs,
        out_specs=out_specs + xchg.out_specs, scratch_shapes=scratch_shapes + xchg.scratch,
        compiler_params=_params(*["arbitrary"] * len(grid)),
    )(*args, *xchg.srcs)
    return list(res[:n_out]), list(res[n_out:])


def _fox_fwd(proj3, cum, *, name, xchg=None):
    B, T, _ = proj3.shape
    H = cum.shape[1]
    W = H * _FOX_HD
    npair = W // _LANES
    tq = _rows(T, 256)
    nq = T // tq

    def body(q_ref, k_ref, v_ref, cum_ref, o_ref, lse_ref):
        p = pl.program_id(1)
        i = pl.program_id(2)
        q0 = pl.multiple_of(i * tq, tq)
        lane = lax.broadcasted_iota(jnp.int32, (1, _LANES), 1)
        q2 = q_ref[0] * _FOX_SCALE
        o2 = jnp.zeros((tq, _LANES), _F32)
        for hh in range(2):
            msk = (lane < _FOX_HD) if hh == 0 else (lane >= _FOX_HD)
            h = 2 * p + hh
            qm = jnp.where(msk, q2, 0.0).astype(_MXU)
            c0 = _cum_row(cum_ref, h, q0, _LANES)[:, 0:1]

            def step(jj, carry, masked, qm=qm, msk=msk, h=h, c0=c0):
                m_prev, l_prev, acc = carry
                k0 = pl.multiple_of(jj * tq, tq)
                k2 = k_ref[0, pl.ds(k0, tq), :].astype(_MXU)
                vm = jnp.where(msk, v_ref[0, pl.ds(k0, tq), :], 0.0).astype(_MXU)
                s = _dot(qm, k2, 1, 1) + (c0 - _cum_row(cum_ref, h, k0, tq))
                if masked:
                    s = jnp.where(_causal(tq, q0, k0), s, -jnp.inf)
                m_new = jnp.maximum(m_prev, jnp.max(s, axis=1, keepdims=True))
                alpha = jnp.exp(m_prev - m_new)
                e = jnp.exp(s - m_new)
                l_new = alpha * l_prev + jnp.sum(e, axis=1, keepdims=True)
                acc = alpha * acc + _dot(e.astype(_MXU), vm, 1, 0)
                return m_new, l_new, acc

            init = (jnp.full((tq, 1), -jnp.inf, _F32), jnp.zeros((tq, 1), _F32), jnp.zeros((tq, _LANES), _F32))
            carry = lax.fori_loop(0, i, functools.partial(step, masked=False), init)
            m, l, acc = step(i, carry, True)
            o2 = o2 + acc / l
            lse_ref[0, hh] = jnp.broadcast_to(m + jnp.log(l), (tq, _LANES))
        o_ref[0] = o2.astype(o_ref.dtype)

    (o, lse), got = _fused_call(
        body, name=name,
        out_shape=(jax.ShapeDtypeStruct((B, T, W), _MXU), jax.ShapeDtypeStruct((B, H, T, _LANES), _F32)),
        grid=(B, npair, nq),
        in_specs=[pl.BlockSpec((1, tq, _LANES), lambda b, p, i: (b, i, p)),
                  pl.BlockSpec((1, T, _LANES), lambda b, p, i: (b, 0, npair + p)),
                  pl.BlockSpec((1, T, _LANES), lambda b, p, i: (b, 0, 2 * npair + p)),
                  pl.BlockSpec((1, H, T), lambda b, p, i: (b, 0, 0))],
        out_specs=(pl.BlockSpec((1, tq, _LANES), lambda b, p, i: (b, i, p)),
                   pl.BlockSpec((1, 2, tq, _LANES), lambda b, p, i: (b, p, i, 0))),
        scratch_shapes=[], sem=("parallel", "parallel", "parallel"), args=(proj3, proj3, proj3, cum), xchg=xchg)
    return o, lse, got


def _fox_bwd(proj3, cum, do3, lse, *, name, xchg=None):
    B, T, _ = proj3.shape
    H = cum.shape[1]
    W = H * _FOX_HD
    npair = W // _LANES
    tq = _rows(T, 256)
    nq = T // tq

    def body(q_ref, k_ref, v_ref, cum_ref, do_ref, lse_ref, dq_ref, dk_ref, dv_ref, dc_ref, p_scr, dp_scr, dk_acc, dv_acc, dc_acc):
        p = pl.program_id(1)
        i = pl.program_id(2)
        q0 = pl.multiple_of(i * tq, tq)
        lane = lax.broadcasted_iota(jnp.int32, (1, _LANES), 1)

        @pl.when(i == 0)
        def _():
            dk_acc[...] = jnp.zeros_like(dk_acc)
            dv_acc[...] = jnp.zeros_like(dv_acc)
            dc_acc[...] = jnp.zeros_like(dc_acc)

        q2 = q_ref[0] * _FOX_SCALE
        do2 = do_ref[0].astype(_F32)
        dq2 = jnp.zeros((tq, _LANES), _F32)
        for hh in range(2):
            msk = (lane < _FOX_HD) if hh == 0 else (lane >= _FOX_HD)
            h = 2 * p + hh
            qm = jnp.where(msk, q2, 0.0).astype(_MXU)
            dom = jnp.where(msk, do2, 0.0).astype(_MXU)
            c0 = _cum_row(cum_ref, h, q0, _LANES)[:, 0:1]
            lse_h = lse_ref[0, hh][:, 0:1]

            def first(jj, delta, masked, h=h, qm=qm, dom=dom, c0=c0, lse_h=lse_h):
                k0 = pl.multiple_of(jj * tq, tq)
                kb = k_ref[0, pl.ds(k0, tq), :].astype(_MXU)
                vb = v_ref[0, pl.ds(k0, tq), :].astype(_MXU)
                s = _dot(qm, kb, 1, 1) + (c0 - _cum_row(cum_ref, h, k0, tq))
                pr = jnp.exp(s - lse_h)
                if masked:
                    pr = jnp.where(_causal(tq, q0, k0), pr, 0.0)
                dp = _dot(dom, vb, 1, 1)
                p_scr[jj] = pr
                dp_scr[jj] = dp
                return delta + jnp.sum(pr * dp, axis=1, keepdims=True)

            delta = lax.fori_loop(0, i, functools.partial(first, masked=False), jnp.zeros((tq, 1), _F32))
            delta = first(i, delta, True)

            def second(jj, dq, msk=msk, qm=qm, dom=dom, delta=delta, hh=hh):
                k0 = pl.multiple_of(jj * tq, tq)
                km = jnp.where(msk, k_ref[0, pl.ds(k0, tq), :], 0.0).astype(_MXU)
                pr = p_scr[jj]
                ds = pr * (dp_scr[jj] - delta)
                dsb = ds.astype(_MXU)
                dv_acc[pl.ds(k0, tq), :] += _dot(pr.astype(_MXU), dom, 0, 0)
                dk_acc[pl.ds(k0, tq), :] += _dot(dsb, qm, 0, 0)
                dc_acc[hh:hh + 1, pl.ds(k0, tq)] += jnp.sum(ds, axis=0, keepdims=True)
                return dq + _dot(dsb, km, 1, 0)

            dq2 = dq2 + lax.fori_loop(0, i + 1, second, jnp.zeros((tq, _LANES), _F32))
        dq_ref[0] = (dq2 * _FOX_SCALE).astype(dq_ref.dtype)

        @pl.when(i == nq - 1)
        def _():
            dk_ref[0] = dk_acc[...].astype(dk_ref.dtype)
            dv_ref[0] = dv_acc[...].astype(dv_ref.dtype)
            dc_ref[0, 0] = -dc_acc[...]

    full = lambda blk: pl.BlockSpec((1, T, _LANES), lambda b, p, i, blk=blk: (b, 0, blk * npair + p))
    part = lambda blk: pl.BlockSpec((1, tq, _LANES), lambda b, p, i, blk=blk: (b, i, blk * npair + p))
    (dq, dk, dv, dcum), got = _fused_call(
        body, name=name,
        out_shape=(jax.ShapeDtypeStruct((B, T, W), _MXU), jax.ShapeDtypeStruct((B, T, W), _MXU),
                   jax.ShapeDtypeStruct((B, T, W), _MXU), jax.ShapeDtypeStruct((B, npair, 2, T), _F32)),
        grid=(B, npair, nq),
        in_specs=[part(0), full(1), full(2), pl.BlockSpec((1, H, T), lambda b, p, i: (b, 0, 0)), part(0),
                  pl.BlockSpec((1, 2, tq, _LANES), lambda b, p, i: (b, p, i, 0))],
        out_specs=(part(0), full(0), full(0), pl.BlockSpec((1, 1, 2, T), lambda b, p, i: (b, p, 0, 0))),
        scratch_shapes=[pltpu.VMEM((nq, tq, tq), _F32), pltpu.VMEM((nq, tq, tq), _F32), pltpu.VMEM((T, _LANES), _F32),
                        pltpu.VMEM((T, _LANES), _F32), pltpu.VMEM((2, T), _F32)],
        sem=("parallel", "parallel", "arbitrary"), args=(proj3, proj3, proj3, cum, do3, lse), xchg=xchg)
    return dq, dk, dv, dcum, got


def _xa_probs(qh, kh, scale):
    s = _dot(qh, kh, 1, 1) * scale
    e = jnp.exp(s - jnp.max(s, axis=1, keepdims=True))
    return e / jnp.sum(e, axis=1, keepdims=True)


def _xa_fwd(q3, kv3, *, name):
    B, T, D = q3.shape
    M = kv3.shape[1]
    hd = D // _XA_HEADS
    scale = hd ** -0.5
    tq = _rows(T, 512)

    def body(q_ref, kv_ref, o_ref):
        for h in range(_XA_HEADS):
            sl = slice(h * hd, (h + 1) * hd)
            p = _xa_probs(q_ref[0, :, sl], kv_ref[0, :, sl], scale)
            o_ref[0, :, sl] = _dot(p.astype(_MXU), kv_ref[0, :, D + h * hd:D + (h + 1) * hd], 1, 0).astype(o_ref.dtype)

    return pl.pallas_call(
        body, name=name, out_shape=jax.ShapeDtypeStruct((B, T, D), _MXU), grid=(B, T // tq),
        in_specs=[pl.BlockSpec((1, tq, D), lambda b, i: (b, i, 0)), pl.BlockSpec((1, M, 2 * D), lambda b, i: (b, 0, 0))],
        out_specs=pl.BlockSpec((1, tq, D), lambda b, i: (b, i, 0)), compiler_params=_params("parallel", "parallel"),
    )(q3, kv3)


def _xa_bwd(q3, kv3, do3, *, name):
    B, T, D = q3.shape
    M = kv3.shape[1]
    hd = D // _XA_HEADS
    scale = hd ** -0.5
    tq = _rows(T, 512)

    def body(q_ref, kv_ref, do_ref, dq_ref, dkv_ref):
        @pl.when(pl.program_id(1) == 0)
        def _():
            dkv_ref[...] = jnp.zeros_like(dkv_ref)

        for h in range(_XA_HEADS):
            sl = slice(h * hd, (h + 1) * hd)
            slv = slice(D + h * hd, D + (h + 1) * hd)
            qh, kh, vh, doh = q_ref[0, :, sl], kv_ref[0, :, sl], kv_ref[0, :, slv], do_ref[0, :, sl]
            p = _xa_probs(qh, kh, scale)
            dkv_ref[0, :, slv] += _dot(p.astype(_MXU), doh, 0, 0)
            dp = _dot(doh, vh, 1, 1)
            ds = (p * (dp - jnp.sum(p * dp, axis=1, keepdims=True))).astype(_MXU)
            dq_ref[0, :, sl] = (_dot(ds, kh, 1, 0) * scale).astype(dq_ref.dtype)
            dkv_ref[0, :, sl] += _dot(ds, qh, 0, 0) * scale

    blk = pl.BlockSpec((1, tq, D), lambda b, i: (b, i, 0))
    kvs = pl.BlockSpec((1, M, 2 * D), lambda b, i: (b, 0, 0))
    return pl.pallas_call(
        body, name=name,
        out_shape=(jax.ShapeDtypeStruct((B, T, D), _MXU), jax.ShapeDtypeStruct((B, M, 2 * D), _F32)),
        grid=(B, T // tq), in_specs=[blk, kvs, blk], out_specs=(blk, kvs),
        compiler_params=_params("parallel", "arbitrary"),
    )(q3, kv3, do3)


def _glu_fwd(ag, *, name):
    N, C2 = ag.shape
    C = C2 // 2
    tr = _rows(N, 512)

    def body(a_ref, g_ref, o_ref):
        o_ref[...] = a_ref[...] * _sigmoid(g_ref[...])

    return pl.pallas_call(
        body, name=name, out_shape=jax.ShapeDtypeStruct((N, C), _F32), grid=(N // tr,),
        in_specs=[pl.BlockSpec((tr, C), lambda i: (i, 0)), pl.BlockSpec((tr, C), lambda i: (i, 1))],
        out_specs=pl.BlockSpec((tr, C), lambda i: (i, 0)), compiler_params=_params("parallel"),
    )(ag, ag)


def _conv_fwd(y3, dw_w, dw_b, ln_g, ln_b, *, name):
    B, T, C = y3.shape
    tt = _rows(T, 256)
    nt = T // tt

    def body(prev_ref, cur_ref, w_ref, b_ref, g_ref, lb_ref, y2_ref, y4_ref, ext):
        i = pl.program_id(1)
        ext[0:_HALO, :] = jnp.where(i > 0, prev_ref[0, tt - _HALO:tt, :], 0.0)
        ext[_HALO:_HALO + tt, :] = cur_ref[0]
        acc = jnp.broadcast_to(b_ref[...], (tt, C))
        for j in range(_CONV_K):
            off = _HALO - (_CONV_K - 1) + j
            acc = acc + w_ref[j:j + 1, :] * ext[off:off + tt, :]
        y2_ref[0] = acc
        xh, _ = _layernorm_stats(acc)
        z = xh * g_ref[...] + lb_ref[...]
        y4_ref[0] = (z * _sigmoid(z)).astype(y4_ref.dtype)

    vec = pl.BlockSpec((1, C), lambda b, i: (0, 0))
    blk = pl.BlockSpec((1, tt, C), lambda b, i: (b, i, 0))
    return pl.pallas_call(
        body, name=name,
        out_shape=(jax.ShapeDtypeStruct((B, T, C), _F32), jax.ShapeDtypeStruct((B, T, C), _MXU)),
        grid=(B, nt),
        in_specs=[pl.BlockSpec((1, tt, C), lambda b, i: (b, jnp.maximum(i - 1, 0), 0)), blk,
                  pl.BlockSpec((_HALO, C), lambda b, i: (0, 0)), vec, vec, vec],
        out_specs=(blk, blk), scratch_shapes=[pltpu.VMEM((tt + _HALO, C), _F32)],
        compiler_params=_params("parallel", "parallel"),
    )(y3, y3, dw_w, dw_b.reshape(1, C), ln_g.reshape(1, C), ln_b.reshape(1, C))


def _conv_ln_bwd(y2, dy4, ln_g, ln_b, *, name):
    N, C = y2.shape
    tr = _rows(N, 256)

    def body(y_ref, d_ref, g_ref, b_ref, dy_ref, dg_ref, db_ref, dwb_ref):
        @pl.when(pl.program_id(0) == 0)
        def _():
            dg_ref[...] = jnp.zeros_like(dg_ref)
            db_ref[...] = jnp.zeros_like(db_ref)
            dwb_ref[...] = jnp.zeros_like(dwb_ref)

        xh, rstd = _layernorm_stats(y_ref[...])
        gv = g_ref[...]
        z = xh * gv + b_ref[...]
        sg = _sigmoid(z)
        dz = d_ref[...] * (sg * (1.0 + z * (1.0 - sg)))
        dg_ref[...] += jnp.sum(dz * xh, axis=0, keepdims=True)
        db_ref[...] += jnp.sum(dz, axis=0, keepdims=True)
        dxh = dz * gv
        dy = rstd * (dxh - jnp.mean(dxh, axis=-1, keepdims=True) - xh * jnp.mean(dxh * xh, axis=-1, keepdims=True))
        dwb_ref[...] += jnp.sum(dy, axis=0, keepdims=True)
        dy_ref[...] = dy

    row = pl.BlockSpec((tr, C), lambda i: (i, 0))
    vec = pl.BlockSpec((1, C), lambda i: (0, 0))
    v = jax.ShapeDtypeStruct((1, C), _F32)
    return pl.pallas_call(
        body, name=name, out_shape=(jax.ShapeDtypeStruct((N, C), _F32), v, v, v), grid=(N // tr,),
        in_specs=[row, row, vec, vec], out_specs=(row, vec, vec, vec), compiler_params=_params("arbitrary"),
    )(y2, dy4, ln_g.reshape(1, C), ln_b.reshape(1, C))


def _conv_bwd(y3, dy23, ag3, dw_w, *, name):
    B, T, C = y3.shape
    tt = _rows(T, 256)
    nt = T // tt

    def body(yp_ref, yc_ref, dc_ref, dn_ref, a_ref, g_ref, w_ref, dag_ref, dw_ref, dbin_ref, yext, dext):
        b = pl.program_id(0)
        i = pl.program_id(1)

        @pl.when((b == 0) & (i == 0))
        def _():
            dw_ref[...] = jnp.zeros_like(dw_ref)
            dbin_ref[...] = jnp.zeros_like(dbin_ref)

        yext[0:_HALO, :] = jnp.where(i > 0, yp_ref[0, tt - _HALO:tt, :], 0.0)
        yext[_HALO:_HALO + tt, :] = yc_ref[0]
        d_cur = dc_ref[0]
        dext[0:tt, :] = d_cur
        dext[tt:tt + _HALO, :] = jnp.where(i < nt - 1, dn_ref[0, 0:_HALO, :], 0.0)
        dy = jnp.zeros((tt, C), _F32)
        for j in range(_CONV_K):
            sh = _CONV_K - 1 - j
            dy = dy + w_ref[j:j + 1, :] * dext[sh:sh + tt, :]
            off = _HALO - sh
            dw_ref[j:j + 1, :] += jnp.sum(d_cur * yext[off:off + tt, :], axis=0, keepdims=True)
        a, g = a_ref[0], g_ref[0]
        sg = _sigmoid(g)
        da = dy * sg
        dg = dy * a * (sg * (1.0 - sg))
        dag_ref[0, :, :C] = da.astype(dag_ref.dtype)
        dag_ref[0, :, C:] = dg.astype(dag_ref.dtype)
        dbin_ref[:, :C] += jnp.sum(da, axis=0, keepdims=True)
        dbin_ref[:, C:] += jnp.sum(dg, axis=0, keepdims=True)

    blk = pl.BlockSpec((1, tt, C), lambda b, i: (b, i, 0))
    return pl.pallas_call(
        body, name=name,
        out_shape=(jax.ShapeDtypeStruct((B, T, 2 * C), _MXU), jax.ShapeDtypeStruct((_HALO, C), _F32),
                   jax.ShapeDtypeStruct((1, 2 * C), _F32)),
        grid=(B, nt),
        in_specs=[pl.BlockSpec((1, tt, C), lambda b, i: (b, jnp.maximum(i - 1, 0), 0)), blk, blk,
                  pl.BlockSpec((1, tt, C), lambda b, i: (b, jnp.minimum(i + 1, nt - 1), 0)),
                  blk, pl.BlockSpec((1, tt, C), lambda b, i: (b, i, 1)), pl.BlockSpec((_HALO, C), lambda b, i: (0, 0))],
        out_specs=(pl.BlockSpec((1, tt, 2 * C), lambda b, i: (b, i, 0)), pl.BlockSpec((_HALO, C), lambda b, i: (0, 0)),
                   pl.BlockSpec((1, 2 * C), lambda b, i: (0, 0))),
        scratch_shapes=[pltpu.VMEM((tt + _HALO, C), _F32), pltpu.VMEM((tt + _HALO, C), _F32)],
        compiler_params=_params("arbitrary", "arbitrary"),
    )(y3, y3, dy23, dy23, ag3, ag3, dw_w)


class _Exchange:
    def __init__(self, items):
        self.per_peer = [pp for _, pp in items]
        self.srcs, self.out_shapes, self.pieces = [], [], []
        for t, (srcs, per_peer) in enumerate(items):
            blk = srcs[0].shape[1:] if per_peer else srcs[0].shape
            self.out_shapes.append(jax.ShapeDtypeStruct((len(srcs), _N_DEV) + tuple(blk), srcs[0].dtype))
            for l, s in enumerate(srcs):
                self.pieces.append((t, l, len(self.srcs)))
                self.srcs.append(s)
        self.n_src, self.n_dst, n_pc = len(self.srcs), len(items), len(self.pieces)
        self.in_specs = [pl.BlockSpec(memory_space=pl.ANY)] * self.n_src
        self.out_specs = [pl.BlockSpec(memory_space=pl.ANY)] * self.n_dst
        self.scratch = [pltpu.SemaphoreType.DMA((n_pc, _N_DEV - 1)), pltpu.SemaphoreType.DMA((n_pc, _N_DEV - 1)),
                        pltpu.SemaphoreType.DMA((n_pc,))]

    def _copies(self, src_refs, dst_refs, sems, arrivals):
        send_sems, recv_sems, loc_sems = sems
        x, y, c = lax.axis_index("x"), lax.axis_index("y"), lax.axis_index("c")
        me = 4 * x + 2 * y + c
        local, remote = [], []
        for i, (t, l, s) in enumerate(self.pieces):
            def src_for(p, s=s, t=t):
                return src_refs[s].at[p] if self.per_peer[t] else src_refs[s]

            local.append(pltpu.make_async_copy(src_for(me), dst_refs[t].at[l, me], loc_sems.at[i]))
            for k in range(1, _N_DEV):
                px, py, pc = (1 - x if k & 4 else x), (1 - y if k & 2 else y), (1 - c if k & 1 else c)
                p = 4 * px + 2 * py + pc
                remote.append(pltpu.make_async_remote_copy(
                    src_ref=src_for(p), dst_ref=dst_refs[t].at[l, p if arrivals else me],
                    send_sem=send_sems.at[i, k - 1], recv_sem=recv_sems.at[i, k - 1],
                    device_id=(px, py, pc), device_id_type=pl.DeviceIdType.MESH))
        return local, remote

    def start(self, src_refs, dst_refs, sems):
        local, remote = self._copies(src_refs, dst_refs, sems, False)
        for cp in local + remote:
            cp.start()

    def finish(self, src_refs, dst_refs, sems):
        local, sends = self._copies(src_refs, dst_refs, sems, False)
        for cp in sends:
            cp.wait_send()
        for cp in self._copies(src_refs, dst_refs, sems, True)[1]:
            cp.wait_recv()
        for cp in local:
            cp.wait()


def _exchange(items, *, name):
    ex = _Exchange(items)

    def body(*refs):
        parts = refs[:ex.n_src], refs[ex.n_src:ex.n_src + ex.n_dst], refs[ex.n_src + ex.n_dst:]
        ex.start(*parts)
        ex.finish(*parts)

    return pl.pallas_call(
        body, name=name, out_shape=ex.out_shapes, in_specs=ex.in_specs, out_specs=ex.out_specs, scratch_shapes=ex.scratch,
        compiler_params=pltpu.CompilerParams(has_side_effects=True),
    )(*ex.srcs)


def _adam_update(g, w, m, v):
    c1 = 1.0 / (1.0 - _ADAM_B1 ** _ADAM_STEP)
    c2 = 1.0 / (1.0 - _ADAM_B2 ** _ADAM_STEP)
    m2 = _ADAM_B1 * m + (1.0 - _ADAM_B1) * g
    v2 = _ADAM_B2 * v + (1.0 - _ADAM_B2) * (g * g)
    return -_ADAM_LR * ((m2 * c1) / (jnp.sqrt(v2 * c2) + _ADAM_EPS) + _ADAM_WD * w), m2, v2


def _adamw_big(recv, w, m, v, *, name):
    L, R, C = w.shape
    tr = _rows(R, 256)

    def body(r_ref, w_ref, m_ref, v_ref, g_ref, d_ref, mo_ref, vo_ref):
        g = r_ref[0, 0].astype(_F32)
        for k in range(1, _N_DEV):
            g = g + r_ref[0, k].astype(_F32)
        g_ref[0] = g
        d_ref[0], mo_ref[0], vo_ref[0] = _adam_update(g, w_ref[0], m_ref[0], v_ref[0])

    blk = pl.BlockSpec((1, tr, C), lambda l, i: (l, i, 0))
    o = jax.ShapeDtypeStruct((L, R, C), _F32)
    return pl.pallas_call(
        body, name=name, out_shape=(o, o, o, o), grid=(L, R // tr),
        in_specs=[pl.BlockSpec((1, _N_DEV, tr, C), lambda l, i: (l, 0, i, 0)), blk, blk, blk], out_specs=(blk, blk, blk, blk),
        compiler_params=_params("parallel", "parallel"),
    )(recv, w, m, v)


def _adamw_small(tensors, *, name):
    n = len(tensors)
    lanes = [t[4] for t in tensors]

    def body(*refs):
        ins, outs = refs[:4 * n], refs[4 * n:]
        for t in range(n):
            r_ref, w_ref, m_ref, v_ref = ins[4 * t:4 * t + 4]
            g_ref, d_ref, mo_ref, vo_ref = outs[4 * t:4 * t + 4]
            for l in range(w_ref.shape[0]):
                g = r_ref[l, 0]
                for k in range(1, _N_DEV):
                    g = g + r_ref[l, k]
                if lanes[t] is not None:
                    g = g[..., :lanes[t]]
                g_ref[l] = g
                d_ref[l], mo_ref[l], vo_ref[l] = _adam_update(g, w_ref[l], m_ref[l], v_ref[l])

    args, out_shape = [], []
    for recv, w, m, v, _ in tensors:
        args += [recv, w, m, v]
        out_shape += [jax.ShapeDtypeStruct(w.shape, _F32)] * 4
    outs = pl.pallas_call(
        body, name=name, out_shape=out_shape,
        in_specs=[pl.BlockSpec(memory_space=pltpu.VMEM)] * len(args), out_specs=[pl.BlockSpec(memory_space=pltpu.VMEM)] * len(out_shape),
        compiler_params=_params(),
    )(*args)
    return [tuple(outs[4 * t:4 * t + 4]) for t in range(n)]


_BIG = (("w_in_e", 2), ("w_out_e", 1), ("conv_w_in", 2), ("conv_w_out", 1), ("xa_wq", 1), ("xa_wkv", 2), ("xa_wo", 1),
        ("ffn_w_gu", 2), ("ffn_w_down", 1))
_SMALL_SHARDED = (("mix_norm_o", 1), ("conv_b_in", 1), ("conv_dw_w", 2), ("conv_dw_b", 1), ("conv_ln_g", 1),
                  ("conv_ln_b", 1), ("conv_b_out", 1))
_REPLICATED = ("mix_norm_e", "fox_f_bias", "gmlp_ln_g", "gmlp_ln_b", "gmlp_w_s", "gmlp_b_s", "xa_norm", "mem_norm",
               "ffn_norm", "final_norm")
_WEIGHTS = ("mix_norm_e", "w_in_e", "fox_f_bias", "gmlp_ln_g", "gmlp_ln_b", "gmlp_w_s", "gmlp_b_s", "w_out_e", "mix_norm_o",
            "conv_w_in", "conv_b_in", "conv_dw_w", "conv_dw_b", "conv_ln_g", "conv_ln_b", "conv_w_out", "conv_b_out",
            "xa_norm", "mem_norm", "xa_wq", "xa_wkv", "xa_wo", "ffn_norm", "ffn_w_gu", "ffn_w_down", "final_norm")


def _cols_to_peers(g, n=_N_DEV):
    K, N = g.shape[-2:]
    return jnp.swapaxes(g.reshape(g.shape[:-1] + (n, N // n)), -3, -2)


def _weight_items(names, w):
    return [([w[n][l] for l in range(w[n].shape[0])] if n in dict(_BIG) else [w[n]], False) for n in names]


def _place_weights(P, names, gathered):
    axis = dict(_BIG + _SMALL_SHARDED)
    for n, g in zip(names, gathered):
        if n in dict(_BIG):
            P[n] = g.reshape(g.shape[0], -1, g.shape[-1]) if axis[n] == 1 else _peers_to_cols(g)
        else:
            P[n] = _peers_to_cols(g[0, :, 0])[None] if axis[n] == 2 else g.reshape(1, -1)


def _grad_items(names, G):
    axis = dict(_BIG + _SMALL_SHARDED)
    items = []
    for n in names:
        if n in _REPLICATED:
            items.append((G[n], False))
        elif n == "ffn_w_gu":
            half = _N_DEV // 2
            items.append(([jnp.concatenate([_cols_to_peers(a, half), _cols_to_peers(b, half)], axis=0) for a, b in G[n]], True))
        elif n in dict(_BIG):
            items.append(([g.reshape(_N_DEV, -1, g.shape[-1]) if axis[n] == 1 else _cols_to_peers(g) for g in G[n]], True))
        else:
            g = G[n][0]
            items.append(([_cols_to_peers(g) if axis[n] == 2 else g.reshape(_N_DEV, 1, -1)], True))
    return items


def _peers_to_cols(d):
    K, c = d.shape[-2:]
    return jnp.swapaxes(d, -3, -2).reshape(d.shape[:-3] + (K, _N_DEV * c))


def _local_step(x, mem, tgt, P, gather_rest=None, scatter_main=None):
    B, T, D = x.shape
    M = mem.shape[1]
    N = B * T
    W = D // 2
    H = W // _FOX_HD
    f_blk = 5 * W // _LANES
    G = {}
    x0 = x.reshape(N, D)
    memf = mem.reshape(B * M, D)

    h_e = _rms_fwd(x0, P["mix_norm_e"][0], name="rms_mix_e")
    proj = _mm(h_e, P["w_in_pad"], bl=0, name="mm_in_e", tn=384)
    proj3 = proj.reshape(B, T, -1)
    cum = _fox_cum(proj3, f_blk, P["fox_f_bias"][0], name="fox_cum")
    o_fox, lse, got = _fox_fwd(proj3, cum, name="fox_fwd", xchg=gather_rest[0] if gather_rest else None)
    if gather_rest:
        gather_rest[1](P, got)
    bias_full = jnp.repeat(P["gmlp_b_s"][0].T, _GRP, axis=1)
    a_out = _gmlp_fwd(proj, P["gmlp_ln_g"][0], P["gmlp_ln_b"][0], P["gmlp_w_s"][0], bias_full, name="gmlp_fwd")
    mixcat = jnp.concatenate([o_fox.reshape(N, W), a_out], axis=1)
    x1 = _mm(mixcat, P["w_out_e"], bl=0, res=x0, name="mm_out_e")

    def xa_ffn_fwd(xin, l):
        s = {}
        s["h_xa"] = _rms_fwd(xin, P["xa_norm"][l], name=f"rms_xa{l}")
        s["q"] = _mm(s["h_xa"], P["xa_wq"], bl=l, out_dtype=_MXU, name=f"mm_q{l}")
        s["mn"] = _rms_fwd(memf, P["mem_norm"][l], name=f"rms_mem{l}")
        s["kv"] = _mm(s["mn"], P["xa_wkv"], bl=l, out_dtype=_MXU, name=f"mm_kv{l}")
        s["o"] = _xa_fwd(s["q"].reshape(B, T, D), s["kv"].reshape(B, M, 2 * D), name=f"xa_fwd{l}").reshape(N, D)
        s["x_mid"] = _mm(s["o"], P["xa_wo"], bl=l, res=xin, name=f"mm_o{l}")
        s["h_ffn"] = _rms_fwd(s["x_mid"], P["ffn_norm"][l], name=f"rms_ffn{l}")
        s["gu"], s["act"] = _mm_gu(s["h_ffn"], P["ffn_w_gu"], l, name=f"mm_gu{l}")
        s["x_in"] = xin
        xout = _mm(s["act"], P["ffn_w_down"], bl=l, res=s["x_mid"], name=f"mm_down{l}", tn=512)
        return xout, s

    x3, s0 = xa_ffn_fwd(x1, 0)
    h_o = _rms_fwd(x3, P["mix_norm_o"][0], name="rms_mix_o")
    ag = _mm(h_o, P["conv_w_in"], bl=0, bias=P["conv_b_in"][0], name="mm_conv_in")
    C = ag.shape[1] // 2
    y = _glu_fwd(ag, name="glu_fwd")
    dw_w = jnp.pad(P["conv_dw_w"][0], ((0, _HALO - _CONV_K), (0, 0)))
    y2, y4 = _conv_fwd(y.reshape(B, T, C), dw_w, P["conv_dw_b"][0], P["conv_ln_g"][0], P["conv_ln_b"][0], name="conv_fwd")
    x4 = _mm(y4.reshape(N, C), P["conv_w_out"], bl=0, bias=P["conv_b_out"][0], res=x3, name="mm_conv_out")
    x6, s1 = xa_ffn_fwd(x4, 1)
    loss, dx, dg = _final_loss(x6, P["final_norm"], tgt.reshape(N, D), name="final_loss")
    G["final_norm"] = [dg]

    def xa_ffn_bwd(dx, s, l):
        g = {}
        dgu = _mm_dgu(dx, P["ffn_w_down"], l, s["gu"], name=f"mm_dgu{l}")
        g["ffn_w_down"] = _mm(s["act"], dx, ta=True, out_dtype=_MXU, name=f"mm_dwdown{l}", tm=1408)
        g["ffn_w_gu"] = (_mm(s["h_ffn"], dgu, ta=True, bl=0, out_dtype=_MXU, name=f"mm_dwg{l}", tn=1408),
                         _mm(s["h_ffn"], dgu, ta=True, bl=1, out_dtype=_MXU, name=f"mm_dwu{l}", tn=1408))
        dh = _mm(dgu, P["ffn_w_gu"], al=0, bl=l, tb=True, name=f"mm_dhffn_g{l}", tk=1408)
        dh = _mm(dgu, P["ffn_w_gu"], al=1, bl=l, bk0=dgu.shape[-1], tb=True, res=dh, name=f"mm_dhffn_u{l}", tk=1408)
        dx, g["ffn_norm"] = _rms_bwd(s["x_mid"], P["ffn_norm"][l], dh, dx, name=f"rms_ffn_bwd{l}")
        do = _mm(dx, P["xa_wo"], bl=l, tb=True, out_dtype=_MXU, name=f"mm_do{l}")
        g["xa_wo"] = _mm(s["o"], dx, ta=True, out_dtype=_MXU, name=f"mm_dwo{l}")
        dq, dkv = _xa_bwd(s["q"].reshape(B, T, D), s["kv"].reshape(B, M, 2 * D), do.reshape(B, T, D), name=f"xa_bwd{l}")
        dq, dkv = dq.reshape(N, D), dkv.reshape(B * M, 2 * D)
        g["xa_wq"] = _mm(s["h_xa"], dq, ta=True, out_dtype=_MXU, name=f"mm_dwq{l}")
        dh = _mm(dq, P["xa_wq"], bl=l, tb=True, name=f"mm_dhxa{l}")
        g["xa_wkv"] = _mm(s["mn"], dkv, ta=True, out_dtype=_MXU, name=f"mm_dwkv{l}")
        dmn = _mm(dkv, P["xa_wkv"], bl=l, tb=True, name=f"mm_dmn{l}")
        g["mem_norm"] = _rms_bwd(memf, P["mem_norm"][l], dmn, None, name=f"rms_mem_bwd{l}")
        dx, g["xa_norm"] = _rms_bwd(s["x_in"], P["xa_norm"][l], dh, dx, name=f"rms_xa_bwd{l}")
        return dx, g

    dx, g1 = xa_ffn_bwd(dx, s1, 1)
    G["conv_b_out"] = [_colsum(dx, name="colsum_b_out")]
    dy4 = _mm(dx, P["conv_w_out"], bl=0, tb=True, name="mm_dy4")
    G["conv_w_out"] = [_mm(y4.reshape(N, C), dx, ta=True, out_dtype=_MXU, name="mm_dwconv_out")]
    dy2, dlg, dlb, ddb = _conv_ln_bwd(y2.reshape(N, C), dy4, P["conv_ln_g"][0], P["conv_ln_b"][0], name="conv_ln_bwd")
    G["conv_ln_g"], G["conv_ln_b"], G["conv_dw_b"] = [dlg], [dlb], [ddb]
    dag, ddw, dbin = _conv_bwd(y.reshape(B, T, C), dy2.reshape(B, T, C), ag.reshape(B, T, 2 * C), dw_w, name="conv_bwd")
    G["conv_dw_w"], G["conv_b_in"] = [ddw[:_CONV_K]], [dbin]
    dag = dag.reshape(N, 2 * C)
    G["conv_w_in"] = [_mm(h_o, dag, ta=True, out_dtype=_MXU, name="mm_dwconv_in")]
    dh = _mm(dag, P["conv_w_in"], bl=0, tb=True, name="mm_dh_o")
    dx, dg = _rms_bwd(x3, P["mix_norm_o"][0], dh, dx, name="rms_mix_o_bwd")
    G["mix_norm_o"] = [dg]
    dx, g0 = xa_ffn_bwd(dx, s0, 0)
    for k in g0:
        G[k] = [g0[k], g1[k]]
    G["w_out_e"] = [_mm(mixcat, dx, ta=True, out_dtype=_MXU, name="mm_dwout_e")]
    dmix = _mm(dx, P["w_out_e"], bl=0, tb=True, name="mm_dmix")
    dz, dlg, dlb, dws, dbias = _gmlp_bwd(proj, dmix, 1, P["gmlp_ln_g"][0], P["gmlp_ln_b"][0], P["gmlp_w_s"][0], bias_full,
                                         name="gmlp_bwd")
    G["gmlp_ln_g"], G["gmlp_ln_b"], G["gmlp_w_s"] = [dlg], [dlb], [dws]
    G["gmlp_b_s"] = [dbias[:, :2 * (W // _LANES)].T]
    dmix3 = dmix.reshape(B, T, D)
    dq, dk, dv, dcum, sent = _fox_bwd(proj3, cum, dmix3, lse, name="fox_bwd", xchg=scatter_main(G) if scatter_main else None)
    df, dfb = _fox_cum_bwd(proj3, f_blk, P["fox_f_bias"][0], dcum.reshape(B, H, T), name="fox_cum_bwd")
    G["fox_f_bias"] = [dfb]
    dproj = jnp.concatenate([dq.reshape(N, W), dk.reshape(N, W), dv.reshape(N, W), dz, df.reshape(N, _LANES).astype(_MXU)], axis=1)
    G["w_in_pad"] = [_mm(h_e, dproj, ta=True, out_dtype=_MXU, name="mm_dwin_e", tn=384)]
    dh = _mm(dproj, P["w_in_pad"], bl=0, tb=True, name="mm_dh_e", tk=384)
    dx, dg = _rms_bwd(x0, P["mix_norm_e"][0], dh, dx, name="rms_mix_e_bwd")
    G["mix_norm_e"] = [dg]
    return loss, dx.reshape(B, T, D), G, sent


def _pad_w_in(w_in, W, H):
    f = w_in[:, 3 * W:3 * W + H]
    return jnp.concatenate([w_in[:, :3 * W], w_in[:, 3 * W + H:], jnp.pad(f, ((0, 0), (0, _LANES - H)))], axis=1)


def _unpad_w_in(g, W, H):
    return jnp.concatenate([g[:, :3 * W], g[:, 5 * W:5 * W + H], g[:, 3 * W:5 * W]], axis=1)


def kernel(x, mem, mix_norm_e, w_in_e, fox_f_bias, gmlp_ln_g, gmlp_ln_b, gmlp_w_s, gmlp_b_s, w_out_e, mix_norm_o, conv_w_in, conv_b_in, conv_dw_w, conv_dw_b, conv_ln_g, conv_ln_b, conv_w_out, conv_b_out, xa_norm, mem_norm, xa_wq, xa_wkv, xa_wo, ffn_norm, ffn_w_gu, ffn_w_down, final_norm, loss_target, m_mix_norm_e, m_w_in_e, m_fox_f_bias, m_gmlp_ln_g, m_gmlp_ln_b, m_gmlp_w_s, m_gmlp_b_s, m_w_out_e, m_mix_norm_o, m_conv_w_in, m_conv_b_in, m_conv_dw_w, m_conv_dw_b, m_conv_ln_g, m_conv_ln_b, m_conv_w_out, m_conv_b_out, m_xa_norm, m_mem_norm, m_xa_wq, m_xa_wkv, m_xa_wo, m_ffn_norm, m_ffn_w_gu, m_ffn_w_down, m_final_norm, v_mix_norm_e, v_w_in_e, v_fox_f_bias, v_gmlp_ln_g, v_gmlp_ln_b, v_gmlp_w_s, v_gmlp_b_s, v_w_out_e, v_mix_norm_o, v_conv_w_in, v_conv_b_in, v_conv_dw_w, v_conv_dw_b, v_conv_ln_g, v_conv_ln_b, v_conv_w_out, v_conv_b_out, v_xa_norm, v_mem_norm, v_xa_wq, v_xa_wkv, v_xa_wo, v_ffn_norm, v_ffn_w_gu, v_ffn_w_down, v_final_norm):
    env = dict(locals())
    w = {n: env[n] for n in _WEIGHTS}
    mom = {n: env["m_" + n] for n in _WEIGHTS}
    var = {n: env["v_" + n] for n in _WEIGHTS}
    D = x.shape[-1]
    W = D // 2
    H = W // _FOX_HD

    wsrc = {n: (w[n].astype(_MXU) if n in dict(_BIG) else w[n]) for n, _ in _BIG + _SMALL_SHARDED}
    P = {n: w[n] for n in _REPLICATED}
    _place_weights(P, ["w_in_e"], _exchange(_weight_items(["w_in_e"], wsrc), name="gather_first"))
    P["w_in_pad"] = _pad_w_in(P.pop("w_in_e")[0], W, H)[None]
    rest = [n for n, _ in _BIG + _SMALL_SHARDED if n != "w_in_e"]
    gather_rest = (_Exchange(_weight_items(rest, wsrc)), lambda P, got: _place_weights(P, rest, got))

    late = ["w_in_e", "mix_norm_e", "fox_f_bias"]
    early = [n for n, _ in _BIG + _SMALL_SHARDED if n not in late] + [n for n in _REPLICATED if n not in late]
    loss, grad_x, G, sent = _local_step(x, mem, loss_target, P, gather_rest, lambda G: _Exchange(_grad_items(early, G)))
    loss = lax.psum(loss[0, 0], ("x", "y", "c"))
    G["w_in_e"] = [_unpad_w_in(G.pop("w_in_pad")[0], W, H)]
    recv = dict(zip(early, sent))
    recv.update(zip(late, _exchange(_grad_items(late, G), name="scatter_last")))

    res = {n: _adamw_big(recv[n], w[n], mom[n], var[n], name="adamw_" + n) for n, _ in _BIG}
    small = [n for n, _ in _SMALL_SHARDED] + list(_REPLICATED)

    def rows(a, r):
        return a.reshape((r.shape[0],) + r.shape[2:-1] + (-1,))

    outs = _adamw_small([(recv[n], rows(w[n], recv[n]), rows(mom[n], recv[n]), rows(var[n], recv[n]),
                          w[n].shape[-1] if w[n].shape[-1] != recv[n].shape[-1] else None) for n in small], name="adamw_small")
    for n, o in zip(small, outs):
        res[n] = tuple(a.reshape(w[n].shape) for a in o)
    return (loss, grad_x, *[res[n][0] for n in _WEIGHTS], *[res[n][1] for n in _WEIGHTS],
            *[res[n][2] for n in _WEIGHTS], *[res[n][3] for n in _WEIGHTS])
```

```python
import functools
import math

import jax
import jax.numpy as jnp
from jax import lax
from jax.experimental import pallas as pl
from jax.experimental.pallas import tpu as pltpu

_F32 = jnp.float32
_MXU = jnp.bfloat16
_VMEM_LIMIT = 48 * 1024 * 1024
_LANES = 128
_EPS = 1e-6
_N_DEV = 8
_FOX_HD = 64
_FOX_SCALE = _FOX_HD ** -0.5
_FOX_TQ = 512
_CHUNK = 128
_GRP = 64
_CONV_K = 31
_HALO = 32
_XA_HEADS = 4
_GELU_C = math.sqrt(2.0 / math.pi)
_ADAM_LR, _ADAM_B1, _ADAM_B2, _ADAM_EPS, _ADAM_WD, _ADAM_STEP = 0.001, 0.9, 0.999, 1e-08, 0.01, 10
_FLAT_W = 1024
_FLAT_ALIGN = 16 * _FLAT_W
_BIG_ROWS = 128


def _params(*sem):
    return pltpu.CompilerParams(dimension_semantics=sem if sem else None, vmem_limit_bytes=_VMEM_LIMIT)


def _pick(n, pref):
    if n <= pref:
        return n
    best = None
    for t in range(_LANES, pref + 1, _LANES):
        if n % t == 0:
            best = t
    assert best is not None, (n, pref)
    return best


def _rows(n, pref):
    if n <= pref:
        return n
    t = pref
    while n % t:
        t //= 2
    assert t >= 8, (n, pref)
    return t


def _sigmoid(x):
    return 1.0 / (1.0 + jnp.exp(-x))


def _gelu(x):
    t = jnp.tanh(_GELU_C * (x + 0.044715 * (x * x * x)))
    return 0.5 * x * (1.0 + t)


def _gelu_grad(x):
    x2 = x * x
    t = jnp.tanh(_GELU_C * (x + 0.044715 * (x2 * x)))
    return 0.5 * (1.0 + t) + 0.5 * x * (1.0 - t * t) * (_GELU_C * (1.0 + 3.0 * 0.044715 * x2))


def _dot(a, b, ca, cb):
    return lax.dot_general(a, b, (((ca,), (cb,)), ((), ())), preferred_element_type=_F32)


def _mm(a, b, *, name, ta=False, tb=False, al=None, bl=None, bk0=0, bias=None, res=None, out_dtype=_F32, tm=1024, tn=512, tk=1024,
        xchg=None):
    if ta:
        K, M = a.shape[-2:]
    else:
        M, K = a.shape[-2:]
    if tb:
        N, K2 = b.shape[-2:]
    else:
        K2, N = b.shape[-2:]
    assert K == K2 or (tb and K2 > K), (a.shape, b.shape, ta, tb)
    tm, tn = _pick(M, tm), _pick(N, tn)
    tk = K if (not ta and K <= 2816 and K2 == K) else _pick(K, tk)
    nk = K // tk
    assert bk0 % tk == 0
    kb = bk0 // tk
    grid = (M // tm, N // tn, nk)
    if a.ndim == 3:
        a_spec = (pl.BlockSpec((None, tk, tm), lambda i, j, k: (al, k, i)) if ta
                  else pl.BlockSpec((None, tm, tk), lambda i, j, k: (al, i, k)))
    else:
        a_spec = pl.BlockSpec((tk, tm), lambda i, j, k: (k, i)) if ta else pl.BlockSpec((tm, tk), lambda i, j, k: (i, k))
    if b.ndim == 3:
        b_spec = (pl.BlockSpec((None, tn, tk), lambda i, j, k: (bl, j, k + kb)) if tb
                  else pl.BlockSpec((None, tk, tn), lambda i, j, k: (bl, k, j)))
    else:
        b_spec = pl.BlockSpec((tn, tk), lambda i, j, k: (j, k)) if tb else pl.BlockSpec((tk, tn), lambda i, j, k: (k, j))
    in_specs, args = [a_spec, b_spec], [a, b]
    if bias is not None:
        in_specs.append(pl.BlockSpec((1, tn), lambda i, j, k: (0, j)))
        args.append(bias.reshape(1, N).astype(_F32))
    if res is not None:
        in_specs.append(pl.BlockSpec((tm, tn), lambda i, j, k: (i, j)))
        args.append(res)
    has_bias, has_res = bias is not None, res is not None

    def body(*refs):
        a_ref, b_ref = refs[0], refs[1]
        pos = 2
        bias_ref = res_ref = None
        if has_bias:
            bias_ref = refs[pos]
            pos += 1
        if has_res:
            res_ref = refs[pos]
            pos += 1
        o_ref = refs[pos]
        acc_ref = refs[pos + 1] if nk > 1 else None
        p = _dot(a_ref[...].astype(_MXU), b_ref[...].astype(_MXU), 0 if ta else 1, 1 if tb else 0)

        def finish(acc):
            if has_bias:
                acc = acc + bias_ref[...]
            if has_res:
                acc = acc + res_ref[...]
            o_ref[...] = acc.astype(o_ref.dtype)

        if nk == 1:
            finish(p)
        else:
            k = pl.program_id(2)

            @pl.when(k == 0)
            def _():
                acc_ref[...] = p

            @pl.when(k > 0)
            def _():
                acc_ref[...] += p

            @pl.when(k == nk - 1)
            def _():
                finish(acc_ref[...])

    (out,), got = _fused_call(
        body, name=name, out_shape=[jax.ShapeDtypeStruct((M, N), out_dtype)], grid=grid, in_specs=in_specs,
        out_specs=[pl.BlockSpec((tm, tn), lambda i, j, k: (i, j))],
        scratch_shapes=[pltpu.VMEM((tm, tn), _F32)] if nk > 1 else [],
        sem=("parallel", "parallel", "arbitrary"), args=args, xchg=xchg)
    return out if xchg is None else (out, got)


def _rms_fwd(x, g, *, name):
    N, D = x.shape
    tr = _rows(N, 512)

    def body(x_ref, g_ref, o_ref):
        xv = x_ref[...]
        r = lax.rsqrt(jnp.mean(xv * xv, axis=-1, keepdims=True) + _EPS)
        o_ref[...] = (xv * r * g_ref[...]).astype(o_ref.dtype)

    return pl.pallas_call(
        body,
        name=name,
        out_shape=jax.ShapeDtypeStruct((N, D), _MXU),
        grid=(N // tr,),
        in_specs=[pl.BlockSpec((tr, D), lambda i: (i, 0)), pl.BlockSpec((1, D), lambda i: (0, 0))],
        out_specs=pl.BlockSpec((tr, D), lambda i: (i, 0)),
        compiler_params=_params("parallel"),
    )(x, g.reshape(1, D))


def _rms_bwd(x, g, dh, dres, *, name):
    N, D = x.shape
    tr = _rows(N, 256)
    has_res = dres is not None

    def body(*refs):
        if has_res:
            x_ref, g_ref, dh_ref, dres_ref, dx_ref, dg_ref = refs
        else:
            x_ref, g_ref, dh_ref, dg_ref = refs
        xv = x_ref[...]
        r = lax.rsqrt(jnp.mean(xv * xv, axis=-1, keepdims=True) + _EPS)
        xh = xv * r
        dhv = dh_ref[...].astype(_F32)

        @pl.when(pl.program_id(0) == 0)
        def _():
            dg_ref[...] = jnp.zeros_like(dg_ref)

        dg_ref[...] += jnp.sum(dhv * xh, axis=0, keepdims=True)
        if has_res:
            dxn = dhv * g_ref[...]
            dx = r * (dxn - xh * jnp.mean(dxn * xh, axis=-1, keepdims=True))
            dx_ref[...] = dres_ref[...] + dx

    row = pl.BlockSpec((tr, D), lambda i: (i, 0))
    vec = pl.BlockSpec((1, D), lambda i: (0, 0))
    if has_res:
        out_shape = (jax.ShapeDtypeStruct((N, D), _F32), jax.ShapeDtypeStruct((1, D), _F32))
        out_specs = (row, vec)
        in_specs, args = [row, vec, row, row], (x, g.reshape(1, D), dh, dres)
    else:
        out_shape = jax.ShapeDtypeStruct((1, D), _F32)
        out_specs = vec
        in_specs, args = [row, vec, row], (x, g.reshape(1, D), dh)
    return pl.pallas_call(
        body, name=name, out_shape=out_shape, grid=(N // tr,), in_specs=in_specs, out_specs=out_specs,
        compiler_params=_params("arbitrary"),
    )(*args)


def _colsum(a, *, name):
    M, C = a.shape
    tr = _rows(M, 512)

    def body(a_ref, o_ref):
        @pl.when(pl.program_id(0) == 0)
        def _():
            o_ref[...] = jnp.zeros_like(o_ref)

        o_ref[...] += jnp.sum(a_ref[...].astype(_F32), axis=0, keepdims=True)

    return pl.pallas_call(
        body, name=name, out_shape=jax.ShapeDtypeStruct((1, C), _F32), grid=(M // tr,),
        in_specs=[pl.BlockSpec((tr, C), lambda i: (i, 0))], out_specs=pl.BlockSpec((1, C), lambda i: (0, 0)),
        compiler_params=_params("arbitrary"),
    )(a)


def _final_loss(x, g, tgt, *, name):
    N, D = x.shape
    tr = _rows(N, 256)

    def body(x_ref, g_ref, t_ref, loss_ref, dx_ref, dg_ref):
        xv = x_ref[...]
        r = lax.rsqrt(jnp.mean(xv * xv, axis=-1, keepdims=True) + _EPS)
        xh = xv * r
        gv = g_ref[...]
        diff = xh * gv - t_ref[...]

        @pl.when(pl.program_id(0) == 0)
        def _():
            loss_ref[...] = jnp.zeros_like(loss_ref)
            dg_ref[...] = jnp.zeros_like(dg_ref)

        part = jnp.sum(jnp.sum(diff * diff, axis=1, keepdims=True), axis=0, keepdims=True) * (0.5 / D)
        loss_ref[...] += jnp.broadcast_to(part, loss_ref.shape)
        dy = diff * (1.0 / D)
        dg_ref[...] += jnp.sum(dy * xh, axis=0, keepdims=True)
        dxn = dy * gv
        dx_ref[...] = r * (dxn - xh * jnp.mean(dxn * xh, axis=-1, keepdims=True))

    row = pl.BlockSpec((tr, D), lambda i: (i, 0))
    vec = pl.BlockSpec((1, D), lambda i: (0, 0))
    return pl.pallas_call(
        body, name=name,
        out_shape=(jax.ShapeDtypeStruct((8, _LANES), _F32), jax.ShapeDtypeStruct((N, D), _F32), jax.ShapeDtypeStruct((1, D), _F32)),
        grid=(N // tr,), in_specs=[row, vec, row],
        out_specs=(pl.BlockSpec((8, _LANES), lambda i: (0, 0)), row, vec),
        compiler_params=_params("arbitrary"),
    )(x, g.reshape(1, D), tgt)


def _mm_gu(h, w_gu, l, *, name, tm=512, tn=1408, xchg=None):
    N, K = h.shape
    H = w_gu.shape[-1] // 2
    tm, tn = _pick(N, tm), _pick(H, tn)
    nj = H // tn

    def body(h_ref, wg_ref, wu_ref, gu_ref, act_ref):
        hv = h_ref[...].astype(_MXU)
        g = _dot(hv, wg_ref[...].astype(_MXU), 1, 0)
        u = _dot(hv, wu_ref[...].astype(_MXU), 1, 0)
        gu_ref[0] = g.astype(gu_ref.dtype)
        gu_ref[1] = u.astype(gu_ref.dtype)
        act_ref[...] = (g * _sigmoid(g) * u).astype(act_ref.dtype)

    (gu, act), got = _fused_call(
        body, name=name,
        out_shape=(jax.ShapeDtypeStruct((2, N, H), _MXU), jax.ShapeDtypeStruct((N, H), _MXU)), grid=(N // tm, nj),
        in_specs=[pl.BlockSpec((tm, K), lambda i, j: (i, 0)), pl.BlockSpec((None, K, tn), lambda i, j: (l, 0, j)),
                  pl.BlockSpec((None, K, tn), lambda i, j: (l, 0, j + nj))],
        out_specs=(pl.BlockSpec((2, tm, tn), lambda i, j: (0, i, j)), pl.BlockSpec((tm, tn), lambda i, j: (i, j))),
        scratch_shapes=[], sem=("parallel", "parallel"), args=(h, w_gu, w_gu), xchg=xchg)
    return gu, act, got


def _mm_dgu(dx, w_down, l, gu, *, name, tm=512, tn=1408):
    N, K = dx.shape
    H = w_down.shape[-2]
    tm, tn = _pick(N, tm), _pick(H, tn)

    def body(dx_ref, w_ref, gu_ref, o_ref):
        d = _dot(dx_ref[...].astype(_MXU), w_ref[...].astype(_MXU), 1, 1)
        g, u = gu_ref[0].astype(_F32), gu_ref[1].astype(_F32)
        sg = _sigmoid(g)
        o_ref[0] = (d * u * (sg * (1.0 + g * (1.0 - sg)))).astype(o_ref.dtype)
        o_ref[1] = (d * (g * sg)).astype(o_ref.dtype)

    return pl.pallas_call(
        body, name=name, out_shape=jax.ShapeDtypeStruct((2, N, H), _MXU), grid=(N // tm, H // tn),
        in_specs=[pl.BlockSpec((tm, K), lambda i, j: (i, 0)), pl.BlockSpec((None, tn, K), lambda i, j: (l, j, 0)),
                  pl.BlockSpec((2, tm, tn), lambda i, j: (0, i, j))],
        out_specs=pl.BlockSpec((2, tm, tn), lambda i, j: (0, i, j)), compiler_params=_params("parallel", "parallel"),
    )(dx, w_down, gu)


def _gmlp_mix(vb, w, trans):
    tr, W = vb.shape
    lane = lax.broadcasted_iota(jnp.int32, (_CHUNK, _LANES), 1)
    rows = []
    for c in range(tr // _CHUNK):
        tiles = []
        for j in range(W // _LANES):
            t = vb[c * _CHUNK:(c + 1) * _CHUNK, j * _LANES:(j + 1) * _LANES]
            ma = _dot(w[2 * j], t, 0 if trans else 1, 0)
            mb = _dot(w[2 * j + 1], t, 0 if trans else 1, 0)
            tiles.append(jnp.where(lane < _GRP, ma, mb))
        rows.append(jnp.concatenate(tiles, axis=1))
    return jnp.concatenate(rows, axis=0)


def _tril_w(w_ref):
    r = lax.broadcasted_iota(jnp.int32, (_CHUNK, _CHUNK), 0)
    c = lax.broadcasted_iota(jnp.int32, (_CHUNK, _CHUNK), 1)
    return jnp.where((r >= c)[None], w_ref[...], 0.0).astype(_MXU)


def _layernorm_stats(v):
    mu = jnp.mean(v, axis=-1, keepdims=True)
    xc = v - mu
    rstd = lax.rsqrt(jnp.mean(xc * xc, axis=-1, keepdims=True) + _EPS)
    return xc * rstd, rstd


def _gmlp_fwd(proj, ln_g, ln_b, w_s, bias_full, *, name):
    N = proj.shape[0]
    W = ln_g.shape[-1]
    G = w_s.shape[0]
    tr = _rows(N, 512)
    ub, vb_ = 3, 4

    def body(u_ref, v_ref, g_ref, b_ref, w_ref, bias_ref, o_ref):
        u = _gelu(u_ref[...])
        xh, _ = _layernorm_stats(_gelu(v_ref[...]))
        vgn = xh * g_ref[...] + b_ref[...]
        mixed = _gmlp_mix(vgn.astype(_MXU), _tril_w(w_ref), False)
        bias = jnp.concatenate([bias_ref[...]] * (tr // _CHUNK), axis=0)
        o_ref[...] = (u * (mixed + bias)).astype(o_ref.dtype)

    vec = pl.BlockSpec((1, W), lambda i: (0, 0))
    return pl.pallas_call(
        body, name=name, out_shape=jax.ShapeDtypeStruct((N, W), _MXU), grid=(N // tr,),
        in_specs=[pl.BlockSpec((tr, W), lambda i: (i, ub)), pl.BlockSpec((tr, W), lambda i: (i, vb_)), vec, vec,
                  pl.BlockSpec((G, _CHUNK, _CHUNK), lambda i: (0, 0, 0)), pl.BlockSpec((_CHUNK, W), lambda i: (0, 0))],
        out_specs=pl.BlockSpec((tr, W), lambda i: (i, 0)), compiler_params=_params("parallel"),
    )(proj, proj, ln_g.reshape(1, W), ln_b.reshape(1, W), w_s, bias_full)


def _gmlp_bwd(proj, da_src, da_blk, ln_g, ln_b, w_s, bias_full, *, name):
    N = proj.shape[0]
    W = ln_g.shape[-1]
    G = w_s.shape[0]
    tr = _rows(N, 512)
    nch = tr // _CHUNK

    def body(u_ref, v_ref, da_ref, g_ref, b_ref, w_ref, bias_ref, dz_ref, dg_ref, db_ref, dw_ref, dbias_ref):
        @pl.when(pl.program_id(0) == 0)
        def _():
            dg_ref[...] = jnp.zeros_like(dg_ref)
            db_ref[...] = jnp.zeros_like(db_ref)
            dw_ref[...] = jnp.zeros_like(dw_ref)
            dbias_ref[...] = jnp.zeros_like(dbias_ref)

        u_pre, v_pre = u_ref[...], v_ref[...]
        ug = _gelu(u_pre)
        xh, rstd = _layernorm_stats(_gelu(v_pre))
        lg = g_ref[...]
        vgn = xh * lg + b_ref[...]
        vb = vgn.astype(_MXU)
        wt = _tril_w(w_ref)
        mixed = _gmlp_mix(vb, wt, False)
        bias = jnp.concatenate([bias_ref[...]] * nch, axis=0)
        da = da_ref[...].astype(_F32)
        du = da * (mixed + bias)
        dm = da * ug
        dmb = dm.astype(_MXU)
        lane = lax.broadcasted_iota(jnp.int32, (_CHUNK, _LANES), 1)
        r = lax.broadcasted_iota(jnp.int32, (_CHUNK, _CHUNK), 0)
        c = lax.broadcasted_iota(jnp.int32, (_CHUNK, _CHUNK), 1)
        tril = r >= c
        dmsum = dm[0:_CHUNK]
        for ch in range(1, nch):
            dmsum = dmsum + dm[ch * _CHUNK:(ch + 1) * _CHUNK]
        dbias = jnp.zeros((_CHUNK, _LANES), _F32)
        for j in range(W // _LANES):
            tile = dmsum[:, j * _LANES:(j + 1) * _LANES]
            sa = jnp.sum(jnp.where(lane < _GRP, tile, 0.0), axis=1, keepdims=True)
            sb = jnp.sum(jnp.where(lane >= _GRP, tile, 0.0), axis=1, keepdims=True)
            dbias = dbias + jnp.where(lane == 2 * j, sa, 0.0) + jnp.where(lane == 2 * j + 1, sb, 0.0)
            acc_a = jnp.zeros((_CHUNK, _CHUNK), _F32)
            acc_b = jnp.zeros((_CHUNK, _CHUNK), _F32)
            for ch in range(nch):
                dt = dmb[ch * _CHUNK:(ch + 1) * _CHUNK, j * _LANES:(j + 1) * _LANES]
                vt = vb[ch * _CHUNK:(ch + 1) * _CHUNK, j * _LANES:(j + 1) * _LANES]
                acc_a = acc_a + _dot(jnp.where(lane < _GRP, dt, jnp.zeros_like(dt)), vt, 1, 1)
                acc_b = acc_b + _dot(jnp.where(lane >= _GRP, dt, jnp.zeros_like(dt)), vt, 1, 1)
            dw_ref[2 * j] += jnp.where(tril, acc_a, 0.0)
            dw_ref[2 * j + 1] += jnp.where(tril, acc_b, 0.0)
        dbias_ref[...] += dbias
        dvgn = _gmlp_mix(dmb, wt, True)
        dg_ref[...] += jnp.sum(dvgn * xh, axis=0, keepdims=True)
        db_ref[...] += jnp.sum(dvgn, axis=0, keepdims=True)
        dxh = dvgn * lg
        dvg = rstd * (dxh - jnp.mean(dxh, axis=-1, keepdims=True) - xh * jnp.mean(dxh * xh, axis=-1, keepdims=True))
        dz_ref[:, :W] = (du * _gelu_grad(u_pre)).astype(dz_ref.dtype)
        dz_ref[:, W:] = (dvg * _gelu_grad(v_pre)).astype(dz_ref.dtype)

    vec = pl.BlockSpec((1, W), lambda i: (0, 0))
    wspec = pl.BlockSpec((G, _CHUNK, _CHUNK), lambda i: (0, 0, 0))
    return pl.pallas_call(
        body, name=name,
        out_shape=(jax.ShapeDtypeStruct((N, 2 * W), _MXU), jax.ShapeDtypeStruct((1, W), _F32), jax.ShapeDtypeStruct((1, W), _F32),
                   jax.ShapeDtypeStruct((G, _CHUNK, _CHUNK), _F32), jax.ShapeDtypeStruct((_CHUNK, _LANES), _F32)),
        grid=(N // tr,),
        in_specs=[pl.BlockSpec((tr, W), lambda i: (i, 3)), pl.BlockSpec((tr, W), lambda i: (i, 4)),
                  pl.BlockSpec((tr, W), lambda i: (i, da_blk)), vec, vec, wspec, pl.BlockSpec((_CHUNK, W), lambda i: (0, 0))],
        out_specs=(pl.BlockSpec((tr, 2 * W), lambda i: (i, 0)), vec, vec, wspec, pl.BlockSpec((_CHUNK, _LANES), lambda i: (0, 0))),
        compiler_params=_params("arbitrary"),
    )(proj, proj, da_src, ln_g.reshape(1, W), ln_b.reshape(1, W), w_s, bias_full)


def _lane_cumsum(v):
    T = v.shape[1]
    lane = lax.broadcasted_iota(jnp.int32, (8, _LANES), 1)
    carry = jnp.zeros((8, 1), _F32)
    out = []
    for ch in range(T // _LANES):
        blk = v[:, ch * _LANES:(ch + 1) * _LANES]
        sh = 1
        while sh < _LANES:
            blk = blk + jnp.where(lane >= sh, pltpu.roll(blk, sh, 1), 0.0)
            sh *= 2
        blk = blk + carry
        carry = blk[:, _LANES - 1:_LANES]
        out.append(blk)
    return jnp.concatenate(out, axis=1), carry


def _log_sigmoid(x):
    return jnp.minimum(x, 0.0) - jnp.log(1.0 + jnp.exp(-jnp.abs(x)))


def _fox_cum(proj3, f_blk, f_bias, *, name):
    B, T, _ = proj3.shape
    H = f_bias.shape[-1]
    assert H == 8

    def body(f_ref, b_ref, o_ref):
        x = f_ref[0].T[0:8, :] + b_ref[...]
        cum, _ = _lane_cumsum(_log_sigmoid(x))
        o_ref[0] = cum

    return pl.pallas_call(
        body, name=name, out_shape=jax.ShapeDtypeStruct((B, 8, T), _F32), grid=(B,),
        in_specs=[pl.BlockSpec((1, T, _LANES), lambda b: (b, 0, f_blk)), pl.BlockSpec((8, 1), lambda b: (0, 0))],
        out_specs=pl.BlockSpec((1, 8, T), lambda b: (b, 0, 0)), compiler_params=_params("parallel"),
    )(proj3, f_bias.reshape(8, 1))


def _fox_cum_bwd(proj3, f_blk, f_bias, dcum, *, name):
    B, T, _ = proj3.shape

    def body(f_ref, b_ref, dc_ref, df_ref, dbias_ref):
        @pl.when(pl.program_id(0) == 0)
        def _():
            dbias_ref[...] = jnp.zeros_like(dbias_ref)

        x = f_ref[0].T[0:8, :] + b_ref[...]
        dc = dc_ref[0]
        incl, total = _lane_cumsum(dc)
        dlf = total - incl + dc
        df = dlf * _sigmoid(-x)
        full = jnp.concatenate([df, jnp.zeros((_LANES - 8, T), _F32)], axis=0).T
        dbias_ref[...] += jnp.sum(full, axis=0, keepdims=True)
        df_ref[0] = full

    return pl.pallas_call(
        body, name=name,
        out_shape=(jax.ShapeDtypeStruct((B, T, _LANES), _F32), jax.ShapeDtypeStruct((1, _LANES), _F32)), grid=(B,),
        in_specs=[pl.BlockSpec((1, T, _LANES), lambda b: (b, 0, f_blk)), pl.BlockSpec((8, 1), lambda b: (0, 0)),
                  pl.BlockSpec((1, 8, T), lambda b: (b, 0, 0))],
        out_specs=(pl.BlockSpec((1, T, _LANES), lambda b: (b, 0, 0)), pl.BlockSpec((1, _LANES), lambda b: (0, 0))),
        compiler_params=_params("arbitrary"),
    )(proj3, f_bias.reshape(8, 1), dcum)


def _cum_row(cum_ref, h, start, size):
    blk = cum_ref[0, :, pl.ds(start, size)]
    sub = lax.broadcasted_iota(jnp.int32, (blk.shape[0], 1), 0)
    return jnp.sum(jnp.where(sub == h, blk, 0.0), axis=0, keepdims=True)


def _causal(tq, q0, k0):
    r = lax.broadcasted_iota(jnp.int32, (tq, tq), 0)
    c = lax.broadcasted_iota(jnp.int32, (tq, tq), 1)
    return (r + q0) >= (c + k0)


def _fused_call(body, *, name, out_shape, grid, in_specs, out_specs, scratch_shapes, sem, args, xchg):
    out_shape, in_specs, out_specs, scratch_shapes = list(out_shape), list(in_specs), list(out_specs), list(scratch_shapes)
    if xchg is None:
        res = pl.pallas_call(body, name=name, out_shape=out_shape, grid=grid, in_specs=in_specs, out_specs=out_specs,
                             scratch_shapes=scratch_shapes, compiler_params=_params(*sem))(*args)
        return list(res), []
    n_in, n_out, n_scr = len(in_specs), len(out_specs), len(scratch_shapes)

    def fused(*refs):
        ins, refs = refs[:n_in], refs[n_in:]
        xs, refs = refs[:xchg.n_src], refs[xchg.n_src:]
        outs, refs = refs[:n_out], refs[n_out:]
        xd, refs = refs[:xchg.n_dst], refs[xchg.n_dst:]
        scr, sems = refs[:n_scr], refs[n_scr:]
        first = last = None
        for d, g in enumerate(grid):
            i = pl.program_id(d)
            first = (i == 0) if first is None else first & (i == 0)
            last = (i == g - 1) if last is None else last & (i == g - 1)

        @pl.when(first)
        def _():
            xchg.start(xs, xd, sems)

        body(*ins, *outs, *scr)

        @pl.when(last)
        def _():
            xchg.finish(xs, xd, sems)

    res = pl.pallas_call(
        fused, name=name, out_shape=out_shape + xchg.out_shapes, grid=grid, in_specs=in_specs + xchg.in_specs,
        out_specs=out_specs + xchg.out_specs, scratch_shapes=scratch_shapes + xchg.scratch,
        compiler_params=_params(*["arbitrary"] * len(grid)),
    )(*args, *xchg.srcs)
    return list(res[:n_out]), list(res[n_out:])


def _fox_fwd(proj3, cum, *, name, xchg=None):
    B, T, _ = proj3.shape
    H = cum.shape[1]
    W = H * _FOX_HD
    npair = W // _LANES
    tq = _rows(T, _FOX_TQ)
    nq = T // tq

    def body(q_ref, k_ref, v_ref, cum_ref, o_ref, lse_ref):
        p = pl.program_id(1)
        i = pl.program_id(2)
        q0 = pl.multiple_of(i * tq, tq)
        lane = lax.broadcasted_iota(jnp.int32, (1, _LANES), 1)
        q2 = q_ref[0] * _FOX_SCALE
        heads = []
        for hh in range(2):
            msk = (lane < _FOX_HD) if hh == 0 else (lane >= _FOX_HD)
            h = 2 * p + hh
            heads.append((msk, h, jnp.where(msk, q2, 0.0).astype(_MXU), _cum_row(cum_ref, h, q0, _LANES)[:, 0:1]))

        def step(jj, carry, masked):
            k0 = pl.multiple_of(jj * tq, tq)
            k2 = k_ref[0, pl.ds(k0, tq), :].astype(_MXU)
            v2 = v_ref[0, pl.ds(k0, tq), :]
            out = []
            for (msk, h, qm, c0), (m_prev, l_prev, acc) in zip(heads, carry):
                s = _dot(qm, k2, 1, 1) + (c0 - _cum_row(cum_ref, h, k0, tq))
                if masked:
                    s = jnp.where(_causal(tq, q0, k0), s, -jnp.inf)
                m_new = jnp.maximum(m_prev, jnp.max(s, axis=1, keepdims=True))
                alpha = jnp.exp(m_prev - m_new)
                e = jnp.exp(s - m_new)
                l_new = alpha * l_prev + jnp.sum(e, axis=1, keepdims=True)
                vm = jnp.where(msk, v2, 0.0).astype(_MXU)
                out.append((m_new, l_new, alpha * acc + _dot(e.astype(_MXU), vm, 1, 0)))
            return tuple(out)

        init = tuple((jnp.full((tq, 1), -jnp.inf, _F32), jnp.zeros((tq, 1), _F32), jnp.zeros((tq, _LANES), _F32)) for _ in heads)
        carry = step(i, lax.fori_loop(0, i, functools.partial(step, masked=False), init), True)
        o2 = jnp.zeros((tq, _LANES), _F32)
        for hh, (m, l, acc) in enumerate(carry):
            o2 = o2 + acc / l
            lse_ref[0, hh] = jnp.broadcast_to(m + jnp.log(l), (tq, _LANES))
        o_ref[0] = o2.astype(o_ref.dtype)

    (o, lse), got = _fused_call(
        body, name=name,
        out_shape=(jax.ShapeDtypeStruct((B, T, W), _MXU), jax.ShapeDtypeStruct((B, H, T, _LANES), _F32)),
        grid=(B, npair, nq),
        in_specs=[pl.BlockSpec((1, tq, _LANES), lambda b, p, i: (b, i, p)),
                  pl.BlockSpec((1, T, _LANES), lambda b, p, i: (b, 0, npair + p)),
                  pl.BlockSpec((1, T, _LANES), lambda b, p, i: (b, 0, 2 * npair + p)),
                  pl.BlockSpec((1, H, T), lambda b, p, i: (b, 0, 0))],
        out_specs=(pl.BlockSpec((1, tq, _LANES), lambda b, p, i: (b, i, p)),
                   pl.BlockSpec((1, 2, tq, _LANES), lambda b, p, i: (b, p, i, 0))),
        scratch_shapes=[], sem=("parallel", "parallel", "parallel"), args=(proj3, proj3, proj3, cum), xchg=xchg)
    return o, lse, got


def _fox_bwd(proj3, cum, do3, lse, *, name, xchg=None):
    B, T, _ = proj3.shape
    H = cum.shape[1]
    W = H * _FOX_HD
    npair = W // _LANES
    tq = _rows(T, _FOX_TQ)
    nq = T // tq

    def body(q_ref, k_ref, v_ref, cum_ref, do_ref, lse_ref, dq_ref, dk_ref, dv_ref, dc_ref, p_scr, dp_scr, dk_acc, dv_acc, dc_acc):
        p = pl.program_id(1)
        i = pl.program_id(2)
        q0 = pl.multiple_of(i * tq, tq)
        lane = lax.broadcasted_iota(jnp.int32, (1, _LANES), 1)

        @pl.when(i == 0)
        def _():
            dk_acc[...] = jnp.zeros_like(dk_acc)
            dv_acc[...] = jnp.zeros_like(dv_acc)
            dc_acc[...] = jnp.zeros_like(dc_acc)

        q2 = q_ref[0] * _FOX_SCALE
        do2 = do_ref[0].astype(_F32)
        heads = []
        for hh in range(2):
            msk = (lane < _FOX_HD) if hh == 0 else (lane >= _FOX_HD)
            h = 2 * p + hh
            heads.append((hh, msk, h, jnp.where(msk, q2, 0.0).astype(_MXU), jnp.where(msk, do2, 0.0).astype(_MXU),
                          _cum_row(cum_ref, h, q0, _LANES)[:, 0:1], lse_ref[0, hh][:, 0:1]))

        def first(jj, deltas, masked):
            k0 = pl.multiple_of(jj * tq, tq)
            kb = k_ref[0, pl.ds(k0, tq), :].astype(_MXU)
            vb = v_ref[0, pl.ds(k0, tq), :].astype(_MXU)
            out = []
            for (hh, _, h, qm, dom, c0, lse_h), delta in zip(heads, deltas):
                s = _dot(qm, kb, 1, 1) + (c0 - _cum_row(cum_ref, h, k0, tq))
                pr = jnp.exp(s - lse_h)
                if masked:
                    pr = jnp.where(_causal(tq, q0, k0), pr, 0.0)
                dp = _dot(dom, vb, 1, 1)
                p_scr[hh, jj] = pr
                dp_scr[hh, jj] = dp
                out.append(delta + jnp.sum(pr * dp, axis=1, keepdims=True))
            return tuple(out)

        zero = tuple(jnp.zeros((tq, 1), _F32) for _ in heads)
        deltas = first(i, lax.fori_loop(0, i, functools.partial(first, masked=False), zero), True)

        def second(jj, dq):
            k0 = pl.multiple_of(jj * tq, tq)
            k2 = k_ref[0, pl.ds(k0, tq), :]
            dk = jnp.zeros((tq, _LANES), _F32)
            dv = jnp.zeros((tq, _LANES), _F32)
            for (hh, msk, _, qm, dom, _, _), delta in zip(heads, deltas):
                pr = p_scr[hh, jj]
                ds = pr * (dp_scr[hh, jj] - delta)
                dsb = ds.astype(_MXU)
                dv = dv + _dot(pr.astype(_MXU), dom, 0, 0)
                dk = dk + _dot(dsb, qm, 0, 0)
                dc_acc[hh:hh + 1, pl.ds(k0, tq)] += jnp.sum(ds, axis=0, keepdims=True)
                dq = dq + _dot(dsb, jnp.where(msk, k2, 0.0).astype(_MXU), 1, 0)
            dv_acc[pl.ds(k0, tq), :] += dv
            dk_acc[pl.ds(k0, tq), :] += dk
            return dq

        dq2 = lax.fori_loop(0, i + 1, second, jnp.zeros((tq, _LANES), _F32))
        dq_ref[0] = (dq2 * _FOX_SCALE).astype(dq_ref.dtype)

        @pl.when(i == nq - 1)
        def _():
            dk_ref[0] = dk_acc[...].astype(dk_ref.dtype)
            dv_ref[0] = dv_acc[...].astype(dv_ref.dtype)
            dc_ref[0, 0] = -dc_acc[...]

    full = lambda blk: pl.BlockSpec((1, T, _LANES), lambda b, p, i, blk=blk: (b, 0, blk * npair + p))
    part = lambda blk: pl.BlockSpec((1, tq, _LANES), lambda b, p, i, blk=blk: (b, i, blk * npair + p))
    (dq, dk, dv, dcum), got = _fused_call(
        body, name=name,
        out_shape=(jax.ShapeDtypeStruct((B, T, W), _MXU), jax.ShapeDtypeStruct((B, T, W), _MXU),
                   jax.ShapeDtypeStruct((B, T, W), _MXU), jax.ShapeDtypeStruct((B, npair, 2, T), _F32)),
        grid=(B, npair, nq),
        in_specs=[part(0), full(1), full(2), pl.BlockSpec((1, H, T), lambda b, p, i: (b, 0, 0)), part(0),
                  pl.BlockSpec((1, 2, tq, _LANES), lambda b, p, i: (b, p, i, 0))],
        out_specs=(part(0), full(0), full(0), pl.BlockSpec((1, 1, 2, T), lambda b, p, i: (b, p, 0, 0))),
        scratch_shapes=[pltpu.VMEM((2, nq, tq, tq), _F32), pltpu.VMEM((2, nq, tq, tq), _F32), pltpu.VMEM((T, _LANES), _F32),
                        pltpu.VMEM((T, _LANES), _F32), pltpu.VMEM((2, T), _F32)],
        sem=("parallel", "parallel", "arbitrary"), args=(proj3, proj3, proj3, cum, do3, lse), xchg=xchg)
    return dq, dk, dv, dcum, got


def _xa_probs(qh, kh, scale):
    s = _dot(qh, kh, 1, 1) * scale
    e = jnp.exp(s - jnp.max(s, axis=1, keepdims=True))
    return e / jnp.sum(e, axis=1, keepdims=True)


def _xa_fwd(q3, kv3, *, name):
    B, T, D = q3.shape
    M = kv3.shape[1]
    hd = D // _XA_HEADS
    scale = hd ** -0.5
    tq = _rows(T, 512)

    def body(q_ref, kv_ref, o_ref):
        for h in range(_XA_HEADS):
            sl = slice(h * hd, (h + 1) * hd)
            p = _xa_probs(q_ref[0, :, sl], kv_ref[0, :, sl], scale)
            o_ref[0, :, sl] = _dot(p.astype(_MXU), kv_ref[0, :, D + h * hd:D + (h + 1) * hd], 1, 0).astype(o_ref.dtype)

    return pl.pallas_call(
        body, name=name, out_shape=jax.ShapeDtypeStruct((B, T, D), _MXU), grid=(B, T // tq),
        in_specs=[pl.BlockSpec((1, tq, D), lambda b, i: (b, i, 0)), pl.BlockSpec((1, M, 2 * D), lambda b, i: (b, 0, 0))],
        out_specs=pl.BlockSpec((1, tq, D), lambda b, i: (b, i, 0)), compiler_params=_params("parallel", "parallel"),
    )(q3, kv3)


def _xa_bwd(q3, kv3, do3, *, name):
    B, T, D = q3.shape
    M = kv3.shape[1]
    hd = D // _XA_HEADS
    scale = hd ** -0.5
    tq = _rows(T, 512)

    def body(q_ref, kv_ref, do_ref, dq_ref, dkv_ref):
        @pl.when(pl.program_id(1) == 0)
        def _():
            dkv_ref[...] = jnp.zeros_like(dkv_ref)

        for h in range(_XA_HEADS):
            sl = slice(h * hd, (h + 1) * hd)
            slv = slice(D + h * hd, D + (h + 1) * hd)
            qh, kh, vh, doh = q_ref[0, :, sl], kv_ref[0, :, sl], kv_ref[0, :, slv], do_ref[0, :, sl]
            p = _xa_probs(qh, kh, scale)
            dkv_ref[0, :, slv] += _dot(p.astype(_MXU), doh, 0, 0)
            dp = _dot(doh, vh, 1, 1)
            ds = (p * (dp - jnp.sum(p * dp, axis=1, keepdims=True))).astype(_MXU)
            dq_ref[0, :, sl] = (_dot(ds, kh, 1, 0) * scale).astype(dq_ref.dtype)
            dkv_ref[0, :, sl] += _dot(ds, qh, 0, 0) * scale

    blk = pl.BlockSpec((1, tq, D), lambda b, i: (b, i, 0))
    kvs = pl.BlockSpec((1, M, 2 * D), lambda b, i: (b, 0, 0))
    return pl.pallas_call(
        body, name=name,
        out_shape=(jax.ShapeDtypeStruct((B, T, D), _MXU), jax.ShapeDtypeStruct((B, M, 2 * D), _F32)),
        grid=(B, T // tq), in_specs=[blk, kvs, blk], out_specs=(blk, kvs),
        compiler_params=_params("parallel", "arbitrary"),
    )(q3, kv3, do3)


def _glu_fwd(ag, *, name):
    N, C2 = ag.shape
    C = C2 // 2
    tr = _rows(N, 512)

    def body(a_ref, g_ref, o_ref):
        o_ref[...] = a_ref[...] * _sigmoid(g_ref[...])

    return pl.pallas_call(
        body, name=name, out_shape=jax.ShapeDtypeStruct((N, C), _F32), grid=(N // tr,),
        in_specs=[pl.BlockSpec((tr, C), lambda i: (i, 0)), pl.BlockSpec((tr, C), lambda i: (i, 1))],
        out_specs=pl.BlockSpec((tr, C), lambda i: (i, 0)), compiler_params=_params("parallel"),
    )(ag, ag)


def _conv_fwd(y3, dw_w, dw_b, ln_g, ln_b, *, name, xchg=None):
    B, T, C = y3.shape
    tt = _rows(T, 256)
    nt = T // tt

    def body(prev_ref, cur_ref, w_ref, b_ref, g_ref, lb_ref, y2_ref, y4_ref, ext):
        i = pl.program_id(1)
        ext[0:_HALO, :] = jnp.where(i > 0, prev_ref[0, tt - _HALO:tt, :], 0.0)
        ext[_HALO:_HALO + tt, :] = cur_ref[0]
        acc = jnp.broadcast_to(b_ref[...], (tt, C))
        for j in range(_CONV_K):
            off = _HALO - (_CONV_K - 1) + j
            acc = acc + w_ref[j:j + 1, :] * ext[off:off + tt, :]
        y2_ref[0] = acc
        xh, _ = _layernorm_stats(acc)
        z = xh * g_ref[...] + lb_ref[...]
        y4_ref[0] = (z * _sigmoid(z)).astype(y4_ref.dtype)

    vec = pl.BlockSpec((1, C), lambda b, i: (0, 0))
    blk = pl.BlockSpec((1, tt, C), lambda b, i: (b, i, 0))
    (y2, y4), got = _fused_call(
        body, name=name,
        out_shape=(jax.ShapeDtypeStruct((B, T, C), _F32), jax.ShapeDtypeStruct((B, T, C), _MXU)),
        grid=(B, nt),
        in_specs=[pl.BlockSpec((1, tt, C), lambda b, i: (b, jnp.maximum(i - 1, 0), 0)), blk,
                  pl.BlockSpec((_HALO, C), lambda b, i: (0, 0)), vec, vec, vec],
        out_specs=(blk, blk), scratch_shapes=[pltpu.VMEM((tt + _HALO, C), _F32)],
        sem=("parallel", "parallel"), args=(y3, y3, dw_w, dw_b.reshape(1, C), ln_g.reshape(1, C), ln_b.reshape(1, C)), xchg=xchg)
    return y2, y4, got


def _conv_ln_bwd(y2, dy4, ln_g, ln_b, *, name):
    N, C = y2.shape
    tr = _rows(N, 256)

    def body(y_ref, d_ref, g_ref, b_ref, dy_ref, dg_ref, db_ref, dwb_ref):
        @pl.when(pl.program_id(0) == 0)
        def _():
            dg_ref[...] = jnp.zeros_like(dg_ref)
            db_ref[...] = jnp.zeros_like(db_ref)
            dwb_ref[...] = jnp.zeros_like(dwb_ref)

        xh, rstd = _layernorm_stats(y_ref[...])
        gv = g_ref[...]
        z = xh * gv + b_ref[...]
        sg = _sigmoid(z)
        dz = d_ref[...] * (sg * (1.0 + z * (1.0 - sg)))
        dg_ref[...] += jnp.sum(dz * xh, axis=0, keepdims=True)
        db_ref[...] += jnp.sum(dz, axis=0, keepdims=True)
        dxh = dz * gv
        dy = rstd * (dxh - jnp.mean(dxh, axis=-1, keepdims=True) - xh * jnp.mean(dxh * xh, axis=-1, keepdims=True))
        dwb_ref[...] += jnp.sum(dy, axis=0, keepdims=True)
        dy_ref[...] = dy

    row = pl.BlockSpec((tr, C), lambda i: (i, 0))
    vec = pl.BlockSpec((1, C), lambda i: (0, 0))
    v = jax.ShapeDtypeStruct((1, C), _F32)
    return pl.pallas_call(
        body, name=name, out_shape=(jax.ShapeDtypeStruct((N, C), _F32), v, v, v), grid=(N // tr,),
        in_specs=[row, row, vec, vec], out_specs=(row, vec, vec, vec), compiler_params=_params("arbitrary"),
    )(y2, dy4, ln_g.reshape(1, C), ln_b.reshape(1, C))


def _conv_bwd(y3, dy23, ag3, dw_w, *, name, xchg=None):
    B, T, C = y3.shape
    tt = _rows(T, 256)
    nt = T // tt

    def body(yp_ref, yc_ref, dc_ref, dn_ref, a_ref, g_ref, w_ref, dag_ref, dw_ref, dbin_ref, yext, dext):
        b = pl.program_id(0)
        i = pl.program_id(1)

        @pl.when((b == 0) & (i == 0))
        def _():
            dw_ref[...] = jnp.zeros_like(dw_ref)
            dbin_ref[...] = jnp.zeros_like(dbin_ref)

        yext[0:_HALO, :] = jnp.where(i > 0, yp_ref[0, tt - _HALO:tt, :], 0.0)
        yext[_HALO:_HALO + tt, :] = yc_ref[0]
        d_cur = dc_ref[0]
        dext[0:tt, :] = d_cur
        dext[tt:tt + _HALO, :] = jnp.where(i < nt - 1, dn_ref[0, 0:_HALO, :], 0.0)
        dy = jnp.zeros((tt, C), _F32)
        for j in range(_CONV_K):
            sh = _CONV_K - 1 - j
            dy = dy + w_ref[j:j + 1, :] * dext[sh:sh + tt, :]
            off = _HALO - sh
            dw_ref[j:j + 1, :] += jnp.sum(d_cur * yext[off:off + tt, :], axis=0, keepdims=True)
        a, g = a_ref[0], g_ref[0]
        sg = _sigmoid(g)
        da = dy * sg
        dg = dy * a * (sg * (1.0 - sg))
        dag_ref[0, :, :C] = da.astype(dag_ref.dtype)
        dag_ref[0, :, C:] = dg.astype(dag_ref.dtype)
        dbin_ref[:, :C] += jnp.sum(da, axis=0, keepdims=True)
        dbin_ref[:, C:] += jnp.sum(dg, axis=0, keepdims=True)

    blk = pl.BlockSpec((1, tt, C), lambda b, i: (b, i, 0))
    (dag, ddw, dbin), got = _fused_call(
        body, name=name,
        out_shape=(jax.ShapeDtypeStruct((B, T, 2 * C), _MXU), jax.ShapeDtypeStruct((_HALO, C), _F32),
                   jax.ShapeDtypeStruct((1, 2 * C), _F32)),
        grid=(B, nt),
        in_specs=[pl.BlockSpec((1, tt, C), lambda b, i: (b, jnp.maximum(i - 1, 0), 0)), blk, blk,
                  pl.BlockSpec((1, tt, C), lambda b, i: (b, jnp.minimum(i + 1, nt - 1), 0)),
                  blk, pl.BlockSpec((1, tt, C), lambda b, i: (b, i, 1)), pl.BlockSpec((_HALO, C), lambda b, i: (0, 0))],
        out_specs=(pl.BlockSpec((1, tt, 2 * C), lambda b, i: (b, i, 0)), pl.BlockSpec((_HALO, C), lambda b, i: (0, 0)),
                   pl.BlockSpec((1, 2 * C), lambda b, i: (0, 0))),
        scratch_shapes=[pltpu.VMEM((tt + _HALO, C), _F32), pltpu.VMEM((tt + _HALO, C), _F32)],
        sem=("arbitrary", "arbitrary"), args=(y3, y3, dy23, dy23, ag3, ag3, dw_w), xchg=xchg)
    return dag, ddw, dbin, got


class _Exchange:
    def __init__(self, items):
        self.per_peer = [pp for _, pp in items]
        self.srcs, self.out_shapes, self.pieces = [], [], []
        for t, (srcs, per_peer) in enumerate(items):
            blk = srcs[0].shape[1:] if per_peer else srcs[0].shape
            self.out_shapes.append(jax.ShapeDtypeStruct((len(srcs), _N_DEV) + tuple(blk), srcs[0].dtype))
            for l, s in enumerate(srcs):
                self.pieces.append((t, l, len(self.srcs)))
                self.srcs.append(s)
        self.n_src, self.n_dst, n_pc = len(self.srcs), len(items), len(self.pieces)
        self.in_specs = [pl.BlockSpec(memory_space=pl.ANY)] * self.n_src
        self.out_specs = [pl.BlockSpec(memory_space=pl.ANY)] * self.n_dst
        self.scratch = [pltpu.SemaphoreType.DMA((n_pc, _N_DEV - 1)), pltpu.SemaphoreType.DMA((n_pc, _N_DEV - 1)),
                        pltpu.SemaphoreType.DMA((n_pc,))]

    def _copies(self, src_refs, dst_refs, sems, kind):
        send_sems, recv_sems, loc_sems = sems
        x, y, c = lax.axis_index("x"), lax.axis_index("y"), lax.axis_index("c")
        me = 4 * x + 2 * y + c
        out = []
        for i, (t, l, s) in enumerate(self.pieces):
            def src_for(p, s=s, t=t):
                return src_refs[s].at[p] if self.per_peer[t] else src_refs[s]

            if kind == "local":
                out.append(pltpu.make_async_copy(src_for(me), dst_refs[t].at[l, me], loc_sems.at[i]))
                continue
            for k in range(1, _N_DEV):
                px, py, pc = (1 - x if k & 4 else x), (1 - y if k & 2 else y), (1 - c if k & 1 else c)
                p = 4 * px + 2 * py + pc
                out.append(pltpu.make_async_remote_copy(
                    src_ref=src_for(p), dst_ref=dst_refs[t].at[l, p if kind == "recv" else me],
                    send_sem=send_sems.at[i, k - 1], recv_sem=recv_sems.at[i, k - 1],
                    device_id=(px, py, pc), device_id_type=pl.DeviceIdType.MESH))
        return out

    def start(self, src_refs, dst_refs, sems):
        for cp in self._copies(src_refs, dst_refs, sems, "local") + self._copies(src_refs, dst_refs, sems, "send"):
            cp.start()

    def finish(self, src_refs, dst_refs, sems):
        for cp in self._copies(src_refs, dst_refs, sems, "send"):
            cp.wait_send()
        for cp in self._copies(src_refs, dst_refs, sems, "recv"):
            cp.wait_recv()
        for cp in self._copies(src_refs, dst_refs, sems, "local"):
            cp.wait()


def _exchange(items, *, name):
    ex = _Exchange(items)

    def body(*refs):
        parts = refs[:ex.n_src], refs[ex.n_src:ex.n_src + ex.n_dst], refs[ex.n_src + ex.n_dst:]
        ex.start(*parts)
        ex.finish(*parts)

    return pl.pallas_call(
        body, name=name, out_shape=ex.out_shapes, in_specs=ex.in_specs, out_specs=ex.out_specs, scratch_shapes=ex.scratch,
        compiler_params=pltpu.CompilerParams(has_side_effects=True),
    )(*ex.srcs)


def _adam_update(g, w, m, v):
    c1 = 1.0 / (1.0 - _ADAM_B1 ** _ADAM_STEP)
    c2 = 1.0 / (1.0 - _ADAM_B2 ** _ADAM_STEP)
    m2 = _ADAM_B1 * m + (1.0 - _ADAM_B1) * g
    v2 = _ADAM_B2 * v + (1.0 - _ADAM_B2) * (g * g)
    return -_ADAM_LR * ((m2 * c1) / (jnp.sqrt(v2 * c2) + _ADAM_EPS) + _ADAM_WD * w), m2, v2


def _adamw_big(recvs, w, m, v, *, name):
    L, R, C = w.shape
    tr = _rows(R, 256)
    nb = R // tr

    def body(*refs):
        r_refs = refs[:L]
        w_ref, m_ref, v_ref, g_ref, d_ref, mo_ref, vo_ref = refs[L:]
        for l in range(L):
            @pl.when(pl.program_id(0) == l)
            def _(r_ref=r_refs[l]):
                g = r_ref[0, 0].astype(_F32)
                for k in range(1, _N_DEV):
                    g = g + r_ref[0, k].astype(_F32)
                g_ref[0] = g
                d_ref[0], mo_ref[0], vo_ref[0] = _adam_update(g, w_ref[0], m_ref[0], v_ref[0])

    def recv_spec(l):
        return pl.BlockSpec((1, _N_DEV, tr, C), lambda ll, i: (0, 0, jnp.where(ll == l, i, jnp.where(ll < l, 0, nb - 1)), 0))

    blk = pl.BlockSpec((1, tr, C), lambda l, i: (l, i, 0))
    o = jax.ShapeDtypeStruct((L, R, C), _F32)
    return pl.pallas_call(
        body, name=name, out_shape=(o, o, o, o), grid=(L, nb),
        in_specs=[recv_spec(l) for l in range(L)] + [blk, blk, blk], out_specs=(blk, blk, blk, blk),
        compiler_params=_params("arbitrary", "arbitrary"),
    )(*recvs, w, m, v)


def _adamw_small(tensors, *, name):
    n = len(tensors)
    lanes = [t[4] for t in tensors]
    layers = [len(t[0]) for t in tensors]

    def body(*refs):
        pos = 0
        ins = []
        for t in range(n):
            ins.append((refs[pos:pos + layers[t]], *refs[pos + layers[t]:pos + layers[t] + 3]))
            pos += layers[t] + 3
        outs = refs[pos:]
        for t in range(n):
            r_refs, w_ref, m_ref, v_ref = ins[t]
            g_ref, d_ref, mo_ref, vo_ref = outs[4 * t:4 * t + 4]
            for l in range(layers[t]):
                g = r_refs[l][0, 0]
                for k in range(1, _N_DEV):
                    g = g + r_refs[l][0, k]
                if lanes[t] is not None:
                    g = g[..., :lanes[t]]
                g_ref[l] = g
                d_ref[l], mo_ref[l], vo_ref[l] = _adam_update(g, w_ref[l], m_ref[l], v_ref[l])

    args, out_shape = [], []
    for recvs, w, m, v, _ in tensors:
        args += [*recvs, w, m, v]
        out_shape += [jax.ShapeDtypeStruct(w.shape, _F32)] * 4
    outs = pl.pallas_call(
        body, name=name, out_shape=out_shape,
        in_specs=[pl.BlockSpec(memory_space=pltpu.VMEM)] * len(args), out_specs=[pl.BlockSpec(memory_space=pltpu.VMEM)] * len(out_shape),
        compiler_params=_params(),
    )(*args)
    return [tuple(outs[4 * t:4 * t + 4]) for t in range(n)]


_BIG = (("w_in_e", 2), ("w_out_e", 1), ("conv_w_in", 2), ("conv_w_out", 1), ("xa_wq", 1), ("xa_wkv", 2), ("xa_wo", 1),
        ("ffn_w_gu", 2), ("ffn_w_down", 1))
_SMALL_SHARDED = (("mix_norm_o", 1), ("conv_b_in", 1), ("conv_dw_w", 2), ("conv_dw_b", 1), ("conv_ln_g", 1),
                  ("conv_ln_b", 1), ("conv_b_out", 1))
_REPLICATED = ("mix_norm_e", "fox_f_bias", "gmlp_ln_g", "gmlp_ln_b", "gmlp_w_s", "gmlp_b_s", "xa_norm", "mem_norm",
               "ffn_norm", "final_norm")
_WEIGHTS = ("mix_norm_e", "w_in_e", "fox_f_bias", "gmlp_ln_g", "gmlp_ln_b", "gmlp_w_s", "gmlp_b_s", "w_out_e", "mix_norm_o",
            "conv_w_in", "conv_b_in", "conv_dw_w", "conv_dw_b", "conv_ln_g", "conv_ln_b", "conv_w_out", "conv_b_out",
            "xa_norm", "mem_norm", "xa_wq", "xa_wkv", "xa_wo", "ffn_norm", "ffn_w_gu", "ffn_w_down", "final_norm")


def _cols_to_peers(g, n=_N_DEV):
    K, N = g.shape[-2:]
    return jnp.swapaxes(g.reshape(g.shape[:-1] + (n, N // n)), -3, -2)


def _weight_items(pieces, wsrc):
    return [([wsrc[n][l]] if n in dict(_BIG) else [wsrc[n]], False) for n, l in pieces]


def _place_weights(P, pieces, gathered):
    axis = dict(_BIG + _SMALL_SHARDED)
    for (n, l), g in zip(pieces, gathered):
        if n in dict(_BIG):
            P.setdefault(n, {})[l] = g.reshape(1, -1, g.shape[-1]) if axis[n] == 1 else _peers_to_cols(g)
        else:
            P[n] = _peers_to_cols(g[0, :, 0])[None] if axis[n] == 2 else g.reshape(1, -1)


def _grad_items(pieces, G):
    axis = dict(_BIG + _SMALL_SHARDED)
    items = []
    for n, l in pieces:
        g = G[n][l]
        if n in _REPLICATED:
            items.append(([g], False))
        elif n == "ffn_w_gu":
            half = _N_DEV // 2
            items.append(([jnp.concatenate([_cols_to_peers(g[0], half), _cols_to_peers(g[1], half)], axis=0)], True))
        elif n in dict(_BIG):
            items.append(([g.reshape(_N_DEV, -1, g.shape[-1]) if axis[n] == 1 else _cols_to_peers(g)], True))
        else:
            items.append(([_cols_to_peers(g) if axis[n] == 2 else g.reshape(_N_DEV, 1, -1)], True))
    return items


def _peers_to_cols(d):
    K, c = d.shape[-2:]
    return jnp.swapaxes(d, -3, -2).reshape(d.shape[:-3] + (K, _N_DEV * c))


def _local_step(x, mem, tgt, P, wsrc=None, fwd_hooks=None, bwd_hooks=None):
    fwd_hooks, bwd_hooks = fwd_hooks or {}, bwd_hooks or {}
    sent = {}

    def gather(kernel_name):
        return _Exchange(_weight_items(fwd_hooks[kernel_name], wsrc)) if kernel_name in fwd_hooks else None

    def placed(kernel_name, got):
        if kernel_name in fwd_hooks:
            _place_weights(P, fwd_hooks[kernel_name], got)

    def scatter(kernel_name):
        return _Exchange(_grad_items(bwd_hooks[kernel_name], G)) if kernel_name in bwd_hooks else None

    def received(kernel_name, got):
        if kernel_name in bwd_hooks:
            sent.update(zip(bwd_hooks[kernel_name], got))

    B, T, D = x.shape
    M = mem.shape[1]
    N = B * T
    W = D // 2
    H = W // _FOX_HD
    f_blk = 5 * W // _LANES
    G = {}
    x0 = x.reshape(N, D)
    memf = mem.reshape(B * M, D)

    h_e = _rms_fwd(x0, P["mix_norm_e"][0], name="rms_mix_e")
    proj = _mm(h_e, P["w_in_pad"][0], bl=0, name="mm_in_e", tn=384)
    proj3 = proj.reshape(B, T, -1)
    cum = _fox_cum(proj3, f_blk, P["fox_f_bias"][0], name="fox_cum")
    o_fox, lse, got = _fox_fwd(proj3, cum, name="fox_fwd", xchg=gather("fox_fwd"))
    placed("fox_fwd", got)
    bias_full = jnp.repeat(P["gmlp_b_s"][0].T, _GRP, axis=1)
    a_out = _gmlp_fwd(proj, P["gmlp_ln_g"][0], P["gmlp_ln_b"][0], P["gmlp_w_s"][0], bias_full, name="gmlp_fwd")
    mixcat = jnp.concatenate([o_fox.reshape(N, W), a_out], axis=1)
    x1 = _mm(mixcat, P["w_out_e"][0], bl=0, res=x0, name="mm_out_e")

    def xa_ffn_fwd(xin, l):
        s = {}
        s["h_xa"] = _rms_fwd(xin, P["xa_norm"][l], name=f"rms_xa{l}")
        s["q"] = _mm(s["h_xa"], P["xa_wq"][l], bl=0, out_dtype=_MXU, name=f"mm_q{l}")
        s["mn"] = _rms_fwd(memf, P["mem_norm"][l], name=f"rms_mem{l}")
        s["kv"] = _mm(s["mn"], P["xa_wkv"][l], bl=0, out_dtype=_MXU, name=f"mm_kv{l}")
        s["o"] = _xa_fwd(s["q"].reshape(B, T, D), s["kv"].reshape(B, M, 2 * D), name=f"xa_fwd{l}").reshape(N, D)
        s["x_mid"] = _mm(s["o"], P["xa_wo"][l], bl=0, res=xin, name=f"mm_o{l}")
        s["h_ffn"] = _rms_fwd(s["x_mid"], P["ffn_norm"][l], name=f"rms_ffn{l}")
        s["gu"], s["act"], got = _mm_gu(s["h_ffn"], P["ffn_w_gu"][l], 0, name=f"mm_gu{l}", xchg=gather(f"mm_gu{l}"))
        placed(f"mm_gu{l}", got)
        s["x_in"] = xin
        xout = _mm(s["act"], P["ffn_w_down"][l], bl=0, res=s["x_mid"], name=f"mm_down{l}", tn=512, xchg=gather(f"mm_down{l}"))
        if f"mm_down{l}" in fwd_hooks:
            xout, got = xout
            placed(f"mm_down{l}", got)
        return xout, s

    x3, s0 = xa_ffn_fwd(x1, 0)
    h_o = _rms_fwd(x3, P["mix_norm_o"][0], name="rms_mix_o")
    ag = _mm(h_o, P["conv_w_in"][0], bl=0, bias=P["conv_b_in"][0], name="mm_conv_in")
    C = ag.shape[1] // 2
    y = _glu_fwd(ag, name="glu_fwd")
    dw_w = jnp.pad(P["conv_dw_w"][0], ((0, _HALO - _CONV_K), (0, 0)))
    y2, y4, got = _conv_fwd(y.reshape(B, T, C), dw_w, P["conv_dw_b"][0], P["conv_ln_g"][0], P["conv_ln_b"][0], name="conv_fwd",
                            xchg=gather("conv_fwd"))
    placed("conv_fwd", got)
    x4 = _mm(y4.reshape(N, C), P["conv_w_out"][0], bl=0, bias=P["conv_b_out"][0], res=x3, name="mm_conv_out")
    x6, s1 = xa_ffn_fwd(x4, 1)
    loss, dx, dg = _final_loss(x6, P["final_norm"], tgt.reshape(N, D), name="final_loss")
    G["final_norm"] = [dg]

    def xa_ffn_bwd(dx, s, l):
        g = {}
        dgu = _mm_dgu(dx, P["ffn_w_down"][l], 0, s["gu"], name=f"mm_dgu{l}")
        g["ffn_w_down"] = _mm(s["act"], dx, ta=True, out_dtype=_MXU, name=f"mm_dwdown{l}", tm=1408)
        g["ffn_w_gu"] = (_mm(s["h_ffn"], dgu, ta=True, bl=0, out_dtype=_MXU, name=f"mm_dwg{l}", tn=1408),
                         _mm(s["h_ffn"], dgu, ta=True, bl=1, out_dtype=_MXU, name=f"mm_dwu{l}", tn=1408))
        dh = _mm(dgu, P["ffn_w_gu"][l], al=0, bl=0, tb=True, name=f"mm_dhffn_g{l}", tk=1408)
        dh = _mm(dgu, P["ffn_w_gu"][l], al=1, bl=0, bk0=dgu.shape[-1], tb=True, res=dh, name=f"mm_dhffn_u{l}", tk=1408)
        dx, g["ffn_norm"] = _rms_bwd(s["x_mid"], P["ffn_norm"][l], dh, dx, name=f"rms_ffn_bwd{l}")
        do = _mm(dx, P["xa_wo"][l], bl=0, tb=True, out_dtype=_MXU, name=f"mm_do{l}")
        g["xa_wo"] = _mm(s["o"], dx, ta=True, out_dtype=_MXU, name=f"mm_dwo{l}")
        dq, dkv = _xa_bwd(s["q"].reshape(B, T, D), s["kv"].reshape(B, M, 2 * D), do.reshape(B, T, D), name=f"xa_bwd{l}")
        dq, dkv = dq.reshape(N, D), dkv.reshape(B * M, 2 * D)
        g["xa_wq"] = _mm(s["h_xa"], dq, ta=True, out_dtype=_MXU, name=f"mm_dwq{l}")
        dh = _mm(dq, P["xa_wq"][l], bl=0, tb=True, name=f"mm_dhxa{l}")
        g["xa_wkv"] = _mm(s["mn"], dkv, ta=True, out_dtype=_MXU, name=f"mm_dwkv{l}")
        dmn = _mm(dkv, P["xa_wkv"][l], bl=0, tb=True, name=f"mm_dmn{l}")
        g["mem_norm"] = _rms_bwd(memf, P["mem_norm"][l], dmn, None, name=f"rms_mem_bwd{l}")
        dx, g["xa_norm"] = _rms_bwd(s["x_in"], P["xa_norm"][l], dh, dx, name=f"rms_xa_bwd{l}")
        return dx, g

    dx, g1 = xa_ffn_bwd(dx, s1, 1)
    for k in g1:
        G[k] = {1: g1[k]}
    G["conv_b_out"] = [_colsum(dx, name="colsum_b_out")]
    dy4 = _mm(dx, P["conv_w_out"][0], bl=0, tb=True, name="mm_dy4")
    G["conv_w_out"] = [_mm(y4.reshape(N, C), dx, ta=True, out_dtype=_MXU, name="mm_dwconv_out")]
    dy2, dlg, dlb, ddb = _conv_ln_bwd(y2.reshape(N, C), dy4, P["conv_ln_g"][0], P["conv_ln_b"][0], name="conv_ln_bwd")
    G["conv_ln_g"], G["conv_ln_b"], G["conv_dw_b"] = [dlg], [dlb], [ddb]
    dag, ddw, dbin, got = _conv_bwd(y.reshape(B, T, C), dy2.reshape(B, T, C), ag.reshape(B, T, 2 * C), dw_w, name="conv_bwd",
                                    xchg=scatter("conv_bwd"))
    received("conv_bwd", got)
    G["conv_dw_w"], G["conv_b_in"] = [ddw[:_CONV_K]], [dbin]
    dag = dag.reshape(N, 2 * C)
    G["conv_w_in"] = [_mm(h_o, dag, ta=True, out_dtype=_MXU, name="mm_dwconv_in")]
    dh = _mm(dag, P["conv_w_in"][0], bl=0, tb=True, name="mm_dh_o")
    dx, dg = _rms_bwd(x3, P["mix_norm_o"][0], dh, dx, name="rms_mix_o_bwd")
    G["mix_norm_o"] = [dg]
    dx, g0 = xa_ffn_bwd(dx, s0, 0)
    for k in g0:
        G[k][0] = g0[k]
    G["w_out_e"] = [_mm(mixcat, dx, ta=True, out_dtype=_MXU, name="mm_dwout_e")]
    dmix = _mm(dx, P["w_out_e"][0], bl=0, tb=True, name="mm_dmix")
    dz, dlg, dlb, dws, dbias = _gmlp_bwd(proj, dmix, 1, P["gmlp_ln_g"][0], P["gmlp_ln_b"][0], P["gmlp_w_s"][0], bias_full,
                                         name="gmlp_bwd")
    G["gmlp_ln_g"], G["gmlp_ln_b"], G["gmlp_w_s"] = [dlg], [dlb], [dws]
    G["gmlp_b_s"] = [dbias[:, :2 * (W // _LANES)].T]
    dmix3 = dmix.reshape(B, T, D)
    dq, dk, dv, dcum, got = _fox_bwd(proj3, cum, dmix3, lse, name="fox_bwd", xchg=scatter("fox_bwd"))
    received("fox_bwd", got)
    df, dfb = _fox_cum_bwd(proj3, f_blk, P["fox_f_bias"][0], dcum.reshape(B, H, T), name="fox_cum_bwd")
    G["fox_f_bias"] = [dfb]
    dproj = jnp.concatenate([dq.reshape(N, W), dk.reshape(N, W), dv.reshape(N, W), dz, df.reshape(N, _LANES).astype(_MXU)], axis=1)
    G["w_in_pad"] = [_mm(h_e, dproj, ta=True, out_dtype=_MXU, name="mm_dwin_e", tn=384)]
    dh = _mm(dproj, P["w_in_pad"][0], bl=0, tb=True, name="mm_dh_e", tk=384)
    dx, dg = _rms_bwd(x0, P["mix_norm_e"][0], dh, dx, name="rms_mix_e_bwd")
    G["mix_norm_e"] = [dg]
    return loss, dx.reshape(B, T, D), G, sent


def _pad_w_in(w_in, W, H):
    f = w_in[:, 3 * W:3 * W + H]
    return jnp.concatenate([w_in[:, :3 * W], w_in[:, 3 * W + H:], jnp.pad(f, ((0, 0), (0, _LANES - H)))], axis=1)


def _unpad_w_in(g, W, H):
    return jnp.concatenate([g[:, :3 * W], g[:, 5 * W:5 * W + H], g[:, 3 * W:5 * W]], axis=1)


def kernel(x, mem, mix_norm_e, w_in_e, fox_f_bias, gmlp_ln_g, gmlp_ln_b, gmlp_w_s, gmlp_b_s, w_out_e, mix_norm_o, conv_w_in, conv_b_in, conv_dw_w, conv_dw_b, conv_ln_g, conv_ln_b, conv_w_out, conv_b_out, xa_norm, mem_norm, xa_wq, xa_wkv, xa_wo, ffn_norm, ffn_w_gu, ffn_w_down, final_norm, loss_target, m_mix_norm_e, m_w_in_e, m_fox_f_bias, m_gmlp_ln_g, m_gmlp_ln_b, m_gmlp_w_s, m_gmlp_b_s, m_w_out_e, m_mix_norm_o, m_conv_w_in, m_conv_b_in, m_conv_dw_w, m_conv_dw_b, m_conv_ln_g, m_conv_ln_b, m_conv_w_out, m_conv_b_out, m_xa_norm, m_mem_norm, m_xa_wq, m_xa_wkv, m_xa_wo, m_ffn_norm, m_ffn_w_gu, m_ffn_w_down, m_final_norm, v_mix_norm_e, v_w_in_e, v_fox_f_bias, v_gmlp_ln_g, v_gmlp_ln_b, v_gmlp_w_s, v_gmlp_b_s, v_w_out_e, v_mix_norm_o, v_conv_w_in, v_conv_b_in, v_conv_dw_w, v_conv_dw_b, v_conv_ln_g, v_conv_ln_b, v_conv_w_out, v_conv_b_out, v_xa_norm, v_mem_norm, v_xa_wq, v_xa_wkv, v_xa_wo, v_ffn_norm, v_ffn_w_gu, v_ffn_w_down, v_final_norm):
    env = dict(locals())
    w = {n: env[n] for n in _WEIGHTS}
    mom = {n: env["m_" + n] for n in _WEIGHTS}
    var = {n: env["v_" + n] for n in _WEIGHTS}
    D = x.shape[-1]
    W = D // 2
    H = W // _FOX_HD

    def layers(n):
        return w[n].shape[0] if w[n].ndim > 1 else 1

    wsrc = {n: (w[n].astype(_MXU) if n in dict(_BIG) else w[n]) for n, _ in _BIG + _SMALL_SHARDED}
    P = {n: w[n] for n in _REPLICATED}
    first = [("w_in_e", 0)]
    _place_weights(P, first, _exchange(_weight_items(first, wsrc), name="gather_first"))
    P["w_in_pad"] = {0: _pad_w_in(P.pop("w_in_e")[0][0], W, H)[None]}
    fwd_hooks = {
        "fox_fwd": [("w_out_e", 0), ("xa_wq", 0), ("xa_wkv", 0), ("xa_wo", 0), ("ffn_w_gu", 0)],
        "mm_gu0": [("ffn_w_down", 0), ("conv_w_in", 0), ("conv_w_out", 0)] + [(n, 0) for n, _ in _SMALL_SHARDED],
        "mm_down0": [("xa_wq", 1), ("xa_wkv", 1)],
        "conv_fwd": [("xa_wo", 1), ("ffn_w_gu", 1)],
        "mm_gu1": [("ffn_w_down", 1)],
    }

    late = [("w_in_e", 0), ("mix_norm_e", 0), ("fox_f_bias", 0)]
    in_conv = [(n, 1) for n in ("ffn_w_gu", "ffn_w_down", "xa_wq", "xa_wkv", "xa_wo", "xa_norm", "mem_norm", "ffn_norm")]
    in_conv += [("final_norm", 0), ("conv_w_out", 0)]
    every = [(n, l) for n in [n for n, _ in _BIG + _SMALL_SHARDED] + list(_REPLICATED) for l in range(layers(n))]
    bwd_hooks = {"conv_bwd": in_conv, "fox_bwd": [pc for pc in every if pc not in late and pc not in in_conv]}
    loss, grad_x, G, recv = _local_step(x, mem, loss_target, P, wsrc, fwd_hooks, bwd_hooks)
    loss = lax.psum(loss[0, 0], ("x", "y", "c"))
    G["w_in_e"] = [_unpad_w_in(G.pop("w_in_pad")[0], W, H)]
    recv.update(zip(late, _exchange(_grad_items(late, G), name="scatter_last")))

    def partials(n):
        return [recv[(n, l)] for l in range(layers(n))]

    res = {n: _adamw_big(partials(n), w[n], mom[n], var[n], name="adamw_" + n) for n, _ in _BIG}
    small = [n for n, _ in _SMALL_SHARDED] + list(_REPLICATED)

    def rows(a, n):
        r = recv[(n, 0)]
        return a.reshape((layers(n),) + r.shape[2:-1] + (-1,))

    outs = _adamw_small([(partials(n), rows(w[n], n), rows(mom[n], n), rows(var[n], n),
                          w[n].shape[-1] if w[n].shape[-1] != recv[(n, 0)].shape[-1] else None) for n in small], name="adamw_small")
    for n, o in zip(small, outs):
        res[n] = tuple(a.reshape(w[n].shape) for a in o)
    return (loss, grad_x, *[res[n][0] for n in _WEIGHTS], *[res[n][1] for n in _WEIGHTS],
            *[res[n][2] for n in _WEIGHTS], *[res[n][3] for n in _WEIGHTS])
```

```python
import functools
import math

import jax
import jax.numpy as jnp
from jax import lax
from jax.experimental import pallas as pl
from jax.experimental.pallas import tpu as pltpu

_F32 = jnp.float32
_MXU = jnp.bfloat16
_VMEM_LIMIT = 48 * 1024 * 1024
_LANES = 128
_EPS = 1e-6
_N_DEV = 8
_FOX_HD = 64
_FOX_SCALE = _FOX_HD ** -0.5
_FOX_TQ = 512
_CHUNK = 128
_GRP = 64
_CONV_K = 31
_HALO = 32
_XA_HEADS = 4
_GELU_C = math.sqrt(2.0 / math.pi)
_ADAM_LR, _ADAM_B1, _ADAM_B2, _ADAM_EPS, _ADAM_WD, _ADAM_STEP = 0.001, 0.9, 0.999, 1e-08, 0.01, 10
_FLAT_W = 1024
_FLAT_ALIGN = 16 * _FLAT_W
_BIG_ROWS = 128


def _params(*sem):
    return pltpu.CompilerParams(dimension_semantics=sem if sem else None, vmem_limit_bytes=_VMEM_LIMIT)


def _pick(n, pref):
    if n <= pref:
        return n
    best = None
    for t in range(_LANES, pref + 1, _LANES):
        if n % t == 0:
            best = t
    assert best is not None, (n, pref)
    return best


def _rows(n, pref):
    if n <= pref:
        return n
    t = pref
    while n % t:
        t //= 2
    assert t >= 8, (n, pref)
    return t


def _sigmoid(x):
    return 1.0 / (1.0 + jnp.exp(-x))


def _gelu(x):
    t = jnp.tanh(_GELU_C * (x + 0.044715 * (x * x * x)))
    return 0.5 * x * (1.0 + t)


def _gelu_grad(x):
    x2 = x * x
    t = jnp.tanh(_GELU_C * (x + 0.044715 * (x2 * x)))
    return 0.5 * (1.0 + t) + 0.5 * x * (1.0 - t * t) * (_GELU_C * (1.0 + 3.0 * 0.044715 * x2))


def _dot(a, b, ca, cb):
    return lax.dot_general(a, b, (((ca,), (cb,)), ((), ())), preferred_element_type=_F32)


def _mm(a, b, *, name, ta=False, tb=False, al=None, bl=None, bk0=0, bias=None, res=None, rms_bwd=None, out_dtype=_F32,
        tm=1024, tn=512, tk=1024, xchg=None):
    if ta:
        K, M = a.shape[-2:]
    else:
        M, K = a.shape[-2:]
    if tb:
        N, K2 = b.shape[-2:]
    else:
        K2, N = b.shape[-2:]
    assert K == K2 or (tb and K2 > K), (a.shape, b.shape, ta, tb)
    tm, tn = _pick(M, tm), _pick(N, tn)
    tk = K if (not ta and K <= 2816 and K2 == K) else _pick(K, tk)
    nk = K // tk
    assert bk0 % tk == 0
    kb = bk0 // tk
    grid = (M // tm, N // tn, nk)
    if a.ndim == 3:
        a_spec = (pl.BlockSpec((None, tk, tm), lambda i, j, k: (al, k, i)) if ta
                  else pl.BlockSpec((None, tm, tk), lambda i, j, k: (al, i, k)))
    else:
        a_spec = pl.BlockSpec((tk, tm), lambda i, j, k: (k, i)) if ta else pl.BlockSpec((tm, tk), lambda i, j, k: (i, k))
    if b.ndim == 3:
        b_spec = (pl.BlockSpec((None, tn, tk), lambda i, j, k: (bl, j, k + kb)) if tb
                  else pl.BlockSpec((None, tk, tn), lambda i, j, k: (bl, k, j)))
    else:
        b_spec = pl.BlockSpec((tn, tk), lambda i, j, k: (j, k)) if tb else pl.BlockSpec((tk, tn), lambda i, j, k: (k, j))
    in_specs, args = [a_spec, b_spec], [a, b]
    if bias is not None:
        in_specs.append(pl.BlockSpec((1, tn), lambda i, j, k: (0, j)))
        args.append(bias.reshape(1, N).astype(_F32))
    if res is not None:
        in_specs.append(pl.BlockSpec((tm, tn), lambda i, j, k: (i, j)))
        args.append(res)
    has_bias, has_res, has_rms = bias is not None, res is not None, rms_bwd is not None
    if has_rms:
        assert tn == N, (tn, N)
        x, g, dres = rms_bwd
        in_specs += [pl.BlockSpec((tm, N), lambda i, j, k: (i, 0)), pl.BlockSpec((1, N), lambda i, j, k: (0, 0)),
                     pl.BlockSpec((tm, N), lambda i, j, k: (i, 0))]
        args += [x, g.reshape(1, N), dres]

    def body(*refs):
        a_ref, b_ref = refs[0], refs[1]
        pos = 2
        bias_ref = res_ref = None
        if has_bias:
            bias_ref = refs[pos]
            pos += 1
        if has_res:
            res_ref = refs[pos]
            pos += 1
        if has_rms:
            x_ref, g_ref, dres_ref = refs[pos:pos + 3]
            pos += 3
        o_ref = refs[pos]
        pos += 1
        if has_rms:
            dg_ref = refs[pos]
            pos += 1
        acc_ref = refs[pos] if nk > 1 else None
        first_rows = pl.program_id(0) == 0
        p = _dot(a_ref[...].astype(_MXU), b_ref[...].astype(_MXU), 0 if ta else 1, 1 if tb else 0)

        def finish(acc):
            if has_bias:
                acc = acc + bias_ref[...]
            if has_res:
                acc = acc + res_ref[...]
            if has_rms:
                @pl.when(first_rows)
                def _():
                    dg_ref[...] = jnp.zeros_like(dg_ref)

                xv = x_ref[...]
                r = lax.rsqrt(jnp.mean(xv * xv, axis=-1, keepdims=True) + _EPS)
                xh = xv * r
                dg_ref[...] += jnp.sum(acc * xh, axis=0, keepdims=True)
                dxn = acc * g_ref[...]
                acc = dres_ref[...] + r * (dxn - xh * jnp.mean(dxn * xh, axis=-1, keepdims=True))
            o_ref[...] = acc.astype(o_ref.dtype)

        if nk == 1:
            finish(p)
        else:
            k = pl.program_id(2)

            @pl.when(k == 0)
            def _():
                acc_ref[...] = p

            @pl.when(k > 0)
            def _():
                acc_ref[...] += p

            @pl.when(k == nk - 1)
            def _():
                finish(acc_ref[...])

    out_shape = [jax.ShapeDtypeStruct((M, N), out_dtype)]
    out_specs = [pl.BlockSpec((tm, tn), lambda i, j, k: (i, j))]
    if has_rms:
        out_shape.append(jax.ShapeDtypeStruct((1, N), _F32))
        out_specs.append(pl.BlockSpec((1, N), lambda i, j, k: (0, 0)))
    outs, got = _fused_call(
        body, name=name, out_shape=out_shape, grid=grid, in_specs=in_specs, out_specs=out_specs,
        scratch_shapes=[pltpu.VMEM((tm, tn), _F32)] if nk > 1 else [],
        sem=("arbitrary",) * 3 if has_rms else ("parallel", "parallel", "arbitrary"), args=args, xchg=xchg)
    out = tuple(outs) if has_rms else outs[0]
    return out if xchg is None else (out, got)


def _rms_fwd(x, g, *, name):
    N, D = x.shape
    tr = _rows(N, 512)

    def body(x_ref, g_ref, o_ref):
        xv = x_ref[...]
        r = lax.rsqrt(jnp.mean(xv * xv, axis=-1, keepdims=True) + _EPS)
        o_ref[...] = (xv * r * g_ref[...]).astype(o_ref.dtype)

    return pl.pallas_call(
        body,
        name=name,
        out_shape=jax.ShapeDtypeStruct((N, D), _MXU),
        grid=(N // tr,),
        in_specs=[pl.BlockSpec((tr, D), lambda i: (i, 0)), pl.BlockSpec((1, D), lambda i: (0, 0))],
        out_specs=pl.BlockSpec((tr, D), lambda i: (i, 0)),
        compiler_params=_params("parallel"),
    )(x, g.reshape(1, D))


def _rms_bwd(x, g, dh, dres, *, name):
    N, D = x.shape
    tr = _rows(N, 256)
    has_res = dres is not None

    def body(*refs):
        if has_res:
            x_ref, g_ref, dh_ref, dres_ref, dx_ref, dg_ref = refs
        else:
            x_ref, g_ref, dh_ref, dg_ref = refs
        xv = x_ref[...]
        r = lax.rsqrt(jnp.mean(xv * xv, axis=-1, keepdims=True) + _EPS)
        xh = xv * r
        dhv = dh_ref[...].astype(_F32)

        @pl.when(pl.program_id(0) == 0)
        def _():
            dg_ref[...] = jnp.zeros_like(dg_ref)

        dg_ref[...] += jnp.sum(dhv * xh, axis=0, keepdims=True)
        if has_res:
            dxn = dhv * g_ref[...]
            dx = r * (dxn - xh * jnp.mean(dxn * xh, axis=-1, keepdims=True))
            dx_ref[...] = dres_ref[...] + dx

    row = pl.BlockSpec((tr, D), lambda i: (i, 0))
    vec = pl.BlockSpec((1, D), lambda i: (0, 0))
    if has_res:
        out_shape = (jax.ShapeDtypeStruct((N, D), _F32), jax.ShapeDtypeStruct((1, D), _F32))
        out_specs = (row, vec)
        in_specs, args = [row, vec, row, row], (x, g.reshape(1, D), dh, dres)
    else:
        out_shape = jax.ShapeDtypeStruct((1, D), _F32)
        out_specs = vec
        in_specs, args = [row, vec, row], (x, g.reshape(1, D), dh)
    return pl.pallas_call(
        body, name=name, out_shape=out_shape, grid=(N // tr,), in_specs=in_specs, out_specs=out_specs,
        compiler_params=_params("arbitrary"),
    )(*args)


def _colsum(a, *, name):
    M, C = a.shape
    tr = _rows(M, 512)

    def body(a_ref, o_ref):
        @pl.when(pl.program_id(0) == 0)
        def _():
            o_ref[...] = jnp.zeros_like(o_ref)

        o_ref[...] += jnp.sum(a_ref[...].astype(_F32), axis=0, keepdims=True)

    return pl.pallas_call(
        body, name=name, out_shape=jax.ShapeDtypeStruct((1, C), _F32), grid=(M // tr,),
        in_specs=[pl.BlockSpec((tr, C), lambda i: (i, 0))], out_specs=pl.BlockSpec((1, C), lambda i: (0, 0)),
        compiler_params=_params("arbitrary"),
    )(a)


def _final_loss(x, g, tgt, *, name):
    N, D = x.shape
    tr = _rows(N, 256)

    def body(x_ref, g_ref, t_ref, loss_ref, dx_ref, dg_ref):
        xv = x_ref[...]
        r = lax.rsqrt(jnp.mean(xv * xv, axis=-1, keepdims=True) + _EPS)
        xh = xv * r
        gv = g_ref[...]
        diff = xh * gv - t_ref[...]

        @pl.when(pl.program_id(0) == 0)
        def _():
            loss_ref[...] = jnp.zeros_like(loss_ref)
            dg_ref[...] = jnp.zeros_like(dg_ref)

        part = jnp.sum(jnp.sum(diff * diff, axis=1, keepdims=True), axis=0, keepdims=True) * (0.5 / D)
        loss_ref[...] += jnp.broadcast_to(part, loss_ref.shape)
        dy = diff * (1.0 / D)
        dg_ref[...] += jnp.sum(dy * xh, axis=0, keepdims=True)
        dxn = dy * gv
        dx_ref[...] = r * (dxn - xh * jnp.mean(dxn * xh, axis=-1, keepdims=True))

    row = pl.BlockSpec((tr, D), lambda i: (i, 0))
    vec = pl.BlockSpec((1, D), lambda i: (0, 0))
    return pl.pallas_call(
        body, name=name,
        out_shape=(jax.ShapeDtypeStruct((8, _LANES), _F32), jax.ShapeDtypeStruct((N, D), _F32), jax.ShapeDtypeStruct((1, D), _F32)),
        grid=(N // tr,), in_specs=[row, vec, row],
        out_specs=(pl.BlockSpec((8, _LANES), lambda i: (0, 0)), row, vec),
        compiler_params=_params("arbitrary"),
    )(x, g.reshape(1, D), tgt)


def _mm_gu(h, w_gu, l, *, name, tm=512, tn=1408, xchg=None):
    N, K = h.shape
    H = w_gu.shape[-1] // 2
    tm, tn = _pick(N, tm), _pick(H, tn)
    nj = H // tn

    def body(h_ref, wg_ref, wu_ref, gu_ref, act_ref):
        hv = h_ref[...].astype(_MXU)
        g = _dot(hv, wg_ref[...].astype(_MXU), 1, 0)
        u = _dot(hv, wu_ref[...].astype(_MXU), 1, 0)
        gu_ref[0] = g.astype(gu_ref.dtype)
        gu_ref[1] = u.astype(gu_ref.dtype)
        act_ref[...] = (g * _sigmoid(g) * u).astype(act_ref.dtype)

    (gu, act), got = _fused_call(
        body, name=name,
        out_shape=(jax.ShapeDtypeStruct((2, N, H), _MXU), jax.ShapeDtypeStruct((N, H), _MXU)), grid=(N // tm, nj),
        in_specs=[pl.BlockSpec((tm, K), lambda i, j: (i, 0)), pl.BlockSpec((None, K, tn), lambda i, j: (l, 0, j)),
                  pl.BlockSpec((None, K, tn), lambda i, j: (l, 0, j + nj))],
        out_specs=(pl.BlockSpec((2, tm, tn), lambda i, j: (0, i, j)), pl.BlockSpec((tm, tn), lambda i, j: (i, j))),
        scratch_shapes=[], sem=("parallel", "parallel"), args=(h, w_gu, w_gu), xchg=xchg)
    return gu, act, got


def _mm_dgu(dx, w_down, l, gu, *, name, tm=512, tn=1408):
    N, K = dx.shape
    H = w_down.shape[-2]
    tm, tn = _pick(N, tm), _pick(H, tn)

    def body(dx_ref, w_ref, gu_ref, o_ref):
        d = _dot(dx_ref[...].astype(_MXU), w_ref[...].astype(_MXU), 1, 1)
        g, u = gu_ref[0].astype(_F32), gu_ref[1].astype(_F32)
        sg = _sigmoid(g)
        o_ref[0] = (d * u * (sg * (1.0 + g * (1.0 - sg)))).astype(o_ref.dtype)
        o_ref[1] = (d * (g * sg)).astype(o_ref.dtype)

    return pl.pallas_call(
        body, name=name, out_shape=jax.ShapeDtypeStruct((2, N, H), _MXU), grid=(N // tm, H // tn),
        in_specs=[pl.BlockSpec((tm, K), lambda i, j: (i, 0)), pl.BlockSpec((None, tn, K), lambda i, j: (l, j, 0)),
                  pl.BlockSpec((2, tm, tn), lambda i, j: (0, i, j))],
        out_specs=pl.BlockSpec((2, tm, tn), lambda i, j: (0, i, j)), compiler_params=_params("parallel", "parallel"),
    )(dx, w_down, gu)


def _gmlp_mix(vb, w, trans):
    tr, W = vb.shape
    lane = lax.broadcasted_iota(jnp.int32, (_CHUNK, _LANES), 1)
    rows = []
    for c in range(tr // _CHUNK):
        tiles = []
        for j in range(W // _LANES):
            t = vb[c * _CHUNK:(c + 1) * _CHUNK, j * _LANES:(j + 1) * _LANES]
            ma = _dot(w[2 * j], t, 0 if trans else 1, 0)
            mb = _dot(w[2 * j + 1], t, 0 if trans else 1, 0)
            tiles.append(jnp.where(lane < _GRP, ma, mb))
        rows.append(jnp.concatenate(tiles, axis=1))
    return jnp.concatenate(rows, axis=0)


def _tril_w(w_ref):
    r = lax.broadcasted_iota(jnp.int32, (_CHUNK, _CHUNK), 0)
    c = lax.broadcasted_iota(jnp.int32, (_CHUNK, _CHUNK), 1)
    return jnp.where((r >= c)[None], w_ref[...], 0.0).astype(_MXU)


def _layernorm_stats(v):
    mu = jnp.mean(v, axis=-1, keepdims=True)
    xc = v - mu
    rstd = lax.rsqrt(jnp.mean(xc * xc, axis=-1, keepdims=True) + _EPS)
    return xc * rstd, rstd


def _gmlp_fwd(proj, ln_g, ln_b, w_s, bias_full, *, name):
    N = proj.shape[0]
    W = ln_g.shape[-1]
    G = w_s.shape[0]
    tr = _rows(N, 512)
    ub, vb_ = 3, 4

    def body(u_ref, v_ref, g_ref, b_ref, w_ref, bias_ref, o_ref):
        u = _gelu(u_ref[...])
        xh, _ = _layernorm_stats(_gelu(v_ref[...]))
        vgn = xh * g_ref[...] + b_ref[...]
        mixed = _gmlp_mix(vgn.astype(_MXU), _tril_w(w_ref), False)
        bias = jnp.concatenate([bias_ref[...]] * (tr // _CHUNK), axis=0)
        o_ref[...] = (u * (mixed + bias)).astype(o_ref.dtype)

    vec = pl.BlockSpec((1, W), lambda i: (0, 0))
    return pl.pallas_call(
        body, name=name, out_shape=jax.ShapeDtypeStruct((N, W), _MXU), grid=(N // tr,),
        in_specs=[pl.BlockSpec((tr, W), lambda i: (i, ub)), pl.BlockSpec((tr, W), lambda i: (i, vb_)), vec, vec,
                  pl.BlockSpec((G, _CHUNK, _CHUNK), lambda i: (0, 0, 0)), pl.BlockSpec((_CHUNK, W), lambda i: (0, 0))],
        out_specs=pl.BlockSpec((tr, W), lambda i: (i, 0)), compiler_params=_params("parallel"),
    )(proj, proj, ln_g.reshape(1, W), ln_b.reshape(1, W), w_s, bias_full)


def _gmlp_bwd(proj, da_src, da_blk, ln_g, ln_b, w_s, bias_full, *, name):
    N = proj.shape[0]
    W = ln_g.shape[-1]
    G = w_s.shape[0]
    tr = _rows(N, 512)
    nch = tr // _CHUNK

    def body(u_ref, v_ref, da_ref, g_ref, b_ref, w_ref, bias_ref, dz_ref, dg_ref, db_ref, dw_ref, dbias_ref):
        @pl.when(pl.program_id(0) == 0)
        def _():
            dg_ref[...] = jnp.zeros_like(dg_ref)
            db_ref[...] = jnp.zeros_like(db_ref)
            dw_ref[...] = jnp.zeros_like(dw_ref)
            dbias_ref[...] = jnp.zeros_like(dbias_ref)

        u_pre, v_pre = u_ref[...], v_ref[...]
        ug = _gelu(u_pre)
        xh, rstd = _layernorm_stats(_gelu(v_pre))
        lg = g_ref[...]
        vgn = xh * lg + b_ref[...]
        vb = vgn.astype(_MXU)
        wt = _tril_w(w_ref)
        mixed = _gmlp_mix(vb, wt, False)
        bias = jnp.concatenate([bias_ref[...]] * nch, axis=0)
        da = da_ref[...].astype(_F32)
        du = da * (mixed + bias)
        dm = da * ug
        dmb = dm.astype(_MXU)
        lane = lax.broadcasted_iota(jnp.int32, (_CHUNK, _LANES), 1)
        r = lax.broadcasted_iota(jnp.int32, (_CHUNK, _CHUNK), 0)
        c = lax.broadcasted_iota(jnp.int32, (_CHUNK, _CHUNK), 1)
        tril = r >= c
        dmsum = dm[0:_CHUNK]
        for ch in range(1, nch):
            dmsum = dmsum + dm[ch * _CHUNK:(ch + 1) * _CHUNK]
        dbias = jnp.zeros((_CHUNK, _LANES), _F32)
        for j in range(W // _LANES):
            tile = dmsum[:, j * _LANES:(j + 1) * _LANES]
            sa = jnp.sum(jnp.where(lane < _GRP, tile, 0.0), axis=1, keepdims=True)
            sb = jnp.sum(jnp.where(lane >= _GRP, tile, 0.0), axis=1, keepdims=True)
            dbias = dbias + jnp.where(lane == 2 * j, sa, 0.0) + jnp.where(lane == 2 * j + 1, sb, 0.0)
            acc_a = jnp.zeros((_CHUNK, _CHUNK), _F32)
            acc_b = jnp.zeros((_CHUNK, _CHUNK), _F32)
            for ch in range(nch):
                dt = dmb[ch * _CHUNK:(ch + 1) * _CHUNK, j * _LANES:(j + 1) * _LANES]
                vt = vb[ch * _CHUNK:(ch + 1) * _CHUNK, j * _LANES:(j + 1) * _LANES]
                acc_a = acc_a + _dot(jnp.where(lane < _GRP, dt, jnp.zeros_like(dt)), vt, 1, 1)
                acc_b = acc_b + _dot(jnp.where(lane >= _GRP, dt, jnp.zeros_like(dt)), vt, 1, 1)
            dw_ref[2 * j] += jnp.where(tril, acc_a, 0.0)
            dw_ref[2 * j + 1] += jnp.where(tril, acc_b, 0.0)
        dbias_ref[...] += dbias
        dvgn = _gmlp_mix(dmb, wt, True)
        dg_ref[...] += jnp.sum(dvgn * xh, axis=0, keepdims=True)
        db_ref[...] += jnp.sum(dvgn, axis=0, keepdims=True)
        dxh = dvgn * lg
        dvg = rstd * (dxh - jnp.mean(dxh, axis=-1, keepdims=True) - xh * jnp.mean(dxh * xh, axis=-1, keepdims=True))
        dz_ref[:, :W] = (du * _gelu_grad(u_pre)).astype(dz_ref.dtype)
        dz_ref[:, W:] = (dvg * _gelu_grad(v_pre)).astype(dz_ref.dtype)

    vec = pl.BlockSpec((1, W), lambda i: (0, 0))
    wspec = pl.BlockSpec((G, _CHUNK, _CHUNK), lambda i: (0, 0, 0))
    return pl.pallas_call(
        body, name=name,
        out_shape=(jax.ShapeDtypeStruct((N, 2 * W), _MXU), jax.ShapeDtypeStruct((1, W), _F32), jax.ShapeDtypeStruct((1, W), _F32),
                   jax.ShapeDtypeStruct((G, _CHUNK, _CHUNK), _F32), jax.ShapeDtypeStruct((_CHUNK, _LANES), _F32)),
        grid=(N // tr,),
        in_specs=[pl.BlockSpec((tr, W), lambda i: (i, 3)), pl.BlockSpec((tr, W), lambda i: (i, 4)),
                  pl.BlockSpec((tr, W), lambda i: (i, da_blk)), vec, vec, wspec, pl.BlockSpec((_CHUNK, W), lambda i: (0, 0))],
        out_specs=(pl.BlockSpec((tr, 2 * W), lambda i: (i, 0)), vec, vec, wspec, pl.BlockSpec((_CHUNK, _LANES), lambda i: (0, 0))),
        compiler_params=_params("arbitrary"),
    )(proj, proj, da_src, ln_g.reshape(1, W), ln_b.reshape(1, W), w_s, bias_full)


def _lane_cumsum(v):
    T = v.shape[1]
    lane = lax.broadcasted_iota(jnp.int32, (8, _LANES), 1)
    carry = jnp.zeros((8, 1), _F32)
    out = []
    for ch in range(T // _LANES):
        blk = v[:, ch * _LANES:(ch + 1) * _LANES]
        sh = 1
        while sh < _LANES:
            blk = blk + jnp.where(lane >= sh, pltpu.roll(blk, sh, 1), 0.0)
            sh *= 2
        blk = blk + carry
        carry = blk[:, _LANES - 1:_LANES]
        out.append(blk)
    return jnp.concatenate(out, axis=1), carry


def _log_sigmoid(x):
    return jnp.minimum(x, 0.0) - jnp.log(1.0 + jnp.exp(-jnp.abs(x)))


def _fox_cum(proj3, f_blk, f_bias, *, name):
    B, T, _ = proj3.shape
    H = f_bias.shape[-1]
    assert H == 8

    def body(f_ref, b_ref, o_ref):
        x = f_ref[0].T[0:8, :] + b_ref[...]
        cum, _ = _lane_cumsum(_log_sigmoid(x))
        o_ref[0] = cum

    return pl.pallas_call(
        body, name=name, out_shape=jax.ShapeDtypeStruct((B, 8, T), _F32), grid=(B,),
        in_specs=[pl.BlockSpec((1, T, _LANES), lambda b: (b, 0, f_blk)), pl.BlockSpec((8, 1), lambda b: (0, 0))],
        out_specs=pl.BlockSpec((1, 8, T), lambda b: (b, 0, 0)), compiler_params=_params("parallel"),
    )(proj3, f_bias.reshape(8, 1))


def _fox_cum_bwd(proj3, f_blk, f_bias, dcum, *, name):
    B, T, _ = proj3.shape

    def body(f_ref, b_ref, dc_ref, df_ref, dbias_ref):
        @pl.when(pl.program_id(0) == 0)
        def _():
            dbias_ref[...] = jnp.zeros_like(dbias_ref)

        x = f_ref[0].T[0:8, :] + b_ref[...]
        dc = dc_ref[0]
        incl, total = _lane_cumsum(dc)
        dlf = total - incl + dc
        df = dlf * _sigmoid(-x)
        full = jnp.concatenate([df, jnp.zeros((_LANES - 8, T), _F32)], axis=0).T
        dbias_ref[...] += jnp.sum(full, axis=0, keepdims=True)
        df_ref[0] = full

    return pl.pallas_call(
        body, name=name,
        out_shape=(jax.ShapeDtypeStruct((B, T, _LANES), _F32), jax.ShapeDtypeStruct((1, _LANES), _F32)), grid=(B,),
        in_specs=[pl.BlockSpec((1, T, _LANES), lambda b: (b, 0, f_blk)), pl.BlockSpec((8, 1), lambda b: (0, 0)),
                  pl.BlockSpec((1, 8, T), lambda b: (b, 0, 0))],
        out_specs=(pl.BlockSpec((1, T, _LANES), lambda b: (b, 0, 0)), pl.BlockSpec((1, _LANES), lambda b: (0, 0))),
        compiler_params=_params("arbitrary"),
    )(proj3, f_bias.reshape(8, 1), dcum)


def _cum_row(cum_ref, h, start, size):
    blk = cum_ref[0, :, pl.ds(start, size)]
    sub = lax.broadcasted_iota(jnp.int32, (blk.shape[0], 1), 0)
    return jnp.sum(jnp.where(sub == h, blk, 0.0), axis=0, keepdims=True)


def _causal(tq, q0, k0):
    r = lax.broadcasted_iota(jnp.int32, (tq, tq), 0)
    c = lax.broadcasted_iota(jnp.int32, (tq, tq), 1)
    return (r + q0) >= (c + k0)


def _fused_call(body, *, name, out_shape, grid, in_specs, out_specs, scratch_shapes, sem, args, xchg):
    out_shape, in_specs, out_specs, scratch_shapes = list(out_shape), list(in_specs), list(out_specs), list(scratch_shapes)
    if xchg is None:
        res = pl.pallas_call(body, name=name, out_shape=out_shape, grid=grid, in_specs=in_specs, out_specs=out_specs,
                             scratch_shapes=scratch_shapes, compiler_params=_params(*sem))(*args)
        return list(res), []
    n_in, n_out, n_scr = len(in_specs), len(out_specs), len(scratch_shapes)

    def fused(*refs):
        ins, refs = refs[:n_in], refs[n_in:]
        xs, refs = refs[:xchg.n_src], refs[xchg.n_src:]
        outs, refs = refs[:n_out], refs[n_out:]
        xd, refs = refs[:xchg.n_dst], refs[xchg.n_dst:]
        scr, sems = refs[:n_scr], refs[n_scr:]
        first = last = None
        for d, g in enumerate(grid):
            i = pl.program_id(d)
            first = (i == 0) if first is None else first & (i == 0)
            last = (i == g - 1) if last is None else last & (i == g - 1)

        @pl.when(first)
        def _():
            xchg.start(xs, xd, sems)

        body(*ins, *outs, *scr)

        @pl.when(last)
        def _():
            xchg.finish(xs, xd, sems)

    res = pl.pallas_call(
        fused, name=name, out_shape=out_shape + xchg.out_shapes, grid=grid, in_specs=in_specs + xchg.in_specs,
        out_specs=out_specs + xchg.out_specs, scratch_shapes=scratch_shapes + xchg.scratch,
        compiler_params=_params(*["arbitrary"] * len(grid)),
    )(*args, *xchg.srcs)
    return list(res[:n_out]), list(res[n_out:])


def _fox_fwd(proj3, cum, *, name, xchg=None):
    B, T, _ = proj3.shape
    H = cum.shape[1]
    W = H * _FOX_HD
    npair = W // _LANES
    tq = _rows(T, _FOX_TQ)
    nq = T // tq

    def body(q_ref, k_ref, v_ref, cum_ref, o_ref, lse_ref):
        p = pl.program_id(1)
        i = pl.program_id(2)
        q0 = pl.multiple_of(i * tq, tq)
        lane = lax.broadcasted_iota(jnp.int32, (1, _LANES), 1)
        q2 = q_ref[0] * _FOX_SCALE
        heads = []
        for hh in range(2):
            msk = (lane < _FOX_HD) if hh == 0 else (lane >= _FOX_HD)
            h = 2 * p + hh
            heads.append((msk, h, jnp.where(msk, q2, 0.0).astype(_MXU), _cum_row(cum_ref, h, q0, _LANES)[:, 0:1]))

        def step(jj, carry, masked):
            k0 = pl.multiple_of(jj * tq, tq)
            k2 = k_ref[0, pl.ds(k0, tq), :].astype(_MXU)
            v2 = v_ref[0, pl.ds(k0, tq), :]
            out = []
            for (msk, h, qm, c0), (m_prev, l_prev, acc) in zip(heads, carry):
                s = _dot(qm, k2, 1, 1) + (c0 - _cum_row(cum_ref, h, k0, tq))
                if masked:
                    s = jnp.where(_causal(tq, q0, k0), s, -jnp.inf)
                m_new = jnp.maximum(m_prev, jnp.max(s, axis=1, keepdims=True))
                alpha = jnp.exp(m_prev - m_new)
                e = jnp.exp(s - m_new)
                l_new = alpha * l_prev + jnp.sum(e, axis=1, keepdims=True)
                vm = jnp.where(msk, v2, 0.0).astype(_MXU)
                out.append((m_new, l_new, alpha * acc + _dot(e.astype(_MXU), vm, 1, 0)))
            return tuple(out)

        init = tuple((jnp.full((tq, 1), -jnp.inf, _F32), jnp.zeros((tq, 1), _F32), jnp.zeros((tq, _LANES), _F32)) for _ in heads)
        carry = step(i, lax.fori_loop(0, i, functools.partial(step, masked=False), init), True)
        o2 = jnp.zeros((tq, _LANES), _F32)
        for hh, (m, l, acc) in enumerate(carry):
            o2 = o2 + acc / l
            lse_ref[0, hh] = jnp.broadcast_to(m + jnp.log(l), (tq, _LANES))
        o_ref[0] = o2.astype(o_ref.dtype)

    (o, lse), got = _fused_call(
        body, name=name,
        out_shape=(jax.ShapeDtypeStruct((B, T, W), _MXU), jax.ShapeDtypeStruct((B, H, T, _LANES), _F32)),
        grid=(B, npair, nq),
        in_specs=[pl.BlockSpec((1, tq, _LANES), lambda b, p, i: (b, i, p)),
                  pl.BlockSpec((1, T, _LANES), lambda b, p, i: (b, 0, npair + p)),
                  pl.BlockSpec((1, T, _LANES), lambda b, p, i: (b, 0, 2 * npair + p)),
                  pl.BlockSpec((1, H, T), lambda b, p, i: (b, 0, 0))],
        out_specs=(pl.BlockSpec((1, tq, _LANES), lambda b, p, i: (b, i, p)),
                   pl.BlockSpec((1, 2, tq, _LANES), lambda b, p, i: (b, p, i, 0))),
        scratch_shapes=[], sem=("parallel", "parallel", "parallel"), args=(proj3, proj3, proj3, cum), xchg=xchg)
    return o, lse, got


def _fox_bwd(proj3, cum, do3, lse, *, name, xchg=None):
    B, T, _ = proj3.shape
    H = cum.shape[1]
    W = H * _FOX_HD
    npair = W // _LANES
    tq = _rows(T, _FOX_TQ)
    nq = T // tq

    def body(q_ref, k_ref, v_ref, cum_ref, do_ref, lse_ref, dq_ref, dk_ref, dv_ref, dc_ref, p_scr, dp_scr, dk_acc, dv_acc, dc_acc):
        p = pl.program_id(1)
        i = pl.program_id(2)
        q0 = pl.multiple_of(i * tq, tq)
        lane = lax.broadcasted_iota(jnp.int32, (1, _LANES), 1)

        @pl.when(i == 0)
        def _():
            dk_acc[...] = jnp.zeros_like(dk_acc)
            dv_acc[...] = jnp.zeros_like(dv_acc)
            dc_acc[...] = jnp.zeros_like(dc_acc)

        q2 = q_ref[0] * _FOX_SCALE
        do2 = do_ref[0].astype(_F32)
        heads = []
        for hh in range(2):
            msk = (lane < _FOX_HD) if hh == 0 else (lane >= _FOX_HD)
            h = 2 * p + hh
            heads.append((hh, msk, h, jnp.where(msk, q2, 0.0).astype(_MXU), jnp.where(msk, do2, 0.0).astype(_MXU),
                          _cum_row(cum_ref, h, q0, _LANES)[:, 0:1], lse_ref[0, hh][:, 0:1]))

        def first(jj, deltas, masked):
            k0 = pl.multiple_of(jj * tq, tq)
            kb = k_ref[0, pl.ds(k0, tq), :].astype(_MXU)
            vb = v_ref[0, pl.ds(k0, tq), :].astype(_MXU)
            out = []
            for (hh, _, h, qm, dom, c0, lse_h), delta in zip(heads, deltas):
                s = _dot(qm, kb, 1, 1) + (c0 - _cum_row(cum_ref, h, k0, tq))
                pr = jnp.exp(s - lse_h)
                if masked:
                    pr = jnp.where(_causal(tq, q0, k0), pr, 0.0)
                dp = _dot(dom, vb, 1, 1)
                p_scr[hh, jj] = pr
                dp_scr[hh, jj] = dp
                out.append(delta + jnp.sum(pr * dp, axis=1, keepdims=True))
            return tuple(out)

        zero = tuple(jnp.zeros((tq, 1), _F32) for _ in heads)
        deltas = first(i, lax.fori_loop(0, i, functools.partial(first, masked=False), zero), True)

        def second(jj, dq):
            k0 = pl.multiple_of(jj * tq, tq)
            k2 = k_ref[0, pl.ds(k0, tq), :]
            dk = jnp.zeros((tq, _LANES), _F32)
            dv = jnp.zeros((tq, _LANES), _F32)
            for (hh, msk, _, qm, dom, _, _), delta in zip(heads, deltas):
                pr = p_scr[hh, jj]
                ds = pr * (dp_scr[hh, jj] - delta)
                dsb = ds.astype(_MXU)
                dv = dv + _dot(pr.astype(_MXU), dom, 0, 0)
                dk = dk + _dot(dsb, qm, 0, 0)
                dc_acc[hh:hh + 1, pl.ds(k0, tq)] += jnp.sum(ds, axis=0, keepdims=True)
                dq = dq + _dot(dsb, jnp.where(msk, k2, 0.0).astype(_MXU), 1, 0)
            dv_acc[pl.ds(k0, tq), :] += dv
            dk_acc[pl.ds(k0, tq), :] += dk
            return dq

        dq2 = lax.fori_loop(0, i + 1, second, jnp.zeros((tq, _LANES), _F32))
        dq_ref[0] = (dq2 * _FOX_SCALE).astype(dq_ref.dtype)

        @pl.when(i == nq - 1)
        def _():
            dk_ref[0] = dk_acc[...].astype(dk_ref.dtype)
            dv_ref[0] = dv_acc[...].astype(dv_ref.dtype)
            dc_ref[0, 0] = -dc_acc[...]

    full = lambda blk: pl.BlockSpec((1, T, _LANES), lambda b, p, i, blk=blk: (b, 0, blk * npair + p))
    part = lambda blk: pl.BlockSpec((1, tq, _LANES), lambda b, p, i, blk=blk: (b, i, blk * npair + p))
    (dq, dk, dv, dcum), got = _fused_call(
        body, name=name,
        out_shape=(jax.ShapeDtypeStruct((B, T, W), _MXU), jax.ShapeDtypeStruct((B, T, W), _MXU),
                   jax.ShapeDtypeStruct((B, T, W), _MXU), jax.ShapeDtypeStruct((B, npair, 2, T), _F32)),
        grid=(B, npair, nq),
        in_specs=[part(0), full(1), full(2), pl.BlockSpec((1, H, T), lambda b, p, i: (b, 0, 0)), part(0),
                  pl.BlockSpec((1, 2, tq, _LANES), lambda b, p, i: (b, p, i, 0))],
        out_specs=(part(0), full(0), full(0), pl.BlockSpec((1, 1, 2, T), lambda b, p, i: (b, p, 0, 0))),
        scratch_shapes=[pltpu.VMEM((2, nq, tq, tq), _F32), pltpu.VMEM((2, nq, tq, tq), _F32), pltpu.VMEM((T, _LANES), _F32),
                        pltpu.VMEM((T, _LANES), _F32), pltpu.VMEM((2, T), _F32)],
        sem=("parallel", "parallel", "arbitrary"), args=(proj3, proj3, proj3, cum, do3, lse), xchg=xchg)
    return dq, dk, dv, dcum, got


def _xa_probs(qh, kh, scale):
    s = _dot(qh, kh, 1, 1) * scale
    e = jnp.exp(s - jnp.max(s, axis=1, keepdims=True))
    return e / jnp.sum(e, axis=1, keepdims=True)


def _xa_fwd(q3, kv3, *, name):
    B, T, D = q3.shape
    M = kv3.shape[1]
    hd = D // _XA_HEADS
    scale = hd ** -0.5
    tq = _rows(T, 512)

    def body(q_ref, kv_ref, o_ref):
        for h in range(_XA_HEADS):
            sl = slice(h * hd, (h + 1) * hd)
            p = _xa_probs(q_ref[0, :, sl], kv_ref[0, :, sl], scale)
            o_ref[0, :, sl] = _dot(p.astype(_MXU), kv_ref[0, :, D + h * hd:D + (h + 1) * hd], 1, 0).astype(o_ref.dtype)

    return pl.pallas_call(
        body, name=name, out_shape=jax.ShapeDtypeStruct((B, T, D), _MXU), grid=(B, T // tq),
        in_specs=[pl.BlockSpec((1, tq, D), lambda b, i: (b, i, 0)), pl.BlockSpec((1, M, 2 * D), lambda b, i: (b, 0, 0))],
        out_specs=pl.BlockSpec((1, tq, D), lambda b, i: (b, i, 0)), compiler_params=_params("parallel", "parallel"),
    )(q3, kv3)


def _xa_bwd(q3, kv3, do3, *, name):
    B, T, D = q3.shape
    M = kv3.shape[1]
    hd = D // _XA_HEADS
    scale = hd ** -0.5
    tq = _rows(T, 512)

    def body(q_ref, kv_ref, do_ref, dq_ref, dkv_ref):
        @pl.when(pl.program_id(1) == 0)
        def _():
            dkv_ref[...] = jnp.zeros_like(dkv_ref)

        for h in range(_XA_HEADS):
            sl = slice(h * hd, (h + 1) * hd)
            slv = slice(D + h * hd, D + (h + 1) * hd)
            qh, kh, vh, doh = q_ref[0, :, sl], kv_ref[0, :, sl], kv_ref[0, :, slv], do_ref[0, :, sl]
            p = _xa_probs(qh, kh, scale)
            dkv_ref[0, :, slv] += _dot(p.astype(_MXU), doh, 0, 0)
            dp = _dot(doh, vh, 1, 1)
            ds = (p * (dp - jnp.sum(p * dp, axis=1, keepdims=True))).astype(_MXU)
            dq_ref[0, :, sl] = (_dot(ds, kh, 1, 0) * scale).astype(dq_ref.dtype)
            dkv_ref[0, :, sl] += _dot(ds, qh, 0, 0) * scale

    blk = pl.BlockSpec((1, tq, D), lambda b, i: (b, i, 0))
    kvs = pl.BlockSpec((1, M, 2 * D), lambda b, i: (b, 0, 0))
    return pl.pallas_call(
        body, name=name,
        out_shape=(jax.ShapeDtypeStruct((B, T, D), _MXU), jax.ShapeDtypeStruct((B, M, 2 * D), _F32)),
        grid=(B, T // tq), in_specs=[blk, kvs, blk], out_specs=(blk, kvs),
        compiler_params=_params("parallel", "arbitrary"),
    )(q3, kv3, do3)


def _glu_fwd(ag, *, name):
    N, C2 = ag.shape
    C = C2 // 2
    tr = _rows(N, 512)

    def body(a_ref, g_ref, o_ref):
        o_ref[...] = a_ref[...] * _sigmoid(g_ref[...])

    return pl.pallas_call(
        body, name=name, out_shape=jax.ShapeDtypeStruct((N, C), _F32), grid=(N // tr,),
        in_specs=[pl.BlockSpec((tr, C), lambda i: (i, 0)), pl.BlockSpec((tr, C), lambda i: (i, 1))],
        out_specs=pl.BlockSpec((tr, C), lambda i: (i, 0)), compiler_params=_params("parallel"),
    )(ag, ag)


def _rotated_copies(ext, rot, tt):
    rot[0] = ext[...]
    for b in range(1, 8):
        rot[b, 0:tt + _HALO - 8, :] = ext[b:b + tt + _HALO - 8, :]


def _shifted(rot, off, r0, rows, c0):
    a, b = divmod(off, 8)
    return rot[b, 8 * a + r0:8 * a + r0 + rows, c0:c0 + _LANES]


def _conv_fwd(y3, dw_w, dw_b, ln_g, ln_b, *, name, xchg=None):
    B, T, C = y3.shape
    tt = _rows(T, 256)
    nt = T // tt

    def body(prev_ref, cur_ref, w_ref, b_ref, g_ref, lb_ref, y2_ref, y4_ref, ext, rot):
        i = pl.program_id(1)
        ext[0:_HALO, :] = jnp.where(i > 0, prev_ref[0, tt - _HALO:tt, :], 0.0)
        ext[_HALO:_HALO + tt, :] = cur_ref[0]
        _rotated_copies(ext, rot, tt)
        for c0 in range(0, C, _LANES):
            acc = jnp.broadcast_to(b_ref[:, c0:c0 + _LANES], (tt, _LANES))
            for j in range(_CONV_K):
                acc = acc + w_ref[j:j + 1, c0:c0 + _LANES] * _shifted(rot, _HALO - (_CONV_K - 1) + j, 0, tt, c0)
            y2_ref[0, :, c0:c0 + _LANES] = acc
        xh, _ = _layernorm_stats(y2_ref[0])
        z = xh * g_ref[...] + lb_ref[...]
        y4_ref[0] = (z * _sigmoid(z)).astype(y4_ref.dtype)

    vec = pl.BlockSpec((1, C), lambda b, i: (0, 0))
    blk = pl.BlockSpec((1, tt, C), lambda b, i: (b, i, 0))
    (y2, y4), got = _fused_call(
        body, name=name,
        out_shape=(jax.ShapeDtypeStruct((B, T, C), _F32), jax.ShapeDtypeStruct((B, T, C), _MXU)),
        grid=(B, nt),
        in_specs=[pl.BlockSpec((1, tt, C), lambda b, i: (b, jnp.maximum(i - 1, 0), 0)), blk,
                  pl.BlockSpec((_HALO, C), lambda b, i: (0, 0)), vec, vec, vec],
        out_specs=(blk, blk),
        scratch_shapes=[pltpu.VMEM((tt + _HALO, C), _F32), pltpu.VMEM((8, tt + _HALO, C), _F32)],
        sem=("parallel", "parallel"), args=(y3, y3, dw_w, dw_b.reshape(1, C), ln_g.reshape(1, C), ln_b.reshape(1, C)), xchg=xchg)
    return y2, y4, got


def _conv_ln_bwd(y2, dy4, ln_g, ln_b, *, name):
    N, C = y2.shape
    tr = _rows(N, 256)

    def body(y_ref, d_ref, g_ref, b_ref, dy_ref, dg_ref, db_ref, dwb_ref):
        @pl.when(pl.program_id(0) == 0)
        def _():
            dg_ref[...] = jnp.zeros_like(dg_ref)
            db_ref[...] = jnp.zeros_like(db_ref)
            dwb_ref[...] = jnp.zeros_like(dwb_ref)

        xh, rstd = _layernorm_stats(y_ref[...])
        gv = g_ref[...]
        z = xh * gv + b_ref[...]
        sg = _sigmoid(z)
        dz = d_ref[...] * (sg * (1.0 + z * (1.0 - sg)))
        dg_ref[...] += jnp.sum(dz * xh, axis=0, keepdims=True)
        db_ref[...] += jnp.sum(dz, axis=0, keepdims=True)
        dxh = dz * gv
        dy = rstd * (dxh - jnp.mean(dxh, axis=-1, keepdims=True) - xh * jnp.mean(dxh * xh, axis=-1, keepdims=True))
        dwb_ref[...] += jnp.sum(dy, axis=0, keepdims=True)
        dy_ref[...] = dy

    row = pl.BlockSpec((tr, C), lambda i: (i, 0))
    vec = pl.BlockSpec((1, C), lambda i: (0, 0))
    v = jax.ShapeDtypeStruct((1, C), _F32)
    return pl.pallas_call(
        body, name=name, out_shape=(jax.ShapeDtypeStruct((N, C), _F32), v, v, v), grid=(N // tr,),
        in_specs=[row, row, vec, vec], out_specs=(row, vec, vec, vec), compiler_params=_params("arbitrary"),
    )(y2, dy4, ln_g.reshape(1, C), ln_b.reshape(1, C))


def _conv_bwd(y3, dy23, ag3, dw_w, *, name, xchg=None):
    B, T, C = y3.shape
    tt = _rows(T, 256)
    nt = T // tt

    rs = _rows(tt, 128)

    def groups(v):
        return jnp.sum(v.reshape(rs // 8, 8, _LANES), axis=0)

    def body(yp_ref, yc_ref, dc_ref, dn_ref, a_ref, g_ref, w_ref, dag_ref, dw_ref, dbin_ref, yext, dext, yrot, drot, dw_acc, db_acc):
        b = pl.program_id(0)
        i = pl.program_id(1)

        @pl.when((b == 0) & (i == 0))
        def _():
            dw_acc[...] = jnp.zeros_like(dw_acc)
            db_acc[...] = jnp.zeros_like(db_acc)

        yext[0:_HALO, :] = jnp.where(i > 0, yp_ref[0, tt - _HALO:tt, :], 0.0)
        yext[_HALO:_HALO + tt, :] = yc_ref[0]
        dext[0:tt, :] = dc_ref[0]
        dext[tt:tt + _HALO, :] = jnp.where(i < nt - 1, dn_ref[0, 0:_HALO, :], 0.0)
        _rotated_copies(yext, yrot, tt)
        _rotated_copies(dext, drot, tt)
        for c0 in range(0, C, _LANES):
            for r0 in range(0, tt, rs):
                d_cur = dext[r0:r0 + rs, c0:c0 + _LANES]
                dy = jnp.zeros((rs, _LANES), _F32)
                for j in range(_CONV_K):
                    sh = _CONV_K - 1 - j
                    dy = dy + w_ref[j:j + 1, c0:c0 + _LANES] * _shifted(drot, sh, r0, rs, c0)
                    dw_acc[j, :, c0:c0 + _LANES] += groups(d_cur * _shifted(yrot, _HALO - sh, r0, rs, c0))
                a, g = a_ref[0, r0:r0 + rs, c0:c0 + _LANES], g_ref[0, r0:r0 + rs, c0:c0 + _LANES]
                sg = _sigmoid(g)
                da = dy * sg
                dg = dy * a * (sg * (1.0 - sg))
                dag_ref[0, r0:r0 + rs, c0:c0 + _LANES] = da.astype(dag_ref.dtype)
                dag_ref[0, r0:r0 + rs, C + c0:C + c0 + _LANES] = dg.astype(dag_ref.dtype)
                db_acc[:, c0:c0 + _LANES] += groups(da)
                db_acc[:, C + c0:C + c0 + _LANES] += groups(dg)

        @pl.when((b == B - 1) & (i == nt - 1))
        def _():
            dw_ref[...] = jnp.sum(dw_acc[...], axis=1)
            dbin_ref[...] = jnp.sum(db_acc[...], axis=0, keepdims=True)

    blk = pl.BlockSpec((1, tt, C), lambda b, i: (b, i, 0))
    (dag, ddw, dbin), got = _fused_call(
        body, name=name,
        out_shape=(jax.ShapeDtypeStruct((B, T, 2 * C), _MXU), jax.ShapeDtypeStruct((_HALO, C), _F32),
                   jax.ShapeDtypeStruct((1, 2 * C), _F32)),
        grid=(B, nt),
        in_specs=[pl.BlockSpec((1, tt, C), lambda b, i: (b, jnp.maximum(i - 1, 0), 0)), blk, blk,
                  pl.BlockSpec((1, tt, C), lambda b, i: (b, jnp.minimum(i + 1, nt - 1), 0)),
                  blk, pl.BlockSpec((1, tt, C), lambda b, i: (b, i, 1)), pl.BlockSpec((_HALO, C), lambda b, i: (0, 0))],
        out_specs=(pl.BlockSpec((1, tt, 2 * C), lambda b, i: (b, i, 0)), pl.BlockSpec((_HALO, C), lambda b, i: (0, 0)),
                   pl.BlockSpec((1, 2 * C), lambda b, i: (0, 0))),
        scratch_shapes=[pltpu.VMEM((tt + _HALO, C), _F32), pltpu.VMEM((tt + _HALO, C), _F32),
                        pltpu.VMEM((8, tt + _HALO, C), _F32), pltpu.VMEM((8, tt + _HALO, C), _F32),
                        pltpu.VMEM((_HALO, 8, C), _F32), pltpu.VMEM((8, 2 * C), _F32)],
        sem=("arbitrary", "arbitrary"), args=(y3, y3, dy23, dy23, ag3, ag3, dw_w), xchg=xchg)
    return dag, ddw, dbin, got


class _Exchange:
    def __init__(self, items):
        self.per_peer = [pp for _, pp in items]
        self.srcs, self.out_shapes, self.pieces = [], [], []
        for t, (srcs, per_peer) in enumerate(items):
            blk = srcs[0].shape[1:] if per_peer else srcs[0].shape
            self.out_shapes.append(jax.ShapeDtypeStruct((len(srcs), _N_DEV) + tuple(blk), srcs[0].dtype))
            for l, s in enumerate(srcs):
                self.pieces.append((t, l, len(self.srcs)))
                self.srcs.append(s)
        self.n_src, self.n_dst, n_pc = len(self.srcs), len(items), len(self.pieces)
        self.in_specs = [pl.BlockSpec(memory_space=pl.ANY)] * self.n_src
        self.out_specs = [pl.BlockSpec(memory_space=pl.ANY)] * self.n_dst
        self.scratch = [pltpu.SemaphoreType.DMA((n_pc, _N_DEV - 1)), pltpu.SemaphoreType.DMA((n_pc, _N_DEV - 1)),
                        pltpu.SemaphoreType.DMA((n_pc,))]

    def _copies(self, src_refs, dst_refs, sems, kind):
        send_sems, recv_sems, loc_sems = sems
        x, y, c = lax.axis_index("x"), lax.axis_index("y"), lax.axis_index("c")
        me = 4 * x + 2 * y + c
        out = []
        for i, (t, l, s) in enumerate(self.pieces):
            def src_for(p, s=s, t=t):
                return src_refs[s].at[p] if self.per_peer[t] else src_refs[s]

            if kind == "local":
                out.append(pltpu.make_async_copy(src_for(me), dst_refs[t].at[l, me], loc_sems.at[i]))
                continue
            for k in range(1, _N_DEV):
                px, py, pc = (1 - x if k & 4 else x), (1 - y if k & 2 else y), (1 - c if k & 1 else c)
                p = 4 * px + 2 * py + pc
                out.append(pltpu.make_async_remote_copy(
                    src_ref=src_for(p), dst_ref=dst_refs[t].at[l, p if kind == "recv" else me],
                    send_sem=send_sems.at[i, k - 1], recv_sem=recv_sems.at[i, k - 1],
                    device_id=(px, py, pc), device_id_type=pl.DeviceIdType.MESH))
        return out

    def start(self, src_refs, dst_refs, sems):
        for cp in self._copies(src_refs, dst_refs, sems, "local") + self._copies(src_refs, dst_refs, sems, "send"):
            cp.start()

    def finish(self, src_refs, dst_refs, sems):
        for cp in self._copies(src_refs, dst_refs, sems, "send"):
            cp.wait_send()
        for cp in self._copies(src_refs, dst_refs, sems, "recv"):
            cp.wait_recv()
        for cp in self._copies(src_refs, dst_refs, sems, "local"):
            cp.wait()


def _exchange(items, *, name):
    ex = _Exchange(items)

    def body(*refs):
        parts = refs[:ex.n_src], refs[ex.n_src:ex.n_src + ex.n_dst], refs[ex.n_src + ex.n_dst:]
        ex.start(*parts)
        ex.finish(*parts)

    return pl.pallas_call(
        body, name=name, out_shape=ex.out_shapes, in_specs=ex.in_specs, out_specs=ex.out_specs, scratch_shapes=ex.scratch,
        compiler_params=pltpu.CompilerParams(has_side_effects=True),
    )(*ex.srcs)


def _adam_update(g, w, m, v):
    c1 = 1.0 / (1.0 - _ADAM_B1 ** _ADAM_STEP)
    c2 = 1.0 / (1.0 - _ADAM_B2 ** _ADAM_STEP)
    m2 = _ADAM_B1 * m + (1.0 - _ADAM_B1) * g
    v2 = _ADAM_B2 * v + (1.0 - _ADAM_B2) * (g * g)
    return -_ADAM_LR * ((m2 * c1) / (jnp.sqrt(v2 * c2) + _ADAM_EPS) + _ADAM_WD * w), m2, v2


def _adamw_big(recvs, w, m, v, *, name):
    L, R, C = w.shape
    tr = _rows(R, 256)
    nb = R // tr

    def body(*refs):
        r_refs = refs[:L]
        w_ref, m_ref, v_ref, g_ref, d_ref, mo_ref, vo_ref = refs[L:]
        for l in range(L):
            @pl.when(pl.program_id(0) == l)
            def _(r_ref=r_refs[l]):
                g = r_ref[0, 0].astype(_F32)
                for k in range(1, _N_DEV):
                    g = g + r_ref[0, k].astype(_F32)
                g_ref[0] = g
                d_ref[0], mo_ref[0], vo_ref[0] = _adam_update(g, w_ref[0], m_ref[0], v_ref[0])

    def recv_spec(l):
        return pl.BlockSpec((1, _N_DEV, tr, C), lambda ll, i: (0, 0, jnp.where(ll == l, i, jnp.where(ll < l, 0, nb - 1)), 0))

    blk = pl.BlockSpec((1, tr, C), lambda l, i: (l, i, 0))
    o = jax.ShapeDtypeStruct((L, R, C), _F32)
    return pl.pallas_call(
        body, name=name, out_shape=(o, o, o, o), grid=(L, nb),
        in_specs=[recv_spec(l) for l in range(L)] + [blk, blk, blk], out_specs=(blk, blk, blk, blk),
        compiler_params=_params("arbitrary", "arbitrary"),
    )(*recvs, w, m, v)


def _adamw_small(tensors, *, name):
    n = len(tensors)
    lanes = [t[4] for t in tensors]
    layers = [len(t[0]) for t in tensors]

    def body(*refs):
        pos = 0
        ins = []
        for t in range(n):
            ins.append((refs[pos:pos + layers[t]], *refs[pos + layers[t]:pos + layers[t] + 3]))
            pos += layers[t] + 3
        outs = refs[pos:]
        for t in range(n):
            r_refs, w_ref, m_ref, v_ref = ins[t]
            g_ref, d_ref, mo_ref, vo_ref = outs[4 * t:4 * t + 4]
            for l in range(layers[t]):
                g = r_refs[l][0, 0]
                for k in range(1, _N_DEV):
                    g = g + r_refs[l][0, k]
                if lanes[t] is not None:
                    g = g[..., :lanes[t]]
                g_ref[l] = g
                d_ref[l], mo_ref[l], vo_ref[l] = _adam_update(g, w_ref[l], m_ref[l], v_ref[l])

    args, out_shape = [], []
    for recvs, w, m, v, _ in tensors:
        args += [*recvs, w, m, v]
        out_shape += [jax.ShapeDtypeStruct(w.shape, _F32)] * 4
    outs = pl.pallas_call(
        body, name=name, out_shape=out_shape,
        in_specs=[pl.BlockSpec(memory_space=pltpu.VMEM)] * len(args), out_specs=[pl.BlockSpec(memory_space=pltpu.VMEM)] * len(out_shape),
        compiler_params=_params(),
    )(*args)
    return [tuple(outs[4 * t:4 * t + 4]) for t in range(n)]


_BIG = (("w_in_e", 2), ("w_out_e", 1), ("conv_w_in", 2), ("conv_w_out", 1), ("xa_wq", 1), ("xa_wkv", 2), ("xa_wo", 1),
        ("ffn_w_gu", 2), ("ffn_w_down", 1))
_SMALL_SHARDED = (("mix_norm_o", 1), ("conv_b_in", 1), ("conv_dw_w", 2), ("conv_dw_b", 1), ("conv_ln_g", 1),
                  ("conv_ln_b", 1), ("conv_b_out", 1))
_REPLICATED = ("mix_norm_e", "fox_f_bias", "gmlp_ln_g", "gmlp_ln_b", "gmlp_w_s", "gmlp_b_s", "xa_norm", "mem_norm",
               "ffn_norm", "final_norm")
_WEIGHTS = ("mix_norm_e", "w_in_e", "fox_f_bias", "gmlp_ln_g", "gmlp_ln_b", "gmlp_w_s", "gmlp_b_s", "w_out_e", "mix_norm_o",
            "conv_w_in", "conv_b_in", "conv_dw_w", "conv_dw_b", "conv_ln_g", "conv_ln_b", "conv_w_out", "conv_b_out",
            "xa_norm", "mem_norm", "xa_wq", "xa_wkv", "xa_wo", "ffn_norm", "ffn_w_gu", "ffn_w_down", "final_norm")


def _cols_to_peers(g, n=_N_DEV):
    K, N = g.shape[-2:]
    return jnp.swapaxes(g.reshape(g.shape[:-1] + (n, N // n)), -3, -2)


def _weight_items(pieces, wsrc):
    return [([wsrc[n][l]] if n in dict(_BIG) else [wsrc[n]], False) for n, l in pieces]


def _place_weights(P, pieces, gathered):
    axis = dict(_BIG + _SMALL_SHARDED)
    for (n, l), g in zip(pieces, gathered):
        if n in dict(_BIG):
            P.setdefault(n, {})[l] = g.reshape(1, -1, g.shape[-1]) if axis[n] == 1 else _peers_to_cols(g)
        else:
            P[n] = _peers_to_cols(g[0, :, 0])[None] if axis[n] == 2 else g.reshape(1, -1)


def _grad_items(pieces, G):
    axis = dict(_BIG + _SMALL_SHARDED)
    items = []
    for n, l in pieces:
        g = G[n][l]
        if n in _REPLICATED:
            items.append(([g], False))
        elif n == "ffn_w_gu":
            half = _N_DEV // 2
            items.append(([jnp.concatenate([_cols_to_peers(g[0], half), _cols_to_peers(g[1], half)], axis=0)], True))
        elif n in dict(_BIG):
            items.append(([g.reshape(_N_DEV, -1, g.shape[-1]) if axis[n] == 1 else _cols_to_peers(g)], True))
        else:
            items.append(([_cols_to_peers(g) if axis[n] == 2 else g.reshape(_N_DEV, 1, -1)], True))
    return items


def _peers_to_cols(d):
    K, c = d.shape[-2:]
    return jnp.swapaxes(d, -3, -2).reshape(d.shape[:-3] + (K, _N_DEV * c))


def _local_step(x, mem, tgt, P, wsrc=None, fwd_hooks=None, bwd_hooks=None):
    fwd_hooks, bwd_hooks = fwd_hooks or {}, bwd_hooks or {}
    sent = {}

    def gather(kernel_name):
        return _Exchange(_weight_items(fwd_hooks[kernel_name], wsrc)) if kernel_name in fwd_hooks else None

    def placed(kernel_name, got):
        if kernel_name in fwd_hooks:
            _place_weights(P, fwd_hooks[kernel_name], got)

    def scatter(kernel_name):
        return _Exchange(_grad_items(bwd_hooks[kernel_name], G)) if kernel_name in bwd_hooks else None

    def received(kernel_name, got):
        if kernel_name in bwd_hooks:
            sent.update(zip(bwd_hooks[kernel_name], got))

    B, T, D = x.shape
    M = mem.shape[1]
    N = B * T
    W = D // 2
    H = W // _FOX_HD
    f_blk = 5 * W // _LANES
    G = {}
    x0 = x.reshape(N, D)
    memf = mem.reshape(B * M, D)

    h_e = _rms_fwd(x0, P["mix_norm_e"][0], name="rms_mix_e")
    proj = _mm(h_e, P["w_in_pad"][0], bl=0, name="mm_in_e", tn=384)
    proj3 = proj.reshape(B, T, -1)
    cum = _fox_cum(proj3, f_blk, P["fox_f_bias"][0], name="fox_cum")
    o_fox, lse, got = _fox_fwd(proj3, cum, name="fox_fwd", xchg=gather("fox_fwd"))
    placed("fox_fwd", got)
    bias_full = jnp.repeat(P["gmlp_b_s"][0].T, _GRP, axis=1)
    a_out = _gmlp_fwd(proj, P["gmlp_ln_g"][0], P["gmlp_ln_b"][0], P["gmlp_w_s"][0], bias_full, name="gmlp_fwd")
    mixcat = jnp.concatenate([o_fox.reshape(N, W), a_out], axis=1)
    x1 = _mm(mixcat, P["w_out_e"][0], bl=0, res=x0, name="mm_out_e")

    def xa_ffn_fwd(xin, l):
        s = {}
        s["h_xa"] = _rms_fwd(xin, P["xa_norm"][l], name=f"rms_xa{l}")
        s["q"] = _mm(s["h_xa"], P["xa_wq"][l], bl=0, out_dtype=_MXU, name=f"mm_q{l}")
        s["mn"] = _rms_fwd(memf, P["mem_norm"][l], name=f"rms_mem{l}")
        s["kv"] = _mm(s["mn"], P["xa_wkv"][l], bl=0, out_dtype=_MXU, name=f"mm_kv{l}")
        s["o"] = _xa_fwd(s["q"].reshape(B, T, D), s["kv"].reshape(B, M, 2 * D), name=f"xa_fwd{l}").reshape(N, D)
        s["x_mid"] = _mm(s["o"], P["xa_wo"][l], bl=0, res=xin, name=f"mm_o{l}")
        s["h_ffn"] = _rms_fwd(s["x_mid"], P["ffn_norm"][l], name=f"rms_ffn{l}")
        s["gu"], s["act"], got = _mm_gu(s["h_ffn"], P["ffn_w_gu"][l], 0, name=f"mm_gu{l}", xchg=gather(f"mm_gu{l}"))
        placed(f"mm_gu{l}", got)
        s["x_in"] = xin
        xout = _mm(s["act"], P["ffn_w_down"][l], bl=0, res=s["x_mid"], name=f"mm_down{l}", tn=512, xchg=gather(f"mm_down{l}"))
        if f"mm_down{l}" in fwd_hooks:
            xout, got = xout
            placed(f"mm_down{l}", got)
        return xout, s

    x3, s0 = xa_ffn_fwd(x1, 0)
    h_o = _rms_fwd(x3, P["mix_norm_o"][0], name="rms_mix_o")
    ag = _mm(h_o, P["conv_w_in"][0], bl=0, bias=P["conv_b_in"][0], name="mm_conv_in")
    C = ag.shape[1] // 2
    y = _glu_fwd(ag, name="glu_fwd")
    dw_w = jnp.pad(P["conv_dw_w"][0], ((0, _HALO - _CONV_K), (0, 0)))
    y2, y4, got = _conv_fwd(y.reshape(B, T, C), dw_w, P["conv_dw_b"][0], P["conv_ln_g"][0], P["conv_ln_b"][0], name="conv_fwd",
                            xchg=gather("conv_fwd"))
    placed("conv_fwd", got)
    x4 = _mm(y4.reshape(N, C), P["conv_w_out"][0], bl=0, bias=P["conv_b_out"][0], res=x3, name="mm_conv_out")
    x6, s1 = xa_ffn_fwd(x4, 1)
    loss, dx, dg = _final_loss(x6, P["final_norm"], tgt.reshape(N, D), name="final_loss")
    G["final_norm"] = [dg]

    def xa_ffn_bwd(dx, s, l):
        g = {}
        dgu = _mm_dgu(dx, P["ffn_w_down"][l], 0, s["gu"], name=f"mm_dgu{l}")
        g["ffn_w_down"] = _mm(s["act"], dx, ta=True, out_dtype=_MXU, name=f"mm_dwdown{l}", tm=1408)
        g["ffn_w_gu"] = (_mm(s["h_ffn"], dgu, ta=True, bl=0, out_dtype=_MXU, name=f"mm_dwg{l}", tn=1408),
                         _mm(s["h_ffn"], dgu, ta=True, bl=1, out_dtype=_MXU, name=f"mm_dwu{l}", tn=1408))
        dh = _mm(dgu, P["ffn_w_gu"][l], al=0, bl=0, tb=True, name=f"mm_dhffn_g{l}", tk=1408)
        dx, g["ffn_norm"] = _mm(dgu, P["ffn_w_gu"][l], al=1, bl=0, bk0=dgu.shape[-1], tb=True, res=dh,
                                rms_bwd=(s["x_mid"], P["ffn_norm"][l], dx), name=f"mm_dhffn_u{l}", tm=512, tn=D, tk=1408)
        do = _mm(dx, P["xa_wo"][l], bl=0, tb=True, out_dtype=_MXU, name=f"mm_do{l}")
        g["xa_wo"] = _mm(s["o"], dx, ta=True, out_dtype=_MXU, name=f"mm_dwo{l}")
        dq, dkv = _xa_bwd(s["q"].reshape(B, T, D), s["kv"].reshape(B, M, 2 * D), do.reshape(B, T, D), name=f"xa_bwd{l}")
        dq, dkv = dq.reshape(N, D), dkv.reshape(B * M, 2 * D)
        g["xa_wq"] = _mm(s["h_xa"], dq, ta=True, out_dtype=_MXU, name=f"mm_dwq{l}")
        dx, g["xa_norm"] = _mm(dq, P["xa_wq"][l], bl=0, tb=True, rms_bwd=(s["x_in"], P["xa_norm"][l], dx), name=f"mm_dhxa{l}",
                               tm=512, tn=D)
        g["xa_wkv"] = _mm(s["mn"], dkv, ta=True, out_dtype=_MXU, name=f"mm_dwkv{l}")
        dmn = _mm(dkv, P["xa_wkv"][l], bl=0, tb=True, name=f"mm_dmn{l}")
        g["mem_norm"] = _rms_bwd(memf, P["mem_norm"][l], dmn, None, name=f"rms_mem_bwd{l}")
        return dx, g

    dx, g1 = xa_ffn_bwd(dx, s1, 1)
    for k in g1:
        G[k] = {1: g1[k]}
    G["conv_b_out"] = [_colsum(dx, name="colsum_b_out")]
    dy4 = _mm(dx, P["conv_w_out"][0], bl=0, tb=True, name="mm_dy4")
    G["conv_w_out"] = [_mm(y4.reshape(N, C), dx, ta=True, out_dtype=_MXU, name="mm_dwconv_out")]
    dy2, dlg, dlb, ddb = _conv_ln_bwd(y2.reshape(N, C), dy4, P["conv_ln_g"][0], P["conv_ln_b"][0], name="conv_ln_bwd")
    G["conv_ln_g"], G["conv_ln_b"], G["conv_dw_b"] = [dlg], [dlb], [ddb]
    dag, ddw, dbin, got = _conv_bwd(y.reshape(B, T, C), dy2.reshape(B, T, C), ag.reshape(B, T, 2 * C), dw_w, name="conv_bwd",
                                    xchg=scatter("conv_bwd"))
    received("conv_bwd", got)
    G["conv_dw_w"], G["conv_b_in"] = [ddw[:_CONV_K]], [dbin]
    dag = dag.reshape(N, 2 * C)
    G["conv_w_in"] = [_mm(h_o, dag, ta=True, out_dtype=_MXU, name="mm_dwconv_in")]
    dx, dg = _mm(dag, P["conv_w_in"][0], bl=0, tb=True, rms_bwd=(x3, P["mix_norm_o"][0], dx), name="mm_dh_o", tm=512, tn=D)
    G["mix_norm_o"] = [dg]
    dx, g0 = xa_ffn_bwd(dx, s0, 0)
    for k in g0:
        G[k][0] = g0[k]
    G["w_out_e"] = [_mm(mixcat, dx, ta=True, out_dtype=_MXU, name="mm_dwout_e")]
    dmix = _mm(dx, P["w_out_e"][0], bl=0, tb=True, name="mm_dmix")
    dz, dlg, dlb, dws, dbias = _gmlp_bwd(proj, dmix, 1, P["gmlp_ln_g"][0], P["gmlp_ln_b"][0], P["gmlp_w_s"][0], bias_full,
                                         name="gmlp_bwd")
    G["gmlp_ln_g"], G["gmlp_ln_b"], G["gmlp_w_s"] = [dlg], [dlb], [dws]
    G["gmlp_b_s"] = [dbias[:, :2 * (W // _LANES)].T]
    dmix3 = dmix.reshape(B, T, D)
    dq, dk, dv, dcum, got = _fox_bwd(proj3, cum, dmix3, lse, name="fox_bwd", xchg=scatter("fox_bwd"))
    received("fox_bwd", got)
    df, dfb = _fox_cum_bwd(proj3, f_blk, P["fox_f_bias"][0], dcum.reshape(B, H, T), name="fox_cum_bwd")
    G["fox_f_bias"] = [dfb]
    dproj = jnp.concatenate([dq.reshape(N, W), dk.reshape(N, W), dv.reshape(N, W), dz, df.reshape(N, _LANES).astype(_MXU)], axis=1)
    G["w_in_pad"] = [_mm(h_e, dproj, ta=True, out_dtype=_MXU, name="mm_dwin_e", tn=384)]
    dx, dg = _mm(dproj, P["w_in_pad"][0], bl=0, tb=True, rms_bwd=(x0, P["mix_norm_e"][0], dx), name="mm_dh_e", tm=512, tn=D)
    G["mix_norm_e"] = [dg]
    return loss, dx.reshape(B, T, D), G, sent


def _pad_w_in(w_in, W, H):
    f = w_in[:, 3 * W:3 * W + H]
    return jnp.concatenate([w_in[:, :3 * W], w_in[:, 3 * W + H:], jnp.pad(f, ((0, 0), (0, _LANES - H)))], axis=1)


def _unpad_w_in(g, W, H):
    return jnp.concatenate([g[:, :3 * W], g[:, 5 * W:5 * W + H], g[:, 3 * W:5 * W]], axis=1)


def kernel(x, mem, mix_norm_e, w_in_e, fox_f_bias, gmlp_ln_g, gmlp_ln_b, gmlp_w_s, gmlp_b_s, w_out_e, mix_norm_o, conv_w_in, conv_b_in, conv_dw_w, conv_dw_b, conv_ln_g, conv_ln_b, conv_w_out, conv_b_out, xa_norm, mem_norm, xa_wq, xa_wkv, xa_wo, ffn_norm, ffn_w_gu, ffn_w_down, final_norm, loss_target, m_mix_norm_e, m_w_in_e, m_fox_f_bias, m_gmlp_ln_g, m_gmlp_ln_b, m_gmlp_w_s, m_gmlp_b_s, m_w_out_e, m_mix_norm_o, m_conv_w_in, m_conv_b_in, m_conv_dw_w, m_conv_dw_b, m_conv_ln_g, m_conv_ln_b, m_conv_w_out, m_conv_b_out, m_xa_norm, m_mem_norm, m_xa_wq, m_xa_wkv, m_xa_wo, m_ffn_norm, m_ffn_w_gu, m_ffn_w_down, m_final_norm, v_mix_norm_e, v_w_in_e, v_fox_f_bias, v_gmlp_ln_g, v_gmlp_ln_b, v_gmlp_w_s, v_gmlp_b_s, v_w_out_e, v_mix_norm_o, v_conv_w_in, v_conv_b_in, v_conv_dw_w, v_conv_dw_b, v_conv_ln_g, v_conv_ln_b, v_conv_w_out, v_conv_b_out, v_xa_norm, v_mem_norm, v_xa_wq, v_xa_wkv, v_xa_wo, v_ffn_norm, v_ffn_w_gu, v_ffn_w_down, v_final_norm):
    env = dict(locals())
    w = {n: env[n] for n in _WEIGHTS}
    mom = {n: env["m_" + n] for n in _WEIGHTS}
    var = {n: env["v_" + n] for n in _WEIGHTS}
    D = x.shape[-1]
    W = D // 2
    H = W // _FOX_HD

    def layers(n):
        return w[n].shape[0] if w[n].ndim > 1 else 1

    wsrc = {n: (w[n].astype(_MXU) if n in dict(_BIG) else w[n]) for n, _ in _BIG + _SMALL_SHARDED}
    P = {n: w[n] for n in _REPLICATED}
    first = [("w_in_e", 0)]
    _place_weights(P, first, _exchange(_weight_items(first, wsrc), name="gather_first"))
    P["w_in_pad"] = {0: _pad_w_in(P.pop("w_in_e")[0][0], W, H)[None]}
    fwd_hooks = {
        "fox_fwd": [("w_out_e", 0), ("xa_wq", 0), ("xa_wkv", 0), ("xa_wo", 0), ("ffn_w_gu", 0)],
        "mm_gu0": [("ffn_w_down", 0), ("conv_w_in", 0), ("conv_w_out", 0)] + [(n, 0) for n, _ in _SMALL_SHARDED],
        "mm_down0": [("xa_wq", 1), ("xa_wkv", 1)],
        "conv_fwd": [("xa_wo", 1), ("ffn_w_gu", 1)],
        "mm_gu1": [("ffn_w_down", 1)],
    }

    late = [("w_in_e", 0), ("mix_norm_e", 0), ("fox_f_bias", 0)]
    in_conv = [(n, 1) for n in ("ffn_w_gu", "ffn_w_down", "xa_wq", "xa_wkv", "xa_wo", "xa_norm", "mem_norm", "ffn_norm")]
    in_conv += [("final_norm", 0), ("conv_w_out", 0)]
    every = [(n, l) for n in [n for n, _ in _BIG + _SMALL_SHARDED] + list(_REPLICATED) for l in range(layers(n))]
    bwd_hooks = {"conv_bwd": in_conv, "fox_bwd": [pc for pc in every if pc not in late and pc not in in_conv]}
    loss, grad_x, G, recv = _local_step(x, mem, loss_target, P, wsrc, fwd_hooks, bwd_hooks)
    loss = lax.psum(loss[0, 0], ("x", "y", "c"))
    G["w_in_e"] = [_unpad_w_in(G.pop("w_in_pad")[0], W, H)]
    recv.update(zip(late, _exchange(_grad_items(late, G), name="scatter_last")))

    def partials(n):
        return [recv[(n, l)] for l in range(layers(n))]

    res = {n: _adamw_big(partials(n), w[n], mom[n], var[n], name="adamw_" + n) for n, _ in _BIG}
    small = [n for n, _ in _SMALL_SHARDED] + list(_REPLICATED)

    def rows(a, n):
        r = recv[(n, 0)]
        return a.reshape((layers(n),) + r.shape[2:-1] + (-1,))

    outs = _adamw_small([(partials(n), rows(w[n], n), rows(mom[n], n), rows(var[n], n),
                          w[n].shape[-1] if w[n].shape[-1] != recv[(n, 0)].shape[-1] else None) for n in small], name="adamw_small")
    for n, o in zip(small, outs):
        res[n] = tuple(a.reshape(w[n].shape) for a in o)
    return (loss, grad_x, *[res[n][0] for n in _WEIGHTS], *[res[n][1] for n in _WEIGHTS],
            *[res[n][2] for n in _WEIGHTS], *[res[n][3] for n in _WEIGHTS])
```

```python
import functools
import math

import jax
import jax.numpy as jnp
from jax import lax
from jax.experimental import pallas as pl
from jax.experimental.pallas import tpu as pltpu

_F32 = jnp.float32
_MXU = jnp.bfloat16
_VMEM_LIMIT = 48 * 1024 * 1024
_LANES = 128
_EPS = 1e-6
_N_DEV = 8
_FOX_HD = 64
_FOX_SCALE = _FOX_HD ** -0.5
_FOX_TQ = 512
_CHUNK = 128
_GRP = 64
_CONV_K = 31
_HALO = 32
_XA_HEADS = 4
_GELU_C = math.sqrt(2.0 / math.pi)
_ADAM_LR, _ADAM_B1, _ADAM_B2, _ADAM_EPS, _ADAM_WD, _ADAM_STEP = 0.001, 0.9, 0.999, 1e-08, 0.01, 10
_FLAT_W = 1024
_FLAT_ALIGN = 16 * _FLAT_W
_BIG_ROWS = 128


def _params(*sem):
    return pltpu.CompilerParams(dimension_semantics=sem if sem else None, vmem_limit_bytes=_VMEM_LIMIT)


def _pick(n, pref):
    if n <= pref:
        return n
    best = None
    for t in range(_LANES, pref + 1, _LANES):
        if n % t == 0:
            best = t
    assert best is not None, (n, pref)
    return best


def _rows(n, pref):
    if n <= pref:
        return n
    t = pref
    while n % t:
        t //= 2
    assert t >= 8, (n, pref)
    return t


def _sigmoid(x):
    return 1.0 / (1.0 + jnp.exp(-x))


def _gelu(x):
    t = jnp.tanh(_GELU_C * (x + 0.044715 * (x * x * x)))
    return 0.5 * x * (1.0 + t)


def _gelu_grad(x):
    x2 = x * x
    t = jnp.tanh(_GELU_C * (x + 0.044715 * (x2 * x)))
    return 0.5 * (1.0 + t) + 0.5 * x * (1.0 - t * t) * (_GELU_C * (1.0 + 3.0 * 0.044715 * x2))


def _dot(a, b, ca, cb):
    return lax.dot_general(a, b, (((ca,), (cb,)), ((), ())), preferred_element_type=_F32)


def _mm(a, b, *, name, ta=False, tb=False, al=None, bl=None, bk0=0, bias=None, res=None, rms_bwd=None, out_dtype=_F32,
        tm=1024, tn=512, tk=1024, xchg=None):
    if ta:
        K, M = a.shape[-2:]
    else:
        M, K = a.shape[-2:]
    if al == "cat":
        assert not ta
        K = a.shape[0] * a.shape[-1]
    if tb:
        N, K2 = b.shape[-2:]
    else:
        K2, N = b.shape[-2:]
    assert K == K2 or (tb and K2 > K), (a.shape, b.shape, ta, tb)
    tm, tn = _pick(M, tm), _pick(N, tn)
    tk = K if (not ta and K <= 2816 and K2 == K) else _pick(a.shape[-1] if al == "cat" else K, tk)
    nk = K // tk
    assert bk0 % tk == 0
    kb = bk0 // tk
    grid = (M // tm, N // tn, nk)
    if al == "cat":
        per = a.shape[-1] // tk
        a_spec = pl.BlockSpec((None, tm, tk), lambda i, j, k: (k // per, i, k % per))
    elif a.ndim == 3:
        a_spec = (pl.BlockSpec((None, tk, tm), lambda i, j, k: (al, k, i)) if ta
                  else pl.BlockSpec((None, tm, tk), lambda i, j, k: (al, i, k)))
    else:
        a_spec = pl.BlockSpec((tk, tm), lambda i, j, k: (k, i)) if ta else pl.BlockSpec((tm, tk), lambda i, j, k: (i, k))
    if b.ndim == 3:
        b_spec = (pl.BlockSpec((None, tn, tk), lambda i, j, k: (bl, j, k + kb)) if tb
                  else pl.BlockSpec((None, tk, tn), lambda i, j, k: (bl, k, j)))
    else:
        b_spec = pl.BlockSpec((tn, tk), lambda i, j, k: (j, k)) if tb else pl.BlockSpec((tk, tn), lambda i, j, k: (k, j))
    in_specs, args = [a_spec, b_spec], [a, b]
    if bias is not None:
        in_specs.append(pl.BlockSpec((1, tn), lambda i, j, k: (0, j)))
        args.append(bias.reshape(1, N).astype(_F32))
    if res is not None:
        in_specs.append(pl.BlockSpec((tm, tn), lambda i, j, k: (i, j)))
        args.append(res)
    has_bias, has_res, has_rms = bias is not None, res is not None, rms_bwd is not None
    if has_rms:
        assert tn == N, (tn, N)
        x, g, dres = rms_bwd
        in_specs += [pl.BlockSpec((tm, N), lambda i, j, k: (i, 0)), pl.BlockSpec((1, N), lambda i, j, k: (0, 0)),
                     pl.BlockSpec((tm, N), lambda i, j, k: (i, 0))]
        args += [x, g.reshape(1, N), dres]

    def body(*refs):
        a_ref, b_ref = refs[0], refs[1]
        pos = 2
        bias_ref = res_ref = None
        if has_bias:
            bias_ref = refs[pos]
            pos += 1
        if has_res:
            res_ref = refs[pos]
            pos += 1
        if has_rms:
            x_ref, g_ref, dres_ref = refs[pos:pos + 3]
            pos += 3
        o_ref = refs[pos]
        pos += 1
        if has_rms:
            dg_ref = refs[pos]
            pos += 1
        acc_ref = refs[pos] if nk > 1 else None
        first_rows = pl.program_id(0) == 0
        p = _dot(a_ref[...].astype(_MXU), b_ref[...].astype(_MXU), 0 if ta else 1, 1 if tb else 0)

        def finish(acc):
            if has_bias:
                acc = acc + bias_ref[...]
            if has_res:
                acc = acc + res_ref[...]
            if has_rms:
                @pl.when(first_rows)
                def _():
                    dg_ref[...] = jnp.zeros_like(dg_ref)

                xv = x_ref[...]
                r = lax.rsqrt(jnp.mean(xv * xv, axis=-1, keepdims=True) + _EPS)
                xh = xv * r
                dg_ref[...] += jnp.sum(acc * xh, axis=0, keepdims=True)
                dxn = acc * g_ref[...]
                acc = dres_ref[...] + r * (dxn - xh * jnp.mean(dxn * xh, axis=-1, keepdims=True))
            o_ref[...] = acc.astype(o_ref.dtype)

        if nk == 1:
            finish(p)
        else:
            k = pl.program_id(2)

            @pl.when(k == 0)
            def _():
                acc_ref[...] = p

            @pl.when(k > 0)
            def _():
                acc_ref[...] += p

            @pl.when(k == nk - 1)
            def _():
                finish(acc_ref[...])

    out_shape = [jax.ShapeDtypeStruct((M, N), out_dtype)]
    out_specs = [pl.BlockSpec((tm, tn), lambda i, j, k: (i, j))]
    if has_rms:
        out_shape.append(jax.ShapeDtypeStruct((1, N), _F32))
        out_specs.append(pl.BlockSpec((1, N), lambda i, j, k: (0, 0)))
    outs, got = _fused_call(
        body, name=name, out_shape=out_shape, grid=grid, in_specs=in_specs, out_specs=out_specs,
        scratch_shapes=[pltpu.VMEM((tm, tn), _F32)] if nk > 1 else [],
        sem=("arbitrary",) * 3 if has_rms else ("parallel", "parallel", "arbitrary"), args=args, xchg=xchg)
    out = tuple(outs) if has_rms else outs[0]
    return out if xchg is None else (out, got)


def _rms_fwd(x, g, *, name, xchg=None):
    N, D = x.shape
    tr = _rows(N, 512)

    def body(x_ref, g_ref, o_ref):
        xv = x_ref[...]
        r = lax.rsqrt(jnp.mean(xv * xv, axis=-1, keepdims=True) + _EPS)
        o_ref[...] = (xv * r * g_ref[...]).astype(o_ref.dtype)

    (out,), got = _fused_call(
        body, name=name, out_shape=[jax.ShapeDtypeStruct((N, D), _MXU)], grid=(N // tr,),
        in_specs=[pl.BlockSpec((tr, D), lambda i: (i, 0)), pl.BlockSpec((1, D), lambda i: (0, 0))],
        out_specs=[pl.BlockSpec((tr, D), lambda i: (i, 0))], scratch_shapes=[], sem=("parallel",),
        args=(x, g.reshape(1, D)), xchg=xchg)
    return out if xchg is None else (out, got)


def _rms_bwd(x, g, dh, dres, *, name):
    N, D = x.shape
    tr = _rows(N, 256)
    has_res = dres is not None

    def body(*refs):
        if has_res:
            x_ref, g_ref, dh_ref, dres_ref, dx_ref, dg_ref = refs
        else:
            x_ref, g_ref, dh_ref, dg_ref = refs
        xv = x_ref[...]
        r = lax.rsqrt(jnp.mean(xv * xv, axis=-1, keepdims=True) + _EPS)
        xh = xv * r
        dhv = dh_ref[...].astype(_F32)

        @pl.when(pl.program_id(0) == 0)
        def _():
            dg_ref[...] = jnp.zeros_like(dg_ref)

        dg_ref[...] += jnp.sum(dhv * xh, axis=0, keepdims=True)
        if has_res:
            dxn = dhv * g_ref[...]
            dx = r * (dxn - xh * jnp.mean(dxn * xh, axis=-1, keepdims=True))
            dx_ref[...] = dres_ref[...] + dx

    row = pl.BlockSpec((tr, D), lambda i: (i, 0))
    vec = pl.BlockSpec((1, D), lambda i: (0, 0))
    if has_res:
        out_shape = (jax.ShapeDtypeStruct((N, D), _F32), jax.ShapeDtypeStruct((1, D), _F32))
        out_specs = (row, vec)
        in_specs, args = [row, vec, row, row], (x, g.reshape(1, D), dh, dres)
    else:
        out_shape = jax.ShapeDtypeStruct((1, D), _F32)
        out_specs = vec
        in_specs, args = [row, vec, row], (x, g.reshape(1, D), dh)
    return pl.pallas_call(
        body, name=name, out_shape=out_shape, grid=(N // tr,), in_specs=in_specs, out_specs=out_specs,
        compiler_params=_params("arbitrary"),
    )(*args)


def _colsum(a, *, name):
    M, C = a.shape
    tr = _rows(M, 512)

    def body(a_ref, o_ref):
        @pl.when(pl.program_id(0) == 0)
        def _():
            o_ref[...] = jnp.zeros_like(o_ref)

        o_ref[...] += jnp.sum(a_ref[...].astype(_F32), axis=0, keepdims=True)

    return pl.pallas_call(
        body, name=name, out_shape=jax.ShapeDtypeStruct((1, C), _F32), grid=(M // tr,),
        in_specs=[pl.BlockSpec((tr, C), lambda i: (i, 0))], out_specs=pl.BlockSpec((1, C), lambda i: (0, 0)),
        compiler_params=_params("arbitrary"),
    )(a)


def _final_loss(x, g, tgt, *, name):
    N, D = x.shape
    tr = _rows(N, 256)

    def body(x_ref, g_ref, t_ref, loss_ref, dx_ref, dg_ref):
        xv = x_ref[...]
        r = lax.rsqrt(jnp.mean(xv * xv, axis=-1, keepdims=True) + _EPS)
        xh = xv * r
        gv = g_ref[...]
        diff = xh * gv - t_ref[...]

        @pl.when(pl.program_id(0) == 0)
        def _():
            loss_ref[...] = jnp.zeros_like(loss_ref)
            dg_ref[...] = jnp.zeros_like(dg_ref)

        part = jnp.sum(jnp.sum(diff * diff, axis=1, keepdims=True), axis=0, keepdims=True) * (0.5 / D)
        loss_ref[...] += jnp.broadcast_to(part, loss_ref.shape)
        dy = diff * (1.0 / D)
        dg_ref[...] += jnp.sum(dy * xh, axis=0, keepdims=True)
        dxn = dy * gv
        dx_ref[...] = r * (dxn - xh * jnp.mean(dxn * xh, axis=-1, keepdims=True))

    row = pl.BlockSpec((tr, D), lambda i: (i, 0))
    vec = pl.BlockSpec((1, D), lambda i: (0, 0))
    return pl.pallas_call(
        body, name=name,
        out_shape=(jax.ShapeDtypeStruct((8, _LANES), _F32), jax.ShapeDtypeStruct((N, D), _F32), jax.ShapeDtypeStruct((1, D), _F32)),
        grid=(N // tr,), in_specs=[row, vec, row],
        out_specs=(pl.BlockSpec((8, _LANES), lambda i: (0, 0)), row, vec),
        compiler_params=_params("arbitrary"),
    )(x, g.reshape(1, D), tgt)


def _mm_gu(h, w_gu, l, *, name, bias=None, glu=False, keep=None, tm=512, tn=1408, xchg=None):
    keep = _MXU if keep is None else keep
    N, K = h.shape
    H = w_gu.shape[-1] // 2
    tm, tn = _pick(N, tm), _pick(H, tn)
    nj = H // tn
    has_bias = bias is not None

    def body(*refs):
        h_ref, wp_ref, wq_ref = refs[:3]
        pair_ref, act_ref = refs[-2:]
        hv = h_ref[...].astype(_MXU)
        p = _dot(hv, wp_ref[...].astype(_MXU), 1, 0)
        q = _dot(hv, wq_ref[...].astype(_MXU), 1, 0)
        if has_bias:
            p = p + refs[3][...]
            q = q + refs[4][...]
        pair_ref[0] = p.astype(pair_ref.dtype)
        pair_ref[1] = q.astype(pair_ref.dtype)
        act_ref[...] = (p * _sigmoid(q) if glu else p * _sigmoid(p) * q).astype(act_ref.dtype)

    in_specs = [pl.BlockSpec((tm, K), lambda i, j: (i, 0)), pl.BlockSpec((None, K, tn), lambda i, j: (l, 0, j)),
                pl.BlockSpec((None, K, tn), lambda i, j: (l, 0, j + nj))]
    args = [h, w_gu, w_gu]
    if has_bias:
        b2 = bias.reshape(1, 2 * H).astype(_F32)
        in_specs += [pl.BlockSpec((1, tn), lambda i, j: (0, j)), pl.BlockSpec((1, tn), lambda i, j: (0, j + nj))]
        args += [b2, b2]
    (pair, act), got = _fused_call(
        body, name=name,
        out_shape=(jax.ShapeDtypeStruct((2, N, H), keep), jax.ShapeDtypeStruct((N, H), keep)), grid=(N // tm, nj),
        in_specs=in_specs,
        out_specs=(pl.BlockSpec((2, tm, tn), lambda i, j: (0, i, j)), pl.BlockSpec((tm, tn), lambda i, j: (i, j))),
        scratch_shapes=[], sem=("parallel", "parallel"), args=args, xchg=xchg)
    return pair, act, got


def _mm_dgu(dx, w_down, l, gu, *, name, tm=512, tn=1408):
    N, K = dx.shape
    H = w_down.shape[-2]
    tm, tn = _pick(N, tm), _pick(H, tn)

    def body(dx_ref, w_ref, gu_ref, o_ref):
        d = _dot(dx_ref[...].astype(_MXU), w_ref[...].astype(_MXU), 1, 1)
        g, u = gu_ref[0].astype(_F32), gu_ref[1].astype(_F32)
        sg = _sigmoid(g)
        o_ref[0] = (d * u * (sg * (1.0 + g * (1.0 - sg)))).astype(o_ref.dtype)
        o_ref[1] = (d * (g * sg)).astype(o_ref.dtype)

    return pl.pallas_call(
        body, name=name, out_shape=jax.ShapeDtypeStruct((2, N, H), _MXU), grid=(N // tm, H // tn),
        in_specs=[pl.BlockSpec((tm, K), lambda i, j: (i, 0)), pl.BlockSpec((None, tn, K), lambda i, j: (l, j, 0)),
                  pl.BlockSpec((2, tm, tn), lambda i, j: (0, i, j))],
        out_specs=pl.BlockSpec((2, tm, tn), lambda i, j: (0, i, j)), compiler_params=_params("parallel", "parallel"),
    )(dx, w_down, gu)


def _gmlp_mix(vb, w, trans):
    tr, W = vb.shape
    lane = lax.broadcasted_iota(jnp.int32, (_CHUNK, _LANES), 1)
    rows = []
    for c in range(tr // _CHUNK):
        tiles = []
        for j in range(W // _LANES):
            t = vb[c * _CHUNK:(c + 1) * _CHUNK, j * _LANES:(j + 1) * _LANES]
            ma = _dot(w[2 * j], t, 0 if trans else 1, 0)
            mb = _dot(w[2 * j + 1], t, 0 if trans else 1, 0)
            tiles.append(jnp.where(lane < _GRP, ma, mb))
        rows.append(jnp.concatenate(tiles, axis=1))
    return jnp.concatenate(rows, axis=0)


def _tril_w(w_ref):
    r = lax.broadcasted_iota(jnp.int32, (_CHUNK, _CHUNK), 0)
    c = lax.broadcasted_iota(jnp.int32, (_CHUNK, _CHUNK), 1)
    return jnp.where((r >= c)[None], w_ref[...], 0.0).astype(_MXU)


def _layernorm_stats(v):
    mu = jnp.mean(v, axis=-1, keepdims=True)
    xc = v - mu
    rstd = lax.rsqrt(jnp.mean(xc * xc, axis=-1, keepdims=True) + _EPS)
    return xc * rstd, rstd


def _gmlp_fwd(proj, ln_g, ln_b, w_s, bias_full, *, name):
    N = proj.shape[0]
    W = ln_g.shape[-1]
    G = w_s.shape[0]
    tr = _rows(N, 512)
    ub, vb_ = 3, 4

    def body(u_ref, v_ref, g_ref, b_ref, w_ref, bias_ref, o_ref):
        u = _gelu(u_ref[...])
        xh, _ = _layernorm_stats(_gelu(v_ref[...]))
        vgn = xh * g_ref[...] + b_ref[...]
        mixed = _gmlp_mix(vgn.astype(_MXU), _tril_w(w_ref), False)
        bias = jnp.concatenate([bias_ref[...]] * (tr // _CHUNK), axis=0)
        o_ref[...] = (u * (mixed + bias)).astype(o_ref.dtype)

    vec = pl.BlockSpec((1, W), lambda i: (0, 0))
    return pl.pallas_call(
        body, name=name, out_shape=jax.ShapeDtypeStruct((N, W), _MXU), grid=(N // tr,),
        in_specs=[pl.BlockSpec((tr, W), lambda i: (i, ub)), pl.BlockSpec((tr, W), lambda i: (i, vb_)), vec, vec,
                  pl.BlockSpec((G, _CHUNK, _CHUNK), lambda i: (0, 0, 0)), pl.BlockSpec((_CHUNK, W), lambda i: (0, 0))],
        out_specs=pl.BlockSpec((tr, W), lambda i: (i, 0)), compiler_params=_params("parallel"),
    )(proj, proj, ln_g.reshape(1, W), ln_b.reshape(1, W), w_s, bias_full)


def _gmlp_bwd(proj, da_src, da_blk, ln_g, ln_b, w_s, bias_full, *, name):
    N = proj.shape[0]
    W = ln_g.shape[-1]
    G = w_s.shape[0]
    tr = _rows(N, 512)
    nch = tr // _CHUNK

    def body(u_ref, v_ref, da_ref, g_ref, b_ref, w_ref, bias_ref, dz_ref, dg_ref, db_ref, dw_ref, dbias_ref):
        @pl.when(pl.program_id(0) == 0)
        def _():
            dg_ref[...] = jnp.zeros_like(dg_ref)
            db_ref[...] = jnp.zeros_like(db_ref)
            dw_ref[...] = jnp.zeros_like(dw_ref)
            dbias_ref[...] = jnp.zeros_like(dbias_ref)

        u_pre, v_pre = u_ref[...], v_ref[...]
        ug = _gelu(u_pre)
        xh, rstd = _layernorm_stats(_gelu(v_pre))
        lg = g_ref[...]
        vgn = xh * lg + b_ref[...]
        vb = vgn.astype(_MXU)
        wt = _tril_w(w_ref)
        mixed = _gmlp_mix(vb, wt, False)
        bias = jnp.concatenate([bias_ref[...]] * nch, axis=0)
        da = da_ref[...].astype(_F32)
        du = da * (mixed + bias)
        dm = da * ug
        dmb = dm.astype(_MXU)
        lane = lax.broadcasted_iota(jnp.int32, (_CHUNK, _LANES), 1)
        r = lax.broadcasted_iota(jnp.int32, (_CHUNK, _CHUNK), 0)
        c = lax.broadcasted_iota(jnp.int32, (_CHUNK, _CHUNK), 1)
        tril = r >= c
        dmsum = dm[0:_CHUNK]
        for ch in range(1, nch):
            dmsum = dmsum + dm[ch * _CHUNK:(ch + 1) * _CHUNK]
        dbias = jnp.zeros((_CHUNK, _LANES), _F32)
        for j in range(W // _LANES):
            tile = dmsum[:, j * _LANES:(j + 1) * _LANES]
            sa = jnp.sum(jnp.where(lane < _GRP, tile, 0.0), axis=1, keepdims=True)
            sb = jnp.sum(jnp.where(lane >= _GRP, tile, 0.0), axis=1, keepdims=True)
            dbias = dbias + jnp.where(lane == 2 * j, sa, 0.0) + jnp.where(lane == 2 * j + 1, sb, 0.0)
            acc_a = jnp.zeros((_CHUNK, _CHUNK), _F32)
            acc_b = jnp.zeros((_CHUNK, _CHUNK), _F32)
            for ch in range(nch):
                dt = dmb[ch * _CHUNK:(ch + 1) * _CHUNK, j * _LANES:(j + 1) * _LANES]
                vt = vb[ch * _CHUNK:(ch + 1) * _CHUNK, j * _LANES:(j + 1) * _LANES]
                acc_a = acc_a + _dot(jnp.where(lane < _GRP, dt, jnp.zeros_like(dt)), vt, 1, 1)
                acc_b = acc_b + _dot(jnp.where(lane >= _GRP, dt, jnp.zeros_like(dt)), vt, 1, 1)
            dw_ref[2 * j] += jnp.where(tril, acc_a, 0.0)
            dw_ref[2 * j + 1] += jnp.where(tril, acc_b, 0.0)
        dbias_ref[...] += dbias
        dvgn = _gmlp_mix(dmb, wt, True)
        dg_ref[...] += jnp.sum(dvgn * xh, axis=0, keepdims=True)
        db_ref[...] += jnp.sum(dvgn, axis=0, keepdims=True)
        dxh = dvgn * lg
        dvg = rstd * (dxh - jnp.mean(dxh, axis=-1, keepdims=True) - xh * jnp.mean(dxh * xh, axis=-1, keepdims=True))
        dz_ref[:, :W] = (du * _gelu_grad(u_pre)).astype(dz_ref.dtype)
        dz_ref[:, W:] = (dvg * _gelu_grad(v_pre)).astype(dz_ref.dtype)

    vec = pl.BlockSpec((1, W), lambda i: (0, 0))
    wspec = pl.BlockSpec((G, _CHUNK, _CHUNK), lambda i: (0, 0, 0))
    return pl.pallas_call(
        body, name=name,
        out_shape=(jax.ShapeDtypeStruct((N, 2 * W), _MXU), jax.ShapeDtypeStruct((1, W), _F32), jax.ShapeDtypeStruct((1, W), _F32),
                   jax.ShapeDtypeStruct((G, _CHUNK, _CHUNK), _F32), jax.ShapeDtypeStruct((_CHUNK, _LANES), _F32)),
        grid=(N // tr,),
        in_specs=[pl.BlockSpec((tr, W), lambda i: (i, 3)), pl.BlockSpec((tr, W), lambda i: (i, 4)),
                  pl.BlockSpec((tr, W), lambda i: (i, da_blk)), vec, vec, wspec, pl.BlockSpec((_CHUNK, W), lambda i: (0, 0))],
        out_specs=(pl.BlockSpec((tr, 2 * W), lambda i: (i, 0)), vec, vec, wspec, pl.BlockSpec((_CHUNK, _LANES), lambda i: (0, 0))),
        compiler_params=_params("arbitrary"),
    )(proj, proj, da_src, ln_g.reshape(1, W), ln_b.reshape(1, W), w_s, bias_full)


def _lane_cumsum(v):
    T = v.shape[1]
    lane = lax.broadcasted_iota(jnp.int32, (8, _LANES), 1)
    carry = jnp.zeros((8, 1), _F32)
    out = []
    for ch in range(T // _LANES):
        blk = v[:, ch * _LANES:(ch + 1) * _LANES]
        sh = 1
        while sh < _LANES:
            blk = blk + jnp.where(lane >= sh, pltpu.roll(blk, sh, 1), 0.0)
            sh *= 2
        blk = blk + carry
        carry = blk[:, _LANES - 1:_LANES]
        out.append(blk)
    return jnp.concatenate(out, axis=1), carry


def _log_sigmoid(x):
    return jnp.minimum(x, 0.0) - jnp.log(1.0 + jnp.exp(-jnp.abs(x)))


def _fox_cum(proj3, f_blk, f_bias, *, name):
    B, T, _ = proj3.shape
    H = f_bias.shape[-1]
    assert H == 8

    def body(f_ref, b_ref, o_ref):
        x = f_ref[0].T[0:8, :] + b_ref[...]
        cum, _ = _lane_cumsum(_log_sigmoid(x))
        o_ref[0] = cum

    return pl.pallas_call(
        body, name=name, out_shape=jax.ShapeDtypeStruct((B, 8, T), _F32), grid=(B,),
        in_specs=[pl.BlockSpec((1, T, _LANES), lambda b: (b, 0, f_blk)), pl.BlockSpec((8, 1), lambda b: (0, 0))],
        out_specs=pl.BlockSpec((1, 8, T), lambda b: (b, 0, 0)), compiler_params=_params("parallel"),
    )(proj3, f_bias.reshape(8, 1))


def _fox_cum_bwd(proj3, f_blk, f_bias, dcum, *, name):
    B, T, _ = proj3.shape

    def body(f_ref, b_ref, dc_ref, df_ref, dbias_ref):
        @pl.when(pl.program_id(0) == 0)
        def _():
            dbias_ref[...] = jnp.zeros_like(dbias_ref)

        x = f_ref[0].T[0:8, :] + b_ref[...]
        dc = dc_ref[0]
        incl, total = _lane_cumsum(dc)
        dlf = total - incl + dc
        df = dlf * _sigmoid(-x)
        full = jnp.concatenate([df, jnp.zeros((_LANES - 8, T), _F32)], axis=0).T
        dbias_ref[...] += jnp.sum(full, axis=0, keepdims=True)
        df_ref[0] = full

    return pl.pallas_call(
        body, name=name,
        out_shape=(jax.ShapeDtypeStruct((B, T, _LANES), _F32), jax.ShapeDtypeStruct((1, _LANES), _F32)), grid=(B,),
        in_specs=[pl.BlockSpec((1, T, _LANES), lambda b: (b, 0, f_blk)), pl.BlockSpec((8, 1), lambda b: (0, 0)),
                  pl.BlockSpec((1, 8, T), lambda b: (b, 0, 0))],
        out_specs=(pl.BlockSpec((1, T, _LANES), lambda b: (b, 0, 0)), pl.BlockSpec((1, _LANES), lambda b: (0, 0))),
        compiler_params=_params("arbitrary"),
    )(proj3, f_bias.reshape(8, 1), dcum)


def _cum_row(cum_ref, h, start, size):
    blk = cum_ref[0, :, pl.ds(start, size)]
    sub = lax.broadcasted_iota(jnp.int32, (blk.shape[0], 1), 0)
    return jnp.sum(jnp.where(sub == h, blk, 0.0), axis=0, keepdims=True)


def _causal(tq, q0, k0):
    r = lax.broadcasted_iota(jnp.int32, (tq, tq), 0)
    c = lax.broadcasted_iota(jnp.int32, (tq, tq), 1)
    return (r + q0) >= (c + k0)


def _fused_call(body, *, name, out_shape, grid, in_specs, out_specs, scratch_shapes, sem, args, xchg):
    out_shape, in_specs, out_specs, scratch_shapes = list(out_shape), list(in_specs), list(out_specs), list(scratch_shapes)
    if xchg is None:
        res = pl.pallas_call(body, name=name, out_shape=out_shape, grid=grid, in_specs=in_specs, out_specs=out_specs,
                             scratch_shapes=scratch_shapes, compiler_params=_params(*sem))(*args)
        return list(res), []
    n_in, n_out, n_scr = len(in_specs), len(out_specs), len(scratch_shapes)

    def fused(*refs):
        ins, refs = refs[:n_in], refs[n_in:]
        xs, refs = refs[:xchg.n_src], refs[xchg.n_src:]
        outs, refs = refs[:n_out], refs[n_out:]
        xd, refs = refs[:xchg.n_dst], refs[xchg.n_dst:]
        scr, sems = refs[:n_scr], refs[n_scr:]
        first = last = None
        for d, g in enumerate(grid):
            i = pl.program_id(d)
            first = (i == 0) if first is None else first & (i == 0)
            last = (i == g - 1) if last is None else last & (i == g - 1)

        @pl.when(first)
        def _():
            xchg.start(xs, xd, sems)

        body(*ins, *outs, *scr)

        @pl.when(last)
        def _():
            xchg.finish(xs, xd, sems)

    res = pl.pallas_call(
        fused, name=name, out_shape=out_shape + xchg.out_shapes, grid=grid, in_specs=in_specs + xchg.in_specs,
        out_specs=out_specs + xchg.out_specs, scratch_shapes=scratch_shapes + xchg.scratch,
        compiler_params=_params(*["arbitrary"] * len(grid)),
    )(*args, *xchg.srcs)
    return list(res[:n_out]), list(res[n_out:])


def _fox_fwd(proj3, cum, *, name, xchg=None):
    B, T, _ = proj3.shape
    H = cum.shape[1]
    W = H * _FOX_HD
    npair = W // _LANES
    tq = _rows(T, _FOX_TQ)
    nq = T // tq

    def body(q_ref, k_ref, v_ref, cum_ref, o_ref, lse_ref):
        p = pl.program_id(1)
        i = pl.program_id(2)
        q0 = pl.multiple_of(i * tq, tq)
        lane = lax.broadcasted_iota(jnp.int32, (1, _LANES), 1)
        q2 = q_ref[0] * _FOX_SCALE
        heads = []
        for hh in range(2):
            msk = (lane < _FOX_HD) if hh == 0 else (lane >= _FOX_HD)
            h = 2 * p + hh
            heads.append((msk, h, jnp.where(msk, q2, 0.0).astype(_MXU), _cum_row(cum_ref, h, q0, _LANES)[:, 0:1]))

        def step(jj, carry, masked):
            k0 = pl.multiple_of(jj * tq, tq)
            k2 = k_ref[0, pl.ds(k0, tq), :].astype(_MXU)
            v2 = v_ref[0, pl.ds(k0, tq), :]
            out = []
            for (msk, h, qm, c0), (m_prev, l_prev, acc) in zip(heads, carry):
                s = _dot(qm, k2, 1, 1) + (c0 - _cum_row(cum_ref, h, k0, tq))
                if masked:
                    s = jnp.where(_causal(tq, q0, k0), s, -jnp.inf)
                m_new = jnp.maximum(m_prev, jnp.max(s, axis=1, keepdims=True))
                alpha = jnp.exp(m_prev - m_new)
                e = jnp.exp(s - m_new)
                l_new = alpha * l_prev + jnp.sum(e, axis=1, keepdims=True)
                vm = jnp.where(msk, v2, 0.0).astype(_MXU)
                out.append((m_new, l_new, alpha * acc + _dot(e.astype(_MXU), vm, 1, 0)))
            return tuple(out)

        init = tuple((jnp.full((tq, 1), -jnp.inf, _F32), jnp.zeros((tq, 1), _F32), jnp.zeros((tq, _LANES), _F32)) for _ in heads)
        carry = step(i, lax.fori_loop(0, i, functools.partial(step, masked=False), init), True)
        o2 = jnp.zeros((tq, _LANES), _F32)
        for hh, (m, l, acc) in enumerate(carry):
            o2 = o2 + acc / l
            lse_ref[0, hh] = jnp.broadcast_to(m + jnp.log(l), (tq, _LANES))
        o_ref[0] = o2.astype(o_ref.dtype)

    (o, lse), got = _fused_call(
        body, name=name,
        out_shape=(jax.ShapeDtypeStruct((B, T, W), _MXU), jax.ShapeDtypeStruct((B, H, T, _LANES), _F32)),
        grid=(B, npair, nq),
        in_specs=[pl.BlockSpec((1, tq, _LANES), lambda b, p, i: (b, i, p)),
                  pl.BlockSpec((1, T, _LANES), lambda b, p, i: (b, 0, npair + p)),
                  pl.BlockSpec((1, T, _LANES), lambda b, p, i: (b, 0, 2 * npair + p)),
                  pl.BlockSpec((1, H, T), lambda b, p, i: (b, 0, 0))],
        out_specs=(pl.BlockSpec((1, tq, _LANES), lambda b, p, i: (b, i, p)),
                   pl.BlockSpec((1, 2, tq, _LANES), lambda b, p, i: (b, p, i, 0))),
        scratch_shapes=[], sem=("parallel", "parallel", "parallel"), args=(proj3, proj3, proj3, cum), xchg=xchg)
    return o, lse, got


def _fox_bwd(proj3, cum, do3, lse, *, name, xchg=None):
    B, T, _ = proj3.shape
    H = cum.shape[1]
    W = H * _FOX_HD
    npair = W // _LANES
    tq = _rows(T, _FOX_TQ)
    nq = T // tq

    def body(q_ref, k_ref, v_ref, cum_ref, do_ref, lse_ref, dq_ref, dk_ref, dv_ref, dc_ref, p_scr, dp_scr, dk_acc, dv_acc, dc_acc):
        p = pl.program_id(1)
        i = pl.program_id(2)
        q0 = pl.multiple_of(i * tq, tq)
        lane = lax.broadcasted_iota(jnp.int32, (1, _LANES), 1)

        @pl.when(i == 0)
        def _():
            dk_acc[...] = jnp.zeros_like(dk_acc)
            dv_acc[...] = jnp.zeros_like(dv_acc)
            dc_acc[...] = jnp.zeros_like(dc_acc)

        q2 = q_ref[0] * _FOX_SCALE
        do2 = do_ref[0].astype(_F32)
        heads = []
        for hh in range(2):
            msk = (lane < _FOX_HD) if hh == 0 else (lane >= _FOX_HD)
            h = 2 * p + hh
            heads.append((hh, msk, h, jnp.where(msk, q2, 0.0).astype(_MXU), jnp.where(msk, do2, 0.0).astype(_MXU),
                          _cum_row(cum_ref, h, q0, _LANES)[:, 0:1], lse_ref[0, hh][:, 0:1]))

        def first(jj, deltas, masked):
            k0 = pl.multiple_of(jj * tq, tq)
            kb = k_ref[0, pl.ds(k0, tq), :].astype(_MXU)
            vb = v_ref[0, pl.ds(k0, tq), :].astype(_MXU)
            out = []
            for (hh, _, h, qm, dom, c0, lse_h), delta in zip(heads, deltas):
                s = _dot(qm, kb, 1, 1) + (c0 - _cum_row(cum_ref, h, k0, tq))
                pr = jnp.exp(s - lse_h)
                if masked:
                    pr = jnp.where(_causal(tq, q0, k0), pr, 0.0)
                dp = _dot(dom, vb, 1, 1)
                p_scr[hh, jj] = pr
                dp_scr[hh, jj] = dp
                out.append(delta + jnp.sum(pr * dp, axis=1, keepdims=True))
            return tuple(out)

        zero = tuple(jnp.zeros((tq, 1), _F32) for _ in heads)
        deltas = first(i, lax.fori_loop(0, i, functools.partial(first, masked=False), zero), True)

        def second(jj, dq):
            k0 = pl.multiple_of(jj * tq, tq)
            k2 = k_ref[0, pl.ds(k0, tq), :]
            dk = jnp.zeros((tq, _LANES), _F32)
            dv = jnp.zeros((tq, _LANES), _F32)
            for (hh, msk, _, qm, dom, _, _), delta in zip(heads, deltas):
                pr = p_scr[hh, jj]
                ds = pr * (dp_scr[hh, jj] - delta)
                dsb = ds.astype(_MXU)
                dv = dv + _dot(pr.astype(_MXU), dom, 0, 0)
                dk = dk + _dot(dsb, qm, 0, 0)
                dc_acc[hh:hh + 1, pl.ds(k0, tq)] += jnp.sum(ds, axis=0, keepdims=True)
                dq = dq + _dot(dsb, jnp.where(msk, k2, 0.0).astype(_MXU), 1, 0)
            dv_acc[pl.ds(k0, tq), :] += dv
            dk_acc[pl.ds(k0, tq), :] += dk
            return dq

        dq2 = lax.fori_loop(0, i + 1, second, jnp.zeros((tq, _LANES), _F32))
        dq_ref[0] = (dq2 * _FOX_SCALE).astype(dq_ref.dtype)

        @pl.when(i == nq - 1)
        def _():
            dk_ref[0] = dk_acc[...].astype(dk_ref.dtype)
            dv_ref[0] = dv_acc[...].astype(dv_ref.dtype)
            dc_ref[0, 0] = -dc_acc[...]

    full = lambda blk: pl.BlockSpec((1, T, _LANES), lambda b, p, i, blk=blk: (b, 0, blk * npair + p))
    part = lambda blk: pl.BlockSpec((1, tq, _LANES), lambda b, p, i, blk=blk: (b, i, blk * npair + p))
    (dq, dk, dv, dcum), got = _fused_call(
        body, name=name,
        out_shape=(jax.ShapeDtypeStruct((B, T, W), _MXU), jax.ShapeDtypeStruct((B, T, W), _MXU),
                   jax.ShapeDtypeStruct((B, T, W), _MXU), jax.ShapeDtypeStruct((B, npair, 2, T), _F32)),
        grid=(B, npair, nq),
        in_specs=[part(0), full(1), full(2), pl.BlockSpec((1, H, T), lambda b, p, i: (b, 0, 0)), part(0),
                  pl.BlockSpec((1, 2, tq, _LANES), lambda b, p, i: (b, p, i, 0))],
        out_specs=(part(0), full(0), full(0), pl.BlockSpec((1, 1, 2, T), lambda b, p, i: (b, p, 0, 0))),
        scratch_shapes=[pltpu.VMEM((2, nq, tq, tq), _F32), pltpu.VMEM((2, nq, tq, tq), _F32), pltpu.VMEM((T, _LANES), _F32),
                        pltpu.VMEM((T, _LANES), _F32), pltpu.VMEM((2, T), _F32)],
        sem=("parallel", "parallel", "arbitrary"), args=(proj3, proj3, proj3, cum, do3, lse), xchg=xchg)
    return dq, dk, dv, dcum, got


def _xa_probs(qh, kh, scale):
    s = _dot(qh, kh, 1, 1) * scale
    e = jnp.exp(s - jnp.max(s, axis=1, keepdims=True))
    return e / jnp.sum(e, axis=1, keepdims=True)


def _xa_fwd(q3, kv3, *, name):
    B, T, D = q3.shape
    M = kv3.shape[1]
    hd = D // _XA_HEADS
    scale = hd ** -0.5
    tq = _rows(T, 512)

    def body(q_ref, kv_ref, o_ref):
        for h in range(_XA_HEADS):
            sl = slice(h * hd, (h + 1) * hd)
            p = _xa_probs(q_ref[0, :, sl], kv_ref[0, :, sl], scale)
            o_ref[0, :, sl] = _dot(p.astype(_MXU), kv_ref[0, :, D + h * hd:D + (h + 1) * hd], 1, 0).astype(o_ref.dtype)

    return pl.pallas_call(
        body, name=name, out_shape=jax.ShapeDtypeStruct((B, T, D), _MXU), grid=(B, T // tq),
        in_specs=[pl.BlockSpec((1, tq, D), lambda b, i: (b, i, 0)), pl.BlockSpec((1, M, 2 * D), lambda b, i: (b, 0, 0))],
        out_specs=pl.BlockSpec((1, tq, D), lambda b, i: (b, i, 0)), compiler_params=_params("parallel", "parallel"),
    )(q3, kv3)


def _xa_bwd(q3, kv3, do3, *, name):
    B, T, D = q3.shape
    M = kv3.shape[1]
    hd = D // _XA_HEADS
    scale = hd ** -0.5
    tq = _rows(T, 512)

    def body(q_ref, kv_ref, do_ref, dq_ref, dkv_ref):
        @pl.when(pl.program_id(1) == 0)
        def _():
            dkv_ref[...] = jnp.zeros_like(dkv_ref)

        for h in range(_XA_HEADS):
            sl = slice(h * hd, (h + 1) * hd)
            slv = slice(D + h * hd, D + (h + 1) * hd)
            qh, kh, vh, doh = q_ref[0, :, sl], kv_ref[0, :, sl], kv_ref[0, :, slv], do_ref[0, :, sl]
            p = _xa_probs(qh, kh, scale)
            dkv_ref[0, :, slv] += _dot(p.astype(_MXU), doh, 0, 0)
            dp = _dot(doh, vh, 1, 1)
            ds = (p * (dp - jnp.sum(p * dp, axis=1, keepdims=True))).astype(_MXU)
            dq_ref[0, :, sl] = (_dot(ds, kh, 1, 0) * scale).astype(dq_ref.dtype)
            dkv_ref[0, :, sl] += _dot(ds, qh, 0, 0) * scale

    blk = pl.BlockSpec((1, tq, D), lambda b, i: (b, i, 0))
    kvs = pl.BlockSpec((1, M, 2 * D), lambda b, i: (b, 0, 0))
    return pl.pallas_call(
        body, name=name,
        out_shape=(jax.ShapeDtypeStruct((B, T, D), _MXU), jax.ShapeDtypeStruct((B, M, 2 * D), _F32)),
        grid=(B, T // tq), in_specs=[blk, kvs, blk], out_specs=(blk, kvs),
        compiler_params=_params("parallel", "arbitrary"),
    )(q3, kv3, do3)


def _rotated_copies(ext, rot, tt):
    rot[0] = ext[...]
    for b in range(1, 8):
        rot[b, 0:tt + _HALO - 8, :] = ext[b:b + tt + _HALO - 8, :]


def _shifted(rot, off, r0, rows, c0):
    a, b = divmod(off, 8)
    return rot[b, 8 * a + r0:8 * a + r0 + rows, c0:c0 + _LANES]


def _conv_fwd(y3, dw_w, dw_b, ln_g, ln_b, *, name, xchg=None):
    B, T, C = y3.shape
    tt = _rows(T, 256)
    nt = T // tt

    def body(prev_ref, cur_ref, w_ref, b_ref, g_ref, lb_ref, y2_ref, y4_ref, ext, rot):
        i = pl.program_id(1)
        ext[0:_HALO, :] = jnp.where(i > 0, prev_ref[0, tt - _HALO:tt, :], 0.0)
        ext[_HALO:_HALO + tt, :] = cur_ref[0]
        _rotated_copies(ext, rot, tt)
        for c0 in range(0, C, _LANES):
            acc = jnp.broadcast_to(b_ref[:, c0:c0 + _LANES], (tt, _LANES))
            for j in range(_CONV_K):
                acc = acc + w_ref[j:j + 1, c0:c0 + _LANES] * _shifted(rot, _HALO - (_CONV_K - 1) + j, 0, tt, c0)
            y2_ref[0, :, c0:c0 + _LANES] = acc
        xh, _ = _layernorm_stats(y2_ref[0])
        z = xh * g_ref[...] + lb_ref[...]
        y4_ref[0] = (z * _sigmoid(z)).astype(y4_ref.dtype)

    vec = pl.BlockSpec((1, C), lambda b, i: (0, 0))
    blk = pl.BlockSpec((1, tt, C), lambda b, i: (b, i, 0))
    (y2, y4), got = _fused_call(
        body, name=name,
        out_shape=(jax.ShapeDtypeStruct((B, T, C), _F32), jax.ShapeDtypeStruct((B, T, C), _MXU)),
        grid=(B, nt),
        in_specs=[pl.BlockSpec((1, tt, C), lambda b, i: (b, jnp.maximum(i - 1, 0), 0)), blk,
                  pl.BlockSpec((_HALO, C), lambda b, i: (0, 0)), vec, vec, vec],
        out_specs=(blk, blk),
        scratch_shapes=[pltpu.VMEM((tt + _HALO, C), _F32), pltpu.VMEM((8, tt + _HALO, C), _F32)],
        sem=("parallel", "parallel"), args=(y3, y3, dw_w, dw_b.reshape(1, C), ln_g.reshape(1, C), ln_b.reshape(1, C)), xchg=xchg)
    return y2, y4, got


def _conv_ln_bwd(y2, dy4, ln_g, ln_b, *, name):
    N, C = y2.shape
    tr = _rows(N, 256)

    def body(y_ref, d_ref, g_ref, b_ref, dy_ref, dg_ref, db_ref, dwb_ref):
        @pl.when(pl.program_id(0) == 0)
        def _():
            dg_ref[...] = jnp.zeros_like(dg_ref)
            db_ref[...] = jnp.zeros_like(db_ref)
            dwb_ref[...] = jnp.zeros_like(dwb_ref)

        xh, rstd = _layernorm_stats(y_ref[...])
        gv = g_ref[...]
        z = xh * gv + b_ref[...]
        sg = _sigmoid(z)
        dz = d_ref[...] * (sg * (1.0 + z * (1.0 - sg)))
        dg_ref[...] += jnp.sum(dz * xh, axis=0, keepdims=True)
        db_ref[...] += jnp.sum(dz, axis=0, keepdims=True)
        dxh = dz * gv
        dy = rstd * (dxh - jnp.mean(dxh, axis=-1, keepdims=True) - xh * jnp.mean(dxh * xh, axis=-1, keepdims=True))
        dwb_ref[...] += jnp.sum(dy, axis=0, keepdims=True)
        dy_ref[...] = dy

    row = pl.BlockSpec((tr, C), lambda i: (i, 0))
    vec = pl.BlockSpec((1, C), lambda i: (0, 0))
    v = jax.ShapeDtypeStruct((1, C), _F32)
    return pl.pallas_call(
        body, name=name, out_shape=(jax.ShapeDtypeStruct((N, C), _F32), v, v, v), grid=(N // tr,),
        in_specs=[row, row, vec, vec], out_specs=(row, vec, vec, vec), compiler_params=_params("arbitrary"),
    )(y2, dy4, ln_g.reshape(1, C), ln_b.reshape(1, C))


def _conv_bwd(y3, dy23, ag3, dw_w, *, name, xchg=None):
    B, T, C = y3.shape
    tt = _rows(T, 256)
    nt = T // tt

    rs = _rows(tt, 128)

    def groups(v):
        return jnp.sum(v.reshape(rs // 8, 8, _LANES), axis=0)

    def body(yp_ref, yc_ref, dc_ref, dn_ref, a_ref, g_ref, w_ref, dag_ref, dw_ref, dbin_ref, yext, dext, yrot, drot, dw_acc, db_acc):
        b = pl.program_id(0)
        i = pl.program_id(1)

        @pl.when((b == 0) & (i == 0))
        def _():
            dw_acc[...] = jnp.zeros_like(dw_acc)
            db_acc[...] = jnp.zeros_like(db_acc)

        yext[0:_HALO, :] = jnp.where(i > 0, yp_ref[0, tt - _HALO:tt, :], 0.0)
        yext[_HALO:_HALO + tt, :] = yc_ref[0]
        dext[0:tt, :] = dc_ref[0]
        dext[tt:tt + _HALO, :] = jnp.where(i < nt - 1, dn_ref[0, 0:_HALO, :], 0.0)
        _rotated_copies(yext, yrot, tt)
        _rotated_copies(dext, drot, tt)
        for c0 in range(0, C, _LANES):
            for r0 in range(0, tt, rs):
                d_cur = dext[r0:r0 + rs, c0:c0 + _LANES]
                dy = jnp.zeros((rs, _LANES), _F32)
                for j in range(_CONV_K):
                    sh = _CONV_K - 1 - j
                    dy = dy + w_ref[j:j + 1, c0:c0 + _LANES] * _shifted(drot, sh, r0, rs, c0)
                    dw_acc[j, :, c0:c0 + _LANES] += groups(d_cur * _shifted(yrot, _HALO - sh, r0, rs, c0))
                a, g = a_ref[0, r0:r0 + rs, c0:c0 + _LANES], g_ref[0, r0:r0 + rs, c0:c0 + _LANES]
                sg = _sigmoid(g)
                da = dy * sg
                dg = dy * a * (sg * (1.0 - sg))
                dag_ref[0, r0:r0 + rs, c0:c0 + _LANES] = da.astype(dag_ref.dtype)
                dag_ref[0, r0:r0 + rs, C + c0:C + c0 + _LANES] = dg.astype(dag_ref.dtype)
                db_acc[:, c0:c0 + _LANES] += groups(da)
                db_acc[:, C + c0:C + c0 + _LANES] += groups(dg)

        @pl.when((b == B - 1) & (i == nt - 1))
        def _():
            dw_ref[...] = jnp.sum(dw_acc[...], axis=1)
            dbin_ref[...] = jnp.sum(db_acc[...], axis=0, keepdims=True)

    blk = pl.BlockSpec((1, tt, C), lambda b, i: (b, i, 0))
    (dag, ddw, dbin), got = _fused_call(
        body, name=name,
        out_shape=(jax.ShapeDtypeStruct((B, T, 2 * C), _MXU), jax.ShapeDtypeStruct((_HALO, C), _F32),
                   jax.ShapeDtypeStruct((1, 2 * C), _F32)),
        grid=(B, nt),
        in_specs=[pl.BlockSpec((1, tt, C), lambda b, i: (b, jnp.maximum(i - 1, 0), 0)), blk, blk,
                  pl.BlockSpec((1, tt, C), lambda b, i: (b, jnp.minimum(i + 1, nt - 1), 0)),
                  pl.BlockSpec((None, 1, tt, C), lambda b, i: (0, b, i, 0)), pl.BlockSpec((None, 1, tt, C), lambda b, i: (1, b, i, 0)),
                  pl.BlockSpec((_HALO, C), lambda b, i: (0, 0))],
        out_specs=(pl.BlockSpec((1, tt, 2 * C), lambda b, i: (b, i, 0)), pl.BlockSpec((_HALO, C), lambda b, i: (0, 0)),
                   pl.BlockSpec((1, 2 * C), lambda b, i: (0, 0))),
        scratch_shapes=[pltpu.VMEM((tt + _HALO, C), _F32), pltpu.VMEM((tt + _HALO, C), _F32),
                        pltpu.VMEM((8, tt + _HALO, C), _F32), pltpu.VMEM((8, tt + _HALO, C), _F32),
                        pltpu.VMEM((_HALO, 8, C), _F32), pltpu.VMEM((8, 2 * C), _F32)],
        sem=("arbitrary", "arbitrary"), args=(y3, y3, dy23, dy23, ag3, ag3, dw_w), xchg=xchg)
    return dag, ddw, dbin, got


class _Exchange:
    def __init__(self, items):
        self.per_peer = [pp for _, pp in items]
        self.srcs, self.out_shapes, self.pieces = [], [], []
        for t, (srcs, per_peer) in enumerate(items):
            blk = srcs[0].shape[1:] if per_peer else srcs[0].shape
            self.out_shapes.append(jax.ShapeDtypeStruct((len(srcs), _N_DEV) + tuple(blk), srcs[0].dtype))
            for l, s in enumerate(srcs):
                self.pieces.append((t, l, len(self.srcs)))
                self.srcs.append(s)
        self.n_src, self.n_dst, n_pc = len(self.srcs), len(items), len(self.pieces)
        self.in_specs = [pl.BlockSpec(memory_space=pl.ANY)] * self.n_src
        self.out_specs = [pl.BlockSpec(memory_space=pl.ANY)] * self.n_dst
        self.scratch = [pltpu.SemaphoreType.DMA((n_pc, _N_DEV - 1)), pltpu.SemaphoreType.DMA((n_pc, _N_DEV - 1)),
                        pltpu.SemaphoreType.DMA((n_pc,))]

    def _copies(self, src_refs, dst_refs, sems, kind):
        send_sems, recv_sems, loc_sems = sems
        x, y, c = lax.axis_index("x"), lax.axis_index("y"), lax.axis_index("c")
        me = 4 * x + 2 * y + c
        out = []
        for i, (t, l, s) in enumerate(self.pieces):
            def src_for(p, s=s, t=t):
                return src_refs[s].at[p] if self.per_peer[t] else src_refs[s]

            if kind == "local":
                out.append(pltpu.make_async_copy(src_for(me), dst_refs[t].at[l, me], loc_sems.at[i]))
                continue
            for k in range(1, _N_DEV):
                px, py, pc = (1 - x if k & 4 else x), (1 - y if k & 2 else y), (1 - c if k & 1 else c)
                p = 4 * px + 2 * py + pc
                out.append(pltpu.make_async_remote_copy(
                    src_ref=src_for(p), dst_ref=dst_refs[t].at[l, p if kind == "recv" else me],
                    send_sem=send_sems.at[i, k - 1], recv_sem=recv_sems.at[i, k - 1],
                    device_id=(px, py, pc), device_id_type=pl.DeviceIdType.MESH))
        return out

    def start(self, src_refs, dst_refs, sems):
        for cp in self._copies(src_refs, dst_refs, sems, "local") + self._copies(src_refs, dst_refs, sems, "send"):
            cp.start()

    def finish(self, src_refs, dst_refs, sems):
        for cp in self._copies(src_refs, dst_refs, sems, "send"):
            cp.wait_send()
        for cp in self._copies(src_refs, dst_refs, sems, "recv"):
            cp.wait_recv()
        for cp in self._copies(src_refs, dst_refs, sems, "local"):
            cp.wait()


def _exchange(items, *, name):
    ex = _Exchange(items)

    def body(*refs):
        parts = refs[:ex.n_src], refs[ex.n_src:ex.n_src + ex.n_dst], refs[ex.n_src + ex.n_dst:]
        ex.start(*parts)
        ex.finish(*parts)

    return pl.pallas_call(
        body, name=name, out_shape=ex.out_shapes, in_specs=ex.in_specs, out_specs=ex.out_specs, scratch_shapes=ex.scratch,
        compiler_params=pltpu.CompilerParams(has_side_effects=True),
    )(*ex.srcs)


def _adam_update(g, w, m, v):
    c1 = 1.0 / (1.0 - _ADAM_B1 ** _ADAM_STEP)
    c2 = 1.0 / (1.0 - _ADAM_B2 ** _ADAM_STEP)
    m2 = _ADAM_B1 * m + (1.0 - _ADAM_B1) * g
    v2 = _ADAM_B2 * v + (1.0 - _ADAM_B2) * (g * g)
    return -_ADAM_LR * ((m2 * c1) / (jnp.sqrt(v2 * c2) + _ADAM_EPS) + _ADAM_WD * w), m2, v2


def _adamw_big(recvs, w, m, v, *, name):
    L, R, C = w.shape
    tr = _rows(R, 256)
    nb = R // tr

    def body(*refs):
        r_refs = refs[:L]
        w_ref, m_ref, v_ref, g_ref, d_ref, mo_ref, vo_ref = refs[L:]
        for l in range(L):
            @pl.when(pl.program_id(0) == l)
            def _(r_ref=r_refs[l]):
                g = r_ref[0, 0].astype(_F32)
                for k in range(1, _N_DEV):
                    g = g + r_ref[0, k].astype(_F32)
                g_ref[0] = g
                d_ref[0], mo_ref[0], vo_ref[0] = _adam_update(g, w_ref[0], m_ref[0], v_ref[0])

    def recv_spec(l):
        return pl.BlockSpec((1, _N_DEV, tr, C), lambda ll, i: (0, 0, jnp.where(ll == l, i, jnp.where(ll < l, 0, nb - 1)), 0))

    blk = pl.BlockSpec((1, tr, C), lambda l, i: (l, i, 0))
    o = jax.ShapeDtypeStruct((L, R, C), _F32)
    return pl.pallas_call(
        body, name=name, out_shape=(o, o, o, o), grid=(L, nb),
        in_specs=[recv_spec(l) for l in range(L)] + [blk, blk, blk], out_specs=(blk, blk, blk, blk),
        compiler_params=_params("arbitrary", "arbitrary"),
    )(*recvs, w, m, v)


def _adamw_small(tensors, *, name):
    n = len(tensors)
    lanes = [t[4] for t in tensors]
    layers = [len(t[0]) for t in tensors]

    def body(*refs):
        pos = 0
        ins = []
        for t in range(n):
            ins.append((refs[pos:pos + layers[t]], *refs[pos + layers[t]:pos + layers[t] + 3]))
            pos += layers[t] + 3
        outs = refs[pos:]
        for t in range(n):
            r_refs, w_ref, m_ref, v_ref = ins[t]
            g_ref, d_ref, mo_ref, vo_ref = outs[4 * t:4 * t + 4]
            for l in range(layers[t]):
                g = r_refs[l][0, 0]
                for k in range(1, _N_DEV):
                    g = g + r_refs[l][0, k]
                if lanes[t] is not None:
                    g = g[..., :lanes[t]]
                g_ref[l] = g
                d_ref[l], mo_ref[l], vo_ref[l] = _adam_update(g, w_ref[l], m_ref[l], v_ref[l])

    args, out_shape = [], []
    for recvs, w, m, v, _ in tensors:
        args += [*recvs, w, m, v]
        out_shape += [jax.ShapeDtypeStruct(w.shape, _F32)] * 4
    outs = pl.pallas_call(
        body, name=name, out_shape=out_shape,
        in_specs=[pl.BlockSpec(memory_space=pltpu.VMEM)] * len(args), out_specs=[pl.BlockSpec(memory_space=pltpu.VMEM)] * len(out_shape),
        compiler_params=_params(),
    )(*args)
    return [tuple(outs[4 * t:4 * t + 4]) for t in range(n)]


_BIG = (("w_in_e", 2), ("w_out_e", 1), ("conv_w_in", 2), ("conv_w_out", 1), ("xa_wq", 1), ("xa_wkv", 2), ("xa_wo", 1),
        ("ffn_w_gu", 2), ("ffn_w_down", 1))
_SMALL_SHARDED = (("mix_norm_o", 1), ("conv_b_in", 1), ("conv_dw_w", 2), ("conv_dw_b", 1), ("conv_ln_g", 1),
                  ("conv_ln_b", 1), ("conv_b_out", 1))
_REPLICATED = ("mix_norm_e", "fox_f_bias", "gmlp_ln_g", "gmlp_ln_b", "gmlp_w_s", "gmlp_b_s", "xa_norm", "mem_norm",
               "ffn_norm", "final_norm")
_WEIGHTS = ("mix_norm_e", "w_in_e", "fox_f_bias", "gmlp_ln_g", "gmlp_ln_b", "gmlp_w_s", "gmlp_b_s", "w_out_e", "mix_norm_o",
            "conv_w_in", "conv_b_in", "conv_dw_w", "conv_dw_b", "conv_ln_g", "conv_ln_b", "conv_w_out", "conv_b_out",
            "xa_norm", "mem_norm", "xa_wq", "xa_wkv", "xa_wo", "ffn_norm", "ffn_w_gu", "ffn_w_down", "final_norm")


def _cols_to_peers(g, n=_N_DEV):
    K, N = g.shape[-2:]
    return jnp.swapaxes(g.reshape(g.shape[:-1] + (n, N // n)), -3, -2)


def _weight_items(pieces, wsrc):
    return [([wsrc[n][l]] if n in dict(_BIG) else [wsrc[n]], False) for n, l in pieces]


def _place_weights(P, pieces, gathered):
    axis = dict(_BIG + _SMALL_SHARDED)
    for (n, l), g in zip(pieces, gathered):
        if n in dict(_BIG):
            P.setdefault(n, {})[l] = g.reshape(1, -1, g.shape[-1]) if axis[n] == 1 else _peers_to_cols(g)
        else:
            P[n] = _peers_to_cols(g[0, :, 0])[None] if axis[n] == 2 else g.reshape(1, -1)


def _grad_items(pieces, G):
    axis = dict(_BIG + _SMALL_SHARDED)
    items = []
    for n, l in pieces:
        g = G[n][l]
        if n in _REPLICATED:
            items.append(([g], False))
        elif n == "ffn_w_gu":
            half = _N_DEV // 2
            items.append(([jnp.concatenate([_cols_to_peers(g[0], half), _cols_to_peers(g[1], half)], axis=0)], True))
        elif n in dict(_BIG):
            items.append(([g.reshape(_N_DEV, -1, g.shape[-1]) if axis[n] == 1 else _cols_to_peers(g)], True))
        else:
            items.append(([_cols_to_peers(g) if axis[n] == 2 else g.reshape(_N_DEV, 1, -1)], True))
    return items


def _peers_to_cols(d):
    K, c = d.shape[-2:]
    return jnp.swapaxes(d, -3, -2).reshape(d.shape[:-3] + (K, _N_DEV * c))


def _local_step(x, mem, tgt, P, wsrc=None, fwd_hooks=None, bwd_hooks=None):
    fwd_hooks, bwd_hooks = fwd_hooks or {}, bwd_hooks or {}
    sent = {}

    def gather(kernel_name):
        return _Exchange(_weight_items(fwd_hooks[kernel_name], wsrc)) if kernel_name in fwd_hooks else None

    def placed(kernel_name, got):
        if kernel_name in fwd_hooks:
            _place_weights(P, fwd_hooks[kernel_name], got)

    def scatter(kernel_name):
        return _Exchange(_grad_items(bwd_hooks[kernel_name], G)) if kernel_name in bwd_hooks else None

    def received(kernel_name, got):
        if kernel_name in bwd_hooks:
            sent.update(zip(bwd_hooks[kernel_name], got))

    B, T, D = x.shape
    M = mem.shape[1]
    N = B * T
    W = D // 2
    H = W // _FOX_HD
    f_blk = 5 * W // _LANES
    G = {}
    x0 = x.reshape(N, D)
    memf = mem.reshape(B * M, D)

    h_e = _rms_fwd(x0, P["mix_norm_e"][0], name="rms_mix_e", xchg=gather("rms_mix_e"))
    if "rms_mix_e" in fwd_hooks:
        h_e, got = h_e
        placed("rms_mix_e", got)
    w_in_pad = _pad_w_in(P["w_in_e"][0][0], W, H)[None]
    proj = _mm(h_e, w_in_pad, bl=0, name="mm_in_e", tn=896)
    proj3 = proj.reshape(B, T, -1)
    cum = _fox_cum(proj3, f_blk, P["fox_f_bias"][0], name="fox_cum")
    o_fox, lse, got = _fox_fwd(proj3, cum, name="fox_fwd", xchg=gather("fox_fwd"))
    placed("fox_fwd", got)
    bias_full = jnp.repeat(P["gmlp_b_s"][0].T, _GRP, axis=1)
    a_out = _gmlp_fwd(proj, P["gmlp_ln_g"][0], P["gmlp_ln_b"][0], P["gmlp_w_s"][0], bias_full, name="gmlp_fwd")
    mixcat = jnp.concatenate([o_fox.reshape(N, W), a_out], axis=1)
    x1 = _mm(mixcat, P["w_out_e"][0], bl=0, res=x0, name="mm_out_e")

    def xa_ffn_fwd(xin, l):
        s = {}
        s["h_xa"] = _rms_fwd(xin, P["xa_norm"][l], name=f"rms_xa{l}")
        s["q"] = _mm(s["h_xa"], P["xa_wq"][l], bl=0, out_dtype=_MXU, name=f"mm_q{l}")
        s["mn"] = _rms_fwd(memf, P["mem_norm"][l], name=f"rms_mem{l}")
        s["kv"] = _mm(s["mn"], P["xa_wkv"][l], bl=0, out_dtype=_MXU, name=f"mm_kv{l}")
        s["o"] = _xa_fwd(s["q"].reshape(B, T, D), s["kv"].reshape(B, M, 2 * D), name=f"xa_fwd{l}").reshape(N, D)
        s["x_mid"] = _mm(s["o"], P["xa_wo"][l], bl=0, res=xin, name=f"mm_o{l}")
        s["h_ffn"] = _rms_fwd(s["x_mid"], P["ffn_norm"][l], name=f"rms_ffn{l}")
        s["gu"], s["act"], got = _mm_gu(s["h_ffn"], P["ffn_w_gu"][l], 0, name=f"mm_gu{l}", xchg=gather(f"mm_gu{l}"))
        placed(f"mm_gu{l}", got)
        s["x_in"] = xin
        xout = _mm(s["act"], P["ffn_w_down"][l], bl=0, res=s["x_mid"], name=f"mm_down{l}", tn=512, xchg=gather(f"mm_down{l}"))
        if f"mm_down{l}" in fwd_hooks:
            xout, got = xout
            placed(f"mm_down{l}", got)
        return xout, s

    x3, s0 = xa_ffn_fwd(x1, 0)
    h_o = _rms_fwd(x3, P["mix_norm_o"][0], name="rms_mix_o")
    ag, y, _ = _mm_gu(h_o, P["conv_w_in"][0], 0, bias=P["conv_b_in"][0], glu=True, keep=_F32, name="mm_conv_in")
    C = y.shape[1]
    dw_w = jnp.pad(P["conv_dw_w"][0], ((0, _HALO - _CONV_K), (0, 0)))
    y2, y4, got = _conv_fwd(y.reshape(B, T, C), dw_w, P["conv_dw_b"][0], P["conv_ln_g"][0], P["conv_ln_b"][0], name="conv_fwd",
                            xchg=gather("conv_fwd"))
    placed("conv_fwd", got)
    x4 = _mm(y4.reshape(N, C), P["conv_w_out"][0], bl=0, bias=P["conv_b_out"][0], res=x3, name="mm_conv_out")
    x6, s1 = xa_ffn_fwd(x4, 1)
    loss, dx, dg = _final_loss(x6, P["final_norm"], tgt.reshape(N, D), name="final_loss")
    G["final_norm"] = [dg]

    def xa_ffn_bwd(dx, s, l):
        g = {}
        dgu = _mm_dgu(dx, P["ffn_w_down"][l], 0, s["gu"], name=f"mm_dgu{l}")
        g["ffn_w_down"] = _mm(s["act"], dx, ta=True, out_dtype=_MXU, name=f"mm_dwdown{l}", tm=1408)
        g["ffn_w_gu"] = (_mm(s["h_ffn"], dgu, ta=True, bl=0, out_dtype=_MXU, name=f"mm_dwg{l}", tn=1408),
                         _mm(s["h_ffn"], dgu, ta=True, bl=1, out_dtype=_MXU, name=f"mm_dwu{l}", tn=1408))
        dx, g["ffn_norm"] = _mm(dgu, P["ffn_w_gu"][l], al="cat", bl=0, tb=True, rms_bwd=(s["x_mid"], P["ffn_norm"][l], dx),
                                name=f"mm_dhffn{l}", tm=512, tn=D, tk=1408)
        do = _mm(dx, P["xa_wo"][l], bl=0, tb=True, out_dtype=_MXU, name=f"mm_do{l}")
        g["xa_wo"] = _mm(s["o"], dx, ta=True, out_dtype=_MXU, name=f"mm_dwo{l}")
        dq, dkv = _xa_bwd(s["q"].reshape(B, T, D), s["kv"].reshape(B, M, 2 * D), do.reshape(B, T, D), name=f"xa_bwd{l}")
        dq, dkv = dq.reshape(N, D), dkv.reshape(B * M, 2 * D)
        g["xa_wq"] = _mm(s["h_xa"], dq, ta=True, out_dtype=_MXU, name=f"mm_dwq{l}")
        dx, g["xa_norm"] = _mm(dq, P["xa_wq"][l], bl=0, tb=True, rms_bwd=(s["x_in"], P["xa_norm"][l], dx), name=f"mm_dhxa{l}",
                               tm=512, tn=D)
        g["xa_wkv"] = _mm(s["mn"], dkv, ta=True, out_dtype=_MXU, name=f"mm_dwkv{l}")
        dmn = _mm(dkv, P["xa_wkv"][l], bl=0, tb=True, name=f"mm_dmn{l}")
        g["mem_norm"] = _rms_bwd(memf, P["mem_norm"][l], dmn, None, name=f"rms_mem_bwd{l}")
        return dx, g

    dx, g1 = xa_ffn_bwd(dx, s1, 1)
    for k in g1:
        G[k] = {1: g1[k]}
    G["conv_b_out"] = [_colsum(dx, name="colsum_b_out")]
    dy4 = _mm(dx, P["conv_w_out"][0], bl=0, tb=True, name="mm_dy4")
    G["conv_w_out"] = [_mm(y4.reshape(N, C), dx, ta=True, out_dtype=_MXU, name="mm_dwconv_out")]
    dy2, dlg, dlb, ddb = _conv_ln_bwd(y2.reshape(N, C), dy4, P["conv_ln_g"][0], P["conv_ln_b"][0], name="conv_ln_bwd")
    G["conv_ln_g"], G["conv_ln_b"], G["conv_dw_b"] = [dlg], [dlb], [ddb]
    dag, ddw, dbin, got = _conv_bwd(y.reshape(B, T, C), dy2.reshape(B, T, C), ag.reshape(2, B, T, C), dw_w, name="conv_bwd",
                                    xchg=scatter("conv_bwd"))
    received("conv_bwd", got)
    G["conv_dw_w"], G["conv_b_in"] = [ddw[:_CONV_K]], [dbin]
    dag = dag.reshape(N, 2 * C)
    G["conv_w_in"] = [_mm(h_o, dag, ta=True, out_dtype=_MXU, name="mm_dwconv_in")]
    dx, dg = _mm(dag, P["conv_w_in"][0], bl=0, tb=True, rms_bwd=(x3, P["mix_norm_o"][0], dx), name="mm_dh_o", tm=512, tn=D)
    G["mix_norm_o"] = [dg]
    dx, g0 = xa_ffn_bwd(dx, s0, 0)
    for k in g0:
        G[k][0] = g0[k]
    G["w_out_e"] = [_mm(mixcat, dx, ta=True, out_dtype=_MXU, name="mm_dwout_e")]
    dmix = _mm(dx, P["w_out_e"][0], bl=0, tb=True, name="mm_dmix")
    dz, dlg, dlb, dws, dbias = _gmlp_bwd(proj, dmix, 1, P["gmlp_ln_g"][0], P["gmlp_ln_b"][0], P["gmlp_w_s"][0], bias_full,
                                         name="gmlp_bwd")
    G["gmlp_ln_g"], G["gmlp_ln_b"], G["gmlp_w_s"] = [dlg], [dlb], [dws]
    G["gmlp_b_s"] = [dbias[:, :2 * (W // _LANES)].T]
    dmix3 = dmix.reshape(B, T, D)
    dq, dk, dv, dcum, got = _fox_bwd(proj3, cum, dmix3, lse, name="fox_bwd", xchg=scatter("fox_bwd"))
    received("fox_bwd", got)
    df, dfb = _fox_cum_bwd(proj3, f_blk, P["fox_f_bias"][0], dcum.reshape(B, H, T), name="fox_cum_bwd")
    G["fox_f_bias"] = [dfb]
    dproj = jnp.concatenate([dq.reshape(N, W), dk.reshape(N, W), dv.reshape(N, W), dz, df.reshape(N, _LANES).astype(_MXU)], axis=1)
    G["w_in_e"] = [_unpad_w_in(_mm(h_e, dproj, ta=True, out_dtype=_MXU, name="mm_dwin_e", tn=896), W, H)]
    out = _mm(dproj, w_in_pad, bl=0, tb=True, rms_bwd=(x0, P["mix_norm_e"][0], dx), name="mm_dh_e", tm=512, tn=D,
              xchg=scatter("mm_dh_e"))
    if "mm_dh_e" in bwd_hooks:
        out, got = out
        received("mm_dh_e", got)
    dx, dg = out
    G["mix_norm_e"] = [dg]
    return loss, dx.reshape(B, T, D), G, sent


def _pad_w_in(w_in, W, H):
    f = w_in[:, 3 * W:3 * W + H]
    return jnp.concatenate([w_in[:, :3 * W], w_in[:, 3 * W + H:], jnp.pad(f, ((0, 0), (0, _LANES - H)))], axis=1)


def _unpad_w_in(g, W, H):
    return jnp.concatenate([g[:, :3 * W], g[:, 5 * W:5 * W + H], g[:, 3 * W:5 * W]], axis=1)


def kernel(x, mem, mix_norm_e, w_in_e, fox_f_bias, gmlp_ln_g, gmlp_ln_b, gmlp_w_s, gmlp_b_s, w_out_e, mix_norm_o, conv_w_in, conv_b_in, conv_dw_w, conv_dw_b, conv_ln_g, conv_ln_b, conv_w_out, conv_b_out, xa_norm, mem_norm, xa_wq, xa_wkv, xa_wo, ffn_norm, ffn_w_gu, ffn_w_down, final_norm, loss_target, m_mix_norm_e, m_w_in_e, m_fox_f_bias, m_gmlp_ln_g, m_gmlp_ln_b, m_gmlp_w_s, m_gmlp_b_s, m_w_out_e, m_mix_norm_o, m_conv_w_in, m_conv_b_in, m_conv_dw_w, m_conv_dw_b, m_conv_ln_g, m_conv_ln_b, m_conv_w_out, m_conv_b_out, m_xa_norm, m_mem_norm, m_xa_wq, m_xa_wkv, m_xa_wo, m_ffn_norm, m_ffn_w_gu, m_ffn_w_down, m_final_norm, v_mix_norm_e, v_w_in_e, v_fox_f_bias, v_gmlp_ln_g, v_gmlp_ln_b, v_gmlp_w_s, v_gmlp_b_s, v_w_out_e, v_mix_norm_o, v_conv_w_in, v_conv_b_in, v_conv_dw_w, v_conv_dw_b, v_conv_ln_g, v_conv_ln_b, v_conv_w_out, v_conv_b_out, v_xa_norm, v_mem_norm, v_xa_wq, v_xa_wkv, v_xa_wo, v_ffn_norm, v_ffn_w_gu, v_ffn_w_down, v_final_norm):
    env = dict(locals())
    w = {n: env[n] for n in _WEIGHTS}
    mom = {n: env["m_" + n] for n in _WEIGHTS}
    var = {n: env["v_" + n] for n in _WEIGHTS}
    D = x.shape[-1]
    W = D // 2
    H = W // _FOX_HD

    def layers(n):
        return w[n].shape[0] if w[n].ndim > 1 else 1

    wsrc = {n: (w[n].astype(_MXU) if n in dict(_BIG) else w[n]) for n, _ in _BIG + _SMALL_SHARDED}
    P = {n: w[n] for n in _REPLICATED}
    fwd_hooks = {
        "rms_mix_e": [("w_in_e", 0)],
        "fox_fwd": [("w_out_e", 0), ("xa_wq", 0), ("xa_wkv", 0), ("xa_wo", 0), ("ffn_w_gu", 0)],
        "mm_gu0": [("ffn_w_down", 0), ("conv_w_in", 0), ("conv_w_out", 0)] + [(n, 0) for n, _ in _SMALL_SHARDED],
        "mm_down0": [("xa_wq", 1), ("xa_wkv", 1)],
        "conv_fwd": [("xa_wo", 1), ("ffn_w_gu", 1)],
        "mm_gu1": [("ffn_w_down", 1)],
    }

    last = [("mix_norm_e", 0)]
    in_dh_e = [("w_in_e", 0), ("fox_f_bias", 0)]
    in_conv = [(n, 1) for n in ("ffn_w_gu", "ffn_w_down", "xa_wq", "xa_wkv", "xa_wo", "xa_norm", "mem_norm", "ffn_norm")]
    in_conv += [("final_norm", 0), ("conv_w_out", 0)]
    every = [(n, l) for n in [n for n, _ in _BIG + _SMALL_SHARDED] + list(_REPLICATED) for l in range(layers(n))]
    bwd_hooks = {"conv_bwd": in_conv, "mm_dh_e": in_dh_e,
                 "fox_bwd": [pc for pc in every if pc not in last + in_dh_e + in_conv]}
    loss, grad_x, G, recv = _local_step(x, mem, loss_target, P, wsrc, fwd_hooks, bwd_hooks)
    loss = lax.psum(loss[0, 0], ("x", "y", "c"))
    recv.update(zip(last, _exchange(_grad_items(last, G), name="scatter_last")))

    def partials(n):
        return [recv[(n, l)] for l in range(layers(n))]

    res = {n: _adamw_big(partials(n), w[n], mom[n], var[n], name="adamw_" + n) for n, _ in _BIG}
    small = [n for n, _ in _SMALL_SHARDED] + list(_REPLICATED)

    def rows(a, n):
        r = recv[(n, 0)]
        return a.reshape((layers(n),) + r.shape[2:-1] + (-1,))

    outs = _adamw_small([(partials(n), rows(w[n], n), rows(mom[n], n), rows(var[n], n),
                          w[n].shape[-1] if w[n].shape[-1] != recv[(n, 0)].shape[-1] else None) for n in small], name="adamw_small")
    for n, o in zip(small, outs):
        res[n] = tuple(a.reshape(w[n].shape) for a in o)
    return (loss, grad_x, *[res[n][0] for n in _WEIGHTS], *[res[n][1] for n in _WEIGHTS],
            *[res[n][2] for n in _WEIGHTS], *[res[n][3] for n in _WEIGHTS])
```

```python
import functools
import math

import jax
import jax.numpy as jnp
from jax import lax
from jax.experimental import pallas as pl
from jax.experimental.pallas import tpu as pltpu

_F32 = jnp.float32
_MXU = jnp.bfloat16
_VMEM_LIMIT = 48 * 1024 * 1024
_LANES = 128
_EPS = 1e-6
_N_DEV = 8
_FOX_HD = 64
_FOX_SCALE = _FOX_HD ** -0.5
_FOX_TQ = 512
_CHUNK = 128
_GRP = 64
_CONV_K = 31
_HALO = 32
_XA_HEADS = 4
_GELU_C = math.sqrt(2.0 / math.pi)
_ADAM_LR, _ADAM_B1, _ADAM_B2, _ADAM_EPS, _ADAM_WD, _ADAM_STEP = 0.001, 0.9, 0.999, 1e-08, 0.01, 10
_FLAT_W = 1024
_FLAT_ALIGN = 16 * _FLAT_W
_BIG_ROWS = 128


def _params(*sem):
    return pltpu.CompilerParams(dimension_semantics=sem if sem else None, vmem_limit_bytes=_VMEM_LIMIT)


def _pick(n, pref):
    if n <= pref:
        return n
    best = None
    for t in range(_LANES, pref + 1, _LANES):
        if n % t == 0:
            best = t
    assert best is not None, (n, pref)
    return best


def _rows(n, pref):
    if n <= pref:
        return n
    t = pref
    while n % t:
        t //= 2
    assert t >= 8, (n, pref)
    return t


def _sigmoid(x):
    return 1.0 / (1.0 + jnp.exp(-x))


def _gelu(x):
    t = jnp.tanh(_GELU_C * (x + 0.044715 * (x * x * x)))
    return 0.5 * x * (1.0 + t)


def _gelu_grad(x):
    x2 = x * x
    t = jnp.tanh(_GELU_C * (x + 0.044715 * (x2 * x)))
    return 0.5 * (1.0 + t) + 0.5 * x * (1.0 - t * t) * (_GELU_C * (1.0 + 3.0 * 0.044715 * x2))


def _dot(a, b, ca, cb):
    return lax.dot_general(a, b, (((ca,), (cb,)), ((), ())), preferred_element_type=_F32)


def _rms_rows(xv, gain):
    return (xv * lax.rsqrt(jnp.mean(xv * xv, axis=-1, keepdims=True) + _EPS) * gain).astype(_MXU)


def _mm(a, b, *, name, ta=False, tb=False, al=None, bl=None, bk0=0, bias=None, res=None, rms_bwd=None, rms_fwd=None,
        out_dtype=_F32, tm=1024, tn=512, tk=1024, xchg=None):
    if ta:
        K, M = a.shape[-2:]
    else:
        M, K = a.shape[-2:]
    if al == "cat":
        assert not ta
        K = a.shape[0] * a.shape[-1]
    if tb:
        N, K2 = b.shape[-2:]
    else:
        K2, N = b.shape[-2:]
    assert K == K2 or (tb and K2 > K), (a.shape, b.shape, ta, tb)
    tm, tn = _pick(M, tm), _pick(N, tn)
    tk = K if (not ta and K <= 2816 and K2 == K) else _pick(a.shape[-1] if al == "cat" else K, tk)
    nk = K // tk
    assert bk0 % tk == 0
    kb = bk0 // tk
    grid = (M // tm, N // tn, nk)
    if al == "cat":
        per = a.shape[-1] // tk
        a_spec = pl.BlockSpec((None, tm, tk), lambda i, j, k: (k // per, i, k % per))
    elif a.ndim == 3:
        a_spec = (pl.BlockSpec((None, tk, tm), lambda i, j, k: (al, k, i)) if ta
                  else pl.BlockSpec((None, tm, tk), lambda i, j, k: (al, i, k)))
    else:
        a_spec = pl.BlockSpec((tk, tm), lambda i, j, k: (k, i)) if ta else pl.BlockSpec((tm, tk), lambda i, j, k: (i, k))
    if b.ndim == 3:
        b_spec = (pl.BlockSpec((None, tn, tk), lambda i, j, k: (bl, j, k + kb)) if tb
                  else pl.BlockSpec((None, tk, tn), lambda i, j, k: (bl, k, j)))
    else:
        b_spec = pl.BlockSpec((tn, tk), lambda i, j, k: (j, k)) if tb else pl.BlockSpec((tk, tn), lambda i, j, k: (k, j))
    in_specs, args = [a_spec, b_spec], [a, b]
    if bias is not None:
        in_specs.append(pl.BlockSpec((1, tn), lambda i, j, k: (0, j)))
        args.append(bias.reshape(1, N).astype(_F32))
    if res is not None:
        in_specs.append(pl.BlockSpec((tm, tn), lambda i, j, k: (i, j)))
        args.append(res)
    has_bias, has_res, has_rms = bias is not None, res is not None, rms_bwd is not None
    if has_rms:
        assert tn == N, (tn, N)
        x, g, dres = rms_bwd
        in_specs += [pl.BlockSpec((tm, N), lambda i, j, k: (i, 0)), pl.BlockSpec((1, N), lambda i, j, k: (0, 0)),
                     pl.BlockSpec((tm, N), lambda i, j, k: (i, 0))]
        args += [x, g.reshape(1, N), dres]
    has_norm = rms_fwd is not None
    if has_norm:
        assert not ta and nk == 1 and a.ndim == 2
        in_specs.append(pl.BlockSpec((1, K), lambda i, j, k: (0, 0)))
        args.append(rms_fwd.reshape(1, K))

    def body(*refs):
        a_ref, b_ref = refs[0], refs[1]
        pos = 2
        bias_ref = res_ref = None
        if has_bias:
            bias_ref = refs[pos]
            pos += 1
        if has_res:
            res_ref = refs[pos]
            pos += 1
        if has_rms:
            x_ref, g_ref, dres_ref = refs[pos:pos + 3]
            pos += 3
        if has_norm:
            gain_ref = refs[pos]
            pos += 1
        o_ref = refs[pos]
        pos += 1
        if has_rms:
            dg_ref = refs[pos]
            pos += 1
        if has_norm:
            h_ref = refs[pos]
            pos += 1
        acc_ref = refs[pos] if nk > 1 else None
        first_rows = pl.program_id(0) == 0
        if has_norm:
            av = _rms_rows(a_ref[...], gain_ref[...])
            h_ref[...] = av
        else:
            av = a_ref[...].astype(_MXU)
        p = _dot(av, b_ref[...].astype(_MXU), 0 if ta else 1, 1 if tb else 0)

        def finish(acc):
            if has_bias:
                acc = acc + bias_ref[...]
            if has_res:
                acc = acc + res_ref[...]
            if has_rms:
                @pl.when(first_rows)
                def _():
                    dg_ref[...] = jnp.zeros_like(dg_ref)

                xv = x_ref[...]
                r = lax.rsqrt(jnp.mean(xv * xv, axis=-1, keepdims=True) + _EPS)
                xh = xv * r
                dg_ref[...] += jnp.sum(acc * xh, axis=0, keepdims=True)
                dxn = acc * g_ref[...]
                acc = dres_ref[...] + r * (dxn - xh * jnp.mean(dxn * xh, axis=-1, keepdims=True))
            o_ref[...] = acc.astype(o_ref.dtype)

        if nk == 1:
            finish(p)
        else:
            k = pl.program_id(2)

            @pl.when(k == 0)
            def _():
                acc_ref[...] = p

            @pl.when(k > 0)
            def _():
                acc_ref[...] += p

            @pl.when(k == nk - 1)
            def _():
                finish(acc_ref[...])

    out_shape = [jax.ShapeDtypeStruct((M, N), out_dtype)]
    out_specs = [pl.BlockSpec((tm, tn), lambda i, j, k: (i, j))]
    if has_rms:
        out_shape.append(jax.ShapeDtypeStruct((1, N), _F32))
        out_specs.append(pl.BlockSpec((1, N), lambda i, j, k: (0, 0)))
    if has_norm:
        out_shape.append(jax.ShapeDtypeStruct((M, K), _MXU))
        out_specs.append(pl.BlockSpec((tm, K), lambda i, j, k: (i, 0)))
    outs, got = _fused_call(
        body, name=name, out_shape=out_shape, grid=grid, in_specs=in_specs, out_specs=out_specs,
        scratch_shapes=[pltpu.VMEM((tm, tn), _F32)] if nk > 1 else [],
        sem=("arbitrary",) * 3 if has_rms or has_norm else ("parallel", "parallel", "arbitrary"), args=args, xchg=xchg)
    out = tuple(outs) if has_rms or has_norm else outs[0]
    return out if xchg is None else (out, got)


def _rms_fwd(x, g, *, name, xchg=None):
    N, D = x.shape
    tr = _rows(N, 512)

    def body(x_ref, g_ref, o_ref):
        xv = x_ref[...]
        r = lax.rsqrt(jnp.mean(xv * xv, axis=-1, keepdims=True) + _EPS)
        o_ref[...] = (xv * r * g_ref[...]).astype(o_ref.dtype)

    (out,), got = _fused_call(
        body, name=name, out_shape=[jax.ShapeDtypeStruct((N, D), _MXU)], grid=(N // tr,),
        in_specs=[pl.BlockSpec((tr, D), lambda i: (i, 0)), pl.BlockSpec((1, D), lambda i: (0, 0))],
        out_specs=[pl.BlockSpec((tr, D), lambda i: (i, 0))], scratch_shapes=[], sem=("parallel",),
        args=(x, g.reshape(1, D)), xchg=xchg)
    return out if xchg is None else (out, got)


def _rms_bwd(x, g, dh, dres, *, name):
    N, D = x.shape
    tr = _rows(N, 256)
    has_res = dres is not None

    def body(*refs):
        if has_res:
            x_ref, g_ref, dh_ref, dres_ref, dx_ref, dg_ref = refs
        else:
            x_ref, g_ref, dh_ref, dg_ref = refs
        xv = x_ref[...]
        r = lax.rsqrt(jnp.mean(xv * xv, axis=-1, keepdims=True) + _EPS)
        xh = xv * r
        dhv = dh_ref[...].astype(_F32)

        @pl.when(pl.program_id(0) == 0)
        def _():
            dg_ref[...] = jnp.zeros_like(dg_ref)

        dg_ref[...] += jnp.sum(dhv * xh, axis=0, keepdims=True)
        if has_res:
            dxn = dhv * g_ref[...]
            dx = r * (dxn - xh * jnp.mean(dxn * xh, axis=-1, keepdims=True))
            dx_ref[...] = dres_ref[...] + dx

    row = pl.BlockSpec((tr, D), lambda i: (i, 0))
    vec = pl.BlockSpec((1, D), lambda i: (0, 0))
    if has_res:
        out_shape = (jax.ShapeDtypeStruct((N, D), _F32), jax.ShapeDtypeStruct((1, D), _F32))
        out_specs = (row, vec)
        in_specs, args = [row, vec, row, row], (x, g.reshape(1, D), dh, dres)
    else:
        out_shape = jax.ShapeDtypeStruct((1, D), _F32)
        out_specs = vec
        in_specs, args = [row, vec, row], (x, g.reshape(1, D), dh)
    return pl.pallas_call(
        body, name=name, out_shape=out_shape, grid=(N // tr,), in_specs=in_specs, out_specs=out_specs,
        compiler_params=_params("arbitrary"),
    )(*args)


def _colsum(a, *, name):
    M, C = a.shape
    tr = _rows(M, 512)

    def body(a_ref, o_ref):
        @pl.when(pl.program_id(0) == 0)
        def _():
            o_ref[...] = jnp.zeros_like(o_ref)

        o_ref[...] += jnp.sum(a_ref[...].astype(_F32), axis=0, keepdims=True)

    return pl.pallas_call(
        body, name=name, out_shape=jax.ShapeDtypeStruct((1, C), _F32), grid=(M // tr,),
        in_specs=[pl.BlockSpec((tr, C), lambda i: (i, 0))], out_specs=pl.BlockSpec((1, C), lambda i: (0, 0)),
        compiler_params=_params("arbitrary"),
    )(a)


def _final_loss(x, g, tgt, *, name):
    N, D = x.shape
    tr = _rows(N, 256)

    def body(x_ref, g_ref, t_ref, loss_ref, dx_ref, dg_ref):
        xv = x_ref[...]
        r = lax.rsqrt(jnp.mean(xv * xv, axis=-1, keepdims=True) + _EPS)
        xh = xv * r
        gv = g_ref[...]
        diff = xh * gv - t_ref[...]

        @pl.when(pl.program_id(0) == 0)
        def _():
            loss_ref[...] = jnp.zeros_like(loss_ref)
            dg_ref[...] = jnp.zeros_like(dg_ref)

        part = jnp.sum(jnp.sum(diff * diff, axis=1, keepdims=True), axis=0, keepdims=True) * (0.5 / D)
        loss_ref[...] += jnp.broadcast_to(part, loss_ref.shape)
        dy = diff * (1.0 / D)
        dg_ref[...] += jnp.sum(dy * xh, axis=0, keepdims=True)
        dxn = dy * gv
        dx_ref[...] = r * (dxn - xh * jnp.mean(dxn * xh, axis=-1, keepdims=True))

    row = pl.BlockSpec((tr, D), lambda i: (i, 0))
    vec = pl.BlockSpec((1, D), lambda i: (0, 0))
    return pl.pallas_call(
        body, name=name,
        out_shape=(jax.ShapeDtypeStruct((8, _LANES), _F32), jax.ShapeDtypeStruct((N, D), _F32), jax.ShapeDtypeStruct((1, D), _F32)),
        grid=(N // tr,), in_specs=[row, vec, row],
        out_specs=(pl.BlockSpec((8, _LANES), lambda i: (0, 0)), row, vec),
        compiler_params=_params("arbitrary"),
    )(x, g.reshape(1, D), tgt)


def _mm_gu(h, w_gu, l, *, name, bias=None, glu=False, keep=None, rms_fwd=None, tm=512, tn=1408, xchg=None):
    keep = _MXU if keep is None else keep
    N, K = h.shape
    H = w_gu.shape[-1] // 2
    tm, tn = _pick(N, tm), _pick(H, tn)
    nj = H // tn
    has_bias, has_norm = bias is not None, rms_fwd is not None

    def body(*refs):
        h_ref, wp_ref, wq_ref = refs[:3]
        pair_ref, act_ref = refs[3 + 2 * has_bias + has_norm:][:2]
        if has_norm:
            hv = _rms_rows(h_ref[...], refs[3 + 2 * has_bias][...])
            refs[-1][...] = hv
        else:
            hv = h_ref[...].astype(_MXU)
        p = _dot(hv, wp_ref[...].astype(_MXU), 1, 0)
        q = _dot(hv, wq_ref[...].astype(_MXU), 1, 0)
        if has_bias:
            p = p + refs[3][...]
            q = q + refs[4][...]
        pair_ref[0] = p.astype(pair_ref.dtype)
        pair_ref[1] = q.astype(pair_ref.dtype)
        act_ref[...] = (p * _sigmoid(q) if glu else p * _sigmoid(p) * q).astype(act_ref.dtype)

    in_specs = [pl.BlockSpec((tm, K), lambda i, j: (i, 0)), pl.BlockSpec((None, K, tn), lambda i, j: (l, 0, j)),
                pl.BlockSpec((None, K, tn), lambda i, j: (l, 0, j + nj))]
    args = [h, w_gu, w_gu]
    if has_bias:
        b2 = bias.reshape(1, 2 * H).astype(_F32)
        in_specs += [pl.BlockSpec((1, tn), lambda i, j: (0, j)), pl.BlockSpec((1, tn), lambda i, j: (0, j + nj))]
        args += [b2, b2]
    out_shape = [jax.ShapeDtypeStruct((2, N, H), keep), jax.ShapeDtypeStruct((N, H), keep)]
    out_specs = [pl.BlockSpec((2, tm, tn), lambda i, j: (0, i, j)), pl.BlockSpec((tm, tn), lambda i, j: (i, j))]
    if has_norm:
        in_specs.append(pl.BlockSpec((1, K), lambda i, j: (0, 0)))
        args.append(rms_fwd.reshape(1, K))
        out_shape.append(jax.ShapeDtypeStruct((N, K), _MXU))
        out_specs.append(pl.BlockSpec((tm, K), lambda i, j: (i, 0)))
    outs, got = _fused_call(
        body, name=name, out_shape=out_shape, grid=(N // tm, nj), in_specs=in_specs, out_specs=out_specs,
        scratch_shapes=[], sem=("arbitrary", "arbitrary") if has_norm else ("parallel", "parallel"), args=args, xchg=xchg)
    return (*outs, got)


def _mm_dgu(dx, w_down, l, gu, *, name, tm=512, tn=1408):
    N, K = dx.shape
    H = w_down.shape[-2]
    tm, tn = _pick(N, tm), _pick(H, tn)

    def body(dx_ref, w_ref, gu_ref, o_ref):
        d = _dot(dx_ref[...].astype(_MXU), w_ref[...].astype(_MXU), 1, 1)
        g, u = gu_ref[0].astype(_F32), gu_ref[1].astype(_F32)
        sg = _sigmoid(g)
        o_ref[0] = (d * u * (sg * (1.0 + g * (1.0 - sg)))).astype(o_ref.dtype)
        o_ref[1] = (d * (g * sg)).astype(o_ref.dtype)

    return pl.pallas_call(
        body, name=name, out_shape=jax.ShapeDtypeStruct((2, N, H), _MXU), grid=(N // tm, H // tn),
        in_specs=[pl.BlockSpec((tm, K), lambda i, j: (i, 0)), pl.BlockSpec((None, tn, K), lambda i, j: (l, j, 0)),
                  pl.BlockSpec((2, tm, tn), lambda i, j: (0, i, j))],
        out_specs=pl.BlockSpec((2, tm, tn), lambda i, j: (0, i, j)), compiler_params=_params("parallel", "parallel"),
    )(dx, w_down, gu)


def _gmlp_mix(vb, w, trans):
    tr, W = vb.shape
    lane = lax.broadcasted_iota(jnp.int32, (_CHUNK, _LANES), 1)
    rows = []
    for c in range(tr // _CHUNK):
        tiles = []
        for j in range(W // _LANES):
            t = vb[c * _CHUNK:(c + 1) * _CHUNK, j * _LANES:(j + 1) * _LANES]
            ma = _dot(w[2 * j], t, 0 if trans else 1, 0)
            mb = _dot(w[2 * j + 1], t, 0 if trans else 1, 0)
            tiles.append(jnp.where(lane < _GRP, ma, mb))
        rows.append(jnp.concatenate(tiles, axis=1))
    return jnp.concatenate(rows, axis=0)


def _tril_w(w_ref):
    r = lax.broadcasted_iota(jnp.int32, (_CHUNK, _CHUNK), 0)
    c = lax.broadcasted_iota(jnp.int32, (_CHUNK, _CHUNK), 1)
    return jnp.where((r >= c)[None], w_ref[...], 0.0).astype(_MXU)


def _layernorm_stats(v):
    mu = jnp.mean(v, axis=-1, keepdims=True)
    xc = v - mu
    rstd = lax.rsqrt(jnp.mean(xc * xc, axis=-1, keepdims=True) + _EPS)
    return xc * rstd, rstd


def _gmlp_fwd(proj, ln_g, ln_b, w_s, bias_full, *, name):
    N = proj.shape[0]
    W = ln_g.shape[-1]
    G = w_s.shape[0]
    tr = _rows(N, 512)
    ub, vb_ = 3, 4

    def body(u_ref, v_ref, g_ref, b_ref, w_ref, bias_ref, o_ref):
        u = _gelu(u_ref[...])
        xh, _ = _layernorm_stats(_gelu(v_ref[...]))
        vgn = xh * g_ref[...] + b_ref[...]
        mixed = _gmlp_mix(vgn.astype(_MXU), _tril_w(w_ref), False)
        bias = jnp.concatenate([bias_ref[...]] * (tr // _CHUNK), axis=0)
        o_ref[...] = (u * (mixed + bias)).astype(o_ref.dtype)

    vec = pl.BlockSpec((1, W), lambda i: (0, 0))
    return pl.pallas_call(
        body, name=name, out_shape=jax.ShapeDtypeStruct((N, W), _MXU), grid=(N // tr,),
        in_specs=[pl.BlockSpec((tr, W), lambda i: (i, ub)), pl.BlockSpec((tr, W), lambda i: (i, vb_)), vec, vec,
                  pl.BlockSpec((G, _CHUNK, _CHUNK), lambda i: (0, 0, 0)), pl.BlockSpec((_CHUNK, W), lambda i: (0, 0))],
        out_specs=pl.BlockSpec((tr, W), lambda i: (i, 0)), compiler_params=_params("parallel"),
    )(proj, proj, ln_g.reshape(1, W), ln_b.reshape(1, W), w_s, bias_full)


def _gmlp_bwd(proj, da_src, da_blk, ln_g, ln_b, w_s, bias_full, *, name):
    N = proj.shape[0]
    W = ln_g.shape[-1]
    G = w_s.shape[0]
    tr = _rows(N, 512)
    nch = tr // _CHUNK

    def body(u_ref, v_ref, da_ref, g_ref, b_ref, w_ref, bias_ref, dz_ref, dg_ref, db_ref, dw_ref, dbias_ref):
        @pl.when(pl.program_id(0) == 0)
        def _():
            dg_ref[...] = jnp.zeros_like(dg_ref)
            db_ref[...] = jnp.zeros_like(db_ref)
            dw_ref[...] = jnp.zeros_like(dw_ref)
            dbias_ref[...] = jnp.zeros_like(dbias_ref)

        u_pre, v_pre = u_ref[...], v_ref[...]
        ug = _gelu(u_pre)
        xh, rstd = _layernorm_stats(_gelu(v_pre))
        lg = g_ref[...]
        vgn = xh * lg + b_ref[...]
        vb = vgn.astype(_MXU)
        wt = _tril_w(w_ref)
        mixed = _gmlp_mix(vb, wt, False)
        bias = jnp.concatenate([bias_ref[...]] * nch, axis=0)
        da = da_ref[...].astype(_F32)
        du = da * (mixed + bias)
        dm = da * ug
        dmb = dm.astype(_MXU)
        lane = lax.broadcasted_iota(jnp.int32, (_CHUNK, _LANES), 1)
        r = lax.broadcasted_iota(jnp.int32, (_CHUNK, _CHUNK), 0)
        c = lax.broadcasted_iota(jnp.int32, (_CHUNK, _CHUNK), 1)
        tril = r >= c
        dmsum = dm[0:_CHUNK]
        for ch in range(1, nch):
            dmsum = dmsum + dm[ch * _CHUNK:(ch + 1) * _CHUNK]
        dbias = jnp.zeros((_CHUNK, _LANES), _F32)
        for j in range(W // _LANES):
            tile = dmsum[:, j * _LANES:(j + 1) * _LANES]
            sa = jnp.sum(jnp.where(lane < _GRP, tile, 0.0), axis=1, keepdims=True)
            sb = jnp.sum(jnp.where(lane >= _GRP, tile, 0.0), axis=1, keepdims=True)
            dbias = dbias + jnp.where(lane == 2 * j, sa, 0.0) + jnp.where(lane == 2 * j + 1, sb, 0.0)
            acc_a = jnp.zeros((_CHUNK, _CHUNK), _F32)
            acc_b = jnp.zeros((_CHUNK, _CHUNK), _F32)
            for ch in range(nch):
                dt = dmb[ch * _CHUNK:(ch + 1) * _CHUNK, j * _LANES:(j + 1) * _LANES]
                vt = vb[ch * _CHUNK:(ch + 1) * _CHUNK, j * _LANES:(j + 1) * _LANES]
                acc_a = acc_a + _dot(jnp.where(lane < _GRP, dt, jnp.zeros_like(dt)), vt, 1, 1)
                acc_b = acc_b + _dot(jnp.where(lane >= _GRP, dt, jnp.zeros_like(dt)), vt, 1, 1)
            dw_ref[2 * j] += jnp.where(tril, acc_a, 0.0)
            dw_ref[2 * j + 1] += jnp.where(tril, acc_b, 0.0)
        dbias_ref[...] += dbias
        dvgn = _gmlp_mix(dmb, wt, True)
        dg_ref[...] += jnp.sum(dvgn * xh, axis=0, keepdims=True)
        db_ref[...] += jnp.sum(dvgn, axis=0, keepdims=True)
        dxh = dvgn * lg
        dvg = rstd * (dxh - jnp.mean(dxh, axis=-1, keepdims=True) - xh * jnp.mean(dxh * xh, axis=-1, keepdims=True))
        dz_ref[:, :W] = (du * _gelu_grad(u_pre)).astype(dz_ref.dtype)
        dz_ref[:, W:] = (dvg * _gelu_grad(v_pre)).astype(dz_ref.dtype)

    vec = pl.BlockSpec((1, W), lambda i: (0, 0))
    wspec = pl.BlockSpec((G, _CHUNK, _CHUNK), lambda i: (0, 0, 0))
    return pl.pallas_call(
        body, name=name,
        out_shape=(jax.ShapeDtypeStruct((N, 2 * W), _MXU), jax.ShapeDtypeStruct((1, W), _F32), jax.ShapeDtypeStruct((1, W), _F32),
                   jax.ShapeDtypeStruct((G, _CHUNK, _CHUNK), _F32), jax.ShapeDtypeStruct((_CHUNK, _LANES), _F32)),
        grid=(N // tr,),
        in_specs=[pl.BlockSpec((tr, W), lambda i: (i, 3)), pl.BlockSpec((tr, W), lambda i: (i, 4)),
                  pl.BlockSpec((tr, W), lambda i: (i, da_blk)), vec, vec, wspec, pl.BlockSpec((_CHUNK, W), lambda i: (0, 0))],
        out_specs=(pl.BlockSpec((tr, 2 * W), lambda i: (i, 0)), vec, vec, wspec, pl.BlockSpec((_CHUNK, _LANES), lambda i: (0, 0))),
        compiler_params=_params("arbitrary"),
    )(proj, proj, da_src, ln_g.reshape(1, W), ln_b.reshape(1, W), w_s, bias_full)


def _lane_cumsum(v):
    T = v.shape[1]
    lane = lax.broadcasted_iota(jnp.int32, (8, _LANES), 1)
    carry = jnp.zeros((8, 1), _F32)
    out = []
    for ch in range(T // _LANES):
        blk = v[:, ch * _LANES:(ch + 1) * _LANES]
        sh = 1
        while sh < _LANES:
            blk = blk + jnp.where(lane >= sh, pltpu.roll(blk, sh, 1), 0.0)
            sh *= 2
        blk = blk + carry
        carry = blk[:, _LANES - 1:_LANES]
        out.append(blk)
    return jnp.concatenate(out, axis=1), carry


def _log_sigmoid(x):
    return jnp.minimum(x, 0.0) - jnp.log(1.0 + jnp.exp(-jnp.abs(x)))


def _fox_cum(proj3, f_blk, f_bias, *, name):
    B, T, _ = proj3.shape
    H = f_bias.shape[-1]
    assert H == 8

    def body(f_ref, b_ref, o_ref):
        x = f_ref[0].T[0:8, :] + b_ref[...]
        cum, _ = _lane_cumsum(_log_sigmoid(x))
        o_ref[0] = cum

    return pl.pallas_call(
        body, name=name, out_shape=jax.ShapeDtypeStruct((B, 8, T), _F32), grid=(B,),
        in_specs=[pl.BlockSpec((1, T, _LANES), lambda b: (b, 0, f_blk)), pl.BlockSpec((8, 1), lambda b: (0, 0))],
        out_specs=pl.BlockSpec((1, 8, T), lambda b: (b, 0, 0)), compiler_params=_params("parallel"),
    )(proj3, f_bias.reshape(8, 1))


def _fox_cum_bwd(proj3, f_blk, f_bias, dcum, *, name):
    B, T, _ = proj3.shape

    def body(f_ref, b_ref, dc_ref, df_ref, dbias_ref):
        @pl.when(pl.program_id(0) == 0)
        def _():
            dbias_ref[...] = jnp.zeros_like(dbias_ref)

        x = f_ref[0].T[0:8, :] + b_ref[...]
        dc = dc_ref[0]
        incl, total = _lane_cumsum(dc)
        dlf = total - incl + dc
        df = dlf * _sigmoid(-x)
        full = jnp.concatenate([df, jnp.zeros((_LANES - 8, T), _F32)], axis=0).T
        dbias_ref[...] += jnp.sum(full, axis=0, keepdims=True)
        df_ref[0] = full

    return pl.pallas_call(
        body, name=name,
        out_shape=(jax.ShapeDtypeStruct((B, T, _LANES), _F32), jax.ShapeDtypeStruct((1, _LANES), _F32)), grid=(B,),
        in_specs=[pl.BlockSpec((1, T, _LANES), lambda b: (b, 0, f_blk)), pl.BlockSpec((8, 1), lambda b: (0, 0)),
                  pl.BlockSpec((1, 8, T), lambda b: (b, 0, 0))],
        out_specs=(pl.BlockSpec((1, T, _LANES), lambda b: (b, 0, 0)), pl.BlockSpec((1, _LANES), lambda b: (0, 0))),
        compiler_params=_params("arbitrary"),
    )(proj3, f_bias.reshape(8, 1), dcum)


def _cum_row(cum_ref, h, start, size):
    blk = cum_ref[0, :, pl.ds(start, size)]
    sub = lax.broadcasted_iota(jnp.int32, (blk.shape[0], 1), 0)
    return jnp.sum(jnp.where(sub == h, blk, 0.0), axis=0, keepdims=True)


def _causal(tq, q0, k0):
    r = lax.broadcasted_iota(jnp.int32, (tq, tq), 0)
    c = lax.broadcasted_iota(jnp.int32, (tq, tq), 1)
    return (r + q0) >= (c + k0)


def _fused_call(body, *, name, out_shape, grid, in_specs, out_specs, scratch_shapes, sem, args, xchg):
    out_shape, in_specs, out_specs, scratch_shapes = list(out_shape), list(in_specs), list(out_specs), list(scratch_shapes)
    if xchg is None:
        res = pl.pallas_call(body, name=name, out_shape=out_shape, grid=grid, in_specs=in_specs, out_specs=out_specs,
                             scratch_shapes=scratch_shapes, compiler_params=_params(*sem))(*args)
        return list(res), []
    n_in, n_out, n_scr = len(in_specs), len(out_specs), len(scratch_shapes)

    def fused(*refs):
        ins, refs = refs[:n_in], refs[n_in:]
        xs, refs = refs[:xchg.n_src], refs[xchg.n_src:]
        outs, refs = refs[:n_out], refs[n_out:]
        xd, refs = refs[:xchg.n_dst], refs[xchg.n_dst:]
        scr, sems = refs[:n_scr], refs[n_scr:]
        first = last = None
        for d, g in enumerate(grid):
            i = pl.program_id(d)
            first = (i == 0) if first is None else first & (i == 0)
            last = (i == g - 1) if last is None else last & (i == g - 1)

        @pl.when(first)
        def _():
            xchg.start(xs, xd, sems)

        body(*ins, *outs, *scr)

        @pl.when(last)
        def _():
            xchg.finish(xs, xd, sems)

    res = pl.pallas_call(
        fused, name=name, out_shape=out_shape + xchg.out_shapes, grid=grid, in_specs=in_specs + xchg.in_specs,
        out_specs=out_specs + xchg.out_specs, scratch_shapes=scratch_shapes + xchg.scratch,
        compiler_params=_params(*["arbitrary"] * len(grid)),
    )(*args, *xchg.srcs)
    return list(res[:n_out]), list(res[n_out:])


def _fox_fwd(proj3, cum, *, name, xchg=None):
    B, T, _ = proj3.shape
    H = cum.shape[1]
    W = H * _FOX_HD
    npair = W // _LANES
    tq = _rows(T, _FOX_TQ)
    nq = T // tq

    def body(q_ref, k_ref, v_ref, cum_ref, o_ref, lse_ref):
        p = pl.program_id(1)
        i = pl.program_id(2)
        q0 = pl.multiple_of(i * tq, tq)
        lane = lax.broadcasted_iota(jnp.int32, (1, _LANES), 1)
        q2 = q_ref[0] * _FOX_SCALE
        heads = []
        for hh in range(2):
            msk = (lane < _FOX_HD) if hh == 0 else (lane >= _FOX_HD)
            h = 2 * p + hh
            heads.append((msk, h, jnp.where(msk, q2, 0.0).astype(_MXU), _cum_row(cum_ref, h, q0, _LANES)[:, 0:1]))

        def step(jj, carry, masked):
            k0 = pl.multiple_of(jj * tq, tq)
            k2 = k_ref[0, pl.ds(k0, tq), :].astype(_MXU)
            v2 = v_ref[0, pl.ds(k0, tq), :]
            out = []
            for (msk, h, qm, c0), (m_prev, l_prev, acc) in zip(heads, carry):
                s = _dot(qm, k2, 1, 1) + (c0 - _cum_row(cum_ref, h, k0, tq))
                if masked:
                    s = jnp.where(_causal(tq, q0, k0), s, -jnp.inf)
                m_new = jnp.maximum(m_prev, jnp.max(s, axis=1, keepdims=True))
                alpha = jnp.exp(m_prev - m_new)
                e = jnp.exp(s - m_new)
                l_new = alpha * l_prev + jnp.sum(e, axis=1, keepdims=True)
                vm = jnp.where(msk, v2, 0.0).astype(_MXU)
                out.append((m_new, l_new, alpha * acc + _dot(e.astype(_MXU), vm, 1, 0)))
            return tuple(out)

        init = tuple((jnp.full((tq, 1), -jnp.inf, _F32), jnp.zeros((tq, 1), _F32), jnp.zeros((tq, _LANES), _F32)) for _ in heads)
        carry = step(i, lax.fori_loop(0, i, functools.partial(step, masked=False), init), True)
        o2 = jnp.zeros((tq, _LANES), _F32)
        for hh, (m, l, acc) in enumerate(carry):
            o2 = o2 + acc / l
            lse_ref[0, hh] = jnp.broadcast_to(m + jnp.log(l), (tq, _LANES))
        o_ref[0] = o2.astype(o_ref.dtype)

    (o, lse), got = _fused_call(
        body, name=name,
        out_shape=(jax.ShapeDtypeStruct((B, T, W), _MXU), jax.ShapeDtypeStruct((B, H, T, _LANES), _F32)),
        grid=(B, npair, nq),
        in_specs=[pl.BlockSpec((1, tq, _LANES), lambda b, p, i: (b, i, p)),
                  pl.BlockSpec((1, T, _LANES), lambda b, p, i: (b, 0, npair + p)),
                  pl.BlockSpec((1, T, _LANES), lambda b, p, i: (b, 0, 2 * npair + p)),
                  pl.BlockSpec((1, H, T), lambda b, p, i: (b, 0, 0))],
        out_specs=(pl.BlockSpec((1, tq, _LANES), lambda b, p, i: (b, i, p)),
                   pl.BlockSpec((1, 2, tq, _LANES), lambda b, p, i: (b, p, i, 0))),
        scratch_shapes=[], sem=("parallel", "parallel", "parallel"), args=(proj3, proj3, proj3, cum), xchg=xchg)
    return o, lse, got


def _fox_bwd(proj3, cum, do3, lse, *, name, xchg=None):
    B, T, _ = proj3.shape
    H = cum.shape[1]
    W = H * _FOX_HD
    npair = W // _LANES
    tq = _rows(T, _FOX_TQ)
    nq = T // tq

    def body(q_ref, k_ref, v_ref, cum_ref, do_ref, lse_ref, dq_ref, dk_ref, dv_ref, dc_ref, p_scr, dp_scr, dk_acc, dv_acc, dc_acc):
        p = pl.program_id(1)
        i = pl.program_id(2)
        q0 = pl.multiple_of(i * tq, tq)
        lane = lax.broadcasted_iota(jnp.int32, (1, _LANES), 1)

        @pl.when(i == 0)
        def _():
            dk_acc[...] = jnp.zeros_like(dk_acc)
            dv_acc[...] = jnp.zeros_like(dv_acc)
            dc_acc[...] = jnp.zeros_like(dc_acc)

        q2 = q_ref[0] * _FOX_SCALE
        do2 = do_ref[0].astype(_F32)
        heads = []
        for hh in range(2):
            msk = (lane < _FOX_HD) if hh == 0 else (lane >= _FOX_HD)
            h = 2 * p + hh
            heads.append((hh, msk, h, jnp.where(msk, q2, 0.0).astype(_MXU), jnp.where(msk, do2, 0.0).astype(_MXU),
                          _cum_row(cum_ref, h, q0, _LANES)[:, 0:1], lse_ref[0, hh][:, 0:1]))

        def first(jj, deltas, masked):
            k0 = pl.multiple_of(jj * tq, tq)
            kb = k_ref[0, pl.ds(k0, tq), :].astype(_MXU)
            vb = v_ref[0, pl.ds(k0, tq), :].astype(_MXU)
            out = []
            for (hh, _, h, qm, dom, c0, lse_h), delta in zip(heads, deltas):
                s = _dot(qm, kb, 1, 1) + (c0 - _cum_row(cum_ref, h, k0, tq))
                pr = jnp.exp(s - lse_h)
                if masked:
                    pr = jnp.where(_causal(tq, q0, k0), pr, 0.0)
                dp = _dot(dom, vb, 1, 1)
                p_scr[hh, jj] = pr
                dp_scr[hh, jj] = dp
                out.append(delta + jnp.sum(pr * dp, axis=1, keepdims=True))
            return tuple(out)

        zero = tuple(jnp.zeros((tq, 1), _F32) for _ in heads)
        deltas = first(i, lax.fori_loop(0, i, functools.partial(first, masked=False), zero), True)

        def second(jj, dq):
            k0 = pl.multiple_of(jj * tq, tq)
            k2 = k_ref[0, pl.ds(k0, tq), :]
            dk = jnp.zeros((tq, _LANES), _F32)
            dv = jnp.zeros((tq, _LANES), _F32)
            for (hh, msk, _, qm, dom, _, _), delta in zip(heads, deltas):
                pr = p_scr[hh, jj]
                ds = pr * (dp_scr[hh, jj] - delta)
                dsb = ds.astype(_MXU)
                dv = dv + _dot(pr.astype(_MXU), dom, 0, 0)
                dk = dk + _dot(dsb, qm, 0, 0)
                dc_acc[hh:hh + 1, pl.ds(k0, tq)] += jnp.sum(ds, axis=0, keepdims=True)
                dq = dq + _dot(dsb, jnp.where(msk, k2, 0.0).astype(_MXU), 1, 0)
            dv_acc[pl.ds(k0, tq), :] += dv
            dk_acc[pl.ds(k0, tq), :] += dk
            return dq

        dq2 = lax.fori_loop(0, i + 1, second, jnp.zeros((tq, _LANES), _F32))
        dq_ref[0] = (dq2 * _FOX_SCALE).astype(dq_ref.dtype)

        @pl.when(i == nq - 1)
        def _():
            dk_ref[0] = dk_acc[...].astype(dk_ref.dtype)
            dv_ref[0] = dv_acc[...].astype(dv_ref.dtype)
            dc_ref[0, 0] = -dc_acc[...]

    full = lambda blk: pl.BlockSpec((1, T, _LANES), lambda b, p, i, blk=blk: (b, 0, blk * npair + p))
    part = lambda blk: pl.BlockSpec((1, tq, _LANES), lambda b, p, i, blk=blk: (b, i, blk * npair + p))
    (dq, dk, dv, dcum), got = _fused_call(
        body, name=name,
        out_shape=(jax.ShapeDtypeStruct((B, T, W), _MXU), jax.ShapeDtypeStruct((B, T, W), _MXU),
                   jax.ShapeDtypeStruct((B, T, W), _MXU), jax.ShapeDtypeStruct((B, npair, 2, T), _F32)),
        grid=(B, npair, nq),
        in_specs=[part(0), full(1), full(2), pl.BlockSpec((1, H, T), lambda b, p, i: (b, 0, 0)), part(0),
                  pl.BlockSpec((1, 2, tq, _LANES), lambda b, p, i: (b, p, i, 0))],
        out_specs=(part(0), full(0), full(0), pl.BlockSpec((1, 1, 2, T), lambda b, p, i: (b, p, 0, 0))),
        scratch_shapes=[pltpu.VMEM((2, nq, tq, tq), _F32), pltpu.VMEM((2, nq, tq, tq), _F32), pltpu.VMEM((T, _LANES), _F32),
                        pltpu.VMEM((T, _LANES), _F32), pltpu.VMEM((2, T), _F32)],
        sem=("parallel", "parallel", "arbitrary"), args=(proj3, proj3, proj3, cum, do3, lse), xchg=xchg)
    return dq, dk, dv, dcum, got


def _xa_probs(qh, kh, scale):
    s = _dot(qh, kh, 1, 1) * scale
    e = jnp.exp(s - jnp.max(s, axis=1, keepdims=True))
    return e / jnp.sum(e, axis=1, keepdims=True)


def _xa_fwd(q3, kv3, *, name):
    B, T, D = q3.shape
    M = kv3.shape[1]
    hd = D // _XA_HEADS
    scale = hd ** -0.5
    tq = _rows(T, 512)

    def body(q_ref, kv_ref, o_ref):
        for h in range(_XA_HEADS):
            sl = slice(h * hd, (h + 1) * hd)
            p = _xa_probs(q_ref[0, :, sl], kv_ref[0, :, sl], scale)
            o_ref[0, :, sl] = _dot(p.astype(_MXU), kv_ref[0, :, D + h * hd:D + (h + 1) * hd], 1, 0).astype(o_ref.dtype)

    return pl.pallas_call(
        body, name=name, out_shape=jax.ShapeDtypeStruct((B, T, D), _MXU), grid=(B, T // tq),
        in_specs=[pl.BlockSpec((1, tq, D), lambda b, i: (b, i, 0)), pl.BlockSpec((1, M, 2 * D), lambda b, i: (b, 0, 0))],
        out_specs=pl.BlockSpec((1, tq, D), lambda b, i: (b, i, 0)), compiler_params=_params("parallel", "parallel"),
    )(q3, kv3)


def _xa_bwd(q3, kv3, do3, *, name):
    B, T, D = q3.shape
    M = kv3.shape[1]
    hd = D // _XA_HEADS
    scale = hd ** -0.5
    tq = _rows(T, 512)

    def body(q_ref, kv_ref, do_ref, dq_ref, dkv_ref):
        @pl.when(pl.program_id(1) == 0)
        def _():
            dkv_ref[...] = jnp.zeros_like(dkv_ref)

        for h in range(_XA_HEADS):
            sl = slice(h * hd, (h + 1) * hd)
            slv = slice(D + h * hd, D + (h + 1) * hd)
            qh, kh, vh, doh = q_ref[0, :, sl], kv_ref[0, :, sl], kv_ref[0, :, slv], do_ref[0, :, sl]
            p = _xa_probs(qh, kh, scale)
            dkv_ref[0, :, slv] += _dot(p.astype(_MXU), doh, 0, 0)
            dp = _dot(doh, vh, 1, 1)
            ds = (p * (dp - jnp.sum(p * dp, axis=1, keepdims=True))).astype(_MXU)
            dq_ref[0, :, sl] = (_dot(ds, kh, 1, 0) * scale).astype(dq_ref.dtype)
            dkv_ref[0, :, sl] += _dot(ds, qh, 0, 0) * scale

    blk = pl.BlockSpec((1, tq, D), lambda b, i: (b, i, 0))
    kvs = pl.BlockSpec((1, M, 2 * D), lambda b, i: (b, 0, 0))
    return pl.pallas_call(
        body, name=name,
        out_shape=(jax.ShapeDtypeStruct((B, T, D), _MXU), jax.ShapeDtypeStruct((B, M, 2 * D), _F32)),
        grid=(B, T // tq), in_specs=[blk, kvs, blk], out_specs=(blk, kvs),
        compiler_params=_params("parallel", "arbitrary"),
    )(q3, kv3, do3)


def _rotated_copies(ext, rot, tt):
    rot[0] = ext[...]
    for b in range(1, 8):
        rot[b, 0:tt + _HALO - 8, :] = ext[b:b + tt + _HALO - 8, :]


def _shifted(rot, off, r0, rows, c0):
    a, b = divmod(off, 8)
    return rot[b, 8 * a + r0:8 * a + r0 + rows, c0:c0 + _LANES]


def _conv_fwd(y3, dw_w, dw_b, ln_g, ln_b, *, name, xchg=None):
    B, T, C = y3.shape
    tt = _rows(T, 256)
    nt = T // tt

    def body(prev_ref, cur_ref, w_ref, b_ref, g_ref, lb_ref, y2_ref, y4_ref, ext, rot):
        i = pl.program_id(1)
        ext[0:_HALO, :] = jnp.where(i > 0, prev_ref[0, tt - _HALO:tt, :], 0.0)
        ext[_HALO:_HALO + tt, :] = cur_ref[0]
        _rotated_copies(ext, rot, tt)
        for c0 in range(0, C, _LANES):
            acc = jnp.broadcast_to(b_ref[:, c0:c0 + _LANES], (tt, _LANES))
            for j in range(_CONV_K):
                acc = acc + w_ref[j:j + 1, c0:c0 + _LANES] * _shifted(rot, _HALO - (_CONV_K - 1) + j, 0, tt, c0)
            y2_ref[0, :, c0:c0 + _LANES] = acc
        xh, _ = _layernorm_stats(y2_ref[0])
        z = xh * g_ref[...] + lb_ref[...]
        y4_ref[0] = (z * _sigmoid(z)).astype(y4_ref.dtype)

    vec = pl.BlockSpec((1, C), lambda b, i: (0, 0))
    blk = pl.BlockSpec((1, tt, C), lambda b, i: (b, i, 0))
    (y2, y4), got = _fused_call(
        body, name=name,
        out_shape=(jax.ShapeDtypeStruct((B, T, C), _F32), jax.ShapeDtypeStruct((B, T, C), _MXU)),
        grid=(B, nt),
        in_specs=[pl.BlockSpec((1, tt, C), lambda b, i: (b, jnp.maximum(i - 1, 0), 0)), blk,
                  pl.BlockSpec((_HALO, C), lambda b, i: (0, 0)), vec, vec, vec],
        out_specs=(blk, blk),
        scratch_shapes=[pltpu.VMEM((tt + _HALO, C), _F32), pltpu.VMEM((8, tt + _HALO, C), _F32)],
        sem=("parallel", "parallel"), args=(y3, y3, dw_w, dw_b.reshape(1, C), ln_g.reshape(1, C), ln_b.reshape(1, C)), xchg=xchg)
    return y2, y4, got


def _conv_ln_bwd(y2, dy4, ln_g, ln_b, *, name):
    N, C = y2.shape
    tr = _rows(N, 256)

    def body(y_ref, d_ref, g_ref, b_ref, dy_ref, dg_ref, db_ref, dwb_ref):
        @pl.when(pl.program_id(0) == 0)
        def _():
            dg_ref[...] = jnp.zeros_like(dg_ref)
            db_ref[...] = jnp.zeros_like(db_ref)
            dwb_ref[...] = jnp.zeros_like(dwb_ref)

        xh, rstd = _layernorm_stats(y_ref[...])
        gv = g_ref[...]
        z = xh * gv + b_ref[...]
        sg = _sigmoid(z)
        dz = d_ref[...] * (sg * (1.0 + z * (1.0 - sg)))
        dg_ref[...] += jnp.sum(dz * xh, axis=0, keepdims=True)
        db_ref[...] += jnp.sum(dz, axis=0, keepdims=True)
        dxh = dz * gv
        dy = rstd * (dxh - jnp.mean(dxh, axis=-1, keepdims=True) - xh * jnp.mean(dxh * xh, axis=-1, keepdims=True))
        dwb_ref[...] += jnp.sum(dy, axis=0, keepdims=True)
        dy_ref[...] = dy

    row = pl.BlockSpec((tr, C), lambda i: (i, 0))
    vec = pl.BlockSpec((1, C), lambda i: (0, 0))
    v = jax.ShapeDtypeStruct((1, C), _F32)
    return pl.pallas_call(
        body, name=name, out_shape=(jax.ShapeDtypeStruct((N, C), _F32), v, v, v), grid=(N // tr,),
        in_specs=[row, row, vec, vec], out_specs=(row, vec, vec, vec), compiler_params=_params("arbitrary"),
    )(y2, dy4, ln_g.reshape(1, C), ln_b.reshape(1, C))


def _conv_bwd(y3, dy23, ag3, dw_w, *, name, xchg=None):
    B, T, C = y3.shape
    tt = _rows(T, 256)
    nt = T // tt

    rs = _rows(tt, 128)

    def groups(v):
        return jnp.sum(v.reshape(rs // 8, 8, _LANES), axis=0)

    def body(yp_ref, yc_ref, dc_ref, dn_ref, a_ref, g_ref, w_ref, dag_ref, dw_ref, dbin_ref, yext, dext, yrot, drot, dw_acc, db_acc):
        b = pl.program_id(0)
        i = pl.program_id(1)

        @pl.when((b == 0) & (i == 0))
        def _():
            dw_acc[...] = jnp.zeros_like(dw_acc)
            db_acc[...] = jnp.zeros_like(db_acc)

        yext[0:_HALO, :] = jnp.where(i > 0, yp_ref[0, tt - _HALO:tt, :], 0.0)
        yext[_HALO:_HALO + tt, :] = yc_ref[0]
        dext[0:tt, :] = dc_ref[0]
        dext[tt:tt + _HALO, :] = jnp.where(i < nt - 1, dn_ref[0, 0:_HALO, :], 0.0)
        _rotated_copies(yext, yrot, tt)
        _rotated_copies(dext, drot, tt)
        for c0 in range(0, C, _LANES):
            for r0 in range(0, tt, rs):
                d_cur = dext[r0:r0 + rs, c0:c0 + _LANES]
                dy = jnp.zeros((rs, _LANES), _F32)
                for j in range(_CONV_K):
                    sh = _CONV_K - 1 - j
                    dy = dy + w_ref[j:j + 1, c0:c0 + _LANES] * _shifted(drot, sh, r0, rs, c0)
                    dw_acc[j, :, c0:c0 + _LANES] += groups(d_cur * _shifted(yrot, _HALO - sh, r0, rs, c0))
                a, g = a_ref[0, r0:r0 + rs, c0:c0 + _LANES], g_ref[0, r0:r0 + rs, c0:c0 + _LANES]
                sg = _sigmoid(g)
                da = dy * sg
                dg = dy * a * (sg * (1.0 - sg))
                dag_ref[0, r0:r0 + rs, c0:c0 + _LANES] = da.astype(dag_ref.dtype)
                dag_ref[0, r0:r0 + rs, C + c0:C + c0 + _LANES] = dg.astype(dag_ref.dtype)
                db_acc[:, c0:c0 + _LANES] += groups(da)
                db_acc[:, C + c0:C + c0 + _LANES] += groups(dg)

        @pl.when((b == B - 1) & (i == nt - 1))
        def _():
            dw_ref[...] = jnp.sum(dw_acc[...], axis=1)
            dbin_ref[...] = jnp.sum(db_acc[...], axis=0, keepdims=True)

    blk = pl.BlockSpec((1, tt, C), lambda b, i: (b, i, 0))
    (dag, ddw, dbin), got = _fused_call(
        body, name=name,
        out_shape=(jax.ShapeDtypeStruct((B, T, 2 * C), _MXU), jax.ShapeDtypeStruct((_HALO, C), _F32),
                   jax.ShapeDtypeStruct((1, 2 * C), _F32)),
        grid=(B, nt),
        in_specs=[pl.BlockSpec((1, tt, C), lambda b, i: (b, jnp.maximum(i - 1, 0), 0)), blk, blk,
                  pl.BlockSpec((1, tt, C), lambda b, i: (b, jnp.minimum(i + 1, nt - 1), 0)),
                  pl.BlockSpec((None, 1, tt, C), lambda b, i: (0, b, i, 0)), pl.BlockSpec((None, 1, tt, C), lambda b, i: (1, b, i, 0)),
                  pl.BlockSpec((_HALO, C), lambda b, i: (0, 0))],
        out_specs=(pl.BlockSpec((1, tt, 2 * C), lambda b, i: (b, i, 0)), pl.BlockSpec((_HALO, C), lambda b, i: (0, 0)),
                   pl.BlockSpec((1, 2 * C), lambda b, i: (0, 0))),
        scratch_shapes=[pltpu.VMEM((tt + _HALO, C), _F32), pltpu.VMEM((tt + _HALO, C), _F32),
                        pltpu.VMEM((8, tt + _HALO, C), _F32), pltpu.VMEM((8, tt + _HALO, C), _F32),
                        pltpu.VMEM((_HALO, 8, C), _F32), pltpu.VMEM((8, 2 * C), _F32)],
        sem=("arbitrary", "arbitrary"), args=(y3, y3, dy23, dy23, ag3, ag3, dw_w), xchg=xchg)
    return dag, ddw, dbin, got


class _Exchange:
    def __init__(self, items):
        self.per_peer = [pp for _, pp in items]
        self.srcs, self.out_shapes, self.pieces = [], [], []
        for t, (srcs, per_peer) in enumerate(items):
            blk = srcs[0].shape[1:] if per_peer else srcs[0].shape
            self.out_shapes.append(jax.ShapeDtypeStruct((len(srcs), _N_DEV) + tuple(blk), srcs[0].dtype))
            for l, s in enumerate(srcs):
                self.pieces.append((t, l, len(self.srcs)))
                self.srcs.append(s)
        self.n_src, self.n_dst, n_pc = len(self.srcs), len(items), len(self.pieces)
        self.in_specs = [pl.BlockSpec(memory_space=pl.ANY)] * self.n_src
        self.out_specs = [pl.BlockSpec(memory_space=pl.ANY)] * self.n_dst
        self.scratch = [pltpu.SemaphoreType.DMA((n_pc, _N_DEV - 1)), pltpu.SemaphoreType.DMA((n_pc, _N_DEV - 1)),
                        pltpu.SemaphoreType.DMA((n_pc,))]

    def _copies(self, src_refs, dst_refs, sems, kind):
        send_sems, recv_sems, loc_sems = sems
        x, y, c = lax.axis_index("x"), lax.axis_index("y"), lax.axis_index("c")
        me = 4 * x + 2 * y + c
        out = []
        for i, (t, l, s) in enumerate(self.pieces):
            def src_for(p, s=s, t=t):
                return src_refs[s].at[p] if self.per_peer[t] else src_refs[s]

            if kind == "local":
                out.append(pltpu.make_async_copy(src_for(me), dst_refs[t].at[l, me], loc_sems.at[i]))
                continue
            for k in range(1, _N_DEV):
                px, py, pc = (1 - x if k & 4 else x), (1 - y if k & 2 else y), (1 - c if k & 1 else c)
                p = 4 * px + 2 * py + pc
                out.append(pltpu.make_async_remote_copy(
                    src_ref=src_for(p), dst_ref=dst_refs[t].at[l, p if kind == "recv" else me],
                    send_sem=send_sems.at[i, k - 1], recv_sem=recv_sems.at[i, k - 1],
                    device_id=(px, py, pc), device_id_type=pl.DeviceIdType.MESH))
        return out

    def start(self, src_refs, dst_refs, sems):
        for cp in self._copies(src_refs, dst_refs, sems, "local") + self._copies(src_refs, dst_refs, sems, "send"):
            cp.start()

    def finish(self, src_refs, dst_refs, sems):
        for cp in self._copies(src_refs, dst_refs, sems, "send"):
            cp.wait_send()
        for cp in self._copies(src_refs, dst_refs, sems, "recv"):
            cp.wait_recv()
        for cp in self._copies(src_refs, dst_refs, sems, "local"):
            cp.wait()


def _exchange(items, *, name):
    ex = _Exchange(items)

    def body(*refs):
        parts = refs[:ex.n_src], refs[ex.n_src:ex.n_src + ex.n_dst], refs[ex.n_src + ex.n_dst:]
        ex.start(*parts)
        ex.finish(*parts)

    return pl.pallas_call(
        body, name=name, out_shape=ex.out_shapes, in_specs=ex.in_specs, out_specs=ex.out_specs, scratch_shapes=ex.scratch,
        compiler_params=pltpu.CompilerParams(has_side_effects=True),
    )(*ex.srcs)


def _adam_update(g, w, m, v):
    c1 = 1.0 / (1.0 - _ADAM_B1 ** _ADAM_STEP)
    c2 = 1.0 / (1.0 - _ADAM_B2 ** _ADAM_STEP)
    m2 = _ADAM_B1 * m + (1.0 - _ADAM_B1) * g
    v2 = _ADAM_B2 * v + (1.0 - _ADAM_B2) * (g * g)
    return -_ADAM_LR * ((m2 * c1) / (jnp.sqrt(v2 * c2) + _ADAM_EPS) + _ADAM_WD * w), m2, v2


def _adamw_big(recvs, w, m, v, *, name):
    L, R, C = w.shape
    tr = _rows(R, 256)
    nb = R // tr

    def body(*refs):
        r_refs = refs[:L]
        w_ref, m_ref, v_ref, g_ref, d_ref, mo_ref, vo_ref = refs[L:]
        for l in range(L):
            @pl.when(pl.program_id(0) == l)
            def _(r_ref=r_refs[l]):
                g = r_ref[0, 0].astype(_F32)
                for k in range(1, _N_DEV):
                    g = g + r_ref[0, k].astype(_F32)
                g_ref[0] = g
                d_ref[0], mo_ref[0], vo_ref[0] = _adam_update(g, w_ref[0], m_ref[0], v_ref[0])

    def recv_spec(l):
        return pl.BlockSpec((1, _N_DEV, tr, C), lambda ll, i: (0, 0, jnp.where(ll == l, i, jnp.where(ll < l, 0, nb - 1)), 0))

    blk = pl.BlockSpec((1, tr, C), lambda l, i: (l, i, 0))
    o = jax.ShapeDtypeStruct((L, R, C), _F32)
    return pl.pallas_call(
        body, name=name, out_shape=(o, o, o, o), grid=(L, nb),
        in_specs=[recv_spec(l) for l in range(L)] + [blk, blk, blk], out_specs=(blk, blk, blk, blk),
        compiler_params=_params("arbitrary", "arbitrary"),
    )(*recvs, w, m, v)


def _adamw_small(tensors, *, name):
    n = len(tensors)
    lanes = [t[4] for t in tensors]
    layers = [len(t[0]) for t in tensors]

    def body(*refs):
        pos = 0
        ins = []
        for t in range(n):
            ins.append((refs[pos:pos + layers[t]], *refs[pos + layers[t]:pos + layers[t] + 3]))
            pos += layers[t] + 3
        outs = refs[pos:]
        for t in range(n):
            r_refs, w_ref, m_ref, v_ref = ins[t]
            g_ref, d_ref, mo_ref, vo_ref = outs[4 * t:4 * t + 4]
            for l in range(layers[t]):
                g = r_refs[l][0, 0]
                for k in range(1, _N_DEV):
                    g = g + r_refs[l][0, k]
                if lanes[t] is not None:
                    g = g[..., :lanes[t]]
                g_ref[l] = g
                d_ref[l], mo_ref[l], vo_ref[l] = _adam_update(g, w_ref[l], m_ref[l], v_ref[l])

    args, out_shape = [], []
    for recvs, w, m, v, _ in tensors:
        args += [*recvs, w, m, v]
        out_shape += [jax.ShapeDtypeStruct(w.shape, _F32)] * 4
    outs = pl.pallas_call(
        body, name=name, out_shape=out_shape,
        in_specs=[pl.BlockSpec(memory_space=pltpu.VMEM)] * len(args), out_specs=[pl.BlockSpec(memory_space=pltpu.VMEM)] * len(out_shape),
        compiler_params=_params(),
    )(*args)
    return [tuple(outs[4 * t:4 * t + 4]) for t in range(n)]


_BIG = (("w_in_e", 2), ("w_out_e", 1), ("conv_w_in", 2), ("conv_w_out", 1), ("xa_wq", 1), ("xa_wkv", 2), ("xa_wo", 1),
        ("ffn_w_gu", 2), ("ffn_w_down", 1))
_SMALL_SHARDED = (("mix_norm_o", 1), ("conv_b_in", 1), ("conv_dw_w", 2), ("conv_dw_b", 1), ("conv_ln_g", 1),
                  ("conv_ln_b", 1), ("conv_b_out", 1))
_REPLICATED = ("mix_norm_e", "fox_f_bias", "gmlp_ln_g", "gmlp_ln_b", "gmlp_w_s", "gmlp_b_s", "xa_norm", "mem_norm",
               "ffn_norm", "final_norm")
_WEIGHTS = ("mix_norm_e", "w_in_e", "fox_f_bias", "gmlp_ln_g", "gmlp_ln_b", "gmlp_w_s", "gmlp_b_s", "w_out_e", "mix_norm_o",
            "conv_w_in", "conv_b_in", "conv_dw_w", "conv_dw_b", "conv_ln_g", "conv_ln_b", "conv_w_out", "conv_b_out",
            "xa_norm", "mem_norm", "xa_wq", "xa_wkv", "xa_wo", "ffn_norm", "ffn_w_gu", "ffn_w_down", "final_norm")


def _cols_to_peers(g, n=_N_DEV):
    K, N = g.shape[-2:]
    return jnp.swapaxes(g.reshape(g.shape[:-1] + (n, N // n)), -3, -2)


def _weight_items(pieces, wsrc):
    return [([wsrc[n][l]] if n in dict(_BIG) else [wsrc[n]], False) for n, l in pieces]


def _place_weights(P, pieces, gathered):
    axis = dict(_BIG + _SMALL_SHARDED)
    for (n, l), g in zip(pieces, gathered):
        if n in dict(_BIG):
            P.setdefault(n, {})[l] = g.reshape(1, -1, g.shape[-1]) if axis[n] == 1 else _peers_to_cols(g)
        else:
            P[n] = _peers_to_cols(g[0, :, 0])[None] if axis[n] == 2 else g.reshape(1, -1)


def _grad_items(pieces, G):
    axis = dict(_BIG + _SMALL_SHARDED)
    items = []
    for n, l in pieces:
        g = G[n][l]
        if n in _REPLICATED:
            items.append(([g], False))
        elif n == "ffn_w_gu":
            half = _N_DEV // 2
            items.append(([jnp.concatenate([_cols_to_peers(g[0], half), _cols_to_peers(g[1], half)], axis=0)], True))
        elif n in dict(_BIG):
            items.append(([g.reshape(_N_DEV, -1, g.shape[-1]) if axis[n] == 1 else _cols_to_peers(g)], True))
        else:
            items.append(([_cols_to_peers(g) if axis[n] == 2 else g.reshape(_N_DEV, 1, -1)], True))
    return items


def _peers_to_cols(d):
    K, c = d.shape[-2:]
    return jnp.swapaxes(d, -3, -2).reshape(d.shape[:-3] + (K, _N_DEV * c))


def _local_step(x, mem, tgt, P, wsrc=None, fwd_hooks=None, bwd_hooks=None):
    fwd_hooks, bwd_hooks = fwd_hooks or {}, bwd_hooks or {}
    sent = {}

    def gather(kernel_name):
        return _Exchange(_weight_items(fwd_hooks[kernel_name], wsrc)) if kernel_name in fwd_hooks else None

    def placed(kernel_name, got):
        if kernel_name in fwd_hooks:
            _place_weights(P, fwd_hooks[kernel_name], got)

    def scatter(kernel_name):
        return _Exchange(_grad_items(bwd_hooks[kernel_name], G)) if kernel_name in bwd_hooks else None

    def received(kernel_name, got):
        if kernel_name in bwd_hooks:
            sent.update(zip(bwd_hooks[kernel_name], got))

    def mm(a, b, *, name, **kw):
        ex = gather(name) or scatter(name)
        out = _mm(a, b, name=name, xchg=ex, **kw)
        if ex is None:
            return out
        out, got = out
        placed(name, got)
        received(name, got)
        return out

    B, T, D = x.shape
    M = mem.shape[1]
    N = B * T
    W = D // 2
    H = W // _FOX_HD
    f_blk = 5 * W // _LANES
    G = {}
    x0 = x.reshape(N, D)
    memf = mem.reshape(B * M, D)

    h_e = _rms_fwd(x0, P["mix_norm_e"][0], name="rms_mix_e", xchg=gather("rms_mix_e"))
    if "rms_mix_e" in fwd_hooks:
        h_e, got = h_e
        placed("rms_mix_e", got)
    w_in_pad = _pad_w_in(P["w_in_e"][0][0], W, H)[None]
    proj = mm(h_e, w_in_pad, bl=0, name="mm_in_e", tn=896)
    proj3 = proj.reshape(B, T, -1)
    cum = _fox_cum(proj3, f_blk, P["fox_f_bias"][0], name="fox_cum")
    o_fox, lse, got = _fox_fwd(proj3, cum, name="fox_fwd", xchg=gather("fox_fwd"))
    placed("fox_fwd", got)
    bias_full = jnp.repeat(P["gmlp_b_s"][0].T, _GRP, axis=1)
    a_out = _gmlp_fwd(proj, P["gmlp_ln_g"][0], P["gmlp_ln_b"][0], P["gmlp_w_s"][0], bias_full, name="gmlp_fwd")
    mixcat = jnp.concatenate([o_fox.reshape(N, W), a_out], axis=1)
    x1 = mm(mixcat, P["w_out_e"][0], bl=0, res=x0, name="mm_out_e")

    def xa_ffn_fwd(xin, l):
        s = {}
        s["q"], s["h_xa"] = mm(xin, P["xa_wq"][l], bl=0, rms_fwd=P["xa_norm"][l], out_dtype=_MXU, name=f"mm_q{l}")
        s["mn"] = _rms_fwd(memf, P["mem_norm"][l], name=f"rms_mem{l}")
        s["kv"] = mm(s["mn"], P["xa_wkv"][l], bl=0, out_dtype=_MXU, name=f"mm_kv{l}")
        s["o"] = _xa_fwd(s["q"].reshape(B, T, D), s["kv"].reshape(B, M, 2 * D), name=f"xa_fwd{l}").reshape(N, D)
        s["x_mid"] = mm(s["o"], P["xa_wo"][l], bl=0, res=xin, name=f"mm_o{l}")
        s["gu"], s["act"], s["h_ffn"], got = _mm_gu(s["x_mid"], P["ffn_w_gu"][l], 0, rms_fwd=P["ffn_norm"][l], name=f"mm_gu{l}",
                                                    xchg=gather(f"mm_gu{l}"))
        placed(f"mm_gu{l}", got)
        s["x_in"] = xin
        xout = mm(s["act"], P["ffn_w_down"][l], bl=0, res=s["x_mid"], name=f"mm_down{l}", tn=512)
        return xout, s

    x3, s0 = xa_ffn_fwd(x1, 0)
    ag, y, h_o, _ = _mm_gu(x3, P["conv_w_in"][0], 0, bias=P["conv_b_in"][0], glu=True, keep=_F32, rms_fwd=P["mix_norm_o"][0],
                           name="mm_conv_in")
    C = y.shape[1]
    dw_w = jnp.pad(P["conv_dw_w"][0], ((0, _HALO - _CONV_K), (0, 0)))
    y2, y4, got = _conv_fwd(y.reshape(B, T, C), dw_w, P["conv_dw_b"][0], P["conv_ln_g"][0], P["conv_ln_b"][0], name="conv_fwd",
                            xchg=gather("conv_fwd"))
    placed("conv_fwd", got)
    x4 = mm(y4.reshape(N, C), P["conv_w_out"][0], bl=0, bias=P["conv_b_out"][0], res=x3, name="mm_conv_out")
    x6, s1 = xa_ffn_fwd(x4, 1)
    loss, dx, dg = _final_loss(x6, P["final_norm"], tgt.reshape(N, D), name="final_loss")
    G["final_norm"] = [dg]

    def xa_ffn_bwd(dx, s, l):
        for k in ("ffn_w_down", "ffn_w_gu", "ffn_norm", "xa_wo", "xa_wq", "xa_norm", "xa_wkv", "mem_norm"):
            G.setdefault(k, {})
        dgu = _mm_dgu(dx, P["ffn_w_down"][l], 0, s["gu"], name=f"mm_dgu{l}")
        G["ffn_w_down"][l] = mm(s["act"], dx, ta=True, out_dtype=_MXU, name=f"mm_dwdown{l}", tm=1408)
        G["ffn_w_gu"][l] = (mm(s["h_ffn"], dgu, ta=True, bl=0, out_dtype=_MXU, name=f"mm_dwg{l}", tn=1408),
                            mm(s["h_ffn"], dgu, ta=True, bl=1, out_dtype=_MXU, name=f"mm_dwu{l}", tn=1408))
        dx, G["ffn_norm"][l] = mm(dgu, P["ffn_w_gu"][l], al="cat", bl=0, tb=True, rms_bwd=(s["x_mid"], P["ffn_norm"][l], dx),
                                  name=f"mm_dhffn{l}", tm=512, tn=D, tk=1408)
        do = mm(dx, P["xa_wo"][l], bl=0, tb=True, out_dtype=_MXU, name=f"mm_do{l}")
        G["xa_wo"][l] = mm(s["o"], dx, ta=True, out_dtype=_MXU, name=f"mm_dwo{l}")
        dq, dkv = _xa_bwd(s["q"].reshape(B, T, D), s["kv"].reshape(B, M, 2 * D), do.reshape(B, T, D), name=f"xa_bwd{l}")
        dq, dkv = dq.reshape(N, D), dkv.reshape(B * M, 2 * D)
        G["xa_wq"][l] = mm(s["h_xa"], dq, ta=True, out_dtype=_MXU, name=f"mm_dwq{l}")
        dx, G["xa_norm"][l] = mm(dq, P["xa_wq"][l], bl=0, tb=True, rms_bwd=(s["x_in"], P["xa_norm"][l], dx), name=f"mm_dhxa{l}",
                                 tm=512, tn=D)
        G["xa_wkv"][l] = mm(s["mn"], dkv, ta=True, out_dtype=_MXU, name=f"mm_dwkv{l}")
        dmn = mm(dkv, P["xa_wkv"][l], bl=0, tb=True, name=f"mm_dmn{l}")
        G["mem_norm"][l] = _rms_bwd(memf, P["mem_norm"][l], dmn, None, name=f"rms_mem_bwd{l}")
        return dx

    dx = xa_ffn_bwd(dx, s1, 1)
    G["conv_b_out"] = [_colsum(dx, name="colsum_b_out")]
    dy4 = mm(dx, P["conv_w_out"][0], bl=0, tb=True, name="mm_dy4")
    G["conv_w_out"] = [mm(y4.reshape(N, C), dx, ta=True, out_dtype=_MXU, name="mm_dwconv_out")]
    dy2, dlg, dlb, ddb = _conv_ln_bwd(y2.reshape(N, C), dy4, P["conv_ln_g"][0], P["conv_ln_b"][0], name="conv_ln_bwd")
    G["conv_ln_g"], G["conv_ln_b"], G["conv_dw_b"] = [dlg], [dlb], [ddb]
    dag, ddw, dbin, got = _conv_bwd(y.reshape(B, T, C), dy2.reshape(B, T, C), ag.reshape(2, B, T, C), dw_w, name="conv_bwd",
                                    xchg=scatter("conv_bwd"))
    received("conv_bwd", got)
    G["conv_dw_w"], G["conv_b_in"] = [ddw[:_CONV_K]], [dbin]
    dag = dag.reshape(N, 2 * C)
    G["conv_w_in"] = [mm(h_o, dag, ta=True, out_dtype=_MXU, name="mm_dwconv_in")]
    dx, dg = mm(dag, P["conv_w_in"][0], bl=0, tb=True, rms_bwd=(x3, P["mix_norm_o"][0], dx), name="mm_dh_o", tm=512, tn=D)
    G["mix_norm_o"] = [dg]
    dx = xa_ffn_bwd(dx, s0, 0)
    G["w_out_e"] = [mm(mixcat, dx, ta=True, out_dtype=_MXU, name="mm_dwout_e")]
    dmix = mm(dx, P["w_out_e"][0], bl=0, tb=True, name="mm_dmix")
    dz, dlg, dlb, dws, dbias = _gmlp_bwd(proj, dmix, 1, P["gmlp_ln_g"][0], P["gmlp_ln_b"][0], P["gmlp_w_s"][0], bias_full,
                                         name="gmlp_bwd")
    G["gmlp_ln_g"], G["gmlp_ln_b"], G["gmlp_w_s"] = [dlg], [dlb], [dws]
    G["gmlp_b_s"] = [dbias[:, :2 * (W // _LANES)].T]
    dmix3 = dmix.reshape(B, T, D)
    dq, dk, dv, dcum, got = _fox_bwd(proj3, cum, dmix3, lse, name="fox_bwd", xchg=scatter("fox_bwd"))
    received("fox_bwd", got)
    df, dfb = _fox_cum_bwd(proj3, f_blk, P["fox_f_bias"][0], dcum.reshape(B, H, T), name="fox_cum_bwd")
    G["fox_f_bias"] = [dfb]
    dproj = jnp.concatenate([dq.reshape(N, W), dk.reshape(N, W), dv.reshape(N, W), dz, df.reshape(N, _LANES).astype(_MXU)], axis=1)
    G["w_in_e"] = [_unpad_w_in(mm(h_e, dproj, ta=True, out_dtype=_MXU, name="mm_dwin_e", tn=896), W, H)]
    dx, dg = mm(dproj, w_in_pad, bl=0, tb=True, rms_bwd=(x0, P["mix_norm_e"][0], dx), name="mm_dh_e", tm=512, tn=D)
    G["mix_norm_e"] = [dg]
    return loss, dx.reshape(B, T, D), G, sent


def _pad_w_in(w_in, W, H):
    f = w_in[:, 3 * W:3 * W + H]
    return jnp.concatenate([w_in[:, :3 * W], w_in[:, 3 * W + H:], jnp.pad(f, ((0, 0), (0, _LANES - H)))], axis=1)


def _unpad_w_in(g, W, H):
    return jnp.concatenate([g[:, :3 * W], g[:, 5 * W:5 * W + H], g[:, 3 * W:5 * W]], axis=1)


def kernel(x, mem, mix_norm_e, w_in_e, fox_f_bias, gmlp_ln_g, gmlp_ln_b, gmlp_w_s, gmlp_b_s, w_out_e, mix_norm_o, conv_w_in, conv_b_in, conv_dw_w, conv_dw_b, conv_ln_g, conv_ln_b, conv_w_out, conv_b_out, xa_norm, mem_norm, xa_wq, xa_wkv, xa_wo, ffn_norm, ffn_w_gu, ffn_w_down, final_norm, loss_target, m_mix_norm_e, m_w_in_e, m_fox_f_bias, m_gmlp_ln_g, m_gmlp_ln_b, m_gmlp_w_s, m_gmlp_b_s, m_w_out_e, m_mix_norm_o, m_conv_w_in, m_conv_b_in, m_conv_dw_w, m_conv_dw_b, m_conv_ln_g, m_conv_ln_b, m_conv_w_out, m_conv_b_out, m_xa_norm, m_mem_norm, m_xa_wq, m_xa_wkv, m_xa_wo, m_ffn_norm, m_ffn_w_gu, m_ffn_w_down, m_final_norm, v_mix_norm_e, v_w_in_e, v_fox_f_bias, v_gmlp_ln_g, v_gmlp_ln_b, v_gmlp_w_s, v_gmlp_b_s, v_w_out_e, v_mix_norm_o, v_conv_w_in, v_conv_b_in, v_conv_dw_w, v_conv_dw_b, v_conv_ln_g, v_conv_ln_b, v_conv_w_out, v_conv_b_out, v_xa_norm, v_mem_norm, v_xa_wq, v_xa_wkv, v_xa_wo, v_ffn_norm, v_ffn_w_gu, v_ffn_w_down, v_final_norm):
    env = dict(locals())
    w = {n: env[n] for n in _WEIGHTS}
    mom = {n: env["m_" + n] for n in _WEIGHTS}
    var = {n: env["v_" + n] for n in _WEIGHTS}
    D = x.shape[-1]
    W = D // 2
    H = W // _FOX_HD

    def layers(n):
        return w[n].shape[0] if w[n].ndim > 1 else 1

    wsrc = {n: (w[n].astype(_MXU) if n in dict(_BIG) else w[n]) for n, _ in _BIG + _SMALL_SHARDED}
    P = {n: w[n] for n in _REPLICATED}
    fwd_hooks = {
        "rms_mix_e": [("w_in_e", 0)],
        "mm_in_e": [("w_out_e", 0), ("xa_wq", 0)],
        "fox_fwd": [("xa_wkv", 0), ("xa_wo", 0), ("ffn_w_gu", 0)],
        "mm_o0": [("xa_wq", 1)],
        "mm_gu0": [("ffn_w_down", 0), ("conv_w_in", 0), ("conv_w_out", 0)] + [(n, 0) for n, _ in _SMALL_SHARDED],
        "mm_down0": [("xa_wkv", 1)],
        "conv_fwd": [("xa_wo", 1), ("ffn_w_gu", 1)],
        "mm_gu1": [("ffn_w_down", 1)],
    }

    last = [("mix_norm_e", 0)]
    in_dh_e = [("w_in_e", 0), ("fox_f_bias", 0)]
    in_conv = [(n, 1) for n in ("ffn_w_gu", "ffn_w_down", "xa_wq", "xa_wkv", "xa_wo", "xa_norm", "mem_norm", "ffn_norm")]
    in_conv += [("final_norm", 0), ("conv_w_out", 0)]
    every = [(n, l) for n in [n for n, _ in _BIG + _SMALL_SHARDED] + list(_REPLICATED) for l in range(layers(n))]
    in_dhffn0 = [("ffn_w_gu", 0)]
    bwd_hooks = {"conv_bwd": in_conv, "mm_dhffn0": in_dhffn0, "mm_dh_e": in_dh_e,
                 "fox_bwd": [pc for pc in every if pc not in last + in_dh_e + in_conv + in_dhffn0]}
    loss, grad_x, G, recv = _local_step(x, mem, loss_target, P, wsrc, fwd_hooks, bwd_hooks)
    loss = lax.psum(loss[0, 0], ("x", "y", "c"))
    recv.update(zip(last, _exchange(_grad_items(last, G), name="scatter_last")))

    def partials(n):
        return [recv[(n, l)] for l in range(layers(n))]

    res = {n: _adamw_big(partials(n), w[n], mom[n], var[n], name="adamw_" + n) for n, _ in _BIG}
    small = [n for n, _ in _SMALL_SHARDED] + list(_REPLICATED)

    def rows(a, n):
        r = recv[(n, 0)]
        return a.reshape((layers(n),) + r.shape[2:-1] + (-1,))

    outs = _adamw_small([(partials(n), rows(w[n], n), rows(mom[n], n), rows(var[n], n),
                          w[n].shape[-1] if w[n].shape[-1] != recv[(n, 0)].shape[-1] else None) for n in small], name="adamw_small")
    for n, o in zip(small, outs):
        res[n] = tuple(a.reshape(w[n].shape) for a in o)
    return (loss, grad_x, *[res[n][0] for n in _WEIGHTS], *[res[n][1] for n in _WEIGHTS],
            *[res[n][2] for n in _WEIGHTS], *[res[n][3] for n in _WEIGHTS])
```

```python
import functools
import math

import jax
import jax.numpy as jnp
from jax import lax
from jax.experimental import pallas as pl
from jax.experimental.pallas import tpu as pltpu

_F32 = jnp.float32
_MXU = jnp.bfloat16
_VMEM_LIMIT = 48 * 1024 * 1024
_LANES = 128
_EPS = 1e-6
_N_DEV = 8
_FOX_HD = 64
_FOX_SCALE = _FOX_HD ** -0.5
_FOX_TQ = 512
_CHUNK = 128
_GRP = 64
_CONV_K = 31
_HALO = 32
_XA_HEADS = 4
_GELU_C = math.sqrt(2.0 / math.pi)
_ADAM_LR, _ADAM_B1, _ADAM_B2, _ADAM_EPS, _ADAM_WD, _ADAM_STEP = 0.001, 0.9, 0.999, 1e-08, 0.01, 10
_FLAT_W = 1024
_FLAT_ALIGN = 16 * _FLAT_W
_BIG_ROWS = 128


def _params(*sem):
    return pltpu.CompilerParams(dimension_semantics=sem if sem else None, vmem_limit_bytes=_VMEM_LIMIT)


def _pick(n, pref):
    if n <= pref:
        return n
    best = None
    for t in range(_LANES, pref + 1, _LANES):
        if n % t == 0:
            best = t
    assert best is not None, (n, pref)
    return best


def _rows(n, pref):
    if n <= pref:
        return n
    t = pref
    while n % t:
        t //= 2
    assert t >= 8, (n, pref)
    return t


def _sigmoid(x):
    return 1.0 / (1.0 + jnp.exp(-x))


def _gelu(x):
    t = jnp.tanh(_GELU_C * (x + 0.044715 * (x * x * x)))
    return 0.5 * x * (1.0 + t)


def _gelu_grad(x):
    x2 = x * x
    t = jnp.tanh(_GELU_C * (x + 0.044715 * (x2 * x)))
    return 0.5 * (1.0 + t) + 0.5 * x * (1.0 - t * t) * (_GELU_C * (1.0 + 3.0 * 0.044715 * x2))


def _dot(a, b, ca, cb):
    return lax.dot_general(a, b, (((ca,), (cb,)), ((), ())), preferred_element_type=_F32)


def _rms_rows(xv, gain):
    return (xv * lax.rsqrt(jnp.mean(xv * xv, axis=-1, keepdims=True) + _EPS) * gain).astype(_MXU)


def _mm(a, b, *, name, ta=False, tb=False, al=None, bl=None, bk0=0, bias=None, res=None, rms_bwd=None, rms_fwd=None,
        out_dtype=_F32, tm=1024, tn=512, tk=1024, xchg=None):
    if ta:
        K, M = a.shape[-2:]
    else:
        M, K = a.shape[-2:]
    if al == "cat":
        assert not ta
        K = a.shape[0] * a.shape[-1]
    if tb:
        N, K2 = b.shape[-2:]
    else:
        K2, N = b.shape[-2:]
    assert K == K2 or (tb and K2 > K), (a.shape, b.shape, ta, tb)
    tm, tn = _pick(M, tm), _pick(N, tn)
    tk = K if (not ta and K <= 2816 and K2 == K) else _pick(a.shape[-1] if al == "cat" else K, tk)
    nk = K // tk
    assert bk0 % tk == 0
    kb = bk0 // tk
    grid = (M // tm, N // tn, nk)
    if al == "cat":
        per = a.shape[-1] // tk
        a_spec = pl.BlockSpec((None, tm, tk), lambda i, j, k: (k // per, i, k % per))
    elif a.ndim == 3:
        a_spec = (pl.BlockSpec((None, tk, tm), lambda i, j, k: (al, k, i)) if ta
                  else pl.BlockSpec((None, tm, tk), lambda i, j, k: (al, i, k)))
    else:
        a_spec = pl.BlockSpec((tk, tm), lambda i, j, k: (k, i)) if ta else pl.BlockSpec((tm, tk), lambda i, j, k: (i, k))
    if b.ndim == 3:
        b_spec = (pl.BlockSpec((None, tn, tk), lambda i, j, k: (bl, j, k + kb)) if tb
                  else pl.BlockSpec((None, tk, tn), lambda i, j, k: (bl, k, j)))
    else:
        b_spec = pl.BlockSpec((tn, tk), lambda i, j, k: (j, k)) if tb else pl.BlockSpec((tk, tn), lambda i, j, k: (k, j))
    in_specs, args = [a_spec, b_spec], [a, b]
    if bias is not None:
        in_specs.append(pl.BlockSpec((1, tn), lambda i, j, k: (0, j)))
        args.append(bias.reshape(1, N).astype(_F32))
    if res is not None:
        in_specs.append(pl.BlockSpec((tm, tn), lambda i, j, k: (i, j)))
        args.append(res)
    has_bias, has_res, has_rms = bias is not None, res is not None, rms_bwd is not None
    if has_rms:
        assert tn == N, (tn, N)
        x, g, dres = rms_bwd
        in_specs += [pl.BlockSpec((tm, N), lambda i, j, k: (i, 0)), pl.BlockSpec((1, N), lambda i, j, k: (0, 0)),
                     pl.BlockSpec((tm, N), lambda i, j, k: (i, 0))]
        args += [x, g.reshape(1, N), dres]
    has_norm = rms_fwd is not None
    if has_norm:
        assert not ta and nk == 1 and a.ndim == 2
        in_specs.append(pl.BlockSpec((1, K), lambda i, j, k: (0, 0)))
        args.append(rms_fwd.reshape(1, K))

    def body(*refs):
        a_ref, b_ref = refs[0], refs[1]
        pos = 2
        bias_ref = res_ref = None
        if has_bias:
            bias_ref = refs[pos]
            pos += 1
        if has_res:
            res_ref = refs[pos]
            pos += 1
        if has_rms:
            x_ref, g_ref, dres_ref = refs[pos:pos + 3]
            pos += 3
        if has_norm:
            gain_ref = refs[pos]
            pos += 1
        o_ref = refs[pos]
        pos += 1
        if has_rms:
            dg_ref = refs[pos]
            pos += 1
        if has_norm:
            h_ref = refs[pos]
            pos += 1
        acc_ref = refs[pos] if nk > 1 else None
        first_rows = pl.program_id(0) == 0
        if has_norm:
            av = _rms_rows(a_ref[...], gain_ref[...])
            h_ref[...] = av
        else:
            av = a_ref[...].astype(_MXU)
        p = _dot(av, b_ref[...].astype(_MXU), 0 if ta else 1, 1 if tb else 0)

        def finish(acc):
            if has_bias:
                acc = acc + bias_ref[...]
            if has_res:
                acc = acc + res_ref[...]
            if has_rms:
                @pl.when(first_rows)
                def _():
                    dg_ref[...] = jnp.zeros_like(dg_ref)

                xv = x_ref[...]
                r = lax.rsqrt(jnp.mean(xv * xv, axis=-1, keepdims=True) + _EPS)
                xh = xv * r
                dg_ref[...] += jnp.sum(acc * xh, axis=0, keepdims=True)
                dxn = acc * g_ref[...]
                acc = dres_ref[...] + r * (dxn - xh * jnp.mean(dxn * xh, axis=-1, keepdims=True))
            o_ref[...] = acc.astype(o_ref.dtype)

        if nk == 1:
            finish(p)
        else:
            k = pl.program_id(2)

            @pl.when(k == 0)
            def _():
                acc_ref[...] = p

            @pl.when(k > 0)
            def _():
                acc_ref[...] += p

            @pl.when(k == nk - 1)
            def _():
                finish(acc_ref[...])

    out_shape = [jax.ShapeDtypeStruct((M, N), out_dtype)]
    out_specs = [pl.BlockSpec((tm, tn), lambda i, j, k: (i, j))]
    if has_rms:
        out_shape.append(jax.ShapeDtypeStruct((1, N), _F32))
        out_specs.append(pl.BlockSpec((1, N), lambda i, j, k: (0, 0)))
    if has_norm:
        out_shape.append(jax.ShapeDtypeStruct((M, K), _MXU))
        out_specs.append(pl.BlockSpec((tm, K), lambda i, j, k: (i, 0)))
    outs, got = _fused_call(
        body, name=name, out_shape=out_shape, grid=grid, in_specs=in_specs, out_specs=out_specs,
        scratch_shapes=[pltpu.VMEM((tm, tn), _F32)] if nk > 1 else [],
        sem=("arbitrary",) * 3 if has_rms or has_norm else ("parallel", "parallel", "arbitrary"), args=args, xchg=xchg)
    out = tuple(outs) if has_rms or has_norm else outs[0]
    return out if xchg is None else (out, got)


def _rms_fwd(x, g, *, name, xchg=None):
    N, D = x.shape
    tr = _rows(N, 512)

    def body(x_ref, g_ref, o_ref):
        xv = x_ref[...]
        r = lax.rsqrt(jnp.mean(xv * xv, axis=-1, keepdims=True) + _EPS)
        o_ref[...] = (xv * r * g_ref[...]).astype(o_ref.dtype)

    (out,), got = _fused_call(
        body, name=name, out_shape=[jax.ShapeDtypeStruct((N, D), _MXU)], grid=(N // tr,),
        in_specs=[pl.BlockSpec((tr, D), lambda i: (i, 0)), pl.BlockSpec((1, D), lambda i: (0, 0))],
        out_specs=[pl.BlockSpec((tr, D), lambda i: (i, 0))], scratch_shapes=[], sem=("parallel",),
        args=(x, g.reshape(1, D)), xchg=xchg)
    return out if xchg is None else (out, got)


def _rms_bwd(x, g, dh, dres, *, name):
    N, D = x.shape
    tr = _rows(N, 256)
    has_res = dres is not None

    def body(*refs):
        if has_res:
            x_ref, g_ref, dh_ref, dres_ref, dx_ref, dg_ref = refs
        else:
            x_ref, g_ref, dh_ref, dg_ref = refs
        xv = x_ref[...]
        r = lax.rsqrt(jnp.mean(xv * xv, axis=-1, keepdims=True) + _EPS)
        xh = xv * r
        dhv = dh_ref[...].astype(_F32)

        @pl.when(pl.program_id(0) == 0)
        def _():
            dg_ref[...] = jnp.zeros_like(dg_ref)

        dg_ref[...] += jnp.sum(dhv * xh, axis=0, keepdims=True)
        if has_res:
            dxn = dhv * g_ref[...]
            dx = r * (dxn - xh * jnp.mean(dxn * xh, axis=-1, keepdims=True))
            dx_ref[...] = dres_ref[...] + dx

    row = pl.BlockSpec((tr, D), lambda i: (i, 0))
    vec = pl.BlockSpec((1, D), lambda i: (0, 0))
    if has_res:
        out_shape = (jax.ShapeDtypeStruct((N, D), _F32), jax.ShapeDtypeStruct((1, D), _F32))
        out_specs = (row, vec)
        in_specs, args = [row, vec, row, row], (x, g.reshape(1, D), dh, dres)
    else:
        out_shape = jax.ShapeDtypeStruct((1, D), _F32)
        out_specs = vec
        in_specs, args = [row, vec, row], (x, g.reshape(1, D), dh)
    return pl.pallas_call(
        body, name=name, out_shape=out_shape, grid=(N // tr,), in_specs=in_specs, out_specs=out_specs,
        compiler_params=_params("arbitrary"),
    )(*args)


def _colsum(a, *, name):
    M, C = a.shape
    tr = _rows(M, 512)

    def body(a_ref, o_ref):
        @pl.when(pl.program_id(0) == 0)
        def _():
            o_ref[...] = jnp.zeros_like(o_ref)

        o_ref[...] += jnp.sum(a_ref[...].astype(_F32), axis=0, keepdims=True)

    return pl.pallas_call(
        body, name=name, out_shape=jax.ShapeDtypeStruct((1, C), _F32), grid=(M // tr,),
        in_specs=[pl.BlockSpec((tr, C), lambda i: (i, 0))], out_specs=pl.BlockSpec((1, C), lambda i: (0, 0)),
        compiler_params=_params("arbitrary"),
    )(a)


def _final_loss(x, g, tgt, *, name):
    N, D = x.shape
    tr = _rows(N, 256)

    def body(x_ref, g_ref, t_ref, loss_ref, dx_ref, dg_ref):
        xv = x_ref[...]
        r = lax.rsqrt(jnp.mean(xv * xv, axis=-1, keepdims=True) + _EPS)
        xh = xv * r
        gv = g_ref[...]
        diff = xh * gv - t_ref[...]

        @pl.when(pl.program_id(0) == 0)
        def _():
            loss_ref[...] = jnp.zeros_like(loss_ref)
            dg_ref[...] = jnp.zeros_like(dg_ref)

        part = jnp.sum(jnp.sum(diff * diff, axis=1, keepdims=True), axis=0, keepdims=True) * (0.5 / D)
        loss_ref[...] += jnp.broadcast_to(part, loss_ref.shape)
        dy = diff * (1.0 / D)
        dg_ref[...] += jnp.sum(dy * xh, axis=0, keepdims=True)
        dxn = dy * gv
        dx_ref[...] = r * (dxn - xh * jnp.mean(dxn * xh, axis=-1, keepdims=True))

    row = pl.BlockSpec((tr, D), lambda i: (i, 0))
    vec = pl.BlockSpec((1, D), lambda i: (0, 0))
    return pl.pallas_call(
        body, name=name,
        out_shape=(jax.ShapeDtypeStruct((8, _LANES), _F32), jax.ShapeDtypeStruct((N, D), _F32), jax.ShapeDtypeStruct((1, D), _F32)),
        grid=(N // tr,), in_specs=[row, vec, row],
        out_specs=(pl.BlockSpec((8, _LANES), lambda i: (0, 0)), row, vec),
        compiler_params=_params("arbitrary"),
    )(x, g.reshape(1, D), tgt)


def _mm_gu(h, w_gu, l, *, name, bias=None, glu=False, keep=None, rms_fwd=None, tm=512, tn=1408, xchg=None):
    keep = _MXU if keep is None else keep
    N, K = h.shape
    H = w_gu.shape[-1] // 2
    tm, tn = _pick(N, tm), _pick(H, tn)
    nj = H // tn
    has_bias, has_norm = bias is not None, rms_fwd is not None

    def body(*refs):
        h_ref, wp_ref, wq_ref = refs[:3]
        pair_ref, act_ref = refs[3 + 2 * has_bias + has_norm:][:2]
        if has_norm:
            hv = _rms_rows(h_ref[...], refs[3 + 2 * has_bias][...])
            refs[-1][...] = hv
        else:
            hv = h_ref[...].astype(_MXU)
        p = _dot(hv, wp_ref[...].astype(_MXU), 1, 0)
        q = _dot(hv, wq_ref[...].astype(_MXU), 1, 0)
        if has_bias:
            p = p + refs[3][...]
            q = q + refs[4][...]
        pair_ref[0] = p.astype(pair_ref.dtype)
        pair_ref[1] = q.astype(pair_ref.dtype)
        act_ref[...] = (p * _sigmoid(q) if glu else p * _sigmoid(p) * q).astype(act_ref.dtype)

    in_specs = [pl.BlockSpec((tm, K), lambda i, j: (i, 0)), pl.BlockSpec((None, K, tn), lambda i, j: (l, 0, j)),
                pl.BlockSpec((None, K, tn), lambda i, j: (l, 0, j + nj))]
    args = [h, w_gu, w_gu]
    if has_bias:
        b2 = bias.reshape(1, 2 * H).astype(_F32)
        in_specs += [pl.BlockSpec((1, tn), lambda i, j: (0, j)), pl.BlockSpec((1, tn), lambda i, j: (0, j + nj))]
        args += [b2, b2]
    out_shape = [jax.ShapeDtypeStruct((2, N, H), keep), jax.ShapeDtypeStruct((N, H), keep)]
    out_specs = [pl.BlockSpec((2, tm, tn), lambda i, j: (0, i, j)), pl.BlockSpec((tm, tn), lambda i, j: (i, j))]
    if has_norm:
        in_specs.append(pl.BlockSpec((1, K), lambda i, j: (0, 0)))
        args.append(rms_fwd.reshape(1, K))
        out_shape.append(jax.ShapeDtypeStruct((N, K), _MXU))
        out_specs.append(pl.BlockSpec((tm, K), lambda i, j: (i, 0)))
    outs, got = _fused_call(
        body, name=name, out_shape=out_shape, grid=(N // tm, nj), in_specs=in_specs, out_specs=out_specs,
        scratch_shapes=[], sem=("arbitrary", "arbitrary") if has_norm else ("parallel", "parallel"), args=args, xchg=xchg)
    return (*outs, got)


def _mm_dgu(dx, w_down, l, gu, *, name, tm=512, tn=1408):
    N, K = dx.shape
    H = w_down.shape[-2]
    tm, tn = _pick(N, tm), _pick(H, tn)

    def body(dx_ref, w_ref, gu_ref, o_ref):
        d = _dot(dx_ref[...].astype(_MXU), w_ref[...].astype(_MXU), 1, 1)
        g, u = gu_ref[0].astype(_F32), gu_ref[1].astype(_F32)
        sg = _sigmoid(g)
        o_ref[0] = (d * u * (sg * (1.0 + g * (1.0 - sg)))).astype(o_ref.dtype)
        o_ref[1] = (d * (g * sg)).astype(o_ref.dtype)

    return pl.pallas_call(
        body, name=name, out_shape=jax.ShapeDtypeStruct((2, N, H), _MXU), grid=(N // tm, H // tn),
        in_specs=[pl.BlockSpec((tm, K), lambda i, j: (i, 0)), pl.BlockSpec((None, tn, K), lambda i, j: (l, j, 0)),
                  pl.BlockSpec((2, tm, tn), lambda i, j: (0, i, j))],
        out_specs=pl.BlockSpec((2, tm, tn), lambda i, j: (0, i, j)), compiler_params=_params("parallel", "parallel"),
    )(dx, w_down, gu)


def _gmlp_mix(vb, w, trans):
    tr, W = vb.shape
    lane = lax.broadcasted_iota(jnp.int32, (_CHUNK, _LANES), 1)
    rows = []
    for c in range(tr // _CHUNK):
        tiles = []
        for j in range(W // _LANES):
            t = vb[c * _CHUNK:(c + 1) * _CHUNK, j * _LANES:(j + 1) * _LANES]
            ma = _dot(w[2 * j], t, 0 if trans else 1, 0)
            mb = _dot(w[2 * j + 1], t, 0 if trans else 1, 0)
            tiles.append(jnp.where(lane < _GRP, ma, mb))
        rows.append(jnp.concatenate(tiles, axis=1))
    return jnp.concatenate(rows, axis=0)


def _tril_w(w_ref):
    r = lax.broadcasted_iota(jnp.int32, (_CHUNK, _CHUNK), 0)
    c = lax.broadcasted_iota(jnp.int32, (_CHUNK, _CHUNK), 1)
    return jnp.where((r >= c)[None], w_ref[...], 0.0).astype(_MXU)


def _layernorm_stats(v):
    mu = jnp.mean(v, axis=-1, keepdims=True)
    xc = v - mu
    rstd = lax.rsqrt(jnp.mean(xc * xc, axis=-1, keepdims=True) + _EPS)
    return xc * rstd, rstd


def _gmlp_fwd(proj, ln_g, ln_b, w_s, bias_full, *, name):
    N = proj.shape[0]
    W = ln_g.shape[-1]
    G = w_s.shape[0]
    tr = _rows(N, 512)
    ub, vb_ = 3, 4

    def body(u_ref, v_ref, g_ref, b_ref, w_ref, bias_ref, o_ref):
        u = _gelu(u_ref[...])
        xh, _ = _layernorm_stats(_gelu(v_ref[...]))
        vgn = xh * g_ref[...] + b_ref[...]
        mixed = _gmlp_mix(vgn.astype(_MXU), _tril_w(w_ref), False)
        bias = jnp.concatenate([bias_ref[...]] * (tr // _CHUNK), axis=0)
        o_ref[...] = (u * (mixed + bias)).astype(o_ref.dtype)

    vec = pl.BlockSpec((1, W), lambda i: (0, 0))
    return pl.pallas_call(
        body, name=name, out_shape=jax.ShapeDtypeStruct((N, W), _MXU), grid=(N // tr,),
        in_specs=[pl.BlockSpec((tr, W), lambda i: (i, ub)), pl.BlockSpec((tr, W), lambda i: (i, vb_)), vec, vec,
                  pl.BlockSpec((G, _CHUNK, _CHUNK), lambda i: (0, 0, 0)), pl.BlockSpec((_CHUNK, W), lambda i: (0, 0))],
        out_specs=pl.BlockSpec((tr, W), lambda i: (i, 0)), compiler_params=_params("parallel"),
    )(proj, proj, ln_g.reshape(1, W), ln_b.reshape(1, W), w_s, bias_full)


def _gmlp_bwd(proj, da_src, da_blk, ln_g, ln_b, w_s, bias_full, *, name):
    N = proj.shape[0]
    W = ln_g.shape[-1]
    G = w_s.shape[0]
    tr = _rows(N, 512)
    nch = tr // _CHUNK

    def body(u_ref, v_ref, da_ref, g_ref, b_ref, w_ref, bias_ref, dz_ref, dg_ref, db_ref, dw_ref, dbias_ref):
        @pl.when(pl.program_id(0) == 0)
        def _():
            dg_ref[...] = jnp.zeros_like(dg_ref)
            db_ref[...] = jnp.zeros_like(db_ref)
            dw_ref[...] = jnp.zeros_like(dw_ref)
            dbias_ref[...] = jnp.zeros_like(dbias_ref)

        u_pre, v_pre = u_ref[...], v_ref[...]
        ug = _gelu(u_pre)
        xh, rstd = _layernorm_stats(_gelu(v_pre))
        lg = g_ref[...]
        vgn = xh * lg + b_ref[...]
        vb = vgn.astype(_MXU)
        wt = _tril_w(w_ref)
        mixed = _gmlp_mix(vb, wt, False)
        bias = jnp.concatenate([bias_ref[...]] * nch, axis=0)
        da = da_ref[...].astype(_F32)
        du = da * (mixed + bias)
        dm = da * ug
        dmb = dm.astype(_MXU)
        lane = lax.broadcasted_iota(jnp.int32, (_CHUNK, _LANES), 1)
        r = lax.broadcasted_iota(jnp.int32, (_CHUNK, _CHUNK), 0)
        c = lax.broadcasted_iota(jnp.int32, (_CHUNK, _CHUNK), 1)
        tril = r >= c
        dmsum = dm[0:_CHUNK]
        for ch in range(1, nch):
            dmsum = dmsum + dm[ch * _CHUNK:(ch + 1) * _CHUNK]
        dbias = jnp.zeros((_CHUNK, _LANES), _F32)
        for j in range(W // _LANES):
            tile = dmsum[:, j * _LANES:(j + 1) * _LANES]
            sa = jnp.sum(jnp.where(lane < _GRP, tile, 0.0), axis=1, keepdims=True)
            sb = jnp.sum(jnp.where(lane >= _GRP, tile, 0.0), axis=1, keepdims=True)
            dbias = dbias + jnp.where(lane == 2 * j, sa, 0.0) + jnp.where(lane == 2 * j + 1, sb, 0.0)
            acc_a = jnp.zeros((_CHUNK, _CHUNK), _F32)
            acc_b = jnp.zeros((_CHUNK, _CHUNK), _F32)
            for ch in range(nch):
                dt = dmb[ch * _CHUNK:(ch + 1) * _CHUNK, j * _LANES:(j + 1) * _LANES]
                vt = vb[ch * _CHUNK:(ch + 1) * _CHUNK, j * _LANES:(j + 1) * _LANES]
                acc_a = acc_a + _dot(jnp.where(lane < _GRP, dt, jnp.zeros_like(dt)), vt, 1, 1)
                acc_b = acc_b + _dot(jnp.where(lane >= _GRP, dt, jnp.zeros_like(dt)), vt, 1, 1)
            dw_ref[2 * j] += jnp.where(tril, acc_a, 0.0)
            dw_ref[2 * j + 1] += jnp.where(tril, acc_b, 0.0)
        dbias_ref[...] += dbias
        dvgn = _gmlp_mix(dmb, wt, True)
        dg_ref[...] += jnp.sum(dvgn * xh, axis=0, keepdims=True)
        db_ref[...] += jnp.sum(dvgn, axis=0, keepdims=True)
        dxh = dvgn * lg
        dvg = rstd * (dxh - jnp.mean(dxh, axis=-1, keepdims=True) - xh * jnp.mean(dxh * xh, axis=-1, keepdims=True))
        dz_ref[:, :W] = (du * _gelu_grad(u_pre)).astype(dz_ref.dtype)
        dz_ref[:, W:] = (dvg * _gelu_grad(v_pre)).astype(dz_ref.dtype)

    vec = pl.BlockSpec((1, W), lambda i: (0, 0))
    wspec = pl.BlockSpec((G, _CHUNK, _CHUNK), lambda i: (0, 0, 0))
    return pl.pallas_call(
        body, name=name,
        out_shape=(jax.ShapeDtypeStruct((N, 2 * W), _MXU), jax.ShapeDtypeStruct((1, W), _F32), jax.ShapeDtypeStruct((1, W), _F32),
                   jax.ShapeDtypeStruct((G, _CHUNK, _CHUNK), _F32), jax.ShapeDtypeStruct((_CHUNK, _LANES), _F32)),
        grid=(N // tr,),
        in_specs=[pl.BlockSpec((tr, W), lambda i: (i, 3)), pl.BlockSpec((tr, W), lambda i: (i, 4)),
                  pl.BlockSpec((tr, W), lambda i: (i, da_blk)), vec, vec, wspec, pl.BlockSpec((_CHUNK, W), lambda i: (0, 0))],
        out_specs=(pl.BlockSpec((tr, 2 * W), lambda i: (i, 0)), vec, vec, wspec, pl.BlockSpec((_CHUNK, _LANES), lambda i: (0, 0))),
        compiler_params=_params("arbitrary"),
    )(proj, proj, da_src, ln_g.reshape(1, W), ln_b.reshape(1, W), w_s, bias_full)


def _lane_cumsum(v):
    T = v.shape[1]
    lane = lax.broadcasted_iota(jnp.int32, (8, _LANES), 1)
    carry = jnp.zeros((8, 1), _F32)
    out = []
    for ch in range(T // _LANES):
        blk = v[:, ch * _LANES:(ch + 1) * _LANES]
        sh = 1
        while sh < _LANES:
            blk = blk + jnp.where(lane >= sh, pltpu.roll(blk, sh, 1), 0.0)
            sh *= 2
        blk = blk + carry
        carry = blk[:, _LANES - 1:_LANES]
        out.append(blk)
    return jnp.concatenate(out, axis=1), carry


def _log_sigmoid(x):
    return jnp.minimum(x, 0.0) - jnp.log(1.0 + jnp.exp(-jnp.abs(x)))


def _fox_cum(proj3, f_blk, f_bias, *, name):
    B, T, _ = proj3.shape
    H = f_bias.shape[-1]
    assert H == 8

    def body(f_ref, b_ref, o_ref):
        x = f_ref[0].T[0:8, :] + b_ref[...]
        cum, _ = _lane_cumsum(_log_sigmoid(x))
        o_ref[0] = cum

    return pl.pallas_call(
        body, name=name, out_shape=jax.ShapeDtypeStruct((B, 8, T), _F32), grid=(B,),
        in_specs=[pl.BlockSpec((1, T, _LANES), lambda b: (b, 0, f_blk)), pl.BlockSpec((8, 1), lambda b: (0, 0))],
        out_specs=pl.BlockSpec((1, 8, T), lambda b: (b, 0, 0)), compiler_params=_params("parallel"),
    )(proj3, f_bias.reshape(8, 1))


def _fox_cum_bwd(proj3, f_blk, f_bias, dcum, *, name):
    B, T, _ = proj3.shape

    def body(f_ref, b_ref, dc_ref, df_ref, dbias_ref):
        @pl.when(pl.program_id(0) == 0)
        def _():
            dbias_ref[...] = jnp.zeros_like(dbias_ref)

        x = f_ref[0].T[0:8, :] + b_ref[...]
        dc = dc_ref[0]
        incl, total = _lane_cumsum(dc)
        dlf = total - incl + dc
        df = dlf * _sigmoid(-x)
        full = jnp.concatenate([df, jnp.zeros((_LANES - 8, T), _F32)], axis=0).T
        dbias_ref[...] += jnp.sum(full, axis=0, keepdims=True)
        df_ref[0] = full

    return pl.pallas_call(
        body, name=name,
        out_shape=(jax.ShapeDtypeStruct((B, T, _LANES), _F32), jax.ShapeDtypeStruct((1, _LANES), _F32)), grid=(B,),
        in_specs=[pl.BlockSpec((1, T, _LANES), lambda b: (b, 0, f_blk)), pl.BlockSpec((8, 1), lambda b: (0, 0)),
                  pl.BlockSpec((1, 8, T), lambda b: (b, 0, 0))],
        out_specs=(pl.BlockSpec((1, T, _LANES), lambda b: (b, 0, 0)), pl.BlockSpec((1, _LANES), lambda b: (0, 0))),
        compiler_params=_params("arbitrary"),
    )(proj3, f_bias.reshape(8, 1), dcum)


def _cum_row(cum_ref, h, start, size):
    blk = cum_ref[0, :, pl.ds(start, size)]
    sub = lax.broadcasted_iota(jnp.int32, (blk.shape[0], 1), 0)
    return jnp.sum(jnp.where(sub == h, blk, 0.0), axis=0, keepdims=True)


def _causal(tq, q0, k0):
    r = lax.broadcasted_iota(jnp.int32, (tq, tq), 0)
    c = lax.broadcasted_iota(jnp.int32, (tq, tq), 1)
    return (r + q0) >= (c + k0)


def _fused_call(body, *, name, out_shape, grid, in_specs, out_specs, scratch_shapes, sem, args, xchg):
    out_shape, in_specs, out_specs, scratch_shapes = list(out_shape), list(in_specs), list(out_specs), list(scratch_shapes)
    if xchg is None:
        res = pl.pallas_call(body, name=name, out_shape=out_shape, grid=grid, in_specs=in_specs, out_specs=out_specs,
                             scratch_shapes=scratch_shapes, compiler_params=_params(*sem))(*args)
        return list(res), []
    n_in, n_out, n_scr = len(in_specs), len(out_specs), len(scratch_shapes)

    def fused(*refs):
        ins, refs = refs[:n_in], refs[n_in:]
        xs, refs = refs[:xchg.n_src], refs[xchg.n_src:]
        outs, refs = refs[:n_out], refs[n_out:]
        xd, refs = refs[:xchg.n_dst], refs[xchg.n_dst:]
        scr, sems = refs[:n_scr], refs[n_scr:]
        first = last = None
        for d, g in enumerate(grid):
            i = pl.program_id(d)
            first = (i == 0) if first is None else first & (i == 0)
            last = (i == g - 1) if last is None else last & (i == g - 1)

        @pl.when(first)
        def _():
            xchg.start(xs, xd, sems)

        body(*ins, *outs, *scr)

        @pl.when(last)
        def _():
            xchg.finish(xs, xd, sems)

    res = pl.pallas_call(
        fused, name=name, out_shape=out_shape + xchg.out_shapes, grid=grid, in_specs=in_specs + xchg.in_specs,
        out_specs=out_specs + xchg.out_specs, scratch_shapes=scratch_shapes + xchg.scratch,
        compiler_params=_params(*["arbitrary"] * len(grid)),
    )(*args, *xchg.srcs)
    return list(res[:n_out]), list(res[n_out:])


def _fox_fwd(proj3, cum, *, name, xchg=None):
    B, T, _ = proj3.shape
    H = cum.shape[1]
    W = H * _FOX_HD
    npair = W // _LANES
    tq = _rows(T, _FOX_TQ)
    nq = T // tq

    def body(q_ref, k_ref, v_ref, cum_ref, o_ref, lse_ref):
        p = pl.program_id(1)
        i = pl.program_id(2)
        q0 = pl.multiple_of(i * tq, tq)
        lane = lax.broadcasted_iota(jnp.int32, (1, _LANES), 1)
        q2 = q_ref[0] * _FOX_SCALE
        heads = []
        for hh in range(2):
            msk = (lane < _FOX_HD) if hh == 0 else (lane >= _FOX_HD)
            h = 2 * p + hh
            heads.append((msk, h, jnp.where(msk, q2, 0.0).astype(_MXU), _cum_row(cum_ref, h, q0, _LANES)[:, 0:1]))

        def step(jj, carry, masked):
            k0 = pl.multiple_of(jj * tq, tq)
            k2 = k_ref[0, pl.ds(k0, tq), :].astype(_MXU)
            v2 = v_ref[0, pl.ds(k0, tq), :]
            out = []
            for (msk, h, qm, c0), (m_prev, l_prev, acc) in zip(heads, carry):
                s = _dot(qm, k2, 1, 1) + (c0 - _cum_row(cum_ref, h, k0, tq))
                if masked:
                    s = jnp.where(_causal(tq, q0, k0), s, -jnp.inf)
                m_new = jnp.maximum(m_prev, jnp.max(s, axis=1, keepdims=True))
                alpha = jnp.exp(m_prev - m_new)
                e = jnp.exp(s - m_new)
                l_new = alpha * l_prev + jnp.sum(e, axis=1, keepdims=True)
                vm = jnp.where(msk, v2, 0.0).astype(_MXU)
                out.append((m_new, l_new, alpha * acc + _dot(e.astype(_MXU), vm, 1, 0)))
            return tuple(out)

        init = tuple((jnp.full((tq, 1), -jnp.inf, _F32), jnp.zeros((tq, 1), _F32), jnp.zeros((tq, _LANES), _F32)) for _ in heads)
        carry = step(i, lax.fori_loop(0, i, functools.partial(step, masked=False), init), True)
        o2 = jnp.zeros((tq, _LANES), _F32)
        for hh, (m, l, acc) in enumerate(carry):
            o2 = o2 + acc / l
            lse_ref[0, hh] = jnp.broadcast_to(m + jnp.log(l), (tq, _LANES))
        o_ref[0] = o2.astype(o_ref.dtype)

    (o, lse), got = _fused_call(
        body, name=name,
        out_shape=(jax.ShapeDtypeStruct((B, T, W), _MXU), jax.ShapeDtypeStruct((B, H, T, _LANES), _F32)),
        grid=(B, npair, nq),
        in_specs=[pl.BlockSpec((1, tq, _LANES), lambda b, p, i: (b, i, p)),
                  pl.BlockSpec((1, T, _LANES), lambda b, p, i: (b, 0, npair + p)),
                  pl.BlockSpec((1, T, _LANES), lambda b, p, i: (b, 0, 2 * npair + p)),
                  pl.BlockSpec((1, H, T), lambda b, p, i: (b, 0, 0))],
        out_specs=(pl.BlockSpec((1, tq, _LANES), lambda b, p, i: (b, i, p)),
                   pl.BlockSpec((1, 2, tq, _LANES), lambda b, p, i: (b, p, i, 0))),
        scratch_shapes=[], sem=("parallel", "parallel", "parallel"), args=(proj3, proj3, proj3, cum), xchg=xchg)
    return o, lse, got


def _fox_bwd(proj3, cum, do3, lse, *, name, xchg=None):
    B, T, _ = proj3.shape
    H = cum.shape[1]
    W = H * _FOX_HD
    npair = W // _LANES
    tq = _rows(T, _FOX_TQ)
    nq = T // tq

    def body(q_ref, k_ref, v_ref, cum_ref, do_ref, lse_ref, dq_ref, dk_ref, dv_ref, dc_ref, p_scr, dp_scr, dk_acc, dv_acc, dc_acc):
        p = pl.program_id(1)
        i = pl.program_id(2)
        q0 = pl.multiple_of(i * tq, tq)
        lane = lax.broadcasted_iota(jnp.int32, (1, _LANES), 1)

        @pl.when(i == 0)
        def _():
            dk_acc[...] = jnp.zeros_like(dk_acc)
            dv_acc[...] = jnp.zeros_like(dv_acc)
            dc_acc[...] = jnp.zeros_like(dc_acc)

        q2 = q_ref[0] * _FOX_SCALE
        do2 = do_ref[0].astype(_F32)
        heads = []
        for hh in range(2):
            msk = (lane < _FOX_HD) if hh == 0 else (lane >= _FOX_HD)
            h = 2 * p + hh
            heads.append((hh, msk, h, jnp.where(msk, q2, 0.0).astype(_MXU), jnp.where(msk, do2, 0.0).astype(_MXU),
                          _cum_row(cum_ref, h, q0, _LANES)[:, 0:1], lse_ref[0, hh][:, 0:1]))

        def first(jj, deltas, masked):
            k0 = pl.multiple_of(jj * tq, tq)
            kb = k_ref[0, pl.ds(k0, tq), :].astype(_MXU)
            vb = v_ref[0, pl.ds(k0, tq), :].astype(_MXU)
            out = []
            for (hh, _, h, qm, dom, c0, lse_h), delta in zip(heads, deltas):
                s = _dot(qm, kb, 1, 1) + (c0 - _cum_row(cum_ref, h, k0, tq))
                pr = jnp.exp(s - lse_h)
                if masked:
                    pr = jnp.where(_causal(tq, q0, k0), pr, 0.0)
                dp = _dot(dom, vb, 1, 1)
                p_scr[hh, jj] = pr
                dp_scr[hh, jj] = dp
                out.append(delta + jnp.sum(pr * dp, axis=1, keepdims=True))
            return tuple(out)

        zero = tuple(jnp.zeros((tq, 1), _F32) for _ in heads)
        deltas = first(i, lax.fori_loop(0, i, functools.partial(first, masked=False), zero), True)

        def second(jj, dq):
            k0 = pl.multiple_of(jj * tq, tq)
            k2 = k_ref[0, pl.ds(k0, tq), :]
            dk = jnp.zeros((tq, _LANES), _F32)
            dv = jnp.zeros((tq, _LANES), _F32)
            for (hh, msk, _, qm, dom, _, _), delta in zip(heads, deltas):
                pr = p_scr[hh, jj]
                ds = pr * (dp_scr[hh, jj] - delta)
                dsb = ds.astype(_MXU)
                dv = dv + _dot(pr.astype(_MXU), dom, 0, 0)
                dk = dk + _dot(dsb, qm, 0, 0)
                dc_acc[hh:hh + 1, pl.ds(k0, tq)] += jnp.sum(ds, axis=0, keepdims=True)
                dq = dq + _dot(dsb, jnp.where(msk, k2, 0.0).astype(_MXU), 1, 0)
            dv_acc[pl.ds(k0, tq), :] += dv
            dk_acc[pl.ds(k0, tq), :] += dk
            return dq

        dq2 = lax.fori_loop(0, i + 1, second, jnp.zeros((tq, _LANES), _F32))
        dq_ref[0] = (dq2 * _FOX_SCALE).astype(dq_ref.dtype)

        @pl.when(i == nq - 1)
        def _():
            dk_ref[0] = dk_acc[...].astype(dk_ref.dtype)
            dv_ref[0] = dv_acc[...].astype(dv_ref.dtype)
            dc_ref[0, 0] = -dc_acc[...]

    full = lambda blk: pl.BlockSpec((1, T, _LANES), lambda b, p, i, blk=blk: (b, 0, blk * npair + p))
    part = lambda blk: pl.BlockSpec((1, tq, _LANES), lambda b, p, i, blk=blk: (b, i, blk * npair + p))
    (dq, dk, dv, dcum), got = _fused_call(
        body, name=name,
        out_shape=(jax.ShapeDtypeStruct((B, T, W), _MXU), jax.ShapeDtypeStruct((B, T, W), _MXU),
                   jax.ShapeDtypeStruct((B, T, W), _MXU), jax.ShapeDtypeStruct((B, npair, 2, T), _F32)),
        grid=(B, npair, nq),
        in_specs=[part(0), full(1), full(2), pl.BlockSpec((1, H, T), lambda b, p, i: (b, 0, 0)), part(0),
                  pl.BlockSpec((1, 2, tq, _LANES), lambda b, p, i: (b, p, i, 0))],
        out_specs=(part(0), full(0), full(0), pl.BlockSpec((1, 1, 2, T), lambda b, p, i: (b, p, 0, 0))),
        scratch_shapes=[pltpu.VMEM((2, nq, tq, tq), _F32), pltpu.VMEM((2, nq, tq, tq), _F32), pltpu.VMEM((T, _LANES), _F32),
                        pltpu.VMEM((T, _LANES), _F32), pltpu.VMEM((2, T), _F32)],
        sem=("parallel", "parallel", "arbitrary"), args=(proj3, proj3, proj3, cum, do3, lse), xchg=xchg)
    return dq, dk, dv, dcum, got


def _xa_probs(qh, kh, scale):
    s = _dot(qh, kh, 1, 1) * scale
    e = jnp.exp(s - jnp.max(s, axis=1, keepdims=True))
    return e / jnp.sum(e, axis=1, keepdims=True)


def _xa_fwd(q3, kv3, *, name):
    B, T, D = q3.shape
    M = kv3.shape[1]
    hd = D // _XA_HEADS
    scale = hd ** -0.5
    tq = _rows(T, 512)

    def body(q_ref, kv_ref, o_ref):
        for h in range(_XA_HEADS):
            sl = slice(h * hd, (h + 1) * hd)
            p = _xa_probs(q_ref[0, :, sl], kv_ref[0, :, sl], scale)
            o_ref[0, :, sl] = _dot(p.astype(_MXU), kv_ref[0, :, D + h * hd:D + (h + 1) * hd], 1, 0).astype(o_ref.dtype)

    return pl.pallas_call(
        body, name=name, out_shape=jax.ShapeDtypeStruct((B, T, D), _MXU), grid=(B, T // tq),
        in_specs=[pl.BlockSpec((1, tq, D), lambda b, i: (b, i, 0)), pl.BlockSpec((1, M, 2 * D), lambda b, i: (b, 0, 0))],
        out_specs=pl.BlockSpec((1, tq, D), lambda b, i: (b, i, 0)), compiler_params=_params("parallel", "parallel"),
    )(q3, kv3)


def _xa_bwd(q3, kv3, do3, *, name):
    B, T, D = q3.shape
    M = kv3.shape[1]
    hd = D // _XA_HEADS
    scale = hd ** -0.5
    tq = _rows(T, 512)

    def body(q_ref, kv_ref, do_ref, dq_ref, dkv_ref):
        @pl.when(pl.program_id(1) == 0)
        def _():
            dkv_ref[...] = jnp.zeros_like(dkv_ref)

        for h in range(_XA_HEADS):
            sl = slice(h * hd, (h + 1) * hd)
            slv = slice(D + h * hd, D + (h + 1) * hd)
            qh, kh, vh, doh = q_ref[0, :, sl], kv_ref[0, :, sl], kv_ref[0, :, slv], do_ref[0, :, sl]
            p = _xa_probs(qh, kh, scale)
            dkv_ref[0, :, slv] += _dot(p.astype(_MXU), doh, 0, 0)
            dp = _dot(doh, vh, 1, 1)
            ds = (p * (dp - jnp.sum(p * dp, axis=1, keepdims=True))).astype(_MXU)
            dq_ref[0, :, sl] = (_dot(ds, kh, 1, 0) * scale).astype(dq_ref.dtype)
            dkv_ref[0, :, sl] += _dot(ds, qh, 0, 0) * scale

    blk = pl.BlockSpec((1, tq, D), lambda b, i: (b, i, 0))
    kvs = pl.BlockSpec((1, M, 2 * D), lambda b, i: (b, 0, 0))
    return pl.pallas_call(
        body, name=name,
        out_shape=(jax.ShapeDtypeStruct((B, T, D), _MXU), jax.ShapeDtypeStruct((B, M, 2 * D), _F32)),
        grid=(B, T // tq), in_specs=[blk, kvs, blk], out_specs=(blk, kvs),
        compiler_params=_params("parallel", "arbitrary"),
    )(q3, kv3, do3)


def _rotated_copies(ext, rot, tt):
    rot[0] = ext[...]
    for b in range(1, 8):
        rot[b, 0:tt + _HALO - 8, :] = ext[b:b + tt + _HALO - 8, :]


def _shifted(rot, off, r0, rows, c0):
    a, b = divmod(off, 8)
    return rot[b, 8 * a + r0:8 * a + r0 + rows, c0:c0 + _LANES]


def _conv_fwd(y3, dw_w, dw_b, ln_g, ln_b, *, name, xchg=None):
    B, T, C = y3.shape
    tt = _rows(T, 256)
    nt = T // tt

    def body(prev_ref, cur_ref, w_ref, b_ref, g_ref, lb_ref, y2_ref, y4_ref, ext, rot):
        i = pl.program_id(1)
        ext[0:_HALO, :] = jnp.where(i > 0, prev_ref[0, tt - _HALO:tt, :], 0.0)
        ext[_HALO:_HALO + tt, :] = cur_ref[0]
        _rotated_copies(ext, rot, tt)
        for c0 in range(0, C, _LANES):
            acc = jnp.broadcast_to(b_ref[:, c0:c0 + _LANES], (tt, _LANES))
            for j in range(_CONV_K):
                acc = acc + w_ref[j:j + 1, c0:c0 + _LANES] * _shifted(rot, _HALO - (_CONV_K - 1) + j, 0, tt, c0)
            y2_ref[0, :, c0:c0 + _LANES] = acc
        xh, _ = _layernorm_stats(y2_ref[0])
        z = xh * g_ref[...] + lb_ref[...]
        y4_ref[0] = (z * _sigmoid(z)).astype(y4_ref.dtype)

    vec = pl.BlockSpec((1, C), lambda b, i: (0, 0))
    blk = pl.BlockSpec((1, tt, C), lambda b, i: (b, i, 0))
    (y2, y4), got = _fused_call(
        body, name=name,
        out_shape=(jax.ShapeDtypeStruct((B, T, C), _F32), jax.ShapeDtypeStruct((B, T, C), _MXU)),
        grid=(B, nt),
        in_specs=[pl.BlockSpec((1, tt, C), lambda b, i: (b, jnp.maximum(i - 1, 0), 0)), blk,
                  pl.BlockSpec((_HALO, C), lambda b, i: (0, 0)), vec, vec, vec],
        out_specs=(blk, blk),
        scratch_shapes=[pltpu.VMEM((tt + _HALO, C), _F32), pltpu.VMEM((8, tt + _HALO, C), _F32)],
        sem=("parallel", "parallel"), args=(y3, y3, dw_w, dw_b.reshape(1, C), ln_g.reshape(1, C), ln_b.reshape(1, C)), xchg=xchg)
    return y2, y4, got


def _conv_ln_bwd(y2, dx, w_out, ln_g, ln_b, *, name):
    N, C = y2.shape
    D = dx.shape[1]
    tr = _rows(N, 512)

    def body(y_ref, dx_ref, w_ref, g_ref, b_ref, dy_ref, dg_ref, db_ref, dwb_ref):
        @pl.when(pl.program_id(0) == 0)
        def _():
            dg_ref[...] = jnp.zeros_like(dg_ref)
            db_ref[...] = jnp.zeros_like(db_ref)
            dwb_ref[...] = jnp.zeros_like(dwb_ref)

        dy4 = _dot(dx_ref[...].astype(_MXU), w_ref[...].astype(_MXU), 1, 1)
        xh, rstd = _layernorm_stats(y_ref[...])
        gv = g_ref[...]
        z = xh * gv + b_ref[...]
        sg = _sigmoid(z)
        dz = dy4 * (sg * (1.0 + z * (1.0 - sg)))
        dg_ref[...] += jnp.sum(dz * xh, axis=0, keepdims=True)
        db_ref[...] += jnp.sum(dz, axis=0, keepdims=True)
        dxh = dz * gv
        dy = rstd * (dxh - jnp.mean(dxh, axis=-1, keepdims=True) - xh * jnp.mean(dxh * xh, axis=-1, keepdims=True))
        dwb_ref[...] += jnp.sum(dy, axis=0, keepdims=True)
        dy_ref[...] = dy

    row = pl.BlockSpec((tr, C), lambda i: (i, 0))
    vec = pl.BlockSpec((1, C), lambda i: (0, 0))
    v = jax.ShapeDtypeStruct((1, C), _F32)
    return pl.pallas_call(
        body, name=name, out_shape=(jax.ShapeDtypeStruct((N, C), _F32), v, v, v), grid=(N // tr,),
        in_specs=[row, pl.BlockSpec((tr, D), lambda i: (i, 0)), pl.BlockSpec((None, C, D), lambda i: (0, 0, 0)), vec, vec],
        out_specs=(row, vec, vec, vec), compiler_params=_params("arbitrary"),
    )(y2, dx, w_out, ln_g.reshape(1, C), ln_b.reshape(1, C))


def _conv_bwd(y3, dy23, ag3, dw_w, *, name, xchg=None):
    B, T, C = y3.shape
    tt = _rows(T, 256)
    nt = T // tt

    rs = _rows(tt, 128)

    def groups(v):
        return jnp.sum(v.reshape(rs // 8, 8, _LANES), axis=0)

    def body(yp_ref, yc_ref, dc_ref, dn_ref, a_ref, g_ref, w_ref, dag_ref, dw_ref, dbin_ref, yext, dext, yrot, drot, dw_acc, db_acc):
        b = pl.program_id(0)
        i = pl.program_id(1)

        @pl.when((b == 0) & (i == 0))
        def _():
            dw_acc[...] = jnp.zeros_like(dw_acc)
            db_acc[...] = jnp.zeros_like(db_acc)

        yext[0:_HALO, :] = jnp.where(i > 0, yp_ref[0, tt - _HALO:tt, :], 0.0)
        yext[_HALO:_HALO + tt, :] = yc_ref[0]
        dext[0:tt, :] = dc_ref[0]
        dext[tt:tt + _HALO, :] = jnp.where(i < nt - 1, dn_ref[0, 0:_HALO, :], 0.0)
        _rotated_copies(yext, yrot, tt)
        _rotated_copies(dext, drot, tt)
        for c0 in range(0, C, _LANES):
            for r0 in range(0, tt, rs):
                d_cur = dext[r0:r0 + rs, c0:c0 + _LANES]
                dy = jnp.zeros((rs, _LANES), _F32)
                for j in range(_CONV_K):
                    sh = _CONV_K - 1 - j
                    dy = dy + w_ref[j:j + 1, c0:c0 + _LANES] * _shifted(drot, sh, r0, rs, c0)
                    dw_acc[j, :, c0:c0 + _LANES] += groups(d_cur * _shifted(yrot, _HALO - sh, r0, rs, c0))
                a, g = a_ref[0, r0:r0 + rs, c0:c0 + _LANES], g_ref[0, r0:r0 + rs, c0:c0 + _LANES]
                sg = _sigmoid(g)
                da = dy * sg
                dg = dy * a * (sg * (1.0 - sg))
                dag_ref[0, r0:r0 + rs, c0:c0 + _LANES] = da.astype(dag_ref.dtype)
                dag_ref[0, r0:r0 + rs, C + c0:C + c0 + _LANES] = dg.astype(dag_ref.dtype)
                db_acc[:, c0:c0 + _LANES] += groups(da)
                db_acc[:, C + c0:C + c0 + _LANES] += groups(dg)

        @pl.when((b == B - 1) & (i == nt - 1))
        def _():
            dw_ref[...] = jnp.sum(dw_acc[...], axis=1)
            dbin_ref[...] = jnp.sum(db_acc[...], axis=0, keepdims=True)

    blk = pl.BlockSpec((1, tt, C), lambda b, i: (b, i, 0))
    (dag, ddw, dbin), got = _fused_call(
        body, name=name,
        out_shape=(jax.ShapeDtypeStruct((B, T, 2 * C), _MXU), jax.ShapeDtypeStruct((_HALO, C), _F32),
                   jax.ShapeDtypeStruct((1, 2 * C), _F32)),
        grid=(B, nt),
        in_specs=[pl.BlockSpec((1, tt, C), lambda b, i: (b, jnp.maximum(i - 1, 0), 0)), blk, blk,
                  pl.BlockSpec((1, tt, C), lambda b, i: (b, jnp.minimum(i + 1, nt - 1), 0)),
                  pl.BlockSpec((None, 1, tt, C), lambda b, i: (0, b, i, 0)), pl.BlockSpec((None, 1, tt, C), lambda b, i: (1, b, i, 0)),
                  pl.BlockSpec((_HALO, C), lambda b, i: (0, 0))],
        out_specs=(pl.BlockSpec((1, tt, 2 * C), lambda b, i: (b, i, 0)), pl.BlockSpec((_HALO, C), lambda b, i: (0, 0)),
                   pl.BlockSpec((1, 2 * C), lambda b, i: (0, 0))),
        scratch_shapes=[pltpu.VMEM((tt + _HALO, C), _F32), pltpu.VMEM((tt + _HALO, C), _F32),
                        pltpu.VMEM((8, tt + _HALO, C), _F32), pltpu.VMEM((8, tt + _HALO, C), _F32),
                        pltpu.VMEM((_HALO, 8, C), _F32), pltpu.VMEM((8, 2 * C), _F32)],
        sem=("arbitrary", "arbitrary"), args=(y3, y3, dy23, dy23, ag3, ag3, dw_w), xchg=xchg)
    return dag, ddw, dbin, got


class _Exchange:
    def __init__(self, items):
        self.per_peer = [pp for _, pp in items]
        self.srcs, self.out_shapes, self.pieces = [], [], []
        for t, (srcs, per_peer) in enumerate(items):
            blk = srcs[0].shape[1:] if per_peer else srcs[0].shape
            self.out_shapes.append(jax.ShapeDtypeStruct((len(srcs), _N_DEV) + tuple(blk), srcs[0].dtype))
            for l, s in enumerate(srcs):
                self.pieces.append((t, l, len(self.srcs)))
                self.srcs.append(s)
        self.n_src, self.n_dst, n_pc = len(self.srcs), len(items), len(self.pieces)
        self.in_specs = [pl.BlockSpec(memory_space=pl.ANY)] * self.n_src
        self.out_specs = [pl.BlockSpec(memory_space=pl.ANY)] * self.n_dst
        self.scratch = [pltpu.SemaphoreType.DMA((n_pc, _N_DEV - 1)), pltpu.SemaphoreType.DMA((n_pc, _N_DEV - 1)),
                        pltpu.SemaphoreType.DMA((n_pc,))]

    def _copies(self, src_refs, dst_refs, sems, kind):
        send_sems, recv_sems, loc_sems = sems
        x, y, c = lax.axis_index("x"), lax.axis_index("y"), lax.axis_index("c")
        me = 4 * x + 2 * y + c
        out = []
        for i, (t, l, s) in enumerate(self.pieces):
            def src_for(p, s=s, t=t):
                return src_refs[s].at[p] if self.per_peer[t] else src_refs[s]

            if kind == "local":
                out.append(pltpu.make_async_copy(src_for(me), dst_refs[t].at[l, me], loc_sems.at[i]))
                continue
            for k in range(1, _N_DEV):
                px, py, pc = (1 - x if k & 4 else x), (1 - y if k & 2 else y), (1 - c if k & 1 else c)
                p = 4 * px + 2 * py + pc
                out.append(pltpu.make_async_remote_copy(
                    src_ref=src_for(p), dst_ref=dst_refs[t].at[l, p if kind == "recv" else me],
                    send_sem=send_sems.at[i, k - 1], recv_sem=recv_sems.at[i, k - 1],
                    device_id=(px, py, pc), device_id_type=pl.DeviceIdType.MESH))
        return out

    def start(self, src_refs, dst_refs, sems):
        for cp in self._copies(src_refs, dst_refs, sems, "local") + self._copies(src_refs, dst_refs, sems, "send"):
            cp.start()

    def finish(self, src_refs, dst_refs, sems):
        for cp in self._copies(src_refs, dst_refs, sems, "send"):
            cp.wait_send()
        for cp in self._copies(src_refs, dst_refs, sems, "recv"):
            cp.wait_recv()
        for cp in self._copies(src_refs, dst_refs, sems, "local"):
            cp.wait()


def _exchange(items, *, name):
    ex = _Exchange(items)

    def body(*refs):
        parts = refs[:ex.n_src], refs[ex.n_src:ex.n_src + ex.n_dst], refs[ex.n_src + ex.n_dst:]
        ex.start(*parts)
        ex.finish(*parts)

    return pl.pallas_call(
        body, name=name, out_shape=ex.out_shapes, in_specs=ex.in_specs, out_specs=ex.out_specs, scratch_shapes=ex.scratch,
        compiler_params=pltpu.CompilerParams(has_side_effects=True),
    )(*ex.srcs)


def _adam_update(g, w, m, v):
    c1 = 1.0 / (1.0 - _ADAM_B1 ** _ADAM_STEP)
    c2 = 1.0 / (1.0 - _ADAM_B2 ** _ADAM_STEP)
    m2 = _ADAM_B1 * m + (1.0 - _ADAM_B1) * g
    v2 = _ADAM_B2 * v + (1.0 - _ADAM_B2) * (g * g)
    return -_ADAM_LR * ((m2 * c1) / (jnp.sqrt(v2 * c2) + _ADAM_EPS) + _ADAM_WD * w), m2, v2


def _adamw_big(recvs, w, m, v, *, name):
    L, R, C = w.shape
    tr = _rows(R, 256)
    nb = R // tr

    def body(*refs):
        r_refs = refs[:L]
        w_ref, m_ref, v_ref, g_ref, d_ref, mo_ref, vo_ref = refs[L:]
        for l in range(L):
            @pl.when(pl.program_id(0) == l)
            def _(r_ref=r_refs[l]):
                g = r_ref[0, 0].astype(_F32)
                for k in range(1, _N_DEV):
                    g = g + r_ref[0, k].astype(_F32)
                g_ref[0] = g
                d_ref[0], mo_ref[0], vo_ref[0] = _adam_update(g, w_ref[0], m_ref[0], v_ref[0])

    def recv_spec(l):
        return pl.BlockSpec((1, _N_DEV, tr, C), lambda ll, i: (0, 0, jnp.where(ll == l, i, jnp.where(ll < l, 0, nb - 1)), 0))

    blk = pl.BlockSpec((1, tr, C), lambda l, i: (l, i, 0))
    o = jax.ShapeDtypeStruct((L, R, C), _F32)
    return pl.pallas_call(
        body, name=name, out_shape=(o, o, o, o), grid=(L, nb),
        in_specs=[recv_spec(l) for l in range(L)] + [blk, blk, blk], out_specs=(blk, blk, blk, blk),
        compiler_params=_params("arbitrary", "arbitrary"),
    )(*recvs, w, m, v)


def _adamw_small(tensors, *, name):
    n = len(tensors)
    lanes = [t[4] for t in tensors]
    layers = [len(t[0]) for t in tensors]

    def body(*refs):
        pos = 0
        ins = []
        for t in range(n):
            ins.append((refs[pos:pos + layers[t]], *refs[pos + layers[t]:pos + layers[t] + 3]))
            pos += layers[t] + 3
        outs = refs[pos:]
        for t in range(n):
            r_refs, w_ref, m_ref, v_ref = ins[t]
            g_ref, d_ref, mo_ref, vo_ref = outs[4 * t:4 * t + 4]
            for l in range(layers[t]):
                g = r_refs[l][0, 0]
                for k in range(1, _N_DEV):
                    g = g + r_refs[l][0, k]
                if lanes[t] is not None:
                    g = g[..., :lanes[t]]
                g_ref[l] = g
                d_ref[l], mo_ref[l], vo_ref[l] = _adam_update(g, w_ref[l], m_ref[l], v_ref[l])

    args, out_shape = [], []
    for recvs, w, m, v, _ in tensors:
        args += [*recvs, w, m, v]
        out_shape += [jax.ShapeDtypeStruct(w.shape, _F32)] * 4
    outs = pl.pallas_call(
        body, name=name, out_shape=out_shape,
        in_specs=[pl.BlockSpec(memory_space=pltpu.VMEM)] * len(args), out_specs=[pl.BlockSpec(memory_space=pltpu.VMEM)] * len(out_shape),
        compiler_params=_params(),
    )(*args)
    return [tuple(outs[4 * t:4 * t + 4]) for t in range(n)]


_BIG = (("w_in_e", 2), ("w_out_e", 1), ("conv_w_in", 2), ("conv_w_out", 1), ("xa_wq", 1), ("xa_wkv", 2), ("xa_wo", 1),
        ("ffn_w_gu", 2), ("ffn_w_down", 1))
_SMALL_SHARDED = (("mix_norm_o", 1), ("conv_b_in", 1), ("conv_dw_w", 2), ("conv_dw_b", 1), ("conv_ln_g", 1),
                  ("conv_ln_b", 1), ("conv_b_out", 1))
_REPLICATED = ("mix_norm_e", "fox_f_bias", "gmlp_ln_g", "gmlp_ln_b", "gmlp_w_s", "gmlp_b_s", "xa_norm", "mem_norm",
               "ffn_norm", "final_norm")
_WEIGHTS = ("mix_norm_e", "w_in_e", "fox_f_bias", "gmlp_ln_g", "gmlp_ln_b", "gmlp_w_s", "gmlp_b_s", "w_out_e", "mix_norm_o",
            "conv_w_in", "conv_b_in", "conv_dw_w", "conv_dw_b", "conv_ln_g", "conv_ln_b", "conv_w_out", "conv_b_out",
            "xa_norm", "mem_norm", "xa_wq", "xa_wkv", "xa_wo", "ffn_norm", "ffn_w_gu", "ffn_w_down", "final_norm")


def _cols_to_peers(g, n=_N_DEV):
    K, N = g.shape[-2:]
    return jnp.swapaxes(g.reshape(g.shape[:-1] + (n, N // n)), -3, -2)


def _weight_items(pieces, wsrc):
    return [([wsrc[n][l]] if n in dict(_BIG) else [wsrc[n]], False) for n, l in pieces]


def _place_weights(P, pieces, gathered):
    axis = dict(_BIG + _SMALL_SHARDED)
    for (n, l), g in zip(pieces, gathered):
        if n in dict(_BIG):
            P.setdefault(n, {})[l] = g.reshape(1, -1, g.shape[-1]) if axis[n] == 1 else _peers_to_cols(g)
        else:
            P[n] = _peers_to_cols(g[0, :, 0])[None] if axis[n] == 2 else g.reshape(1, -1)


def _grad_items(pieces, G):
    axis = dict(_BIG + _SMALL_SHARDED)
    items = []
    for n, l in pieces:
        g = G[n][l]
        if n in _REPLICATED:
            items.append(([g], False))
        elif n == "ffn_w_gu":
            half = _N_DEV // 2
            items.append(([jnp.concatenate([_cols_to_peers(g[0], half), _cols_to_peers(g[1], half)], axis=0)], True))
        elif n in dict(_BIG):
            items.append(([g.reshape(_N_DEV, -1, g.shape[-1]) if axis[n] == 1 else _cols_to_peers(g)], True))
        else:
            items.append(([_cols_to_peers(g) if axis[n] == 2 else g.reshape(_N_DEV, 1, -1)], True))
    return items


def _peers_to_cols(d):
    K, c = d.shape[-2:]
    return jnp.swapaxes(d, -3, -2).reshape(d.shape[:-3] + (K, _N_DEV * c))


def _local_step(x, mem, tgt, P, wsrc=None, fwd_hooks=None, bwd_hooks=None):
    fwd_hooks, bwd_hooks = fwd_hooks or {}, bwd_hooks or {}
    sent = {}

    def gather(kernel_name):
        return _Exchange(_weight_items(fwd_hooks[kernel_name], wsrc)) if kernel_name in fwd_hooks else None

    def placed(kernel_name, got):
        if kernel_name in fwd_hooks:
            _place_weights(P, fwd_hooks[kernel_name], got)

    def scatter(kernel_name):
        return _Exchange(_grad_items(bwd_hooks[kernel_name], G)) if kernel_name in bwd_hooks else None

    def received(kernel_name, got):
        if kernel_name in bwd_hooks:
            sent.update(zip(bwd_hooks[kernel_name], got))

    def mm(a, b, *, name, **kw):
        ex = gather(name) or scatter(name)
        out = _mm(a, b, name=name, xchg=ex, **kw)
        if ex is None:
            return out
        out, got = out
        placed(name, got)
        received(name, got)
        return out

    B, T, D = x.shape
    M = mem.shape[1]
    N = B * T
    W = D // 2
    H = W // _FOX_HD
    f_blk = 5 * W // _LANES
    G = {}
    x0 = x.reshape(N, D)
    memf = mem.reshape(B * M, D)

    h_e = _rms_fwd(x0, P["mix_norm_e"][0], name="rms_mix_e", xchg=gather("rms_mix_e"))
    if "rms_mix_e" in fwd_hooks:
        h_e, got = h_e
        placed("rms_mix_e", got)
    w_in_pad = _pad_w_in(P["w_in_e"][0][0], W, H)[None]
    proj = mm(h_e, w_in_pad, bl=0, name="mm_in_e", tn=896)
    proj3 = proj.reshape(B, T, -1)
    cum = _fox_cum(proj3, f_blk, P["fox_f_bias"][0], name="fox_cum")
    o_fox, lse, got = _fox_fwd(proj3, cum, name="fox_fwd", xchg=gather("fox_fwd"))
    placed("fox_fwd", got)
    bias_full = jnp.repeat(P["gmlp_b_s"][0].T, _GRP, axis=1)
    a_out = _gmlp_fwd(proj, P["gmlp_ln_g"][0], P["gmlp_ln_b"][0], P["gmlp_w_s"][0], bias_full, name="gmlp_fwd")
    mixcat = jnp.concatenate([o_fox.reshape(N, W), a_out], axis=1)
    x1 = mm(mixcat, P["w_out_e"][0], bl=0, res=x0, name="mm_out_e")

    def xa_ffn_fwd(xin, l):
        s = {}
        s["q"], s["h_xa"] = mm(xin, P["xa_wq"][l], bl=0, rms_fwd=P["xa_norm"][l], out_dtype=_MXU, name=f"mm_q{l}")
        s["mn"] = _rms_fwd(memf, P["mem_norm"][l], name=f"rms_mem{l}")
        s["kv"] = mm(s["mn"], P["xa_wkv"][l], bl=0, out_dtype=_MXU, name=f"mm_kv{l}")
        s["o"] = _xa_fwd(s["q"].reshape(B, T, D), s["kv"].reshape(B, M, 2 * D), name=f"xa_fwd{l}").reshape(N, D)
        s["x_mid"] = mm(s["o"], P["xa_wo"][l], bl=0, res=xin, name=f"mm_o{l}")
        s["gu"], s["act"], s["h_ffn"], got = _mm_gu(s["x_mid"], P["ffn_w_gu"][l], 0, rms_fwd=P["ffn_norm"][l], name=f"mm_gu{l}",
                                                    xchg=gather(f"mm_gu{l}"))
        placed(f"mm_gu{l}", got)
        s["x_in"] = xin
        xout = mm(s["act"], P["ffn_w_down"][l], bl=0, res=s["x_mid"], name=f"mm_down{l}", tn=512)
        return xout, s

    x3, s0 = xa_ffn_fwd(x1, 0)
    ag, y, h_o, _ = _mm_gu(x3, P["conv_w_in"][0], 0, bias=P["conv_b_in"][0], glu=True, keep=_F32, rms_fwd=P["mix_norm_o"][0],
                           name="mm_conv_in")
    C = y.shape[1]
    dw_w = jnp.pad(P["conv_dw_w"][0], ((0, _HALO - _CONV_K), (0, 0)))
    y2, y4, got = _conv_fwd(y.reshape(B, T, C), dw_w, P["conv_dw_b"][0], P["conv_ln_g"][0], P["conv_ln_b"][0], name="conv_fwd",
                            xchg=gather("conv_fwd"))
    placed("conv_fwd", got)
    x4 = mm(y4.reshape(N, C), P["conv_w_out"][0], bl=0, bias=P["conv_b_out"][0], res=x3, name="mm_conv_out")
    x6, s1 = xa_ffn_fwd(x4, 1)
    loss, dx, dg = _final_loss(x6, P["final_norm"], tgt.reshape(N, D), name="final_loss")
    G["final_norm"] = [dg]

    def xa_ffn_bwd(dx, s, l):
        for k in ("ffn_w_down", "ffn_w_gu", "ffn_norm", "xa_wo", "xa_wq", "xa_norm", "xa_wkv", "mem_norm"):
            G.setdefault(k, {})
        dgu = _mm_dgu(dx, P["ffn_w_down"][l], 0, s["gu"], name=f"mm_dgu{l}")
        G["ffn_w_down"][l] = mm(s["act"], dx, ta=True, out_dtype=_MXU, name=f"mm_dwdown{l}", tm=1408)
        G["ffn_w_gu"][l] = (mm(s["h_ffn"], dgu, ta=True, bl=0, out_dtype=_MXU, name=f"mm_dwg{l}", tn=1408),
                            mm(s["h_ffn"], dgu, ta=True, bl=1, out_dtype=_MXU, name=f"mm_dwu{l}", tn=1408))
        dx, G["ffn_norm"][l] = mm(dgu, P["ffn_w_gu"][l], al="cat", bl=0, tb=True, rms_bwd=(s["x_mid"], P["ffn_norm"][l], dx),
                                  name=f"mm_dhffn{l}", tm=512, tn=D, tk=1408)
        do = mm(dx, P["xa_wo"][l], bl=0, tb=True, out_dtype=_MXU, name=f"mm_do{l}")
        G["xa_wo"][l] = mm(s["o"], dx, ta=True, out_dtype=_MXU, name=f"mm_dwo{l}")
        dq, dkv = _xa_bwd(s["q"].reshape(B, T, D), s["kv"].reshape(B, M, 2 * D), do.reshape(B, T, D), name=f"xa_bwd{l}")
        dq, dkv = dq.reshape(N, D), dkv.reshape(B * M, 2 * D)
        G["xa_wq"][l] = mm(s["h_xa"], dq, ta=True, out_dtype=_MXU, name=f"mm_dwq{l}")
        dx, G["xa_norm"][l] = mm(dq, P["xa_wq"][l], bl=0, tb=True, rms_bwd=(s["x_in"], P["xa_norm"][l], dx), name=f"mm_dhxa{l}",
                                 tm=512, tn=D)
        G["xa_wkv"][l] = mm(s["mn"], dkv, ta=True, out_dtype=_MXU, name=f"mm_dwkv{l}")
        dmn = mm(dkv, P["xa_wkv"][l], bl=0, tb=True, name=f"mm_dmn{l}")
        G["mem_norm"][l] = _rms_bwd(memf, P["mem_norm"][l], dmn, None, name=f"rms_mem_bwd{l}")
        return dx

    dx = xa_ffn_bwd(dx, s1, 1)
    G["conv_b_out"] = [_colsum(dx, name="colsum_b_out")]
    G["conv_w_out"] = [mm(y4.reshape(N, C), dx, ta=True, out_dtype=_MXU, name="mm_dwconv_out")]
    dy2, dlg, dlb, ddb = _conv_ln_bwd(y2.reshape(N, C), dx, P["conv_w_out"][0], P["conv_ln_g"][0], P["conv_ln_b"][0],
                                      name="conv_ln_bwd")
    G["conv_ln_g"], G["conv_ln_b"], G["conv_dw_b"] = [dlg], [dlb], [ddb]
    dag, ddw, dbin, got = _conv_bwd(y.reshape(B, T, C), dy2.reshape(B, T, C), ag.reshape(2, B, T, C), dw_w, name="conv_bwd",
                                    xchg=scatter("conv_bwd"))
    received("conv_bwd", got)
    G["conv_dw_w"], G["conv_b_in"] = [ddw[:_CONV_K]], [dbin]
    dag = dag.reshape(N, 2 * C)
    G["conv_w_in"] = [mm(h_o, dag, ta=True, out_dtype=_MXU, name="mm_dwconv_in")]
    dx, dg = mm(dag, P["conv_w_in"][0], bl=0, tb=True, rms_bwd=(x3, P["mix_norm_o"][0], dx), name="mm_dh_o", tm=512, tn=D)
    G["mix_norm_o"] = [dg]
    dx = xa_ffn_bwd(dx, s0, 0)
    G["w_out_e"] = [mm(mixcat, dx, ta=True, out_dtype=_MXU, name="mm_dwout_e")]
    dmix = mm(dx, P["w_out_e"][0], bl=0, tb=True, name="mm_dmix")
    dz, dlg, dlb, dws, dbias = _gmlp_bwd(proj, dmix, 1, P["gmlp_ln_g"][0], P["gmlp_ln_b"][0], P["gmlp_w_s"][0], bias_full,
                                         name="gmlp_bwd")
    G["gmlp_ln_g"], G["gmlp_ln_b"], G["gmlp_w_s"] = [dlg], [dlb], [dws]
    G["gmlp_b_s"] = [dbias[:, :2 * (W // _LANES)].T]
    dmix3 = dmix.reshape(B, T, D)
    dq, dk, dv, dcum, got = _fox_bwd(proj3, cum, dmix3, lse, name="fox_bwd", xchg=scatter("fox_bwd"))
    received("fox_bwd", got)
    df, dfb = _fox_cum_bwd(proj3, f_blk, P["fox_f_bias"][0], dcum.reshape(B, H, T), name="fox_cum_bwd")
    G["fox_f_bias"] = [dfb]
    dproj = jnp.concatenate([dq.reshape(N, W), dk.reshape(N, W), dv.reshape(N, W), dz, df.reshape(N, _LANES).astype(_MXU)], axis=1)
    G["w_in_e"] = [_unpad_w_in(mm(h_e, dproj, ta=True, out_dtype=_MXU, name="mm_dwin_e", tn=896), W, H)]
    dx, dg = mm(dproj, w_in_pad, bl=0, tb=True, rms_bwd=(x0, P["mix_norm_e"][0], dx), name="mm_dh_e", tm=512, tn=D)
    G["mix_norm_e"] = [dg]
    return loss, dx.reshape(B, T, D), G, sent


def _pad_w_in(w_in, W, H):
    f = w_in[:, 3 * W:3 * W + H]
    return jnp.concatenate([w_in[:, :3 * W], w_in[:, 3 * W + H:], jnp.pad(f, ((0, 0), (0, _LANES - H)))], axis=1)


def _unpad_w_in(g, W, H):
    return jnp.concatenate([g[:, :3 * W], g[:, 5 * W:5 * W + H], g[:, 3 * W:5 * W]], axis=1)


def kernel(x, mem, mix_norm_e, w_in_e, fox_f_bias, gmlp_ln_g, gmlp_ln_b, gmlp_w_s, gmlp_b_s, w_out_e, mix_norm_o, conv_w_in, conv_b_in, conv_dw_w, conv_dw_b, conv_ln_g, conv_ln_b, conv_w_out, conv_b_out, xa_norm, mem_norm, xa_wq, xa_wkv, xa_wo, ffn_norm, ffn_w_gu, ffn_w_down, final_norm, loss_target, m_mix_norm_e, m_w_in_e, m_fox_f_bias, m_gmlp_ln_g, m_gmlp_ln_b, m_gmlp_w_s, m_gmlp_b_s, m_w_out_e, m_mix_norm_o, m_conv_w_in, m_conv_b_in, m_conv_dw_w, m_conv_dw_b, m_conv_ln_g, m_conv_ln_b, m_conv_w_out, m_conv_b_out, m_xa_norm, m_mem_norm, m_xa_wq, m_xa_wkv, m_xa_wo, m_ffn_norm, m_ffn_w_gu, m_ffn_w_down, m_final_norm, v_mix_norm_e, v_w_in_e, v_fox_f_bias, v_gmlp_ln_g, v_gmlp_ln_b, v_gmlp_w_s, v_gmlp_b_s, v_w_out_e, v_mix_norm_o, v_conv_w_in, v_conv_b_in, v_conv_dw_w, v_conv_dw_b, v_conv_ln_g, v_conv_ln_b, v_conv_w_out, v_conv_b_out, v_xa_norm, v_mem_norm, v_xa_wq, v_xa_wkv, v_xa_wo, v_ffn_norm, v_ffn_w_gu, v_ffn_w_down, v_final_norm):
    env = dict(locals())
    w = {n: env[n] for n in _WEIGHTS}
    mom = {n: env["m_" + n] for n in _WEIGHTS}
    var = {n: env["v_" + n] for n in _WEIGHTS}
    D = x.shape[-1]
    W = D // 2
    H = W // _FOX_HD

    def layers(n):
        return w[n].shape[0] if w[n].ndim > 1 else 1

    wsrc = {n: (w[n].astype(_MXU) if n in dict(_BIG) else w[n]) for n, _ in _BIG + _SMALL_SHARDED}
    P = {n: w[n] for n in _REPLICATED}
    fwd_hooks = {
        "rms_mix_e": [("w_in_e", 0)],
        "mm_in_e": [("w_out_e", 0), ("xa_wq", 0)],
        "fox_fwd": [("xa_wkv", 0), ("xa_wo", 0), ("ffn_w_gu", 0)],
        "mm_o0": [("xa_wq", 1)],
        "mm_gu0": [("ffn_w_down", 0), ("conv_w_in", 0), ("conv_w_out", 0)] + [(n, 0) for n, _ in _SMALL_SHARDED],
        "mm_down0": [("xa_wkv", 1)],
        "conv_fwd": [("xa_wo", 1), ("ffn_w_gu", 1)],
        "mm_gu1": [("ffn_w_down", 1)],
    }

    last = [("mix_norm_e", 0)]
    in_dh_e = [("w_in_e", 0), ("fox_f_bias", 0)]
    in_conv = [(n, 1) for n in ("ffn_w_gu", "ffn_w_down", "xa_wq", "xa_wkv", "xa_wo", "xa_norm", "mem_norm", "ffn_norm")]
    in_conv += [("final_norm", 0), ("conv_w_out", 0)]
    every = [(n, l) for n in [n for n, _ in _BIG + _SMALL_SHARDED] + list(_REPLICATED) for l in range(layers(n))]
    in_dhffn0 = [("ffn_w_gu", 0)]
    bwd_hooks = {"conv_bwd": in_conv, "mm_dhffn0": in_dhffn0, "mm_dh_e": in_dh_e,
                 "fox_bwd": [pc for pc in every if pc not in last + in_dh_e + in_conv + in_dhffn0]}
    loss, grad_x, G, recv = _local_step(x, mem, loss_target, P, wsrc, fwd_hooks, bwd_hooks)
    loss = lax.psum(loss[0, 0], ("x", "y", "c"))
    recv.update(zip(last, _exchange(_grad_items(last, G), name="scatter_last")))

    def partials(n):
        return [recv[(n, l)] for l in range(layers(n))]

    res = {n: _adamw_big(partials(n), w[n], mom[n], var[n], name="adamw_" + n) for n, _ in _BIG}
    small = [n for n, _ in _SMALL_SHARDED] + list(_REPLICATED)

    def rows(a, n):
        r = recv[(n, 0)]
        return a.reshape((layers(n),) + r.shape[2:-1] + (-1,))

    outs = _adamw_small([(partials(n), rows(w[n], n), rows(mom[n], n), rows(var[n], n),
                          w[n].shape[-1] if w[n].shape[-1] != recv[(n, 0)].shape[-1] else None) for n in small], name="adamw_small")
    for n, o in zip(small, outs):
        res[n] = tuple(a.reshape(w[n].shape) for a in o)
    return (loss, grad_x, *[res[n][0] for n in _WEIGHTS], *[res[n][1] for n in _WEIGHTS],
            *[res[n][2] for n in _WEIGHTS], *[res[n][3] for n in _WEIGHTS])
```

```python
import functools
import math

import jax
import jax.numpy as jnp
from jax import lax
from jax.experimental import pallas as pl
from jax.experimental.pallas import tpu as pltpu

_F32 = jnp.float32
_MXU = jnp.bfloat16
_VMEM_LIMIT = 48 * 1024 * 1024
_LANES = 128
_EPS = 1e-6
_N_DEV = 8
_FOX_HD = 64
_FOX_SCALE = _FOX_HD ** -0.5
_FOX_TQ = 512
_CHUNK = 128
_GRP = 64
_CONV_K = 31
_HALO = 32
_CONV_ROWS = 128
_XA_HEADS = 4
_GELU_C = math.sqrt(2.0 / math.pi)
_ADAM_LR, _ADAM_B1, _ADAM_B2, _ADAM_EPS, _ADAM_WD, _ADAM_STEP = 0.001, 0.9, 0.999, 1e-08, 0.01, 10
_FLAT_W = 1024
_FLAT_ALIGN = 16 * _FLAT_W
_BIG_ROWS = 128


def _params(*sem):
    return pltpu.CompilerParams(dimension_semantics=sem if sem else None, vmem_limit_bytes=_VMEM_LIMIT)


def _pick(n, pref):
    if n <= pref:
        return n
    best = None
    for t in range(_LANES, pref + 1, _LANES):
        if n % t == 0:
            best = t
    assert best is not None, (n, pref)
    return best


def _rows(n, pref):
    if n <= pref:
        return n
    t = pref
    while n % t:
        t //= 2
    assert t >= 8, (n, pref)
    return t


def _sigmoid(x):
    return 1.0 / (1.0 + jnp.exp(-x))


def _gelu(x):
    t = jnp.tanh(_GELU_C * (x + 0.044715 * (x * x * x)))
    return 0.5 * x * (1.0 + t)


def _gelu_grad(x):
    x2 = x * x
    t = jnp.tanh(_GELU_C * (x + 0.044715 * (x2 * x)))
    return 0.5 * (1.0 + t) + 0.5 * x * (1.0 - t * t) * (_GELU_C * (1.0 + 3.0 * 0.044715 * x2))


def _dot(a, b, ca, cb):
    return lax.dot_general(a, b, (((ca,), (cb,)), ((), ())), preferred_element_type=_F32)


def _rms_rows(xv, gain):
    return (xv * lax.rsqrt(jnp.mean(xv * xv, axis=-1, keepdims=True) + _EPS) * gain).astype(_MXU)


def _mm(a, b, *, name, ta=False, tb=False, al=None, bl=None, bk0=0, bias=None, res=None, rms_bwd=None, rms_fwd=None,
        out_dtype=_F32, tm=1024, tn=512, tk=1024, xchg=None):
    if ta:
        K, M = a.shape[-2:]
    else:
        M, K = a.shape[-2:]
    if al == "cat":
        assert not ta
        K = a.shape[0] * a.shape[-1]
    if tb:
        N, K2 = b.shape[-2:]
    else:
        K2, N = b.shape[-2:]
    assert K == K2 or (tb and K2 > K), (a.shape, b.shape, ta, tb)
    tm, tn = _pick(M, tm), _pick(N, tn)
    tk = K if (not ta and K <= 2816 and K2 == K) else _pick(a.shape[-1] if al == "cat" else K, tk)
    nk = K // tk
    assert bk0 % tk == 0
    kb = bk0 // tk
    grid = (M // tm, N // tn, nk)
    if al == "cat":
        per = a.shape[-1] // tk
        a_spec = pl.BlockSpec((None, tm, tk), lambda i, j, k: (k // per, i, k % per))
    elif a.ndim == 3:
        a_spec = (pl.BlockSpec((None, tk, tm), lambda i, j, k: (al, k, i)) if ta
                  else pl.BlockSpec((None, tm, tk), lambda i, j, k: (al, i, k)))
    else:
        a_spec = pl.BlockSpec((tk, tm), lambda i, j, k: (k, i)) if ta else pl.BlockSpec((tm, tk), lambda i, j, k: (i, k))
    if b.ndim == 3:
        b_spec = (pl.BlockSpec((None, tn, tk), lambda i, j, k: (bl, j, k + kb)) if tb
                  else pl.BlockSpec((None, tk, tn), lambda i, j, k: (bl, k, j)))
    else:
        b_spec = pl.BlockSpec((tn, tk), lambda i, j, k: (j, k)) if tb else pl.BlockSpec((tk, tn), lambda i, j, k: (k, j))
    in_specs, args = [a_spec, b_spec], [a, b]
    if bias is not None:
        in_specs.append(pl.BlockSpec((1, tn), lambda i, j, k: (0, j)))
        args.append(bias.reshape(1, N).astype(_F32))
    if res is not None:
        in_specs.append(pl.BlockSpec((tm, tn), lambda i, j, k: (i, j)))
        args.append(res)
    has_bias, has_res, has_rms = bias is not None, res is not None, rms_bwd is not None
    if has_rms:
        assert tn == N, (tn, N)
        x, g, dres = rms_bwd
        in_specs += [pl.BlockSpec((tm, N), lambda i, j, k: (i, 0)), pl.BlockSpec((1, N), lambda i, j, k: (0, 0)),
                     pl.BlockSpec((tm, N), lambda i, j, k: (i, 0))]
        args += [x, g.reshape(1, N), dres]
    has_norm = rms_fwd is not None
    if has_norm:
        assert not ta and nk == 1 and a.ndim == 2
        in_specs.append(pl.BlockSpec((1, K), lambda i, j, k: (0, 0)))
        args.append(rms_fwd.reshape(1, K))

    def body(*refs):
        a_ref, b_ref = refs[0], refs[1]
        pos = 2
        bias_ref = res_ref = None
        if has_bias:
            bias_ref = refs[pos]
            pos += 1
        if has_res:
            res_ref = refs[pos]
            pos += 1
        if has_rms:
            x_ref, g_ref, dres_ref = refs[pos:pos + 3]
            pos += 3
        if has_norm:
            gain_ref = refs[pos]
            pos += 1
        o_ref = refs[pos]
        pos += 1
        if has_rms:
            dg_ref = refs[pos]
            pos += 1
        if has_norm:
            h_ref = refs[pos]
            pos += 1
        acc_ref = refs[pos] if nk > 1 else None
        first_rows = pl.program_id(0) == 0
        if has_norm:
            av = _rms_rows(a_ref[...], gain_ref[...])
            h_ref[...] = av
        else:
            av = a_ref[...].astype(_MXU)
        p = _dot(av, b_ref[...].astype(_MXU), 0 if ta else 1, 1 if tb else 0)

        def finish(acc):
            if has_bias:
                acc = acc + bias_ref[...]
            if has_res:
                acc = acc + res_ref[...]
            if has_rms:
                @pl.when(first_rows)
                def _():
                    dg_ref[...] = jnp.zeros_like(dg_ref)

                xv = x_ref[...]
                r = lax.rsqrt(jnp.mean(xv * xv, axis=-1, keepdims=True) + _EPS)
                xh = xv * r
                dg_ref[...] += jnp.sum(acc * xh, axis=0, keepdims=True)
                dxn = acc * g_ref[...]
                acc = dres_ref[...] + r * (dxn - xh * jnp.mean(dxn * xh, axis=-1, keepdims=True))
            o_ref[...] = acc.astype(o_ref.dtype)

        if nk == 1:
            finish(p)
        else:
            k = pl.program_id(2)

            @pl.when(k == 0)
            def _():
                acc_ref[...] = p

            @pl.when(k > 0)
            def _():
                acc_ref[...] += p

            @pl.when(k == nk - 1)
            def _():
                finish(acc_ref[...])

    out_shape = [jax.ShapeDtypeStruct((M, N), out_dtype)]
    out_specs = [pl.BlockSpec((tm, tn), lambda i, j, k: (i, j))]
    if has_rms:
        out_shape.append(jax.ShapeDtypeStruct((1, N), _F32))
        out_specs.append(pl.BlockSpec((1, N), lambda i, j, k: (0, 0)))
    if has_norm:
        out_shape.append(jax.ShapeDtypeStruct((M, K), _MXU))
        out_specs.append(pl.BlockSpec((tm, K), lambda i, j, k: (i, 0)))
    outs, got = _fused_call(
        body, name=name, out_shape=out_shape, grid=grid, in_specs=in_specs, out_specs=out_specs,
        scratch_shapes=[pltpu.VMEM((tm, tn), _F32)] if nk > 1 else [],
        sem=("arbitrary",) * 3 if has_rms or has_norm else ("parallel", "parallel", "arbitrary"), args=args, xchg=xchg)
    out = tuple(outs) if has_rms or has_norm else outs[0]
    return out if xchg is None else (out, got)


def _rms_fwd(x, g, *, name, xchg=None):
    N, D = x.shape
    tr = _rows(N, 512)

    def body(x_ref, g_ref, o_ref):
        xv = x_ref[...]
        r = lax.rsqrt(jnp.mean(xv * xv, axis=-1, keepdims=True) + _EPS)
        o_ref[...] = (xv * r * g_ref[...]).astype(o_ref.dtype)

    (out,), got = _fused_call(
        body, name=name, out_shape=[jax.ShapeDtypeStruct((N, D), _MXU)], grid=(N // tr,),
        in_specs=[pl.BlockSpec((tr, D), lambda i: (i, 0)), pl.BlockSpec((1, D), lambda i: (0, 0))],
        out_specs=[pl.BlockSpec((tr, D), lambda i: (i, 0))], scratch_shapes=[], sem=("parallel",),
        args=(x, g.reshape(1, D)), xchg=xchg)
    return out if xchg is None else (out, got)


def _rms_bwd(x, g, dh, dres, *, name):
    N, D = x.shape
    tr = _rows(N, 256)
    has_res = dres is not None

    def body(*refs):
        if has_res:
            x_ref, g_ref, dh_ref, dres_ref, dx_ref, dg_ref = refs
        else:
            x_ref, g_ref, dh_ref, dg_ref = refs
        xv = x_ref[...]
        r = lax.rsqrt(jnp.mean(xv * xv, axis=-1, keepdims=True) + _EPS)
        xh = xv * r
        dhv = dh_ref[...].astype(_F32)

        @pl.when(pl.program_id(0) == 0)
        def _():
            dg_ref[...] = jnp.zeros_like(dg_ref)

        dg_ref[...] += jnp.sum(dhv * xh, axis=0, keepdims=True)
        if has_res:
            dxn = dhv * g_ref[...]
            dx = r * (dxn - xh * jnp.mean(dxn * xh, axis=-1, keepdims=True))
            dx_ref[...] = dres_ref[...] + dx

    row = pl.BlockSpec((tr, D), lambda i: (i, 0))
    vec = pl.BlockSpec((1, D), lambda i: (0, 0))
    if has_res:
        out_shape = (jax.ShapeDtypeStruct((N, D), _F32), jax.ShapeDtypeStruct((1, D), _F32))
        out_specs = (row, vec)
        in_specs, args = [row, vec, row, row], (x, g.reshape(1, D), dh, dres)
    else:
        out_shape = jax.ShapeDtypeStruct((1, D), _F32)
        out_specs = vec
        in_specs, args = [row, vec, row], (x, g.reshape(1, D), dh)
    return pl.pallas_call(
        body, name=name, out_shape=out_shape, grid=(N // tr,), in_specs=in_specs, out_specs=out_specs,
        compiler_params=_params("arbitrary"),
    )(*args)


def _colsum(a, *, name):
    M, C = a.shape
    tr = _rows(M, 512)

    def body(a_ref, o_ref):
        @pl.when(pl.program_id(0) == 0)
        def _():
            o_ref[...] = jnp.zeros_like(o_ref)

        o_ref[...] += jnp.sum(a_ref[...].astype(_F32), axis=0, keepdims=True)

    return pl.pallas_call(
        body, name=name, out_shape=jax.ShapeDtypeStruct((1, C), _F32), grid=(M // tr,),
        in_specs=[pl.BlockSpec((tr, C), lambda i: (i, 0))], out_specs=pl.BlockSpec((1, C), lambda i: (0, 0)),
        compiler_params=_params("arbitrary"),
    )(a)


def _final_loss(act, w_down, res, g, tgt, *, name):
    N, D = res.shape
    K = act.shape[1]
    tr = _rows(N, 512)

    def body(a_ref, w_ref, res_ref, g_ref, t_ref, loss_ref, dx_ref, dg_ref):
        xv = _dot(a_ref[...].astype(_MXU), w_ref[...].astype(_MXU), 1, 0) + res_ref[...]
        r = lax.rsqrt(jnp.mean(xv * xv, axis=-1, keepdims=True) + _EPS)
        xh = xv * r
        gv = g_ref[...]
        diff = xh * gv - t_ref[...]

        @pl.when(pl.program_id(0) == 0)
        def _():
            loss_ref[...] = jnp.zeros_like(loss_ref)
            dg_ref[...] = jnp.zeros_like(dg_ref)

        part = jnp.sum(jnp.sum(diff * diff, axis=1, keepdims=True), axis=0, keepdims=True) * (0.5 / D)
        loss_ref[...] += jnp.broadcast_to(part, loss_ref.shape)
        dy = diff * (1.0 / D)
        dg_ref[...] += jnp.sum(dy * xh, axis=0, keepdims=True)
        dxn = dy * gv
        dx_ref[...] = r * (dxn - xh * jnp.mean(dxn * xh, axis=-1, keepdims=True))

    row = pl.BlockSpec((tr, D), lambda i: (i, 0))
    vec = pl.BlockSpec((1, D), lambda i: (0, 0))
    return pl.pallas_call(
        body, name=name,
        out_shape=(jax.ShapeDtypeStruct((8, _LANES), _F32), jax.ShapeDtypeStruct((N, D), _F32), jax.ShapeDtypeStruct((1, D), _F32)),
        grid=(N // tr,),
        in_specs=[pl.BlockSpec((tr, K), lambda i: (i, 0)), pl.BlockSpec((None, K, D), lambda i: (0, 0, 0)), row, vec, row],
        out_specs=(pl.BlockSpec((8, _LANES), lambda i: (0, 0)), row, vec),
        compiler_params=_params("arbitrary"),
    )(act, w_down, res, g.reshape(1, D), tgt)


def _mm_gu(h, w_gu, l, *, name, bias=None, glu=False, keep=None, rms_fwd=None, tm=512, tn=1408, xchg=None):
    keep = _MXU if keep is None else keep
    N, K = h.shape
    H = w_gu.shape[-1] // 2
    tm, tn = _pick(N, tm), _pick(H, tn)
    nj = H // tn
    has_bias, has_norm = bias is not None, rms_fwd is not None

    def body(*refs):
        h_ref, wp_ref, wq_ref = refs[:3]
        pair_ref, act_ref = refs[3 + 2 * has_bias + has_norm:][:2]
        if has_norm:
            hv = _rms_rows(h_ref[...], refs[3 + 2 * has_bias][...])
            refs[-1][...] = hv
        else:
            hv = h_ref[...].astype(_MXU)
        p = _dot(hv, wp_ref[...].astype(_MXU), 1, 0)
        q = _dot(hv, wq_ref[...].astype(_MXU), 1, 0)
        if has_bias:
            p = p + refs[3][...]
            q = q + refs[4][...]
        pair_ref[0] = p.astype(pair_ref.dtype)
        pair_ref[1] = q.astype(pair_ref.dtype)
        act_ref[...] = (p * _sigmoid(q) if glu else p * _sigmoid(p) * q).astype(act_ref.dtype)

    in_specs = [pl.BlockSpec((tm, K), lambda i, j: (i, 0)), pl.BlockSpec((None, K, tn), lambda i, j: (l, 0, j)),
                pl.BlockSpec((None, K, tn), lambda i, j: (l, 0, j + nj))]
    args = [h, w_gu, w_gu]
    if has_bias:
        b2 = bias.reshape(1, 2 * H).astype(_F32)
        in_specs += [pl.BlockSpec((1, tn), lambda i, j: (0, j)), pl.BlockSpec((1, tn), lambda i, j: (0, j + nj))]
        args += [b2, b2]
    out_shape = [jax.ShapeDtypeStruct((2, N, H), keep), jax.ShapeDtypeStruct((N, H), keep)]
    out_specs = [pl.BlockSpec((2, tm, tn), lambda i, j: (0, i, j)), pl.BlockSpec((tm, tn), lambda i, j: (i, j))]
    if has_norm:
        in_specs.append(pl.BlockSpec((1, K), lambda i, j: (0, 0)))
        args.append(rms_fwd.reshape(1, K))
        out_shape.append(jax.ShapeDtypeStruct((N, K), _MXU))
        out_specs.append(pl.BlockSpec((tm, K), lambda i, j: (i, 0)))
    outs, got = _fused_call(
        body, name=name, out_shape=out_shape, grid=(N // tm, nj), in_specs=in_specs, out_specs=out_specs,
        scratch_shapes=[], sem=("arbitrary", "arbitrary") if has_norm else ("parallel", "parallel"), args=args, xchg=xchg)
    return (*outs, got)


def _mm_dgu(dx, w_down, l, gu, *, name, tm=512, tn=1408):
    N, K = dx.shape
    H = w_down.shape[-2]
    tm, tn = _pick(N, tm), _pick(H, tn)

    def body(dx_ref, w_ref, gu_ref, o_ref):
        d = _dot(dx_ref[...].astype(_MXU), w_ref[...].astype(_MXU), 1, 1)
        g, u = gu_ref[0].astype(_F32), gu_ref[1].astype(_F32)
        sg = _sigmoid(g)
        o_ref[0] = (d * u * (sg * (1.0 + g * (1.0 - sg)))).astype(o_ref.dtype)
        o_ref[1] = (d * (g * sg)).astype(o_ref.dtype)

    return pl.pallas_call(
        body, name=name, out_shape=jax.ShapeDtypeStruct((2, N, H), _MXU), grid=(N // tm, H // tn),
        in_specs=[pl.BlockSpec((tm, K), lambda i, j: (i, 0)), pl.BlockSpec((None, tn, K), lambda i, j: (l, j, 0)),
                  pl.BlockSpec((2, tm, tn), lambda i, j: (0, i, j))],
        out_specs=pl.BlockSpec((2, tm, tn), lambda i, j: (0, i, j)), compiler_params=_params("parallel", "parallel"),
    )(dx, w_down, gu)


def _gmlp_mix(vb, w, trans):
    tr, W = vb.shape
    lane = lax.broadcasted_iota(jnp.int32, (_CHUNK, _LANES), 1)
    rows = []
    for c in range(tr // _CHUNK):
        tiles = []
        for j in range(W // _LANES):
            t = vb[c * _CHUNK:(c + 1) * _CHUNK, j * _LANES:(j + 1) * _LANES]
            ma = _dot(w[2 * j], t, 0 if trans else 1, 0)
            mb = _dot(w[2 * j + 1], t, 0 if trans else 1, 0)
            tiles.append(jnp.where(lane < _GRP, ma, mb))
        rows.append(jnp.concatenate(tiles, axis=1))
    return jnp.concatenate(rows, axis=0)


def _tril_w(w_ref):
    r = lax.broadcasted_iota(jnp.int32, (_CHUNK, _CHUNK), 0)
    c = lax.broadcasted_iota(jnp.int32, (_CHUNK, _CHUNK), 1)
    return jnp.where((r >= c)[None], w_ref[...], 0.0).astype(_MXU)


def _layernorm_stats(v):
    mu = jnp.mean(v, axis=-1, keepdims=True)
    xc = v - mu
    rstd = lax.rsqrt(jnp.mean(xc * xc, axis=-1, keepdims=True) + _EPS)
    return xc * rstd, rstd


def _gmlp_fwd(proj, ln_g, ln_b, w_s, bias_full, *, name):
    N = proj.shape[0]
    W = ln_g.shape[-1]
    G = w_s.shape[0]
    tr = _rows(N, 512)
    ub, vb_ = 3, 4

    def body(u_ref, v_ref, g_ref, b_ref, w_ref, bias_ref, o_ref):
        u = _gelu(u_ref[...])
        xh, _ = _layernorm_stats(_gelu(v_ref[...]))
        vgn = xh * g_ref[...] + b_ref[...]
        mixed = _gmlp_mix(vgn.astype(_MXU), _tril_w(w_ref), False)
        bias = jnp.concatenate([bias_ref[...]] * (tr // _CHUNK), axis=0)
        o_ref[...] = (u * (mixed + bias)).astype(o_ref.dtype)

    vec = pl.BlockSpec((1, W), lambda i: (0, 0))
    return pl.pallas_call(
        body, name=name, out_shape=jax.ShapeDtypeStruct((N, W), _MXU), grid=(N // tr,),
        in_specs=[pl.BlockSpec((tr, W), lambda i: (i, ub)), pl.BlockSpec((tr, W), lambda i: (i, vb_)), vec, vec,
                  pl.BlockSpec((G, _CHUNK, _CHUNK), lambda i: (0, 0, 0)), pl.BlockSpec((_CHUNK, W), lambda i: (0, 0))],
        out_specs=pl.BlockSpec((tr, W), lambda i: (i, 0)), compiler_params=_params("parallel"),
    )(proj, proj, ln_g.reshape(1, W), ln_b.reshape(1, W), w_s, bias_full)


def _gmlp_bwd(proj, da_src, da_blk, ln_g, ln_b, w_s, bias_full, *, name):
    N = proj.shape[0]
    W = ln_g.shape[-1]
    G = w_s.shape[0]
    tr = _rows(N, 512)
    nch = tr // _CHUNK

    def body(u_ref, v_ref, da_ref, g_ref, b_ref, w_ref, bias_ref, dz_ref, dg_ref, db_ref, dw_ref, dbias_ref):
        @pl.when(pl.program_id(0) == 0)
        def _():
            dg_ref[...] = jnp.zeros_like(dg_ref)
            db_ref[...] = jnp.zeros_like(db_ref)
            dw_ref[...] = jnp.zeros_like(dw_ref)
            dbias_ref[...] = jnp.zeros_like(dbias_ref)

        u_pre, v_pre = u_ref[...], v_ref[...]
        ug = _gelu(u_pre)
        xh, rstd = _layernorm_stats(_gelu(v_pre))
        lg = g_ref[...]
        vgn = xh * lg + b_ref[...]
        vb = vgn.astype(_MXU)
        wt = _tril_w(w_ref)
        mixed = _gmlp_mix(vb, wt, False)
        bias = jnp.concatenate([bias_ref[...]] * nch, axis=0)
        da = da_ref[...].astype(_F32)
        du = da * (mixed + bias)
        dm = da * ug
        dmb = dm.astype(_MXU)
        lane = lax.broadcasted_iota(jnp.int32, (_CHUNK, _LANES), 1)
        r = lax.broadcasted_iota(jnp.int32, (_CHUNK, _CHUNK), 0)
        c = lax.broadcasted_iota(jnp.int32, (_CHUNK, _CHUNK), 1)
        tril = r >= c
        dmsum = dm[0:_CHUNK]
        for ch in range(1, nch):
            dmsum = dmsum + dm[ch * _CHUNK:(ch + 1) * _CHUNK]
        dbias = jnp.zeros((_CHUNK, _LANES), _F32)
        for j in range(W // _LANES):
            tile = dmsum[:, j * _LANES:(j + 1) * _LANES]
            sa = jnp.sum(jnp.where(lane < _GRP, tile, 0.0), axis=1, keepdims=True)
            sb = jnp.sum(jnp.where(lane >= _GRP, tile, 0.0), axis=1, keepdims=True)
            dbias = dbias + jnp.where(lane == 2 * j, sa, 0.0) + jnp.where(lane == 2 * j + 1, sb, 0.0)
            acc_a = jnp.zeros((_CHUNK, _CHUNK), _F32)
            acc_b = jnp.zeros((_CHUNK, _CHUNK), _F32)
            for ch in range(nch):
                dt = dmb[ch * _CHUNK:(ch + 1) * _CHUNK, j * _LANES:(j + 1) * _LANES]
                vt = vb[ch * _CHUNK:(ch + 1) * _CHUNK, j * _LANES:(j + 1) * _LANES]
                acc_a = acc_a + _dot(jnp.where(lane < _GRP, dt, jnp.zeros_like(dt)), vt, 1, 1)
                acc_b = acc_b + _dot(jnp.where(lane >= _GRP, dt, jnp.zeros_like(dt)), vt, 1, 1)
            dw_ref[2 * j] += jnp.where(tril, acc_a, 0.0)
            dw_ref[2 * j + 1] += jnp.where(tril, acc_b, 0.0)
        dbias_ref[...] += dbias
        dvgn = _gmlp_mix(dmb, wt, True)
        dg_ref[...] += jnp.sum(dvgn * xh, axis=0, keepdims=True)
        db_ref[...] += jnp.sum(dvgn, axis=0, keepdims=True)
        dxh = dvgn * lg
        dvg = rstd * (dxh - jnp.mean(dxh, axis=-1, keepdims=True) - xh * jnp.mean(dxh * xh, axis=-1, keepdims=True))
        dz_ref[:, :W] = (du * _gelu_grad(u_pre)).astype(dz_ref.dtype)
        dz_ref[:, W:] = (dvg * _gelu_grad(v_pre)).astype(dz_ref.dtype)

    vec = pl.BlockSpec((1, W), lambda i: (0, 0))
    wspec = pl.BlockSpec((G, _CHUNK, _CHUNK), lambda i: (0, 0, 0))
    return pl.pallas_call(
        body, name=name,
        out_shape=(jax.ShapeDtypeStruct((N, 2 * W), _MXU), jax.ShapeDtypeStruct((1, W), _F32), jax.ShapeDtypeStruct((1, W), _F32),
                   jax.ShapeDtypeStruct((G, _CHUNK, _CHUNK), _F32), jax.ShapeDtypeStruct((_CHUNK, _LANES), _F32)),
        grid=(N // tr,),
        in_specs=[pl.BlockSpec((tr, W), lambda i: (i, 3)), pl.BlockSpec((tr, W), lambda i: (i, 4)),
                  pl.BlockSpec((tr, W), lambda i: (i, da_blk)), vec, vec, wspec, pl.BlockSpec((_CHUNK, W), lambda i: (0, 0))],
        out_specs=(pl.BlockSpec((tr, 2 * W), lambda i: (i, 0)), vec, vec, wspec, pl.BlockSpec((_CHUNK, _LANES), lambda i: (0, 0))),
        compiler_params=_params("arbitrary"),
    )(proj, proj, da_src, ln_g.reshape(1, W), ln_b.reshape(1, W), w_s, bias_full)


def _lane_cumsum(v):
    T = v.shape[1]
    lane = lax.broadcasted_iota(jnp.int32, (8, _LANES), 1)
    carry = jnp.zeros((8, 1), _F32)
    out = []
    for ch in range(T // _LANES):
        blk = v[:, ch * _LANES:(ch + 1) * _LANES]
        sh = 1
        while sh < _LANES:
            blk = blk + jnp.where(lane >= sh, pltpu.roll(blk, sh, 1), 0.0)
            sh *= 2
        blk = blk + carry
        carry = blk[:, _LANES - 1:_LANES]
        out.append(blk)
    return jnp.concatenate(out, axis=1), carry


def _log_sigmoid(x):
    return jnp.minimum(x, 0.0) - jnp.log(1.0 + jnp.exp(-jnp.abs(x)))


def _fox_cum(proj3, f_blk, f_bias, *, name):
    B, T, _ = proj3.shape
    H = f_bias.shape[-1]
    assert H == 8

    def body(f_ref, b_ref, o_ref):
        x = f_ref[0].T[0:8, :] + b_ref[...]
        cum, _ = _lane_cumsum(_log_sigmoid(x))
        o_ref[0] = cum

    return pl.pallas_call(
        body, name=name, out_shape=jax.ShapeDtypeStruct((B, 8, T), _F32), grid=(B,),
        in_specs=[pl.BlockSpec((1, T, _LANES), lambda b: (b, 0, f_blk)), pl.BlockSpec((8, 1), lambda b: (0, 0))],
        out_specs=pl.BlockSpec((1, 8, T), lambda b: (b, 0, 0)), compiler_params=_params("parallel"),
    )(proj3, f_bias.reshape(8, 1))


def _fox_cum_bwd(proj3, f_blk, f_bias, dcum, *, name):
    B, T, _ = proj3.shape

    def body(f_ref, b_ref, dc_ref, df_ref, dbias_ref):
        @pl.when(pl.program_id(0) == 0)
        def _():
            dbias_ref[...] = jnp.zeros_like(dbias_ref)

        x = f_ref[0].T[0:8, :] + b_ref[...]
        dc = dc_ref[0]
        incl, total = _lane_cumsum(dc)
        dlf = total - incl + dc
        df = dlf * _sigmoid(-x)
        full = jnp.concatenate([df, jnp.zeros((_LANES - 8, T), _F32)], axis=0).T
        dbias_ref[...] += jnp.sum(full, axis=0, keepdims=True)
        df_ref[0] = full

    return pl.pallas_call(
        body, name=name,
        out_shape=(jax.ShapeDtypeStruct((B, T, _LANES), _F32), jax.ShapeDtypeStruct((1, _LANES), _F32)), grid=(B,),
        in_specs=[pl.BlockSpec((1, T, _LANES), lambda b: (b, 0, f_blk)), pl.BlockSpec((8, 1), lambda b: (0, 0)),
                  pl.BlockSpec((1, 8, T), lambda b: (b, 0, 0))],
        out_specs=(pl.BlockSpec((1, T, _LANES), lambda b: (b, 0, 0)), pl.BlockSpec((1, _LANES), lambda b: (0, 0))),
        compiler_params=_params("arbitrary"),
    )(proj3, f_bias.reshape(8, 1), dcum)


def _cum_row(cum_ref, h, start, size):
    blk = cum_ref[0, :, pl.ds(start, size)]
    sub = lax.broadcasted_iota(jnp.int32, (blk.shape[0], 1), 0)
    return jnp.sum(jnp.where(sub == h, blk, 0.0), axis=0, keepdims=True)


def _causal(tq, q0, k0):
    r = lax.broadcasted_iota(jnp.int32, (tq, tq), 0)
    c = lax.broadcasted_iota(jnp.int32, (tq, tq), 1)
    return (r + q0) >= (c + k0)


def _fused_call(body, *, name, out_shape, grid, in_specs, out_specs, scratch_shapes, sem, args, xchg):
    out_shape, in_specs, out_specs, scratch_shapes = list(out_shape), list(in_specs), list(out_specs), list(scratch_shapes)
    if xchg is None:
        res = pl.pallas_call(body, name=name, out_shape=out_shape, grid=grid, in_specs=in_specs, out_specs=out_specs,
                             scratch_shapes=scratch_shapes, compiler_params=_params(*sem))(*args)
        return list(res), []
    n_in, n_out, n_scr = len(in_specs), len(out_specs), len(scratch_shapes)

    def fused(*refs):
        ins, refs = refs[:n_in], refs[n_in:]
        xs, refs = refs[:xchg.n_src], refs[xchg.n_src:]
        outs, refs = refs[:n_out], refs[n_out:]
        xd, refs = refs[:xchg.n_dst], refs[xchg.n_dst:]
        scr, sems = refs[:n_scr], refs[n_scr:]
        first = last = None
        for d, g in enumerate(grid):
            i = pl.program_id(d)
            first = (i == 0) if first is None else first & (i == 0)
            last = (i == g - 1) if last is None else last & (i == g - 1)

        @pl.when(first)
        def _():
            xchg.start(xs, xd, sems)

        body(*ins, *outs, *scr)

        @pl.when(last)
        def _():
            xchg.finish(xs, xd, sems)

    res = pl.pallas_call(
        fused, name=name, out_shape=out_shape + xchg.out_shapes, grid=grid, in_specs=in_specs + xchg.in_specs,
        out_specs=out_specs + xchg.out_specs, scratch_shapes=scratch_shapes + xchg.scratch,
        compiler_params=_params(*["arbitrary"] * len(grid)),
    )(*args, *xchg.srcs)
    return list(res[:n_out]), list(res[n_out:])


def _fox_fwd(proj3, cum, *, name, xchg=None):
    B, T, _ = proj3.shape
    H = cum.shape[1]
    W = H * _FOX_HD
    npair = W // _LANES
    tq = _rows(T, _FOX_TQ)
    nq = T // tq

    def body(q_ref, k_ref, v_ref, cum_ref, o_ref, lse_ref):
        p = pl.program_id(1)
        i = pl.program_id(2)
        q0 = pl.multiple_of(i * tq, tq)
        lane = lax.broadcasted_iota(jnp.int32, (1, _LANES), 1)
        q2 = q_ref[0] * _FOX_SCALE
        heads = []
        for hh in range(2):
            msk = (lane < _FOX_HD) if hh == 0 else (lane >= _FOX_HD)
            h = 2 * p + hh
            heads.append((msk, h, jnp.where(msk, q2, 0.0).astype(_MXU), _cum_row(cum_ref, h, q0, _LANES)[:, 0:1]))

        def step(jj, carry, masked):
            k0 = pl.multiple_of(jj * tq, tq)
            k2 = k_ref[0, pl.ds(k0, tq), :].astype(_MXU)
            v2 = v_ref[0, pl.ds(k0, tq), :]
            out = []
            for (msk, h, qm, c0), (m_prev, l_prev, acc) in zip(heads, carry):
                s = _dot(qm, k2, 1, 1) + (c0 - _cum_row(cum_ref, h, k0, tq))
                if masked:
                    s = jnp.where(_causal(tq, q0, k0), s, -jnp.inf)
                m_new = jnp.maximum(m_prev, jnp.max(s, axis=1, keepdims=True))
                alpha = jnp.exp(m_prev - m_new)
                e = jnp.exp(s - m_new)
                l_new = alpha * l_prev + jnp.sum(e, axis=1, keepdims=True)
                vm = jnp.where(msk, v2, 0.0).astype(_MXU)
                out.append((m_new, l_new, alpha * acc + _dot(e.astype(_MXU), vm, 1, 0)))
            return tuple(out)

        init = tuple((jnp.full((tq, 1), -jnp.inf, _F32), jnp.zeros((tq, 1), _F32), jnp.zeros((tq, _LANES), _F32)) for _ in heads)
        carry = step(i, lax.fori_loop(0, i, functools.partial(step, masked=False), init), True)
        o2 = jnp.zeros((tq, _LANES), _F32)
        for hh, (m, l, acc) in enumerate(carry):
            o2 = o2 + acc / l
            lse_ref[0, hh] = jnp.broadcast_to(m + jnp.log(l), (tq, _LANES))
        o_ref[0] = o2.astype(o_ref.dtype)

    (o, lse), got = _fused_call(
        body, name=name,
        out_shape=(jax.ShapeDtypeStruct((B, T, W), _MXU), jax.ShapeDtypeStruct((B, H, T, _LANES), _F32)),
        grid=(B, npair, nq),
        in_specs=[pl.BlockSpec((1, tq, _LANES), lambda b, p, i: (b, i, p)),
                  pl.BlockSpec((1, T, _LANES), lambda b, p, i: (b, 0, npair + p)),
                  pl.BlockSpec((1, T, _LANES), lambda b, p, i: (b, 0, 2 * npair + p)),
                  pl.BlockSpec((1, H, T), lambda b, p, i: (b, 0, 0))],
        out_specs=(pl.BlockSpec((1, tq, _LANES), lambda b, p, i: (b, i, p)),
                   pl.BlockSpec((1, 2, tq, _LANES), lambda b, p, i: (b, p, i, 0))),
        scratch_shapes=[], sem=("parallel", "parallel", "parallel"), args=(proj3, proj3, proj3, cum), xchg=xchg)
    return o, lse, got


def _fox_bwd(proj3, cum, do3, lse, *, name, xchg=None):
    B, T, _ = proj3.shape
    H = cum.shape[1]
    W = H * _FOX_HD
    npair = W // _LANES
    tq = _rows(T, _FOX_TQ)
    nq = T // tq

    def body(q_ref, k_ref, v_ref, cum_ref, do_ref, lse_ref, dq_ref, dk_ref, dv_ref, dc_ref, p_scr, dp_scr, dk_acc, dv_acc, dc_acc):
        p = pl.program_id(1)
        i = pl.program_id(2)
        q0 = pl.multiple_of(i * tq, tq)
        lane = lax.broadcasted_iota(jnp.int32, (1, _LANES), 1)

        @pl.when(i == 0)
        def _():
            dk_acc[...] = jnp.zeros_like(dk_acc)
            dv_acc[...] = jnp.zeros_like(dv_acc)
            dc_acc[...] = jnp.zeros_like(dc_acc)

        q2 = q_ref[0] * _FOX_SCALE
        do2 = do_ref[0].astype(_F32)
        heads = []
        for hh in range(2):
            msk = (lane < _FOX_HD) if hh == 0 else (lane >= _FOX_HD)
            h = 2 * p + hh
            heads.append((hh, msk, h, jnp.where(msk, q2, 0.0).astype(_MXU), jnp.where(msk, do2, 0.0).astype(_MXU),
                          _cum_row(cum_ref, h, q0, _LANES)[:, 0:1], lse_ref[0, hh][:, 0:1]))

        def first(jj, deltas, masked):
            k0 = pl.multiple_of(jj * tq, tq)
            kb = k_ref[0, pl.ds(k0, tq), :].astype(_MXU)
            vb = v_ref[0, pl.ds(k0, tq), :].astype(_MXU)
            out = []
            for (hh, _, h, qm, dom, c0, lse_h), delta in zip(heads, deltas):
                s = _dot(qm, kb, 1, 1) + (c0 - _cum_row(cum_ref, h, k0, tq))
                pr = jnp.exp(s - lse_h)
                if masked:
                    pr = jnp.where(_causal(tq, q0, k0), pr, 0.0)
                dp = _dot(dom, vb, 1, 1)
                p_scr[hh, jj] = pr
                dp_scr[hh, jj] = dp
                out.append(delta + jnp.sum(pr * dp, axis=1, keepdims=True))
            return tuple(out)

        zero = tuple(jnp.zeros((tq, 1), _F32) for _ in heads)
        deltas = first(i, lax.fori_loop(0, i, functools.partial(first, masked=False), zero), True)

        def second(jj, dq):
            k0 = pl.multiple_of(jj * tq, tq)
            k2 = k_ref[0, pl.ds(k0, tq), :]
            dk = jnp.zeros((tq, _LANES), _F32)
            dv = jnp.zeros((tq, _LANES), _F32)
            for (hh, msk, _, qm, dom, _, _), delta in zip(heads, deltas):
                pr = p_scr[hh, jj]
                ds = pr * (dp_scr[hh, jj] - delta)
                dsb = ds.astype(_MXU)
                dv = dv + _dot(pr.astype(_MXU), dom, 0, 0)
                dk = dk + _dot(dsb, qm, 0, 0)
                dc_acc[hh:hh + 1, pl.ds(k0, tq)] += jnp.sum(ds, axis=0, keepdims=True)
                dq = dq + _dot(dsb, jnp.where(msk, k2, 0.0).astype(_MXU), 1, 0)
            dv_acc[pl.ds(k0, tq), :] += dv
            dk_acc[pl.ds(k0, tq), :] += dk
            return dq

        dq2 = lax.fori_loop(0, i + 1, second, jnp.zeros((tq, _LANES), _F32))
        dq_ref[0] = (dq2 * _FOX_SCALE).astype(dq_ref.dtype)

        @pl.when(i == nq - 1)
        def _():
            dk_ref[0] = dk_acc[...].astype(dk_ref.dtype)
            dv_ref[0] = dv_acc[...].astype(dv_ref.dtype)
            dc_ref[0, 0] = -dc_acc[...]

    full = lambda blk: pl.BlockSpec((1, T, _LANES), lambda b, p, i, blk=blk: (b, 0, blk * npair + p))
    part = lambda blk: pl.BlockSpec((1, tq, _LANES), lambda b, p, i, blk=blk: (b, i, blk * npair + p))
    (dq, dk, dv, dcum), got = _fused_call(
        body, name=name,
        out_shape=(jax.ShapeDtypeStruct((B, T, W), _MXU), jax.ShapeDtypeStruct((B, T, W), _MXU),
                   jax.ShapeDtypeStruct((B, T, W), _MXU), jax.ShapeDtypeStruct((B, npair, 2, T), _F32)),
        grid=(B, npair, nq),
        in_specs=[part(0), full(1), full(2), pl.BlockSpec((1, H, T), lambda b, p, i: (b, 0, 0)), part(0),
                  pl.BlockSpec((1, 2, tq, _LANES), lambda b, p, i: (b, p, i, 0))],
        out_specs=(part(0), full(0), full(0), pl.BlockSpec((1, 1, 2, T), lambda b, p, i: (b, p, 0, 0))),
        scratch_shapes=[pltpu.VMEM((2, nq, tq, tq), _F32), pltpu.VMEM((2, nq, tq, tq), _F32), pltpu.VMEM((T, _LANES), _F32),
                        pltpu.VMEM((T, _LANES), _F32), pltpu.VMEM((2, T), _F32)],
        sem=("parallel", "parallel", "arbitrary"), args=(proj3, proj3, proj3, cum, do3, lse), xchg=xchg)
    return dq, dk, dv, dcum, got


def _xa_probs(qh, kh, scale):
    s = _dot(qh, kh, 1, 1) * scale
    e = jnp.exp(s - jnp.max(s, axis=1, keepdims=True))
    return e / jnp.sum(e, axis=1, keepdims=True)


def _xa_fwd(q3, kv3, *, name):
    B, T, D = q3.shape
    M = kv3.shape[1]
    hd = D // _XA_HEADS
    scale = hd ** -0.5
    tq = _rows(T, 512)

    def body(q_ref, kv_ref, o_ref):
        for h in range(_XA_HEADS):
            sl = slice(h * hd, (h + 1) * hd)
            p = _xa_probs(q_ref[0, :, sl], kv_ref[0, :, sl], scale)
            o_ref[0, :, sl] = _dot(p.astype(_MXU), kv_ref[0, :, D + h * hd:D + (h + 1) * hd], 1, 0).astype(o_ref.dtype)

    return pl.pallas_call(
        body, name=name, out_shape=jax.ShapeDtypeStruct((B, T, D), _MXU), grid=(B, T // tq),
        in_specs=[pl.BlockSpec((1, tq, D), lambda b, i: (b, i, 0)), pl.BlockSpec((1, M, 2 * D), lambda b, i: (b, 0, 0))],
        out_specs=pl.BlockSpec((1, tq, D), lambda b, i: (b, i, 0)), compiler_params=_params("parallel", "parallel"),
    )(q3, kv3)


def _xa_bwd(q3, kv3, do3, *, name):
    B, T, D = q3.shape
    M = kv3.shape[1]
    hd = D // _XA_HEADS
    scale = hd ** -0.5
    tq = _rows(T, 512)

    def body(q_ref, kv_ref, do_ref, dq_ref, dkv_ref):
        @pl.when(pl.program_id(1) == 0)
        def _():
            dkv_ref[...] = jnp.zeros_like(dkv_ref)

        for h in range(_XA_HEADS):
            sl = slice(h * hd, (h + 1) * hd)
            slv = slice(D + h * hd, D + (h + 1) * hd)
            qh, kh, vh, doh = q_ref[0, :, sl], kv_ref[0, :, sl], kv_ref[0, :, slv], do_ref[0, :, sl]
            p = _xa_probs(qh, kh, scale)
            dkv_ref[0, :, slv] += _dot(p.astype(_MXU), doh, 0, 0)
            dp = _dot(doh, vh, 1, 1)
            ds = (p * (dp - jnp.sum(p * dp, axis=1, keepdims=True))).astype(_MXU)
            dq_ref[0, :, sl] = (_dot(ds, kh, 1, 0) * scale).astype(dq_ref.dtype)
            dkv_ref[0, :, sl] += _dot(ds, qh, 0, 0) * scale

    blk = pl.BlockSpec((1, tq, D), lambda b, i: (b, i, 0))
    kvs = pl.BlockSpec((1, M, 2 * D), lambda b, i: (b, 0, 0))
    return pl.pallas_call(
        body, name=name,
        out_shape=(jax.ShapeDtypeStruct((B, T, D), _MXU), jax.ShapeDtypeStruct((B, M, 2 * D), _F32)),
        grid=(B, T // tq), in_specs=[blk, kvs, blk], out_specs=(blk, kvs),
        compiler_params=_params("parallel", "arbitrary"),
    )(q3, kv3, do3)


def _rotated_copies(ext, rot, tt):
    rot[0] = ext[...]
    for b in range(1, 8):
        rot[b, 0:tt + _HALO - 8, :] = ext[b:b + tt + _HALO - 8, :]


def _shifted(rot, off, r0, rows, c0):
    a, b = divmod(off, 8)
    return rot[b, 8 * a + r0:8 * a + r0 + rows, c0:c0 + _LANES]


def _conv_fwd(y3, dw_w, dw_b, ln_g, ln_b, *, name, xchg=None):
    B, T, C = y3.shape
    tt = _rows(T, 256)
    nt = T // tt

    def body(prev_ref, cur_ref, w_ref, b_ref, g_ref, lb_ref, y2_ref, y4_ref, ext, rot):
        i = pl.program_id(1)
        ext[0:_HALO, :] = jnp.where(i > 0, prev_ref[0, tt - _HALO:tt, :], 0.0)
        ext[_HALO:_HALO + tt, :] = cur_ref[0]
        _rotated_copies(ext, rot, tt)
        for c0 in range(0, C, _LANES):
            acc = jnp.broadcast_to(b_ref[:, c0:c0 + _LANES], (tt, _LANES))
            for j in range(_CONV_K):
                acc = acc + w_ref[j:j + 1, c0:c0 + _LANES] * _shifted(rot, _HALO - (_CONV_K - 1) + j, 0, tt, c0)
            y2_ref[0, :, c0:c0 + _LANES] = acc
        xh, _ = _layernorm_stats(y2_ref[0])
        z = xh * g_ref[...] + lb_ref[...]
        y4_ref[0] = (z * _sigmoid(z)).astype(y4_ref.dtype)

    vec = pl.BlockSpec((1, C), lambda b, i: (0, 0))
    blk = pl.BlockSpec((1, tt, C), lambda b, i: (b, i, 0))
    (y2, y4), got = _fused_call(
        body, name=name,
        out_shape=(jax.ShapeDtypeStruct((B, T, C), _F32), jax.ShapeDtypeStruct((B, T, C), _MXU)),
        grid=(B, nt),
        in_specs=[pl.BlockSpec((1, tt, C), lambda b, i: (b, jnp.maximum(i - 1, 0), 0)), blk,
                  pl.BlockSpec((_HALO, C), lambda b, i: (0, 0)), vec, vec, vec],
        out_specs=(blk, blk),
        scratch_shapes=[pltpu.VMEM((tt + _HALO, C), _F32), pltpu.VMEM((8, tt + _HALO, C), _F32)],
        sem=("parallel", "parallel"), args=(y3, y3, dw_w, dw_b.reshape(1, C), ln_g.reshape(1, C), ln_b.reshape(1, C)), xchg=xchg)
    return y2, y4, got


def _conv_ln_bwd(y2, dx, w_out, ln_g, ln_b, *, name):
    N, C = y2.shape
    D = dx.shape[1]
    tr = _rows(N, 512)

    def body(y_ref, dx_ref, w_ref, g_ref, b_ref, dy_ref, dg_ref, db_ref, dwb_ref):
        @pl.when(pl.program_id(0) == 0)
        def _():
            dg_ref[...] = jnp.zeros_like(dg_ref)
            db_ref[...] = jnp.zeros_like(db_ref)
            dwb_ref[...] = jnp.zeros_like(dwb_ref)

        dy4 = _dot(dx_ref[...].astype(_MXU), w_ref[...].astype(_MXU), 1, 1)
        xh, rstd = _layernorm_stats(y_ref[...])
        gv = g_ref[...]
        z = xh * gv + b_ref[...]
        sg = _sigmoid(z)
        dz = dy4 * (sg * (1.0 + z * (1.0 - sg)))
        dg_ref[...] += jnp.sum(dz * xh, axis=0, keepdims=True)
        db_ref[...] += jnp.sum(dz, axis=0, keepdims=True)
        dxh = dz * gv
        dy = rstd * (dxh - jnp.mean(dxh, axis=-1, keepdims=True) - xh * jnp.mean(dxh * xh, axis=-1, keepdims=True))
        dwb_ref[...] += jnp.sum(dy, axis=0, keepdims=True)
        dy_ref[...] = dy

    row = pl.BlockSpec((tr, C), lambda i: (i, 0))
    vec = pl.BlockSpec((1, C), lambda i: (0, 0))
    v = jax.ShapeDtypeStruct((1, C), _F32)
    return pl.pallas_call(
        body, name=name, out_shape=(jax.ShapeDtypeStruct((N, C), _F32), v, v, v), grid=(N // tr,),
        in_specs=[row, pl.BlockSpec((tr, D), lambda i: (i, 0)), pl.BlockSpec((None, C, D), lambda i: (0, 0, 0)), vec, vec],
        out_specs=(row, vec, vec, vec), compiler_params=_params("arbitrary"),
    )(y2, dx, w_out, ln_g.reshape(1, C), ln_b.reshape(1, C))


def _conv_bwd(y3, dy23, ag3, dw_w, *, name, xchg=None):
    B, T, C = y3.shape
    tt = _rows(T, 256)
    nt = T // tt

    rs = _rows(tt, _CONV_ROWS)

    def groups(v):
        return jnp.sum(v.reshape(rs // 8, 8, _LANES), axis=0)

    def body(yp_ref, yc_ref, dc_ref, dn_ref, a_ref, g_ref, w_ref, dag_ref, dw_ref, dbin_ref, yext, dext, yrot, drot, dw_acc, db_acc):
        b = pl.program_id(0)
        i = pl.program_id(1)

        @pl.when((b == 0) & (i == 0))
        def _():
            dw_acc[...] = jnp.zeros_like(dw_acc)
            db_acc[...] = jnp.zeros_like(db_acc)

        yext[0:_HALO, :] = jnp.where(i > 0, yp_ref[0, tt - _HALO:tt, :], 0.0)
        yext[_HALO:_HALO + tt, :] = yc_ref[0]
        dext[0:tt, :] = dc_ref[0]
        dext[tt:tt + _HALO, :] = jnp.where(i < nt - 1, dn_ref[0, 0:_HALO, :], 0.0)
        _rotated_copies(yext, yrot, tt)
        _rotated_copies(dext, drot, tt)
        for c0 in range(0, C, _LANES):
            for r0 in range(0, tt, rs):
                d_cur = dext[r0:r0 + rs, c0:c0 + _LANES]
                dy = jnp.zeros((rs, _LANES), _F32)
                for j in range(_CONV_K):
                    sh = _CONV_K - 1 - j
                    dy = dy + w_ref[j:j + 1, c0:c0 + _LANES] * _shifted(drot, sh, r0, rs, c0)
                    dw_acc[j, :, c0:c0 + _LANES] += groups(d_cur * _shifted(yrot, _HALO - sh, r0, rs, c0))
                a, g = a_ref[0, r0:r0 + rs, c0:c0 + _LANES], g_ref[0, r0:r0 + rs, c0:c0 + _LANES]
                sg = _sigmoid(g)
                da = dy * sg
                dg = dy * a * (sg * (1.0 - sg))
                dag_ref[0, r0:r0 + rs, c0:c0 + _LANES] = da.astype(dag_ref.dtype)
                dag_ref[0, r0:r0 + rs, C + c0:C + c0 + _LANES] = dg.astype(dag_ref.dtype)
                db_acc[:, c0:c0 + _LANES] += groups(da)
                db_acc[:, C + c0:C + c0 + _LANES] += groups(dg)

        @pl.when((b == B - 1) & (i == nt - 1))
        def _():
            dw_ref[...] = jnp.sum(dw_acc[...], axis=1)
            dbin_ref[...] = jnp.sum(db_acc[...], axis=0, keepdims=True)

    blk = pl.BlockSpec((1, tt, C), lambda b, i: (b, i, 0))
    (dag, ddw, dbin), got = _fused_call(
        body, name=name,
        out_shape=(jax.ShapeDtypeStruct((B, T, 2 * C), _MXU), jax.ShapeDtypeStruct((_HALO, C), _F32),
                   jax.ShapeDtypeStruct((1, 2 * C), _F32)),
        grid=(B, nt),
        in_specs=[pl.BlockSpec((1, tt, C), lambda b, i: (b, jnp.maximum(i - 1, 0), 0)), blk, blk,
                  pl.BlockSpec((1, tt, C), lambda b, i: (b, jnp.minimum(i + 1, nt - 1), 0)),
                  pl.BlockSpec((None, 1, tt, C), lambda b, i: (0, b, i, 0)), pl.BlockSpec((None, 1, tt, C), lambda b, i: (1, b, i, 0)),
                  pl.BlockSpec((_HALO, C), lambda b, i: (0, 0))],
        out_specs=(pl.BlockSpec((1, tt, 2 * C), lambda b, i: (b, i, 0)), pl.BlockSpec((_HALO, C), lambda b, i: (0, 0)),
                   pl.BlockSpec((1, 2 * C), lambda b, i: (0, 0))),
        scratch_shapes=[pltpu.VMEM((tt + _HALO, C), _F32), pltpu.VMEM((tt + _HALO, C), _F32),
                        pltpu.VMEM((8, tt + _HALO, C), _F32), pltpu.VMEM((8, tt + _HALO, C), _F32),
                        pltpu.VMEM((_HALO, 8, C), _F32), pltpu.VMEM((8, 2 * C), _F32)],
        sem=("arbitrary", "arbitrary"), args=(y3, y3, dy23, dy23, ag3, ag3, dw_w), xchg=xchg)
    return dag, ddw, dbin, got


class _Exchange:
    def __init__(self, items):
        self.per_peer = [pp for _, pp in items]
        self.srcs, self.out_shapes, self.pieces = [], [], []
        for t, (srcs, per_peer) in enumerate(items):
            blk = srcs[0].shape[1:] if per_peer else srcs[0].shape
            self.out_shapes.append(jax.ShapeDtypeStruct((len(srcs), _N_DEV) + tuple(blk), srcs[0].dtype))
            for l, s in enumerate(srcs):
                self.pieces.append((t, l, len(self.srcs)))
                self.srcs.append(s)
        self.n_src, self.n_dst, n_pc = len(self.srcs), len(items), len(self.pieces)
        self.in_specs = [pl.BlockSpec(memory_space=pl.ANY)] * self.n_src
        self.out_specs = [pl.BlockSpec(memory_space=pl.ANY)] * self.n_dst
        self.scratch = [pltpu.SemaphoreType.DMA((n_pc, _N_DEV - 1)), pltpu.SemaphoreType.DMA((n_pc, _N_DEV - 1)),
                        pltpu.SemaphoreType.DMA((n_pc,))]

    def _copies(self, src_refs, dst_refs, sems, kind):
        send_sems, recv_sems, loc_sems = sems
        x, y, c = lax.axis_index("x"), lax.axis_index("y"), lax.axis_index("c")
        me = 4 * x + 2 * y + c
        out = []
        for i, (t, l, s) in enumerate(self.pieces):
            def src_for(p, s=s, t=t):
                return src_refs[s].at[p] if self.per_peer[t] else src_refs[s]

            if kind == "local":
                out.append(pltpu.make_async_copy(src_for(me), dst_refs[t].at[l, me], loc_sems.at[i]))
                continue
            for k in range(1, _N_DEV):
                px, py, pc = (1 - x if k & 4 else x), (1 - y if k & 2 else y), (1 - c if k & 1 else c)
                p = 4 * px + 2 * py + pc
                out.append(pltpu.make_async_remote_copy(
                    src_ref=src_for(p), dst_ref=dst_refs[t].at[l, p if kind == "recv" else me],
                    send_sem=send_sems.at[i, k - 1], recv_sem=recv_sems.at[i, k - 1],
                    device_id=(px, py, pc), device_id_type=pl.DeviceIdType.MESH))
        return out

    def start(self, src_refs, dst_refs, sems):
        for cp in self._copies(src_refs, dst_refs, sems, "local") + self._copies(src_refs, dst_refs, sems, "send"):
            cp.start()

    def finish(self, src_refs, dst_refs, sems):
        for cp in self._copies(src_refs, dst_refs, sems, "send"):
            cp.wait_send()
        for cp in self._copies(src_refs, dst_refs, sems, "recv"):
            cp.wait_recv()
        for cp in self._copies(src_refs, dst_refs, sems, "local"):
            cp.wait()


def _exchange(items, *, name):
    ex = _Exchange(items)

    def body(*refs):
        parts = refs[:ex.n_src], refs[ex.n_src:ex.n_src + ex.n_dst], refs[ex.n_src + ex.n_dst:]
        ex.start(*parts)
        ex.finish(*parts)

    return pl.pallas_call(
        body, name=name, out_shape=ex.out_shapes, in_specs=ex.in_specs, out_specs=ex.out_specs, scratch_shapes=ex.scratch,
        compiler_params=pltpu.CompilerParams(has_side_effects=True),
    )(*ex.srcs)


def _adam_update(g, w, m, v):
    c1 = 1.0 / (1.0 - _ADAM_B1 ** _ADAM_STEP)
    c2 = 1.0 / (1.0 - _ADAM_B2 ** _ADAM_STEP)
    m2 = _ADAM_B1 * m + (1.0 - _ADAM_B1) * g
    v2 = _ADAM_B2 * v + (1.0 - _ADAM_B2) * (g * g)
    return -_ADAM_LR * ((m2 * c1) / (jnp.sqrt(v2 * c2) + _ADAM_EPS) + _ADAM_WD * w), m2, v2


def _adamw_big(recvs, w, m, v, *, name):
    L, R, C = w.shape
    tr = _rows(R, 256)
    nb = R // tr

    def body(*refs):
        r_refs = refs[:L]
        w_ref, m_ref, v_ref, g_ref, d_ref, mo_ref, vo_ref = refs[L:]
        for l in range(L):
            @pl.when(pl.program_id(0) == l)
            def _(r_ref=r_refs[l]):
                g = r_ref[0, 0].astype(_F32)
                for k in range(1, _N_DEV):
                    g = g + r_ref[0, k].astype(_F32)
                g_ref[0] = g
                d_ref[0], mo_ref[0], vo_ref[0] = _adam_update(g, w_ref[0], m_ref[0], v_ref[0])

    def recv_spec(l):
        return pl.BlockSpec((1, _N_DEV, tr, C), lambda ll, i: (0, 0, jnp.where(ll == l, i, jnp.where(ll < l, 0, nb - 1)), 0))

    blk = pl.BlockSpec((1, tr, C), lambda l, i: (l, i, 0))
    o = jax.ShapeDtypeStruct((L, R, C), _F32)
    return pl.pallas_call(
        body, name=name, out_shape=(o, o, o, o), grid=(L, nb),
        in_specs=[recv_spec(l) for l in range(L)] + [blk, blk, blk], out_specs=(blk, blk, blk, blk),
        compiler_params=_params("arbitrary", "arbitrary"),
    )(*recvs, w, m, v)


def _adamw_small(tensors, *, name):
    n = len(tensors)
    lanes = [t[4] for t in tensors]
    layers = [len(t[0]) for t in tensors]

    def body(*refs):
        pos = 0
        ins = []
        for t in range(n):
            ins.append((refs[pos:pos + layers[t]], *refs[pos + layers[t]:pos + layers[t] + 3]))
            pos += layers[t] + 3
        outs = refs[pos:]
        for t in range(n):
            r_refs, w_ref, m_ref, v_ref = ins[t]
            g_ref, d_ref, mo_ref, vo_ref = outs[4 * t:4 * t + 4]
            for l in range(layers[t]):
                g = r_refs[l][0, 0]
                for k in range(1, _N_DEV):
                    g = g + r_refs[l][0, k]
                if lanes[t] is not None:
                    g = g[..., :lanes[t]]
                g_ref[l] = g
                d_ref[l], mo_ref[l], vo_ref[l] = _adam_update(g, w_ref[l], m_ref[l], v_ref[l])

    args, out_shape = [], []
    for recvs, w, m, v, _ in tensors:
        args += [*recvs, w, m, v]
        out_shape += [jax.ShapeDtypeStruct(w.shape, _F32)] * 4
    outs = pl.pallas_call(
        body, name=name, out_shape=out_shape,
        in_specs=[pl.BlockSpec(memory_space=pltpu.VMEM)] * len(args), out_specs=[pl.BlockSpec(memory_space=pltpu.VMEM)] * len(out_shape),
        compiler_params=_params(),
    )(*args)
    return [tuple(outs[4 * t:4 * t + 4]) for t in range(n)]


_BIG = (("w_in_e", 2), ("w_out_e", 1), ("conv_w_in", 2), ("conv_w_out", 1), ("xa_wq", 1), ("xa_wkv", 2), ("xa_wo", 1),
        ("ffn_w_gu", 2), ("ffn_w_down", 1))
_SMALL_SHARDED = (("mix_norm_o", 1), ("conv_b_in", 1), ("conv_dw_w", 2), ("conv_dw_b", 1), ("conv_ln_g", 1),
                  ("conv_ln_b", 1), ("conv_b_out", 1))
_REPLICATED = ("mix_norm_e", "fox_f_bias", "gmlp_ln_g", "gmlp_ln_b", "gmlp_w_s", "gmlp_b_s", "xa_norm", "mem_norm",
               "ffn_norm", "final_norm")
_WEIGHTS = ("mix_norm_e", "w_in_e", "fox_f_bias", "gmlp_ln_g", "gmlp_ln_b", "gmlp_w_s", "gmlp_b_s", "w_out_e", "mix_norm_o",
            "conv_w_in", "conv_b_in", "conv_dw_w", "conv_dw_b", "conv_ln_g", "conv_ln_b", "conv_w_out", "conv_b_out",
            "xa_norm", "mem_norm", "xa_wq", "xa_wkv", "xa_wo", "ffn_norm", "ffn_w_gu", "ffn_w_down", "final_norm")


def _cols_to_peers(g, n=_N_DEV):
    K, N = g.shape[-2:]
    return jnp.swapaxes(g.reshape(g.shape[:-1] + (n, N // n)), -3, -2)


def _weight_items(pieces, wsrc):
    return [([wsrc[n][l]] if n in dict(_BIG) else [wsrc[n]], False) for n, l in pieces]


def _place_weights(P, pieces, gathered):
    axis = dict(_BIG + _SMALL_SHARDED)
    for (n, l), g in zip(pieces, gathered):
        if n in dict(_BIG):
            P.setdefault(n, {})[l] = g.reshape(1, -1, g.shape[-1]) if axis[n] == 1 else _peers_to_cols(g)
        else:
            P[n] = _peers_to_cols(g[0, :, 0])[None] if axis[n] == 2 else g.reshape(1, -1)


def _grad_items(pieces, G):
    axis = dict(_BIG + _SMALL_SHARDED)
    items = []
    for n, l in pieces:
        g = G[n][l]
        if n in _REPLICATED:
            items.append(([g], False))
        elif n == "ffn_w_gu":
            half = _N_DEV // 2
            items.append(([jnp.concatenate([_cols_to_peers(g[0], half), _cols_to_peers(g[1], half)], axis=0)], True))
        elif n in dict(_BIG):
            items.append(([g.reshape(_N_DEV, -1, g.shape[-1]) if axis[n] == 1 else _cols_to_peers(g)], True))
        else:
            items.append(([_cols_to_peers(g) if axis[n] == 2 else g.reshape(_N_DEV, 1, -1)], True))
    return items


def _peers_to_cols(d):
    K, c = d.shape[-2:]
    return jnp.swapaxes(d, -3, -2).reshape(d.shape[:-3] + (K, _N_DEV * c))


def _local_step(x, mem, tgt, P, wsrc=None, fwd_hooks=None, bwd_hooks=None):
    fwd_hooks, bwd_hooks = fwd_hooks or {}, bwd_hooks or {}
    sent = {}

    def gather(kernel_name):
        return _Exchange(_weight_items(fwd_hooks[kernel_name], wsrc)) if kernel_name in fwd_hooks else None

    def placed(kernel_name, got):
        if kernel_name in fwd_hooks:
            _place_weights(P, fwd_hooks[kernel_name], got)

    def scatter(kernel_name):
        return _Exchange(_grad_items(bwd_hooks[kernel_name], G)) if kernel_name in bwd_hooks else None

    def received(kernel_name, got):
        if kernel_name in bwd_hooks:
            sent.update(zip(bwd_hooks[kernel_name], got))

    def mm(a, b, *, name, **kw):
        ex = gather(name) or scatter(name)
        out = _mm(a, b, name=name, xchg=ex, **kw)
        if ex is None:
            return out
        out, got = out
        placed(name, got)
        received(name, got)
        return out

    B, T, D = x.shape
    M = mem.shape[1]
    N = B * T
    W = D // 2
    H = W // _FOX_HD
    f_blk = 5 * W // _LANES
    G = {}
    x0 = x.reshape(N, D)
    memf = mem.reshape(B * M, D)

    h_e = _rms_fwd(x0, P["mix_norm_e"][0], name="rms_mix_e", xchg=gather("rms_mix_e"))
    if "rms_mix_e" in fwd_hooks:
        h_e, got = h_e
        placed("rms_mix_e", got)
    w_in_pad = _pad_w_in(P["w_in_e"][0][0], W, H)[None]
    proj = mm(h_e, w_in_pad, bl=0, name="mm_in_e", tn=896)
    proj3 = proj.reshape(B, T, -1)
    cum = _fox_cum(proj3, f_blk, P["fox_f_bias"][0], name="fox_cum")
    o_fox, lse, got = _fox_fwd(proj3, cum, name="fox_fwd", xchg=gather("fox_fwd"))
    placed("fox_fwd", got)
    bias_full = jnp.repeat(P["gmlp_b_s"][0].T, _GRP, axis=1)
    a_out = _gmlp_fwd(proj, P["gmlp_ln_g"][0], P["gmlp_ln_b"][0], P["gmlp_w_s"][0], bias_full, name="gmlp_fwd")
    mixcat = jnp.concatenate([o_fox.reshape(N, W), a_out], axis=1)
    x1 = mm(mixcat, P["w_out_e"][0], bl=0, res=x0, name="mm_out_e")

    def xa_ffn_fwd(xin, l, last=False):
        s = {}
        s["q"], s["h_xa"] = mm(xin, P["xa_wq"][l], bl=0, rms_fwd=P["xa_norm"][l], out_dtype=_MXU, name=f"mm_q{l}")
        s["mn"] = _rms_fwd(memf, P["mem_norm"][l], name=f"rms_mem{l}")
        s["kv"] = mm(s["mn"], P["xa_wkv"][l], bl=0, out_dtype=_MXU, name=f"mm_kv{l}")
        s["o"] = _xa_fwd(s["q"].reshape(B, T, D), s["kv"].reshape(B, M, 2 * D), name=f"xa_fwd{l}").reshape(N, D)
        s["x_mid"] = mm(s["o"], P["xa_wo"][l], bl=0, res=xin, name=f"mm_o{l}")
        s["gu"], s["act"], s["h_ffn"], got = _mm_gu(s["x_mid"], P["ffn_w_gu"][l], 0, rms_fwd=P["ffn_norm"][l], name=f"mm_gu{l}",
                                                    xchg=gather(f"mm_gu{l}"))
        placed(f"mm_gu{l}", got)
        s["x_in"] = xin
        if last:
            return None, s
        xout = mm(s["act"], P["ffn_w_down"][l], bl=0, res=s["x_mid"], name=f"mm_down{l}", tn=512)
        return xout, s

    x3, s0 = xa_ffn_fwd(x1, 0)
    ag, y, h_o, _ = _mm_gu(x3, P["conv_w_in"][0], 0, bias=P["conv_b_in"][0], glu=True, keep=_F32, rms_fwd=P["mix_norm_o"][0],
                           name="mm_conv_in")
    C = y.shape[1]
    dw_w = jnp.pad(P["conv_dw_w"][0], ((0, _HALO - _CONV_K), (0, 0)))
    y2, y4, got = _conv_fwd(y.reshape(B, T, C), dw_w, P["conv_dw_b"][0], P["conv_ln_g"][0], P["conv_ln_b"][0], name="conv_fwd",
                            xchg=gather("conv_fwd"))
    placed("conv_fwd", got)
    x4 = mm(y4.reshape(N, C), P["conv_w_out"][0], bl=0, bias=P["conv_b_out"][0], res=x3, name="mm_conv_out")
    _, s1 = xa_ffn_fwd(x4, 1, last=True)
    loss, dx, dg = _final_loss(s1["act"], P["ffn_w_down"][1], s1["x_mid"], P["final_norm"], tgt.reshape(N, D), name="final_loss")
    G["final_norm"] = [dg]

    def xa_ffn_bwd(dx, s, l):
        for k in ("ffn_w_down", "ffn_w_gu", "ffn_norm", "xa_wo", "xa_wq", "xa_norm", "xa_wkv", "mem_norm"):
            G.setdefault(k, {})
        dgu = _mm_dgu(dx, P["ffn_w_down"][l], 0, s["gu"], name=f"mm_dgu{l}")
        G["ffn_w_down"][l] = mm(s["act"], dx, ta=True, out_dtype=_MXU, name=f"mm_dwdown{l}", tm=1408)
        G["ffn_w_gu"][l] = (mm(s["h_ffn"], dgu, ta=True, bl=0, out_dtype=_MXU, name=f"mm_dwg{l}", tn=1408),
                            mm(s["h_ffn"], dgu, ta=True, bl=1, out_dtype=_MXU, name=f"mm_dwu{l}", tn=1408))
        dx, G["ffn_norm"][l] = mm(dgu, P["ffn_w_gu"][l], al="cat", bl=0, tb=True, rms_bwd=(s["x_mid"], P["ffn_norm"][l], dx),
                                  name=f"mm_dhffn{l}", tm=512, tn=D, tk=2816)
        do = mm(dx, P["xa_wo"][l], bl=0, tb=True, out_dtype=_MXU, name=f"mm_do{l}")
        G["xa_wo"][l] = mm(s["o"], dx, ta=True, out_dtype=_MXU, name=f"mm_dwo{l}")
        dq, dkv = _xa_bwd(s["q"].reshape(B, T, D), s["kv"].reshape(B, M, 2 * D), do.reshape(B, T, D), name=f"xa_bwd{l}")
        dq, dkv = dq.reshape(N, D), dkv.reshape(B * M, 2 * D)
        G["xa_wq"][l] = mm(s["h_xa"], dq, ta=True, out_dtype=_MXU, name=f"mm_dwq{l}")
        dx, G["xa_norm"][l] = mm(dq, P["xa_wq"][l], bl=0, tb=True, rms_bwd=(s["x_in"], P["xa_norm"][l], dx), name=f"mm_dhxa{l}",
                                 tm=512, tn=D)
        G["xa_wkv"][l] = mm(s["mn"], dkv, ta=True, out_dtype=_MXU, name=f"mm_dwkv{l}")
        dmn = mm(dkv, P["xa_wkv"][l], bl=0, tb=True, name=f"mm_dmn{l}")
        G["mem_norm"][l] = _rms_bwd(memf, P["mem_norm"][l], dmn, None, name=f"rms_mem_bwd{l}")
        return dx

    dx = xa_ffn_bwd(dx, s1, 1)
    G["conv_b_out"] = [_colsum(dx, name="colsum_b_out")]
    G["conv_w_out"] = [mm(y4.reshape(N, C), dx, ta=True, out_dtype=_MXU, name="mm_dwconv_out")]
    dy2, dlg, dlb, ddb = _conv_ln_bwd(y2.reshape(N, C), dx, P["conv_w_out"][0], P["conv_ln_g"][0], P["conv_ln_b"][0],
                                      name="conv_ln_bwd")
    G["conv_ln_g"], G["conv_ln_b"], G["conv_dw_b"] = [dlg], [dlb], [ddb]
    dag, ddw, dbin, got = _conv_bwd(y.reshape(B, T, C), dy2.reshape(B, T, C), ag.reshape(2, B, T, C), dw_w, name="conv_bwd",
                                    xchg=scatter("conv_bwd"))
    received("conv_bwd", got)
    G["conv_dw_w"], G["conv_b_in"] = [ddw[:_CONV_K]], [dbin]
    dag = dag.reshape(N, 2 * C)
    G["conv_w_in"] = [mm(h_o, dag, ta=True, out_dtype=_MXU, name="mm_dwconv_in")]
    dx, dg = mm(dag, P["conv_w_in"][0], bl=0, tb=True, rms_bwd=(x3, P["mix_norm_o"][0], dx), name="mm_dh_o", tm=512, tn=D)
    G["mix_norm_o"] = [dg]
    dx = xa_ffn_bwd(dx, s0, 0)
    G["w_out_e"] = [mm(mixcat, dx, ta=True, out_dtype=_MXU, name="mm_dwout_e")]
    dmix = mm(dx, P["w_out_e"][0], bl=0, tb=True, name="mm_dmix")
    dz, dlg, dlb, dws, dbias = _gmlp_bwd(proj, dmix, 1, P["gmlp_ln_g"][0], P["gmlp_ln_b"][0], P["gmlp_w_s"][0], bias_full,
                                         name="gmlp_bwd")
    G["gmlp_ln_g"], G["gmlp_ln_b"], G["gmlp_w_s"] = [dlg], [dlb], [dws]
    G["gmlp_b_s"] = [dbias[:, :2 * (W // _LANES)].T]
    dmix3 = dmix.reshape(B, T, D)
    dq, dk, dv, dcum, got = _fox_bwd(proj3, cum, dmix3, lse, name="fox_bwd", xchg=scatter("fox_bwd"))
    received("fox_bwd", got)
    df, dfb = _fox_cum_bwd(proj3, f_blk, P["fox_f_bias"][0], dcum.reshape(B, H, T), name="fox_cum_bwd")
    G["fox_f_bias"] = [dfb]
    dproj = jnp.concatenate([dq.reshape(N, W), dk.reshape(N, W), dv.reshape(N, W), dz, df.reshape(N, _LANES).astype(_MXU)], axis=1)
    G["w_in_e"] = [_unpad_w_in(mm(h_e, dproj, ta=True, out_dtype=_MXU, name="mm_dwin_e", tn=896), W, H)]
    dx, dg = mm(dproj, w_in_pad, bl=0, tb=True, rms_bwd=(x0, P["mix_norm_e"][0], dx), name="mm_dh_e", tm=512, tn=D)
    G["mix_norm_e"] = [dg]
    return loss, dx.reshape(B, T, D), G, sent


def _pad_w_in(w_in, W, H):
    f = w_in[:, 3 * W:3 * W + H]
    return jnp.concatenate([w_in[:, :3 * W], w_in[:, 3 * W + H:], jnp.pad(f, ((0, 0), (0, _LANES - H)))], axis=1)


def _unpad_w_in(g, W, H):
    return jnp.concatenate([g[:, :3 * W], g[:, 5 * W:5 * W + H], g[:, 3 * W:5 * W]], axis=1)


def kernel(x, mem, mix_norm_e, w_in_e, fox_f_bias, gmlp_ln_g, gmlp_ln_b, gmlp_w_s, gmlp_b_s, w_out_e, mix_norm_o, conv_w_in, conv_b_in, conv_dw_w, conv_dw_b, conv_ln_g, conv_ln_b, conv_w_out, conv_b_out, xa_norm, mem_norm, xa_wq, xa_wkv, xa_wo, ffn_norm, ffn_w_gu, ffn_w_down, final_norm, loss_target, m_mix_norm_e, m_w_in_e, m_fox_f_bias, m_gmlp_ln_g, m_gmlp_ln_b, m_gmlp_w_s, m_gmlp_b_s, m_w_out_e, m_mix_norm_o, m_conv_w_in, m_conv_b_in, m_conv_dw_w, m_conv_dw_b, m_conv_ln_g, m_conv_ln_b, m_conv_w_out, m_conv_b_out, m_xa_norm, m_mem_norm, m_xa_wq, m_xa_wkv, m_xa_wo, m_ffn_norm, m_ffn_w_gu, m_ffn_w_down, m_final_norm, v_mix_norm_e, v_w_in_e, v_fox_f_bias, v_gmlp_ln_g, v_gmlp_ln_b, v_gmlp_w_s, v_gmlp_b_s, v_w_out_e, v_mix_norm_o, v_conv_w_in, v_conv_b_in, v_conv_dw_w, v_conv_dw_b, v_conv_ln_g, v_conv_ln_b, v_conv_w_out, v_conv_b_out, v_xa_norm, v_mem_norm, v_xa_wq, v_xa_wkv, v_xa_wo, v_ffn_norm, v_ffn_w_gu, v_ffn_w_down, v_final_norm):
    env = dict(locals())
    w = {n: env[n] for n in _WEIGHTS}
    mom = {n: env["m_" + n] for n in _WEIGHTS}
    var = {n: env["v_" + n] for n in _WEIGHTS}
    D = x.shape[-1]
    W = D // 2
    H = W // _FOX_HD

    def layers(n):
        return w[n].shape[0] if w[n].ndim > 1 else 1

    wsrc = {n: (w[n].astype(_MXU) if n in dict(_BIG) else w[n]) for n, _ in _BIG + _SMALL_SHARDED}
    P = {n: w[n] for n in _REPLICATED}
    fwd_hooks = {
        "rms_mix_e": [("w_in_e", 0)],
        "mm_in_e": [("w_out_e", 0), ("xa_wq", 0)],
        "fox_fwd": [("xa_wkv", 0), ("xa_wo", 0), ("ffn_w_gu", 0)],
        "mm_o0": [("xa_wq", 1)],
        "mm_gu0": [("ffn_w_down", 0), ("conv_w_in", 0), ("conv_w_out", 0)] + [(n, 0) for n, _ in _SMALL_SHARDED],
        "mm_down0": [("xa_wkv", 1)],
        "conv_fwd": [("xa_wo", 1), ("ffn_w_gu", 1)],
        "mm_gu1": [("ffn_w_down", 1)],
    }

    last = [("mix_norm_e", 0)]
    in_dh_e = [("w_in_e", 0), ("fox_f_bias", 0)]
    in_conv = [(n, 1) for n in ("ffn_w_gu", "ffn_w_down", "xa_wq", "xa_wkv", "xa_wo", "xa_norm", "mem_norm", "ffn_norm")]
    in_conv += [("final_norm", 0), ("conv_w_out", 0)]
    every = [(n, l) for n in [n for n, _ in _BIG + _SMALL_SHARDED] + list(_REPLICATED) for l in range(layers(n))]
    in_dhffn0 = [("ffn_w_gu", 0)]
    bwd_hooks = {"conv_bwd": in_conv, "mm_dhffn0": in_dhffn0, "mm_dh_e": in_dh_e,
                 "fox_bwd": [pc for pc in every if pc not in last + in_dh_e + in_conv + in_dhffn0]}
    loss, grad_x, G, recv = _local_step(x, mem, loss_target, P, wsrc, fwd_hooks, bwd_hooks)
    loss = lax.psum(loss[0, 0], ("x", "y", "c"))
    recv.update(zip(last, _exchange(_grad_items(last, G), name="scatter_last")))

    def partials(n):
        return [recv[(n, l)] for l in range(layers(n))]

    res = {n: _adamw_big(partials(n), w[n], mom[n], var[n], name="adamw_" + n) for n, _ in _BIG}
    small = [n for n, _ in _SMALL_SHARDED] + list(_REPLICATED)

    def rows(a, n):
        r = recv[(n, 0)]
        return a.reshape((layers(n),) + r.shape[2:-1] + (-1,))

    outs = _adamw_small([(partials(n), rows(w[n], n), rows(mom[n], n), rows(var[n], n),
                          w[n].shape[-1] if w[n].shape[-1] != recv[(n, 0)].shape[-1] else None) for n in small], name="adamw_small")
    for n, o in zip(small, outs):
        res[n] = tuple(a.reshape(w[n].shape) for a in o)
    return (loss, grad_x, *[res[n][0] for n in _WEIGHTS], *[res[n][1] for n in _WEIGHTS],
            *[res[n][2] for n in _WEIGHTS], *[res[n][3] for n in _WEIGHTS])
```

```python
import functools
import math

import jax
import jax.numpy as jnp
from jax import lax
from jax.experimental import pallas as pl
from jax.experimental.pallas import tpu as pltpu

_F32 = jnp.float32
_MXU = jnp.bfloat16
_VMEM_LIMIT = 48 * 1024 * 1024
_LANES = 128
_DW_ROWS = 2048
_EPS = 1e-6
_N_DEV = 8
_FOX_HD = 64
_FOX_SCALE = _FOX_HD ** -0.5
_FOX_TQ = 512
_CHUNK = 128
_GRP = 64
_CONV_K = 31
_HALO = 32
_CONV_ROWS = 128
_XA_HEADS = 4
_GELU_C = math.sqrt(2.0 / math.pi)
_ADAM_LR, _ADAM_B1, _ADAM_B2, _ADAM_EPS, _ADAM_WD, _ADAM_STEP = 0.001, 0.9, 0.999, 1e-08, 0.01, 10
_FLAT_W = 1024
_FLAT_ALIGN = 16 * _FLAT_W
_BIG_ROWS = 128


def _params(*sem):
    return pltpu.CompilerParams(dimension_semantics=sem if sem else None, vmem_limit_bytes=_VMEM_LIMIT)


def _pick(n, pref):
    if n <= pref:
        return n
    best = None
    for t in range(_LANES, pref + 1, _LANES):
        if n % t == 0:
            best = t
    assert best is not None, (n, pref)
    return best


def _rows(n, pref):
    if n <= pref:
        return n
    t = pref
    while n % t:
        t //= 2
    assert t >= 8, (n, pref)
    return t


def _sigmoid(x):
    return 1.0 / (1.0 + jnp.exp(-x))


def _gelu(x):
    t = jnp.tanh(_GELU_C * (x + 0.044715 * (x * x * x)))
    return 0.5 * x * (1.0 + t)


def _gelu_grad(x):
    x2 = x * x
    t = jnp.tanh(_GELU_C * (x + 0.044715 * (x2 * x)))
    return 0.5 * (1.0 + t) + 0.5 * x * (1.0 - t * t) * (_GELU_C * (1.0 + 3.0 * 0.044715 * x2))


def _dot(a, b, ca, cb):
    return lax.dot_general(a, b, (((ca,), (cb,)), ((), ())), preferred_element_type=_F32)


def _rms_rows(xv, gain):
    return (xv * lax.rsqrt(jnp.mean(xv * xv, axis=-1, keepdims=True) + _EPS) * gain).astype(_MXU)


def _mm(a, b, *, name, ta=False, tb=False, al=None, bl=None, bk0=0, bias=None, res=None, rms_bwd=None, rms_fwd=None,
        out_dtype=_F32, tm=1024, tn=512, tk=1024, xchg=None):
    if ta:
        K, M = a.shape[-2:]
    else:
        M, K = a.shape[-2:]
    if al == "cat":
        assert not ta
        K = a.shape[0] * a.shape[-1]
    if tb:
        N, K2 = b.shape[-2:]
    else:
        K2, N = b.shape[-2:]
    assert K == K2 or (tb and K2 > K), (a.shape, b.shape, ta, tb)
    tm, tn = _pick(M, tm), _pick(N, tn)
    tk = K if (not ta and K <= 2816 and K2 == K) else _pick(a.shape[-1] if al == "cat" else K, tk)
    nk = K // tk
    assert bk0 % tk == 0
    kb = bk0 // tk
    grid = (M // tm, N // tn, nk)
    if al == "cat":
        per = a.shape[-1] // tk
        a_spec = pl.BlockSpec((None, tm, tk), lambda i, j, k: (k // per, i, k % per))
    elif a.ndim == 3:
        a_spec = (pl.BlockSpec((None, tk, tm), lambda i, j, k: (al, k, i)) if ta
                  else pl.BlockSpec((None, tm, tk), lambda i, j, k: (al, i, k)))
    else:
        a_spec = pl.BlockSpec((tk, tm), lambda i, j, k: (k, i)) if ta else pl.BlockSpec((tm, tk), lambda i, j, k: (i, k))
    if b.ndim == 3:
        b_spec = (pl.BlockSpec((None, tn, tk), lambda i, j, k: (bl, j, k + kb)) if tb
                  else pl.BlockSpec((None, tk, tn), lambda i, j, k: (bl, k, j)))
    else:
        b_spec = pl.BlockSpec((tn, tk), lambda i, j, k: (j, k)) if tb else pl.BlockSpec((tk, tn), lambda i, j, k: (k, j))
    in_specs, args = [a_spec, b_spec], [a, b]
    if bias is not None:
        in_specs.append(pl.BlockSpec((1, tn), lambda i, j, k: (0, j)))
        args.append(bias.reshape(1, N).astype(_F32))
    if res is not None:
        in_specs.append(pl.BlockSpec((tm, tn), lambda i, j, k: (i, j)))
        args.append(res)
    has_bias, has_res, has_rms = bias is not None, res is not None, rms_bwd is not None
    if has_rms:
        assert tn == N, (tn, N)
        x, g, dres = rms_bwd
        in_specs += [pl.BlockSpec((tm, N), lambda i, j, k: (i, 0)), pl.BlockSpec((1, N), lambda i, j, k: (0, 0)),
                     pl.BlockSpec((tm, N), lambda i, j, k: (i, 0))]
        args += [x, g.reshape(1, N), dres]
    has_norm = rms_fwd is not None
    if has_norm:
        assert not ta and nk == 1 and a.ndim == 2
        in_specs.append(pl.BlockSpec((1, K), lambda i, j, k: (0, 0)))
        args.append(rms_fwd.reshape(1, K))

    def body(*refs):
        a_ref, b_ref = refs[0], refs[1]
        pos = 2
        bias_ref = res_ref = None
        if has_bias:
            bias_ref = refs[pos]
            pos += 1
        if has_res:
            res_ref = refs[pos]
            pos += 1
        if has_rms:
            x_ref, g_ref, dres_ref = refs[pos:pos + 3]
            pos += 3
        if has_norm:
            gain_ref = refs[pos]
            pos += 1
        o_ref = refs[pos]
        pos += 1
        if has_rms:
            dg_ref = refs[pos]
            pos += 1
        if has_norm:
            h_ref = refs[pos]
            pos += 1
        acc_ref = refs[pos] if nk > 1 else None
        first_rows = pl.program_id(0) == 0
        if has_norm:
            av = _rms_rows(a_ref[...], gain_ref[...])
            h_ref[...] = av
        else:
            av = a_ref[...].astype(_MXU)
        p = _dot(av, b_ref[...].astype(_MXU), 0 if ta else 1, 1 if tb else 0)

        def finish(acc):
            if has_bias:
                acc = acc + bias_ref[...]
            if has_res:
                acc = acc + res_ref[...]
            if has_rms:
                @pl.when(first_rows)
                def _():
                    dg_ref[...] = jnp.zeros_like(dg_ref)

                xv = x_ref[...]
                r = lax.rsqrt(jnp.mean(xv * xv, axis=-1, keepdims=True) + _EPS)
                xh = xv * r
                dg_ref[...] += jnp.sum(acc * xh, axis=0, keepdims=True)
                dxn = acc * g_ref[...]
                acc = dres_ref[...] + r * (dxn - xh * jnp.mean(dxn * xh, axis=-1, keepdims=True))
            o_ref[...] = acc.astype(o_ref.dtype)

        if nk == 1:
            finish(p)
        else:
            k = pl.program_id(2)

            @pl.when(k == 0)
            def _():
                acc_ref[...] = p

            @pl.when(k > 0)
            def _():
                acc_ref[...] += p

            @pl.when(k == nk - 1)
            def _():
                finish(acc_ref[...])

    out_shape = [jax.ShapeDtypeStruct((M, N), out_dtype)]
    out_specs = [pl.BlockSpec((tm, tn), lambda i, j, k: (i, j))]
    if has_rms:
        out_shape.append(jax.ShapeDtypeStruct((1, N), _F32))
        out_specs.append(pl.BlockSpec((1, N), lambda i, j, k: (0, 0)))
    if has_norm:
        out_shape.append(jax.ShapeDtypeStruct((M, K), _MXU))
        out_specs.append(pl.BlockSpec((tm, K), lambda i, j, k: (i, 0)))
    outs, got = _fused_call(
        body, name=name, out_shape=out_shape, grid=grid, in_specs=in_specs, out_specs=out_specs,
        scratch_shapes=[pltpu.VMEM((tm, tn), _F32)] if nk > 1 else [],
        sem=("arbitrary",) * 3 if has_rms or has_norm else ("parallel", "parallel", "arbitrary"), args=args, xchg=xchg)
    out = tuple(outs) if has_rms or has_norm else outs[0]
    return out if xchg is None else (out, got)


def _rms_fwd(x, g, *, name, xchg=None):
    N, D = x.shape
    tr = _rows(N, 512)

    def body(x_ref, g_ref, o_ref):
        xv = x_ref[...]
        r = lax.rsqrt(jnp.mean(xv * xv, axis=-1, keepdims=True) + _EPS)
        o_ref[...] = (xv * r * g_ref[...]).astype(o_ref.dtype)

    (out,), got = _fused_call(
        body, name=name, out_shape=[jax.ShapeDtypeStruct((N, D), _MXU)], grid=(N // tr,),
        in_specs=[pl.BlockSpec((tr, D), lambda i: (i, 0)), pl.BlockSpec((1, D), lambda i: (0, 0))],
        out_specs=[pl.BlockSpec((tr, D), lambda i: (i, 0))], scratch_shapes=[], sem=("parallel",),
        args=(x, g.reshape(1, D)), xchg=xchg)
    return out if xchg is None else (out, got)


def _rms_bwd(x, g, dh, dres, *, name):
    N, D = x.shape
    tr = _rows(N, 256)
    has_res = dres is not None

    def body(*refs):
        if has_res:
            x_ref, g_ref, dh_ref, dres_ref, dx_ref, dg_ref = refs
        else:
            x_ref, g_ref, dh_ref, dg_ref = refs
        xv = x_ref[...]
        r = lax.rsqrt(jnp.mean(xv * xv, axis=-1, keepdims=True) + _EPS)
        xh = xv * r
        dhv = dh_ref[...].astype(_F32)

        @pl.when(pl.program_id(0) == 0)
        def _():
            dg_ref[...] = jnp.zeros_like(dg_ref)

        dg_ref[...] += jnp.sum(dhv * xh, axis=0, keepdims=True)
        if has_res:
            dxn = dhv * g_ref[...]
            dx = r * (dxn - xh * jnp.mean(dxn * xh, axis=-1, keepdims=True))
            dx_ref[...] = dres_ref[...] + dx

    row = pl.BlockSpec((tr, D), lambda i: (i, 0))
    vec = pl.BlockSpec((1, D), lambda i: (0, 0))
    if has_res:
        out_shape = (jax.ShapeDtypeStruct((N, D), _F32), jax.ShapeDtypeStruct((1, D), _F32))
        out_specs = (row, vec)
        in_specs, args = [row, vec, row, row], (x, g.reshape(1, D), dh, dres)
    else:
        out_shape = jax.ShapeDtypeStruct((1, D), _F32)
        out_specs = vec
        in_specs, args = [row, vec, row], (x, g.reshape(1, D), dh)
    return pl.pallas_call(
        body, name=name, out_shape=out_shape, grid=(N // tr,), in_specs=in_specs, out_specs=out_specs,
        compiler_params=_params("arbitrary"),
    )(*args)


def _colsum(a, *, name):
    M, C = a.shape
    tr = _rows(M, 512)

    def body(a_ref, o_ref):
        @pl.when(pl.program_id(0) == 0)
        def _():
            o_ref[...] = jnp.zeros_like(o_ref)

        o_ref[...] += jnp.sum(a_ref[...].astype(_F32), axis=0, keepdims=True)

    return pl.pallas_call(
        body, name=name, out_shape=jax.ShapeDtypeStruct((1, C), _F32), grid=(M // tr,),
        in_specs=[pl.BlockSpec((tr, C), lambda i: (i, 0))], out_specs=pl.BlockSpec((1, C), lambda i: (0, 0)),
        compiler_params=_params("arbitrary"),
    )(a)


def _final_loss(act, w_down, res, g, tgt, *, name):
    N, D = res.shape
    K = act.shape[1]
    tr = _rows(N, 512)

    def body(a_ref, w_ref, res_ref, g_ref, t_ref, loss_ref, dx_ref, dg_ref):
        xv = _dot(a_ref[...].astype(_MXU), w_ref[...].astype(_MXU), 1, 0) + res_ref[...]
        r = lax.rsqrt(jnp.mean(xv * xv, axis=-1, keepdims=True) + _EPS)
        xh = xv * r
        gv = g_ref[...]
        diff = xh * gv - t_ref[...]

        @pl.when(pl.program_id(0) == 0)
        def _():
            loss_ref[...] = jnp.zeros_like(loss_ref)
            dg_ref[...] = jnp.zeros_like(dg_ref)

        part = jnp.sum(jnp.sum(diff * diff, axis=1, keepdims=True), axis=0, keepdims=True) * (0.5 / D)
        loss_ref[...] += jnp.broadcast_to(part, loss_ref.shape)
        dy = diff * (1.0 / D)
        dg_ref[...] += jnp.sum(dy * xh, axis=0, keepdims=True)
        dxn = dy * gv
        dx_ref[...] = r * (dxn - xh * jnp.mean(dxn * xh, axis=-1, keepdims=True))

    row = pl.BlockSpec((tr, D), lambda i: (i, 0))
    vec = pl.BlockSpec((1, D), lambda i: (0, 0))
    return pl.pallas_call(
        body, name=name,
        out_shape=(jax.ShapeDtypeStruct((8, _LANES), _F32), jax.ShapeDtypeStruct((N, D), _F32), jax.ShapeDtypeStruct((1, D), _F32)),
        grid=(N // tr,),
        in_specs=[pl.BlockSpec((tr, K), lambda i: (i, 0)), pl.BlockSpec((None, K, D), lambda i: (0, 0, 0)), row, vec, row],
        out_specs=(pl.BlockSpec((8, _LANES), lambda i: (0, 0)), row, vec),
        compiler_params=_params("arbitrary"),
    )(act, w_down, res, g.reshape(1, D), tgt)


def _mm_gu(h, w_gu, l, *, name, bias=None, glu=False, keep=None, rms_fwd=None, tm=512, tn=1408, xchg=None):
    keep = _MXU if keep is None else keep
    N, K = h.shape
    H = w_gu.shape[-1] // 2
    tm, tn = _pick(N, tm), _pick(H, tn)
    nj = H // tn
    has_bias, has_norm = bias is not None, rms_fwd is not None

    def body(*refs):
        h_ref, wp_ref, wq_ref = refs[:3]
        pair_ref, act_ref = refs[3 + 2 * has_bias + has_norm:][:2]
        if has_norm:
            hv = _rms_rows(h_ref[...], refs[3 + 2 * has_bias][...])
            refs[-1][...] = hv
        else:
            hv = h_ref[...].astype(_MXU)
        p = _dot(hv, wp_ref[...].astype(_MXU), 1, 0)
        q = _dot(hv, wq_ref[...].astype(_MXU), 1, 0)
        if has_bias:
            p = p + refs[3][...]
            q = q + refs[4][...]
        pair_ref[0] = p.astype(pair_ref.dtype)
        pair_ref[1] = q.astype(pair_ref.dtype)
        act_ref[...] = (p * _sigmoid(q) if glu else p * _sigmoid(p) * q).astype(act_ref.dtype)

    in_specs = [pl.BlockSpec((tm, K), lambda i, j: (i, 0)), pl.BlockSpec((None, K, tn), lambda i, j: (l, 0, j)),
                pl.BlockSpec((None, K, tn), lambda i, j: (l, 0, j + nj))]
    args = [h, w_gu, w_gu]
    if has_bias:
        b2 = bias.reshape(1, 2 * H).astype(_F32)
        in_specs += [pl.BlockSpec((1, tn), lambda i, j: (0, j)), pl.BlockSpec((1, tn), lambda i, j: (0, j + nj))]
        args += [b2, b2]
    out_shape = [jax.ShapeDtypeStruct((2, N, H), keep), jax.ShapeDtypeStruct((N, H), keep)]
    out_specs = [pl.BlockSpec((2, tm, tn), lambda i, j: (0, i, j)), pl.BlockSpec((tm, tn), lambda i, j: (i, j))]
    if has_norm:
        in_specs.append(pl.BlockSpec((1, K), lambda i, j: (0, 0)))
        args.append(rms_fwd.reshape(1, K))
        out_shape.append(jax.ShapeDtypeStruct((N, K), _MXU))
        out_specs.append(pl.BlockSpec((tm, K), lambda i, j: (i, 0)))
    outs, got = _fused_call(
        body, name=name, out_shape=out_shape, grid=(N // tm, nj), in_specs=in_specs, out_specs=out_specs,
        scratch_shapes=[], sem=("arbitrary", "arbitrary") if has_norm else ("parallel", "parallel"), args=args, xchg=xchg)
    return (*outs, got)


def _mm_dgu(dx, w_down, l, gu, *, name, tm=512, tn=1408):
    N, K = dx.shape
    H = w_down.shape[-2]
    tm, tn = _pick(N, tm), _pick(H, tn)

    def body(dx_ref, w_ref, gu_ref, o_ref):
        d = _dot(dx_ref[...].astype(_MXU), w_ref[...].astype(_MXU), 1, 1)
        g, u = gu_ref[0].astype(_F32), gu_ref[1].astype(_F32)
        sg = _sigmoid(g)
        o_ref[0] = (d * u * (sg * (1.0 + g * (1.0 - sg)))).astype(o_ref.dtype)
        o_ref[1] = (d * (g * sg)).astype(o_ref.dtype)

    return pl.pallas_call(
        body, name=name, out_shape=jax.ShapeDtypeStruct((2, N, H), _MXU), grid=(N // tm, H // tn),
        in_specs=[pl.BlockSpec((tm, K), lambda i, j: (i, 0)), pl.BlockSpec((None, tn, K), lambda i, j: (l, j, 0)),
                  pl.BlockSpec((2, tm, tn), lambda i, j: (0, i, j))],
        out_specs=pl.BlockSpec((2, tm, tn), lambda i, j: (0, i, j)), compiler_params=_params("parallel", "parallel"),
    )(dx, w_down, gu)


def _gmlp_mix(vb, w, trans):
    tr, W = vb.shape
    lane = lax.broadcasted_iota(jnp.int32, (_CHUNK, _LANES), 1)
    rows = []
    for c in range(tr // _CHUNK):
        tiles = []
        for j in range(W // _LANES):
            t = vb[c * _CHUNK:(c + 1) * _CHUNK, j * _LANES:(j + 1) * _LANES]
            ma = _dot(w[2 * j], t, 0 if trans else 1, 0)
            mb = _dot(w[2 * j + 1], t, 0 if trans else 1, 0)
            tiles.append(jnp.where(lane < _GRP, ma, mb))
        rows.append(jnp.concatenate(tiles, axis=1))
    return jnp.concatenate(rows, axis=0)


def _tril_w(w_ref):
    r = lax.broadcasted_iota(jnp.int32, (_CHUNK, _CHUNK), 0)
    c = lax.broadcasted_iota(jnp.int32, (_CHUNK, _CHUNK), 1)
    return jnp.where((r >= c)[None], w_ref[...], 0.0).astype(_MXU)


def _layernorm_stats(v):
    mu = jnp.mean(v, axis=-1, keepdims=True)
    xc = v - mu
    rstd = lax.rsqrt(jnp.mean(xc * xc, axis=-1, keepdims=True) + _EPS)
    return xc * rstd, rstd


def _gmlp_fwd(proj, ln_g, ln_b, w_s, bias_full, *, name):
    N = proj.shape[0]
    W = ln_g.shape[-1]
    G = w_s.shape[0]
    tr = _rows(N, 512)
    ub, vb_ = 3, 4

    def body(u_ref, v_ref, g_ref, b_ref, w_ref, bias_ref, o_ref):
        u = _gelu(u_ref[...])
        xh, _ = _layernorm_stats(_gelu(v_ref[...]))
        vgn = xh * g_ref[...] + b_ref[...]
        mixed = _gmlp_mix(vgn.astype(_MXU), _tril_w(w_ref), False)
        bias = jnp.concatenate([bias_ref[...]] * (tr // _CHUNK), axis=0)
        o_ref[...] = (u * (mixed + bias)).astype(o_ref.dtype)

    vec = pl.BlockSpec((1, W), lambda i: (0, 0))
    return pl.pallas_call(
        body, name=name, out_shape=jax.ShapeDtypeStruct((N, W), _MXU), grid=(N // tr,),
        in_specs=[pl.BlockSpec((tr, W), lambda i: (i, ub)), pl.BlockSpec((tr, W), lambda i: (i, vb_)), vec, vec,
                  pl.BlockSpec((G, _CHUNK, _CHUNK), lambda i: (0, 0, 0)), pl.BlockSpec((_CHUNK, W), lambda i: (0, 0))],
        out_specs=pl.BlockSpec((tr, W), lambda i: (i, 0)), compiler_params=_params("parallel"),
    )(proj, proj, ln_g.reshape(1, W), ln_b.reshape(1, W), w_s, bias_full)


def _gmlp_bwd(proj, da_src, da_blk, ln_g, ln_b, w_s, bias_full, *, name):
    N = proj.shape[0]
    W = ln_g.shape[-1]
    G = w_s.shape[0]
    tr = _rows(N, 512)
    nch = tr // _CHUNK

    def body(u_ref, v_ref, da_ref, g_ref, b_ref, w_ref, bias_ref, dz_ref, dg_ref, db_ref, dw_ref, dbias_ref):
        @pl.when(pl.program_id(0) == 0)
        def _():
            dg_ref[...] = jnp.zeros_like(dg_ref)
            db_ref[...] = jnp.zeros_like(db_ref)
            dw_ref[...] = jnp.zeros_like(dw_ref)
            dbias_ref[...] = jnp.zeros_like(dbias_ref)

        u_pre, v_pre = u_ref[...], v_ref[...]
        ug = _gelu(u_pre)
        xh, rstd = _layernorm_stats(_gelu(v_pre))
        lg = g_ref[...]
        vgn = xh * lg + b_ref[...]
        vb = vgn.astype(_MXU)
        wt = _tril_w(w_ref)
        mixed = _gmlp_mix(vb, wt, False)
        bias = jnp.concatenate([bias_ref[...]] * nch, axis=0)
        da = da_ref[...].astype(_F32)
        du = da * (mixed + bias)
        dm = da * ug
        dmb = dm.astype(_MXU)
        lane = lax.broadcasted_iota(jnp.int32, (_CHUNK, _LANES), 1)
        r = lax.broadcasted_iota(jnp.int32, (_CHUNK, _CHUNK), 0)
        c = lax.broadcasted_iota(jnp.int32, (_CHUNK, _CHUNK), 1)
        tril = r >= c
        dmsum = dm[0:_CHUNK]
        for ch in range(1, nch):
            dmsum = dmsum + dm[ch * _CHUNK:(ch + 1) * _CHUNK]
        dbias = jnp.zeros((_CHUNK, _LANES), _F32)
        for j in range(W // _LANES):
            tile = dmsum[:, j * _LANES:(j + 1) * _LANES]
            sa = jnp.sum(jnp.where(lane < _GRP, tile, 0.0), axis=1, keepdims=True)
            sb = jnp.sum(jnp.where(lane >= _GRP, tile, 0.0), axis=1, keepdims=True)
            dbias = dbias + jnp.where(lane == 2 * j, sa, 0.0) + jnp.where(lane == 2 * j + 1, sb, 0.0)
            acc_a = jnp.zeros((_CHUNK, _CHUNK), _F32)
            acc_b = jnp.zeros((_CHUNK, _CHUNK), _F32)
            for ch in range(nch):
                dt = dmb[ch * _CHUNK:(ch + 1) * _CHUNK, j * _LANES:(j + 1) * _LANES]
                vt = vb[ch * _CHUNK:(ch + 1) * _CHUNK, j * _LANES:(j + 1) * _LANES]
                acc_a = acc_a + _dot(jnp.where(lane < _GRP, dt, jnp.zeros_like(dt)), vt, 1, 1)
                acc_b = acc_b + _dot(jnp.where(lane >= _GRP, dt, jnp.zeros_like(dt)), vt, 1, 1)
            dw_ref[2 * j] += jnp.where(tril, acc_a, 0.0)
            dw_ref[2 * j + 1] += jnp.where(tril, acc_b, 0.0)
        dbias_ref[...] += dbias
        dvgn = _gmlp_mix(dmb, wt, True)
        dg_ref[...] += jnp.sum(dvgn * xh, axis=0, keepdims=True)
        db_ref[...] += jnp.sum(dvgn, axis=0, keepdims=True)
        dxh = dvgn * lg
        dvg = rstd * (dxh - jnp.mean(dxh, axis=-1, keepdims=True) - xh * jnp.mean(dxh * xh, axis=-1, keepdims=True))
        dz_ref[:, :W] = (du * _gelu_grad(u_pre)).astype(dz_ref.dtype)
        dz_ref[:, W:] = (dvg * _gelu_grad(v_pre)).astype(dz_ref.dtype)

    vec = pl.BlockSpec((1, W), lambda i: (0, 0))
    wspec = pl.BlockSpec((G, _CHUNK, _CHUNK), lambda i: (0, 0, 0))
    return pl.pallas_call(
        body, name=name,
        out_shape=(jax.ShapeDtypeStruct((N, 2 * W), _MXU), jax.ShapeDtypeStruct((1, W), _F32), jax.ShapeDtypeStruct((1, W), _F32),
                   jax.ShapeDtypeStruct((G, _CHUNK, _CHUNK), _F32), jax.ShapeDtypeStruct((_CHUNK, _LANES), _F32)),
        grid=(N // tr,),
        in_specs=[pl.BlockSpec((tr, W), lambda i: (i, 3)), pl.BlockSpec((tr, W), lambda i: (i, 4)),
                  pl.BlockSpec((tr, W), lambda i: (i, da_blk)), vec, vec, wspec, pl.BlockSpec((_CHUNK, W), lambda i: (0, 0))],
        out_specs=(pl.BlockSpec((tr, 2 * W), lambda i: (i, 0)), vec, vec, wspec, pl.BlockSpec((_CHUNK, _LANES), lambda i: (0, 0))),
        compiler_params=_params("arbitrary"),
    )(proj, proj, da_src, ln_g.reshape(1, W), ln_b.reshape(1, W), w_s, bias_full)


def _lane_cumsum(v):
    T = v.shape[1]
    lane = lax.broadcasted_iota(jnp.int32, (8, _LANES), 1)
    carry = jnp.zeros((8, 1), _F32)
    out = []
    for ch in range(T // _LANES):
        blk = v[:, ch * _LANES:(ch + 1) * _LANES]
        sh = 1
        while sh < _LANES:
            blk = blk + jnp.where(lane >= sh, pltpu.roll(blk, sh, 1), 0.0)
            sh *= 2
        blk = blk + carry
        carry = blk[:, _LANES - 1:_LANES]
        out.append(blk)
    return jnp.concatenate(out, axis=1), carry


def _log_sigmoid(x):
    return jnp.minimum(x, 0.0) - jnp.log(1.0 + jnp.exp(-jnp.abs(x)))


def _fox_cum(proj3, f_blk, f_bias, *, name):
    B, T, _ = proj3.shape
    H = f_bias.shape[-1]
    assert H == 8

    def body(f_ref, b_ref, o_ref):
        x = f_ref[0].T[0:8, :] + b_ref[...]
        cum, _ = _lane_cumsum(_log_sigmoid(x))
        o_ref[0] = cum

    return pl.pallas_call(
        body, name=name, out_shape=jax.ShapeDtypeStruct((B, 8, T), _F32), grid=(B,),
        in_specs=[pl.BlockSpec((1, T, _LANES), lambda b: (b, 0, f_blk)), pl.BlockSpec((8, 1), lambda b: (0, 0))],
        out_specs=pl.BlockSpec((1, 8, T), lambda b: (b, 0, 0)), compiler_params=_params("parallel"),
    )(proj3, f_bias.reshape(8, 1))


def _fox_cum_bwd(proj3, f_blk, f_bias, dcum, *, name):
    B, T, _ = proj3.shape

    def body(f_ref, b_ref, dc_ref, df_ref, dbias_ref):
        @pl.when(pl.program_id(0) == 0)
        def _():
            dbias_ref[...] = jnp.zeros_like(dbias_ref)

        x = f_ref[0].T[0:8, :] + b_ref[...]
        dc = dc_ref[0]
        incl, total = _lane_cumsum(dc)
        dlf = total - incl + dc
        df = dlf * _sigmoid(-x)
        full = jnp.concatenate([df, jnp.zeros((_LANES - 8, T), _F32)], axis=0).T
        dbias_ref[...] += jnp.sum(full, axis=0, keepdims=True)
        df_ref[0] = full

    return pl.pallas_call(
        body, name=name,
        out_shape=(jax.ShapeDtypeStruct((B, T, _LANES), _F32), jax.ShapeDtypeStruct((1, _LANES), _F32)), grid=(B,),
        in_specs=[pl.BlockSpec((1, T, _LANES), lambda b: (b, 0, f_blk)), pl.BlockSpec((8, 1), lambda b: (0, 0)),
                  pl.BlockSpec((1, 8, T), lambda b: (b, 0, 0))],
        out_specs=(pl.BlockSpec((1, T, _LANES), lambda b: (b, 0, 0)), pl.BlockSpec((1, _LANES), lambda b: (0, 0))),
        compiler_params=_params("arbitrary"),
    )(proj3, f_bias.reshape(8, 1), dcum)


def _cum_row(cum_ref, h, start, size):
    blk = cum_ref[0, :, pl.ds(start, size)]
    sub = lax.broadcasted_iota(jnp.int32, (blk.shape[0], 1), 0)
    return jnp.sum(jnp.where(sub == h, blk, 0.0), axis=0, keepdims=True)


def _causal(tq, q0, k0):
    r = lax.broadcasted_iota(jnp.int32, (tq, tq), 0)
    c = lax.broadcasted_iota(jnp.int32, (tq, tq), 1)
    return (r + q0) >= (c + k0)


def _fused_call(body, *, name, out_shape, grid, in_specs, out_specs, scratch_shapes, sem, args, xchg):
    out_shape, in_specs, out_specs, scratch_shapes = list(out_shape), list(in_specs), list(out_specs), list(scratch_shapes)
    if xchg is None:
        res = pl.pallas_call(body, name=name, out_shape=out_shape, grid=grid, in_specs=in_specs, out_specs=out_specs,
                             scratch_shapes=scratch_shapes, compiler_params=_params(*sem))(*args)
        return list(res), []
    n_in, n_out, n_scr = len(in_specs), len(out_specs), len(scratch_shapes)

    def fused(*refs):
        ins, refs = refs[:n_in], refs[n_in:]
        xs, refs = refs[:xchg.n_src], refs[xchg.n_src:]
        outs, refs = refs[:n_out], refs[n_out:]
        xd, refs = refs[:xchg.n_dst], refs[xchg.n_dst:]
        scr, sems = refs[:n_scr], refs[n_scr:]
        first = last = None
        for d, g in enumerate(grid):
            i = pl.program_id(d)
            first = (i == 0) if first is None else first & (i == 0)
            last = (i == g - 1) if last is None else last & (i == g - 1)

        @pl.when(first)
        def _():
            xchg.start(xs, xd, sems)

        body(*ins, *outs, *scr)

        @pl.when(last)
        def _():
            xchg.finish(xs, xd, sems)

    res = pl.pallas_call(
        fused, name=name, out_shape=out_shape + xchg.out_shapes, grid=grid, in_specs=in_specs + xchg.in_specs,
        out_specs=out_specs + xchg.out_specs, scratch_shapes=scratch_shapes + xchg.scratch,
        compiler_params=_params(*["arbitrary"] * len(grid)),
    )(*args, *xchg.srcs)
    return list(res[:n_out]), list(res[n_out:])


def _fox_fwd(proj3, cum, *, name, xchg=None):
    B, T, _ = proj3.shape
    H = cum.shape[1]
    W = H * _FOX_HD
    npair = W // _LANES
    tq = _rows(T, _FOX_TQ)
    nq = T // tq

    def body(q_ref, k_ref, v_ref, cum_ref, o_ref, lse_ref):
        p = pl.program_id(1)
        i = pl.program_id(2)
        q0 = pl.multiple_of(i * tq, tq)
        lane = lax.broadcasted_iota(jnp.int32, (1, _LANES), 1)
        q2 = q_ref[0] * _FOX_SCALE
        heads = []
        for hh in range(2):
            msk = (lane < _FOX_HD) if hh == 0 else (lane >= _FOX_HD)
            h = 2 * p + hh
            heads.append((msk, h, jnp.where(msk, q2, 0.0).astype(_MXU), _cum_row(cum_ref, h, q0, _LANES)[:, 0:1]))

        def step(jj, carry, masked):
            k0 = pl.multiple_of(jj * tq, tq)
            k2 = k_ref[0, pl.ds(k0, tq), :].astype(_MXU)
            v2 = v_ref[0, pl.ds(k0, tq), :]
            out = []
            for (msk, h, qm, c0), (m_prev, l_prev, acc) in zip(heads, carry):
                s = _dot(qm, k2, 1, 1) + (c0 - _cum_row(cum_ref, h, k0, tq))
                if masked:
                    s = jnp.where(_causal(tq, q0, k0), s, -jnp.inf)
                m_new = jnp.maximum(m_prev, jnp.max(s, axis=1, keepdims=True))
                alpha = jnp.exp(m_prev - m_new)
                e = jnp.exp(s - m_new)
                l_new = alpha * l_prev + jnp.sum(e, axis=1, keepdims=True)
                vm = jnp.where(msk, v2, 0.0).astype(_MXU)
                out.append((m_new, l_new, alpha * acc + _dot(e.astype(_MXU), vm, 1, 0)))
            return tuple(out)

        init = tuple((jnp.full((tq, 1), -jnp.inf, _F32), jnp.zeros((tq, 1), _F32), jnp.zeros((tq, _LANES), _F32)) for _ in heads)
        carry = step(i, lax.fori_loop(0, i, functools.partial(step, masked=False), init), True)
        o2 = jnp.zeros((tq, _LANES), _F32)
        for hh, (m, l, acc) in enumerate(carry):
            o2 = o2 + acc / l
            lse_ref[0, hh] = jnp.broadcast_to(m + jnp.log(l), (tq, _LANES))
        o_ref[0] = o2.astype(o_ref.dtype)

    (o, lse), got = _fused_call(
        body, name=name,
        out_shape=(jax.ShapeDtypeStruct((B, T, W), _MXU), jax.ShapeDtypeStruct((B, H, T, _LANES), _F32)),
        grid=(B, npair, nq),
        in_specs=[pl.BlockSpec((1, tq, _LANES), lambda b, p, i: (b, i, p)),
                  pl.BlockSpec((1, T, _LANES), lambda b, p, i: (b, 0, npair + p)),
                  pl.BlockSpec((1, T, _LANES), lambda b, p, i: (b, 0, 2 * npair + p)),
                  pl.BlockSpec((1, H, T), lambda b, p, i: (b, 0, 0))],
        out_specs=(pl.BlockSpec((1, tq, _LANES), lambda b, p, i: (b, i, p)),
                   pl.BlockSpec((1, 2, tq, _LANES), lambda b, p, i: (b, p, i, 0))),
        scratch_shapes=[], sem=("parallel", "parallel", "parallel"), args=(proj3, proj3, proj3, cum), xchg=xchg)
    return o, lse, got


def _fox_bwd(proj3, cum, do3, lse, *, name, xchg=None):
    B, T, _ = proj3.shape
    H = cum.shape[1]
    W = H * _FOX_HD
    npair = W // _LANES
    tq = _rows(T, _FOX_TQ)
    nq = T // tq

    def body(q_ref, k_ref, v_ref, cum_ref, do_ref, lse_ref, dq_ref, dk_ref, dv_ref, dc_ref, p_scr, dp_scr, dk_acc, dv_acc, dc_acc):
        p = pl.program_id(1)
        i = pl.program_id(2)
        q0 = pl.multiple_of(i * tq, tq)
        lane = lax.broadcasted_iota(jnp.int32, (1, _LANES), 1)

        @pl.when(i == 0)
        def _():
            dk_acc[...] = jnp.zeros_like(dk_acc)
            dv_acc[...] = jnp.zeros_like(dv_acc)
            dc_acc[...] = jnp.zeros_like(dc_acc)

        q2 = q_ref[0] * _FOX_SCALE
        do2 = do_ref[0].astype(_F32)
        heads = []
        for hh in range(2):
            msk = (lane < _FOX_HD) if hh == 0 else (lane >= _FOX_HD)
            h = 2 * p + hh
            heads.append((hh, msk, h, jnp.where(msk, q2, 0.0).astype(_MXU), jnp.where(msk, do2, 0.0).astype(_MXU),
                          _cum_row(cum_ref, h, q0, _LANES)[:, 0:1], lse_ref[0, hh][:, 0:1]))

        def first(jj, deltas, masked):
            k0 = pl.multiple_of(jj * tq, tq)
            kb = k_ref[0, pl.ds(k0, tq), :].astype(_MXU)
            vb = v_ref[0, pl.ds(k0, tq), :].astype(_MXU)
            out = []
            for (hh, _, h, qm, dom, c0, lse_h), delta in zip(heads, deltas):
                s = _dot(qm, kb, 1, 1) + (c0 - _cum_row(cum_ref, h, k0, tq))
                pr = jnp.exp(s - lse_h)
                if masked:
                    pr = jnp.where(_causal(tq, q0, k0), pr, 0.0)
                dp = _dot(dom, vb, 1, 1)
                p_scr[hh, jj] = pr
                dp_scr[hh, jj] = dp
                out.append(delta + jnp.sum(pr * dp, axis=1, keepdims=True))
            return tuple(out)

        zero = tuple(jnp.zeros((tq, 1), _F32) for _ in heads)
        deltas = first(i, lax.fori_loop(0, i, functools.partial(first, masked=False), zero), True)

        def second(jj, dq):
            k0 = pl.multiple_of(jj * tq, tq)
            k2 = k_ref[0, pl.ds(k0, tq), :]
            dk = jnp.zeros((tq, _LANES), _F32)
            dv = jnp.zeros((tq, _LANES), _F32)
            for (hh, msk, _, qm, dom, _, _), delta in zip(heads, deltas):
                pr = p_scr[hh, jj]
                ds = pr * (dp_scr[hh, jj] - delta)
                dsb = ds.astype(_MXU)
                dv = dv + _dot(pr.astype(_MXU), dom, 0, 0)
                dk = dk + _dot(dsb, qm, 0, 0)
                dc_acc[hh:hh + 1, pl.ds(k0, tq)] += jnp.sum(ds, axis=0, keepdims=True)
                dq = dq + _dot(dsb, jnp.where(msk, k2, 0.0).astype(_MXU), 1, 0)
            dv_acc[pl.ds(k0, tq), :] += dv
            dk_acc[pl.ds(k0, tq), :] += dk
            return dq

        dq2 = lax.fori_loop(0, i + 1, second, jnp.zeros((tq, _LANES), _F32))
        dq_ref[0] = (dq2 * _FOX_SCALE).astype(dq_ref.dtype)

        @pl.when(i == nq - 1)
        def _():
            dk_ref[0] = dk_acc[...].astype(dk_ref.dtype)
            dv_ref[0] = dv_acc[...].astype(dv_ref.dtype)
            dc_ref[0, 0] = -dc_acc[...]

    full = lambda blk: pl.BlockSpec((1, T, _LANES), lambda b, p, i, blk=blk: (b, 0, blk * npair + p))
    part = lambda blk: pl.BlockSpec((1, tq, _LANES), lambda b, p, i, blk=blk: (b, i, blk * npair + p))
    (dq, dk, dv, dcum), got = _fused_call(
        body, name=name,
        out_shape=(jax.ShapeDtypeStruct((B, T, W), _MXU), jax.ShapeDtypeStruct((B, T, W), _MXU),
                   jax.ShapeDtypeStruct((B, T, W), _MXU), jax.ShapeDtypeStruct((B, npair, 2, T), _F32)),
        grid=(B, npair, nq),
        in_specs=[part(0), full(1), full(2), pl.BlockSpec((1, H, T), lambda b, p, i: (b, 0, 0)), part(0),
                  pl.BlockSpec((1, 2, tq, _LANES), lambda b, p, i: (b, p, i, 0))],
        out_specs=(part(0), full(0), full(0), pl.BlockSpec((1, 1, 2, T), lambda b, p, i: (b, p, 0, 0))),
        scratch_shapes=[pltpu.VMEM((2, nq, tq, tq), _F32), pltpu.VMEM((2, nq, tq, tq), _F32), pltpu.VMEM((T, _LANES), _F32),
                        pltpu.VMEM((T, _LANES), _F32), pltpu.VMEM((2, T), _F32)],
        sem=("parallel", "parallel", "arbitrary"), args=(proj3, proj3, proj3, cum, do3, lse), xchg=xchg)
    return dq, dk, dv, dcum, got


def _xa_probs(qh, kh, scale):
    s = _dot(qh, kh, 1, 1) * scale
    e = jnp.exp(s - jnp.max(s, axis=1, keepdims=True))
    return e / jnp.sum(e, axis=1, keepdims=True)


def _xa_fwd(q3, kv3, *, name):
    B, T, D = q3.shape
    M = kv3.shape[1]
    hd = D // _XA_HEADS
    scale = hd ** -0.5
    tq = _rows(T, 512)

    def body(q_ref, kv_ref, o_ref):
        for h in range(_XA_HEADS):
            sl = slice(h * hd, (h + 1) * hd)
            p = _xa_probs(q_ref[0, :, sl], kv_ref[0, :, sl], scale)
            o_ref[0, :, sl] = _dot(p.astype(_MXU), kv_ref[0, :, D + h * hd:D + (h + 1) * hd], 1, 0).astype(o_ref.dtype)

    return pl.pallas_call(
        body, name=name, out_shape=jax.ShapeDtypeStruct((B, T, D), _MXU), grid=(B, T // tq),
        in_specs=[pl.BlockSpec((1, tq, D), lambda b, i: (b, i, 0)), pl.BlockSpec((1, M, 2 * D), lambda b, i: (b, 0, 0))],
        out_specs=pl.BlockSpec((1, tq, D), lambda b, i: (b, i, 0)), compiler_params=_params("parallel", "parallel"),
    )(q3, kv3)


def _xa_bwd(q3, kv3, do3, *, name):
    B, T, D = q3.shape
    M = kv3.shape[1]
    hd = D // _XA_HEADS
    scale = hd ** -0.5
    tq = _rows(T, 512)

    def body(q_ref, kv_ref, do_ref, dq_ref, dkv_ref):
        @pl.when(pl.program_id(1) == 0)
        def _():
            dkv_ref[...] = jnp.zeros_like(dkv_ref)

        for h in range(_XA_HEADS):
            sl = slice(h * hd, (h + 1) * hd)
            slv = slice(D + h * hd, D + (h + 1) * hd)
            qh, kh, vh, doh = q_ref[0, :, sl], kv_ref[0, :, sl], kv_ref[0, :, slv], do_ref[0, :, sl]
            p = _xa_probs(qh, kh, scale)
            dkv_ref[0, :, slv] += _dot(p.astype(_MXU), doh, 0, 0)
            dp = _dot(doh, vh, 1, 1)
            ds = (p * (dp - jnp.sum(p * dp, axis=1, keepdims=True))).astype(_MXU)
            dq_ref[0, :, sl] = (_dot(ds, kh, 1, 0) * scale).astype(dq_ref.dtype)
            dkv_ref[0, :, sl] += _dot(ds, qh, 0, 0) * scale

    blk = pl.BlockSpec((1, tq, D), lambda b, i: (b, i, 0))
    kvs = pl.BlockSpec((1, M, 2 * D), lambda b, i: (b, 0, 0))
    return pl.pallas_call(
        body, name=name,
        out_shape=(jax.ShapeDtypeStruct((B, T, D), _MXU), jax.ShapeDtypeStruct((B, M, 2 * D), _F32)),
        grid=(B, T // tq), in_specs=[blk, kvs, blk], out_specs=(blk, kvs),
        compiler_params=_params("parallel", "arbitrary"),
    )(q3, kv3, do3)


def _rotated_copies(ext, rot, tt):
    rot[0] = ext[...]
    for b in range(1, 8):
        rot[b, 0:tt + _HALO - 8, :] = ext[b:b + tt + _HALO - 8, :]


def _shifted(rot, off, r0, rows, c0):
    a, b = divmod(off, 8)
    return rot[b, 8 * a + r0:8 * a + r0 + rows, c0:c0 + _LANES]


def _conv_fwd(y3, dw_w, dw_b, ln_g, ln_b, *, name, xchg=None):
    B, T, C = y3.shape
    tt = _rows(T, 256)
    nt = T // tt

    def body(prev_ref, cur_ref, w_ref, b_ref, g_ref, lb_ref, y2_ref, y4_ref, ext, rot):
        i = pl.program_id(1)
        ext[0:_HALO, :] = jnp.where(i > 0, prev_ref[0, tt - _HALO:tt, :], 0.0)
        ext[_HALO:_HALO + tt, :] = cur_ref[0]
        _rotated_copies(ext, rot, tt)
        for c0 in range(0, C, _LANES):
            acc = jnp.broadcast_to(b_ref[:, c0:c0 + _LANES], (tt, _LANES))
            for j in range(_CONV_K):
                acc = acc + w_ref[j:j + 1, c0:c0 + _LANES] * _shifted(rot, _HALO - (_CONV_K - 1) + j, 0, tt, c0)
            y2_ref[0, :, c0:c0 + _LANES] = acc
        xh, _ = _layernorm_stats(y2_ref[0])
        z = xh * g_ref[...] + lb_ref[...]
        y4_ref[0] = (z * _sigmoid(z)).astype(y4_ref.dtype)

    vec = pl.BlockSpec((1, C), lambda b, i: (0, 0))
    blk = pl.BlockSpec((1, tt, C), lambda b, i: (b, i, 0))
    (y2, y4), got = _fused_call(
        body, name=name,
        out_shape=(jax.ShapeDtypeStruct((B, T, C), _F32), jax.ShapeDtypeStruct((B, T, C), _MXU)),
        grid=(B, nt),
        in_specs=[pl.BlockSpec((1, tt, C), lambda b, i: (b, jnp.maximum(i - 1, 0), 0)), blk,
                  pl.BlockSpec((_HALO, C), lambda b, i: (0, 0)), vec, vec, vec],
        out_specs=(blk, blk),
        scratch_shapes=[pltpu.VMEM((tt + _HALO, C), _F32), pltpu.VMEM((8, tt + _HALO, C), _F32)],
        sem=("parallel", "parallel"), args=(y3, y3, dw_w, dw_b.reshape(1, C), ln_g.reshape(1, C), ln_b.reshape(1, C)), xchg=xchg)
    return y2, y4, got


def _conv_ln_bwd(y2, dx, w_out, ln_g, ln_b, *, name):
    N, C = y2.shape
    D = dx.shape[1]
    tr = _rows(N, 512)

    def body(y_ref, dx_ref, w_ref, g_ref, b_ref, dy_ref, dg_ref, db_ref, dwb_ref):
        @pl.when(pl.program_id(0) == 0)
        def _():
            dg_ref[...] = jnp.zeros_like(dg_ref)
            db_ref[...] = jnp.zeros_like(db_ref)
            dwb_ref[...] = jnp.zeros_like(dwb_ref)

        dy4 = _dot(dx_ref[...].astype(_MXU), w_ref[...].astype(_MXU), 1, 1)
        xh, rstd = _layernorm_stats(y_ref[...])
        gv = g_ref[...]
        z = xh * gv + b_ref[...]
        sg = _sigmoid(z)
        dz = dy4 * (sg * (1.0 + z * (1.0 - sg)))
        dg_ref[...] += jnp.sum(dz * xh, axis=0, keepdims=True)
        db_ref[...] += jnp.sum(dz, axis=0, keepdims=True)
        dxh = dz * gv
        dy = rstd * (dxh - jnp.mean(dxh, axis=-1, keepdims=True) - xh * jnp.mean(dxh * xh, axis=-1, keepdims=True))
        dwb_ref[...] += jnp.sum(dy, axis=0, keepdims=True)
        dy_ref[...] = dy

    row = pl.BlockSpec((tr, C), lambda i: (i, 0))
    vec = pl.BlockSpec((1, C), lambda i: (0, 0))
    v = jax.ShapeDtypeStruct((1, C), _F32)
    return pl.pallas_call(
        body, name=name, out_shape=(jax.ShapeDtypeStruct((N, C), _F32), v, v, v), grid=(N // tr,),
        in_specs=[row, pl.BlockSpec((tr, D), lambda i: (i, 0)), pl.BlockSpec((None, C, D), lambda i: (0, 0, 0)), vec, vec],
        out_specs=(row, vec, vec, vec), compiler_params=_params("arbitrary"),
    )(y2, dx, w_out, ln_g.reshape(1, C), ln_b.reshape(1, C))


def _conv_bwd(y3, dy23, ag3, dw_w, *, name, xchg=None):
    B, T, C = y3.shape
    tt = _rows(T, 256)
    nt = T // tt

    rs = _rows(tt, _CONV_ROWS)

    def groups(v):
        return jnp.sum(v.reshape(rs // 8, 8, _LANES), axis=0)

    def body(yp_ref, yc_ref, dc_ref, dn_ref, a_ref, g_ref, w_ref, dag_ref, dw_ref, dbin_ref, yext, dext, yrot, drot, dw_acc, db_acc):
        b = pl.program_id(0)
        i = pl.program_id(1)

        @pl.when((b == 0) & (i == 0))
        def _():
            dw_acc[...] = jnp.zeros_like(dw_acc)
            db_acc[...] = jnp.zeros_like(db_acc)

        yext[0:_HALO, :] = jnp.where(i > 0, yp_ref[0, tt - _HALO:tt, :], 0.0)
        yext[_HALO:_HALO + tt, :] = yc_ref[0]
        dext[0:tt, :] = dc_ref[0]
        dext[tt:tt + _HALO, :] = jnp.where(i < nt - 1, dn_ref[0, 0:_HALO, :], 0.0)
        _rotated_copies(yext, yrot, tt)
        _rotated_copies(dext, drot, tt)
        for c0 in range(0, C, _LANES):
            for r0 in range(0, tt, rs):
                d_cur = dext[r0:r0 + rs, c0:c0 + _LANES]
                dy = jnp.zeros((rs, _LANES), _F32)
                for j in range(_CONV_K):
                    sh = _CONV_K - 1 - j
                    dy = dy + w_ref[j:j + 1, c0:c0 + _LANES] * _shifted(drot, sh, r0, rs, c0)
                    dw_acc[j, :, c0:c0 + _LANES] += groups(d_cur * _shifted(yrot, _HALO - sh, r0, rs, c0))
                a, g = a_ref[0, r0:r0 + rs, c0:c0 + _LANES], g_ref[0, r0:r0 + rs, c0:c0 + _LANES]
                sg = _sigmoid(g)
                da = dy * sg
                dg = dy * a * (sg * (1.0 - sg))
                dag_ref[0, r0:r0 + rs, c0:c0 + _LANES] = da.astype(dag_ref.dtype)
                dag_ref[0, r0:r0 + rs, C + c0:C + c0 + _LANES] = dg.astype(dag_ref.dtype)
                db_acc[:, c0:c0 + _LANES] += groups(da)
                db_acc[:, C + c0:C + c0 + _LANES] += groups(dg)

        @pl.when((b == B - 1) & (i == nt - 1))
        def _():
            dw_ref[...] = jnp.sum(dw_acc[...], axis=1)
            dbin_ref[...] = jnp.sum(db_acc[...], axis=0, keepdims=True)

    blk = pl.BlockSpec((1, tt, C), lambda b, i: (b, i, 0))
    (dag, ddw, dbin), got = _fused_call(
        body, name=name,
        out_shape=(jax.ShapeDtypeStruct((B, T, 2 * C), _MXU), jax.ShapeDtypeStruct((_HALO, C), _F32),
                   jax.ShapeDtypeStruct((1, 2 * C), _F32)),
        grid=(B, nt),
        in_specs=[pl.BlockSpec((1, tt, C), lambda b, i: (b, jnp.maximum(i - 1, 0), 0)), blk, blk,
                  pl.BlockSpec((1, tt, C), lambda b, i: (b, jnp.minimum(i + 1, nt - 1), 0)),
                  pl.BlockSpec((None, 1, tt, C), lambda b, i: (0, b, i, 0)), pl.BlockSpec((None, 1, tt, C), lambda b, i: (1, b, i, 0)),
                  pl.BlockSpec((_HALO, C), lambda b, i: (0, 0))],
        out_specs=(pl.BlockSpec((1, tt, 2 * C), lambda b, i: (b, i, 0)), pl.BlockSpec((_HALO, C), lambda b, i: (0, 0)),
                   pl.BlockSpec((1, 2 * C), lambda b, i: (0, 0))),
        scratch_shapes=[pltpu.VMEM((tt + _HALO, C), _F32), pltpu.VMEM((tt + _HALO, C), _F32),
                        pltpu.VMEM((8, tt + _HALO, C), _F32), pltpu.VMEM((8, tt + _HALO, C), _F32),
                        pltpu.VMEM((_HALO, 8, C), _F32), pltpu.VMEM((8, 2 * C), _F32)],
        sem=("arbitrary", "arbitrary"), args=(y3, y3, dy23, dy23, ag3, ag3, dw_w), xchg=xchg)
    return dag, ddw, dbin, got


class _Exchange:
    def __init__(self, items):
        self.per_peer = [pp for _, pp in items]
        self.srcs, self.out_shapes, self.pieces = [], [], []
        for t, (srcs, per_peer) in enumerate(items):
            blk = srcs[0].shape[1:] if per_peer else srcs[0].shape
            self.out_shapes.append(jax.ShapeDtypeStruct((len(srcs), _N_DEV) + tuple(blk), srcs[0].dtype))
            for l, s in enumerate(srcs):
                self.pieces.append((t, l, len(self.srcs)))
                self.srcs.append(s)
        self.n_src, self.n_dst, n_pc = len(self.srcs), len(items), len(self.pieces)
        self.in_specs = [pl.BlockSpec(memory_space=pl.ANY)] * self.n_src
        self.out_specs = [pl.BlockSpec(memory_space=pl.ANY)] * self.n_dst
        self.scratch = [pltpu.SemaphoreType.DMA((n_pc, _N_DEV - 1)), pltpu.SemaphoreType.DMA((n_pc, _N_DEV - 1)),
                        pltpu.SemaphoreType.DMA((n_pc,))]

    def _copies(self, src_refs, dst_refs, sems, kind):
        send_sems, recv_sems, loc_sems = sems
        x, y, c = lax.axis_index("x"), lax.axis_index("y"), lax.axis_index("c")
        me = 4 * x + 2 * y + c
        out = []
        for i, (t, l, s) in enumerate(self.pieces):
            def src_for(p, s=s, t=t):
                return src_refs[s].at[p] if self.per_peer[t] else src_refs[s]

            if kind == "local":
                out.append(pltpu.make_async_copy(src_for(me), dst_refs[t].at[l, me], loc_sems.at[i]))
                continue
            for k in range(1, _N_DEV):
                px, py, pc = (1 - x if k & 4 else x), (1 - y if k & 2 else y), (1 - c if k & 1 else c)
                p = 4 * px + 2 * py + pc
                out.append(pltpu.make_async_remote_copy(
                    src_ref=src_for(p), dst_ref=dst_refs[t].at[l, p if kind == "recv" else me],
                    send_sem=send_sems.at[i, k - 1], recv_sem=recv_sems.at[i, k - 1],
                    device_id=(px, py, pc), device_id_type=pl.DeviceIdType.MESH))
        return out

    def start(self, src_refs, dst_refs, sems):
        for cp in self._copies(src_refs, dst_refs, sems, "local") + self._copies(src_refs, dst_refs, sems, "send"):
            cp.start()

    def finish(self, src_refs, dst_refs, sems):
        for cp in self._copies(src_refs, dst_refs, sems, "send"):
            cp.wait_send()
        for cp in self._copies(src_refs, dst_refs, sems, "recv"):
            cp.wait_recv()
        for cp in self._copies(src_refs, dst_refs, sems, "local"):
            cp.wait()


def _exchange(items, *, name):
    ex = _Exchange(items)

    def body(*refs):
        parts = refs[:ex.n_src], refs[ex.n_src:ex.n_src + ex.n_dst], refs[ex.n_src + ex.n_dst:]
        ex.start(*parts)
        ex.finish(*parts)

    return pl.pallas_call(
        body, name=name, out_shape=ex.out_shapes, in_specs=ex.in_specs, out_specs=ex.out_specs, scratch_shapes=ex.scratch,
        compiler_params=pltpu.CompilerParams(has_side_effects=True),
    )(*ex.srcs)


def _adam_update(g, w, m, v):
    c1 = 1.0 / (1.0 - _ADAM_B1 ** _ADAM_STEP)
    c2 = 1.0 / (1.0 - _ADAM_B2 ** _ADAM_STEP)
    m2 = _ADAM_B1 * m + (1.0 - _ADAM_B1) * g
    v2 = _ADAM_B2 * v + (1.0 - _ADAM_B2) * (g * g)
    return -_ADAM_LR * ((m2 * c1) / (jnp.sqrt(v2 * c2) + _ADAM_EPS) + _ADAM_WD * w), m2, v2


def _adamw_big(recvs, w, m, v, *, name):
    L, R, C = w.shape
    tr = _rows(R, 256)
    nb = R // tr

    def body(*refs):
        r_refs = refs[:L]
        w_ref, m_ref, v_ref, g_ref, d_ref, mo_ref, vo_ref = refs[L:]
        for l in range(L):
            @pl.when(pl.program_id(0) == l)
            def _(r_ref=r_refs[l]):
                g = r_ref[0, 0].astype(_F32)
                for k in range(1, _N_DEV):
                    g = g + r_ref[0, k].astype(_F32)
                g_ref[0] = g
                d_ref[0], mo_ref[0], vo_ref[0] = _adam_update(g, w_ref[0], m_ref[0], v_ref[0])

    def recv_spec(l):
        return pl.BlockSpec((1, _N_DEV, tr, C), lambda ll, i: (0, 0, jnp.where(ll == l, i, jnp.where(ll < l, 0, nb - 1)), 0))

    blk = pl.BlockSpec((1, tr, C), lambda l, i: (l, i, 0))
    o = jax.ShapeDtypeStruct((L, R, C), _F32)
    return pl.pallas_call(
        body, name=name, out_shape=(o, o, o, o), grid=(L, nb),
        in_specs=[recv_spec(l) for l in range(L)] + [blk, blk, blk], out_specs=(blk, blk, blk, blk),
        compiler_params=_params("arbitrary", "arbitrary"),
    )(*recvs, w, m, v)


def _adamw_small(tensors, *, name):
    n = len(tensors)
    lanes = [t[4] for t in tensors]
    layers = [len(t[0]) for t in tensors]

    def body(*refs):
        pos = 0
        ins = []
        for t in range(n):
            ins.append((refs[pos:pos + layers[t]], *refs[pos + layers[t]:pos + layers[t] + 3]))
            pos += layers[t] + 3
        outs = refs[pos:]
        for t in range(n):
            r_refs, w_ref, m_ref, v_ref = ins[t]
            g_ref, d_ref, mo_ref, vo_ref = outs[4 * t:4 * t + 4]
            for l in range(layers[t]):
                g = r_refs[l][0, 0]
                for k in range(1, _N_DEV):
                    g = g + r_refs[l][0, k]
                if lanes[t] is not None:
                    g = g[..., :lanes[t]]
                g_ref[l] = g
                d_ref[l], mo_ref[l], vo_ref[l] = _adam_update(g, w_ref[l], m_ref[l], v_ref[l])

    args, out_shape = [], []
    for recvs, w, m, v, _ in tensors:
        args += [*recvs, w, m, v]
        out_shape += [jax.ShapeDtypeStruct(w.shape, _F32)] * 4
    outs = pl.pallas_call(
        body, name=name, out_shape=out_shape,
        in_specs=[pl.BlockSpec(memory_space=pltpu.VMEM)] * len(args), out_specs=[pl.BlockSpec(memory_space=pltpu.VMEM)] * len(out_shape),
        compiler_params=_params(),
    )(*args)
    return [tuple(outs[4 * t:4 * t + 4]) for t in range(n)]


_BIG = (("w_in_e", 2), ("w_out_e", 1), ("conv_w_in", 2), ("conv_w_out", 1), ("xa_wq", 1), ("xa_wkv", 2), ("xa_wo", 1),
        ("ffn_w_gu", 2), ("ffn_w_down", 1))
_SMALL_SHARDED = (("mix_norm_o", 1), ("conv_b_in", 1), ("conv_dw_w", 2), ("conv_dw_b", 1), ("conv_ln_g", 1),
                  ("conv_ln_b", 1), ("conv_b_out", 1))
_REPLICATED = ("mix_norm_e", "fox_f_bias", "gmlp_ln_g", "gmlp_ln_b", "gmlp_w_s", "gmlp_b_s", "xa_norm", "mem_norm",
               "ffn_norm", "final_norm")
_WEIGHTS = ("mix_norm_e", "w_in_e", "fox_f_bias", "gmlp_ln_g", "gmlp_ln_b", "gmlp_w_s", "gmlp_b_s", "w_out_e", "mix_norm_o",
            "conv_w_in", "conv_b_in", "conv_dw_w", "conv_dw_b", "conv_ln_g", "conv_ln_b", "conv_w_out", "conv_b_out",
            "xa_norm", "mem_norm", "xa_wq", "xa_wkv", "xa_wo", "ffn_norm", "ffn_w_gu", "ffn_w_down", "final_norm")


def _cols_to_peers(g, n=_N_DEV):
    K, N = g.shape[-2:]
    return jnp.swapaxes(g.reshape(g.shape[:-1] + (n, N // n)), -3, -2)


def _weight_items(pieces, wsrc):
    return [([wsrc[n][l]] if n in dict(_BIG) else [wsrc[n]], False) for n, l in pieces]


def _place_weights(P, pieces, gathered):
    axis = dict(_BIG + _SMALL_SHARDED)
    for (n, l), g in zip(pieces, gathered):
        if n in dict(_BIG):
            P.setdefault(n, {})[l] = g.reshape(1, -1, g.shape[-1]) if axis[n] == 1 else _peers_to_cols(g)
        else:
            P[n] = _peers_to_cols(g[0, :, 0])[None] if axis[n] == 2 else g.reshape(1, -1)


def _grad_items(pieces, G):
    axis = dict(_BIG + _SMALL_SHARDED)
    items = []
    for n, l in pieces:
        g = G[n][l]
        if n in _REPLICATED:
            items.append(([g], False))
        elif n == "ffn_w_gu":
            half = _N_DEV // 2
            items.append(([jnp.concatenate([_cols_to_peers(g[0], half), _cols_to_peers(g[1], half)], axis=0)], True))
        elif n in dict(_BIG):
            items.append(([g.reshape(_N_DEV, -1, g.shape[-1]) if axis[n] == 1 else _cols_to_peers(g)], True))
        else:
            items.append(([_cols_to_peers(g) if axis[n] == 2 else g.reshape(_N_DEV, 1, -1)], True))
    return items


def _peers_to_cols(d):
    K, c = d.shape[-2:]
    return jnp.swapaxes(d, -3, -2).reshape(d.shape[:-3] + (K, _N_DEV * c))


def _local_step(x, mem, tgt, P, wsrc=None, fwd_hooks=None, bwd_hooks=None):
    fwd_hooks, bwd_hooks = fwd_hooks or {}, bwd_hooks or {}
    sent = {}

    def gather(kernel_name):
        return _Exchange(_weight_items(fwd_hooks[kernel_name], wsrc)) if kernel_name in fwd_hooks else None

    def placed(kernel_name, got):
        if kernel_name in fwd_hooks:
            _place_weights(P, fwd_hooks[kernel_name], got)

    def scatter(kernel_name):
        return _Exchange(_grad_items(bwd_hooks[kernel_name], G)) if kernel_name in bwd_hooks else None

    def received(kernel_name, got):
        if kernel_name in bwd_hooks:
            sent.update(zip(bwd_hooks[kernel_name], got))

    def mm(a, b, *, name, **kw):
        ex = gather(name) or scatter(name)
        out = _mm(a, b, name=name, xchg=ex, **kw)
        if ex is None:
            return out
        out, got = out
        placed(name, got)
        received(name, got)
        return out

    B, T, D = x.shape
    M = mem.shape[1]
    N = B * T
    W = D // 2
    H = W // _FOX_HD
    f_blk = 5 * W // _LANES
    G = {}
    x0 = x.reshape(N, D)
    memf = mem.reshape(B * M, D)

    h_e = _rms_fwd(x0, P["mix_norm_e"][0], name="rms_mix_e", xchg=gather("rms_mix_e"))
    if "rms_mix_e" in fwd_hooks:
        h_e, got = h_e
        placed("rms_mix_e", got)
    w_in_pad = _pad_w_in(P["w_in_e"][0][0], W, H)[None]
    proj = mm(h_e, w_in_pad, bl=0, name="mm_in_e", tn=896)
    proj3 = proj.reshape(B, T, -1)
    cum = _fox_cum(proj3, f_blk, P["fox_f_bias"][0], name="fox_cum")
    o_fox, lse, got = _fox_fwd(proj3, cum, name="fox_fwd", xchg=gather("fox_fwd"))
    placed("fox_fwd", got)
    bias_full = jnp.repeat(P["gmlp_b_s"][0].T, _GRP, axis=1)
    a_out = _gmlp_fwd(proj, P["gmlp_ln_g"][0], P["gmlp_ln_b"][0], P["gmlp_w_s"][0], bias_full, name="gmlp_fwd")
    mixcat = jnp.concatenate([o_fox.reshape(N, W), a_out], axis=1)
    x1 = mm(mixcat, P["w_out_e"][0], bl=0, res=x0, name="mm_out_e")

    def xa_ffn_fwd(xin, l, last=False):
        s = {}
        s["q"], s["h_xa"] = mm(xin, P["xa_wq"][l], bl=0, rms_fwd=P["xa_norm"][l], out_dtype=_MXU, name=f"mm_q{l}")
        s["mn"] = _rms_fwd(memf, P["mem_norm"][l], name=f"rms_mem{l}")
        s["kv"] = mm(s["mn"], P["xa_wkv"][l], bl=0, out_dtype=_MXU, name=f"mm_kv{l}")
        s["o"] = _xa_fwd(s["q"].reshape(B, T, D), s["kv"].reshape(B, M, 2 * D), name=f"xa_fwd{l}").reshape(N, D)
        s["x_mid"] = mm(s["o"], P["xa_wo"][l], bl=0, res=xin, name=f"mm_o{l}")
        s["gu"], s["act"], s["h_ffn"], got = _mm_gu(s["x_mid"], P["ffn_w_gu"][l], 0, rms_fwd=P["ffn_norm"][l], name=f"mm_gu{l}",
                                                    xchg=gather(f"mm_gu{l}"))
        placed(f"mm_gu{l}", got)
        s["x_in"] = xin
        if last:
            return None, s
        xout = mm(s["act"], P["ffn_w_down"][l], bl=0, res=s["x_mid"], name=f"mm_down{l}", tn=512)
        return xout, s

    x3, s0 = xa_ffn_fwd(x1, 0)
    ag, y, h_o, _ = _mm_gu(x3, P["conv_w_in"][0], 0, bias=P["conv_b_in"][0], glu=True, keep=_F32, rms_fwd=P["mix_norm_o"][0],
                           name="mm_conv_in")
    C = y.shape[1]
    dw_w = jnp.pad(P["conv_dw_w"][0], ((0, _HALO - _CONV_K), (0, 0)))
    y2, y4, got = _conv_fwd(y.reshape(B, T, C), dw_w, P["conv_dw_b"][0], P["conv_ln_g"][0], P["conv_ln_b"][0], name="conv_fwd",
                            xchg=gather("conv_fwd"))
    placed("conv_fwd", got)
    x4 = mm(y4.reshape(N, C), P["conv_w_out"][0], bl=0, bias=P["conv_b_out"][0], res=x3, name="mm_conv_out")
    _, s1 = xa_ffn_fwd(x4, 1, last=True)
    loss, dx, dg = _final_loss(s1["act"], P["ffn_w_down"][1], s1["x_mid"], P["final_norm"], tgt.reshape(N, D), name="final_loss")
    G["final_norm"] = [dg]

    def xa_ffn_bwd(dx, s, l):
        for k in ("ffn_w_down", "ffn_w_gu", "ffn_norm", "xa_wo", "xa_wq", "xa_norm", "xa_wkv", "mem_norm"):
            G.setdefault(k, {})
        dgu = _mm_dgu(dx, P["ffn_w_down"][l], 0, s["gu"], name=f"mm_dgu{l}")
        G["ffn_w_down"][l] = mm(s["act"], dx, ta=True, tk=_DW_ROWS, out_dtype=_MXU, name=f"mm_dwdown{l}", tm=1408)
        G["ffn_w_gu"][l] = (mm(s["h_ffn"], dgu, ta=True, tk=_DW_ROWS, bl=0, out_dtype=_MXU, name=f"mm_dwg{l}", tn=1408),
                            mm(s["h_ffn"], dgu, ta=True, tk=_DW_ROWS, bl=1, out_dtype=_MXU, name=f"mm_dwu{l}", tn=1408))
        dx, G["ffn_norm"][l] = mm(dgu, P["ffn_w_gu"][l], al="cat", bl=0, tb=True, rms_bwd=(s["x_mid"], P["ffn_norm"][l], dx),
                                  name=f"mm_dhffn{l}", tm=512, tn=D, tk=2816)
        do = mm(dx, P["xa_wo"][l], bl=0, tb=True, out_dtype=_MXU, name=f"mm_do{l}")
        G["xa_wo"][l] = mm(s["o"], dx, ta=True, tk=_DW_ROWS, out_dtype=_MXU, name=f"mm_dwo{l}")
        dq, dkv = _xa_bwd(s["q"].reshape(B, T, D), s["kv"].reshape(B, M, 2 * D), do.reshape(B, T, D), name=f"xa_bwd{l}")
        dq, dkv = dq.reshape(N, D), dkv.reshape(B * M, 2 * D)
        G["xa_wq"][l] = mm(s["h_xa"], dq, ta=True, tk=_DW_ROWS, out_dtype=_MXU, name=f"mm_dwq{l}")
        dx, G["xa_norm"][l] = mm(dq, P["xa_wq"][l], bl=0, tb=True, rms_bwd=(s["x_in"], P["xa_norm"][l], dx), name=f"mm_dhxa{l}",
                                 tm=512, tn=D)
        G["xa_wkv"][l] = mm(s["mn"], dkv, ta=True, tk=_DW_ROWS, out_dtype=_MXU, name=f"mm_dwkv{l}")
        dmn = mm(dkv, P["xa_wkv"][l], bl=0, tb=True, name=f"mm_dmn{l}")
        G["mem_norm"][l] = _rms_bwd(memf, P["mem_norm"][l], dmn, None, name=f"rms_mem_bwd{l}")
        return dx

    dx = xa_ffn_bwd(dx, s1, 1)
    G["conv_b_out"] = [_colsum(dx, name="colsum_b_out")]
    G["conv_w_out"] = [mm(y4.reshape(N, C), dx, ta=True, tk=_DW_ROWS, out_dtype=_MXU, name="mm_dwconv_out")]
    dy2, dlg, dlb, ddb = _conv_ln_bwd(y2.reshape(N, C), dx, P["conv_w_out"][0], P["conv_ln_g"][0], P["conv_ln_b"][0],
                                      name="conv_ln_bwd")
    G["conv_ln_g"], G["conv_ln_b"], G["conv_dw_b"] = [dlg], [dlb], [ddb]
    dag, ddw, dbin, got = _conv_bwd(y.reshape(B, T, C), dy2.reshape(B, T, C), ag.reshape(2, B, T, C), dw_w, name="conv_bwd",
                                    xchg=scatter("conv_bwd"))
    received("conv_bwd", got)
    G["conv_dw_w"], G["conv_b_in"] = [ddw[:_CONV_K]], [dbin]
    dag = dag.reshape(N, 2 * C)
    G["conv_w_in"] = [mm(h_o, dag, ta=True, tk=_DW_ROWS, out_dtype=_MXU, name="mm_dwconv_in")]
    dx, dg = mm(dag, P["conv_w_in"][0], bl=0, tb=True, rms_bwd=(x3, P["mix_norm_o"][0], dx), name="mm_dh_o", tm=512, tn=D)
    G["mix_norm_o"] = [dg]
    dx = xa_ffn_bwd(dx, s0, 0)
    G["w_out_e"] = [mm(mixcat, dx, ta=True, tk=_DW_ROWS, out_dtype=_MXU, name="mm_dwout_e")]
    dmix = mm(dx, P["w_out_e"][0], bl=0, tb=True, name="mm_dmix")
    dz, dlg, dlb, dws, dbias = _gmlp_bwd(proj, dmix, 1, P["gmlp_ln_g"][0], P["gmlp_ln_b"][0], P["gmlp_w_s"][0], bias_full,
                                         name="gmlp_bwd")
    G["gmlp_ln_g"], G["gmlp_ln_b"], G["gmlp_w_s"] = [dlg], [dlb], [dws]
    G["gmlp_b_s"] = [dbias[:, :2 * (W // _LANES)].T]
    dmix3 = dmix.reshape(B, T, D)
    dq, dk, dv, dcum, got = _fox_bwd(proj3, cum, dmix3, lse, name="fox_bwd", xchg=scatter("fox_bwd"))
    received("fox_bwd", got)
    df, dfb = _fox_cum_bwd(proj3, f_blk, P["fox_f_bias"][0], dcum.reshape(B, H, T), name="fox_cum_bwd")
    G["fox_f_bias"] = [dfb]
    dproj = jnp.concatenate([dq.reshape(N, W), dk.reshape(N, W), dv.reshape(N, W), dz, df.reshape(N, _LANES).astype(_MXU)], axis=1)
    G["w_in_e"] = [_unpad_w_in(mm(h_e, dproj, ta=True, tk=_DW_ROWS, out_dtype=_MXU, name="mm_dwin_e", tn=896), W, H)]
    dx, dg = mm(dproj, w_in_pad, bl=0, tb=True, rms_bwd=(x0, P["mix_norm_e"][0], dx), name="mm_dh_e", tm=512, tn=D)
    G["mix_norm_e"] = [dg]
    return loss, dx.reshape(B, T, D), G, sent


def _pad_w_in(w_in, W, H):
    f = w_in[:, 3 * W:3 * W + H]
    return jnp.concatenate([w_in[:, :3 * W], w_in[:, 3 * W + H:], jnp.pad(f, ((0, 0), (0, _LANES - H)))], axis=1)


def _unpad_w_in(g, W, H):
    return jnp.concatenate([g[:, :3 * W], g[:, 5 * W:5 * W + H], g[:, 3 * W:5 * W]], axis=1)


def kernel(x, mem, mix_norm_e, w_in_e, fox_f_bias, gmlp_ln_g, gmlp_ln_b, gmlp_w_s, gmlp_b_s, w_out_e, mix_norm_o, conv_w_in, conv_b_in, conv_dw_w, conv_dw_b, conv_ln_g, conv_ln_b, conv_w_out, conv_b_out, xa_norm, mem_norm, xa_wq, xa_wkv, xa_wo, ffn_norm, ffn_w_gu, ffn_w_down, final_norm, loss_target, m_mix_norm_e, m_w_in_e, m_fox_f_bias, m_gmlp_ln_g, m_gmlp_ln_b, m_gmlp_w_s, m_gmlp_b_s, m_w_out_e, m_mix_norm_o, m_conv_w_in, m_conv_b_in, m_conv_dw_w, m_conv_dw_b, m_conv_ln_g, m_conv_ln_b, m_conv_w_out, m_conv_b_out, m_xa_norm, m_mem_norm, m_xa_wq, m_xa_wkv, m_xa_wo, m_ffn_norm, m_ffn_w_gu, m_ffn_w_down, m_final_norm, v_mix_norm_e, v_w_in_e, v_fox_f_bias, v_gmlp_ln_g, v_gmlp_ln_b, v_gmlp_w_s, v_gmlp_b_s, v_w_out_e, v_mix_norm_o, v_conv_w_in, v_conv_b_in, v_conv_dw_w, v_conv_dw_b, v_conv_ln_g, v_conv_ln_b, v_conv_w_out, v_conv_b_out, v_xa_norm, v_mem_norm, v_xa_wq, v_xa_wkv, v_xa_wo, v_ffn_norm, v_ffn_w_gu, v_ffn_w_down, v_final_norm):
    env = dict(locals())
    w = {n: env[n] for n in _WEIGHTS}
    mom = {n: env["m_" + n] for n in _WEIGHTS}
    var = {n: env["v_" + n] for n in _WEIGHTS}
    D = x.shape[-1]
    W = D // 2
    H = W // _FOX_HD

    def layers(n):
        return w[n].shape[0] if w[n].ndim > 1 else 1

    wsrc = {n: (w[n].astype(_MXU) if n in dict(_BIG) else w[n]) for n, _ in _BIG + _SMALL_SHARDED}
    P = {n: w[n] for n in _REPLICATED}
    fwd_hooks = {
        "rms_mix_e": [("w_in_e", 0)],
        "mm_in_e": [("w_out_e", 0), ("xa_wq", 0)],
        "fox_fwd": [("xa_wkv", 0), ("xa_wo", 0), ("ffn_w_gu", 0)],
        "mm_o0": [("xa_wq", 1)],
        "mm_gu0": [("ffn_w_down", 0), ("conv_w_in", 0), ("conv_w_out", 0)] + [(n, 0) for n, _ in _SMALL_SHARDED],
        "mm_down0": [("xa_wkv", 1)],
        "conv_fwd": [("xa_wo", 1), ("ffn_w_gu", 1)],
        "mm_gu1": [("ffn_w_down", 1)],
    }

    last = [("mix_norm_e", 0)]
    in_dh_e = [("w_in_e", 0), ("fox_f_bias", 0)]
    in_conv = [(n, 1) for n in ("ffn_w_gu", "ffn_w_down", "xa_wq", "xa_wkv", "xa_wo", "xa_norm", "mem_norm", "ffn_norm")]
    in_conv += [("final_norm", 0), ("conv_w_out", 0)]
    every = [(n, l) for n in [n for n, _ in _BIG + _SMALL_SHARDED] + list(_REPLICATED) for l in range(layers(n))]
    in_dhffn0 = [("ffn_w_gu", 0)]
    bwd_hooks = {"conv_bwd": in_conv, "mm_dhffn0": in_dhffn0, "mm_dh_e": in_dh_e,
                 "fox_bwd": [pc for pc in every if pc not in last + in_dh_e + in_conv + in_dhffn0]}
    loss, grad_x, G, recv = _local_step(x, mem, loss_target, P, wsrc, fwd_hooks, bwd_hooks)
    loss = lax.psum(loss[0, 0], ("x", "y", "c"))
    recv.update(zip(last, _exchange(_grad_items(last, G), name="scatter_last")))

    def partials(n):
        return [recv[(n, l)] for l in range(layers(n))]

    res = {n: _adamw_big(partials(n), w[n], mom[n], var[n], name="adamw_" + n) for n, _ in _BIG}
    small = [n for n, _ in _SMALL_SHARDED] + list(_REPLICATED)

    def rows(a, n):
        r = recv[(n, 0)]
        return a.reshape((layers(n),) + r.shape[2:-1] + (-1,))

    outs = _adamw_small([(partials(n), rows(w[n], n), rows(mom[n], n), rows(var[n], n),
                          w[n].shape[-1] if w[n].shape[-1] != recv[(n, 0)].shape[-1] else None) for n in small], name="adamw_small")
    for n, o in zip(small, outs):
        res[n] = tuple(a.reshape(w[n].shape) for a in o)
    return (loss, grad_x, *[res[n][0] for n in _WEIGHTS], *[res[n][1] for n in _WEIGHTS],
            *[res[n][2] for n in _WEIGHTS], *[res[n][3] for n in _WEIGHTS])
```

```python
import functools
import math

import jax
import jax.numpy as jnp
from jax import lax
from jax.experimental import pallas as pl
from jax.experimental.pallas import tpu as pltpu

_F32 = jnp.float32
_MXU = jnp.bfloat16
_VMEM_LIMIT = 48 * 1024 * 1024
_LANES = 128
_DW_ROWS = 2048
_EPS = 1e-6
_N_DEV = 8
_FOX_HD = 64
_FOX_SCALE = _FOX_HD ** -0.5
_FOX_TQ = 512
_CHUNK = 128
_GRP = 64
_CONV_K = 31
_HALO = 32
_CONV_ROWS = 128
_XA_HEADS = 4
_GELU_C = math.sqrt(2.0 / math.pi)
_ADAM_LR, _ADAM_B1, _ADAM_B2, _ADAM_EPS, _ADAM_WD, _ADAM_STEP = 0.001, 0.9, 0.999, 1e-08, 0.01, 10
_FLAT_W = 1024
_FLAT_ALIGN = 16 * _FLAT_W
_BIG_ROWS = 128


def _params(*sem):
    return pltpu.CompilerParams(dimension_semantics=sem if sem else None, vmem_limit_bytes=_VMEM_LIMIT)


def _pick(n, pref):
    if n <= pref:
        return n
    best = None
    for t in range(_LANES, pref + 1, _LANES):
        if n % t == 0:
            best = t
    assert best is not None, (n, pref)
    return best


def _rows(n, pref):
    if n <= pref:
        return n
    t = pref
    while n % t:
        t //= 2
    assert t >= 8, (n, pref)
    return t


def _sigmoid(x):
    return 1.0 / (1.0 + jnp.exp(-x))


def _gelu(x):
    t = jnp.tanh(_GELU_C * (x + 0.044715 * (x * x * x)))
    return 0.5 * x * (1.0 + t)


def _gelu_grad(x):
    x2 = x * x
    t = jnp.tanh(_GELU_C * (x + 0.044715 * (x2 * x)))
    return 0.5 * (1.0 + t) + 0.5 * x * (1.0 - t * t) * (_GELU_C * (1.0 + 3.0 * 0.044715 * x2))


def _dot(a, b, ca, cb):
    return lax.dot_general(a, b, (((ca,), (cb,)), ((), ())), preferred_element_type=_F32)


def _rms_rows(xv, gain):
    return (xv * lax.rsqrt(jnp.mean(xv * xv, axis=-1, keepdims=True) + _EPS) * gain).astype(_MXU)


def _mm(a, b, *, name, ta=False, tb=False, al=None, bl=None, bk0=0, bias=None, res=None, rms_bwd=None, rms_fwd=None,
        out_dtype=_F32, tm=1024, tn=512, tk=1024, xchg=None):
    if ta:
        K, M = a.shape[-2:]
    else:
        M, K = a.shape[-2:]
    if al == "cat":
        assert not ta
        K = a.shape[0] * a.shape[-1]
    if tb:
        N, K2 = b.shape[-2:]
    else:
        K2, N = b.shape[-2:]
    assert K == K2 or (tb and K2 > K), (a.shape, b.shape, ta, tb)
    tm, tn = _pick(M, tm), _pick(N, tn)
    tk = K if (not ta and K <= 2816 and K2 == K) else _pick(a.shape[-1] if al == "cat" else K, tk)
    nk = K // tk
    assert bk0 % tk == 0
    kb = bk0 // tk
    grid = (M // tm, N // tn, nk)
    if al == "cat":
        per = a.shape[-1] // tk
        a_spec = pl.BlockSpec((None, tm, tk), lambda i, j, k: (k // per, i, k % per))
    elif a.ndim == 3:
        a_spec = (pl.BlockSpec((None, tk, tm), lambda i, j, k: (al, k, i)) if ta
                  else pl.BlockSpec((None, tm, tk), lambda i, j, k: (al, i, k)))
    else:
        a_spec = pl.BlockSpec((tk, tm), lambda i, j, k: (k, i)) if ta else pl.BlockSpec((tm, tk), lambda i, j, k: (i, k))
    if b.ndim == 3:
        b_spec = (pl.BlockSpec((None, tn, tk), lambda i, j, k: (bl, j, k + kb)) if tb
                  else pl.BlockSpec((None, tk, tn), lambda i, j, k: (bl, k, j)))
    else:
        b_spec = pl.BlockSpec((tn, tk), lambda i, j, k: (j, k)) if tb else pl.BlockSpec((tk, tn), lambda i, j, k: (k, j))
    in_specs, args = [a_spec, b_spec], [a, b]
    if bias is not None:
        in_specs.append(pl.BlockSpec((1, tn), lambda i, j, k: (0, j)))
        args.append(bias.reshape(1, N).astype(_F32))
    if res is not None:
        in_specs.append(pl.BlockSpec((tm, tn), lambda i, j, k: (i, j)))
        args.append(res)
    has_bias, has_res, has_rms = bias is not None, res is not None, rms_bwd is not None
    if has_rms:
        assert tn == N, (tn, N)
        x, g, dres = rms_bwd
        in_specs += [pl.BlockSpec((tm, N), lambda i, j, k: (i, 0)), pl.BlockSpec((1, N), lambda i, j, k: (0, 0)),
                     pl.BlockSpec((tm, N), lambda i, j, k: (i, 0))]
        args += [x, g.reshape(1, N), dres]
    has_norm = rms_fwd is not None
    if has_norm:
        assert not ta and nk == 1 and a.ndim == 2
        in_specs.append(pl.BlockSpec((1, K), lambda i, j, k: (0, 0)))
        args.append(rms_fwd.reshape(1, K))

    def body(*refs):
        a_ref, b_ref = refs[0], refs[1]
        pos = 2
        bias_ref = res_ref = None
        if has_bias:
            bias_ref = refs[pos]
            pos += 1
        if has_res:
            res_ref = refs[pos]
            pos += 1
        if has_rms:
            x_ref, g_ref, dres_ref = refs[pos:pos + 3]
            pos += 3
        if has_norm:
            gain_ref = refs[pos]
            pos += 1
        o_ref = refs[pos]
        pos += 1
        if has_rms:
            dg_ref = refs[pos]
            pos += 1
        if has_norm:
            h_ref = refs[pos]
            pos += 1
        acc_ref = refs[pos] if nk > 1 else None
        first_rows = pl.program_id(0) == 0
        if has_norm:
            av = _rms_rows(a_ref[...], gain_ref[...])
            h_ref[...] = av
        else:
            av = a_ref[...].astype(_MXU)
        p = _dot(av, b_ref[...].astype(_MXU), 0 if ta else 1, 1 if tb else 0)

        def finish(acc):
            if has_bias:
                acc = acc + bias_ref[...]
            if has_res:
                acc = acc + res_ref[...]
            if has_rms:
                @pl.when(first_rows)
                def _():
                    dg_ref[...] = jnp.zeros_like(dg_ref)

                xv = x_ref[...]
                r = lax.rsqrt(jnp.mean(xv * xv, axis=-1, keepdims=True) + _EPS)
                xh = xv * r
                dg_ref[...] += jnp.sum(acc * xh, axis=0, keepdims=True)
                dxn = acc * g_ref[...]
                acc = dres_ref[...] + r * (dxn - xh * jnp.mean(dxn * xh, axis=-1, keepdims=True))
            o_ref[...] = acc.astype(o_ref.dtype)

        if nk == 1:
            finish(p)
        else:
            k = pl.program_id(2)

            @pl.when(k == 0)
            def _():
                acc_ref[...] = p

            @pl.when(k > 0)
            def _():
                acc_ref[...] += p

            @pl.when(k == nk - 1)
            def _():
                finish(acc_ref[...])

    out_shape = [jax.ShapeDtypeStruct((M, N), out_dtype)]
    out_specs = [pl.BlockSpec((tm, tn), lambda i, j, k: (i, j))]
    if has_rms:
        out_shape.append(jax.ShapeDtypeStruct((1, N), _F32))
        out_specs.append(pl.BlockSpec((1, N), lambda i, j, k: (0, 0)))
    if has_norm:
        out_shape.append(jax.ShapeDtypeStruct((M, K), _MXU))
        out_specs.append(pl.BlockSpec((tm, K), lambda i, j, k: (i, 0)))
    outs, got = _fused_call(
        body, name=name, out_shape=out_shape, grid=grid, in_specs=in_specs, out_specs=out_specs,
        scratch_shapes=[pltpu.VMEM((tm, tn), _F32)] if nk > 1 else [],
        sem=("arbitrary",) * 3 if has_rms or has_norm else ("parallel", "parallel", "arbitrary"), args=args, xchg=xchg)
    out = tuple(outs) if has_rms or has_norm else outs[0]
    return out if xchg is None else (out, got)


def _rms_fwd(x, g, *, name, xchg=None):
    N, D = x.shape
    tr = _rows(N, 512)

    def body(x_ref, g_ref, o_ref):
        xv = x_ref[...]
        r = lax.rsqrt(jnp.mean(xv * xv, axis=-1, keepdims=True) + _EPS)
        o_ref[...] = (xv * r * g_ref[...]).astype(o_ref.dtype)

    (out,), got = _fused_call(
        body, name=name, out_shape=[jax.ShapeDtypeStruct((N, D), _MXU)], grid=(N // tr,),
        in_specs=[pl.BlockSpec((tr, D), lambda i: (i, 0)), pl.BlockSpec((1, D), lambda i: (0, 0))],
        out_specs=[pl.BlockSpec((tr, D), lambda i: (i, 0))], scratch_shapes=[], sem=("parallel",),
        args=(x, g.reshape(1, D)), xchg=xchg)
    return out if xchg is None else (out, got)


def _rms_bwd(x, g, dh, dres, *, name):
    N, D = x.shape
    tr = _rows(N, 256)
    has_res = dres is not None

    def body(*refs):
        if has_res:
            x_ref, g_ref, dh_ref, dres_ref, dx_ref, dg_ref = refs
        else:
            x_ref, g_ref, dh_ref, dg_ref = refs
        xv = x_ref[...]
        r = lax.rsqrt(jnp.mean(xv * xv, axis=-1, keepdims=True) + _EPS)
        xh = xv * r
        dhv = dh_ref[...].astype(_F32)

        @pl.when(pl.program_id(0) == 0)
        def _():
            dg_ref[...] = jnp.zeros_like(dg_ref)

        dg_ref[...] += jnp.sum(dhv * xh, axis=0, keepdims=True)
        if has_res:
            dxn = dhv * g_ref[...]
            dx = r * (dxn - xh * jnp.mean(dxn * xh, axis=-1, keepdims=True))
            dx_ref[...] = dres_ref[...] + dx

    row = pl.BlockSpec((tr, D), lambda i: (i, 0))
    vec = pl.BlockSpec((1, D), lambda i: (0, 0))
    if has_res:
        out_shape = (jax.ShapeDtypeStruct((N, D), _F32), jax.ShapeDtypeStruct((1, D), _F32))
        out_specs = (row, vec)
        in_specs, args = [row, vec, row, row], (x, g.reshape(1, D), dh, dres)
    else:
        out_shape = jax.ShapeDtypeStruct((1, D), _F32)
        out_specs = vec
        in_specs, args = [row, vec, row], (x, g.reshape(1, D), dh)
    return pl.pallas_call(
        body, name=name, out_shape=out_shape, grid=(N // tr,), in_specs=in_specs, out_specs=out_specs,
        compiler_params=_params("arbitrary"),
    )(*args)


def _colsum(a, *, name):
    M, C = a.shape
    tr = _rows(M, 512)

    def body(a_ref, o_ref):
        @pl.when(pl.program_id(0) == 0)
        def _():
            o_ref[...] = jnp.zeros_like(o_ref)

        o_ref[...] += jnp.sum(a_ref[...].astype(_F32), axis=0, keepdims=True)

    return pl.pallas_call(
        body, name=name, out_shape=jax.ShapeDtypeStruct((1, C), _F32), grid=(M // tr,),
        in_specs=[pl.BlockSpec((tr, C), lambda i: (i, 0))], out_specs=pl.BlockSpec((1, C), lambda i: (0, 0)),
        compiler_params=_params("arbitrary"),
    )(a)


def _final_loss(act, w_down, res, g, tgt, *, name):
    N, D = res.shape
    K = act.shape[1]
    tr = _rows(N, 512)

    def body(a_ref, w_ref, res_ref, g_ref, t_ref, loss_ref, dx_ref, dg_ref):
        xv = _dot(a_ref[...].astype(_MXU), w_ref[...].astype(_MXU), 1, 0) + res_ref[...]
        r = lax.rsqrt(jnp.mean(xv * xv, axis=-1, keepdims=True) + _EPS)
        xh = xv * r
        gv = g_ref[...]
        diff = xh * gv - t_ref[...]

        @pl.when(pl.program_id(0) == 0)
        def _():
            loss_ref[...] = jnp.zeros_like(loss_ref)
            dg_ref[...] = jnp.zeros_like(dg_ref)

        part = jnp.sum(jnp.sum(diff * diff, axis=1, keepdims=True), axis=0, keepdims=True) * (0.5 / D)
        loss_ref[...] += jnp.broadcast_to(part, loss_ref.shape)
        dy = diff * (1.0 / D)
        dg_ref[...] += jnp.sum(dy * xh, axis=0, keepdims=True)
        dxn = dy * gv
        dx_ref[...] = r * (dxn - xh * jnp.mean(dxn * xh, axis=-1, keepdims=True))

    row = pl.BlockSpec((tr, D), lambda i: (i, 0))
    vec = pl.BlockSpec((1, D), lambda i: (0, 0))
    return pl.pallas_call(
        body, name=name,
        out_shape=(jax.ShapeDtypeStruct((8, _LANES), _F32), jax.ShapeDtypeStruct((N, D), _F32), jax.ShapeDtypeStruct((1, D), _F32)),
        grid=(N // tr,),
        in_specs=[pl.BlockSpec((tr, K), lambda i: (i, 0)), pl.BlockSpec((None, K, D), lambda i: (0, 0, 0)), row, vec, row],
        out_specs=(pl.BlockSpec((8, _LANES), lambda i: (0, 0)), row, vec),
        compiler_params=_params("arbitrary"),
    )(act, w_down, res, g.reshape(1, D), tgt)


def _mm_gu(h, w_gu, l, *, name, bias=None, glu=False, keep=None, rms_fwd=None, tm=512, tn=1408, xchg=None):
    keep = _MXU if keep is None else keep
    N, K = h.shape
    H = w_gu.shape[-1] // 2
    tm, tn = _pick(N, tm), _pick(H, tn)
    nj = H // tn
    has_bias, has_norm = bias is not None, rms_fwd is not None

    def body(*refs):
        h_ref, wp_ref, wq_ref = refs[:3]
        pair_ref, act_ref = refs[3 + 2 * has_bias + has_norm:][:2]
        if has_norm:
            hv = _rms_rows(h_ref[...], refs[3 + 2 * has_bias][...])
            refs[-1][...] = hv
        else:
            hv = h_ref[...].astype(_MXU)
        p = _dot(hv, wp_ref[...].astype(_MXU), 1, 0)
        q = _dot(hv, wq_ref[...].astype(_MXU), 1, 0)
        if has_bias:
            p = p + refs[3][...]
            q = q + refs[4][...]
        pair_ref[0] = p.astype(pair_ref.dtype)
        pair_ref[1] = q.astype(pair_ref.dtype)
        act_ref[...] = (p * _sigmoid(q) if glu else p * _sigmoid(p) * q).astype(act_ref.dtype)

    in_specs = [pl.BlockSpec((tm, K), lambda i, j: (i, 0)), pl.BlockSpec((None, K, tn), lambda i, j: (l, 0, j)),
                pl.BlockSpec((None, K, tn), lambda i, j: (l, 0, j + nj))]
    args = [h, w_gu, w_gu]
    if has_bias:
        b2 = bias.reshape(1, 2 * H).astype(_F32)
        in_specs += [pl.BlockSpec((1, tn), lambda i, j: (0, j)), pl.BlockSpec((1, tn), lambda i, j: (0, j + nj))]
        args += [b2, b2]
    out_shape = [jax.ShapeDtypeStruct((2, N, H), keep), jax.ShapeDtypeStruct((N, H), keep)]
    out_specs = [pl.BlockSpec((2, tm, tn), lambda i, j: (0, i, j)), pl.BlockSpec((tm, tn), lambda i, j: (i, j))]
    if has_norm:
        in_specs.append(pl.BlockSpec((1, K), lambda i, j: (0, 0)))
        args.append(rms_fwd.reshape(1, K))
        out_shape.append(jax.ShapeDtypeStruct((N, K), _MXU))
        out_specs.append(pl.BlockSpec((tm, K), lambda i, j: (i, 0)))
    outs, got = _fused_call(
        body, name=name, out_shape=out_shape, grid=(N // tm, nj), in_specs=in_specs, out_specs=out_specs,
        scratch_shapes=[], sem=("arbitrary", "arbitrary") if has_norm else ("parallel", "parallel"), args=args, xchg=xchg)
    return (*outs, got)


def _mm_dgu(dx, w_down, l, gu, *, name, tm=512, tn=1408):
    N, K = dx.shape
    H = w_down.shape[-2]
    tm, tn = _pick(N, tm), _pick(H, tn)

    def body(dx_ref, w_ref, gu_ref, o_ref):
        d = _dot(dx_ref[...].astype(_MXU), w_ref[...].astype(_MXU), 1, 1)
        g, u = gu_ref[0].astype(_F32), gu_ref[1].astype(_F32)
        sg = _sigmoid(g)
        o_ref[0] = (d * u * (sg * (1.0 + g * (1.0 - sg)))).astype(o_ref.dtype)
        o_ref[1] = (d * (g * sg)).astype(o_ref.dtype)

    return pl.pallas_call(
        body, name=name, out_shape=jax.ShapeDtypeStruct((2, N, H), _MXU), grid=(N // tm, H // tn),
        in_specs=[pl.BlockSpec((tm, K), lambda i, j: (i, 0)), pl.BlockSpec((None, tn, K), lambda i, j: (l, j, 0)),
                  pl.BlockSpec((2, tm, tn), lambda i, j: (0, i, j))],
        out_specs=pl.BlockSpec((2, tm, tn), lambda i, j: (0, i, j)), compiler_params=_params("parallel", "parallel"),
    )(dx, w_down, gu)


def _gmlp_mix(vb, w, trans):
    tr, W = vb.shape
    lane = lax.broadcasted_iota(jnp.int32, (_CHUNK, _LANES), 1)
    rows = []
    for c in range(tr // _CHUNK):
        tiles = []
        for j in range(W // _LANES):
            t = vb[c * _CHUNK:(c + 1) * _CHUNK, j * _LANES:(j + 1) * _LANES]
            ma = _dot(w[2 * j], t, 0 if trans else 1, 0)
            mb = _dot(w[2 * j + 1], t, 0 if trans else 1, 0)
            tiles.append(jnp.where(lane < _GRP, ma, mb))
        rows.append(jnp.concatenate(tiles, axis=1))
    return jnp.concatenate(rows, axis=0)


def _tril_w(w_ref):
    r = lax.broadcasted_iota(jnp.int32, (_CHUNK, _CHUNK), 0)
    c = lax.broadcasted_iota(jnp.int32, (_CHUNK, _CHUNK), 1)
    return jnp.where((r >= c)[None], w_ref[...], 0.0).astype(_MXU)


def _layernorm_stats(v):
    mu = jnp.mean(v, axis=-1, keepdims=True)
    xc = v - mu
    rstd = lax.rsqrt(jnp.mean(xc * xc, axis=-1, keepdims=True) + _EPS)
    return xc * rstd, rstd


def _gmlp_fwd(proj, ln_g, ln_b, w_s, bias_full, *, name):
    N = proj.shape[0]
    W = ln_g.shape[-1]
    G = w_s.shape[0]
    tr = _rows(N, 512)
    ub, vb_ = 3, 4

    def body(u_ref, v_ref, g_ref, b_ref, w_ref, bias_ref, o_ref):
        u = _gelu(u_ref[...])
        xh, _ = _layernorm_stats(_gelu(v_ref[...]))
        vgn = xh * g_ref[...] + b_ref[...]
        mixed = _gmlp_mix(vgn.astype(_MXU), _tril_w(w_ref), False)
        bias = jnp.concatenate([bias_ref[...]] * (tr // _CHUNK), axis=0)
        o_ref[...] = (u * (mixed + bias)).astype(o_ref.dtype)

    vec = pl.BlockSpec((1, W), lambda i: (0, 0))
    return pl.pallas_call(
        body, name=name, out_shape=jax.ShapeDtypeStruct((N, W), _MXU), grid=(N // tr,),
        in_specs=[pl.BlockSpec((tr, W), lambda i: (i, ub)), pl.BlockSpec((tr, W), lambda i: (i, vb_)), vec, vec,
                  pl.BlockSpec((G, _CHUNK, _CHUNK), lambda i: (0, 0, 0)), pl.BlockSpec((_CHUNK, W), lambda i: (0, 0))],
        out_specs=pl.BlockSpec((tr, W), lambda i: (i, 0)), compiler_params=_params("parallel"),
    )(proj, proj, ln_g.reshape(1, W), ln_b.reshape(1, W), w_s, bias_full)


def _gmlp_bwd(proj, da_src, da_blk, ln_g, ln_b, w_s, bias_full, *, name):
    N = proj.shape[0]
    W = ln_g.shape[-1]
    G = w_s.shape[0]
    tr = _rows(N, 512)
    nch = tr // _CHUNK

    def body(u_ref, v_ref, da_ref, g_ref, b_ref, w_ref, bias_ref, dz_ref, dg_ref, db_ref, dw_ref, dbias_ref):
        @pl.when(pl.program_id(0) == 0)
        def _():
            dg_ref[...] = jnp.zeros_like(dg_ref)
            db_ref[...] = jnp.zeros_like(db_ref)
            dw_ref[...] = jnp.zeros_like(dw_ref)
            dbias_ref[...] = jnp.zeros_like(dbias_ref)

        u_pre, v_pre = u_ref[...], v_ref[...]
        ug = _gelu(u_pre)
        xh, rstd = _layernorm_stats(_gelu(v_pre))
        lg = g_ref[...]
        vgn = xh * lg + b_ref[...]
        vb = vgn.astype(_MXU)
        wt = _tril_w(w_ref)
        mixed = _gmlp_mix(vb, wt, False)
        bias = jnp.concatenate([bias_ref[...]] * nch, axis=0)
        da = da_ref[...].astype(_F32)
        du = da * (mixed + bias)
        dm = da * ug
        dmb = dm.astype(_MXU)
        lane = lax.broadcasted_iota(jnp.int32, (_CHUNK, _LANES), 1)
        r = lax.broadcasted_iota(jnp.int32, (_CHUNK, _CHUNK), 0)
        c = lax.broadcasted_iota(jnp.int32, (_CHUNK, _CHUNK), 1)
        tril = r >= c
        dmsum = dm[0:_CHUNK]
        for ch in range(1, nch):
            dmsum = dmsum + dm[ch * _CHUNK:(ch + 1) * _CHUNK]
        dbias = jnp.zeros((_CHUNK, _LANES), _F32)
        for j in range(W // _LANES):
            tile = dmsum[:, j * _LANES:(j + 1) * _LANES]
            sa = jnp.sum(jnp.where(lane < _GRP, tile, 0.0), axis=1, keepdims=True)
            sb = jnp.sum(jnp.where(lane >= _GRP, tile, 0.0), axis=1, keepdims=True)
            dbias = dbias + jnp.where(lane == 2 * j, sa, 0.0) + jnp.where(lane == 2 * j + 1, sb, 0.0)
            acc_a = jnp.zeros((_CHUNK, _CHUNK), _F32)
            acc_b = jnp.zeros((_CHUNK, _CHUNK), _F32)
            for ch in range(nch):
                dt = dmb[ch * _CHUNK:(ch + 1) * _CHUNK, j * _LANES:(j + 1) * _LANES]
                vt = vb[ch * _CHUNK:(ch + 1) * _CHUNK, j * _LANES:(j + 1) * _LANES]
                acc_a = acc_a + _dot(jnp.where(lane < _GRP, dt, jnp.zeros_like(dt)), vt, 1, 1)
                acc_b = acc_b + _dot(jnp.where(lane >= _GRP, dt, jnp.zeros_like(dt)), vt, 1, 1)
            dw_ref[2 * j] += jnp.where(tril, acc_a, 0.0)
            dw_ref[2 * j + 1] += jnp.where(tril, acc_b, 0.0)
        dbias_ref[...] += dbias
        dvgn = _gmlp_mix(dmb, wt, True)
        dg_ref[...] += jnp.sum(dvgn * xh, axis=0, keepdims=True)
        db_ref[...] += jnp.sum(dvgn, axis=0, keepdims=True)
        dxh = dvgn * lg
        dvg = rstd * (dxh - jnp.mean(dxh, axis=-1, keepdims=True) - xh * jnp.mean(dxh * xh, axis=-1, keepdims=True))
        dz_ref[:, :W] = (du * _gelu_grad(u_pre)).astype(dz_ref.dtype)
        dz_ref[:, W:] = (dvg * _gelu_grad(v_pre)).astype(dz_ref.dtype)

    vec = pl.BlockSpec((1, W), lambda i: (0, 0))
    wspec = pl.BlockSpec((G, _CHUNK, _CHUNK), lambda i: (0, 0, 0))
    return pl.pallas_call(
        body, name=name,
        out_shape=(jax.ShapeDtypeStruct((N, 2 * W), _MXU), jax.ShapeDtypeStruct((1, W), _F32), jax.ShapeDtypeStruct((1, W), _F32),
                   jax.ShapeDtypeStruct((G, _CHUNK, _CHUNK), _F32), jax.ShapeDtypeStruct((_CHUNK, _LANES), _F32)),
        grid=(N // tr,),
        in_specs=[pl.BlockSpec((tr, W), lambda i: (i, 3)), pl.BlockSpec((tr, W), lambda i: (i, 4)),
                  pl.BlockSpec((tr, W), lambda i: (i, da_blk)), vec, vec, wspec, pl.BlockSpec((_CHUNK, W), lambda i: (0, 0))],
        out_specs=(pl.BlockSpec((tr, 2 * W), lambda i: (i, 0)), vec, vec, wspec, pl.BlockSpec((_CHUNK, _LANES), lambda i: (0, 0))),
        compiler_params=_params("arbitrary"),
    )(proj, proj, da_src, ln_g.reshape(1, W), ln_b.reshape(1, W), w_s, bias_full)


def _lane_cumsum(v):
    T = v.shape[1]
    lane = lax.broadcasted_iota(jnp.int32, (8, _LANES), 1)
    carry = jnp.zeros((8, 1), _F32)
    out = []
    for ch in range(T // _LANES):
        blk = v[:, ch * _LANES:(ch + 1) * _LANES]
        sh = 1
        while sh < _LANES:
            blk = blk + jnp.where(lane >= sh, pltpu.roll(blk, sh, 1), 0.0)
            sh *= 2
        blk = blk + carry
        carry = blk[:, _LANES - 1:_LANES]
        out.append(blk)
    return jnp.concatenate(out, axis=1), carry


def _log_sigmoid(x):
    return jnp.minimum(x, 0.0) - jnp.log(1.0 + jnp.exp(-jnp.abs(x)))


def _fox_cum(proj3, f_blk, f_bias, *, name):
    B, T, _ = proj3.shape
    H = f_bias.shape[-1]
    assert H == 8

    def body(f_ref, b_ref, o_ref):
        x = f_ref[0].T[0:8, :] + b_ref[...]
        cum, _ = _lane_cumsum(_log_sigmoid(x))
        o_ref[0] = cum

    return pl.pallas_call(
        body, name=name, out_shape=jax.ShapeDtypeStruct((B, 8, T), _F32), grid=(B,),
        in_specs=[pl.BlockSpec((1, T, _LANES), lambda b: (b, 0, f_blk)), pl.BlockSpec((8, 1), lambda b: (0, 0))],
        out_specs=pl.BlockSpec((1, 8, T), lambda b: (b, 0, 0)), compiler_params=_params("parallel"),
    )(proj3, f_bias.reshape(8, 1))


def _fox_cum_bwd(proj3, f_blk, f_bias, dcum, *, name):
    B, T, _ = proj3.shape

    def body(f_ref, b_ref, dc_ref, df_ref, dbias_ref):
        @pl.when(pl.program_id(0) == 0)
        def _():
            dbias_ref[...] = jnp.zeros_like(dbias_ref)

        x = f_ref[0].T[0:8, :] + b_ref[...]
        dc = dc_ref[0]
        incl, total = _lane_cumsum(dc)
        dlf = total - incl + dc
        df = dlf * _sigmoid(-x)
        full = jnp.concatenate([df, jnp.zeros((_LANES - 8, T), _F32)], axis=0).T
        dbias_ref[...] += jnp.sum(full, axis=0, keepdims=True)
        df_ref[0] = full

    return pl.pallas_call(
        body, name=name,
        out_shape=(jax.ShapeDtypeStruct((B, T, _LANES), _F32), jax.ShapeDtypeStruct((1, _LANES), _F32)), grid=(B,),
        in_specs=[pl.BlockSpec((1, T, _LANES), lambda b: (b, 0, f_blk)), pl.BlockSpec((8, 1), lambda b: (0, 0)),
                  pl.BlockSpec((1, 8, T), lambda b: (b, 0, 0))],
        out_specs=(pl.BlockSpec((1, T, _LANES), lambda b: (b, 0, 0)), pl.BlockSpec((1, _LANES), lambda b: (0, 0))),
        compiler_params=_params("arbitrary"),
    )(proj3, f_bias.reshape(8, 1), dcum)


def _cum_row(cum_ref, h, start, size):
    blk = cum_ref[0, :, pl.ds(start, size)]
    sub = lax.broadcasted_iota(jnp.int32, (blk.shape[0], 1), 0)
    return jnp.sum(jnp.where(sub == h, blk, 0.0), axis=0, keepdims=True)


def _causal(tq, q0, k0):
    r = lax.broadcasted_iota(jnp.int32, (tq, tq), 0)
    c = lax.broadcasted_iota(jnp.int32, (tq, tq), 1)
    return (r + q0) >= (c + k0)


def _fused_call(body, *, name, out_shape, grid, in_specs, out_specs, scratch_shapes, sem, args, xchg):
    out_shape, in_specs, out_specs, scratch_shapes = list(out_shape), list(in_specs), list(out_specs), list(scratch_shapes)
    if xchg is None:
        res = pl.pallas_call(body, name=name, out_shape=out_shape, grid=grid, in_specs=in_specs, out_specs=out_specs,
                             scratch_shapes=scratch_shapes, compiler_params=_params(*sem))(*args)
        return list(res), []
    n_in, n_out, n_scr = len(in_specs), len(out_specs), len(scratch_shapes)

    def fused(*refs):
        ins, refs = refs[:n_in], refs[n_in:]
        xs, refs = refs[:xchg.n_src], refs[xchg.n_src:]
        outs, refs = refs[:n_out], refs[n_out:]
        xd, refs = refs[:xchg.n_dst], refs[xchg.n_dst:]
        scr, sems = refs[:n_scr], refs[n_scr:]
        first = last = None
        for d, g in enumerate(grid):
            i = pl.program_id(d)
            first = (i == 0) if first is None else first & (i == 0)
            last = (i == g - 1) if last is None else last & (i == g - 1)

        @pl.when(first)
        def _():
            xchg.start(xs, xd, sems)

        body(*ins, *outs, *scr)

        @pl.when(last)
        def _():
            xchg.finish(xs, xd, sems)

    res = pl.pallas_call(
        fused, name=name, out_shape=out_shape + xchg.out_shapes, grid=grid, in_specs=in_specs + xchg.in_specs,
        out_specs=out_specs + xchg.out_specs, scratch_shapes=scratch_shapes + xchg.scratch,
        compiler_params=_params(*["arbitrary"] * len(grid)),
    )(*args, *xchg.srcs)
    return list(res[:n_out]), list(res[n_out:])


def _fox_fwd(proj3, cum, *, name, xchg=None):
    B, T, _ = proj3.shape
    H = cum.shape[1]
    W = H * _FOX_HD
    npair = W // _LANES
    tq = _rows(T, _FOX_TQ)
    nq = T // tq

    def body(q_ref, k_ref, v_ref, cum_ref, o_ref, lse_ref):
        p = pl.program_id(1)
        i = pl.program_id(2)
        q0 = pl.multiple_of(i * tq, tq)
        lane = lax.broadcasted_iota(jnp.int32, (1, _LANES), 1)
        q2 = q_ref[0] * _FOX_SCALE
        heads = []
        for hh in range(2):
            msk = (lane < _FOX_HD) if hh == 0 else (lane >= _FOX_HD)
            h = 2 * p + hh
            heads.append((msk, h, jnp.where(msk, q2, 0.0).astype(_MXU), _cum_row(cum_ref, h, q0, _LANES)[:, 0:1]))

        def step(jj, carry, masked):
            k0 = pl.multiple_of(jj * tq, tq)
            k2 = k_ref[0, pl.ds(k0, tq), :].astype(_MXU)
            v2 = v_ref[0, pl.ds(k0, tq), :]
            out = []
            for (msk, h, qm, c0), (m_prev, l_prev, acc) in zip(heads, carry):
                s = _dot(qm, k2, 1, 1) + (c0 - _cum_row(cum_ref, h, k0, tq))
                if masked:
                    s = jnp.where(_causal(tq, q0, k0), s, -jnp.inf)
                m_new = jnp.maximum(m_prev, jnp.max(s, axis=1, keepdims=True))
                alpha = jnp.exp(m_prev - m_new)
                e = jnp.exp(s - m_new)
                l_new = alpha * l_prev + jnp.sum(e, axis=1, keepdims=True)
                vm = jnp.where(msk, v2, 0.0).astype(_MXU)
                out.append((m_new, l_new, alpha * acc + _dot(e.astype(_MXU), vm, 1, 0)))
            return tuple(out)

        init = tuple((jnp.full((tq, 1), -jnp.inf, _F32), jnp.zeros((tq, 1), _F32), jnp.zeros((tq, _LANES), _F32)) for _ in heads)
        carry = step(i, lax.fori_loop(0, i, functools.partial(step, masked=False), init), True)
        o2 = jnp.zeros((tq, _LANES), _F32)
        for hh, (m, l, acc) in enumerate(carry):
            o2 = o2 + acc / l
            lse_ref[0, hh] = jnp.broadcast_to(m + jnp.log(l), (tq, _LANES))
        o_ref[0] = o2.astype(o_ref.dtype)

    (o, lse), got = _fused_call(
        body, name=name,
        out_shape=(jax.ShapeDtypeStruct((B, T, W), _MXU), jax.ShapeDtypeStruct((B, H, T, _LANES), _F32)),
        grid=(B, npair, nq),
        in_specs=[pl.BlockSpec((1, tq, _LANES), lambda b, p, i: (b, i, p)),
                  pl.BlockSpec((1, T, _LANES), lambda b, p, i: (b, 0, npair + p)),
                  pl.BlockSpec((1, T, _LANES), lambda b, p, i: (b, 0, 2 * npair + p)),
                  pl.BlockSpec((1, H, T), lambda b, p, i: (b, 0, 0))],
        out_specs=(pl.BlockSpec((1, tq, _LANES), lambda b, p, i: (b, i, p)),
                   pl.BlockSpec((1, 2, tq, _LANES), lambda b, p, i: (b, p, i, 0))),
        scratch_shapes=[], sem=("parallel", "parallel", "parallel"), args=(proj3, proj3, proj3, cum), xchg=xchg)
    return o, lse, got


def _fox_bwd(proj3, cum, do3, lse, *, name, xchg=None):
    B, T, _ = proj3.shape
    H = cum.shape[1]
    W = H * _FOX_HD
    npair = W // _LANES
    tq = _rows(T, _FOX_TQ)
    nq = T // tq

    def body(q_ref, k_ref, v_ref, cum_ref, do_ref, lse_ref, dq_ref, dk_ref, dv_ref, dc_ref, p_scr, dp_scr, dk_acc, dv_acc, dc_acc):
        p = pl.program_id(1)
        i = pl.program_id(2)
        q0 = pl.multiple_of(i * tq, tq)
        lane = lax.broadcasted_iota(jnp.int32, (1, _LANES), 1)

        @pl.when(i == 0)
        def _():
            dk_acc[...] = jnp.zeros_like(dk_acc)
            dv_acc[...] = jnp.zeros_like(dv_acc)
            dc_acc[...] = jnp.zeros_like(dc_acc)

        q2 = q_ref[0] * _FOX_SCALE
        do2 = do_ref[0].astype(_F32)
        heads = []
        for hh in range(2):
            msk = (lane < _FOX_HD) if hh == 0 else (lane >= _FOX_HD)
            h = 2 * p + hh
            heads.append((hh, msk, h, jnp.where(msk, q2, 0.0).astype(_MXU), jnp.where(msk, do2, 0.0).astype(_MXU),
                          _cum_row(cum_ref, h, q0, _LANES)[:, 0:1], lse_ref[0, hh][:, 0:1]))

        def first(jj, deltas, masked):
            k0 = pl.multiple_of(jj * tq, tq)
            kb = k_ref[0, pl.ds(k0, tq), :].astype(_MXU)
            vb = v_ref[0, pl.ds(k0, tq), :].astype(_MXU)
            out = []
            for (hh, _, h, qm, dom, c0, lse_h), delta in zip(heads, deltas):
                s = _dot(qm, kb, 1, 1) + (c0 - _cum_row(cum_ref, h, k0, tq))
                pr = jnp.exp(s - lse_h)
                if masked:
                    pr = jnp.where(_causal(tq, q0, k0), pr, 0.0)
                dp = _dot(dom, vb, 1, 1)
                p_scr[hh, jj] = pr
                dp_scr[hh, jj] = dp
                out.append(delta + jnp.sum(pr * dp, axis=1, keepdims=True))
            return tuple(out)

        zero = tuple(jnp.zeros((tq, 1), _F32) for _ in heads)
        deltas = first(i, lax.fori_loop(0, i, functools.partial(first, masked=False), zero), True)

        def second(jj, dq):
            k0 = pl.multiple_of(jj * tq, tq)
            k2 = k_ref[0, pl.ds(k0, tq), :]
            dk = jnp.zeros((tq, _LANES), _F32)
            dv = jnp.zeros((tq, _LANES), _F32)
            for (hh, msk, _, qm, dom, _, _), delta in zip(heads, deltas):
                pr = p_scr[hh, jj]
                ds = pr * (dp_scr[hh, jj] - delta)
                dsb = ds.astype(_MXU)
                dv = dv + _dot(pr.astype(_MXU), dom, 0, 0)
                dk = dk + _dot(dsb, qm, 0, 0)
                dc_acc[hh:hh + 1, pl.ds(k0, tq)] += jnp.sum(ds, axis=0, keepdims=True)
                dq = dq + _dot(dsb, jnp.where(msk, k2, 0.0).astype(_MXU), 1, 0)
            dv_acc[pl.ds(k0, tq), :] += dv
            dk_acc[pl.ds(k0, tq), :] += dk
            return dq

        dq2 = lax.fori_loop(0, i + 1, second, jnp.zeros((tq, _LANES), _F32))
        dq_ref[0] = (dq2 * _FOX_SCALE).astype(dq_ref.dtype)

        @pl.when(i == nq - 1)
        def _():
            dk_ref[0] = dk_acc[...].astype(dk_ref.dtype)
            dv_ref[0] = dv_acc[...].astype(dv_ref.dtype)
            dc_ref[0, 0] = -dc_acc[...]

    full = lambda blk: pl.BlockSpec((1, T, _LANES), lambda b, p, i, blk=blk: (b, 0, blk * npair + p))
    part = lambda blk: pl.BlockSpec((1, tq, _LANES), lambda b, p, i, blk=blk: (b, i, blk * npair + p))
    (dq, dk, dv, dcum), got = _fused_call(
        body, name=name,
        out_shape=(jax.ShapeDtypeStruct((B, T, W), _MXU), jax.ShapeDtypeStruct((B, T, W), _MXU),
                   jax.ShapeDtypeStruct((B, T, W), _MXU), jax.ShapeDtypeStruct((B, npair, 2, T), _F32)),
        grid=(B, npair, nq),
        in_specs=[part(0), full(1), full(2), pl.BlockSpec((1, H, T), lambda b, p, i: (b, 0, 0)), part(0),
                  pl.BlockSpec((1, 2, tq, _LANES), lambda b, p, i: (b, p, i, 0))],
        out_specs=(part(0), full(0), full(0), pl.BlockSpec((1, 1, 2, T), lambda b, p, i: (b, p, 0, 0))),
        scratch_shapes=[pltpu.VMEM((2, nq, tq, tq), _F32), pltpu.VMEM((2, nq, tq, tq), _F32), pltpu.VMEM((T, _LANES), _F32),
                        pltpu.VMEM((T, _LANES), _F32), pltpu.VMEM((2, T), _F32)],
        sem=("parallel", "parallel", "arbitrary"), args=(proj3, proj3, proj3, cum, do3, lse), xchg=xchg)
    return dq, dk, dv, dcum, got


def _xa_probs(qh, kh, scale):
    s = _dot(qh, kh, 1, 1) * scale
    e = jnp.exp(s - jnp.max(s, axis=1, keepdims=True))
    return e / jnp.sum(e, axis=1, keepdims=True)


def _xa_fwd(q3, kv3, *, name):
    B, T, D = q3.shape
    M = kv3.shape[1]
    hd = D // _XA_HEADS
    scale = hd ** -0.5
    tq = _rows(T, 512)

    def body(q_ref, kv_ref, o_ref):
        for h in range(_XA_HEADS):
            sl = slice(h * hd, (h + 1) * hd)
            p = _xa_probs(q_ref[0, :, sl], kv_ref[0, :, sl], scale)
            o_ref[0, :, sl] = _dot(p.astype(_MXU), kv_ref[0, :, D + h * hd:D + (h + 1) * hd], 1, 0).astype(o_ref.dtype)

    return pl.pallas_call(
        body, name=name, out_shape=jax.ShapeDtypeStruct((B, T, D), _MXU), grid=(B, T // tq),
        in_specs=[pl.BlockSpec((1, tq, D), lambda b, i: (b, i, 0)), pl.BlockSpec((1, M, 2 * D), lambda b, i: (b, 0, 0))],
        out_specs=pl.BlockSpec((1, tq, D), lambda b, i: (b, i, 0)), compiler_params=_params("parallel", "parallel"),
    )(q3, kv3)


def _xa_bwd(q3, kv3, do3, *, name):
    B, T, D = q3.shape
    M = kv3.shape[1]
    hd = D // _XA_HEADS
    scale = hd ** -0.5
    tq = _rows(T, 512)

    def body(q_ref, kv_ref, do_ref, dq_ref, dkv_ref):
        @pl.when(pl.program_id(1) == 0)
        def _():
            dkv_ref[...] = jnp.zeros_like(dkv_ref)

        for h in range(_XA_HEADS):
            sl = slice(h * hd, (h + 1) * hd)
            slv = slice(D + h * hd, D + (h + 1) * hd)
            qh, kh, vh, doh = q_ref[0, :, sl], kv_ref[0, :, sl], kv_ref[0, :, slv], do_ref[0, :, sl]
            p = _xa_probs(qh, kh, scale)
            dkv_ref[0, :, slv] += _dot(p.astype(_MXU), doh, 0, 0)
            dp = _dot(doh, vh, 1, 1)
            ds = (p * (dp - jnp.sum(p * dp, axis=1, keepdims=True))).astype(_MXU)
            dq_ref[0, :, sl] = (_dot(ds, kh, 1, 0) * scale).astype(dq_ref.dtype)
            dkv_ref[0, :, sl] += _dot(ds, qh, 0, 0) * scale

    blk = pl.BlockSpec((1, tq, D), lambda b, i: (b, i, 0))
    kvs = pl.BlockSpec((1, M, 2 * D), lambda b, i: (b, 0, 0))
    return pl.pallas_call(
        body, name=name,
        out_shape=(jax.ShapeDtypeStruct((B, T, D), _MXU), jax.ShapeDtypeStruct((B, M, 2 * D), _F32)),
        grid=(B, T // tq), in_specs=[blk, kvs, blk], out_specs=(blk, kvs),
        compiler_params=_params("parallel", "arbitrary"),
    )(q3, kv3, do3)


def _rotated_copies(ext, rot, tt):
    rot[0] = ext[...]
    for b in range(1, 8):
        rot[b, 0:tt + _HALO - 8, :] = ext[b:b + tt + _HALO - 8, :]


def _shifted(rot, off, r0, rows, c0):
    a, b = divmod(off, 8)
    return rot[b, 8 * a + r0:8 * a + r0 + rows, c0:c0 + _LANES]


def _conv_fwd(y3, dw_w, dw_b, ln_g, ln_b, *, name, xchg=None):
    B, T, C = y3.shape
    tt = _rows(T, 256)
    nt = T // tt

    def body(prev_ref, cur_ref, w_ref, b_ref, g_ref, lb_ref, y2_ref, y4_ref, ext, rot):
        i = pl.program_id(1)
        ext[0:_HALO, :] = jnp.where(i > 0, prev_ref[0, tt - _HALO:tt, :], 0.0)
        ext[_HALO:_HALO + tt, :] = cur_ref[0]
        _rotated_copies(ext, rot, tt)
        for c0 in range(0, C, _LANES):
            acc = jnp.broadcast_to(b_ref[:, c0:c0 + _LANES], (tt, _LANES))
            for j in range(_CONV_K):
                acc = acc + w_ref[j:j + 1, c0:c0 + _LANES] * _shifted(rot, _HALO - (_CONV_K - 1) + j, 0, tt, c0)
            y2_ref[0, :, c0:c0 + _LANES] = acc
        xh, _ = _layernorm_stats(y2_ref[0])
        z = xh * g_ref[...] + lb_ref[...]
        y4_ref[0] = (z * _sigmoid(z)).astype(y4_ref.dtype)

    vec = pl.BlockSpec((1, C), lambda b, i: (0, 0))
    blk = pl.BlockSpec((1, tt, C), lambda b, i: (b, i, 0))
    (y2, y4), got = _fused_call(
        body, name=name,
        out_shape=(jax.ShapeDtypeStruct((B, T, C), _F32), jax.ShapeDtypeStruct((B, T, C), _MXU)),
        grid=(B, nt),
        in_specs=[pl.BlockSpec((1, tt, C), lambda b, i: (b, jnp.maximum(i - 1, 0), 0)), blk,
                  pl.BlockSpec((_HALO, C), lambda b, i: (0, 0)), vec, vec, vec],
        out_specs=(blk, blk),
        scratch_shapes=[pltpu.VMEM((tt + _HALO, C), _F32), pltpu.VMEM((8, tt + _HALO, C), _F32)],
        sem=("parallel", "parallel"), args=(y3, y3, dw_w, dw_b.reshape(1, C), ln_g.reshape(1, C), ln_b.reshape(1, C)), xchg=xchg)
    return y2, y4, got


def _conv_ln_bwd(y2, dx, w_out, ln_g, ln_b, *, name):
    N, C = y2.shape
    D = dx.shape[1]
    tr = _rows(N, 512)

    def body(y_ref, dx_ref, w_ref, g_ref, b_ref, dy_ref, dg_ref, db_ref, dwb_ref):
        @pl.when(pl.program_id(0) == 0)
        def _():
            dg_ref[...] = jnp.zeros_like(dg_ref)
            db_ref[...] = jnp.zeros_like(db_ref)
            dwb_ref[...] = jnp.zeros_like(dwb_ref)

        dy4 = _dot(dx_ref[...].astype(_MXU), w_ref[...].astype(_MXU), 1, 1)
        xh, rstd = _layernorm_stats(y_ref[...])
        gv = g_ref[...]
        z = xh * gv + b_ref[...]
        sg = _sigmoid(z)
        dz = dy4 * (sg * (1.0 + z * (1.0 - sg)))
        dg_ref[...] += jnp.sum(dz * xh, axis=0, keepdims=True)
        db_ref[...] += jnp.sum(dz, axis=0, keepdims=True)
        dxh = dz * gv
        dy = rstd * (dxh - jnp.mean(dxh, axis=-1, keepdims=True) - xh * jnp.mean(dxh * xh, axis=-1, keepdims=True))
        dwb_ref[...] += jnp.sum(dy, axis=0, keepdims=True)
        dy_ref[...] = dy

    row = pl.BlockSpec((tr, C), lambda i: (i, 0))
    vec = pl.BlockSpec((1, C), lambda i: (0, 0))
    v = jax.ShapeDtypeStruct((1, C), _F32)
    return pl.pallas_call(
        body, name=name, out_shape=(jax.ShapeDtypeStruct((N, C), _F32), v, v, v), grid=(N // tr,),
        in_specs=[row, pl.BlockSpec((tr, D), lambda i: (i, 0)), pl.BlockSpec((None, C, D), lambda i: (0, 0, 0)), vec, vec],
        out_specs=(row, vec, vec, vec), compiler_params=_params("arbitrary"),
    )(y2, dx, w_out, ln_g.reshape(1, C), ln_b.reshape(1, C))


def _conv_bwd(y3, dy23, ag3, dw_w, *, name, xchg=None):
    B, T, C = y3.shape
    tt = _rows(T, 256)
    nt = T // tt

    rs = _rows(tt, _CONV_ROWS)

    def groups(v):
        return jnp.sum(v.reshape(rs // 8, 8, _LANES), axis=0)

    def body(yp_ref, yc_ref, dc_ref, dn_ref, a_ref, g_ref, w_ref, dag_ref, dw_ref, dbin_ref, yext, dext, yrot, drot, dw_acc, db_acc):
        b = pl.program_id(0)
        i = pl.program_id(1)

        @pl.when((b == 0) & (i == 0))
        def _():
            dw_acc[...] = jnp.zeros_like(dw_acc)
            db_acc[...] = jnp.zeros_like(db_acc)

        yext[0:_HALO, :] = jnp.where(i > 0, yp_ref[0, tt - _HALO:tt, :], 0.0)
        yext[_HALO:_HALO + tt, :] = yc_ref[0]
        dext[0:tt, :] = dc_ref[0]
        dext[tt:tt + _HALO, :] = jnp.where(i < nt - 1, dn_ref[0, 0:_HALO, :], 0.0)
        _rotated_copies(yext, yrot, tt)
        _rotated_copies(dext, drot, tt)
        for c0 in range(0, C, _LANES):
            for r0 in range(0, tt, rs):
                d_cur = dext[r0:r0 + rs, c0:c0 + _LANES]
                dy = jnp.zeros((rs, _LANES), _F32)
                for j in range(_CONV_K):
                    sh = _CONV_K - 1 - j
                    dy = dy + w_ref[j:j + 1, c0:c0 + _LANES] * _shifted(drot, sh, r0, rs, c0)
                    dw_acc[j, :, c0:c0 + _LANES] += groups(d_cur * _shifted(yrot, _HALO - sh, r0, rs, c0))
                a, g = a_ref[0, r0:r0 + rs, c0:c0 + _LANES], g_ref[0, r0:r0 + rs, c0:c0 + _LANES]
                sg = _sigmoid(g)
                da = dy * sg
                dg = dy * a * (sg * (1.0 - sg))
                dag_ref[0, r0:r0 + rs, c0:c0 + _LANES] = da.astype(dag_ref.dtype)
                dag_ref[0, r0:r0 + rs, C + c0:C + c0 + _LANES] = dg.astype(dag_ref.dtype)
                db_acc[:, c0:c0 + _LANES] += groups(da)
                db_acc[:, C + c0:C + c0 + _LANES] += groups(dg)

        @pl.when((b == B - 1) & (i == nt - 1))
        def _():
            dw_ref[...] = jnp.sum(dw_acc[...], axis=1)
            dbin_ref[...] = jnp.sum(db_acc[...], axis=0, keepdims=True)

    blk = pl.BlockSpec((1, tt, C), lambda b, i: (b, i, 0))
    (dag, ddw, dbin), got = _fused_call(
        body, name=name,
        out_shape=(jax.ShapeDtypeStruct((B, T, 2 * C), _MXU), jax.ShapeDtypeStruct((_HALO, C), _F32),
                   jax.ShapeDtypeStruct((1, 2 * C), _F32)),
        grid=(B, nt),
        in_specs=[pl.BlockSpec((1, tt, C), lambda b, i: (b, jnp.maximum(i - 1, 0), 0)), blk, blk,
                  pl.BlockSpec((1, tt, C), lambda b, i: (b, jnp.minimum(i + 1, nt - 1), 0)),
                  pl.BlockSpec((None, 1, tt, C), lambda b, i: (0, b, i, 0)), pl.BlockSpec((None, 1, tt, C), lambda b, i: (1, b, i, 0)),
                  pl.BlockSpec((_HALO, C), lambda b, i: (0, 0))],
        out_specs=(pl.BlockSpec((1, tt, 2 * C), lambda b, i: (b, i, 0)), pl.BlockSpec((_HALO, C), lambda b, i: (0, 0)),
                   pl.BlockSpec((1, 2 * C), lambda b, i: (0, 0))),
        scratch_shapes=[pltpu.VMEM((tt + _HALO, C), _F32), pltpu.VMEM((tt + _HALO, C), _F32),
                        pltpu.VMEM((8, tt + _HALO, C), _F32), pltpu.VMEM((8, tt + _HALO, C), _F32),
                        pltpu.VMEM((_HALO, 8, C), _F32), pltpu.VMEM((8, 2 * C), _F32)],
        sem=("arbitrary", "arbitrary"), args=(y3, y3, dy23, dy23, ag3, ag3, dw_w), xchg=xchg)
    return dag, ddw, dbin, got


class _Exchange:
    def __init__(self, items):
        self.per_peer = [pp for _, pp in items]
        self.srcs, self.out_shapes, self.pieces = [], [], []
        for t, (srcs, per_peer) in enumerate(items):
            blk = srcs[0].shape[1:] if per_peer else srcs[0].shape
            self.out_shapes.append(jax.ShapeDtypeStruct((len(srcs), _N_DEV) + tuple(blk), srcs[0].dtype))
            for l, s in enumerate(srcs):
                self.pieces.append((t, l, len(self.srcs)))
                self.srcs.append(s)
        self.n_src, self.n_dst, n_pc = len(self.srcs), len(items), len(self.pieces)
        self.in_specs = [pl.BlockSpec(memory_space=pl.ANY)] * self.n_src
        self.out_specs = [pl.BlockSpec(memory_space=pl.ANY)] * self.n_dst
        self.scratch = [pltpu.SemaphoreType.DMA((n_pc, _N_DEV - 1)), pltpu.SemaphoreType.DMA((n_pc, _N_DEV - 1)),
                        pltpu.SemaphoreType.DMA((n_pc,))]

    def _copies(self, src_refs, dst_refs, sems, kind):
        send_sems, recv_sems, loc_sems = sems
        x, y, c = lax.axis_index("x"), lax.axis_index("y"), lax.axis_index("c")
        me = 4 * x + 2 * y + c
        out = []
        for i, (t, l, s) in enumerate(self.pieces):
            def src_for(p, s=s, t=t):
                return src_refs[s].at[p] if self.per_peer[t] else src_refs[s]

            if kind == "local":
                out.append(pltpu.make_async_copy(src_for(me), dst_refs[t].at[l, me], loc_sems.at[i]))
                continue
            for k in range(1, _N_DEV):
                px, py, pc = (1 - x if k & 4 else x), (1 - y if k & 2 else y), (1 - c if k & 1 else c)
                p = 4 * px + 2 * py + pc
                out.append(pltpu.make_async_remote_copy(
                    src_ref=src_for(p), dst_ref=dst_refs[t].at[l, p if kind == "recv" else me],
                    send_sem=send_sems.at[i, k - 1], recv_sem=recv_sems.at[i, k - 1],
                    device_id=(px, py, pc), device_id_type=pl.DeviceIdType.MESH))
        return out

    def start(self, src_refs, dst_refs, sems):
        for cp in self._copies(src_refs, dst_refs, sems, "local") + self._copies(src_refs, dst_refs, sems, "send"):
            cp.start()

    def finish(self, src_refs, dst_refs, sems):
        for cp in self._copies(src_refs, dst_refs, sems, "send"):
            cp.wait_send()
        for cp in self._copies(src_refs, dst_refs, sems, "recv"):
            cp.wait_recv()
        for cp in self._copies(src_refs, dst_refs, sems, "local"):
            cp.wait()


def _exchange(items, *, name):
    ex = _Exchange(items)

    def body(*refs):
        parts = refs[:ex.n_src], refs[ex.n_src:ex.n_src + ex.n_dst], refs[ex.n_src + ex.n_dst:]
        ex.start(*parts)
        ex.finish(*parts)

    return pl.pallas_call(
        body, name=name, out_shape=ex.out_shapes, in_specs=ex.in_specs, out_specs=ex.out_specs, scratch_shapes=ex.scratch,
        compiler_params=pltpu.CompilerParams(has_side_effects=True),
    )(*ex.srcs)


def _adam_update(g, w, m, v):
    c1 = 1.0 / (1.0 - _ADAM_B1 ** _ADAM_STEP)
    c2 = 1.0 / (1.0 - _ADAM_B2 ** _ADAM_STEP)
    m2 = _ADAM_B1 * m + (1.0 - _ADAM_B1) * g
    v2 = _ADAM_B2 * v + (1.0 - _ADAM_B2) * (g * g)
    return -_ADAM_LR * ((m2 * c1) / (jnp.sqrt(v2 * c2) + _ADAM_EPS) + _ADAM_WD * w), m2, v2


def _adamw_big(recvs, w, m, v, *, name):
    L, R, C = w.shape
    tr = _rows(R, 256)
    nb = R // tr

    def body(*refs):
        r_refs = refs[:L]
        w_ref, m_ref, v_ref, g_ref, d_ref, mo_ref, vo_ref = refs[L:]
        for l in range(L):
            @pl.when(pl.program_id(0) == l)
            def _(r_ref=r_refs[l]):
                g = r_ref[0, 0].astype(_F32)
                for k in range(1, _N_DEV):
                    g = g + r_ref[0, k].astype(_F32)
                g_ref[0] = g
                d_ref[0], mo_ref[0], vo_ref[0] = _adam_update(g, w_ref[0], m_ref[0], v_ref[0])

    def recv_spec(l):
        return pl.BlockSpec((1, _N_DEV, tr, C), lambda ll, i: (0, 0, jnp.where(ll == l, i, jnp.where(ll < l, 0, nb - 1)), 0))

    blk = pl.BlockSpec((1, tr, C), lambda l, i: (l, i, 0))
    o = jax.ShapeDtypeStruct((L, R, C), _F32)
    return pl.pallas_call(
        body, name=name, out_shape=(o, o, o, o), grid=(L, nb),
        in_specs=[recv_spec(l) for l in range(L)] + [blk, blk, blk], out_specs=(blk, blk, blk, blk),
        compiler_params=_params("arbitrary", "arbitrary"),
    )(*recvs, w, m, v)


def _adamw_small(tensors, *, name):
    n = len(tensors)
    lanes = [t[4] for t in tensors]
    layers = [len(t[0]) for t in tensors]

    def body(*refs):
        pos = 0
        ins = []
        for t in range(n):
            ins.append((refs[pos:pos + layers[t]], *refs[pos + layers[t]:pos + layers[t] + 3]))
            pos += layers[t] + 3
        outs = refs[pos:]
        for t in range(n):
            r_refs, w_ref, m_ref, v_ref = ins[t]
            g_ref, d_ref, mo_ref, vo_ref = outs[4 * t:4 * t + 4]
            for l in range(layers[t]):
                g = r_refs[l][0, 0]
                for k in range(1, _N_DEV):
                    g = g + r_refs[l][0, k]
                if lanes[t] is not None:
                    g = g[..., :lanes[t]]
                g_ref[l] = g
                d_ref[l], mo_ref[l], vo_ref[l] = _adam_update(g, w_ref[l], m_ref[l], v_ref[l])

    args, out_shape = [], []
    for recvs, w, m, v, _ in tensors:
        args += [*recvs, w, m, v]
        out_shape += [jax.ShapeDtypeStruct(w.shape, _F32)] * 4
    outs = pl.pallas_call(
        body, name=name, out_shape=out_shape,
        in_specs=[pl.BlockSpec(memory_space=pltpu.VMEM)] * len(args), out_specs=[pl.BlockSpec(memory_space=pltpu.VMEM)] * len(out_shape),
        compiler_params=_params(),
    )(*args)
    return [tuple(outs[4 * t:4 * t + 4]) for t in range(n)]


_BIG = (("w_in_e", 2), ("w_out_e", 1), ("conv_w_in", 2), ("conv_w_out", 1), ("xa_wq", 1), ("xa_wkv", 2), ("xa_wo", 1),
        ("ffn_w_gu", 2), ("ffn_w_down", 1))
_SMALL_SHARDED = (("mix_norm_o", 1), ("conv_b_in", 1), ("conv_dw_w", 2), ("conv_dw_b", 1), ("conv_ln_g", 1),
                  ("conv_ln_b", 1), ("conv_b_out", 1))
_REPLICATED = ("mix_norm_e", "fox_f_bias", "gmlp_ln_g", "gmlp_ln_b", "gmlp_w_s", "gmlp_b_s", "xa_norm", "mem_norm",
               "ffn_norm", "final_norm")
_WEIGHTS = ("mix_norm_e", "w_in_e", "fox_f_bias", "gmlp_ln_g", "gmlp_ln_b", "gmlp_w_s", "gmlp_b_s", "w_out_e", "mix_norm_o",
            "conv_w_in", "conv_b_in", "conv_dw_w", "conv_dw_b", "conv_ln_g", "conv_ln_b", "conv_w_out", "conv_b_out",
            "xa_norm", "mem_norm", "xa_wq", "xa_wkv", "xa_wo", "ffn_norm", "ffn_w_gu", "ffn_w_down", "final_norm")


def _cols_to_peers(g, n=_N_DEV):
    K, N = g.shape[-2:]
    return jnp.swapaxes(g.reshape(g.shape[:-1] + (n, N // n)), -3, -2)


def _weight_items(pieces, wsrc):
    return [([wsrc[n][l]] if n in dict(_BIG) else [wsrc[n]], False) for n, l in pieces]


def _place_weights(P, pieces, gathered):
    axis = dict(_BIG + _SMALL_SHARDED)
    for (n, l), g in zip(pieces, gathered):
        if n in dict(_BIG):
            P.setdefault(n, {})[l] = g.reshape(1, -1, g.shape[-1]) if axis[n] == 1 else _peers_to_cols(g)
        else:
            P[n] = _peers_to_cols(g[0, :, 0])[None] if axis[n] == 2 else g.reshape(1, -1)


def _grad_items(pieces, G):
    axis = dict(_BIG + _SMALL_SHARDED)
    items = []
    for n, l in pieces:
        g = G[n][l]
        if n in _REPLICATED:
            items.append(([g], False))
        elif n == "ffn_w_gu":
            half = _N_DEV // 2
            items.append(([jnp.concatenate([_cols_to_peers(g[0], half), _cols_to_peers(g[1], half)], axis=0)], True))
        elif n in dict(_BIG):
            items.append(([g.reshape(_N_DEV, -1, g.shape[-1]) if axis[n] == 1 else _cols_to_peers(g)], True))
        else:
            items.append(([_cols_to_peers(g) if axis[n] == 2 else g.reshape(_N_DEV, 1, -1)], True))
    return items


def _peers_to_cols(d):
    K, c = d.shape[-2:]
    return jnp.swapaxes(d, -3, -2).reshape(d.shape[:-3] + (K, _N_DEV * c))


def _local_step(x, mem, tgt, P, wsrc=None, fwd_hooks=None, bwd_hooks=None):
    fwd_hooks, bwd_hooks = fwd_hooks or {}, bwd_hooks or {}
    sent = {}

    def gather(kernel_name):
        return _Exchange(_weight_items(fwd_hooks[kernel_name], wsrc)) if kernel_name in fwd_hooks else None

    def placed(kernel_name, got):
        if kernel_name in fwd_hooks:
            _place_weights(P, fwd_hooks[kernel_name], got)

    def scatter(kernel_name):
        return _Exchange(_grad_items(bwd_hooks[kernel_name], G)) if kernel_name in bwd_hooks else None

    def received(kernel_name, got):
        if kernel_name in bwd_hooks:
            sent.update(zip(bwd_hooks[kernel_name], got))

    def mm(a, b, *, name, **kw):
        ex = gather(name) or scatter(name)
        out = _mm(a, b, name=name, xchg=ex, **kw)
        if ex is None:
            return out
        out, got = out
        placed(name, got)
        received(name, got)
        return out

    B, T, D = x.shape
    M = mem.shape[1]
    N = B * T
    W = D // 2
    H = W // _FOX_HD
    f_blk = 5 * W // _LANES
    G = {}
    x0 = x.reshape(N, D)
    memf = mem.reshape(B * M, D)

    h_e = _rms_fwd(x0, P["mix_norm_e"][0], name="rms_mix_e", xchg=gather("rms_mix_e"))
    if "rms_mix_e" in fwd_hooks:
        h_e, got = h_e
        placed("rms_mix_e", got)
    w_in_pad = _pad_w_in(P["w_in_e"][0][0], W, H)[None]
    proj = mm(h_e, w_in_pad, bl=0, name="mm_in_e", tn=896)
    proj3 = proj.reshape(B, T, -1)
    cum = _fox_cum(proj3, f_blk, P["fox_f_bias"][0], name="fox_cum")
    o_fox, lse, got = _fox_fwd(proj3, cum, name="fox_fwd", xchg=gather("fox_fwd"))
    placed("fox_fwd", got)
    bias_full = jnp.repeat(P["gmlp_b_s"][0].T, _GRP, axis=1)
    a_out = _gmlp_fwd(proj, P["gmlp_ln_g"][0], P["gmlp_ln_b"][0], P["gmlp_w_s"][0], bias_full, name="gmlp_fwd")
    mixcat = jnp.concatenate([o_fox.reshape(N, W), a_out], axis=1)
    x1 = mm(mixcat, P["w_out_e"][0], bl=0, res=x0, name="mm_out_e")

    def xa_ffn_fwd(xin, l, last=False):
        s = {}
        s["q"], s["h_xa"] = mm(xin, P["xa_wq"][l], bl=0, rms_fwd=P["xa_norm"][l], out_dtype=_MXU, name=f"mm_q{l}")
        s["mn"] = _rms_fwd(memf, P["mem_norm"][l], name=f"rms_mem{l}")
        s["kv"] = mm(s["mn"], P["xa_wkv"][l], bl=0, out_dtype=_MXU, name=f"mm_kv{l}")
        s["o"] = _xa_fwd(s["q"].reshape(B, T, D), s["kv"].reshape(B, M, 2 * D), name=f"xa_fwd{l}").reshape(N, D)
        s["x_mid"] = mm(s["o"], P["xa_wo"][l], bl=0, res=xin, name=f"mm_o{l}")
        s["gu"], s["act"], s["h_ffn"], got = _mm_gu(s["x_mid"], P["ffn_w_gu"][l], 0, rms_fwd=P["ffn_norm"][l], name=f"mm_gu{l}",
                                                    xchg=gather(f"mm_gu{l}"))
        placed(f"mm_gu{l}", got)
        s["x_in"] = xin
        if last:
            return None, s
        xout = mm(s["act"], P["ffn_w_down"][l], bl=0, res=s["x_mid"], name=f"mm_down{l}", tn=512)
        return xout, s

    x3, s0 = xa_ffn_fwd(x1, 0)
    ag, y, h_o, _ = _mm_gu(x3, P["conv_w_in"][0], 0, bias=P["conv_b_in"][0], glu=True, keep=_F32, rms_fwd=P["mix_norm_o"][0],
                           name="mm_conv_in")
    C = y.shape[1]
    dw_w = jnp.pad(P["conv_dw_w"][0], ((0, _HALO - _CONV_K), (0, 0)))
    y2, y4, got = _conv_fwd(y.reshape(B, T, C), dw_w, P["conv_dw_b"][0], P["conv_ln_g"][0], P["conv_ln_b"][0], name="conv_fwd",
                            xchg=gather("conv_fwd"))
    placed("conv_fwd", got)
    x4 = mm(y4.reshape(N, C), P["conv_w_out"][0], bl=0, bias=P["conv_b_out"][0], res=x3, name="mm_conv_out")
    _, s1 = xa_ffn_fwd(x4, 1, last=True)
    loss, dx, dg = _final_loss(s1["act"], P["ffn_w_down"][1], s1["x_mid"], P["final_norm"], tgt.reshape(N, D), name="final_loss")
    G["final_norm"] = [dg]

    def xa_ffn_bwd(dx, s, l):
        for k in ("ffn_w_down", "ffn_w_gu", "ffn_norm", "xa_wo", "xa_wq", "xa_norm", "xa_wkv", "mem_norm"):
            G.setdefault(k, {})
        dgu = _mm_dgu(dx, P["ffn_w_down"][l], 0, s["gu"], name=f"mm_dgu{l}")
        G["ffn_w_down"][l] = mm(s["act"], dx, ta=True, tk=_DW_ROWS, out_dtype=_MXU, name=f"mm_dwdown{l}", tm=1408)
        G["ffn_w_gu"][l] = (mm(s["h_ffn"], dgu, ta=True, tk=_DW_ROWS, bl=0, out_dtype=_MXU, name=f"mm_dwg{l}", tn=1408),
                            mm(s["h_ffn"], dgu, ta=True, tk=_DW_ROWS, bl=1, out_dtype=_MXU, name=f"mm_dwu{l}", tn=1408))
        dx, G["ffn_norm"][l] = mm(dgu, P["ffn_w_gu"][l], al="cat", bl=0, tb=True, rms_bwd=(s["x_mid"], P["ffn_norm"][l], dx),
                                  name=f"mm_dhffn{l}", tm=512, tn=D, tk=2816)
        do = mm(dx, P["xa_wo"][l], bl=0, tb=True, out_dtype=_MXU, name=f"mm_do{l}")
        G["xa_wo"][l] = mm(s["o"], dx, ta=True, tk=_DW_ROWS, out_dtype=_MXU, name=f"mm_dwo{l}")
        dq, dkv = _xa_bwd(s["q"].reshape(B, T, D), s["kv"].reshape(B, M, 2 * D), do.reshape(B, T, D), name=f"xa_bwd{l}")
        dq, dkv = dq.reshape(N, D), dkv.reshape(B * M, 2 * D)
        G["xa_wq"][l] = mm(s["h_xa"], dq, ta=True, tk=_DW_ROWS, out_dtype=_MXU, name=f"mm_dwq{l}")
        dx, G["xa_norm"][l] = mm(dq, P["xa_wq"][l], bl=0, tb=True, rms_bwd=(s["x_in"], P["xa_norm"][l], dx), name=f"mm_dhxa{l}",
                                 tm=512, tn=D)
        G["xa_wkv"][l] = mm(s["mn"], dkv, ta=True, tk=_DW_ROWS, out_dtype=_MXU, name=f"mm_dwkv{l}")
        dmn = mm(dkv, P["xa_wkv"][l], bl=0, tb=True, name=f"mm_dmn{l}")
        G["mem_norm"][l] = _rms_bwd(memf, P["mem_norm"][l], dmn, None, name=f"rms_mem_bwd{l}")
        return dx

    dx = xa_ffn_bwd(dx, s1, 1)
    G["conv_b_out"] = [_colsum(dx, name="colsum_b_out")]
    G["conv_w_out"] = [mm(y4.reshape(N, C), dx, ta=True, tk=_DW_ROWS, out_dtype=_MXU, name="mm_dwconv_out")]
    dy2, dlg, dlb, ddb = _conv_ln_bwd(y2.reshape(N, C), dx, P["conv_w_out"][0], P["conv_ln_g"][0], P["conv_ln_b"][0],
                                      name="conv_ln_bwd")
    G["conv_ln_g"], G["conv_ln_b"], G["conv_dw_b"] = [dlg], [dlb], [ddb]
    dag, ddw, dbin, got = _conv_bwd(y.reshape(B, T, C), dy2.reshape(B, T, C), ag.reshape(2, B, T, C), dw_w, name="conv_bwd",
                                    xchg=scatter("conv_bwd"))
    received("conv_bwd", got)
    G["conv_dw_w"], G["conv_b_in"] = [ddw[:_CONV_K]], [dbin]
    dag = dag.reshape(N, 2 * C)
    G["conv_w_in"] = [mm(h_o, dag, ta=True, tk=_DW_ROWS, out_dtype=_MXU, name="mm_dwconv_in")]
    dx, dg = mm(dag, P["conv_w_in"][0], bl=0, tb=True, rms_bwd=(x3, P["mix_norm_o"][0], dx), name="mm_dh_o", tm=512, tn=D)
    G["mix_norm_o"] = [dg]
    dx = xa_ffn_bwd(dx, s0, 0)
    G["w_out_e"] = [mm(mixcat, dx, ta=True, tk=_DW_ROWS, out_dtype=_MXU, name="mm_dwout_e")]
    dmix = mm(dx, P["w_out_e"][0], bl=0, tb=True, name="mm_dmix")
    dz, dlg, dlb, dws, dbias = _gmlp_bwd(proj, dmix, 1, P["gmlp_ln_g"][0], P["gmlp_ln_b"][0], P["gmlp_w_s"][0], bias_full,
                                         name="gmlp_bwd")
    G["gmlp_ln_g"], G["gmlp_ln_b"], G["gmlp_w_s"] = [dlg], [dlb], [dws]
    G["gmlp_b_s"] = [dbias[:, :2 * (W // _LANES)].T]
    dmix3 = dmix.reshape(B, T, D)
    dq, dk, dv, dcum, got = _fox_bwd(proj3, cum, dmix3, lse, name="fox_bwd", xchg=scatter("fox_bwd"))
    received("fox_bwd", got)
    df, dfb = _fox_cum_bwd(proj3, f_blk, P["fox_f_bias"][0], dcum.reshape(B, H, T), name="fox_cum_bwd")
    G["fox_f_bias"] = [dfb]
    dproj = jnp.concatenate([dq.reshape(N, W), dk.reshape(N, W), dv.reshape(N, W), dz, df.reshape(N, _LANES).astype(_MXU)], axis=1)
    G["w_in_e"] = [_unpad_w_in(mm(h_e, dproj, ta=True, tk=_DW_ROWS, out_dtype=_MXU, name="mm_dwin_e", tn=896), W, H)]
    dx, dg = mm(dproj, w_in_pad, bl=0, tb=True, rms_bwd=(x0, P["mix_norm_e"][0], dx), name="mm_dh_e", tm=512, tn=D)
    G["mix_norm_e"] = [dg]
    return loss, dx.reshape(B, T, D), G, sent


def _pad_w_in(w_in, W, H):
    f = w_in[:, 3 * W:3 * W + H]
    return jnp.concatenate([w_in[:, :3 * W], w_in[:, 3 * W + H:], jnp.pad(f, ((0, 0), (0, _LANES - H)))], axis=1)


def _unpad_w_in(g, W, H):
    return jnp.concatenate([g[:, :3 * W], g[:, 5 * W:5 * W + H], g[:, 3 * W:5 * W]], axis=1)


def kernel(x, mem, mix_norm_e, w_in_e, fox_f_bias, gmlp_ln_g, gmlp_ln_b, gmlp_w_s, gmlp_b_s, w_out_e, mix_norm_o, conv_w_in, conv_b_in, conv_dw_w, conv_dw_b, conv_ln_g, conv_ln_b, conv_w_out, conv_b_out, xa_norm, mem_norm, xa_wq, xa_wkv, xa_wo, ffn_norm, ffn_w_gu, ffn_w_down, final_norm, loss_target, m_mix_norm_e, m_w_in_e, m_fox_f_bias, m_gmlp_ln_g, m_gmlp_ln_b, m_gmlp_w_s, m_gmlp_b_s, m_w_out_e, m_mix_norm_o, m_conv_w_in, m_conv_b_in, m_conv_dw_w, m_conv_dw_b, m_conv_ln_g, m_conv_ln_b, m_conv_w_out, m_conv_b_out, m_xa_norm, m_mem_norm, m_xa_wq, m_xa_wkv, m_xa_wo, m_ffn_norm, m_ffn_w_gu, m_ffn_w_down, m_final_norm, v_mix_norm_e, v_w_in_e, v_fox_f_bias, v_gmlp_ln_g, v_gmlp_ln_b, v_gmlp_w_s, v_gmlp_b_s, v_w_out_e, v_mix_norm_o, v_conv_w_in, v_conv_b_in, v_conv_dw_w, v_conv_dw_b, v_conv_ln_g, v_conv_ln_b, v_conv_w_out, v_conv_b_out, v_xa_norm, v_mem_norm, v_xa_wq, v_xa_wkv, v_xa_wo, v_ffn_norm, v_ffn_w_gu, v_ffn_w_down, v_final_norm):
    env = dict(locals())
    w = {n: env[n] for n in _WEIGHTS}
    mom = {n: env["m_" + n] for n in _WEIGHTS}
    var = {n: env["v_" + n] for n in _WEIGHTS}
    D = x.shape[-1]
    W = D // 2
    H = W // _FOX_HD

    def layers(n):
        return w[n].shape[0] if w[n].ndim > 1 else 1

    wsrc = {n: (w[n].astype(_MXU) if n in dict(_BIG) else w[n]) for n, _ in _BIG + _SMALL_SHARDED}
    P = {n: w[n] for n in _REPLICATED}
    fwd_hooks = {
        "rms_mix_e": [("w_in_e", 0)],
        "mm_in_e": [("w_out_e", 0), ("xa_wq", 0)],
        "fox_fwd": [("xa_wkv", 0), ("ffn_w_gu", 0)],
        "mm_out_e": [("xa_wo", 0)],
        "mm_q0": [("conv_w_out", 0)],
        "mm_o0": [("xa_wq", 1)],
        "mm_gu0": [("ffn_w_down", 0), ("conv_w_in", 0)] + [(n, 0) for n, _ in _SMALL_SHARDED],
        "mm_down0": [("xa_wkv", 1)],
        "conv_fwd": [("xa_wo", 1), ("ffn_w_gu", 1)],
        "mm_gu1": [("ffn_w_down", 1)],
    }

    last = [("mix_norm_e", 0)]
    in_dh_e = [("w_in_e", 0), ("fox_f_bias", 0)]
    in_conv = [(n, 1) for n in ("ffn_w_gu", "ffn_w_down", "xa_wq", "xa_wkv", "xa_wo", "xa_norm", "mem_norm", "ffn_norm")]
    in_conv += [("final_norm", 0), ("conv_w_out", 0)]
    every = [(n, l) for n in [n for n, _ in _BIG + _SMALL_SHARDED] + list(_REPLICATED) for l in range(layers(n))]
    bwd_hooks = {"conv_bwd": in_conv, "mm_dhffn0": [("ffn_w_gu", 0)], "mm_dhxa0": [("xa_wq", 0), ("xa_wo", 0)],
                 "mm_dmix": [("xa_wkv", 0)], "mm_dh_e": in_dh_e}
    placed_pieces = last + [pc for pieces in bwd_hooks.values() for pc in pieces]
    bwd_hooks["fox_bwd"] = [pc for pc in every if pc not in placed_pieces]
    loss, grad_x, G, recv = _local_step(x, mem, loss_target, P, wsrc, fwd_hooks, bwd_hooks)
    loss = lax.psum(loss[0, 0], ("x", "y", "c"))
    recv.update(zip(last, _exchange(_grad_items(last, G), name="scatter_last")))

    def partials(n):
        return [recv[(n, l)] for l in range(layers(n))]

    res = {n: _adamw_big(partials(n), w[n], mom[n], var[n], name="adamw_" + n) for n, _ in _BIG}
    small = [n for n, _ in _SMALL_SHARDED] + list(_REPLICATED)

    def rows(a, n):
        r = recv[(n, 0)]
        return a.reshape((layers(n),) + r.shape[2:-1] + (-1,))

    outs = _adamw_small([(partials(n), rows(w[n], n), rows(mom[n], n), rows(var[n], n),
                          w[n].shape[-1] if w[n].shape[-1] != recv[(n, 0)].shape[-1] else None) for n in small], name="adamw_small")
    for n, o in zip(small, outs):
        res[n] = tuple(a.reshape(w[n].shape) for a in o)
    return (loss, grad_x, *[res[n][0] for n in _WEIGHTS], *[res[n][1] for n in _WEIGHTS],
            *[res[n][2] for n in _WEIGHTS], *[res[n][3] for n in _WEIGHTS])
```

```python
import functools
import math

import jax
import jax.numpy as jnp
from jax import lax
from jax.experimental import pallas as pl
from jax.experimental.pallas import tpu as pltpu

_F32 = jnp.float32
_MXU = jnp.bfloat16
_VMEM_LIMIT = 48 * 1024 * 1024
_LANES = 128
_DW_ROWS = 2048
_EPS = 1e-6
_N_DEV = 8
_FOX_HD = 64
_FOX_SCALE = _FOX_HD ** -0.5
_FOX_TQ = 512
_CHUNK = 128
_GRP = 64
_CONV_K = 31
_HALO = 32
_CONV_ROWS = 128
_XA_HEADS = 4
_GELU_C = math.sqrt(2.0 / math.pi)
_ADAM_LR, _ADAM_B1, _ADAM_B2, _ADAM_EPS, _ADAM_WD, _ADAM_STEP = 0.001, 0.9, 0.999, 1e-08, 0.01, 10


def _params(*sem):
    return pltpu.CompilerParams(dimension_semantics=sem if sem else None, vmem_limit_bytes=_VMEM_LIMIT)


def _pick(n, pref):
    if n <= pref:
        return n
    best = None
    for t in range(_LANES, pref + 1, _LANES):
        if n % t == 0:
            best = t
    assert best is not None, (n, pref)
    return best


def _rows(n, pref):
    if n <= pref:
        return n
    t = pref
    while n % t:
        t //= 2
    assert t >= 8, (n, pref)
    return t


def _sigmoid(x):
    return 1.0 / (1.0 + jnp.exp(-x))


def _gelu(x):
    t = jnp.tanh(_GELU_C * (x + 0.044715 * (x * x * x)))
    return 0.5 * x * (1.0 + t)


def _gelu_grad(x):
    x2 = x * x
    t = jnp.tanh(_GELU_C * (x + 0.044715 * (x2 * x)))
    return 0.5 * (1.0 + t) + 0.5 * x * (1.0 - t * t) * (_GELU_C * (1.0 + 3.0 * 0.044715 * x2))


def _dot(a, b, ca, cb):
    return lax.dot_general(a, b, (((ca,), (cb,)), ((), ())), preferred_element_type=_F32)


def _rms_rows(xv, gain):
    return (xv * lax.rsqrt(jnp.mean(xv * xv, axis=-1, keepdims=True) + _EPS) * gain).astype(_MXU)


def _mm(a, b, *, name, ta=False, tb=False, al=None, bl=None, bk0=0, bias=None, res=None, rms_bwd=None, rms_fwd=None,
        out_dtype=_F32, tm=1024, tn=512, tk=1024, xchg=None):
    if ta:
        K, M = a.shape[-2:]
    else:
        M, K = a.shape[-2:]
    if al == "cat":
        assert not ta
        K = a.shape[0] * a.shape[-1]
    if tb:
        N, K2 = b.shape[-2:]
    else:
        K2, N = b.shape[-2:]
    assert K == K2 or (tb and K2 > K), (a.shape, b.shape, ta, tb)
    tm, tn = _pick(M, tm), _pick(N, tn)
    tk = K if (not ta and K <= 2816 and K2 == K) else _pick(a.shape[-1] if al == "cat" else K, tk)
    nk = K // tk
    assert bk0 % tk == 0
    kb = bk0 // tk
    grid = (M // tm, N // tn, nk)
    if al == "cat":
        per = a.shape[-1] // tk
        a_spec = pl.BlockSpec((None, tm, tk), lambda i, j, k: (k // per, i, k % per))
    elif a.ndim == 3:
        a_spec = (pl.BlockSpec((None, tk, tm), lambda i, j, k: (al, k, i)) if ta
                  else pl.BlockSpec((None, tm, tk), lambda i, j, k: (al, i, k)))
    else:
        a_spec = pl.BlockSpec((tk, tm), lambda i, j, k: (k, i)) if ta else pl.BlockSpec((tm, tk), lambda i, j, k: (i, k))
    if b.ndim == 3:
        b_spec = (pl.BlockSpec((None, tn, tk), lambda i, j, k: (bl, j, k + kb)) if tb
                  else pl.BlockSpec((None, tk, tn), lambda i, j, k: (bl, k, j)))
    else:
        b_spec = pl.BlockSpec((tn, tk), lambda i, j, k: (j, k)) if tb else pl.BlockSpec((tk, tn), lambda i, j, k: (k, j))
    in_specs, args = [a_spec, b_spec], [a, b]
    if bias is not None:
        in_specs.append(pl.BlockSpec((1, tn), lambda i, j, k: (0, j)))
        args.append(bias.reshape(1, N).astype(_F32))
    if res is not None:
        in_specs.append(pl.BlockSpec((tm, tn), lambda i, j, k: (i, j)))
        args.append(res)
    has_bias, has_res, has_rms = bias is not None, res is not None, rms_bwd is not None
    if has_rms:
        assert tn == N, (tn, N)
        x, g, dres = rms_bwd
        in_specs += [pl.BlockSpec((tm, N), lambda i, j, k: (i, 0)), pl.BlockSpec((1, N), lambda i, j, k: (0, 0)),
                     pl.BlockSpec((tm, N), lambda i, j, k: (i, 0))]
        args += [x, g.reshape(1, N), dres]
    has_norm = rms_fwd is not None
    if has_norm:
        assert not ta and nk == 1 and a.ndim == 2
        in_specs.append(pl.BlockSpec((1, K), lambda i, j, k: (0, 0)))
        args.append(rms_fwd.reshape(1, K))

    def body(*refs):
        a_ref, b_ref = refs[0], refs[1]
        pos = 2
        bias_ref = res_ref = None
        if has_bias:
            bias_ref = refs[pos]
            pos += 1
        if has_res:
            res_ref = refs[pos]
            pos += 1
        if has_rms:
            x_ref, g_ref, dres_ref = refs[pos:pos + 3]
            pos += 3
        if has_norm:
            gain_ref = refs[pos]
            pos += 1
        o_ref = refs[pos]
        pos += 1
        if has_rms:
            dg_ref = refs[pos]
            pos += 1
        if has_norm:
            h_ref = refs[pos]
            pos += 1
        acc_ref = refs[pos] if nk > 1 else None
        first_rows = pl.program_id(0) == 0
        if has_norm:
            av = _rms_rows(a_ref[...], gain_ref[...])
            h_ref[...] = av
        else:
            av = a_ref[...].astype(_MXU)
        p = _dot(av, b_ref[...].astype(_MXU), 0 if ta else 1, 1 if tb else 0)

        def finish(acc):
            if has_bias:
                acc = acc + bias_ref[...]
            if has_res:
                acc = acc + res_ref[...]
            if has_rms:
                @pl.when(first_rows)
                def _():
                    dg_ref[...] = jnp.zeros_like(dg_ref)

                xv = x_ref[...]
                r = lax.rsqrt(jnp.mean(xv * xv, axis=-1, keepdims=True) + _EPS)
                xh = xv * r
                dg_ref[...] += jnp.sum(acc * xh, axis=0, keepdims=True)
                dxn = acc * g_ref[...]
                acc = dres_ref[...] + r * (dxn - xh * jnp.mean(dxn * xh, axis=-1, keepdims=True))
            o_ref[...] = acc.astype(o_ref.dtype)

        if nk == 1:
            finish(p)
        else:
            k = pl.program_id(2)

            @pl.when(k == 0)
            def _():
                acc_ref[...] = p

            @pl.when(k > 0)
            def _():
                acc_ref[...] += p

            @pl.when(k == nk - 1)
            def _():
                finish(acc_ref[...])

    out_shape = [jax.ShapeDtypeStruct((M, N), out_dtype)]
    out_specs = [pl.BlockSpec((tm, tn), lambda i, j, k: (i, j))]
    if has_rms:
        out_shape.append(jax.ShapeDtypeStruct((1, N), _F32))
        out_specs.append(pl.BlockSpec((1, N), lambda i, j, k: (0, 0)))
    if has_norm:
        out_shape.append(jax.ShapeDtypeStruct((M, K), _MXU))
        out_specs.append(pl.BlockSpec((tm, K), lambda i, j, k: (i, 0)))
    outs, got = _fused_call(
        body, name=name, out_shape=out_shape, grid=grid, in_specs=in_specs, out_specs=out_specs,
        scratch_shapes=[pltpu.VMEM((tm, tn), _F32)] if nk > 1 else [],
        sem=("arbitrary",) * 3 if has_rms or has_norm else ("parallel", "parallel", "arbitrary"), args=args, xchg=xchg)
    out = tuple(outs) if has_rms or has_norm else outs[0]
    return out if xchg is None else (out, got)


def _rms_fwd(x, g, *, name, xchg=None):
    N, D = x.shape
    tr = _rows(N, 512)

    def body(x_ref, g_ref, o_ref):
        xv = x_ref[...]
        r = lax.rsqrt(jnp.mean(xv * xv, axis=-1, keepdims=True) + _EPS)
        o_ref[...] = (xv * r * g_ref[...]).astype(o_ref.dtype)

    (out,), got = _fused_call(
        body, name=name, out_shape=[jax.ShapeDtypeStruct((N, D), _MXU)], grid=(N // tr,),
        in_specs=[pl.BlockSpec((tr, D), lambda i: (i, 0)), pl.BlockSpec((1, D), lambda i: (0, 0))],
        out_specs=[pl.BlockSpec((tr, D), lambda i: (i, 0))], scratch_shapes=[], sem=("parallel",),
        args=(x, g.reshape(1, D)), xchg=xchg)
    return out if xchg is None else (out, got)


def _rms_gain_bwd(x, dh, *, name):
    N, D = x.shape
    tr = _rows(N, 256)

    def body(x_ref, dh_ref, dg_ref):
        xv = x_ref[...]
        xh = xv * lax.rsqrt(jnp.mean(xv * xv, axis=-1, keepdims=True) + _EPS)

        @pl.when(pl.program_id(0) == 0)
        def _():
            dg_ref[...] = jnp.zeros_like(dg_ref)

        dg_ref[...] += jnp.sum(dh_ref[...].astype(_F32) * xh, axis=0, keepdims=True)

    row = pl.BlockSpec((tr, D), lambda i: (i, 0))
    return pl.pallas_call(
        body, name=name, out_shape=jax.ShapeDtypeStruct((1, D), _F32), grid=(N // tr,), in_specs=[row, row],
        out_specs=pl.BlockSpec((1, D), lambda i: (0, 0)), compiler_params=_params("arbitrary"),
    )(x, dh)


def _colsum(a, *, name):
    M, C = a.shape
    tr = _rows(M, 512)

    def body(a_ref, o_ref):
        @pl.when(pl.program_id(0) == 0)
        def _():
            o_ref[...] = jnp.zeros_like(o_ref)

        o_ref[...] += jnp.sum(a_ref[...].astype(_F32), axis=0, keepdims=True)

    return pl.pallas_call(
        body, name=name, out_shape=jax.ShapeDtypeStruct((1, C), _F32), grid=(M // tr,),
        in_specs=[pl.BlockSpec((tr, C), lambda i: (i, 0))], out_specs=pl.BlockSpec((1, C), lambda i: (0, 0)),
        compiler_params=_params("arbitrary"),
    )(a)


def _final_loss(act, w_down, res, g, tgt, *, name):
    N, D = res.shape
    K = act.shape[1]
    tr = _rows(N, 512)

    def body(a_ref, w_ref, res_ref, g_ref, t_ref, loss_ref, dx_ref, dg_ref):
        xv = _dot(a_ref[...].astype(_MXU), w_ref[...].astype(_MXU), 1, 0) + res_ref[...]
        r = lax.rsqrt(jnp.mean(xv * xv, axis=-1, keepdims=True) + _EPS)
        xh = xv * r
        gv = g_ref[...]
        diff = xh * gv - t_ref[...]

        @pl.when(pl.program_id(0) == 0)
        def _():
            loss_ref[...] = jnp.zeros_like(loss_ref)
            dg_ref[...] = jnp.zeros_like(dg_ref)

        part = jnp.sum(jnp.sum(diff * diff, axis=1, keepdims=True), axis=0, keepdims=True) * (0.5 / D)
        loss_ref[...] += jnp.broadcast_to(part, loss_ref.shape)
        dy = diff * (1.0 / D)
        dg_ref[...] += jnp.sum(dy * xh, axis=0, keepdims=True)
        dxn = dy * gv
        dx_ref[...] = r * (dxn - xh * jnp.mean(dxn * xh, axis=-1, keepdims=True))

    row = pl.BlockSpec((tr, D), lambda i: (i, 0))
    vec = pl.BlockSpec((1, D), lambda i: (0, 0))
    return pl.pallas_call(
        body, name=name,
        out_shape=(jax.ShapeDtypeStruct((8, _LANES), _F32), jax.ShapeDtypeStruct((N, D), _F32), jax.ShapeDtypeStruct((1, D), _F32)),
        grid=(N // tr,),
        in_specs=[pl.BlockSpec((tr, K), lambda i: (i, 0)), pl.BlockSpec((None, K, D), lambda i: (0, 0, 0)), row, vec, row],
        out_specs=(pl.BlockSpec((8, _LANES), lambda i: (0, 0)), row, vec),
        compiler_params=_params("arbitrary"),
    )(act, w_down, res, g.reshape(1, D), tgt)


def _mm_gu(h, w_gu, l, *, name, bias=None, glu=False, keep=None, rms_fwd=None, tm=512, tn=1408, xchg=None):
    keep = _MXU if keep is None else keep
    N, K = h.shape
    H = w_gu.shape[-1] // 2
    tm, tn = _pick(N, tm), _pick(H, tn)
    nj = H // tn
    has_bias, has_norm = bias is not None, rms_fwd is not None

    def body(*refs):
        h_ref, wp_ref, wq_ref = refs[:3]
        pair_ref, act_ref = refs[3 + 2 * has_bias + has_norm:][:2]
        if has_norm:
            hv = _rms_rows(h_ref[...], refs[3 + 2 * has_bias][...])
            refs[-1][...] = hv
        else:
            hv = h_ref[...].astype(_MXU)
        p = _dot(hv, wp_ref[...].astype(_MXU), 1, 0)
        q = _dot(hv, wq_ref[...].astype(_MXU), 1, 0)
        if has_bias:
            p = p + refs[3][...]
            q = q + refs[4][...]
        pair_ref[0] = p.astype(pair_ref.dtype)
        pair_ref[1] = q.astype(pair_ref.dtype)
        act_ref[...] = (p * _sigmoid(q) if glu else p * _sigmoid(p) * q).astype(act_ref.dtype)

    in_specs = [pl.BlockSpec((tm, K), lambda i, j: (i, 0)), pl.BlockSpec((None, K, tn), lambda i, j: (l, 0, j)),
                pl.BlockSpec((None, K, tn), lambda i, j: (l, 0, j + nj))]
    args = [h, w_gu, w_gu]
    if has_bias:
        b2 = bias.reshape(1, 2 * H).astype(_F32)
        in_specs += [pl.BlockSpec((1, tn), lambda i, j: (0, j)), pl.BlockSpec((1, tn), lambda i, j: (0, j + nj))]
        args += [b2, b2]
    out_shape = [jax.ShapeDtypeStruct((2, N, H), keep), jax.ShapeDtypeStruct((N, H), keep)]
    out_specs = [pl.BlockSpec((2, tm, tn), lambda i, j: (0, i, j)), pl.BlockSpec((tm, tn), lambda i, j: (i, j))]
    if has_norm:
        in_specs.append(pl.BlockSpec((1, K), lambda i, j: (0, 0)))
        args.append(rms_fwd.reshape(1, K))
        out_shape.append(jax.ShapeDtypeStruct((N, K), _MXU))
        out_specs.append(pl.BlockSpec((tm, K), lambda i, j: (i, 0)))
    outs, got = _fused_call(
        body, name=name, out_shape=out_shape, grid=(N // tm, nj), in_specs=in_specs, out_specs=out_specs,
        scratch_shapes=[], sem=("arbitrary", "arbitrary") if has_norm else ("parallel", "parallel"), args=args, xchg=xchg)
    return (*outs, got)


def _mm_dgu(dx, w_down, l, gu, *, name, tm=512, tn=1408):
    N, K = dx.shape
    H = w_down.shape[-2]
    tm, tn = _pick(N, tm), _pick(H, tn)

    def body(dx_ref, w_ref, gu_ref, o_ref):
        d = _dot(dx_ref[...].astype(_MXU), w_ref[...].astype(_MXU), 1, 1)
        g, u = gu_ref[0].astype(_F32), gu_ref[1].astype(_F32)
        sg = _sigmoid(g)
        o_ref[0] = (d * u * (sg * (1.0 + g * (1.0 - sg)))).astype(o_ref.dtype)
        o_ref[1] = (d * (g * sg)).astype(o_ref.dtype)

    return pl.pallas_call(
        body, name=name, out_shape=jax.ShapeDtypeStruct((2, N, H), _MXU), grid=(N // tm, H // tn),
        in_specs=[pl.BlockSpec((tm, K), lambda i, j: (i, 0)), pl.BlockSpec((None, tn, K), lambda i, j: (l, j, 0)),
                  pl.BlockSpec((2, tm, tn), lambda i, j: (0, i, j))],
        out_specs=pl.BlockSpec((2, tm, tn), lambda i, j: (0, i, j)), compiler_params=_params("parallel", "parallel"),
    )(dx, w_down, gu)


def _gmlp_mix(vb, w, trans):
    tr, W = vb.shape
    lane = lax.broadcasted_iota(jnp.int32, (_CHUNK, _LANES), 1)
    rows = []
    for c in range(tr // _CHUNK):
        tiles = []
        for j in range(W // _LANES):
            t = vb[c * _CHUNK:(c + 1) * _CHUNK, j * _LANES:(j + 1) * _LANES]
            ma = _dot(w[2 * j], t, 0 if trans else 1, 0)
            mb = _dot(w[2 * j + 1], t, 0 if trans else 1, 0)
            tiles.append(jnp.where(lane < _GRP, ma, mb))
        rows.append(jnp.concatenate(tiles, axis=1))
    return jnp.concatenate(rows, axis=0)


def _tril_w(w_ref):
    r = lax.broadcasted_iota(jnp.int32, (_CHUNK, _CHUNK), 0)
    c = lax.broadcasted_iota(jnp.int32, (_CHUNK, _CHUNK), 1)
    return jnp.where((r >= c)[None], w_ref[...], 0.0).astype(_MXU)


def _layernorm_stats(v):
    mu = jnp.mean(v, axis=-1, keepdims=True)
    xc = v - mu
    rstd = lax.rsqrt(jnp.mean(xc * xc, axis=-1, keepdims=True) + _EPS)
    return xc * rstd, rstd


def _gmlp_fwd(proj, ln_g, ln_b, w_s, bias_full, *, name):
    N = proj.shape[0]
    W = ln_g.shape[-1]
    G = w_s.shape[0]
    tr = _rows(N, 512)
    ub, vb_ = 3, 4

    def body(u_ref, v_ref, g_ref, b_ref, w_ref, bias_ref, o_ref):
        u = _gelu(u_ref[...])
        xh, _ = _layernorm_stats(_gelu(v_ref[...]))
        vgn = xh * g_ref[...] + b_ref[...]
        mixed = _gmlp_mix(vgn.astype(_MXU), _tril_w(w_ref), False)
        bias = jnp.concatenate([bias_ref[...]] * (tr // _CHUNK), axis=0)
        o_ref[...] = (u * (mixed + bias)).astype(o_ref.dtype)

    vec = pl.BlockSpec((1, W), lambda i: (0, 0))
    return pl.pallas_call(
        body, name=name, out_shape=jax.ShapeDtypeStruct((N, W), _MXU), grid=(N // tr,),
        in_specs=[pl.BlockSpec((tr, W), lambda i: (i, ub)), pl.BlockSpec((tr, W), lambda i: (i, vb_)), vec, vec,
                  pl.BlockSpec((G, _CHUNK, _CHUNK), lambda i: (0, 0, 0)), pl.BlockSpec((_CHUNK, W), lambda i: (0, 0))],
        out_specs=pl.BlockSpec((tr, W), lambda i: (i, 0)), compiler_params=_params("parallel"),
    )(proj, proj, ln_g.reshape(1, W), ln_b.reshape(1, W), w_s, bias_full)


def _gmlp_bwd(proj, da_src, da_blk, ln_g, ln_b, w_s, bias_full, *, name):
    N = proj.shape[0]
    W = ln_g.shape[-1]
    G = w_s.shape[0]
    tr = _rows(N, 512)
    nch = tr // _CHUNK

    def body(u_ref, v_ref, da_ref, g_ref, b_ref, w_ref, bias_ref, dz_ref, dg_ref, db_ref, dw_ref, dbias_ref):
        @pl.when(pl.program_id(0) == 0)
        def _():
            dg_ref[...] = jnp.zeros_like(dg_ref)
            db_ref[...] = jnp.zeros_like(db_ref)
            dw_ref[...] = jnp.zeros_like(dw_ref)
            dbias_ref[...] = jnp.zeros_like(dbias_ref)

        u_pre, v_pre = u_ref[...], v_ref[...]
        ug = _gelu(u_pre)
        xh, rstd = _layernorm_stats(_gelu(v_pre))
        lg = g_ref[...]
        vgn = xh * lg + b_ref[...]
        vb = vgn.astype(_MXU)
        wt = _tril_w(w_ref)
        mixed = _gmlp_mix(vb, wt, False)
        bias = jnp.concatenate([bias_ref[...]] * nch, axis=0)
        da = da_ref[...].astype(_F32)
        du = da * (mixed + bias)
        dm = da * ug
        dmb = dm.astype(_MXU)
        lane = lax.broadcasted_iota(jnp.int32, (_CHUNK, _LANES), 1)
        r = lax.broadcasted_iota(jnp.int32, (_CHUNK, _CHUNK), 0)
        c = lax.broadcasted_iota(jnp.int32, (_CHUNK, _CHUNK), 1)
        tril = r >= c
        dmsum = dm[0:_CHUNK]
        for ch in range(1, nch):
            dmsum = dmsum + dm[ch * _CHUNK:(ch + 1) * _CHUNK]
        dbias = jnp.zeros((_CHUNK, _LANES), _F32)
        for j in range(W // _LANES):
            tile = dmsum[:, j * _LANES:(j + 1) * _LANES]
            sa = jnp.sum(jnp.where(lane < _GRP, tile, 0.0), axis=1, keepdims=True)
            sb = jnp.sum(jnp.where(lane >= _GRP, tile, 0.0), axis=1, keepdims=True)
            dbias = dbias + jnp.where(lane == 2 * j, sa, 0.0) + jnp.where(lane == 2 * j + 1, sb, 0.0)
            acc_a = jnp.zeros((_CHUNK, _CHUNK), _F32)
            acc_b = jnp.zeros((_CHUNK, _CHUNK), _F32)
            for ch in range(nch):
                dt = dmb[ch * _CHUNK:(ch + 1) * _CHUNK, j * _LANES:(j + 1) * _LANES]
                vt = vb[ch * _CHUNK:(ch + 1) * _CHUNK, j * _LANES:(j + 1) * _LANES]
                acc_a = acc_a + _dot(jnp.where(lane < _GRP, dt, jnp.zeros_like(dt)), vt, 1, 1)
                acc_b = acc_b + _dot(jnp.where(lane >= _GRP, dt, jnp.zeros_like(dt)), vt, 1, 1)
            dw_ref[2 * j] += jnp.where(tril, acc_a, 0.0)
            dw_ref[2 * j + 1] += jnp.where(tril, acc_b, 0.0)
        dbias_ref[...] += dbias
        dvgn = _gmlp_mix(dmb, wt, True)
        dg_ref[...] += jnp.sum(dvgn * xh, axis=0, keepdims=True)
        db_ref[...] += jnp.sum(dvgn, axis=0, keepdims=True)
        dxh = dvgn * lg
        dvg = rstd * (dxh - jnp.mean(dxh, axis=-1, keepdims=True) - xh * jnp.mean(dxh * xh, axis=-1, keepdims=True))
        dz_ref[:, :W] = (du * _gelu_grad(u_pre)).astype(dz_ref.dtype)
        dz_ref[:, W:] = (dvg * _gelu_grad(v_pre)).astype(dz_ref.dtype)

    vec = pl.BlockSpec((1, W), lambda i: (0, 0))
    wspec = pl.BlockSpec((G, _CHUNK, _CHUNK), lambda i: (0, 0, 0))
    return pl.pallas_call(
        body, name=name,
        out_shape=(jax.ShapeDtypeStruct((N, 2 * W), _MXU), jax.ShapeDtypeStruct((1, W), _F32), jax.ShapeDtypeStruct((1, W), _F32),
                   jax.ShapeDtypeStruct((G, _CHUNK, _CHUNK), _F32), jax.ShapeDtypeStruct((_CHUNK, _LANES), _F32)),
        grid=(N // tr,),
        in_specs=[pl.BlockSpec((tr, W), lambda i: (i, 3)), pl.BlockSpec((tr, W), lambda i: (i, 4)),
                  pl.BlockSpec((tr, W), lambda i: (i, da_blk)), vec, vec, wspec, pl.BlockSpec((_CHUNK, W), lambda i: (0, 0))],
        out_specs=(pl.BlockSpec((tr, 2 * W), lambda i: (i, 0)), vec, vec, wspec, pl.BlockSpec((_CHUNK, _LANES), lambda i: (0, 0))),
        compiler_params=_params("arbitrary"),
    )(proj, proj, da_src, ln_g.reshape(1, W), ln_b.reshape(1, W), w_s, bias_full)


def _lane_cumsum(v):
    T = v.shape[1]
    lane = lax.broadcasted_iota(jnp.int32, (8, _LANES), 1)
    carry = jnp.zeros((8, 1), _F32)
    out = []
    for ch in range(T // _LANES):
        blk = v[:, ch * _LANES:(ch + 1) * _LANES]
        sh = 1
        while sh < _LANES:
            blk = blk + jnp.where(lane >= sh, pltpu.roll(blk, sh, 1), 0.0)
            sh *= 2
        blk = blk + carry
        carry = blk[:, _LANES - 1:_LANES]
        out.append(blk)
    return jnp.concatenate(out, axis=1), carry


def _log_sigmoid(x):
    return jnp.minimum(x, 0.0) - jnp.log(1.0 + jnp.exp(-jnp.abs(x)))


def _fox_cum(proj3, f_blk, f_bias, *, name):
    B, T, _ = proj3.shape
    H = f_bias.shape[-1]
    assert H == 8

    def body(f_ref, b_ref, o_ref):
        x = f_ref[0].T[0:8, :] + b_ref[...]
        cum, _ = _lane_cumsum(_log_sigmoid(x))
        o_ref[0] = cum

    return pl.pallas_call(
        body, name=name, out_shape=jax.ShapeDtypeStruct((B, 8, T), _F32), grid=(B,),
        in_specs=[pl.BlockSpec((1, T, _LANES), lambda b: (b, 0, f_blk)), pl.BlockSpec((8, 1), lambda b: (0, 0))],
        out_specs=pl.BlockSpec((1, 8, T), lambda b: (b, 0, 0)), compiler_params=_params("parallel"),
    )(proj3, f_bias.reshape(8, 1))


def _fox_cum_bwd(proj3, f_blk, f_bias, dcum, *, name):
    B, T, _ = proj3.shape

    def body(f_ref, b_ref, dc_ref, df_ref, dbias_ref):
        @pl.when(pl.program_id(0) == 0)
        def _():
            dbias_ref[...] = jnp.zeros_like(dbias_ref)

        x = f_ref[0].T[0:8, :] + b_ref[...]
        dc = dc_ref[0]
        incl, total = _lane_cumsum(dc)
        dlf = total - incl + dc
        df = dlf * _sigmoid(-x)
        full = jnp.concatenate([df, jnp.zeros((_LANES - 8, T), _F32)], axis=0).T
        dbias_ref[...] += jnp.sum(full, axis=0, keepdims=True)
        df_ref[0] = full

    return pl.pallas_call(
        body, name=name,
        out_shape=(jax.ShapeDtypeStruct((B, T, _LANES), _F32), jax.ShapeDtypeStruct((1, _LANES), _F32)), grid=(B,),
        in_specs=[pl.BlockSpec((1, T, _LANES), lambda b: (b, 0, f_blk)), pl.BlockSpec((8, 1), lambda b: (0, 0)),
                  pl.BlockSpec((1, 8, T), lambda b: (b, 0, 0))],
        out_specs=(pl.BlockSpec((1, T, _LANES), lambda b: (b, 0, 0)), pl.BlockSpec((1, _LANES), lambda b: (0, 0))),
        compiler_params=_params("arbitrary"),
    )(proj3, f_bias.reshape(8, 1), dcum)


def _cum_row(cum_ref, h, start, size):
    blk = cum_ref[0, :, pl.ds(start, size)]
    sub = lax.broadcasted_iota(jnp.int32, (blk.shape[0], 1), 0)
    return jnp.sum(jnp.where(sub == h, blk, 0.0), axis=0, keepdims=True)


def _causal(tq, q0, k0):
    r = lax.broadcasted_iota(jnp.int32, (tq, tq), 0)
    c = lax.broadcasted_iota(jnp.int32, (tq, tq), 1)
    return (r + q0) >= (c + k0)


def _fused_call(body, *, name, out_shape, grid, in_specs, out_specs, scratch_shapes, sem, args, xchg):
    out_shape, in_specs, out_specs, scratch_shapes = list(out_shape), list(in_specs), list(out_specs), list(scratch_shapes)
    if xchg is None:
        res = pl.pallas_call(body, name=name, out_shape=out_shape, grid=grid, in_specs=in_specs, out_specs=out_specs,
                             scratch_shapes=scratch_shapes, compiler_params=_params(*sem))(*args)
        return list(res), []
    n_in, n_out, n_scr = len(in_specs), len(out_specs), len(scratch_shapes)

    def fused(*refs):
        ins, refs = refs[:n_in], refs[n_in:]
        xs, refs = refs[:xchg.n_src], refs[xchg.n_src:]
        outs, refs = refs[:n_out], refs[n_out:]
        xd, refs = refs[:xchg.n_dst], refs[xchg.n_dst:]
        scr, sems = refs[:n_scr], refs[n_scr:]
        first = last = None
        for d, g in enumerate(grid):
            i = pl.program_id(d)
            first = (i == 0) if first is None else first & (i == 0)
            last = (i == g - 1) if last is None else last & (i == g - 1)

        @pl.when(first)
        def _():
            xchg.start(xs, xd, sems)

        body(*ins, *outs, *scr)

        @pl.when(last)
        def _():
            xchg.finish(xs, xd, sems)

    res = pl.pallas_call(
        fused, name=name, out_shape=out_shape + xchg.out_shapes, grid=grid, in_specs=in_specs + xchg.in_specs,
        out_specs=out_specs + xchg.out_specs, scratch_shapes=scratch_shapes + xchg.scratch,
        compiler_params=_params(*["arbitrary"] * len(grid)),
    )(*args, *xchg.srcs)
    return list(res[:n_out]), list(res[n_out:])


def _fox_fwd(proj3, cum, *, name, xchg=None):
    B, T, _ = proj3.shape
    H = cum.shape[1]
    W = H * _FOX_HD
    npair = W // _LANES
    tq = _rows(T, _FOX_TQ)
    nq = T // tq

    def body(q_ref, k_ref, v_ref, cum_ref, o_ref, lse_ref):
        p = pl.program_id(1)
        i = pl.program_id(2)
        q0 = pl.multiple_of(i * tq, tq)
        lane = lax.broadcasted_iota(jnp.int32, (1, _LANES), 1)
        q2 = q_ref[0] * _FOX_SCALE
        heads = []
        for hh in range(2):
            msk = (lane < _FOX_HD) if hh == 0 else (lane >= _FOX_HD)
            h = 2 * p + hh
            heads.append((msk, h, jnp.where(msk, q2, 0.0).astype(_MXU), _cum_row(cum_ref, h, q0, _LANES)[:, 0:1]))

        def step(jj, carry, masked):
            k0 = pl.multiple_of(jj * tq, tq)
            k2 = k_ref[0, pl.ds(k0, tq), :].astype(_MXU)
            v2 = v_ref[0, pl.ds(k0, tq), :]
            out = []
            for (msk, h, qm, c0), (m_prev, l_prev, acc) in zip(heads, carry):
                s = _dot(qm, k2, 1, 1) + (c0 - _cum_row(cum_ref, h, k0, tq))
                if masked:
                    s = jnp.where(_causal(tq, q0, k0), s, -jnp.inf)
                m_new = jnp.maximum(m_prev, jnp.max(s, axis=1, keepdims=True))
                alpha = jnp.exp(m_prev - m_new)
                e = jnp.exp(s - m_new)
                l_new = alpha * l_prev + jnp.sum(e, axis=1, keepdims=True)
                vm = jnp.where(msk, v2, 0.0).astype(_MXU)
                out.append((m_new, l_new, alpha * acc + _dot(e.astype(_MXU), vm, 1, 0)))
            return tuple(out)

        init = tuple((jnp.full((tq, 1), -jnp.inf, _F32), jnp.zeros((tq, 1), _F32), jnp.zeros((tq, _LANES), _F32)) for _ in heads)
        carry = step(i, lax.fori_loop(0, i, functools.partial(step, masked=False), init), True)
        o2 = jnp.zeros((tq, _LANES), _F32)
        for hh, (m, l, acc) in enumerate(carry):
            o2 = o2 + acc / l
            lse_ref[0, hh] = jnp.broadcast_to(m + jnp.log(l), (tq, _LANES))
        o_ref[0] = o2.astype(o_ref.dtype)

    (o, lse), got = _fused_call(
        body, name=name,
        out_shape=(jax.ShapeDtypeStruct((B, T, W), _MXU), jax.ShapeDtypeStruct((B, H, T, _LANES), _F32)),
        grid=(B, npair, nq),
        in_specs=[pl.BlockSpec((1, tq, _LANES), lambda b, p, i: (b, i, p)),
                  pl.BlockSpec((1, T, _LANES), lambda b, p, i: (b, 0, npair + p)),
                  pl.BlockSpec((1, T, _LANES), lambda b, p, i: (b, 0, 2 * npair + p)),
                  pl.BlockSpec((1, H, T), lambda b, p, i: (b, 0, 0))],
        out_specs=(pl.BlockSpec((1, tq, _LANES), lambda b, p, i: (b, i, p)),
                   pl.BlockSpec((1, 2, tq, _LANES), lambda b, p, i: (b, p, i, 0))),
        scratch_shapes=[], sem=("parallel", "parallel", "parallel"), args=(proj3, proj3, proj3, cum), xchg=xchg)
    return o, lse, got


def _fox_bwd(proj3, cum, do3, lse, *, name, xchg=None):
    B, T, _ = proj3.shape
    H = cum.shape[1]
    W = H * _FOX_HD
    npair = W // _LANES
    tq = _rows(T, _FOX_TQ)
    nq = T // tq

    def body(q_ref, k_ref, v_ref, cum_ref, do_ref, lse_ref, dq_ref, dk_ref, dv_ref, dc_ref, p_scr, dp_scr, dk_acc, dv_acc, dc_acc):
        p = pl.program_id(1)
        i = pl.program_id(2)
        q0 = pl.multiple_of(i * tq, tq)
        lane = lax.broadcasted_iota(jnp.int32, (1, _LANES), 1)

        @pl.when(i == 0)
        def _():
            dk_acc[...] = jnp.zeros_like(dk_acc)
            dv_acc[...] = jnp.zeros_like(dv_acc)
            dc_acc[...] = jnp.zeros_like(dc_acc)

        q2 = q_ref[0] * _FOX_SCALE
        do2 = do_ref[0].astype(_F32)
        heads = []
        for hh in range(2):
            msk = (lane < _FOX_HD) if hh == 0 else (lane >= _FOX_HD)
            h = 2 * p + hh
            heads.append((hh, msk, h, jnp.where(msk, q2, 0.0).astype(_MXU), jnp.where(msk, do2, 0.0).astype(_MXU),
                          _cum_row(cum_ref, h, q0, _LANES)[:, 0:1], lse_ref[0, hh][:, 0:1]))

        def first(jj, deltas, masked):
            k0 = pl.multiple_of(jj * tq, tq)
            kb = k_ref[0, pl.ds(k0, tq), :].astype(_MXU)
            vb = v_ref[0, pl.ds(k0, tq), :].astype(_MXU)
            out = []
            for (hh, _, h, qm, dom, c0, lse_h), delta in zip(heads, deltas):
                s = _dot(qm, kb, 1, 1) + (c0 - _cum_row(cum_ref, h, k0, tq))
                pr = jnp.exp(s - lse_h)
                if masked:
                    pr = jnp.where(_causal(tq, q0, k0), pr, 0.0)
                dp = _dot(dom, vb, 1, 1)
                p_scr[hh, jj] = pr
                dp_scr[hh, jj] = dp
                out.append(delta + jnp.sum(pr * dp, axis=1, keepdims=True))
            return tuple(out)

        zero = tuple(jnp.zeros((tq, 1), _F32) for _ in heads)
        deltas = first(i, lax.fori_loop(0, i, functools.partial(first, masked=False), zero), True)

        def second(jj, dq):
            k0 = pl.multiple_of(jj * tq, tq)
            k2 = k_ref[0, pl.ds(k0, tq), :]
            dk = jnp.zeros((tq, _LANES), _F32)
            dv = jnp.zeros((tq, _LANES), _F32)
            for (hh, msk, _, qm, dom, _, _), delta in zip(heads, deltas):
                pr = p_scr[hh, jj]
                ds = pr * (dp_scr[hh, jj] - delta)
                dsb = ds.astype(_MXU)
                dv = dv + _dot(pr.astype(_MXU), dom, 0, 0)
                dk = dk + _dot(dsb, qm, 0, 0)
                dc_acc[hh:hh + 1, pl.ds(k0, tq)] += jnp.sum(ds, axis=0, keepdims=True)
                dq = dq + _dot(dsb, jnp.where(msk, k2, 0.0).astype(_MXU), 1, 0)
            dv_acc[pl.ds(k0, tq), :] += dv
            dk_acc[pl.ds(k0, tq), :] += dk
            return dq

        dq2 = lax.fori_loop(0, i + 1, second, jnp.zeros((tq, _LANES), _F32))
        dq_ref[0] = (dq2 * _FOX_SCALE).astype(dq_ref.dtype)

        @pl.when(i == nq - 1)
        def _():
            dk_ref[0] = dk_acc[...].astype(dk_ref.dtype)
            dv_ref[0] = dv_acc[...].astype(dv_ref.dtype)
            dc_ref[0, 0] = -dc_acc[...]

    full = lambda blk: pl.BlockSpec((1, T, _LANES), lambda b, p, i, blk=blk: (b, 0, blk * npair + p))
    part = lambda blk: pl.BlockSpec((1, tq, _LANES), lambda b, p, i, blk=blk: (b, i, blk * npair + p))
    (dq, dk, dv, dcum), got = _fused_call(
        body, name=name,
        out_shape=(jax.ShapeDtypeStruct((B, T, W), _MXU), jax.ShapeDtypeStruct((B, T, W), _MXU),
                   jax.ShapeDtypeStruct((B, T, W), _MXU), jax.ShapeDtypeStruct((B, npair, 2, T), _F32)),
        grid=(B, npair, nq),
        in_specs=[part(0), full(1), full(2), pl.BlockSpec((1, H, T), lambda b, p, i: (b, 0, 0)), part(0),
                  pl.BlockSpec((1, 2, tq, _LANES), lambda b, p, i: (b, p, i, 0))],
        out_specs=(part(0), full(0), full(0), pl.BlockSpec((1, 1, 2, T), lambda b, p, i: (b, p, 0, 0))),
        scratch_shapes=[pltpu.VMEM((2, nq, tq, tq), _F32), pltpu.VMEM((2, nq, tq, tq), _F32), pltpu.VMEM((T, _LANES), _F32),
                        pltpu.VMEM((T, _LANES), _F32), pltpu.VMEM((2, T), _F32)],
        sem=("parallel", "parallel", "arbitrary"), args=(proj3, proj3, proj3, cum, do3, lse), xchg=xchg)
    return dq, dk, dv, dcum, got


def _xa_probs(qh, kh, scale):
    s = _dot(qh, kh, 1, 1) * scale
    e = jnp.exp(s - jnp.max(s, axis=1, keepdims=True))
    return e / jnp.sum(e, axis=1, keepdims=True)


def _xa_fwd(q3, kv3, *, name):
    B, T, D = q3.shape
    M = kv3.shape[1]
    hd = D // _XA_HEADS
    scale = hd ** -0.5
    tq = _rows(T, 512)

    def body(q_ref, kv_ref, o_ref):
        for h in range(_XA_HEADS):
            sl = slice(h * hd, (h + 1) * hd)
            p = _xa_probs(q_ref[0, :, sl], kv_ref[0, :, sl], scale)
            o_ref[0, :, sl] = _dot(p.astype(_MXU), kv_ref[0, :, D + h * hd:D + (h + 1) * hd], 1, 0).astype(o_ref.dtype)

    return pl.pallas_call(
        body, name=name, out_shape=jax.ShapeDtypeStruct((B, T, D), _MXU), grid=(B, T // tq),
        in_specs=[pl.BlockSpec((1, tq, D), lambda b, i: (b, i, 0)), pl.BlockSpec((1, M, 2 * D), lambda b, i: (b, 0, 0))],
        out_specs=pl.BlockSpec((1, tq, D), lambda b, i: (b, i, 0)), compiler_params=_params("parallel", "parallel"),
    )(q3, kv3)


def _xa_bwd(q3, kv3, do3, *, name):
    B, T, D = q3.shape
    M = kv3.shape[1]
    hd = D // _XA_HEADS
    scale = hd ** -0.5
    tq = _rows(T, 512)

    def body(q_ref, kv_ref, do_ref, dq_ref, dkv_ref):
        @pl.when(pl.program_id(1) == 0)
        def _():
            dkv_ref[...] = jnp.zeros_like(dkv_ref)

        for h in range(_XA_HEADS):
            sl = slice(h * hd, (h + 1) * hd)
            slv = slice(D + h * hd, D + (h + 1) * hd)
            qh, kh, vh, doh = q_ref[0, :, sl], kv_ref[0, :, sl], kv_ref[0, :, slv], do_ref[0, :, sl]
            p = _xa_probs(qh, kh, scale)
            dkv_ref[0, :, slv] += _dot(p.astype(_MXU), doh, 0, 0)
            dp = _dot(doh, vh, 1, 1)
            ds = (p * (dp - jnp.sum(p * dp, axis=1, keepdims=True))).astype(_MXU)
            dq_ref[0, :, sl] = (_dot(ds, kh, 1, 0) * scale).astype(dq_ref.dtype)
            dkv_ref[0, :, sl] += _dot(ds, qh, 0, 0) * scale

    blk = pl.BlockSpec((1, tq, D), lambda b, i: (b, i, 0))
    kvs = pl.BlockSpec((1, M, 2 * D), lambda b, i: (b, 0, 0))
    return pl.pallas_call(
        body, name=name,
        out_shape=(jax.ShapeDtypeStruct((B, T, D), _MXU), jax.ShapeDtypeStruct((B, M, 2 * D), _F32)),
        grid=(B, T // tq), in_specs=[blk, kvs, blk], out_specs=(blk, kvs),
        compiler_params=_params("parallel", "arbitrary"),
    )(q3, kv3, do3)


def _rotated_copies(ext, rot, tt):
    rot[0] = ext[...]
    for b in range(1, 8):
        rot[b, 0:tt + _HALO - 8, :] = ext[b:b + tt + _HALO - 8, :]


def _shifted(rot, off, r0, rows, c0):
    a, b = divmod(off, 8)
    return rot[b, 8 * a + r0:8 * a + r0 + rows, c0:c0 + _LANES]


def _conv_fwd(y3, dw_w, dw_b, ln_g, ln_b, *, name, xchg=None):
    B, T, C = y3.shape
    tt = _rows(T, 256)
    nt = T // tt

    def body(prev_ref, cur_ref, w_ref, b_ref, g_ref, lb_ref, y2_ref, y4_ref, ext, rot):
        i = pl.program_id(1)
        ext[0:_HALO, :] = jnp.where(i > 0, prev_ref[0, tt - _HALO:tt, :], 0.0)
        ext[_HALO:_HALO + tt, :] = cur_ref[0]
        _rotated_copies(ext, rot, tt)
        for c0 in range(0, C, _LANES):
            acc = jnp.broadcast_to(b_ref[:, c0:c0 + _LANES], (tt, _LANES))
            for j in range(_CONV_K):
                acc = acc + w_ref[j:j + 1, c0:c0 + _LANES] * _shifted(rot, _HALO - (_CONV_K - 1) + j, 0, tt, c0)
            y2_ref[0, :, c0:c0 + _LANES] = acc
        xh, _ = _layernorm_stats(y2_ref[0])
        z = xh * g_ref[...] + lb_ref[...]
        y4_ref[0] = (z * _sigmoid(z)).astype(y4_ref.dtype)

    vec = pl.BlockSpec((1, C), lambda b, i: (0, 0))
    blk = pl.BlockSpec((1, tt, C), lambda b, i: (b, i, 0))
    (y2, y4), got = _fused_call(
        body, name=name,
        out_shape=(jax.ShapeDtypeStruct((B, T, C), _F32), jax.ShapeDtypeStruct((B, T, C), _MXU)),
        grid=(B, nt),
        in_specs=[pl.BlockSpec((1, tt, C), lambda b, i: (b, jnp.maximum(i - 1, 0), 0)), blk,
                  pl.BlockSpec((_HALO, C), lambda b, i: (0, 0)), vec, vec, vec],
        out_specs=(blk, blk),
        scratch_shapes=[pltpu.VMEM((tt + _HALO, C), _F32), pltpu.VMEM((8, tt + _HALO, C), _F32)],
        sem=("parallel", "parallel"), args=(y3, y3, dw_w, dw_b.reshape(1, C), ln_g.reshape(1, C), ln_b.reshape(1, C)), xchg=xchg)
    return y2, y4, got


def _conv_ln_bwd(y2, dx, w_out, ln_g, ln_b, *, name):
    N, C = y2.shape
    D = dx.shape[1]
    tr = _rows(N, 512)

    def body(y_ref, dx_ref, w_ref, g_ref, b_ref, dy_ref, dg_ref, db_ref, dwb_ref):
        @pl.when(pl.program_id(0) == 0)
        def _():
            dg_ref[...] = jnp.zeros_like(dg_ref)
            db_ref[...] = jnp.zeros_like(db_ref)
            dwb_ref[...] = jnp.zeros_like(dwb_ref)

        dy4 = _dot(dx_ref[...].astype(_MXU), w_ref[...].astype(_MXU), 1, 1)
        xh, rstd = _layernorm_stats(y_ref[...])
        gv = g_ref[...]
        z = xh * gv + b_ref[...]
        sg = _sigmoid(z)
        dz = dy4 * (sg * (1.0 + z * (1.0 - sg)))
        dg_ref[...] += jnp.sum(dz * xh, axis=0, keepdims=True)
        db_ref[...] += jnp.sum(dz, axis=0, keepdims=True)
        dxh = dz * gv
        dy = rstd * (dxh - jnp.mean(dxh, axis=-1, keepdims=True) - xh * jnp.mean(dxh * xh, axis=-1, keepdims=True))
        dwb_ref[...] += jnp.sum(dy, axis=0, keepdims=True)
        dy_ref[...] = dy

    row = pl.BlockSpec((tr, C), lambda i: (i, 0))
    vec = pl.BlockSpec((1, C), lambda i: (0, 0))
    v = jax.ShapeDtypeStruct((1, C), _F32)
    return pl.pallas_call(
        body, name=name, out_shape=(jax.ShapeDtypeStruct((N, C), _F32), v, v, v), grid=(N // tr,),
        in_specs=[row, pl.BlockSpec((tr, D), lambda i: (i, 0)), pl.BlockSpec((None, C, D), lambda i: (0, 0, 0)), vec, vec],
        out_specs=(row, vec, vec, vec), compiler_params=_params("arbitrary"),
    )(y2, dx, w_out, ln_g.reshape(1, C), ln_b.reshape(1, C))


def _conv_bwd(y3, dy23, ag3, dw_w, *, name, xchg=None):
    B, T, C = y3.shape
    tt = _rows(T, 256)
    nt = T // tt

    rs = _rows(tt, _CONV_ROWS)

    def groups(v):
        return jnp.sum(v.reshape(rs // 8, 8, _LANES), axis=0)

    def body(yp_ref, yc_ref, dc_ref, dn_ref, a_ref, g_ref, w_ref, dag_ref, dw_ref, dbin_ref, yext, dext, yrot, drot, dw_acc, db_acc):
        b = pl.program_id(0)
        i = pl.program_id(1)

        @pl.when((b == 0) & (i == 0))
        def _():
            dw_acc[...] = jnp.zeros_like(dw_acc)
            db_acc[...] = jnp.zeros_like(db_acc)

        yext[0:_HALO, :] = jnp.where(i > 0, yp_ref[0, tt - _HALO:tt, :], 0.0)
        yext[_HALO:_HALO + tt, :] = yc_ref[0]
        dext[0:tt, :] = dc_ref[0]
        dext[tt:tt + _HALO, :] = jnp.where(i < nt - 1, dn_ref[0, 0:_HALO, :], 0.0)
        _rotated_copies(yext, yrot, tt)
        _rotated_copies(dext, drot, tt)
        for c0 in range(0, C, _LANES):
            for r0 in range(0, tt, rs):
                d_cur = dext[r0:r0 + rs, c0:c0 + _LANES]
                dy = jnp.zeros((rs, _LANES), _F32)
                for j in range(_CONV_K):
                    sh = _CONV_K - 1 - j
                    dy = dy + w_ref[j:j + 1, c0:c0 + _LANES] * _shifted(drot, sh, r0, rs, c0)
                    dw_acc[j, :, c0:c0 + _LANES] += groups(d_cur * _shifted(yrot, _HALO - sh, r0, rs, c0))
                a, g = a_ref[0, r0:r0 + rs, c0:c0 + _LANES], g_ref[0, r0:r0 + rs, c0:c0 + _LANES]
                sg = _sigmoid(g)
                da = dy * sg
                dg = dy * a * (sg * (1.0 - sg))
                dag_ref[0, r0:r0 + rs, c0:c0 + _LANES] = da.astype(dag_ref.dtype)
                dag_ref[0, r0:r0 + rs, C + c0:C + c0 + _LANES] = dg.astype(dag_ref.dtype)
                db_acc[:, c0:c0 + _LANES] += groups(da)
                db_acc[:, C + c0:C + c0 + _LANES] += groups(dg)

        @pl.when((b == B - 1) & (i == nt - 1))
        def _():
            dw_ref[...] = jnp.sum(dw_acc[...], axis=1)
            dbin_ref[...] = jnp.sum(db_acc[...], axis=0, keepdims=True)

    blk = pl.BlockSpec((1, tt, C), lambda b, i: (b, i, 0))
    (dag, ddw, dbin), got = _fused_call(
        body, name=name,
        out_shape=(jax.ShapeDtypeStruct((B, T, 2 * C), _MXU), jax.ShapeDtypeStruct((_HALO, C), _F32),
                   jax.ShapeDtypeStruct((1, 2 * C), _F32)),
        grid=(B, nt),
        in_specs=[pl.BlockSpec((1, tt, C), lambda b, i: (b, jnp.maximum(i - 1, 0), 0)), blk, blk,
                  pl.BlockSpec((1, tt, C), lambda b, i: (b, jnp.minimum(i + 1, nt - 1), 0)),
                  pl.BlockSpec((None, 1, tt, C), lambda b, i: (0, b, i, 0)), pl.BlockSpec((None, 1, tt, C), lambda b, i: (1, b, i, 0)),
                  pl.BlockSpec((_HALO, C), lambda b, i: (0, 0))],
        out_specs=(pl.BlockSpec((1, tt, 2 * C), lambda b, i: (b, i, 0)), pl.BlockSpec((_HALO, C), lambda b, i: (0, 0)),
                   pl.BlockSpec((1, 2 * C), lambda b, i: (0, 0))),
        scratch_shapes=[pltpu.VMEM((tt + _HALO, C), _F32), pltpu.VMEM((tt + _HALO, C), _F32),
                        pltpu.VMEM((8, tt + _HALO, C), _F32), pltpu.VMEM((8, tt + _HALO, C), _F32),
                        pltpu.VMEM((_HALO, 8, C), _F32), pltpu.VMEM((8, 2 * C), _F32)],
        sem=("arbitrary", "arbitrary"), args=(y3, y3, dy23, dy23, ag3, ag3, dw_w), xchg=xchg)
    return dag, ddw, dbin, got


class _Exchange:
    def __init__(self, items):
        self.per_peer = [pp for _, pp in items]
        self.srcs, self.out_shapes, self.pieces = [], [], []
        for t, (srcs, per_peer) in enumerate(items):
            blk = srcs[0].shape[1:] if per_peer else srcs[0].shape
            self.out_shapes.append(jax.ShapeDtypeStruct((len(srcs), _N_DEV) + tuple(blk), srcs[0].dtype))
            for l, s in enumerate(srcs):
                self.pieces.append((t, l, len(self.srcs)))
                self.srcs.append(s)
        self.n_src, self.n_dst, n_pc = len(self.srcs), len(items), len(self.pieces)
        self.in_specs = [pl.BlockSpec(memory_space=pl.ANY)] * self.n_src
        self.out_specs = [pl.BlockSpec(memory_space=pl.ANY)] * self.n_dst
        self.scratch = [pltpu.SemaphoreType.DMA((n_pc, _N_DEV - 1)), pltpu.SemaphoreType.DMA((n_pc, _N_DEV - 1)),
                        pltpu.SemaphoreType.DMA((n_pc,))]

    def _copies(self, src_refs, dst_refs, sems, kind):
        send_sems, recv_sems, loc_sems = sems
        x, y, c = lax.axis_index("x"), lax.axis_index("y"), lax.axis_index("c")
        me = 4 * x + 2 * y + c
        out = []
        for i, (t, l, s) in enumerate(self.pieces):
            def src_for(p, s=s, t=t):
                return src_refs[s].at[p] if self.per_peer[t] else src_refs[s]

            if kind == "local":
                out.append(pltpu.make_async_copy(src_for(me), dst_refs[t].at[l, me], loc_sems.at[i]))
                continue
            for k in range(1, _N_DEV):
                px, py, pc = (1 - x if k & 4 else x), (1 - y if k & 2 else y), (1 - c if k & 1 else c)
                p = 4 * px + 2 * py + pc
                out.append(pltpu.make_async_remote_copy(
                    src_ref=src_for(p), dst_ref=dst_refs[t].at[l, p if kind == "recv" else me],
                    send_sem=send_sems.at[i, k - 1], recv_sem=recv_sems.at[i, k - 1],
                    device_id=(px, py, pc), device_id_type=pl.DeviceIdType.MESH))
        return out

    def start(self, src_refs, dst_refs, sems):
        for cp in self._copies(src_refs, dst_refs, sems, "local") + self._copies(src_refs, dst_refs, sems, "send"):
            cp.start()

    def finish(self, src_refs, dst_refs, sems):
        for cp in self._copies(src_refs, dst_refs, sems, "send"):
            cp.wait_send()
        for cp in self._copies(src_refs, dst_refs, sems, "recv"):
            cp.wait_recv()
        for cp in self._copies(src_refs, dst_refs, sems, "local"):
            cp.wait()


def _exchange(items, *, name):
    ex = _Exchange(items)

    def body(*refs):
        parts = refs[:ex.n_src], refs[ex.n_src:ex.n_src + ex.n_dst], refs[ex.n_src + ex.n_dst:]
        ex.start(*parts)
        ex.finish(*parts)

    return pl.pallas_call(
        body, name=name, out_shape=ex.out_shapes, in_specs=ex.in_specs, out_specs=ex.out_specs, scratch_shapes=ex.scratch,
        compiler_params=pltpu.CompilerParams(has_side_effects=True),
    )(*ex.srcs)


def _adam_update(g, w, m, v):
    c1 = 1.0 / (1.0 - _ADAM_B1 ** _ADAM_STEP)
    c2 = 1.0 / (1.0 - _ADAM_B2 ** _ADAM_STEP)
    m2 = _ADAM_B1 * m + (1.0 - _ADAM_B1) * g
    v2 = _ADAM_B2 * v + (1.0 - _ADAM_B2) * (g * g)
    return -_ADAM_LR * ((m2 * c1) / (jnp.sqrt(v2 * c2) + _ADAM_EPS) + _ADAM_WD * w), m2, v2


def _adamw_big(recvs, w, m, v, *, name):
    L, R, C = w.shape
    tr = _rows(R, 256)
    nb = R // tr

    def body(*refs):
        r_refs = refs[:L]
        w_ref, m_ref, v_ref, g_ref, d_ref, mo_ref, vo_ref = refs[L:]
        for l in range(L):
            @pl.when(pl.program_id(0) == l)
            def _(r_ref=r_refs[l]):
                g = r_ref[0, 0].astype(_F32)
                for k in range(1, _N_DEV):
                    g = g + r_ref[0, k].astype(_F32)
                g_ref[0] = g
                d_ref[0], mo_ref[0], vo_ref[0] = _adam_update(g, w_ref[0], m_ref[0], v_ref[0])

    def recv_spec(l):
        return pl.BlockSpec((1, _N_DEV, tr, C), lambda ll, i: (0, 0, jnp.where(ll == l, i, jnp.where(ll < l, 0, nb - 1)), 0))

    blk = pl.BlockSpec((1, tr, C), lambda l, i: (l, i, 0))
    o = jax.ShapeDtypeStruct((L, R, C), _F32)
    return pl.pallas_call(
        body, name=name, out_shape=(o, o, o, o), grid=(L, nb),
        in_specs=[recv_spec(l) for l in range(L)] + [blk, blk, blk], out_specs=(blk, blk, blk, blk),
        compiler_params=_params("arbitrary", "arbitrary"),
    )(*recvs, w, m, v)


def _adamw_small(tensors, *, name):
    n = len(tensors)
    lanes = [t[4] for t in tensors]
    layers = [len(t[0]) for t in tensors]

    def body(*refs):
        pos = 0
        ins = []
        for t in range(n):
            ins.append((refs[pos:pos + layers[t]], *refs[pos + layers[t]:pos + layers[t] + 3]))
            pos += layers[t] + 3
        outs = refs[pos:]
        for t in range(n):
            r_refs, w_ref, m_ref, v_ref = ins[t]
            g_ref, d_ref, mo_ref, vo_ref = outs[4 * t:4 * t + 4]
            for l in range(layers[t]):
                g = r_refs[l][0, 0]
                for k in range(1, _N_DEV):
                    g = g + r_refs[l][0, k]
                if lanes[t] is not None:
                    g = g[..., :lanes[t]]
                g_ref[l] = g
                d_ref[l], mo_ref[l], vo_ref[l] = _adam_update(g, w_ref[l], m_ref[l], v_ref[l])

    args, out_shape = [], []
    for recvs, w, m, v, _ in tensors:
        args += [*recvs, w, m, v]
        out_shape += [jax.ShapeDtypeStruct(w.shape, _F32)] * 4
    outs = pl.pallas_call(
        body, name=name, out_shape=out_shape,
        in_specs=[pl.BlockSpec(memory_space=pltpu.VMEM)] * len(args), out_specs=[pl.BlockSpec(memory_space=pltpu.VMEM)] * len(out_shape),
        compiler_params=_params(),
    )(*args)
    return [tuple(outs[4 * t:4 * t + 4]) for t in range(n)]


_BIG = (("w_in_e", 2), ("w_out_e", 1), ("conv_w_in", 2), ("conv_w_out", 1), ("xa_wq", 1), ("xa_wkv", 2), ("xa_wo", 1),
        ("ffn_w_gu", 2), ("ffn_w_down", 1))
_SMALL_SHARDED = (("mix_norm_o", 1), ("conv_b_in", 1), ("conv_dw_w", 2), ("conv_dw_b", 1), ("conv_ln_g", 1),
                  ("conv_ln_b", 1), ("conv_b_out", 1))
_REPLICATED = ("mix_norm_e", "fox_f_bias", "gmlp_ln_g", "gmlp_ln_b", "gmlp_w_s", "gmlp_b_s", "xa_norm", "mem_norm",
               "ffn_norm", "final_norm")
_WEIGHTS = ("mix_norm_e", "w_in_e", "fox_f_bias", "gmlp_ln_g", "gmlp_ln_b", "gmlp_w_s", "gmlp_b_s", "w_out_e", "mix_norm_o",
            "conv_w_in", "conv_b_in", "conv_dw_w", "conv_dw_b", "conv_ln_g", "conv_ln_b", "conv_w_out", "conv_b_out",
            "xa_norm", "mem_norm", "xa_wq", "xa_wkv", "xa_wo", "ffn_norm", "ffn_w_gu", "ffn_w_down", "final_norm")


def _cols_to_peers(g, n=_N_DEV):
    K, N = g.shape[-2:]
    return jnp.swapaxes(g.reshape(g.shape[:-1] + (n, N // n)), -3, -2)


def _weight_items(pieces, wsrc):
    return [([wsrc[n][l]] if n in dict(_BIG) else [wsrc[n]], False) for n, l in pieces]


def _place_weights(P, pieces, gathered):
    axis = dict(_BIG + _SMALL_SHARDED)
    for (n, l), g in zip(pieces, gathered):
        if n in dict(_BIG):
            P.setdefault(n, {})[l] = g.reshape(1, -1, g.shape[-1]) if axis[n] == 1 else _peers_to_cols(g)
        else:
            P[n] = _peers_to_cols(g[0, :, 0])[None] if axis[n] == 2 else g.reshape(1, -1)


def _grad_items(pieces, G):
    axis = dict(_BIG + _SMALL_SHARDED)
    items = []
    for n, l in pieces:
        g = G[n][l]
        if n in _REPLICATED:
            items.append(([g], False))
        elif n == "ffn_w_gu":
            half = _N_DEV // 2
            items.append(([jnp.concatenate([_cols_to_peers(g[0], half), _cols_to_peers(g[1], half)], axis=0)], True))
        elif n in dict(_BIG):
            items.append(([g.reshape(_N_DEV, -1, g.shape[-1]) if axis[n] == 1 else _cols_to_peers(g)], True))
        else:
            items.append(([_cols_to_peers(g) if axis[n] == 2 else g.reshape(_N_DEV, 1, -1)], True))
    return items


def _peers_to_cols(d):
    K, c = d.shape[-2:]
    return jnp.swapaxes(d, -3, -2).reshape(d.shape[:-3] + (K, _N_DEV * c))


def _local_step(x, mem, tgt, P, wsrc=None, fwd_hooks=None, bwd_hooks=None):
    fwd_hooks, bwd_hooks = fwd_hooks or {}, bwd_hooks or {}
    sent = {}

    def gather(kernel_name):
        return _Exchange(_weight_items(fwd_hooks[kernel_name], wsrc)) if kernel_name in fwd_hooks else None

    def placed(kernel_name, got):
        if kernel_name in fwd_hooks:
            _place_weights(P, fwd_hooks[kernel_name], got)

    def scatter(kernel_name):
        return _Exchange(_grad_items(bwd_hooks[kernel_name], G)) if kernel_name in bwd_hooks else None

    def received(kernel_name, got):
        if kernel_name in bwd_hooks:
            sent.update(zip(bwd_hooks[kernel_name], got))

    def mm(a, b, *, name, **kw):
        ex = gather(name) or scatter(name)
        out = _mm(a, b, name=name, xchg=ex, **kw)
        if ex is None:
            return out
        out, got = out
        placed(name, got)
        received(name, got)
        return out

    B, T, D = x.shape
    M = mem.shape[1]
    N = B * T
    W = D // 2
    H = W // _FOX_HD
    f_blk = 5 * W // _LANES
    G = {}
    x0 = x.reshape(N, D)
    memf = mem.reshape(B * M, D)

    h_e = _rms_fwd(x0, P["mix_norm_e"][0], name="rms_mix_e", xchg=gather("rms_mix_e"))
    if "rms_mix_e" in fwd_hooks:
        h_e, got = h_e
        placed("rms_mix_e", got)
    w_in_pad = _pad_w_in(P["w_in_e"][0][0], W, H)[None]
    proj = mm(h_e, w_in_pad, bl=0, name="mm_in_e", tn=896)
    proj3 = proj.reshape(B, T, -1)
    cum = _fox_cum(proj3, f_blk, P["fox_f_bias"][0], name="fox_cum")
    o_fox, lse, got = _fox_fwd(proj3, cum, name="fox_fwd", xchg=gather("fox_fwd"))
    placed("fox_fwd", got)
    bias_full = jnp.repeat(P["gmlp_b_s"][0].T, _GRP, axis=1)
    a_out = _gmlp_fwd(proj, P["gmlp_ln_g"][0], P["gmlp_ln_b"][0], P["gmlp_w_s"][0], bias_full, name="gmlp_fwd")
    mixcat = jnp.concatenate([o_fox.reshape(N, W), a_out], axis=1)
    x1 = mm(mixcat, P["w_out_e"][0], bl=0, res=x0, name="mm_out_e")

    def xa_ffn_fwd(xin, l, last=False):
        s = {}
        s["q"], s["h_xa"] = mm(xin, P["xa_wq"][l], bl=0, rms_fwd=P["xa_norm"][l], out_dtype=_MXU, name=f"mm_q{l}")
        s["mn"] = _rms_fwd(memf, P["mem_norm"][l], name=f"rms_mem{l}")
        s["kv"] = mm(s["mn"], P["xa_wkv"][l], bl=0, out_dtype=_MXU, name=f"mm_kv{l}")
        s["o"] = _xa_fwd(s["q"].reshape(B, T, D), s["kv"].reshape(B, M, 2 * D), name=f"xa_fwd{l}").reshape(N, D)
        s["x_mid"] = mm(s["o"], P["xa_wo"][l], bl=0, res=xin, name=f"mm_o{l}")
        s["gu"], s["act"], s["h_ffn"], got = _mm_gu(s["x_mid"], P["ffn_w_gu"][l], 0, rms_fwd=P["ffn_norm"][l], name=f"mm_gu{l}",
                                                    xchg=gather(f"mm_gu{l}"))
        placed(f"mm_gu{l}", got)
        s["x_in"] = xin
        if last:
            return None, s
        xout = mm(s["act"], P["ffn_w_down"][l], bl=0, res=s["x_mid"], name=f"mm_down{l}", tn=512)
        return xout, s

    x3, s0 = xa_ffn_fwd(x1, 0)
    ag, y, h_o, _ = _mm_gu(x3, P["conv_w_in"][0], 0, bias=P["conv_b_in"][0], glu=True, keep=_F32, rms_fwd=P["mix_norm_o"][0],
                           name="mm_conv_in")
    C = y.shape[1]
    dw_w = jnp.pad(P["conv_dw_w"][0], ((0, _HALO - _CONV_K), (0, 0)))
    y2, y4, got = _conv_fwd(y.reshape(B, T, C), dw_w, P["conv_dw_b"][0], P["conv_ln_g"][0], P["conv_ln_b"][0], name="conv_fwd",
                            xchg=gather("conv_fwd"))
    placed("conv_fwd", got)
    x4 = mm(y4.reshape(N, C), P["conv_w_out"][0], bl=0, bias=P["conv_b_out"][0], res=x3, name="mm_conv_out")
    _, s1 = xa_ffn_fwd(x4, 1, last=True)
    loss, dx, dg = _final_loss(s1["act"], P["ffn_w_down"][1], s1["x_mid"], P["final_norm"], tgt.reshape(N, D), name="final_loss")
    G["final_norm"] = [dg]

    def xa_ffn_bwd(dx, s, l):
        for k in ("ffn_w_down", "ffn_w_gu", "ffn_norm", "xa_wo", "xa_wq", "xa_norm", "xa_wkv", "mem_norm"):
            G.setdefault(k, {})
        dgu = _mm_dgu(dx, P["ffn_w_down"][l], 0, s["gu"], name=f"mm_dgu{l}")
        G["ffn_w_down"][l] = mm(s["act"], dx, ta=True, tk=_DW_ROWS, out_dtype=_MXU, name=f"mm_dwdown{l}", tm=1408)
        G["ffn_w_gu"][l] = (mm(s["h_ffn"], dgu, ta=True, tk=_DW_ROWS, bl=0, out_dtype=_MXU, name=f"mm_dwg{l}", tn=1408),
                            mm(s["h_ffn"], dgu, ta=True, tk=_DW_ROWS, bl=1, out_dtype=_MXU, name=f"mm_dwu{l}", tn=1408))
        dx, G["ffn_norm"][l] = mm(dgu, P["ffn_w_gu"][l], al="cat", bl=0, tb=True, rms_bwd=(s["x_mid"], P["ffn_norm"][l], dx),
                                  name=f"mm_dhffn{l}", tm=512, tn=D, tk=2816)
        do = mm(dx, P["xa_wo"][l], bl=0, tb=True, out_dtype=_MXU, name=f"mm_do{l}")
        G["xa_wo"][l] = mm(s["o"], dx, ta=True, tk=_DW_ROWS, out_dtype=_MXU, name=f"mm_dwo{l}")
        dq, dkv = _xa_bwd(s["q"].reshape(B, T, D), s["kv"].reshape(B, M, 2 * D), do.reshape(B, T, D), name=f"xa_bwd{l}")
        dq, dkv = dq.reshape(N, D), dkv.reshape(B * M, 2 * D)
        G["xa_wq"][l] = mm(s["h_xa"], dq, ta=True, tk=_DW_ROWS, out_dtype=_MXU, name=f"mm_dwq{l}")
        dx, G["xa_norm"][l] = mm(dq, P["xa_wq"][l], bl=0, tb=True, rms_bwd=(s["x_in"], P["xa_norm"][l], dx), name=f"mm_dhxa{l}",
                                 tm=512, tn=D)
        G["xa_wkv"][l] = mm(s["mn"], dkv, ta=True, tk=_DW_ROWS, out_dtype=_MXU, name=f"mm_dwkv{l}")
        dmn = mm(dkv, P["xa_wkv"][l], bl=0, tb=True, name=f"mm_dmn{l}")
        G["mem_norm"][l] = _rms_gain_bwd(memf, dmn, name=f"rms_mem_bwd{l}")
        return dx

    dx = xa_ffn_bwd(dx, s1, 1)
    G["conv_b_out"] = [_colsum(dx, name="colsum_b_out")]
    G["conv_w_out"] = [mm(y4.reshape(N, C), dx, ta=True, tk=_DW_ROWS, out_dtype=_MXU, name="mm_dwconv_out")]
    dy2, dlg, dlb, ddb = _conv_ln_bwd(y2.reshape(N, C), dx, P["conv_w_out"][0], P["conv_ln_g"][0], P["conv_ln_b"][0],
                                      name="conv_ln_bwd")
    G["conv_ln_g"], G["conv_ln_b"], G["conv_dw_b"] = [dlg], [dlb], [ddb]
    dag, ddw, dbin, got = _conv_bwd(y.reshape(B, T, C), dy2.reshape(B, T, C), ag.reshape(2, B, T, C), dw_w, name="conv_bwd",
                                    xchg=scatter("conv_bwd"))
    received("conv_bwd", got)
    G["conv_dw_w"], G["conv_b_in"] = [ddw[:_CONV_K]], [dbin]
    dag = dag.reshape(N, 2 * C)
    G["conv_w_in"] = [mm(h_o, dag, ta=True, tk=_DW_ROWS, out_dtype=_MXU, name="mm_dwconv_in")]
    dx, dg = mm(dag, P["conv_w_in"][0], bl=0, tb=True, rms_bwd=(x3, P["mix_norm_o"][0], dx), name="mm_dh_o", tm=512, tn=D)
    G["mix_norm_o"] = [dg]
    dx = xa_ffn_bwd(dx, s0, 0)
    G["w_out_e"] = [mm(mixcat, dx, ta=True, tk=_DW_ROWS, out_dtype=_MXU, name="mm_dwout_e")]
    dmix = mm(dx, P["w_out_e"][0], bl=0, tb=True, name="mm_dmix")
    dz, dlg, dlb, dws, dbias = _gmlp_bwd(proj, dmix, 1, P["gmlp_ln_g"][0], P["gmlp_ln_b"][0], P["gmlp_w_s"][0], bias_full,
                                         name="gmlp_bwd")
    G["gmlp_ln_g"], G["gmlp_ln_b"], G["gmlp_w_s"] = [dlg], [dlb], [dws]
    G["gmlp_b_s"] = [dbias[:, :2 * (W // _LANES)].T]
    dmix3 = dmix.reshape(B, T, D)
    dq, dk, dv, dcum, got = _fox_bwd(proj3, cum, dmix3, lse, name="fox_bwd", xchg=scatter("fox_bwd"))
    received("fox_bwd", got)
    df, dfb = _fox_cum_bwd(proj3, f_blk, P["fox_f_bias"][0], dcum.reshape(B, H, T), name="fox_cum_bwd")
    G["fox_f_bias"] = [dfb]
    dproj = jnp.concatenate([dq.reshape(N, W), dk.reshape(N, W), dv.reshape(N, W), dz, df.reshape(N, _LANES).astype(_MXU)], axis=1)
    G["w_in_e"] = [_unpad_w_in(mm(h_e, dproj, ta=True, tk=_DW_ROWS, out_dtype=_MXU, name="mm_dwin_e", tn=896), W, H)]
    dx, dg = mm(dproj, w_in_pad, bl=0, tb=True, rms_bwd=(x0, P["mix_norm_e"][0], dx), name="mm_dh_e", tm=512, tn=D)
    G["mix_norm_e"] = [dg]
    return loss, dx.reshape(B, T, D), G, sent


def _pad_w_in(w_in, W, H):
    f = w_in[:, 3 * W:3 * W + H]
    return jnp.concatenate([w_in[:, :3 * W], w_in[:, 3 * W + H:], jnp.pad(f, ((0, 0), (0, _LANES - H)))], axis=1)


def _unpad_w_in(g, W, H):
    return jnp.concatenate([g[:, :3 * W], g[:, 5 * W:5 * W + H], g[:, 3 * W:5 * W]], axis=1)


def kernel(x, mem, mix_norm_e, w_in_e, fox_f_bias, gmlp_ln_g, gmlp_ln_b, gmlp_w_s, gmlp_b_s, w_out_e, mix_norm_o, conv_w_in, conv_b_in, conv_dw_w, conv_dw_b, conv_ln_g, conv_ln_b, conv_w_out, conv_b_out, xa_norm, mem_norm, xa_wq, xa_wkv, xa_wo, ffn_norm, ffn_w_gu, ffn_w_down, final_norm, loss_target, m_mix_norm_e, m_w_in_e, m_fox_f_bias, m_gmlp_ln_g, m_gmlp_ln_b, m_gmlp_w_s, m_gmlp_b_s, m_w_out_e, m_mix_norm_o, m_conv_w_in, m_conv_b_in, m_conv_dw_w, m_conv_dw_b, m_conv_ln_g, m_conv_ln_b, m_conv_w_out, m_conv_b_out, m_xa_norm, m_mem_norm, m_xa_wq, m_xa_wkv, m_xa_wo, m_ffn_norm, m_ffn_w_gu, m_ffn_w_down, m_final_norm, v_mix_norm_e, v_w_in_e, v_fox_f_bias, v_gmlp_ln_g, v_gmlp_ln_b, v_gmlp_w_s, v_gmlp_b_s, v_w_out_e, v_mix_norm_o, v_conv_w_in, v_conv_b_in, v_conv_dw_w, v_conv_dw_b, v_conv_ln_g, v_conv_ln_b, v_conv_w_out, v_conv_b_out, v_xa_norm, v_mem_norm, v_xa_wq, v_xa_wkv, v_xa_wo, v_ffn_norm, v_ffn_w_gu, v_ffn_w_down, v_final_norm):
    env = dict(locals())
    w = {n: env[n] for n in _WEIGHTS}
    mom = {n: env["m_" + n] for n in _WEIGHTS}
    var = {n: env["v_" + n] for n in _WEIGHTS}
    D = x.shape[-1]
    W = D // 2
    H = W // _FOX_HD

    def layers(n):
        return w[n].shape[0] if w[n].ndim > 1 else 1

    wsrc = {n: (w[n].astype(_MXU) if n in dict(_BIG) else w[n]) for n, _ in _BIG + _SMALL_SHARDED}
    P = {n: w[n] for n in _REPLICATED}
    fwd_hooks = {
        "rms_mix_e": [("w_in_e", 0)],
        "mm_in_e": [("w_out_e", 0), ("xa_wq", 0)],
        "fox_fwd": [("xa_wkv", 0), ("ffn_w_gu", 0)],
        "mm_out_e": [("xa_wo", 0)],
        "mm_q0": [("conv_w_out", 0)],
        "mm_o0": [("xa_wq", 1)],
        "mm_gu0": [("ffn_w_down", 0), ("conv_w_in", 0)] + [(n, 0) for n, _ in _SMALL_SHARDED],
        "mm_down0": [("xa_wkv", 1)],
        "conv_fwd": [("xa_wo", 1), ("ffn_w_gu", 1)],
        "mm_gu1": [("ffn_w_down", 1)],
    }

    last = [("mix_norm_e", 0)]
    in_dh_e = [("w_in_e", 0), ("fox_f_bias", 0)]
    in_conv = [(n, 1) for n in ("ffn_w_gu", "ffn_w_down", "xa_wq", "xa_wkv", "xa_wo", "xa_norm", "mem_norm", "ffn_norm")]
    in_conv += [("final_norm", 0), ("conv_w_out", 0)]
    every = [(n, l) for n in [n for n, _ in _BIG + _SMALL_SHARDED] + list(_REPLICATED) for l in range(layers(n))]
    bwd_hooks = {"conv_bwd": in_conv, "mm_dhffn0": [("ffn_w_gu", 0)], "mm_dhxa0": [("xa_wq", 0), ("xa_wo", 0)],
                 "mm_dmix": [("xa_wkv", 0)], "mm_dh_e": in_dh_e}
    placed_pieces = last + [pc for pieces in bwd_hooks.values() for pc in pieces]
    bwd_hooks["fox_bwd"] = [pc for pc in every if pc not in placed_pieces]
    loss, grad_x, G, recv = _local_step(x, mem, loss_target, P, wsrc, fwd_hooks, bwd_hooks)
    loss = lax.psum(loss[0, 0], ("x", "y", "c"))
    recv.update(zip(last, _exchange(_grad_items(last, G), name="scatter_last")))

    def partials(n):
        return [recv[(n, l)] for l in range(layers(n))]

    res = {n: _adamw_big(partials(n), w[n], mom[n], var[n], name="adamw_" + n) for n, _ in _BIG}
    small = [n for n, _ in _SMALL_SHARDED] + list(_REPLICATED)

    def rows(a, n):
        r = recv[(n, 0)]
        return a.reshape((layers(n),) + r.shape[2:-1] + (-1,))

    outs = _adamw_small([(partials(n), rows(w[n], n), rows(mom[n], n), rows(var[n], n),
                          w[n].shape[-1] if w[n].shape[-1] != recv[(n, 0)].shape[-1] else None) for n in small], name="adamw_small")
    for n, o in zip(small, outs):
        res[n] = tuple(a.reshape(w[n].shape) for a in o)
    return (loss, grad_x, *[res[n][0] for n in _WEIGHTS], *[res[n][1] for n in _WEIGHTS],
            *[res[n][2] for n in _WEIGHTS], *[res[n][3] for n in _WEIGHTS])
```

```python
import functools
import math

import jax
import jax.numpy as jnp
from jax import lax
from jax.experimental import pallas as pl
from jax.experimental.pallas import tpu as pltpu

_F32 = jnp.float32
_MXU = jnp.bfloat16
_VMEM_LIMIT = 48 * 1024 * 1024
_LANES = 128
_DW_ROWS = 2048
_EPS = 1e-6
_N_DEV = 8
_FOX_HD = 64
_FOX_SCALE = _FOX_HD ** -0.5
_FOX_TQ = 512
_CHUNK = 128
_GRP = 64
_CONV_K = 31
_HALO = 32
_CONV_ROWS = 128
_XA_HEADS = 4
_GELU_C = math.sqrt(2.0 / math.pi)
_ADAM_LR, _ADAM_B1, _ADAM_B2, _ADAM_EPS, _ADAM_WD, _ADAM_STEP = 0.001, 0.9, 0.999, 1e-08, 0.01, 10


def _params(*sem):
    return pltpu.CompilerParams(dimension_semantics=sem if sem else None, vmem_limit_bytes=_VMEM_LIMIT)


def _pick(n, pref):
    if n <= pref:
        return n
    best = None
    for t in range(_LANES, pref + 1, _LANES):
        if n % t == 0:
            best = t
    assert best is not None, (n, pref)
    return best


def _rows(n, pref):
    if n <= pref:
        return n
    t = pref
    while n % t:
        t //= 2
    assert t >= 8, (n, pref)
    return t


def _sigmoid(x):
    return 1.0 / (1.0 + jnp.exp(-x))


def _gelu(x):
    t = jnp.tanh(_GELU_C * (x + 0.044715 * (x * x * x)))
    return 0.5 * x * (1.0 + t)


def _gelu_grad(x):
    x2 = x * x
    t = jnp.tanh(_GELU_C * (x + 0.044715 * (x2 * x)))
    return 0.5 * (1.0 + t) + 0.5 * x * (1.0 - t * t) * (_GELU_C * (1.0 + 3.0 * 0.044715 * x2))


def _dot(a, b, ca, cb):
    return lax.dot_general(a, b, (((ca,), (cb,)), ((), ())), preferred_element_type=_F32)


def _rms_rows(xv, gain):
    return (xv * lax.rsqrt(jnp.mean(xv * xv, axis=-1, keepdims=True) + _EPS) * gain).astype(_MXU)


def _mm(a, b, *, name, ta=False, tb=False, al=None, bl=None, bk0=0, bias=None, res=None, rms_bwd=None, rms_fwd=None,
        out_dtype=_F32, tm=1024, tn=512, tk=1024, xchg=None):
    if ta:
        K, M = a.shape[-2:]
    else:
        M, K = a.shape[-2:]
    if al == "cat":
        assert not ta
        K = a.shape[0] * a.shape[-1]
    if tb:
        N, K2 = b.shape[-2:]
    else:
        K2, N = b.shape[-2:]
    assert K == K2 or (tb and K2 > K), (a.shape, b.shape, ta, tb)
    tm, tn = _pick(M, tm), _pick(N, tn)
    tk = K if (not ta and K <= 2816 and K2 == K) else _pick(a.shape[-1] if al == "cat" else K, tk)
    nk = K // tk
    assert bk0 % tk == 0
    kb = bk0 // tk
    grid = (M // tm, N // tn, nk)
    if al == "cat":
        per = a.shape[-1] // tk
        a_spec = pl.BlockSpec((None, tm, tk), lambda i, j, k: (k // per, i, k % per))
    elif a.ndim == 3:
        a_spec = (pl.BlockSpec((None, tk, tm), lambda i, j, k: (al, k, i)) if ta
                  else pl.BlockSpec((None, tm, tk), lambda i, j, k: (al, i, k)))
    else:
        a_spec = pl.BlockSpec((tk, tm), lambda i, j, k: (k, i)) if ta else pl.BlockSpec((tm, tk), lambda i, j, k: (i, k))
    if b.ndim == 3:
        b_spec = (pl.BlockSpec((None, tn, tk), lambda i, j, k: (bl, j, k + kb)) if tb
                  else pl.BlockSpec((None, tk, tn), lambda i, j, k: (bl, k, j)))
    else:
        b_spec = pl.BlockSpec((tn, tk), lambda i, j, k: (j, k)) if tb else pl.BlockSpec((tk, tn), lambda i, j, k: (k, j))
    in_specs, args = [a_spec, b_spec], [a, b]
    if bias is not None:
        in_specs.append(pl.BlockSpec((1, tn), lambda i, j, k: (0, j)))
        args.append(bias.reshape(1, N).astype(_F32))
    if res is not None:
        in_specs.append(pl.BlockSpec((tm, tn), lambda i, j, k: (i, j)))
        args.append(res)
    has_bias, has_res, has_rms = bias is not None, res is not None, rms_bwd is not None
    if has_rms:
        assert tn == N, (tn, N)
        x, g, dres = rms_bwd
        in_specs += [pl.BlockSpec((tm, N), lambda i, j, k: (i, 0)), pl.BlockSpec((1, N), lambda i, j, k: (0, 0)),
                     pl.BlockSpec((tm, N), lambda i, j, k: (i, 0))]
        args += [x, g.reshape(1, N), dres]
    has_norm = rms_fwd is not None
    if has_norm:
        assert not ta and nk == 1 and a.ndim == 2
        in_specs.append(pl.BlockSpec((1, K), lambda i, j, k: (0, 0)))
        args.append(rms_fwd.reshape(1, K))

    def body(*refs):
        a_ref, b_ref = refs[0], refs[1]
        pos = 2
        bias_ref = res_ref = None
        if has_bias:
            bias_ref = refs[pos]
            pos += 1
        if has_res:
            res_ref = refs[pos]
            pos += 1
        if has_rms:
            x_ref, g_ref, dres_ref = refs[pos:pos + 3]
            pos += 3
        if has_norm:
            gain_ref = refs[pos]
            pos += 1
        o_ref = refs[pos]
        pos += 1
        if has_rms:
            dg_ref = refs[pos]
            pos += 1
        if has_norm:
            h_ref = refs[pos]
            pos += 1
        acc_ref = refs[pos] if nk > 1 else None
        first_rows = pl.program_id(0) == 0
        if has_norm:
            av = _rms_rows(a_ref[...], gain_ref[...])
            h_ref[...] = av
        else:
            av = a_ref[...].astype(_MXU)
        p = _dot(av, b_ref[...].astype(_MXU), 0 if ta else 1, 1 if tb else 0)

        def finish(acc):
            if has_bias:
                acc = acc + bias_ref[...]
            if has_res:
                acc = acc + res_ref[...]
            if has_rms:
                @pl.when(first_rows)
                def _():
                    dg_ref[...] = jnp.zeros_like(dg_ref)

                xv = x_ref[...]
                r = lax.rsqrt(jnp.mean(xv * xv, axis=-1, keepdims=True) + _EPS)
                xh = xv * r
                dg_ref[...] += jnp.sum(acc * xh, axis=0, keepdims=True)
                dxn = acc * g_ref[...]
                acc = dres_ref[...] + r * (dxn - xh * jnp.mean(dxn * xh, axis=-1, keepdims=True))
            o_ref[...] = acc.astype(o_ref.dtype)

        if nk == 1:
            finish(p)
        else:
            k = pl.program_id(2)

            @pl.when(k == 0)
            def _():
                acc_ref[...] = p

            @pl.when(k > 0)
            def _():
                acc_ref[...] += p

            @pl.when(k == nk - 1)
            def _():
                finish(acc_ref[...])

    out_shape = [jax.ShapeDtypeStruct((M, N), out_dtype)]
    out_specs = [pl.BlockSpec((tm, tn), lambda i, j, k: (i, j))]
    if has_rms:
        out_shape.append(jax.ShapeDtypeStruct((1, N), _F32))
        out_specs.append(pl.BlockSpec((1, N), lambda i, j, k: (0, 0)))
    if has_norm:
        out_shape.append(jax.ShapeDtypeStruct((M, K), _MXU))
        out_specs.append(pl.BlockSpec((tm, K), lambda i, j, k: (i, 0)))
    outs, got = _fused_call(
        body, name=name, out_shape=out_shape, grid=grid, in_specs=in_specs, out_specs=out_specs,
        scratch_shapes=[pltpu.VMEM((tm, tn), _F32)] if nk > 1 else [],
        sem=("arbitrary",) * 3 if has_rms or has_norm else ("parallel", "parallel", "arbitrary"), args=args, xchg=xchg)
    out = tuple(outs) if has_rms or has_norm else outs[0]
    return out if xchg is None else (out, got)


def _rms_fwd(x, g, *, name, xchg=None):
    N, D = x.shape
    tr = _rows(N, 512)

    def body(x_ref, g_ref, o_ref):
        xv = x_ref[...]
        r = lax.rsqrt(jnp.mean(xv * xv, axis=-1, keepdims=True) + _EPS)
        o_ref[...] = (xv * r * g_ref[...]).astype(o_ref.dtype)

    (out,), got = _fused_call(
        body, name=name, out_shape=[jax.ShapeDtypeStruct((N, D), _MXU)], grid=(N // tr,),
        in_specs=[pl.BlockSpec((tr, D), lambda i: (i, 0)), pl.BlockSpec((1, D), lambda i: (0, 0))],
        out_specs=[pl.BlockSpec((tr, D), lambda i: (i, 0))], scratch_shapes=[], sem=("parallel",),
        args=(x, g.reshape(1, D)), xchg=xchg)
    return out if xchg is None else (out, got)


def _rms_gain_bwd(x, dh, *, name):
    N, D = x.shape
    tr = _rows(N, 256)

    def body(x_ref, dh_ref, dg_ref):
        xv = x_ref[...]
        xh = xv * lax.rsqrt(jnp.mean(xv * xv, axis=-1, keepdims=True) + _EPS)

        @pl.when(pl.program_id(0) == 0)
        def _():
            dg_ref[...] = jnp.zeros_like(dg_ref)

        dg_ref[...] += jnp.sum(dh_ref[...].astype(_F32) * xh, axis=0, keepdims=True)

    row = pl.BlockSpec((tr, D), lambda i: (i, 0))
    return pl.pallas_call(
        body, name=name, out_shape=jax.ShapeDtypeStruct((1, D), _F32), grid=(N // tr,), in_specs=[row, row],
        out_specs=pl.BlockSpec((1, D), lambda i: (0, 0)), compiler_params=_params("arbitrary"),
    )(x, dh)


def _colsum(a, *, name):
    M, C = a.shape
    tr = _rows(M, 512)

    def body(a_ref, o_ref):
        @pl.when(pl.program_id(0) == 0)
        def _():
            o_ref[...] = jnp.zeros_like(o_ref)

        o_ref[...] += jnp.sum(a_ref[...].astype(_F32), axis=0, keepdims=True)

    return pl.pallas_call(
        body, name=name, out_shape=jax.ShapeDtypeStruct((1, C), _F32), grid=(M // tr,),
        in_specs=[pl.BlockSpec((tr, C), lambda i: (i, 0))], out_specs=pl.BlockSpec((1, C), lambda i: (0, 0)),
        compiler_params=_params("arbitrary"),
    )(a)


def _final_loss(act, w_down, res, g, tgt, *, name):
    N, D = res.shape
    K = act.shape[1]
    tr = _rows(N, 512)

    def body(a_ref, w_ref, res_ref, g_ref, t_ref, loss_ref, dx_ref, dg_ref):
        xv = _dot(a_ref[...].astype(_MXU), w_ref[...].astype(_MXU), 1, 0) + res_ref[...]
        r = lax.rsqrt(jnp.mean(xv * xv, axis=-1, keepdims=True) + _EPS)
        xh = xv * r
        gv = g_ref[...]
        diff = xh * gv - t_ref[...]

        @pl.when(pl.program_id(0) == 0)
        def _():
            loss_ref[...] = jnp.zeros_like(loss_ref)
            dg_ref[...] = jnp.zeros_like(dg_ref)

        part = jnp.sum(jnp.sum(diff * diff, axis=1, keepdims=True), axis=0, keepdims=True) * (0.5 / D)
        loss_ref[...] += jnp.broadcast_to(part, loss_ref.shape)
        dy = diff * (1.0 / D)
        dg_ref[...] += jnp.sum(dy * xh, axis=0, keepdims=True)
        dxn = dy * gv
        dx_ref[...] = r * (dxn - xh * jnp.mean(dxn * xh, axis=-1, keepdims=True))

    row = pl.BlockSpec((tr, D), lambda i: (i, 0))
    vec = pl.BlockSpec((1, D), lambda i: (0, 0))
    return pl.pallas_call(
        body, name=name,
        out_shape=(jax.ShapeDtypeStruct((8, _LANES), _F32), jax.ShapeDtypeStruct((N, D), _F32), jax.ShapeDtypeStruct((1, D), _F32)),
        grid=(N // tr,),
        in_specs=[pl.BlockSpec((tr, K), lambda i: (i, 0)), pl.BlockSpec((None, K, D), lambda i: (0, 0, 0)), row, vec, row],
        out_specs=(pl.BlockSpec((8, _LANES), lambda i: (0, 0)), row, vec),
        compiler_params=_params("arbitrary"),
    )(act, w_down, res, g.reshape(1, D), tgt)


def _mm_gu(h, w_gu, l, *, name, bias=None, glu=False, keep=None, rms_fwd=None, tm=512, tn=1408, xchg=None):
    keep = _MXU if keep is None else keep
    N, K = h.shape
    H = w_gu.shape[-1] // 2
    tm, tn = _pick(N, tm), _pick(H, tn)
    nj = H // tn
    has_bias, has_norm = bias is not None, rms_fwd is not None

    def body(*refs):
        h_ref, wp_ref, wq_ref = refs[:3]
        pair_ref, act_ref = refs[3 + 2 * has_bias + has_norm:][:2]
        if has_norm:
            hv = _rms_rows(h_ref[...], refs[3 + 2 * has_bias][...])
            refs[-1][...] = hv
        else:
            hv = h_ref[...].astype(_MXU)
        p = _dot(hv, wp_ref[...].astype(_MXU), 1, 0)
        q = _dot(hv, wq_ref[...].astype(_MXU), 1, 0)
        if has_bias:
            p = p + refs[3][...]
            q = q + refs[4][...]
        pair_ref[0] = p.astype(pair_ref.dtype)
        pair_ref[1] = q.astype(pair_ref.dtype)
        act_ref[...] = (p * _sigmoid(q) if glu else p * _sigmoid(p) * q).astype(act_ref.dtype)

    in_specs = [pl.BlockSpec((tm, K), lambda i, j: (i, 0)), pl.BlockSpec((None, K, tn), lambda i, j: (l, 0, j)),
                pl.BlockSpec((None, K, tn), lambda i, j: (l, 0, j + nj))]
    args = [h, w_gu, w_gu]
    if has_bias:
        b2 = bias.reshape(1, 2 * H).astype(_F32)
        in_specs += [pl.BlockSpec((1, tn), lambda i, j: (0, j)), pl.BlockSpec((1, tn), lambda i, j: (0, j + nj))]
        args += [b2, b2]
    out_shape = [jax.ShapeDtypeStruct((2, N, H), keep), jax.ShapeDtypeStruct((N, H), keep)]
    out_specs = [pl.BlockSpec((2, tm, tn), lambda i, j: (0, i, j)), pl.BlockSpec((tm, tn), lambda i, j: (i, j))]
    if has_norm:
        in_specs.append(pl.BlockSpec((1, K), lambda i, j: (0, 0)))
        args.append(rms_fwd.reshape(1, K))
        out_shape.append(jax.ShapeDtypeStruct((N, K), _MXU))
        out_specs.append(pl.BlockSpec((tm, K), lambda i, j: (i, 0)))
    outs, got = _fused_call(
        body, name=name, out_shape=out_shape, grid=(N // tm, nj), in_specs=in_specs, out_specs=out_specs,
        scratch_shapes=[], sem=("arbitrary", "arbitrary") if has_norm else ("parallel", "parallel"), args=args, xchg=xchg)
    return (*outs, got)


def _mm_dgu(dx, w_down, l, gu, *, name, tm=512, tn=1408):
    N, K = dx.shape
    H = w_down.shape[-2]
    tm, tn = _pick(N, tm), _pick(H, tn)

    def body(dx_ref, w_ref, gu_ref, o_ref):
        d = _dot(dx_ref[...].astype(_MXU), w_ref[...].astype(_MXU), 1, 1)
        g, u = gu_ref[0].astype(_F32), gu_ref[1].astype(_F32)
        sg = _sigmoid(g)
        o_ref[0] = (d * u * (sg * (1.0 + g * (1.0 - sg)))).astype(o_ref.dtype)
        o_ref[1] = (d * (g * sg)).astype(o_ref.dtype)

    return pl.pallas_call(
        body, name=name, out_shape=jax.ShapeDtypeStruct((2, N, H), _MXU), grid=(N // tm, H // tn),
        in_specs=[pl.BlockSpec((tm, K), lambda i, j: (i, 0)), pl.BlockSpec((None, tn, K), lambda i, j: (l, j, 0)),
                  pl.BlockSpec((2, tm, tn), lambda i, j: (0, i, j))],
        out_specs=pl.BlockSpec((2, tm, tn), lambda i, j: (0, i, j)), compiler_params=_params("parallel", "parallel"),
    )(dx, w_down, gu)


def _gmlp_mix(vb, w, trans):
    tr, W = vb.shape
    lane = lax.broadcasted_iota(jnp.int32, (_CHUNK, _LANES), 1)
    rows = []
    for c in range(tr // _CHUNK):
        tiles = []
        for j in range(W // _LANES):
            t = vb[c * _CHUNK:(c + 1) * _CHUNK, j * _LANES:(j + 1) * _LANES]
            ma = _dot(w[2 * j], t, 0 if trans else 1, 0)
            mb = _dot(w[2 * j + 1], t, 0 if trans else 1, 0)
            tiles.append(jnp.where(lane < _GRP, ma, mb))
        rows.append(jnp.concatenate(tiles, axis=1))
    return jnp.concatenate(rows, axis=0)


def _tril_w(w_ref):
    r = lax.broadcasted_iota(jnp.int32, (_CHUNK, _CHUNK), 0)
    c = lax.broadcasted_iota(jnp.int32, (_CHUNK, _CHUNK), 1)
    return jnp.where((r >= c)[None], w_ref[...], 0.0).astype(_MXU)


def _layernorm_stats(v):
    mu = jnp.mean(v, axis=-1, keepdims=True)
    xc = v - mu
    rstd = lax.rsqrt(jnp.mean(xc * xc, axis=-1, keepdims=True) + _EPS)
    return xc * rstd, rstd


def _gmlp_fwd(proj, ln_g, ln_b, w_s, bias_full, *, name):
    N = proj.shape[0]
    W = ln_g.shape[-1]
    G = w_s.shape[0]
    tr = _rows(N, 512)
    ub, vb_ = 3, 4

    def body(u_ref, v_ref, g_ref, b_ref, w_ref, bias_ref, o_ref):
        u = _gelu(u_ref[...])
        xh, _ = _layernorm_stats(_gelu(v_ref[...]))
        vgn = xh * g_ref[...] + b_ref[...]
        mixed = _gmlp_mix(vgn.astype(_MXU), _tril_w(w_ref), False)
        bias = jnp.concatenate([bias_ref[...]] * (tr // _CHUNK), axis=0)
        o_ref[...] = (u * (mixed + bias)).astype(o_ref.dtype)

    vec = pl.BlockSpec((1, W), lambda i: (0, 0))
    return pl.pallas_call(
        body, name=name, out_shape=jax.ShapeDtypeStruct((N, W), _MXU), grid=(N // tr,),
        in_specs=[pl.BlockSpec((tr, W), lambda i: (i, ub)), pl.BlockSpec((tr, W), lambda i: (i, vb_)), vec, vec,
                  pl.BlockSpec((G, _CHUNK, _CHUNK), lambda i: (0, 0, 0)), pl.BlockSpec((_CHUNK, W), lambda i: (0, 0))],
        out_specs=pl.BlockSpec((tr, W), lambda i: (i, 0)), compiler_params=_params("parallel"),
    )(proj, proj, ln_g.reshape(1, W), ln_b.reshape(1, W), w_s, bias_full)


def _gmlp_bwd(proj, da_src, da_blk, ln_g, ln_b, w_s, bias_full, *, name):
    N = proj.shape[0]
    W = ln_g.shape[-1]
    G = w_s.shape[0]
    tr = _rows(N, 512)
    nch = tr // _CHUNK

    def body(u_ref, v_ref, da_ref, g_ref, b_ref, w_ref, bias_ref, dz_ref, dg_ref, db_ref, dw_ref, dbias_ref):
        @pl.when(pl.program_id(0) == 0)
        def _():
            dg_ref[...] = jnp.zeros_like(dg_ref)
            db_ref[...] = jnp.zeros_like(db_ref)
            dw_ref[...] = jnp.zeros_like(dw_ref)
            dbias_ref[...] = jnp.zeros_like(dbias_ref)

        u_pre, v_pre = u_ref[...], v_ref[...]
        ug = _gelu(u_pre)
        xh, rstd = _layernorm_stats(_gelu(v_pre))
        lg = g_ref[...]
        vgn = xh * lg + b_ref[...]
        vb = vgn.astype(_MXU)
        wt = _tril_w(w_ref)
        mixed = _gmlp_mix(vb, wt, False)
        bias = jnp.concatenate([bias_ref[...]] * nch, axis=0)
        da = da_ref[...].astype(_F32)
        du = da * (mixed + bias)
        dm = da * ug
        dmb = dm.astype(_MXU)
        lane = lax.broadcasted_iota(jnp.int32, (_CHUNK, _LANES), 1)
        r = lax.broadcasted_iota(jnp.int32, (_CHUNK, _CHUNK), 0)
        c = lax.broadcasted_iota(jnp.int32, (_CHUNK, _CHUNK), 1)
        tril = r >= c
        dmsum = dm[0:_CHUNK]
        for ch in range(1, nch):
            dmsum = dmsum + dm[ch * _CHUNK:(ch + 1) * _CHUNK]
        dbias = jnp.zeros((_CHUNK, _LANES), _F32)
        for j in range(W // _LANES):
            tile = dmsum[:, j * _LANES:(j + 1) * _LANES]
            sa = jnp.sum(jnp.where(lane < _GRP, tile, 0.0), axis=1, keepdims=True)
            sb = jnp.sum(jnp.where(lane >= _GRP, tile, 0.0), axis=1, keepdims=True)
            dbias = dbias + jnp.where(lane == 2 * j, sa, 0.0) + jnp.where(lane == 2 * j + 1, sb, 0.0)
            acc_a = jnp.zeros((_CHUNK, _CHUNK), _F32)
            acc_b = jnp.zeros((_CHUNK, _CHUNK), _F32)
            for ch in range(nch):
                dt = dmb[ch * _CHUNK:(ch + 1) * _CHUNK, j * _LANES:(j + 1) * _LANES]
                vt = vb[ch * _CHUNK:(ch + 1) * _CHUNK, j * _LANES:(j + 1) * _LANES]
                acc_a = acc_a + _dot(jnp.where(lane < _GRP, dt, jnp.zeros_like(dt)), vt, 1, 1)
                acc_b = acc_b + _dot(jnp.where(lane >= _GRP, dt, jnp.zeros_like(dt)), vt, 1, 1)
            dw_ref[2 * j] += jnp.where(tril, acc_a, 0.0)
            dw_ref[2 * j + 1] += jnp.where(tril, acc_b, 0.0)
        dbias_ref[...] += dbias
        dvgn = _gmlp_mix(dmb, wt, True)
        dg_ref[...] += jnp.sum(dvgn * xh, axis=0, keepdims=True)
        db_ref[...] += jnp.sum(dvgn, axis=0, keepdims=True)
        dxh = dvgn * lg
        dvg = rstd * (dxh - jnp.mean(dxh, axis=-1, keepdims=True) - xh * jnp.mean(dxh * xh, axis=-1, keepdims=True))
        dz_ref[:, :W] = (du * _gelu_grad(u_pre)).astype(dz_ref.dtype)
        dz_ref[:, W:] = (dvg * _gelu_grad(v_pre)).astype(dz_ref.dtype)

    vec = pl.BlockSpec((1, W), lambda i: (0, 0))
    wspec = pl.BlockSpec((G, _CHUNK, _CHUNK), lambda i: (0, 0, 0))
    return pl.pallas_call(
        body, name=name,
        out_shape=(jax.ShapeDtypeStruct((N, 2 * W), _MXU), jax.ShapeDtypeStruct((1, W), _F32), jax.ShapeDtypeStruct((1, W), _F32),
                   jax.ShapeDtypeStruct((G, _CHUNK, _CHUNK), _F32), jax.ShapeDtypeStruct((_CHUNK, _LANES), _F32)),
        grid=(N // tr,),
        in_specs=[pl.BlockSpec((tr, W), lambda i: (i, 3)), pl.BlockSpec((tr, W), lambda i: (i, 4)),
                  pl.BlockSpec((tr, W), lambda i: (i, da_blk)), vec, vec, wspec, pl.BlockSpec((_CHUNK, W), lambda i: (0, 0))],
        out_specs=(pl.BlockSpec((tr, 2 * W), lambda i: (i, 0)), vec, vec, wspec, pl.BlockSpec((_CHUNK, _LANES), lambda i: (0, 0))),
        compiler_params=_params("arbitrary"),
    )(proj, proj, da_src, ln_g.reshape(1, W), ln_b.reshape(1, W), w_s, bias_full)


def _lane_cumsum(v):
    T = v.shape[1]
    lane = lax.broadcasted_iota(jnp.int32, (8, _LANES), 1)
    carry = jnp.zeros((8, 1), _F32)
    out = []
    for ch in range(T // _LANES):
        blk = v[:, ch * _LANES:(ch + 1) * _LANES]
        sh = 1
        while sh < _LANES:
            blk = blk + jnp.where(lane >= sh, pltpu.roll(blk, sh, 1), 0.0)
            sh *= 2
        blk = blk + carry
        carry = blk[:, _LANES - 1:_LANES]
        out.append(blk)
    return jnp.concatenate(out, axis=1), carry


def _log_sigmoid(x):
    return jnp.minimum(x, 0.0) - jnp.log(1.0 + jnp.exp(-jnp.abs(x)))


def _fox_cum(proj3, f_blk, f_bias, *, name):
    B, T, _ = proj3.shape
    H = f_bias.shape[-1]
    assert H == 8

    def body(f_ref, b_ref, o_ref):
        x = f_ref[0].T[0:8, :] + b_ref[...]
        cum, _ = _lane_cumsum(_log_sigmoid(x))
        o_ref[0] = cum

    return pl.pallas_call(
        body, name=name, out_shape=jax.ShapeDtypeStruct((B, 8, T), _F32), grid=(B,),
        in_specs=[pl.BlockSpec((1, T, _LANES), lambda b: (b, 0, f_blk)), pl.BlockSpec((8, 1), lambda b: (0, 0))],
        out_specs=pl.BlockSpec((1, 8, T), lambda b: (b, 0, 0)), compiler_params=_params("parallel"),
    )(proj3, f_bias.reshape(8, 1))


def _fox_cum_bwd(proj3, f_blk, f_bias, dcum, *, name):
    B, T, _ = proj3.shape

    def body(f_ref, b_ref, dc_ref, df_ref, dbias_ref):
        @pl.when(pl.program_id(0) == 0)
        def _():
            dbias_ref[...] = jnp.zeros_like(dbias_ref)

        x = f_ref[0].T[0:8, :] + b_ref[...]
        dc = dc_ref[0]
        incl, total = _lane_cumsum(dc)
        dlf = total - incl + dc
        df = dlf * _sigmoid(-x)
        full = jnp.concatenate([df, jnp.zeros((_LANES - 8, T), _F32)], axis=0).T
        dbias_ref[...] += jnp.sum(full, axis=0, keepdims=True)
        df_ref[0] = full

    return pl.pallas_call(
        body, name=name,
        out_shape=(jax.ShapeDtypeStruct((B, T, _LANES), _F32), jax.ShapeDtypeStruct((1, _LANES), _F32)), grid=(B,),
        in_specs=[pl.BlockSpec((1, T, _LANES), lambda b: (b, 0, f_blk)), pl.BlockSpec((8, 1), lambda b: (0, 0)),
                  pl.BlockSpec((1, 8, T), lambda b: (b, 0, 0))],
        out_specs=(pl.BlockSpec((1, T, _LANES), lambda b: (b, 0, 0)), pl.BlockSpec((1, _LANES), lambda b: (0, 0))),
        compiler_params=_params("arbitrary"),
    )(proj3, f_bias.reshape(8, 1), dcum)


def _cum_row(cum_ref, h, start, size):
    blk = cum_ref[0, :, pl.ds(start, size)]
    sub = lax.broadcasted_iota(jnp.int32, (blk.shape[0], 1), 0)
    return jnp.sum(jnp.where(sub == h, blk, 0.0), axis=0, keepdims=True)


def _causal(tq, q0, k0):
    r = lax.broadcasted_iota(jnp.int32, (tq, tq), 0)
    c = lax.broadcasted_iota(jnp.int32, (tq, tq), 1)
    return (r + q0) >= (c + k0)


def _fused_call(body, *, name, out_shape, grid, in_specs, out_specs, scratch_shapes, sem, args, xchg):
    out_shape, in_specs, out_specs, scratch_shapes = list(out_shape), list(in_specs), list(out_specs), list(scratch_shapes)
    if xchg is None:
        res = pl.pallas_call(body, name=name, out_shape=out_shape, grid=grid, in_specs=in_specs, out_specs=out_specs,
                             scratch_shapes=scratch_shapes, compiler_params=_params(*sem))(*args)
        return list(res), []
    n_in, n_out, n_scr = len(in_specs), len(out_specs), len(scratch_shapes)

    def fused(*refs):
        ins, refs = refs[:n_in], refs[n_in:]
        xs, refs = refs[:xchg.n_src], refs[xchg.n_src:]
        outs, refs = refs[:n_out], refs[n_out:]
        xd, refs = refs[:xchg.n_dst], refs[xchg.n_dst:]
        scr, sems = refs[:n_scr], refs[n_scr:]
        first = last = None
        for d, g in enumerate(grid):
            i = pl.program_id(d)
            first = (i == 0) if first is None else first & (i == 0)
            last = (i == g - 1) if last is None else last & (i == g - 1)

        @pl.when(first)
        def _():
            xchg.start(xs, xd, sems)

        body(*ins, *outs, *scr)

        @pl.when(last)
        def _():
            xchg.finish(xs, xd, sems)

    res = pl.pallas_call(
        fused, name=name, out_shape=out_shape + xchg.out_shapes, grid=grid, in_specs=in_specs + xchg.in_specs,
        out_specs=out_specs + xchg.out_specs, scratch_shapes=scratch_shapes + xchg.scratch,
        compiler_params=_params(*["arbitrary"] * len(grid)),
    )(*args, *xchg.srcs)
    return list(res[:n_out]), list(res[n_out:])


def _fox_fwd(proj3, cum, *, name, xchg=None):
    B, T, _ = proj3.shape
    H = cum.shape[1]
    W = H * _FOX_HD
    npair = W // _LANES
    tq = _rows(T, _FOX_TQ)
    nq = T // tq

    def body(q_ref, k_ref, v_ref, cum_ref, o_ref, lse_ref):
        p = pl.program_id(1)
        i = pl.program_id(2)
        q0 = pl.multiple_of(i * tq, tq)
        lane = lax.broadcasted_iota(jnp.int32, (1, _LANES), 1)
        q2 = q_ref[0] * _FOX_SCALE
        heads = []
        for hh in range(2):
            msk = (lane < _FOX_HD) if hh == 0 else (lane >= _FOX_HD)
            h = 2 * p + hh
            heads.append((msk, h, jnp.where(msk, q2, 0.0).astype(_MXU), _cum_row(cum_ref, h, q0, _LANES)[:, 0:1]))

        def step(jj, carry, masked):
            k0 = pl.multiple_of(jj * tq, tq)
            k2 = k_ref[0, pl.ds(k0, tq), :].astype(_MXU)
            v2 = v_ref[0, pl.ds(k0, tq), :]
            out = []
            for (msk, h, qm, c0), (m_prev, l_prev, acc) in zip(heads, carry):
                s = _dot(qm, k2, 1, 1) + (c0 - _cum_row(cum_ref, h, k0, tq))
                if masked:
                    s = jnp.where(_causal(tq, q0, k0), s, -jnp.inf)
                m_new = jnp.maximum(m_prev, jnp.max(s, axis=1, keepdims=True))
                alpha = jnp.exp(m_prev - m_new)
                e = jnp.exp(s - m_new)
                l_new = alpha * l_prev + jnp.sum(e, axis=1, keepdims=True)
                vm = jnp.where(msk, v2, 0.0).astype(_MXU)
                out.append((m_new, l_new, alpha * acc + _dot(e.astype(_MXU), vm, 1, 0)))
            return tuple(out)

        init = tuple((jnp.full((tq, 1), -jnp.inf, _F32), jnp.zeros((tq, 1), _F32), jnp.zeros((tq, _LANES), _F32)) for _ in heads)
        carry = step(i, lax.fori_loop(0, i, functools.partial(step, masked=False), init), True)
        o2 = jnp.zeros((tq, _LANES), _F32)
        for hh, (m, l, acc) in enumerate(carry):
            o2 = o2 + acc / l
            lse_ref[0, hh] = jnp.broadcast_to(m + jnp.log(l), (tq, _LANES))
        o_ref[0] = o2.astype(o_ref.dtype)

    (o, lse), got = _fused_call(
        body, name=name,
        out_shape=(jax.ShapeDtypeStruct((B, T, W), _MXU), jax.ShapeDtypeStruct((B, H, T, _LANES), _F32)),
        grid=(B, npair, nq),
        in_specs=[pl.BlockSpec((1, tq, _LANES), lambda b, p, i: (b, i, p)),
                  pl.BlockSpec((1, T, _LANES), lambda b, p, i: (b, 0, npair + p)),
                  pl.BlockSpec((1, T, _LANES), lambda b, p, i: (b, 0, 2 * npair + p)),
                  pl.BlockSpec((1, H, T), lambda b, p, i: (b, 0, 0))],
        out_specs=(pl.BlockSpec((1, tq, _LANES), lambda b, p, i: (b, i, p)),
                   pl.BlockSpec((1, 2, tq, _LANES), lambda b, p, i: (b, p, i, 0))),
        scratch_shapes=[], sem=("parallel", "parallel", "parallel"), args=(proj3, proj3, proj3, cum), xchg=xchg)
    return o, lse, got


def _fox_bwd(proj3, cum, do3, lse, *, name, xchg=None):
    B, T, _ = proj3.shape
    H = cum.shape[1]
    W = H * _FOX_HD
    npair = W // _LANES
    tq = _rows(T, _FOX_TQ)
    nq = T // tq

    def body(q_ref, k_ref, v_ref, cum_ref, do_ref, lse_ref, dq_ref, dk_ref, dv_ref, dc_ref, p_scr, dp_scr, dk_acc, dv_acc, dc_acc):
        p = pl.program_id(1)
        i = pl.program_id(2)
        q0 = pl.multiple_of(i * tq, tq)
        lane = lax.broadcasted_iota(jnp.int32, (1, _LANES), 1)

        @pl.when(i == 0)
        def _():
            dk_acc[...] = jnp.zeros_like(dk_acc)
            dv_acc[...] = jnp.zeros_like(dv_acc)
            dc_acc[...] = jnp.zeros_like(dc_acc)

        q2 = q_ref[0] * _FOX_SCALE
        do2 = do_ref[0].astype(_F32)
        heads = []
        for hh in range(2):
            msk = (lane < _FOX_HD) if hh == 0 else (lane >= _FOX_HD)
            h = 2 * p + hh
            heads.append((hh, msk, h, jnp.where(msk, q2, 0.0).astype(_MXU), jnp.where(msk, do2, 0.0).astype(_MXU),
                          _cum_row(cum_ref, h, q0, _LANES)[:, 0:1], lse_ref[0, hh][:, 0:1]))

        def first(jj, deltas, masked):
            k0 = pl.multiple_of(jj * tq, tq)
            kb = k_ref[0, pl.ds(k0, tq), :].astype(_MXU)
            vb = v_ref[0, pl.ds(k0, tq), :].astype(_MXU)
            out = []
            for (hh, _, h, qm, dom, c0, lse_h), delta in zip(heads, deltas):
                s = _dot(qm, kb, 1, 1) + (c0 - _cum_row(cum_ref, h, k0, tq))
                pr = jnp.exp(s - lse_h)
                if masked:
                    pr = jnp.where(_causal(tq, q0, k0), pr, 0.0)
                dp = _dot(dom, vb, 1, 1)
                p_scr[hh, jj] = pr
                dp_scr[hh, jj] = dp
                out.append(delta + jnp.sum(pr * dp, axis=1, keepdims=True))
            return tuple(out)

        zero = tuple(jnp.zeros((tq, 1), _F32) for _ in heads)
        deltas = first(i, lax.fori_loop(0, i, functools.partial(first, masked=False), zero), True)

        def second(jj, dq):
            k0 = pl.multiple_of(jj * tq, tq)
            k2 = k_ref[0, pl.ds(k0, tq), :]
            dk = jnp.zeros((tq, _LANES), _F32)
            dv = jnp.zeros((tq, _LANES), _F32)
            for (hh, msk, _, qm, dom, _, _), delta in zip(heads, deltas):
                pr = p_scr[hh, jj]
                ds = pr * (dp_scr[hh, jj] - delta)
                dsb = ds.astype(_MXU)
                dv = dv + _dot(pr.astype(_MXU), dom, 0, 0)
                dk = dk + _dot(dsb, qm, 0, 0)
                dc_acc[hh:hh + 1, pl.ds(k0, tq)] += jnp.sum(ds, axis=0, keepdims=True)
                dq = dq + _dot(dsb, jnp.where(msk, k2, 0.0).astype(_MXU), 1, 0)
            dv_acc[pl.ds(k0, tq), :] += dv
            dk_acc[pl.ds(k0, tq), :] += dk
            return dq

        dq2 = lax.fori_loop(0, i + 1, second, jnp.zeros((tq, _LANES), _F32))
        dq_ref[0] = (dq2 * _FOX_SCALE).astype(dq_ref.dtype)

        @pl.when(i == nq - 1)
        def _():
            dk_ref[0] = dk_acc[...].astype(dk_ref.dtype)
            dv_ref[0] = dv_acc[...].astype(dv_ref.dtype)
            dc_ref[0, 0] = -dc_acc[...]

    full = lambda blk: pl.BlockSpec((1, T, _LANES), lambda b, p, i, blk=blk: (b, 0, blk * npair + p))
    part = lambda blk: pl.BlockSpec((1, tq, _LANES), lambda b, p, i, blk=blk: (b, i, blk * npair + p))
    (dq, dk, dv, dcum), got = _fused_call(
        body, name=name,
        out_shape=(jax.ShapeDtypeStruct((B, T, W), _MXU), jax.ShapeDtypeStruct((B, T, W), _MXU),
                   jax.ShapeDtypeStruct((B, T, W), _MXU), jax.ShapeDtypeStruct((B, npair, 2, T), _F32)),
        grid=(B, npair, nq),
        in_specs=[part(0), full(1), full(2), pl.BlockSpec((1, H, T), lambda b, p, i: (b, 0, 0)), part(0),
                  pl.BlockSpec((1, 2, tq, _LANES), lambda b, p, i: (b, p, i, 0))],
        out_specs=(part(0), full(0), full(0), pl.BlockSpec((1, 1, 2, T), lambda b, p, i: (b, p, 0, 0))),
        scratch_shapes=[pltpu.VMEM((2, nq, tq, tq), _F32), pltpu.VMEM((2, nq, tq, tq), _F32), pltpu.VMEM((T, _LANES), _F32),
                        pltpu.VMEM((T, _LANES), _F32), pltpu.VMEM((2, T), _F32)],
        sem=("parallel", "parallel", "arbitrary"), args=(proj3, proj3, proj3, cum, do3, lse), xchg=xchg)
    return dq, dk, dv, dcum, got


def _xa_probs(qh, kh, scale):
    s = _dot(qh, kh, 1, 1) * scale
    e = jnp.exp(s - jnp.max(s, axis=1, keepdims=True))
    return e / jnp.sum(e, axis=1, keepdims=True)


def _xa_fwd(q3, kv3, *, name):
    B, T, D = q3.shape
    M = kv3.shape[1]
    hd = D // _XA_HEADS
    scale = hd ** -0.5
    tq = _rows(T, 512)

    def body(q_ref, kv_ref, o_ref):
        for h in range(_XA_HEADS):
            sl = slice(h * hd, (h + 1) * hd)
            p = _xa_probs(q_ref[0, :, sl], kv_ref[0, :, sl], scale)
            o_ref[0, :, sl] = _dot(p.astype(_MXU), kv_ref[0, :, D + h * hd:D + (h + 1) * hd], 1, 0).astype(o_ref.dtype)

    return pl.pallas_call(
        body, name=name, out_shape=jax.ShapeDtypeStruct((B, T, D), _MXU), grid=(B, T // tq),
        in_specs=[pl.BlockSpec((1, tq, D), lambda b, i: (b, i, 0)), pl.BlockSpec((1, M, 2 * D), lambda b, i: (b, 0, 0))],
        out_specs=pl.BlockSpec((1, tq, D), lambda b, i: (b, i, 0)), compiler_params=_params("parallel", "parallel"),
    )(q3, kv3)


def _xa_bwd(q3, kv3, do3, *, name):
    B, T, D = q3.shape
    M = kv3.shape[1]
    hd = D // _XA_HEADS
    scale = hd ** -0.5
    tq = _rows(T, 512)

    def body(q_ref, kv_ref, do_ref, dq_ref, dkv_ref):
        @pl.when(pl.program_id(1) == 0)
        def _():
            dkv_ref[...] = jnp.zeros_like(dkv_ref)

        for h in range(_XA_HEADS):
            sl = slice(h * hd, (h + 1) * hd)
            slv = slice(D + h * hd, D + (h + 1) * hd)
            qh, kh, vh, doh = q_ref[0, :, sl], kv_ref[0, :, sl], kv_ref[0, :, slv], do_ref[0, :, sl]
            p = _xa_probs(qh, kh, scale)
            dkv_ref[0, :, slv] += _dot(p.astype(_MXU), doh, 0, 0)
            dp = _dot(doh, vh, 1, 1)
            ds = (p * (dp - jnp.sum(p * dp, axis=1, keepdims=True))).astype(_MXU)
            dq_ref[0, :, sl] = (_dot(ds, kh, 1, 0) * scale).astype(dq_ref.dtype)
            dkv_ref[0, :, sl] += _dot(ds, qh, 0, 0) * scale

    blk = pl.BlockSpec((1, tq, D), lambda b, i: (b, i, 0))
    kvs = pl.BlockSpec((1, M, 2 * D), lambda b, i: (b, 0, 0))
    return pl.pallas_call(
        body, name=name,
        out_shape=(jax.ShapeDtypeStruct((B, T, D), _MXU), jax.ShapeDtypeStruct((B, M, 2 * D), _F32)),
        grid=(B, T // tq), in_specs=[blk, kvs, blk], out_specs=(blk, kvs),
        compiler_params=_params("parallel", "arbitrary"),
    )(q3, kv3, do3)


def _rotated_copies(ext, rot, tt):
    rot[0] = ext[...]
    for b in range(1, 8):
        rot[b, 0:tt + _HALO - 8, :] = ext[b:b + tt + _HALO - 8, :]


def _shifted(rot, off, r0, rows, c0):
    a, b = divmod(off, 8)
    return rot[b, 8 * a + r0:8 * a + r0 + rows, c0:c0 + _LANES]


def _conv_fwd(y3, dw_w, dw_b, ln_g, ln_b, *, name, xchg=None):
    B, T, C = y3.shape
    tt = _rows(T, 256)
    nt = T // tt

    def body(prev_ref, cur_ref, w_ref, b_ref, g_ref, lb_ref, y2_ref, y4_ref, ext, rot):
        i = pl.program_id(1)
        ext[0:_HALO, :] = jnp.where(i > 0, prev_ref[0, tt - _HALO:tt, :], 0.0)
        ext[_HALO:_HALO + tt, :] = cur_ref[0]
        _rotated_copies(ext, rot, tt)
        for c0 in range(0, C, _LANES):
            acc = jnp.broadcast_to(b_ref[:, c0:c0 + _LANES], (tt, _LANES))
            for j in range(_CONV_K):
                acc = acc + w_ref[j:j + 1, c0:c0 + _LANES] * _shifted(rot, _HALO - (_CONV_K - 1) + j, 0, tt, c0)
            y2_ref[0, :, c0:c0 + _LANES] = acc
        xh, _ = _layernorm_stats(y2_ref[0])
        z = xh * g_ref[...] + lb_ref[...]
        y4_ref[0] = (z * _sigmoid(z)).astype(y4_ref.dtype)

    vec = pl.BlockSpec((1, C), lambda b, i: (0, 0))
    blk = pl.BlockSpec((1, tt, C), lambda b, i: (b, i, 0))
    (y2, y4), got = _fused_call(
        body, name=name,
        out_shape=(jax.ShapeDtypeStruct((B, T, C), _F32), jax.ShapeDtypeStruct((B, T, C), _MXU)),
        grid=(B, nt),
        in_specs=[pl.BlockSpec((1, tt, C), lambda b, i: (b, jnp.maximum(i - 1, 0), 0)), blk,
                  pl.BlockSpec((_HALO, C), lambda b, i: (0, 0)), vec, vec, vec],
        out_specs=(blk, blk),
        scratch_shapes=[pltpu.VMEM((tt + _HALO, C), _F32), pltpu.VMEM((8, tt + _HALO, C), _F32)],
        sem=("parallel", "parallel"), args=(y3, y3, dw_w, dw_b.reshape(1, C), ln_g.reshape(1, C), ln_b.reshape(1, C)), xchg=xchg)
    return y2, y4, got


def _conv_ln_bwd(y2, dx, w_out, ln_g, ln_b, *, name):
    N, C = y2.shape
    D = dx.shape[1]
    tr = _rows(N, 512)

    def body(y_ref, dx_ref, w_ref, g_ref, b_ref, dy_ref, dg_ref, db_ref, dwb_ref):
        @pl.when(pl.program_id(0) == 0)
        def _():
            dg_ref[...] = jnp.zeros_like(dg_ref)
            db_ref[...] = jnp.zeros_like(db_ref)
            dwb_ref[...] = jnp.zeros_like(dwb_ref)

        dy4 = _dot(dx_ref[...].astype(_MXU), w_ref[...].astype(_MXU), 1, 1)
        xh, rstd = _layernorm_stats(y_ref[...])
        gv = g_ref[...]
        z = xh * gv + b_ref[...]
        sg = _sigmoid(z)
        dz = dy4 * (sg * (1.0 + z * (1.0 - sg)))
        dg_ref[...] += jnp.sum(dz * xh, axis=0, keepdims=True)
        db_ref[...] += jnp.sum(dz, axis=0, keepdims=True)
        dxh = dz * gv
        dy = rstd * (dxh - jnp.mean(dxh, axis=-1, keepdims=True) - xh * jnp.mean(dxh * xh, axis=-1, keepdims=True))
        dwb_ref[...] += jnp.sum(dy, axis=0, keepdims=True)
        dy_ref[...] = dy

    row = pl.BlockSpec((tr, C), lambda i: (i, 0))
    vec = pl.BlockSpec((1, C), lambda i: (0, 0))
    v = jax.ShapeDtypeStruct((1, C), _F32)
    return pl.pallas_call(
        body, name=name, out_shape=(jax.ShapeDtypeStruct((N, C), _F32), v, v, v), grid=(N // tr,),
        in_specs=[row, pl.BlockSpec((tr, D), lambda i: (i, 0)), pl.BlockSpec((None, C, D), lambda i: (0, 0, 0)), vec, vec],
        out_specs=(row, vec, vec, vec), compiler_params=_params("arbitrary"),
    )(y2, dx, w_out, ln_g.reshape(1, C), ln_b.reshape(1, C))


def _conv_bwd(y3, dy23, ag3, dw_w, *, name, xchg=None):
    B, T, C = y3.shape
    tt = _rows(T, 256)
    nt = T // tt

    rs = _rows(tt, _CONV_ROWS)

    def groups(v):
        return jnp.sum(v.reshape(rs // 8, 8, _LANES), axis=0)

    def body(yp_ref, yc_ref, dc_ref, dn_ref, a_ref, g_ref, w_ref, dag_ref, dw_ref, dbin_ref, yext, dext, yrot, drot, dw_acc, db_acc):
        b = pl.program_id(0)
        i = pl.program_id(1)

        @pl.when((b == 0) & (i == 0))
        def _():
            dw_acc[...] = jnp.zeros_like(dw_acc)
            db_acc[...] = jnp.zeros_like(db_acc)

        yext[0:_HALO, :] = jnp.where(i > 0, yp_ref[0, tt - _HALO:tt, :], 0.0)
        yext[_HALO:_HALO + tt, :] = yc_ref[0]
        dext[0:tt, :] = dc_ref[0]
        dext[tt:tt + _HALO, :] = jnp.where(i < nt - 1, dn_ref[0, 0:_HALO, :], 0.0)
        _rotated_copies(yext, yrot, tt)
        _rotated_copies(dext, drot, tt)
        for c0 in range(0, C, _LANES):
            for r0 in range(0, tt, rs):
                d_cur = dext[r0:r0 + rs, c0:c0 + _LANES]
                dy = jnp.zeros((rs, _LANES), _F32)
                for j in range(_CONV_K):
                    sh = _CONV_K - 1 - j
                    dy = dy + w_ref[j:j + 1, c0:c0 + _LANES] * _shifted(drot, sh, r0, rs, c0)
                    dw_acc[j, :, c0:c0 + _LANES] += groups(d_cur * _shifted(yrot, _HALO - sh, r0, rs, c0))
                a, g = a_ref[0, r0:r0 + rs, c0:c0 + _LANES], g_ref[0, r0:r0 + rs, c0:c0 + _LANES]
                sg = _sigmoid(g)
                da = dy * sg
                dg = dy * a * (sg * (1.0 - sg))
                dag_ref[0, r0:r0 + rs, c0:c0 + _LANES] = da.astype(dag_ref.dtype)
                dag_ref[0, r0:r0 + rs, C + c0:C + c0 + _LANES] = dg.astype(dag_ref.dtype)
                db_acc[:, c0:c0 + _LANES] += groups(da)
                db_acc[:, C + c0:C + c0 + _LANES] += groups(dg)

        @pl.when((b == B - 1) & (i == nt - 1))
        def _():
            dw_ref[...] = jnp.sum(dw_acc[...], axis=1)
            dbin_ref[...] = jnp.sum(db_acc[...], axis=0, keepdims=True)

    blk = pl.BlockSpec((1, tt, C), lambda b, i: (b, i, 0))
    (dag, ddw, dbin), got = _fused_call(
        body, name=name,
        out_shape=(jax.ShapeDtypeStruct((B, T, 2 * C), _MXU), jax.ShapeDtypeStruct((_HALO, C), _F32),
                   jax.ShapeDtypeStruct((1, 2 * C), _F32)),
        grid=(B, nt),
        in_specs=[pl.BlockSpec((1, tt, C), lambda b, i: (b, jnp.maximum(i - 1, 0), 0)), blk, blk,
                  pl.BlockSpec((1, tt, C), lambda b, i: (b, jnp.minimum(i + 1, nt - 1), 0)),
                  pl.BlockSpec((None, 1, tt, C), lambda b, i: (0, b, i, 0)), pl.BlockSpec((None, 1, tt, C), lambda b, i: (1, b, i, 0)),
                  pl.BlockSpec((_HALO, C), lambda b, i: (0, 0))],
        out_specs=(pl.BlockSpec((1, tt, 2 * C), lambda b, i: (b, i, 0)), pl.BlockSpec((_HALO, C), lambda b, i: (0, 0)),
                   pl.BlockSpec((1, 2 * C), lambda b, i: (0, 0))),
        scratch_shapes=[pltpu.VMEM((tt + _HALO, C), _F32), pltpu.VMEM((tt + _HALO, C), _F32),
                        pltpu.VMEM((8, tt + _HALO, C), _F32), pltpu.VMEM((8, tt + _HALO, C), _F32),
                        pltpu.VMEM((_HALO, 8, C), _F32), pltpu.VMEM((8, 2 * C), _F32)],
        sem=("arbitrary", "arbitrary"), args=(y3, y3, dy23, dy23, ag3, ag3, dw_w), xchg=xchg)
    return dag, ddw, dbin, got


class _Exchange:
    def __init__(self, items):
        self.per_peer = [pp for _, pp in items]
        self.srcs, self.out_shapes, self.pieces = [], [], []
        for t, (srcs, per_peer) in enumerate(items):
            blk = srcs[0].shape[1:] if per_peer else srcs[0].shape
            self.out_shapes.append(jax.ShapeDtypeStruct((len(srcs), _N_DEV) + tuple(blk), srcs[0].dtype))
            for l, s in enumerate(srcs):
                self.pieces.append((t, l, len(self.srcs)))
                self.srcs.append(s)
        self.n_src, self.n_dst, n_pc = len(self.srcs), len(items), len(self.pieces)
        self.in_specs = [pl.BlockSpec(memory_space=pl.ANY)] * self.n_src
        self.out_specs = [pl.BlockSpec(memory_space=pl.ANY)] * self.n_dst
        self.scratch = [pltpu.SemaphoreType.DMA((n_pc, _N_DEV - 1)), pltpu.SemaphoreType.DMA((n_pc, _N_DEV - 1)),
                        pltpu.SemaphoreType.DMA((n_pc,))]

    def _copies(self, src_refs, dst_refs, sems, kind):
        send_sems, recv_sems, loc_sems = sems
        x, y, c = lax.axis_index("x"), lax.axis_index("y"), lax.axis_index("c")
        me = 4 * x + 2 * y + c
        out = []
        for i, (t, l, s) in enumerate(self.pieces):
            def src_for(p, s=s, t=t):
                return src_refs[s].at[p] if self.per_peer[t] else src_refs[s]

            if kind == "local":
                out.append(pltpu.make_async_copy(src_for(me), dst_refs[t].at[l, me], loc_sems.at[i]))
                continue
            for k in range(1, _N_DEV):
                px, py, pc = (1 - x if k & 4 else x), (1 - y if k & 2 else y), (1 - c if k & 1 else c)
                p = 4 * px + 2 * py + pc
                out.append(pltpu.make_async_remote_copy(
                    src_ref=src_for(p), dst_ref=dst_refs[t].at[l, p if kind == "recv" else me],
                    send_sem=send_sems.at[i, k - 1], recv_sem=recv_sems.at[i, k - 1],
                    device_id=(px, py, pc), device_id_type=pl.DeviceIdType.MESH))
        return out

    def start(self, src_refs, dst_refs, sems):
        for cp in self._copies(src_refs, dst_refs, sems, "local") + self._copies(src_refs, dst_refs, sems, "send"):
            cp.start()

    def finish(self, src_refs, dst_refs, sems):
        for cp in self._copies(src_refs, dst_refs, sems, "send"):
            cp.wait_send()
        for cp in self._copies(src_refs, dst_refs, sems, "recv"):
            cp.wait_recv()
        for cp in self._copies(src_refs, dst_refs, sems, "local"):
            cp.wait()


def _exchange(items, *, name):
    ex = _Exchange(items)

    def body(*refs):
        parts = refs[:ex.n_src], refs[ex.n_src:ex.n_src + ex.n_dst], refs[ex.n_src + ex.n_dst:]
        ex.start(*parts)
        ex.finish(*parts)

    return pl.pallas_call(
        body, name=name, out_shape=ex.out_shapes, in_specs=ex.in_specs, out_specs=ex.out_specs, scratch_shapes=ex.scratch,
        compiler_params=pltpu.CompilerParams(has_side_effects=True),
    )(*ex.srcs)


def _adam_update(g, w, m, v):
    c1 = 1.0 / (1.0 - _ADAM_B1 ** _ADAM_STEP)
    c2 = 1.0 / (1.0 - _ADAM_B2 ** _ADAM_STEP)
    m2 = _ADAM_B1 * m + (1.0 - _ADAM_B1) * g
    v2 = _ADAM_B2 * v + (1.0 - _ADAM_B2) * (g * g)
    return -_ADAM_LR * ((m2 * c1) / (jnp.sqrt(v2 * c2) + _ADAM_EPS) + _ADAM_WD * w), m2, v2


def _adamw_big(recvs, w, m, v, *, name):
    L, R, C = w.shape
    tr = _rows(R, 256)
    nb = R // tr

    def body(*refs):
        r_refs = refs[:L]
        w_ref, m_ref, v_ref, g_ref, d_ref, mo_ref, vo_ref = refs[L:]
        for l in range(L):
            @pl.when(pl.program_id(0) == l)
            def _(r_ref=r_refs[l]):
                g = r_ref[0, 0].astype(_F32)
                for k in range(1, _N_DEV):
                    g = g + r_ref[0, k].astype(_F32)
                g_ref[0] = g
                d_ref[0], mo_ref[0], vo_ref[0] = _adam_update(g, w_ref[0], m_ref[0], v_ref[0])

    def recv_spec(l):
        return pl.BlockSpec((1, _N_DEV, tr, C), lambda ll, i: (0, 0, jnp.where(ll == l, i, jnp.where(ll < l, 0, nb - 1)), 0))

    blk = pl.BlockSpec((1, tr, C), lambda l, i: (l, i, 0))
    o = jax.ShapeDtypeStruct((L, R, C), _F32)
    return pl.pallas_call(
        body, name=name, out_shape=(o, o, o, o), grid=(L, nb),
        in_specs=[recv_spec(l) for l in range(L)] + [blk, blk, blk], out_specs=(blk, blk, blk, blk),
        compiler_params=_params("arbitrary", "arbitrary"),
    )(*recvs, w, m, v)


def _adamw_small(tensors, *, name):
    n = len(tensors)
    lanes = [t[4] for t in tensors]
    layers = [len(t[0]) for t in tensors]

    def body(*refs):
        pos = 0
        ins = []
        for t in range(n):
            ins.append((refs[pos:pos + layers[t]], *refs[pos + layers[t]:pos + layers[t] + 3]))
            pos += layers[t] + 3
        outs = refs[pos:]
        for t in range(n):
            r_refs, w_ref, m_ref, v_ref = ins[t]
            g_ref, d_ref, mo_ref, vo_ref = outs[4 * t:4 * t + 4]
            for l in range(layers[t]):
                g = r_refs[l][0, 0]
                for k in range(1, _N_DEV):
                    g = g + r_refs[l][0, k]
                if lanes[t] is not None:
                    g = g[..., :lanes[t]]
                g_ref[l] = g
                d_ref[l], mo_ref[l], vo_ref[l] = _adam_update(g, w_ref[l], m_ref[l], v_ref[l])

    args, out_shape = [], []
    for recvs, w, m, v, _ in tensors:
        args += [*recvs, w, m, v]
        out_shape += [jax.ShapeDtypeStruct(w.shape, _F32)] * 4
    outs = pl.pallas_call(
        body, name=name, out_shape=out_shape,
        in_specs=[pl.BlockSpec(memory_space=pltpu.VMEM)] * len(args), out_specs=[pl.BlockSpec(memory_space=pltpu.VMEM)] * len(out_shape),
        compiler_params=_params(),
    )(*args)
    return [tuple(outs[4 * t:4 * t + 4]) for t in range(n)]


_BIG = (("w_in_e", 2), ("w_out_e", 1), ("conv_w_in", 2), ("conv_w_out", 1), ("xa_wq", 1), ("xa_wkv", 2), ("xa_wo", 1),
        ("ffn_w_gu", 2), ("ffn_w_down", 1))
_SMALL_SHARDED = (("mix_norm_o", 1), ("conv_b_in", 1), ("conv_dw_w", 2), ("conv_dw_b", 1), ("conv_ln_g", 1),
                  ("conv_ln_b", 1), ("conv_b_out", 1))
_REPLICATED = ("mix_norm_e", "fox_f_bias", "gmlp_ln_g", "gmlp_ln_b", "gmlp_w_s", "gmlp_b_s", "xa_norm", "mem_norm",
               "ffn_norm", "final_norm")
_WEIGHTS = ("mix_norm_e", "w_in_e", "fox_f_bias", "gmlp_ln_g", "gmlp_ln_b", "gmlp_w_s", "gmlp_b_s", "w_out_e", "mix_norm_o",
            "conv_w_in", "conv_b_in", "conv_dw_w", "conv_dw_b", "conv_ln_g", "conv_ln_b", "conv_w_out", "conv_b_out",
            "xa_norm", "mem_norm", "xa_wq", "xa_wkv", "xa_wo", "ffn_norm", "ffn_w_gu", "ffn_w_down", "final_norm")


def _cols_to_peers(g, n=_N_DEV):
    K, N = g.shape[-2:]
    return jnp.swapaxes(g.reshape(g.shape[:-1] + (n, N // n)), -3, -2)


def _weight_items(pieces, wsrc):
    return [([wsrc[n][l]] if n in dict(_BIG) else [wsrc[n]], False) for n, l in pieces]


def _place_weights(P, pieces, gathered):
    axis = dict(_BIG + _SMALL_SHARDED)
    for (n, l), g in zip(pieces, gathered):
        if n in dict(_BIG):
            P.setdefault(n, {})[l] = g.reshape(1, -1, g.shape[-1]) if axis[n] == 1 else _peers_to_cols(g)
        else:
            P[n] = _peers_to_cols(g[0, :, 0])[None] if axis[n] == 2 else g.reshape(1, -1)


def _grad_items(pieces, G):
    axis = dict(_BIG + _SMALL_SHARDED)
    items = []
    for n, l in pieces:
        g = G[n][l]
        if n in _REPLICATED:
            items.append(([g], False))
        elif n == "ffn_w_gu":
            half = _N_DEV // 2
            items.append(([jnp.concatenate([_cols_to_peers(g[0], half), _cols_to_peers(g[1], half)], axis=0)], True))
        elif n in dict(_BIG):
            items.append(([g.reshape(_N_DEV, -1, g.shape[-1]) if axis[n] == 1 else _cols_to_peers(g)], True))
        else:
            items.append(([_cols_to_peers(g) if axis[n] == 2 else g.reshape(_N_DEV, 1, -1)], True))
    return items


def _peers_to_cols(d):
    K, c = d.shape[-2:]
    return jnp.swapaxes(d, -3, -2).reshape(d.shape[:-3] + (K, _N_DEV * c))


def _local_step(x, mem, tgt, P, wsrc=None, fwd_hooks=None, bwd_hooks=None):
    fwd_hooks, bwd_hooks = fwd_hooks or {}, bwd_hooks or {}
    sent = {}

    def gather(kernel_name):
        return _Exchange(_weight_items(fwd_hooks[kernel_name], wsrc)) if kernel_name in fwd_hooks else None

    def placed(kernel_name, got):
        if kernel_name in fwd_hooks:
            _place_weights(P, fwd_hooks[kernel_name], got)

    def scatter(kernel_name):
        return _Exchange(_grad_items(bwd_hooks[kernel_name], G)) if kernel_name in bwd_hooks else None

    def received(kernel_name, got):
        if kernel_name in bwd_hooks:
            sent.update(zip(bwd_hooks[kernel_name], got))

    def mm(a, b, *, name, **kw):
        ex = gather(name) or scatter(name)
        out = _mm(a, b, name=name, xchg=ex, **kw)
        if ex is None:
            return out
        out, got = out
        placed(name, got)
        received(name, got)
        return out

    B, T, D = x.shape
    M = mem.shape[1]
    N = B * T
    W = D // 2
    H = W // _FOX_HD
    f_blk = 5 * W // _LANES
    G = {}
    x0 = x.reshape(N, D)
    memf = mem.reshape(B * M, D)

    h_e = _rms_fwd(x0, P["mix_norm_e"][0], name="rms_mix_e", xchg=gather("rms_mix_e"))
    if "rms_mix_e" in fwd_hooks:
        h_e, got = h_e
        placed("rms_mix_e", got)
    w_in_pad = _pad_w_in(P["w_in_e"][0][0], W, H)[None]
    proj = mm(h_e, w_in_pad, bl=0, name="mm_in_e", tn=896)
    proj3 = proj.reshape(B, T, -1)
    cum = _fox_cum(proj3, f_blk, P["fox_f_bias"][0], name="fox_cum")
    o_fox, lse, got = _fox_fwd(proj3, cum, name="fox_fwd", xchg=gather("fox_fwd"))
    placed("fox_fwd", got)
    bias_full = jnp.repeat(P["gmlp_b_s"][0].T, _GRP, axis=1)
    a_out = _gmlp_fwd(proj, P["gmlp_ln_g"][0], P["gmlp_ln_b"][0], P["gmlp_w_s"][0], bias_full, name="gmlp_fwd")
    mixcat = jnp.concatenate([o_fox.reshape(N, W), a_out], axis=1)
    x1 = mm(mixcat, P["w_out_e"][0], bl=0, res=x0, name="mm_out_e")

    def xa_ffn_fwd(xin, l, last=False):
        s = {}
        s["q"], s["h_xa"] = mm(xin, P["xa_wq"][l], bl=0, rms_fwd=P["xa_norm"][l], out_dtype=_MXU, name=f"mm_q{l}")
        s["mn"] = _rms_fwd(memf, P["mem_norm"][l], name=f"rms_mem{l}")
        s["kv"] = mm(s["mn"], P["xa_wkv"][l], bl=0, out_dtype=_MXU, name=f"mm_kv{l}")
        s["o"] = _xa_fwd(s["q"].reshape(B, T, D), s["kv"].reshape(B, M, 2 * D), name=f"xa_fwd{l}").reshape(N, D)
        s["x_mid"] = mm(s["o"], P["xa_wo"][l], bl=0, res=xin, name=f"mm_o{l}")
        s["gu"], s["act"], s["h_ffn"], got = _mm_gu(s["x_mid"], P["ffn_w_gu"][l], 0, rms_fwd=P["ffn_norm"][l], name=f"mm_gu{l}",
                                                    xchg=gather(f"mm_gu{l}"))
        placed(f"mm_gu{l}", got)
        s["x_in"] = xin
        if last:
            return None, s
        xout = mm(s["act"], P["ffn_w_down"][l], bl=0, res=s["x_mid"], name=f"mm_down{l}", tn=512)
        return xout, s

    x3, s0 = xa_ffn_fwd(x1, 0)
    ag, y, h_o, _ = _mm_gu(x3, P["conv_w_in"][0], 0, bias=P["conv_b_in"][0], glu=True, keep=_F32, rms_fwd=P["mix_norm_o"][0],
                           name="mm_conv_in")
    C = y.shape[1]
    dw_w = jnp.pad(P["conv_dw_w"][0], ((0, _HALO - _CONV_K), (0, 0)))
    y2, y4, got = _conv_fwd(y.reshape(B, T, C), dw_w, P["conv_dw_b"][0], P["conv_ln_g"][0], P["conv_ln_b"][0], name="conv_fwd",
                            xchg=gather("conv_fwd"))
    placed("conv_fwd", got)
    x4 = mm(y4.reshape(N, C), P["conv_w_out"][0], bl=0, bias=P["conv_b_out"][0], res=x3, name="mm_conv_out")
    _, s1 = xa_ffn_fwd(x4, 1, last=True)
    loss, dx, dg = _final_loss(s1["act"], P["ffn_w_down"][1], s1["x_mid"], P["final_norm"], tgt.reshape(N, D), name="final_loss")
    G["final_norm"] = [dg]

    def xa_ffn_bwd(dx, s, l):
        for k in ("ffn_w_down", "ffn_w_gu", "ffn_norm", "xa_wo", "xa_wq", "xa_norm", "xa_wkv", "mem_norm"):
            G.setdefault(k, {})
        dgu = _mm_dgu(dx, P["ffn_w_down"][l], 0, s["gu"], name=f"mm_dgu{l}")
        G["ffn_w_down"][l] = mm(s["act"], dx, ta=True, tk=_DW_ROWS, out_dtype=_MXU, name=f"mm_dwdown{l}", tm=1408)
        G["ffn_w_gu"][l] = (mm(s["h_ffn"], dgu, ta=True, tk=_DW_ROWS, bl=0, out_dtype=_MXU, name=f"mm_dwg{l}", tn=1408),
                            mm(s["h_ffn"], dgu, ta=True, tk=_DW_ROWS, bl=1, out_dtype=_MXU, name=f"mm_dwu{l}", tn=1408))
        dx, G["ffn_norm"][l] = mm(dgu, P["ffn_w_gu"][l], al="cat", bl=0, tb=True, rms_bwd=(s["x_mid"], P["ffn_norm"][l], dx),
                                  name=f"mm_dhffn{l}", tm=512, tn=D, tk=2816)
        do = mm(dx, P["xa_wo"][l], bl=0, tb=True, out_dtype=_MXU, name=f"mm_do{l}")
        G["xa_wo"][l] = mm(s["o"], dx, ta=True, tk=2 * _DW_ROWS, out_dtype=_MXU, name=f"mm_dwo{l}")
        dq, dkv = _xa_bwd(s["q"].reshape(B, T, D), s["kv"].reshape(B, M, 2 * D), do.reshape(B, T, D), name=f"xa_bwd{l}")
        dq, dkv = dq.reshape(N, D), dkv.reshape(B * M, 2 * D)
        G["xa_wq"][l] = mm(s["h_xa"], dq, ta=True, tk=2 * _DW_ROWS, out_dtype=_MXU, name=f"mm_dwq{l}")
        dx, G["xa_norm"][l] = mm(dq, P["xa_wq"][l], bl=0, tb=True, rms_bwd=(s["x_in"], P["xa_norm"][l], dx), name=f"mm_dhxa{l}",
                                 tm=512, tn=D)
        G["xa_wkv"][l] = mm(s["mn"], dkv, ta=True, tk=2 * _DW_ROWS, out_dtype=_MXU, name=f"mm_dwkv{l}")
        dmn = mm(dkv, P["xa_wkv"][l], bl=0, tb=True, name=f"mm_dmn{l}")
        G["mem_norm"][l] = _rms_gain_bwd(memf, dmn, name=f"rms_mem_bwd{l}")
        return dx

    dx = xa_ffn_bwd(dx, s1, 1)
    G["conv_b_out"] = [_colsum(dx, name="colsum_b_out")]
    G["conv_w_out"] = [mm(y4.reshape(N, C), dx, ta=True, tk=2 * _DW_ROWS, out_dtype=_MXU, name="mm_dwconv_out")]
    dy2, dlg, dlb, ddb = _conv_ln_bwd(y2.reshape(N, C), dx, P["conv_w_out"][0], P["conv_ln_g"][0], P["conv_ln_b"][0],
                                      name="conv_ln_bwd")
    G["conv_ln_g"], G["conv_ln_b"], G["conv_dw_b"] = [dlg], [dlb], [ddb]
    dag, ddw, dbin, got = _conv_bwd(y.reshape(B, T, C), dy2.reshape(B, T, C), ag.reshape(2, B, T, C), dw_w, name="conv_bwd",
                                    xchg=scatter("conv_bwd"))
    received("conv_bwd", got)
    G["conv_dw_w"], G["conv_b_in"] = [ddw[:_CONV_K]], [dbin]
    dag = dag.reshape(N, 2 * C)
    G["conv_w_in"] = [mm(h_o, dag, ta=True, tk=2 * _DW_ROWS, out_dtype=_MXU, name="mm_dwconv_in")]
    dx, dg = mm(dag, P["conv_w_in"][0], bl=0, tb=True, rms_bwd=(x3, P["mix_norm_o"][0], dx), name="mm_dh_o", tm=512, tn=D)
    G["mix_norm_o"] = [dg]
    dx = xa_ffn_bwd(dx, s0, 0)
    G["w_out_e"] = [mm(mixcat, dx, ta=True, tk=2 * _DW_ROWS, out_dtype=_MXU, name="mm_dwout_e")]
    dmix = mm(dx, P["w_out_e"][0], bl=0, tb=True, name="mm_dmix")
    dz, dlg, dlb, dws, dbias = _gmlp_bwd(proj, dmix, 1, P["gmlp_ln_g"][0], P["gmlp_ln_b"][0], P["gmlp_w_s"][0], bias_full,
                                         name="gmlp_bwd")
    G["gmlp_ln_g"], G["gmlp_ln_b"], G["gmlp_w_s"] = [dlg], [dlb], [dws]
    G["gmlp_b_s"] = [dbias[:, :2 * (W // _LANES)].T]
    dmix3 = dmix.reshape(B, T, D)
    dq, dk, dv, dcum, got = _fox_bwd(proj3, cum, dmix3, lse, name="fox_bwd", xchg=scatter("fox_bwd"))
    received("fox_bwd", got)
    df, dfb = _fox_cum_bwd(proj3, f_blk, P["fox_f_bias"][0], dcum.reshape(B, H, T), name="fox_cum_bwd")
    G["fox_f_bias"] = [dfb]
    dproj = jnp.concatenate([dq.reshape(N, W), dk.reshape(N, W), dv.reshape(N, W), dz, df.reshape(N, _LANES).astype(_MXU)], axis=1)
    G["w_in_e"] = [_unpad_w_in(mm(h_e, dproj, ta=True, tk=2 * _DW_ROWS, out_dtype=_MXU, name="mm_dwin_e", tn=896), W, H)]
    dx, dg = mm(dproj, w_in_pad, bl=0, tb=True, rms_bwd=(x0, P["mix_norm_e"][0], dx), name="mm_dh_e", tm=512, tn=D)
    G["mix_norm_e"] = [dg]
    return loss, dx.reshape(B, T, D), G, sent


def _pad_w_in(w_in, W, H):
    f = w_in[:, 3 * W:3 * W + H]
    return jnp.concatenate([w_in[:, :3 * W], w_in[:, 3 * W + H:], jnp.pad(f, ((0, 0), (0, _LANES - H)))], axis=1)


def _unpad_w_in(g, W, H):
    return jnp.concatenate([g[:, :3 * W], g[:, 5 * W:5 * W + H], g[:, 3 * W:5 * W]], axis=1)


def kernel(x, mem, mix_norm_e, w_in_e, fox_f_bias, gmlp_ln_g, gmlp_ln_b, gmlp_w_s, gmlp_b_s, w_out_e, mix_norm_o, conv_w_in, conv_b_in, conv_dw_w, conv_dw_b, conv_ln_g, conv_ln_b, conv_w_out, conv_b_out, xa_norm, mem_norm, xa_wq, xa_wkv, xa_wo, ffn_norm, ffn_w_gu, ffn_w_down, final_norm, loss_target, m_mix_norm_e, m_w_in_e, m_fox_f_bias, m_gmlp_ln_g, m_gmlp_ln_b, m_gmlp_w_s, m_gmlp_b_s, m_w_out_e, m_mix_norm_o, m_conv_w_in, m_conv_b_in, m_conv_dw_w, m_conv_dw_b, m_conv_ln_g, m_conv_ln_b, m_conv_w_out, m_conv_b_out, m_xa_norm, m_mem_norm, m_xa_wq, m_xa_wkv, m_xa_wo, m_ffn_norm, m_ffn_w_gu, m_ffn_w_down, m_final_norm, v_mix_norm_e, v_w_in_e, v_fox_f_bias, v_gmlp_ln_g, v_gmlp_ln_b, v_gmlp_w_s, v_gmlp_b_s, v_w_out_e, v_mix_norm_o, v_conv_w_in, v_conv_b_in, v_conv_dw_w, v_conv_dw_b, v_conv_ln_g, v_conv_ln_b, v_conv_w_out, v_conv_b_out, v_xa_norm, v_mem_norm, v_xa_wq, v_xa_wkv, v_xa_wo, v_ffn_norm, v_ffn_w_gu, v_ffn_w_down, v_final_norm):
    env = dict(locals())
    w = {n: env[n] for n in _WEIGHTS}
    mom = {n: env["m_" + n] for n in _WEIGHTS}
    var = {n: env["v_" + n] for n in _WEIGHTS}
    D = x.shape[-1]
    W = D // 2
    H = W // _FOX_HD

    def layers(n):
        return w[n].shape[0] if w[n].ndim > 1 else 1

    wsrc = {n: (w[n].astype(_MXU) if n in dict(_BIG) else w[n]) for n, _ in _BIG + _SMALL_SHARDED}
    P = {n: w[n] for n in _REPLICATED}
    fwd_hooks = {
        "rms_mix_e": [("w_in_e", 0)],
        "mm_in_e": [("w_out_e", 0), ("xa_wq", 0)],
        "fox_fwd": [("xa_wkv", 0), ("ffn_w_gu", 0)],
        "mm_out_e": [("xa_wo", 0)],
        "mm_q0": [("conv_w_out", 0)],
        "mm_o0": [("xa_wq", 1)],
        "mm_gu0": [("ffn_w_down", 0), ("conv_w_in", 0)] + [(n, 0) for n, _ in _SMALL_SHARDED],
        "mm_down0": [("xa_wkv", 1)],
        "conv_fwd": [("xa_wo", 1), ("ffn_w_gu", 1)],
        "mm_gu1": [("ffn_w_down", 1)],
    }

    last = [("mix_norm_e", 0)]
    in_dh_e = [("w_in_e", 0), ("fox_f_bias", 0)]
    in_conv = [(n, 1) for n in ("ffn_w_gu", "ffn_w_down", "xa_wq", "xa_wkv", "xa_wo", "xa_norm", "mem_norm", "ffn_norm")]
    in_conv += [("final_norm", 0), ("conv_w_out", 0)]
    every = [(n, l) for n in [n for n, _ in _BIG + _SMALL_SHARDED] + list(_REPLICATED) for l in range(layers(n))]
    bwd_hooks = {"conv_bwd": in_conv, "mm_dhffn0": [("ffn_w_gu", 0)], "mm_dhxa0": [("xa_wq", 0), ("xa_wo", 0)],
                 "mm_dmix": [("xa_wkv", 0)], "mm_dh_e": in_dh_e}
    placed_pieces = last + [pc for pieces in bwd_hooks.values() for pc in pieces]
    bwd_hooks["fox_bwd"] = [pc for pc in every if pc not in placed_pieces]
    loss, grad_x, G, recv = _local_step(x, mem, loss_target, P, wsrc, fwd_hooks, bwd_hooks)
    loss = lax.psum(loss[0, 0], ("x", "y", "c"))
    recv.update(zip(last, _exchange(_grad_items(last, G), name="scatter_last")))

    def partials(n):
        return [recv[(n, l)] for l in range(layers(n))]

    res = {n: _adamw_big(partials(n), w[n], mom[n], var[n], name="adamw_" + n) for n, _ in _BIG}
    small = [n for n, _ in _SMALL_SHARDED] + list(_REPLICATED)

    def rows(a, n):
        r = recv[(n, 0)]
        return a.reshape((layers(n),) + r.shape[2:-1] + (-1,))

    outs = _adamw_small([(partials(n), rows(w[n], n), rows(mom[n], n), rows(var[n], n),
                          w[n].shape[-1] if w[n].shape[-1] != recv[(n, 0)].shape[-1] else None) for n in small], name="adamw_small")
    for n, o in zip(small, outs):
        res[n] = tuple(a.reshape(w[n].shape) for a in o)
    return (loss, grad_x, *[res[n][0] for n in _WEIGHTS], *[res[n][1] for n in _WEIGHTS],
            *[res[n][2] for n in _WEIGHTS], *[res[n][3] for n in _WEIGHTS])
```
